```python
import math
import jax, jax.numpy as jnp
from jax import lax
import numpy as np

D_MODEL = 1024
BATCH = 16
SEQ = 2048
DEPTH = 4

N_A = DEPTH // 2
N_B = DEPTH - N_A
CONV_WIDTH = 31
FFN_CONV_WIDTH = 3
D_FF = 2816
N_HEADS = 8
HEAD_DIM = D_MODEL // N_HEADS
WINDOWS = (128, 512, 2048)
DILATIONS = (1, 4, 16)
N_GROUPS = len(WINDOWS)
Q_WIDTH = N_GROUPS * N_HEADS * HEAD_DIM
BLOCK = 128
EPS = 1e-6
NEG_INF = -1e30

kernel_name = "yoco_conformer_dilated_hybrid"


def rms_norm(x, g):
    xf = x.astype(jnp.float32)
    y = xf * lax.rsqrt(jnp.mean(xf * xf, axis=-1, keepdims=True) + EPS)
    return (y * g.astype(jnp.float32)).astype(x.dtype)


def layer_norm(x, g, b):
    xf = x.astype(jnp.float32)
    mu = jnp.mean(xf, axis=-1, keepdims=True)
    var = jnp.mean(jnp.square(xf - mu), axis=-1, keepdims=True)
    y = (xf - mu) * lax.rsqrt(var + EPS)
    return (y * g.astype(jnp.float32) + b.astype(jnp.float32)).astype(x.dtype)


def causal_dwconv(x, w, b):
    k, c = w.shape
    y = lax.conv_general_dilated(
        x, w[:, None, :].astype(x.dtype), window_strides=(1,), padding=[(k - 1, 0)],
        dimension_numbers=("NWC", "WIO", "NWC"), feature_group_count=c)
    return y + b


def conformer_conv_module(h, w_in, b_in, dw, dw_b, ln_g, ln_b, w_out, b_out):
    u = h @ w_in + b_in
    u = u[..., :D_MODEL] * jax.nn.sigmoid(u[..., D_MODEL:])
    u = causal_dwconv(u, dw, dw_b)
    u = layer_norm(u, ln_g, ln_b)
    u = jax.nn.silu(u)
    return u @ w_out + b_out


def conv_ffn(h, w_in, dw, dw_b, w_out):
    u = causal_dwconv(h @ w_in, dw, dw_b)
    a, g = u[..., :D_FF], u[..., D_FF:]
    return (jax.nn.silu(g) * a) @ w_out


def dilated_branch(q, k, v, window, dil):
    b, s, h, dh = q.shape
    steps = window // dil
    assert steps <= BLOCK and s % dil == 0
    L = s // dil
    nblk = -(-L // BLOCK)
    lp = nblk * BLOCK

    def to_sub(t):
        t = t.reshape(b, L, dil, h, dh).transpose(0, 2, 3, 1, 4)
        return jnp.pad(t, ((0, 0), (0, 0), (0, 0), (0, lp - L), (0, 0)))

    qs, ks, vs = to_sub(q), to_sub(k), to_sub(v)
    q_blk = qs.reshape(b, dil, h, nblk, BLOCK, dh)

    def band(t):
        tp = jnp.pad(t, ((0, 0), (0, 0), (0, 0), (BLOCK, 0), (0, 0)))
        prev = tp[:, :, :, :lp].reshape(b, dil, h, nblk, BLOCK, dh)
        cur = tp[:, :, :, BLOCK:].reshape(b, dil, h, nblk, BLOCK, dh)
        return jnp.concatenate([prev, cur], axis=-2)

    k_band, v_band = band(ks), band(vs)
    scores = jnp.einsum("brhnqc,brhnkc->brhnqk", q_blk, k_band).astype(jnp.float32)
    scores = scores * (1.0 / math.sqrt(dh))

    blk_i = jnp.arange(nblk)[:, None, None]
    qi = jnp.arange(BLOCK)[None, :, None]
    kk = jnp.arange(2 * BLOCK)[None, None, :]
    dist = qi + BLOCK - kk
    key_pos = blk_i * BLOCK + kk - BLOCK
    valid = (dist >= 0) & (dist <= steps) & (key_pos >= 0)
    scores = jnp.where(valid, scores, NEG_INF)

    m = jnp.max(scores, axis=-1, keepdims=True)
    p = jnp.exp(scores - m)
    den = jnp.sum(p, axis=-1, keepdims=True)
    out = jnp.einsum("brhnqk,brhnkc->brhnqc", p.astype(v.dtype), v_band)
    out = out / den.astype(out.dtype)
    lse = (m + jnp.log(den))[..., 0]

    out = out.reshape(b, dil, h, lp, dh)[:, :, :, :L]
    out = out.transpose(0, 3, 1, 2, 4).reshape(b, s, h, dh)
    lse = lse.reshape(b, dil, h, lp)[:, :, :, :L]
    lse = lse.transpose(0, 3, 1, 2).reshape(b, s, h)
    return out, lse


def dilated_mixture_attention(h, w_q, w_o, k, v):
    b, s, _ = h.shape
    q = (h @ w_q).reshape(b, s, N_GROUPS, N_HEADS, HEAD_DIM)
    outs, lses = [], []
    for g in range(N_GROUPS):
        o, l = dilated_branch(q[:, :, g], k[:, :, g], v[:, :, g], WINDOWS[g], DILATIONS[g])
        outs.append(o)
        lses.append(l)
    outs = jnp.stack(outs, axis=0)
    wts = jax.nn.softmax(jnp.stack(lses, axis=0), axis=0)
    merged = jnp.sum(wts[..., None].astype(outs.dtype) * outs, axis=0)
    return merged.reshape(b, s, N_HEADS * HEAD_DIM) @ w_o


def _fwd_setup_inputs(seed: int = 0) -> dict:
    key = jax.random.key(seed)
    ks = iter(jax.random.split(key, 32))

    def nrm(shape, fan_in):
        return jax.random.normal(next(ks), shape, jnp.float32) * (fan_in ** -0.5)

    def gain(shape):
        return 1.0 + 0.02 * jax.random.normal(next(ks), shape, jnp.float32)

    def bias(shape):
        return 0.01 * jax.random.normal(next(ks), shape, jnp.float32)

    d = D_MODEL
    return {
        "x": jax.random.normal(next(ks), (BATCH, SEQ, d), jnp.float32),
        "mix_pre_g": gain((DEPTH, d)),
        "mix_post_g": gain((DEPTH, d)),
        "ffn_pre_g": gain((DEPTH, d)),
        "ffn_post_g": gain((DEPTH, d)),
        "cm_w_in": nrm((N_A, d, 2 * d), d),
        "cm_b_in": bias((N_A, 2 * d)),
        "cm_dw": nrm((N_A, CONV_WIDTH, d), CONV_WIDTH),
        "cm_dw_b": bias((N_A, d)),
        "cm_ln_g": gain((N_A, d)),
        "cm_ln_b": bias((N_A, d)),
        "cm_w_out": nrm((N_A, d, d), d),
        "cm_b_out": bias((N_A, d)),
        "kv_norm_g": gain((d,)),
        "w_kv": nrm((d, 2 * Q_WIDTH), d),
        "w_q": nrm((N_B, d, Q_WIDTH), d),
        "w_o": nrm((N_B, N_HEADS * HEAD_DIM, d), N_HEADS * HEAD_DIM),
        "ffn_w_in": nrm((DEPTH, d, 2 * D_FF), d),
        "ffn_dw": nrm((DEPTH, FFN_CONV_WIDTH, 2 * D_FF), FFN_CONV_WIDTH),
        "ffn_dw_b": bias((DEPTH, 2 * D_FF)),
        "ffn_w_out": nrm((DEPTH, D_FF, d), D_FF),
    }


def _fwd_reference(x, mix_pre_g, mix_post_g, ffn_pre_g, ffn_post_g,
              cm_w_in, cm_b_in, cm_dw, cm_dw_b, cm_ln_g, cm_ln_b, cm_w_out, cm_b_out,
              kv_norm_g, w_kv, w_q, w_o,
              ffn_w_in, ffn_dw, ffn_dw_b, ffn_w_out):
    b, s, _ = x.shape
    k_sh = v_sh = None
    for i in range(DEPTH):
        h = rms_norm(x, mix_pre_g[i])
        if i < N_A:
            y = conformer_conv_module(h, cm_w_in[i], cm_b_in[i], cm_dw[i], cm_dw_b[i],
                                      cm_ln_g[i], cm_ln_b[i], cm_w_out[i], cm_b_out[i])
        else:
            j = i - N_A
            y = dilated_mixture_attention(h, w_q[j], w_o[j], k_sh, v_sh)
        x = x + rms_norm(y, mix_post_g[i])
        h = rms_norm(x, ffn_pre_g[i])
        y = conv_ffn(h, ffn_w_in[i], ffn_dw[i], ffn_dw_b[i], ffn_w_out[i])
        x = x + rms_norm(y, ffn_post_g[i])
        if i == N_A - 1:
            kv = (rms_norm(x, kv_norm_g) @ w_kv).reshape(b, s, 2, N_GROUPS, N_HEADS, HEAD_DIM)
            k_sh, v_sh = kv[:, :, 0], kv[:, :, 1]
    return x


import jax as _jax
import jax.numpy as _jnp

TWIN_FORMAT = 'train_step'
FWD_PARAMS = ['x', 'mix_pre_g', 'mix_post_g', 'ffn_pre_g', 'ffn_post_g', 'cm_w_in', 'cm_b_in', 'cm_dw', 'cm_dw_b', 'cm_ln_g', 'cm_ln_b', 'cm_w_out', 'cm_b_out', 'kv_norm_g', 'w_kv', 'w_q', 'w_o', 'ffn_w_in', 'ffn_dw', 'ffn_dw_b', 'ffn_w_out']
TWIN_WEIGHTS = ['mix_pre_g', 'mix_post_g', 'ffn_pre_g', 'ffn_post_g', 'cm_w_in', 'cm_b_in', 'cm_dw', 'cm_dw_b', 'cm_ln_g', 'cm_ln_b', 'cm_w_out', 'cm_b_out', 'kv_norm_g', 'w_kv', 'w_q', 'w_o', 'ffn_w_in', 'ffn_dw', 'ffn_dw_b', 'ffn_w_out']
TWIN_DIFF_INPUT = 'x'
TWIN_INPUTS = ['x', 'mix_pre_g', 'mix_post_g', 'ffn_pre_g', 'ffn_post_g', 'cm_w_in', 'cm_b_in', 'cm_dw', 'cm_dw_b', 'cm_ln_g', 'cm_ln_b', 'cm_w_out', 'cm_b_out', 'kv_norm_g', 'w_kv', 'w_q', 'w_o', 'ffn_w_in', 'ffn_dw', 'ffn_dw_b', 'ffn_w_out', 'loss_target', 'm_mix_pre_g', 'm_mix_post_g', 'm_ffn_pre_g', 'm_ffn_post_g', 'm_cm_w_in', 'm_cm_b_in', 'm_cm_dw', 'm_cm_dw_b', 'm_cm_ln_g', 'm_cm_ln_b', 'm_cm_w_out', 'm_cm_b_out', 'm_kv_norm_g', 'm_w_kv', 'm_w_q', 'm_w_o', 'm_ffn_w_in', 'm_ffn_dw', 'm_ffn_dw_b', 'm_ffn_w_out', 'v_mix_pre_g', 'v_mix_post_g', 'v_ffn_pre_g', 'v_ffn_post_g', 'v_cm_w_in', 'v_cm_b_in', 'v_cm_dw', 'v_cm_dw_b', 'v_cm_ln_g', 'v_cm_ln_b', 'v_cm_w_out', 'v_cm_b_out', 'v_kv_norm_g', 'v_w_kv', 'v_w_q', 'v_w_o', 'v_ffn_w_in', 'v_ffn_dw', 'v_ffn_dw_b', 'v_ffn_w_out']
TWIN_OUTPUTS = ['loss', 'grad_x', 'grad_mix_pre_g', 'grad_mix_post_g', 'grad_ffn_pre_g', 'grad_ffn_post_g', 'grad_cm_w_in', 'grad_cm_b_in', 'grad_cm_dw', 'grad_cm_dw_b', 'grad_cm_ln_g', 'grad_cm_ln_b', 'grad_cm_w_out', 'grad_cm_b_out', 'grad_kv_norm_g', 'grad_w_kv', 'grad_w_q', 'grad_w_o', 'grad_ffn_w_in', 'grad_ffn_dw', 'grad_ffn_dw_b', 'grad_ffn_w_out', 'delta_mix_pre_g', 'delta_mix_post_g', 'delta_ffn_pre_g', 'delta_ffn_post_g', 'delta_cm_w_in', 'delta_cm_b_in', 'delta_cm_dw', 'delta_cm_dw_b', 'delta_cm_ln_g', 'delta_cm_ln_b', 'delta_cm_w_out', 'delta_cm_b_out', 'delta_kv_norm_g', 'delta_w_kv', 'delta_w_q', 'delta_w_o', 'delta_ffn_w_in', 'delta_ffn_dw', 'delta_ffn_dw_b', 'delta_ffn_w_out', 'new_m_mix_pre_g', 'new_m_mix_post_g', 'new_m_ffn_pre_g', 'new_m_ffn_post_g', 'new_m_cm_w_in', 'new_m_cm_b_in', 'new_m_cm_dw', 'new_m_cm_dw_b', 'new_m_cm_ln_g', 'new_m_cm_ln_b', 'new_m_cm_w_out', 'new_m_cm_b_out', 'new_m_kv_norm_g', 'new_m_w_kv', 'new_m_w_q', 'new_m_w_o', 'new_m_ffn_w_in', 'new_m_ffn_dw', 'new_m_ffn_dw_b', 'new_m_ffn_w_out', 'new_v_mix_pre_g', 'new_v_mix_post_g', 'new_v_ffn_pre_g', 'new_v_ffn_post_g', 'new_v_cm_w_in', 'new_v_cm_b_in', 'new_v_cm_dw', 'new_v_cm_dw_b', 'new_v_cm_ln_g', 'new_v_cm_ln_b', 'new_v_cm_w_out', 'new_v_cm_b_out', 'new_v_kv_norm_g', 'new_v_w_kv', 'new_v_w_q', 'new_v_w_o', 'new_v_ffn_w_in', 'new_v_ffn_dw', 'new_v_ffn_dw_b', 'new_v_ffn_w_out']
TWIN_LEAF_KINDS = {'loss': 'loss', 'grad_x': 'grad_x', 'grad_mix_pre_g': 'grad_w', 'grad_mix_post_g': 'grad_w', 'grad_ffn_pre_g': 'grad_w', 'grad_ffn_post_g': 'grad_w', 'grad_cm_w_in': 'grad_w', 'grad_cm_b_in': 'grad_w', 'grad_cm_dw': 'grad_w', 'grad_cm_dw_b': 'grad_w', 'grad_cm_ln_g': 'grad_w', 'grad_cm_ln_b': 'grad_w', 'grad_cm_w_out': 'grad_w', 'grad_cm_b_out': 'grad_w', 'grad_kv_norm_g': 'grad_w', 'grad_w_kv': 'grad_w', 'grad_w_q': 'grad_w', 'grad_w_o': 'grad_w', 'grad_ffn_w_in': 'grad_w', 'grad_ffn_dw': 'grad_w', 'grad_ffn_dw_b': 'grad_w', 'grad_ffn_w_out': 'grad_w', 'delta_mix_pre_g': 'delta_w', 'delta_mix_post_g': 'delta_w', 'delta_ffn_pre_g': 'delta_w', 'delta_ffn_post_g': 'delta_w', 'delta_cm_w_in': 'delta_w', 'delta_cm_b_in': 'delta_w', 'delta_cm_dw': 'delta_w', 'delta_cm_dw_b': 'delta_w', 'delta_cm_ln_g': 'delta_w', 'delta_cm_ln_b': 'delta_w', 'delta_cm_w_out': 'delta_w', 'delta_cm_b_out': 'delta_w', 'delta_kv_norm_g': 'delta_w', 'delta_w_kv': 'delta_w', 'delta_w_q': 'delta_w', 'delta_w_o': 'delta_w', 'delta_ffn_w_in': 'delta_w', 'delta_ffn_dw': 'delta_w', 'delta_ffn_dw_b': 'delta_w', 'delta_ffn_w_out': 'delta_w', 'new_m_mix_pre_g': 'new_m', 'new_m_mix_post_g': 'new_m', 'new_m_ffn_pre_g': 'new_m', 'new_m_ffn_post_g': 'new_m', 'new_m_cm_w_in': 'new_m', 'new_m_cm_b_in': 'new_m', 'new_m_cm_dw': 'new_m', 'new_m_cm_dw_b': 'new_m', 'new_m_cm_ln_g': 'new_m', 'new_m_cm_ln_b': 'new_m', 'new_m_cm_w_out': 'new_m', 'new_m_cm_b_out': 'new_m', 'new_m_kv_norm_g': 'new_m', 'new_m_w_kv': 'new_m', 'new_m_w_q': 'new_m', 'new_m_w_o': 'new_m', 'new_m_ffn_w_in': 'new_m', 'new_m_ffn_dw': 'new_m', 'new_m_ffn_dw_b': 'new_m', 'new_m_ffn_w_out': 'new_m', 'new_v_mix_pre_g': 'new_v', 'new_v_mix_post_g': 'new_v', 'new_v_ffn_pre_g': 'new_v', 'new_v_ffn_post_g': 'new_v', 'new_v_cm_w_in': 'new_v', 'new_v_cm_b_in': 'new_v', 'new_v_cm_dw': 'new_v', 'new_v_cm_dw_b': 'new_v', 'new_v_cm_ln_g': 'new_v', 'new_v_cm_ln_b': 'new_v', 'new_v_cm_w_out': 'new_v', 'new_v_cm_b_out': 'new_v', 'new_v_kv_norm_g': 'new_v', 'new_v_w_kv': 'new_v', 'new_v_w_q': 'new_v', 'new_v_w_o': 'new_v', 'new_v_ffn_w_in': 'new_v', 'new_v_ffn_dw': 'new_v', 'new_v_ffn_dw_b': 'new_v', 'new_v_ffn_w_out': 'new_v'}


def _forward(args):
    return _fwd_reference(*[args[k] for k in FWD_PARAMS])


def _output_shape():
    out = _jax.eval_shape(lambda: _forward(_fwd_setup_inputs(0)))
    return out.shape, out.dtype

N_MICROBATCH = 1
ADAM_LR = 0.001
ADAM_B1 = 0.9
ADAM_B2 = 0.999
ADAM_EPS = 1e-08
ADAM_WD = 0.01
ADAM_STEP = 10
PER_EXAMPLE_BATCH_AXIS = {'x': 0, 'loss_target': 0}
SHARED_INPUTS = []
_WEIGHT_DTYPES = {'mix_pre_g': _jnp.float32, 'mix_post_g': _jnp.float32, 'ffn_pre_g': _jnp.float32, 'ffn_post_g': _jnp.float32, 'cm_w_in': _jnp.float32, 'cm_b_in': _jnp.float32, 'cm_dw': _jnp.float32, 'cm_dw_b': _jnp.float32, 'cm_ln_g': _jnp.float32, 'cm_ln_b': _jnp.float32, 'cm_w_out': _jnp.float32, 'cm_b_out': _jnp.float32, 'kv_norm_g': _jnp.float32, 'w_kv': _jnp.float32, 'w_q': _jnp.float32, 'w_o': _jnp.float32, 'ffn_w_in': _jnp.float32, 'ffn_dw': _jnp.float32, 'ffn_dw_b': _jnp.float32, 'ffn_w_out': _jnp.float32}
MOMENT_SCALE = {'mix_pre_g': 6.825322e+00, 'mix_post_g': 4.973338e+01, 'ffn_pre_g': 1.113079e+01, 'ffn_post_g': 3.319250e+01, 'cm_w_in': 6.486503e+00, 'cm_b_in': 4.767903e+01, 'cm_dw': 1.275226e+01, 'cm_dw_b': 1.381499e+02, 'cm_ln_g': 5.318045e+01, 'cm_ln_b': 7.390504e+01, 'cm_w_out': 3.335164e+01, 'cm_b_out': 1.474830e+02, 'kv_norm_g': 3.778318e+01, 'w_kv': 1.610173e+01, 'w_q': 8.147933e-01, 'w_o': 2.762416e+01, 'ffn_w_in': 4.569722e+00, 'ffn_dw': 5.376631e+00, 'ffn_dw_b': 1.555645e+01, 'ffn_w_out': 9.641086e+00}


def _to_microbatches(a, axis):
    t = _jnp.moveaxis(a, axis, 0)
    t = t.reshape((N_MICROBATCH, t.shape[0] // N_MICROBATCH) + t.shape[1:])
    return _jnp.moveaxis(t, 1, axis + 1)


def setup_inputs(seed: int = 0) -> dict:
    inp = _fwd_setup_inputs(seed)
    key = _jax.random.fold_in(_jax.random.key(seed), 7919)
    shape, _ = _output_shape()
    out = dict(inp)
    out["loss_target"] = _jax.random.normal(_jax.random.fold_in(key, 0), shape, _jnp.float32)
    for i, name in enumerate(TWIN_WEIGHTS):
        w = inp[name].astype(_jnp.float32)
        if MOMENT_SCALE is None:
            s = _jnp.sqrt(_jnp.mean(_jnp.square(w)) + 1e-30)
        else:
            s = MOMENT_SCALE[name]
        km, kv = _jax.random.split(_jax.random.fold_in(key, i + 1))
        out[name] = w
        out["m_" + name] = s * _jax.random.normal(km, w.shape, _jnp.float32)
        out["v_" + name] = (s * s) * _jax.random.uniform(kv, w.shape, _jnp.float32, 0.5, 1.5)
    if N_MICROBATCH > 1:
        for name, axis in PER_EXAMPLE_BATCH_AXIS.items():
            out[name] = _to_microbatches(out[name], axis)
    return {'x': out['x'], 'mix_pre_g': out['mix_pre_g'], 'mix_post_g': out['mix_post_g'], 'ffn_pre_g': out['ffn_pre_g'], 'ffn_post_g': out['ffn_post_g'], 'cm_w_in': out['cm_w_in'], 'cm_b_in': out['cm_b_in'], 'cm_dw': out['cm_dw'], 'cm_dw_b': out['cm_dw_b'], 'cm_ln_g': out['cm_ln_g'], 'cm_ln_b': out['cm_ln_b'], 'cm_w_out': out['cm_w_out'], 'cm_b_out': out['cm_b_out'], 'kv_norm_g': out['kv_norm_g'], 'w_kv': out['w_kv'], 'w_q': out['w_q'], 'w_o': out['w_o'], 'ffn_w_in': out['ffn_w_in'], 'ffn_dw': out['ffn_dw'], 'ffn_dw_b': out['ffn_dw_b'], 'ffn_w_out': out['ffn_w_out'], 'loss_target': out['loss_target'], 'm_mix_pre_g': out['m_mix_pre_g'], 'm_mix_post_g': out['m_mix_post_g'], 'm_ffn_pre_g': out['m_ffn_pre_g'], 'm_ffn_post_g': out['m_ffn_post_g'], 'm_cm_w_in': out['m_cm_w_in'], 'm_cm_b_in': out['m_cm_b_in'], 'm_cm_dw': out['m_cm_dw'], 'm_cm_dw_b': out['m_cm_dw_b'], 'm_cm_ln_g': out['m_cm_ln_g'], 'm_cm_ln_b': out['m_cm_ln_b'], 'm_cm_w_out': out['m_cm_w_out'], 'm_cm_b_out': out['m_cm_b_out'], 'm_kv_norm_g': out['m_kv_norm_g'], 'm_w_kv': out['m_w_kv'], 'm_w_q': out['m_w_q'], 'm_w_o': out['m_w_o'], 'm_ffn_w_in': out['m_ffn_w_in'], 'm_ffn_dw': out['m_ffn_dw'], 'm_ffn_dw_b': out['m_ffn_dw_b'], 'm_ffn_w_out': out['m_ffn_w_out'], 'v_mix_pre_g': out['v_mix_pre_g'], 'v_mix_post_g': out['v_mix_post_g'], 'v_ffn_pre_g': out['v_ffn_pre_g'], 'v_ffn_post_g': out['v_ffn_post_g'], 'v_cm_w_in': out['v_cm_w_in'], 'v_cm_b_in': out['v_cm_b_in'], 'v_cm_dw': out['v_cm_dw'], 'v_cm_dw_b': out['v_cm_dw_b'], 'v_cm_ln_g': out['v_cm_ln_g'], 'v_cm_ln_b': out['v_cm_ln_b'], 'v_cm_w_out': out['v_cm_w_out'], 'v_cm_b_out': out['v_cm_b_out'], 'v_kv_norm_g': out['v_kv_norm_g'], 'v_w_kv': out['v_w_kv'], 'v_w_q': out['v_w_q'], 'v_w_o': out['v_w_o'], 'v_ffn_w_in': out['v_ffn_w_in'], 'v_ffn_dw': out['v_ffn_dw'], 'v_ffn_dw_b': out['v_ffn_dw_b'], 'v_ffn_w_out': out['v_ffn_w_out']}


def _loss(weights, diff, rest, loss_target):
    with _jax.named_scope("forward"):
        args = {**rest, TWIN_DIFF_INPUT: diff, **{k: w.astype(_WEIGHT_DTYPES[k]) for k, w in weights.items()}}
        y = _forward(args)
    with _jax.named_scope("loss_head"):
        err = _jnp.square(y.astype(_jnp.float32) - loss_target)
        return 0.5 * _jnp.sum(_jnp.mean(err, axis=-1)) if err.ndim else 0.5 * err


def _adamw(w, g, m, v):
    m = ADAM_B1 * m + (1.0 - ADAM_B1) * g
    v = ADAM_B2 * v + (1.0 - ADAM_B2) * _jnp.square(g)
    m_hat = m / (1.0 - ADAM_B1 ** ADAM_STEP)
    v_hat = v / (1.0 - ADAM_B2 ** ADAM_STEP)
    delta = -ADAM_LR * (m_hat / (_jnp.sqrt(v_hat) + ADAM_EPS) + ADAM_WD * w)
    return delta, m, v


def reference(x, mix_pre_g, mix_post_g, ffn_pre_g, ffn_post_g, cm_w_in, cm_b_in, cm_dw, cm_dw_b, cm_ln_g, cm_ln_b, cm_w_out, cm_b_out, kv_norm_g, w_kv, w_q, w_o, ffn_w_in, ffn_dw, ffn_dw_b, ffn_w_out, loss_target, m_mix_pre_g, m_mix_post_g, m_ffn_pre_g, m_ffn_post_g, m_cm_w_in, m_cm_b_in, m_cm_dw, m_cm_dw_b, m_cm_ln_g, m_cm_ln_b, m_cm_w_out, m_cm_b_out, m_kv_norm_g, m_w_kv, m_w_q, m_w_o, m_ffn_w_in, m_ffn_dw, m_ffn_dw_b, m_ffn_w_out, v_mix_pre_g, v_mix_post_g, v_ffn_pre_g, v_ffn_post_g, v_cm_w_in, v_cm_b_in, v_cm_dw, v_cm_dw_b, v_cm_ln_g, v_cm_ln_b, v_cm_w_out, v_cm_b_out, v_kv_norm_g, v_w_kv, v_w_q, v_w_o, v_ffn_w_in, v_ffn_dw, v_ffn_dw_b, v_ffn_w_out):
    given = dict(x=x, mix_pre_g=mix_pre_g, mix_post_g=mix_post_g, ffn_pre_g=ffn_pre_g, ffn_post_g=ffn_post_g, cm_w_in=cm_w_in, cm_b_in=cm_b_in, cm_dw=cm_dw, cm_dw_b=cm_dw_b, cm_ln_g=cm_ln_g, cm_ln_b=cm_ln_b, cm_w_out=cm_w_out, cm_b_out=cm_b_out, kv_norm_g=kv_norm_g, w_kv=w_kv, w_q=w_q, w_o=w_o, ffn_w_in=ffn_w_in, ffn_dw=ffn_dw, ffn_dw_b=ffn_dw_b, ffn_w_out=ffn_w_out, loss_target=loss_target, m_mix_pre_g=m_mix_pre_g, m_mix_post_g=m_mix_post_g, m_ffn_pre_g=m_ffn_pre_g, m_ffn_post_g=m_ffn_post_g, m_cm_w_in=m_cm_w_in, m_cm_b_in=m_cm_b_in, m_cm_dw=m_cm_dw, m_cm_dw_b=m_cm_dw_b, m_cm_ln_g=m_cm_ln_g, m_cm_ln_b=m_cm_ln_b, m_cm_w_out=m_cm_w_out, m_cm_b_out=m_cm_b_out, m_kv_norm_g=m_kv_norm_g, m_w_kv=m_w_kv, m_w_q=m_w_q, m_w_o=m_w_o, m_ffn_w_in=m_ffn_w_in, m_ffn_dw=m_ffn_dw, m_ffn_dw_b=m_ffn_dw_b, m_ffn_w_out=m_ffn_w_out, v_mix_pre_g=v_mix_pre_g, v_mix_post_g=v_mix_post_g, v_ffn_pre_g=v_ffn_pre_g, v_ffn_post_g=v_ffn_post_g, v_cm_w_in=v_cm_w_in, v_cm_b_in=v_cm_b_in, v_cm_dw=v_cm_dw, v_cm_dw_b=v_cm_dw_b, v_cm_ln_g=v_cm_ln_g, v_cm_ln_b=v_cm_ln_b, v_cm_w_out=v_cm_w_out, v_cm_b_out=v_cm_b_out, v_kv_norm_g=v_kv_norm_g, v_w_kv=v_w_kv, v_w_q=v_w_q, v_w_o=v_w_o, v_ffn_w_in=v_ffn_w_in, v_ffn_dw=v_ffn_dw, v_ffn_dw_b=v_ffn_dw_b, v_ffn_w_out=v_ffn_w_out)
    weights = {n: given[n] for n in TWIN_WEIGHTS}
    shared = {n: given[n] for n in SHARED_INPUTS}
    per_example = {n: given[n] for n in ['x']}
    grad_fn = _jax.value_and_grad(_loss, argnums=(0, 1))

    def one_microbatch(ex, loss_target):
        ex = dict(ex)
        diff = ex.pop(TWIN_DIFF_INPUT)
        return grad_fn(weights, diff, {**shared, **ex}, loss_target)

    if N_MICROBATCH == 1:
        loss, (grad_w, grad_x) = one_microbatch(per_example, given["loss_target"])
    else:
        def body(carry, xs):
            loss_sum, grad_sum = carry
            l_k, (gw_k, gx_k) = one_microbatch(xs[0], xs[1])
            with _jax.named_scope("update"):
                return (loss_sum + l_k, _jax.tree.map(_jnp.add, grad_sum, gw_k)), gx_k

        init = (_jnp.zeros((), _jnp.float32), _jax.tree.map(_jnp.zeros_like, weights))
        (loss, grad_w), grad_x = _jax.lax.scan(body, init, (per_example, given["loss_target"]))
    with _jax.named_scope("update"):
        delta_w, new_m, new_v = {}, {}, {}
        for n in TWIN_WEIGHTS:
            delta_w[n], new_m[n], new_v[n] = _adamw(weights[n], grad_w[n], given["m_" + n], given["v_" + n])
    return (loss, grad_x, *[grad_w[n] for n in TWIN_WEIGHTS], *[delta_w[n] for n in TWIN_WEIGHTS],
            *[new_m[n] for n in TWIN_WEIGHTS], *[new_v[n] for n in TWIN_WEIGHTS])
```

```python
import functools

import jax
import jax.numpy as jnp
from jax import lax
from jax.experimental import pallas as pl
from jax.experimental.pallas import tpu as pltpu

N_DEV = 8
N_A = 2
DEPTH = 4
N_HEADS = 8
N_GROUPS = 3
DILATIONS = (1, 4, 16)
ATT_BLOCK = 128
CONV_W = 31
FFN_CONV_W = 3
CONV_HALO = 32
FFN_HALO = 16
EPS = 1e-6
NEG = -1e30
ADAM_LR, ADAM_B1, ADAM_B2, ADAM_EPS, ADAM_WD, ADAM_STEP = 0.001, 0.9, 0.999, 1e-08, 0.01, 10
MM = jnp.bfloat16
F32 = jnp.float32
VMEM_LIMIT_BYTES = 56 * 1024 * 1024
PACK = 1024
MESH_ID = pl.DeviceIdType.MESH

_pallas = pl.pallas_call


def _call(body, *, name, out_shape, grid=(), in_specs=None, out_specs=None, scratch=()):
    return _pallas(body, name=name, out_shape=out_shape, grid=grid, in_specs=in_specs, out_specs=out_specs,
                   scratch_shapes=list(scratch),
                   compiler_params=pltpu.CompilerParams(vmem_limit_bytes=VMEM_LIMIT_BYTES))


def _tile(n, pref):
    if n <= pref:
        return n
    t = pref - pref % 8
    while n % t:
        t -= 8
    assert t > 0, (n, pref)
    return t


def _sds(shape, dtype):
    return jax.ShapeDtypeStruct(tuple(shape), dtype)


def _dot(a, b):
    return jnp.dot(a, b, preferred_element_type=F32)


def _dot_nt(a, b):
    return lax.dot_general(a, b, (((1,), (1,)), ((), ())), preferred_element_type=F32)


def _dot_tn(a, b):
    return lax.dot_general(a, b, (((0,), (0,)), ((), ())), preferred_element_type=F32)


def _sigmoid(x):
    return 1.0 / (1.0 + jnp.exp(-x))


def _my_index():
    return 4 * lax.axis_index("x") + 2 * lax.axis_index("y") + lax.axis_index("c")


def _exchange(name, arrays, out_shapes, pieces, src_of, dst_of):
    n = len(arrays)
    base = [sum(pieces[:a]) for a in range(n)]
    total = sum(pieces)

    def body(*refs):
        ins, outs = refs[:n], refs[n:2 * n]
        send_sems, recv_sems, local_sems = refs[2 * n:]
        x, y, c = lax.axis_index("x"), lax.axis_index("y"), lax.axis_index("c")
        me = 4 * x + 2 * y + c
        copies = []
        for a in range(n):
            for k, (s, d) in enumerate(zip(src_of(a, ins[a], me), dst_of(a, outs[a], me))):
                cp = pltpu.make_async_copy(s, d, local_sems.at[base[a] + k])
                cp.start()
                copies.append(cp)
        remote = []
        for m in range(1, N_DEV):
            px, py, pc = x ^ (m >> 2), y ^ ((m >> 1) & 1), c ^ (m & 1)
            peer = 4 * px + 2 * py + pc
            for a in range(n):
                for k, (s, d) in enumerate(zip(src_of(a, ins[a], peer), dst_of(a, outs[a], me))):
                    cp = pltpu.make_async_remote_copy(src_ref=s, dst_ref=d, send_sem=send_sems.at[base[a] + k, m - 1],
                                                      recv_sem=recv_sems.at[base[a] + k, m - 1],
                                                      device_id=(px, py, pc), device_id_type=MESH_ID)
                    cp.start()
                    remote.append(cp)
        for cp in copies:
            cp.wait()
        for cp in remote:
            cp.wait_send()
        for m in range(1, N_DEV):
            px, py, pc = x ^ (m >> 2), y ^ ((m >> 1) & 1), c ^ (m & 1)
            peer = 4 * px + 2 * py + pc
            for a in range(n):
                for k, (s, d) in enumerate(zip(src_of(a, ins[a], me), dst_of(a, outs[a], peer))):
                    pltpu.make_async_remote_copy(src_ref=s, dst_ref=d, send_sem=send_sems.at[base[a] + k, m - 1],
                                                 recv_sem=recv_sems.at[base[a] + k, m - 1], device_id=(px, py, pc),
                                                 device_id_type=MESH_ID).wait_recv()

    any_spec = pl.BlockSpec(memory_space=pl.ANY)
    return _call(body, name=name, out_shape=[_sds(s, a.dtype) for s, a in zip(out_shapes, arrays)],
                 in_specs=[any_spec] * n, out_specs=[any_spec] * n,
                 scratch=[pltpu.SemaphoreType.DMA((total, N_DEV - 1)), pltpu.SemaphoreType.DMA((total, N_DEV - 1)),
                          pltpu.SemaphoreType.DMA((total,))])(*arrays)


def all_gather(name, arrays, row_sharded):
    def out_shape(a):
        s = arrays[a].shape
        return (s[0], N_DEV) + s[1:] if row_sharded[a] else (N_DEV,) + s

    def src_of(a, ref, peer):
        if row_sharded[a]:
            return [ref.at[l] for l in range(arrays[a].shape[0])]
        return [ref]

    def dst_of(a, ref, me):
        if row_sharded[a]:
            return [ref.at[l, me] for l in range(arrays[a].shape[0])]
        return [ref.at[me]]

    pieces = [arrays[a].shape[0] if row_sharded[a] else 1 for a in range(len(arrays))]
    return _exchange(name, arrays, [out_shape(a) for a in range(len(arrays))], pieces, src_of, dst_of)


def scatter_partials(name, arrays):
    return _exchange(name, arrays, [a.shape for a in arrays], [1] * len(arrays),
                     lambda a, ref, peer: [ref.at[peer]], lambda a, ref, me: [ref.at[me]])


def norm_mm(name, x, gain, w, layer, bias=None):
    T, D = x.shape
    nsh, _, _, n = w.shape
    tm = _tile(T, 1024)

    def body(*refs):
        if bias is None:
            x_ref, g_ref, w_ref, u_ref, h_ref = refs
        else:
            x_ref, g_ref, w_ref, b_ref, u_ref, h_ref = refs

        @pl.when(pl.program_id(1) == 0)
        def _():
            xf = x_ref[...]
            r = lax.rsqrt(jnp.mean(xf * xf, axis=-1, keepdims=True) + EPS)
            h_ref[...] = (xf * r * g_ref[...]).astype(h_ref.dtype)

        acc = _dot(h_ref[...], w_ref[...])
        if bias is not None:
            acc = acc + b_ref[...]
        u_ref[...] = acc.astype(u_ref.dtype)

    in_specs = [pl.BlockSpec((tm, D), lambda i, j: (i, 0)),
                pl.BlockSpec((1, D), lambda i, j: (0, 0)),
                pl.BlockSpec((None, None, D, n), lambda i, j: (j, layer, 0, 0))]
    args = [x, gain.reshape(1, D), w]
    if bias is not None:
        in_specs.append(pl.BlockSpec((None, None, 1, n), lambda i, j: (j, layer, 0, 0)))
        args.append(bias)
    return _call(body, name=name, grid=(T // tm, nsh), in_specs=in_specs,
                 out_specs=[pl.BlockSpec((None, tm, n), lambda i, j: (j, i, 0)),
                            pl.BlockSpec((tm, D), lambda i, j: (i, 0))],
                 out_shape=[_sds((nsh, T, n), MM), _sds((T, D), MM)])(*args)


def mm_resid_norm(name, a, w, layer, bias, x, gain):
    nk, T, kk = a.shape
    D = x.shape[1]
    tm = _tile(T, 512)

    def body(a_ref, w_ref, b_ref, x_ref, g_ref, y_ref, xn_ref):
        y = _dot(a_ref[0], w_ref[0])
        for q in range(1, nk):
            y = y + _dot(a_ref[q], w_ref[q])
        y = y + b_ref[...]
        y_ref[...] = y
        r = lax.rsqrt(jnp.mean(y * y, axis=-1, keepdims=True) + EPS)
        xn_ref[...] = x_ref[...] + y * r * g_ref[...]

    return _call(body, name=name, grid=(T // tm,),
                 in_specs=[pl.BlockSpec((nk, tm, kk), lambda i: (0, i, 0)),
                           pl.BlockSpec((None, nk, kk, D), lambda i: (layer, 0, 0, 0)),
                           pl.BlockSpec((1, D), lambda i: (0, 0)),
                           pl.BlockSpec((tm, D), lambda i: (i, 0)),
                           pl.BlockSpec((1, D), lambda i: (0, 0))],
                 out_specs=[pl.BlockSpec((tm, D), lambda i: (i, 0))] * 2,
                 out_shape=[_sds((T, D), F32)] * 2)(a, w, bias.reshape(1, D), x, gain.reshape(1, D))


def _halo_maps(tm, hb, T):
    per = tm // hb
    last = T // hb - 1
    return (lambda i: jnp.maximum(i * per - 1, 0)), (lambda i: jnp.minimum((i + 1) * per, last))


def cm_glu_conv(name, u, dw, dwb, layer, seq):
    _, T, n = u.shape
    ct = dw.shape[-1]
    per = n // ct
    nct = 4 * per
    tm = _tile(seq, 512)
    tps = seq // tm
    hb = CONV_HALO
    prev, _ = _halo_maps(tm, hb, T)
    u4 = u.reshape(2, 4, T, n)

    def body(u_ref, uh_ref, w_ref, b_ref, o_ref, pad_ref):
        first = (pl.program_id(0) % tps) == 0
        um = u_ref[...].astype(F32)
        uh = uh_ref[...].astype(F32)
        pad_ref[pl.ds(hb, tm), :] = um[0] * _sigmoid(um[1])
        pad_ref[pl.ds(0, hb), :] = jnp.where(first, 0.0, uh[0] * _sigmoid(uh[1]))
        acc = jnp.zeros((tm, ct), F32) + b_ref[...]
        for k in range(CONV_W):
            acc = acc + w_ref[pl.ds(k, 1), :] * pad_ref[pl.ds(hb - (CONV_W - 1) + k, tm), :]
        o_ref[...] = acc

    return _call(body, name=name, grid=(T // tm, nct),
                 in_specs=[pl.BlockSpec((2, None, tm, ct), lambda i, c: (0, c // per, i, c % per)),
                           pl.BlockSpec((2, None, hb, ct), lambda i, c: (0, c // per, prev(i), c % per)),
                           pl.BlockSpec((None, None, CONV_W, ct), lambda i, c: (c, layer, 0, 0)),
                           pl.BlockSpec((None, None, 1, ct), lambda i, c: (c, layer, 0, 0))],
                 out_specs=pl.BlockSpec((tm, ct), lambda i, c: (i, c)),
                 out_shape=_sds((T, nct * ct), F32),
                 scratch=[pltpu.VMEM((tm + hb, ct), F32)])(u4, u4, dw, dwb)


def ln_silu(name, c, g, b):
    T, D = c.shape
    tm = _tile(T, 512)

    def body(c_ref, g_ref, b_ref, s_ref):
        cf = c_ref[...]
        mu = jnp.mean(cf, axis=-1, keepdims=True)
        xc = cf - mu
        r = lax.rsqrt(jnp.mean(xc * xc, axis=-1, keepdims=True) + EPS)
        t = xc * r * g_ref[...] + b_ref[...]
        s_ref[...] = (t * _sigmoid(t)).astype(s_ref.dtype)

    return _call(body, name=name, grid=(T // tm,),
                 in_specs=[pl.BlockSpec((tm, D), lambda i: (i, 0)), pl.BlockSpec((1, D), lambda i: (0, 0)),
                           pl.BlockSpec((1, D), lambda i: (0, 0))],
                 out_specs=pl.BlockSpec((None, tm, D), lambda i: (0, i, 0)),
                 out_shape=_sds((1, T, D), MM))(c, g.reshape(1, D), b.reshape(1, D))


def ffn_conv_gate(name, u, dw, dwb, layer, seq):
    _, T, n = u.shape
    tm = _tile(seq, 512)
    tps = seq // tm
    hb = FFN_HALO
    prev, _ = _halo_maps(tm, hb, T)
    u4 = u.reshape(2, 4, T, n)
    dw4 = dw.reshape(2, 4, dw.shape[1], FFN_CONV_W, n)
    dwb4 = dwb.reshape(dwb.shape[0], 2, 4, 1, n)

    def body(u_ref, uh_ref, w_ref, b_ref, z_ref, pad_ref):
        first = (pl.program_id(0) % tps) == 0
        conv = []
        for half in range(2):
            pad_ref[pl.ds(hb, tm), :] = u_ref[half].astype(F32)
            pad_ref[pl.ds(0, hb), :] = jnp.where(first, 0.0, uh_ref[half].astype(F32))
            acc = jnp.zeros((tm, n), F32) + b_ref[half]
            for k in range(FFN_CONV_W):
                acc = acc + w_ref[half, pl.ds(k, 1), :] * pad_ref[pl.ds(hb - (FFN_CONV_W - 1) + k, tm), :]
            conv.append(acc)
        a, g = conv
        z_ref[...] = (g * _sigmoid(g) * a).astype(z_ref.dtype)

    return _call(body, name=name, grid=(T // tm, 4),
                 in_specs=[pl.BlockSpec((2, None, tm, n), lambda i, j: (0, j, i, 0)),
                           pl.BlockSpec((2, None, hb, n), lambda i, j: (0, j, prev(i), 0)),
                           pl.BlockSpec((2, None, None, FFN_CONV_W, n), lambda i, j: (0, j, layer, 0, 0)),
                           pl.BlockSpec((None, 2, None, 1, n), lambda i, j: (layer, 0, j, 0, 0))],
                 out_specs=pl.BlockSpec((None, tm, n), lambda i, j: (j, i, 0)),
                 out_shape=_sds((4, T, n), MM),
                 scratch=[pltpu.VMEM((tm + hb, n), F32)])(u4, u4, dw4, dwb4)


def _head_specs(seq, dh, q_heads, kv_heads):
    def spec(per, base):
        def imap(b, h, g):
            f = base + g * N_HEADS + h
            return (f // per, b, f % per)
        return pl.BlockSpec((None, seq, dh), imap)
    return spec(q_heads, 0), spec(kv_heads, 0), spec(kv_heads, N_GROUPS * N_HEADS)


def _rows(start, d):
    return pl.ds(start, ATT_BLOCK, stride=d) if d > 1 else pl.ds(start, ATT_BLOCK)


def _band_masks():
    qi = lax.broadcasted_iota(jnp.int32, (ATT_BLOCK, ATT_BLOCK), 0)
    kj = lax.broadcasted_iota(jnp.int32, (ATT_BLOCK, ATT_BLOCK), 1)
    return kj <= qi, kj >= qi


def attn_fwd(name, q, kv, nb, seq):
    _, T, qn = q.shape
    dh = qn * N_DEV // (N_GROUPS * N_HEADS)
    scale = 1.0 / (dh ** 0.5)
    qs, ks, vs = _head_specs(seq, dh, qn // dh, kv.shape[2] // dh)

    def body(q_ref, k_ref, v_ref, m_ref, l_ref, qf, kf, vf, *branch):
        og, lg = branch[:N_GROUPS], branch[N_GROUPS:]
        cur_ok, prev_ok = _band_masks()
        qf[...] = q_ref[...].astype(F32)
        kf[...] = k_ref[...].astype(F32)
        vf[...] = v_ref[...].astype(F32)
        for g in range(N_GROUPS):
            d = DILATIONS[g]
            nblk = seq // d // ATT_BLOCK

            def block(idx, carry, d=d, nblk=nblk, g=g):
                r, n = idx // nblk, idx % nblk
                rc = _rows(r + d * ATT_BLOCK * n, d)
                qb = qf[rc, :].astype(MM)
                s_c = jnp.where(cur_ok, _dot_nt(qb, kf[rc, :].astype(MM)) * scale, NEG)
                m = jnp.max(s_c, axis=-1, keepdims=True)
                if nblk > 1:
                    rp = _rows(r + d * ATT_BLOCK * jnp.maximum(n - 1, 0), d)
                    s_p = jnp.where(prev_ok & (n > 0), _dot_nt(qb, kf[rp, :].astype(MM)) * scale, NEG)
                    m = jnp.maximum(m, jnp.max(s_p, axis=-1, keepdims=True))
                p_c = jnp.exp(s_c - m)
                den = jnp.sum(p_c, axis=-1, keepdims=True)
                o = _dot(p_c.astype(MM), vf[rc, :].astype(MM))
                if nblk > 1:
                    p_p = jnp.exp(s_p - m)
                    den = den + jnp.sum(p_p, axis=-1, keepdims=True)
                    o = o + _dot(p_p.astype(MM), vf[rp, :].astype(MM))
                og[g][rc, :] = o / den
                lg[g][rc, :] = jnp.broadcast_to(m + jnp.log(den), (ATT_BLOCK, dh))
                return carry

            @pl.when(pl.program_id(2) == g)
            def _(block=block, d=d, nblk=nblk):
                lax.fori_loop(0, d * nblk, block, 0)

        @pl.when(pl.program_id(2) == N_GROUPS - 1)
        def _():
            mx = jnp.maximum(jnp.maximum(lg[0][...], lg[1][...]), lg[2][...])
            e = [jnp.exp(lg[g][...] - mx) for g in range(N_GROUPS)]
            tot = e[0] + e[1] + e[2]
            m_ref[...] = ((e[0] * og[0][...] + e[1] * og[1][...] + e[2] * og[2][...]) / tot).astype(m_ref.dtype)
            l_ref[...] = mx + jnp.log(tot)

    return _call(body, name=name, grid=(nb, N_HEADS, N_GROUPS), in_specs=[qs, ks, vs],
                 out_specs=[pl.BlockSpec((None, seq, dh), lambda b, h, g: (0, b, h)),
                            pl.BlockSpec((seq, dh), lambda b, h, g: (b, h))],
                 out_shape=[_sds((1, T, N_HEADS * dh), MM), _sds((T, N_HEADS * dh), F32)],
                 scratch=[pltpu.VMEM((seq, dh), F32)] * (3 + 2 * N_GROUPS))(q, kv, kv)


def loss_fwd_bwd(name, x, target):
    T, D = x.shape
    tm = _tile(T, 512)

    def body(x_ref, t_ref, dx_ref, l_ref):
        @pl.when(pl.program_id(0) == 0)
        def _():
            l_ref[...] = jnp.zeros_like(l_ref)
        err = x_ref[...] - t_ref[...]
        dx_ref[...] = err * (1.0 / D)
        l_ref[...] += 0.5 * jnp.sum(jnp.mean(err * err, axis=-1, keepdims=True), axis=0, keepdims=True)

    dx, l = _call(body, name=name, grid=(T // tm,),
                  in_specs=[pl.BlockSpec((tm, D), lambda i: (i, 0))] * 2,
                  out_specs=[pl.BlockSpec((tm, D), lambda i: (i, 0)), pl.BlockSpec((1, 1), lambda i: (0, 0))],
                  out_shape=[_sds((T, D), F32), _sds((1, 1), F32)])(x, target)
    return dx, l


def resid_norm_bwd(name, dx, y, gain):
    T, D = y.shape
    tm = _tile(T, 512)

    def body(dx_ref, y_ref, g_ref, dy_ref, dg_ref, db_ref):
        @pl.when(pl.program_id(0) == 0)
        def _():
            dg_ref[...] = jnp.zeros_like(dg_ref)
            db_ref[...] = jnp.zeros_like(db_ref)
        y = y_ref[...]
        d = dx_ref[...]
        r = lax.rsqrt(jnp.mean(y * y, axis=-1, keepdims=True) + EPS)
        yh = y * r
        dyh = d * g_ref[...]
        dy = r * (dyh - yh * jnp.mean(dyh * yh, axis=-1, keepdims=True))
        dy_ref[...] = dy.astype(dy_ref.dtype)
        dg_ref[...] += jnp.sum(d * yh, axis=0, keepdims=True)
        db_ref[...] += jnp.sum(dy, axis=0, keepdims=True)

    return _call(body, name=name, grid=(T // tm,),
                 in_specs=[pl.BlockSpec((tm, D), lambda i: (i, 0))] * 2 + [pl.BlockSpec((1, D), lambda i: (0, 0))],
                 out_specs=[pl.BlockSpec((None, tm, D), lambda i: (0, i, 0))] + [pl.BlockSpec((1, D), lambda i: (0, 0))] * 2,
                 out_shape=[_sds((1, T, D), MM), _sds((1, D), F32), _sds((1, D), F32)])(dx, y, gain.reshape(1, D))


def mm_nt(name, dy, w, layer, out_dtype):
    _, T, D = dy.shape
    _, nk, kk, _ = w.shape
    tm = _tile(T, 1024)

    def body(dy_ref, w_ref, o_ref):
        o_ref[...] = _dot_nt(dy_ref[...], w_ref[...]).astype(o_ref.dtype)

    return _call(body, name=name, grid=(T // tm, nk),
                 in_specs=[pl.BlockSpec((None, tm, D), lambda i, q: (0, i, 0)),
                           pl.BlockSpec((None, None, kk, D), lambda i, q: (layer, q, 0, 0))],
                 out_specs=pl.BlockSpec((None, tm, kk), lambda i, q: (q, i, 0)),
                 out_shape=_sds((nk, T, kk), out_dtype))(dy, w)


def mm_nt_norm_bwd(name, du, w, layer, x_in, gain, dx_res):
    nsh, T, n = du.shape
    D = x_in.shape[1]
    tm = _tile(T, 512)

    def body(du_ref, w_ref, x_ref, g_ref, dr_ref, dx_ref, dg_ref, acc_ref):
        i, j = pl.program_id(0), pl.program_id(1)

        @pl.when((i == 0) & (j == 0))
        def _():
            dg_ref[...] = jnp.zeros_like(dg_ref)

        part = _dot_nt(du_ref[...], w_ref[...])

        @pl.when(j == 0)
        def _():
            acc_ref[...] = part

        @pl.when(j > 0)
        def _():
            acc_ref[...] += part

        @pl.when(j == nsh - 1)
        def _():
            x = x_ref[...]
            dh = acc_ref[...]
            r = lax.rsqrt(jnp.mean(x * x, axis=-1, keepdims=True) + EPS)
            xh = x * r
            dxh = dh * g_ref[...]
            dx_ref[...] = dr_ref[...] + r * (dxh - xh * jnp.mean(dxh * xh, axis=-1, keepdims=True))
            dg_ref[...] += jnp.sum(dh * xh, axis=0, keepdims=True)

    return _call(body, name=name, grid=(T // tm, nsh),
                 in_specs=[pl.BlockSpec((None, tm, n), lambda i, j: (j, i, 0)),
                           pl.BlockSpec((None, None, D, n), lambda i, j: (j, layer, 0, 0)),
                           pl.BlockSpec((tm, D), lambda i, j: (i, 0)),
                           pl.BlockSpec((1, D), lambda i, j: (0, 0)),
                           pl.BlockSpec((tm, D), lambda i, j: (i, 0))],
                 out_specs=[pl.BlockSpec((tm, D), lambda i, j: (i, 0)), pl.BlockSpec((1, D), lambda i, j: (0, 0))],
                 out_shape=[_sds((T, D), F32), _sds((1, D), F32)],
                 scratch=[pltpu.VMEM((tm, D), F32)])(du, w, x_in, gain.reshape(1, D), dx_res)


def mm_tn(name, a, b):
    na, T, ka = a.shape
    nb, _, kb = b.shape
    nj = max(na, nb)
    tk = _tile(T, 1024)
    nt = T // tk

    def body(a_ref, b_ref, o_ref, acc_ref):
        t = pl.program_id(1)
        part = _dot_tn(a_ref[...], b_ref[...])

        @pl.when(t == 0)
        def _():
            acc_ref[...] = part

        @pl.when(t > 0)
        def _():
            acc_ref[...] += part

        @pl.when(t == nt - 1)
        def _():
            o_ref[...] = acc_ref[...].astype(o_ref.dtype)

    return _call(body, name=name, grid=(nj, nt),
                 in_specs=[pl.BlockSpec((None, tk, ka), (lambda j, t: (j, t, 0)) if na > 1 else (lambda j, t: (0, t, 0))),
                           pl.BlockSpec((None, tk, kb), (lambda j, t: (j, t, 0)) if nb > 1 else (lambda j, t: (0, t, 0)))],
                 out_specs=pl.BlockSpec((None, ka, kb), lambda j, t: (j, 0, 0)),
                 out_shape=_sds((nj, ka, kb), MM),
                 scratch=[pltpu.VMEM((ka, kb), F32)])(a, b)


def ffn_conv_gate_bwd(name, u, dz, dw, dwb, layer, seq):
    _, T, n = u.shape
    tm = _tile(seq, 512)
    tps = seq // tm
    hb = FFN_HALO
    prev, nxt = _halo_maps(tm, hb, T)
    u4 = u.reshape(2, 4, T, n)
    dw4 = dw.reshape(2, 4, dw.shape[1], FFN_CONV_W, n)
    dwb4 = dwb.reshape(dwb.shape[0], 2, 4, 1, n)
    K = FFN_CONV_W
    te = tm + hb

    def body(u_ref, up_ref, un_ref, dz_ref, dzn_ref, w_ref, b_ref, du_ref, ddw_ref, ddb_ref, pad_ref, da_ref):
        i = pl.program_id(1)
        first = (i % tps) == 0
        last = (i % tps) == tps - 1

        @pl.when(i == 0)
        def _():
            ddw_ref[...] = jnp.zeros_like(ddw_ref)
            ddb_ref[...] = jnp.zeros_like(ddb_ref)

        for half in range(2):
            pad_ref[half, pl.ds(0, hb), :] = jnp.where(first, 0.0, up_ref[half].astype(F32))
            pad_ref[half, pl.ds(hb, tm), :] = u_ref[half].astype(F32)
            pad_ref[half, pl.ds(hb + tm, hb), :] = un_ref[half].astype(F32)
        conv = []
        for half in range(2):
            acc = jnp.zeros((te, n), F32) + b_ref[half]
            for k in range(K):
                acc = acc + w_ref[half, pl.ds(k, 1), :] * pad_ref[half, pl.ds(hb - (K - 1) + k, te), :]
            conv.append(acc)
        a, g = conv
        dzm = dz_ref[...].astype(F32)
        dzn = jnp.where(last, 0.0, dzn_ref[...].astype(F32))
        sg = _sigmoid(g)
        silu = g * sg
        dsilu = sg * (1.0 + g * (1.0 - sg))
        da_ref[0, pl.ds(0, tm), :] = dzm * silu[:tm]
        da_ref[0, pl.ds(tm, hb), :] = dzn * silu[tm:]
        da_ref[1, pl.ds(0, tm), :] = dzm * a[:tm] * dsilu[:tm]
        da_ref[1, pl.ds(tm, hb), :] = dzn * a[tm:] * dsilu[tm:]
        for half in range(2):
            dmain = da_ref[half, pl.ds(0, tm), :]
            acc = jnp.zeros((tm, n), F32)
            for k in range(K):
                acc = acc + w_ref[half, pl.ds(k, 1), :] * da_ref[half, pl.ds(K - 1 - k, tm), :]
                ddw_ref[half, pl.ds(k, 1), :] += jnp.sum(
                    dmain * pad_ref[half, pl.ds(hb - (K - 1) + k, tm), :], axis=0, keepdims=True)
            du_ref[half] = acc.astype(du_ref.dtype)
            ddb_ref[half] += jnp.sum(dmain, axis=0, keepdims=True)

    du, ddw, ddb = _call(
        body, name=name, grid=(4, T // tm),
        in_specs=[pl.BlockSpec((2, None, tm, n), lambda j, i: (0, j, i, 0)),
                  pl.BlockSpec((2, None, hb, n), lambda j, i: (0, j, prev(i), 0)),
                  pl.BlockSpec((2, None, hb, n), lambda j, i: (0, j, nxt(i), 0)),
                  pl.BlockSpec((None, tm, n), lambda j, i: (j, i, 0)),
                  pl.BlockSpec((None, hb, n), lambda j, i: (j, nxt(i), 0)),
                  pl.BlockSpec((2, None, None, K, n), lambda j, i: (0, j, layer, 0, 0)),
                  pl.BlockSpec((None, 2, None, 1, n), lambda j, i: (layer, 0, j, 0, 0))],
        out_specs=[pl.BlockSpec((2, None, tm, n), lambda j, i: (0, j, i, 0)),
                   pl.BlockSpec((2, None, K, n), lambda j, i: (0, j, 0, 0)),
                   pl.BlockSpec((2, None, 1, n), lambda j, i: (0, j, 0, 0))],
        out_shape=[_sds((2, 4, T, n), MM), _sds((2, 4, K, n), F32), _sds((2, 4, 1, n), F32)],
        scratch=[pltpu.VMEM((2, tm + 2 * hb, n), F32), pltpu.VMEM((2, te, n), F32)])(u4, u4, u4, dz, dz, dw4, dwb4)
    return du.reshape(8, T, n), ddw, ddb


def ln_silu_bwd(name, ds, c, g, b):
    T, D = c.shape
    tm = _tile(T, 512)

    def body(ds_ref, c_ref, g_ref, b_ref, dc_ref, dg_ref, db_ref):
        @pl.when(pl.program_id(0) == 0)
        def _():
            dg_ref[...] = jnp.zeros_like(dg_ref)
            db_ref[...] = jnp.zeros_like(db_ref)
        cf = c_ref[...]
        mu = jnp.mean(cf, axis=-1, keepdims=True)
        xc = cf - mu
        r = lax.rsqrt(jnp.mean(xc * xc, axis=-1, keepdims=True) + EPS)
        xh = xc * r
        t = xh * g_ref[...] + b_ref[...]
        sg = _sigmoid(t)
        dt = ds_ref[...] * (sg * (1.0 + t * (1.0 - sg)))
        dg_ref[...] += jnp.sum(dt * xh, axis=0, keepdims=True)
        db_ref[...] += jnp.sum(dt, axis=0, keepdims=True)
        dxh = dt * g_ref[...]
        dc_ref[...] = r * (dxh - jnp.mean(dxh, axis=-1, keepdims=True)
                           - xh * jnp.mean(dxh * xh, axis=-1, keepdims=True))

    vec = pl.BlockSpec((1, D), lambda i: (0, 0))
    return _call(body, name=name, grid=(T // tm,),
                 in_specs=[pl.BlockSpec((None, tm, D), lambda i: (0, i, 0)), pl.BlockSpec((tm, D), lambda i: (i, 0)),
                           vec, vec],
                 out_specs=[pl.BlockSpec((tm, D), lambda i: (i, 0)), vec, vec],
                 out_shape=[_sds((T, D), F32), _sds((1, D), F32), _sds((1, D), F32)])(
                     ds, c, g.reshape(1, D), b.reshape(1, D))


def cm_glu_conv_bwd(name, u, dc, dw, layer, seq):
    _, T, n = u.shape
    ct = dw.shape[-1]
    per = n // ct
    nct = 4 * per
    tm = _tile(seq, 512)
    tps = seq // tm
    hb = CONV_HALO
    prev, nxt = _halo_maps(tm, hb, T)
    u4 = u.reshape(2, 4, T, n)
    K = CONV_W

    def body(u_ref, up_ref, dc_ref, dcn_ref, w_ref, du_ref, ddw_ref, ddb_ref, dbi_ref, padp_ref, padd_ref):
        i = pl.program_id(1)
        first = (i % tps) == 0
        last = (i % tps) == tps - 1

        @pl.when(i == 0)
        def _():
            ddw_ref[...] = jnp.zeros_like(ddw_ref)
            ddb_ref[...] = jnp.zeros_like(ddb_ref)
            dbi_ref[...] = jnp.zeros_like(dbi_ref)

        um = u_ref[...].astype(F32)
        uh = up_ref[...].astype(F32)
        sg = _sigmoid(um[1])
        padp_ref[pl.ds(hb, tm), :] = um[0] * sg
        padp_ref[pl.ds(0, hb), :] = jnp.where(first, 0.0, uh[0] * _sigmoid(uh[1]))
        dcm = dc_ref[...]
        padd_ref[pl.ds(0, tm), :] = dcm
        padd_ref[pl.ds(tm, hb), :] = jnp.where(last, 0.0, dcn_ref[...])
        dp = jnp.zeros((tm, ct), F32)
        for k in range(K):
            dp = dp + w_ref[pl.ds(k, 1), :] * padd_ref[pl.ds(K - 1 - k, tm), :]
            ddw_ref[pl.ds(k, 1), :] += jnp.sum(dcm * padp_ref[pl.ds(hb - (K - 1) + k, tm), :], axis=0, keepdims=True)
        ddb_ref[...] += jnp.sum(dcm, axis=0, keepdims=True)
        dv = dp * sg
        dg = dp * um[0] * sg * (1.0 - sg)
        du_ref[0] = dv.astype(du_ref.dtype)
        du_ref[1] = dg.astype(du_ref.dtype)
        dbi_ref[0] += jnp.sum(dv, axis=0, keepdims=True)
        dbi_ref[1] += jnp.sum(dg, axis=0, keepdims=True)

    du, ddw, ddb, dbi = _call(
        body, name=name, grid=(nct, T // tm),
        in_specs=[pl.BlockSpec((2, None, tm, ct), lambda c, i: (0, c // per, i, c % per)),
                  pl.BlockSpec((2, None, hb, ct), lambda c, i: (0, c // per, prev(i), c % per)),
                  pl.BlockSpec((tm, ct), lambda c, i: (i, c)),
                  pl.BlockSpec((hb, ct), lambda c, i: (nxt(i), c)),
                  pl.BlockSpec((None, None, K, ct), lambda c, i: (c, layer, 0, 0))],
        out_specs=[pl.BlockSpec((2, None, tm, ct), lambda c, i: (0, c // per, i, c % per)),
                   pl.BlockSpec((None, K, ct), lambda c, i: (c, 0, 0)),
                   pl.BlockSpec((None, 1, ct), lambda c, i: (c, 0, 0)),
                   pl.BlockSpec((2, None, 1, ct), lambda c, i: (0, c // per, 0, c % per))],
        out_shape=[_sds((2, 4, T, n), MM), _sds((nct, K, ct), F32), _sds((nct, 1, ct), F32), _sds((2, 4, 1, n), F32)],
        scratch=[pltpu.VMEM((tm + hb, ct), F32), pltpu.VMEM((tm + hb, ct), F32)])(u4, u4, dc, dc, dw)
    return du.reshape(8, T, n), ddw, ddb, dbi


def attn_bwd(name, q, kv, dm, merged, lse, dkv_prev, nb, seq):
    _, T, qn = q.shape
    kn = kv.shape[2]
    dh = qn * N_DEV // (N_GROUPS * N_HEADS)
    scale = 1.0 / (dh ** 0.5)
    qs, ks, vs = _head_specs(seq, dh, qn // dh, kn // dh)
    has_prev = dkv_prev is not None
    n_in = 6 + (2 if has_prev else 0)

    def body(*refs):
        q_ref, k_ref, v_ref, dm_ref, mg_ref, l_ref = refs[:6]
        pk_ref, pv_ref = refs[6:8] if has_prev else (None, None)
        dq_ref, dk_ref, dv_ref = refs[n_in:n_in + 3]
        qf, kf, vf, dqf, dkf, dvf, dlt = refs[n_in + 3:]
        cur_ok, prev_ok = _band_masks()
        dlt[...] = jnp.broadcast_to(jnp.sum(dm_ref[...] * mg_ref[...].astype(F32), axis=-1, keepdims=True), (seq, dh))
        qf[...] = q_ref[...].astype(F32)
        kf[...] = k_ref[...].astype(F32)
        vf[...] = v_ref[...].astype(F32)
        dkf[...] = jnp.zeros_like(dkf)
        dvf[...] = jnp.zeros_like(dvf)
        for g in range(N_GROUPS):
            d = DILATIONS[g]
            nblk = seq // d // ATT_BLOCK

            def block(idx, carry, d=d, nblk=nblk):
                r, n = idx // nblk, idx % nblk
                rc = _rows(r + d * ATT_BLOCK * n, d)
                qb = qf[rc, :].astype(MM)
                dmb = dm_ref[rc, :].astype(MM)
                lse_b = l_ref[rc, :][:, :1]
                dlt_b = dlt[rc, :][:, :1]

                def side(rk, ok):
                    kb = kf[rk, :].astype(MM)
                    vb = vf[rk, :].astype(MM)
                    s = jnp.where(ok, _dot_nt(qb, kb) * scale, NEG)
                    p = jnp.exp(s - lse_b)
                    dvf[rk, :] += _dot_tn(p.astype(MM), dmb)
                    dsc = (p * (_dot_nt(dmb, vb) - dlt_b) * scale).astype(MM)
                    dkf[rk, :] += _dot_tn(dsc, qb)
                    return _dot(dsc, kb)

                dq = side(rc, cur_ok)
                if nblk > 1:
                    rp = _rows(r + d * ATT_BLOCK * jnp.maximum(n - 1, 0), d)
                    dq = dq + side(rp, prev_ok & (n > 0))
                dqf[rc, :] = dq
                return carry

            @pl.when(pl.program_id(2) == g)
            def _(block=block, d=d, nblk=nblk):
                lax.fori_loop(0, d * nblk, block, 0)

        dq_ref[...] = dqf[...].astype(dq_ref.dtype)
        if has_prev:
            dk_ref[...] = (dkf[...] + pk_ref[...].astype(F32)).astype(dk_ref.dtype)
            dv_ref[...] = (dvf[...] + pv_ref[...].astype(F32)).astype(dv_ref.dtype)
        else:
            dk_ref[...] = dkf[...].astype(dk_ref.dtype)
            dv_ref[...] = dvf[...].astype(dv_ref.dtype)

    full = pl.BlockSpec((None, seq, dh), lambda b, h, g: (0, b, h))
    in_specs = [qs, ks, vs, full, full, pl.BlockSpec((seq, dh), lambda b, h, g: (b, h))]
    args = [q, kv, kv, dm, merged, lse]
    if has_prev:
        in_specs += [ks, ks]
        args += [dkv_prev[0], dkv_prev[1]]
    dq, dk, dv = _call(body, name=name, grid=(nb, N_HEADS, N_GROUPS), in_specs=in_specs, out_specs=[qs, ks, ks],
                       out_shape=[_sds(q.shape, MM), _sds((N_DEV // 2, T, kn), MM), _sds((N_DEV // 2, T, kn), MM)],
                       scratch=[pltpu.VMEM((seq, dh), F32)] * 7)(*args)
    return dq, (dk, dv)


def _adamw_math(w, g, m, v):
    m = ADAM_B1 * m + (1.0 - ADAM_B1) * g
    v = ADAM_B2 * v + (1.0 - ADAM_B2) * (g * g)
    m_hat = m / (1.0 - ADAM_B1 ** ADAM_STEP)
    v_hat = v / (1.0 - ADAM_B2 ** ADAM_STEP)
    delta = -ADAM_LR * (m_hat / (jnp.sqrt(v_hat) + ADAM_EPS) + ADAM_WD * w)
    return delta, m, v


def adamw_sum(name, w, m, v, parts):
    R, C = w.shape
    tr = _tile(R, 256)

    def body(w_ref, m_ref, v_ref, p_ref, g_ref, d_ref, nm_ref, nv_ref):
        g = p_ref[0].astype(F32)
        for k in range(1, N_DEV):
            g = g + p_ref[k].astype(F32)
        g_ref[...] = g
        d_ref[...], nm_ref[...], nv_ref[...] = _adamw_math(w_ref[...], g, m_ref[...], v_ref[...])

    blk = pl.BlockSpec((tr, C), lambda i: (i, 0))
    return _call(body, name=name, grid=(R // tr,),
                 in_specs=[blk, blk, blk, pl.BlockSpec((N_DEV, tr, C), lambda i: (0, i, 0))],
                 out_specs=[blk] * 4, out_shape=[_sds((R, C), F32)] * 4)(w, m, v, parts)


def sum_partials(name, parts):
    _, R, C = parts.shape
    tr = _tile(R, 512)

    def body(p_ref, o_ref):
        g = p_ref[0]
        for k in range(1, N_DEV):
            g = g + p_ref[k]
        o_ref[...] = g

    return _call(body, name=name, grid=(R // tr,),
                 in_specs=[pl.BlockSpec((N_DEV, tr, C), lambda i: (0, i, 0))],
                 out_specs=pl.BlockSpec((tr, C), lambda i: (i, 0)), out_shape=_sds((R, C), F32))(parts)


def adamw_small(name, w, g, m, v):
    R, C = w.shape
    tr = _tile(R, 512)

    def body(w_ref, g_ref, m_ref, v_ref, d_ref, nm_ref, nv_ref):
        d_ref[...], nm_ref[...], nv_ref[...] = _adamw_math(w_ref[...], g_ref[...], m_ref[...], v_ref[...])

    blk = pl.BlockSpec((tr, C), lambda i: (i, 0))
    return _call(body, name=name, grid=(R // tr,), in_specs=[blk] * 4, out_specs=[blk] * 3,
                 out_shape=[_sds((R, C), F32)] * 3)(w, g, m, v)


def _pack(arrays):
    pieces = []
    for a in arrays:
        f = a.reshape(-1).astype(F32)
        pieces.append(jnp.pad(f, (0, (-f.shape[0]) % PACK)))
    return jnp.concatenate(pieces).reshape(-1, 128)


def _unpack(flat, shapes):
    out, off = [], 0
    f = flat.reshape(-1)
    for s in shapes:
        size = 1
        for d in s:
            size *= d
        out.append(f[off:off + size].reshape(s))
        off += size + (-size) % PACK
    return out


def kernel(x, mix_pre_g, mix_post_g, ffn_pre_g, ffn_post_g, cm_w_in, cm_b_in, cm_dw, cm_dw_b, cm_ln_g, cm_ln_b, cm_w_out, cm_b_out, kv_norm_g, w_kv, w_q, w_o, ffn_w_in, ffn_dw, ffn_dw_b, ffn_w_out, loss_target, m_mix_pre_g, m_mix_post_g, m_ffn_pre_g, m_ffn_post_g, m_cm_w_in, m_cm_b_in, m_cm_dw, m_cm_dw_b, m_cm_ln_g, m_cm_ln_b, m_cm_w_out, m_cm_b_out, m_kv_norm_g, m_w_kv, m_w_q, m_w_o, m_ffn_w_in, m_ffn_dw, m_ffn_dw_b, m_ffn_w_out, v_mix_pre_g, v_mix_post_g, v_ffn_pre_g, v_ffn_post_g, v_cm_w_in, v_cm_b_in, v_cm_dw, v_cm_dw_b, v_cm_ln_g, v_cm_ln_b, v_cm_w_out, v_cm_b_out, v_kv_norm_g, v_w_kv, v_w_q, v_w_o, v_ffn_w_in, v_ffn_dw, v_ffn_dw_b, v_ffn_w_out):
    nb, seq, D = x.shape
    T = nb * seq
    me = _my_index()
    n_b = DEPTH - N_A

    big = [cm_w_in, w_kv[None], w_q, ffn_w_in, cm_w_out, w_o, ffn_w_out]
    row_sharded = [False, False, False, False, True, True, True]
    Wcin, Wkv, Wq, Wfin, Wcout, Wo, Wfout = all_gather("gather_weights", [a.astype(MM) for a in big], row_sharded)
    Wcout = Wcout.reshape(N_A, 1, D, D)
    Wo = Wo.reshape(n_b, 1, D, D)
    nf = ffn_w_in.shape[-1]
    Wfout = Wfout.reshape(DEPTH, 4, nf, D)
    small = [cm_b_in[:, None, :], cm_dw, cm_dw_b[:, None, :], cm_ln_g, cm_ln_b, cm_b_out, ffn_dw]
    Bcin, DWc, DWBc, LNg, LNb, Bcout, DWf = all_gather("gather_small", small, [False] * len(small))
    LNg = jnp.swapaxes(LNg, 0, 1).reshape(N_A, D)
    LNb = jnp.swapaxes(LNb, 0, 1).reshape(N_A, D)
    Bcout = jnp.swapaxes(Bcout, 0, 1).reshape(N_A, D)
    DWBf = ffn_dw_b.reshape(DEPTH, N_DEV, 1, nf)
    zero_bias = jnp.zeros((D,), F32)

    xs = x.reshape(T, D)
    sv = []
    kv = hkv = None
    for i in range(DEPTH):
        L = {"x_in": xs}
        if i < N_A:
            L["u"], L["h"] = norm_mm(f"cm_in_{i}", xs, mix_pre_g[i], Wcin, i, Bcin)
            L["c"] = cm_glu_conv(f"cm_conv_{i}", L["u"], DWc, DWBc, i, seq)
            L["s"] = ln_silu(f"cm_ln_{i}", L["c"], LNg[i], LNb[i])
            L["y"], x1 = mm_resid_norm(f"cm_out_{i}", L["s"], Wcout, i, Bcout[i], xs, mix_post_g[i])
        else:
            j = i - N_A
            L["q"], L["h"] = norm_mm(f"attn_q_{i}", xs, mix_pre_g[i], Wq, j)
            L["mg"], L["lse"] = attn_fwd(f"attn_{i}", L["q"], kv, nb, seq)
            L["y"], x1 = mm_resid_norm(f"attn_out_{i}", L["mg"], Wo, j, zero_bias, xs, mix_post_g[i])
        L["x1"] = x1
        L["uf"], L["hf"] = norm_mm(f"ffn_in_{i}", x1, ffn_pre_g[i], Wfin, i)
        L["z"] = ffn_conv_gate(f"ffn_conv_{i}", L["uf"], DWf, DWBf, i, seq)
        L["yf"], xs = mm_resid_norm(f"ffn_out_{i}", L["z"], Wfout, i, zero_bias, x1, ffn_post_g[i])
        if i == N_A - 1:
            kv, hkv = norm_mm("kv_proj", xs, kv_norm_g, Wkv, 0)
        sv.append(L)
    dx, loss_part = loss_fwd_bwd("loss", xs, loss_target.reshape(T, D))

    g_mix_pre, g_mix_post, g_ffn_pre, g_ffn_post = [None] * DEPTH, [None] * DEPTH, [None] * DEPTH, [None] * DEPTH
    g_ffn_dw, g_ffn_dwb = [None] * DEPTH, [None] * DEPTH
    p_fin, p_fout = [None] * DEPTH, [None] * DEPTH
    p_cin, p_cout, p_q, p_o = [None] * N_A, [None] * N_A, [None] * n_b, [None] * n_b
    g_cbin, g_cdw, g_cdwb, g_lng, g_lnb, g_cbout = ([None] * N_A for _ in range(6))
    p_kv = g_kvn = dkv = None
    for i in reversed(range(DEPTH)):
        L = sv[i]
        dyf, g_ffn_post[i], _ = resid_norm_bwd(f"ffn_post_bwd_{i}", dx, L["yf"], ffn_post_g[i])
        dz = mm_nt(f"ffn_out_bwd_{i}", dyf, Wfout, i, MM)
        p_fout[i] = mm_tn(f"ffn_out_wg_{i}", L["z"], dyf).reshape(N_DEV, nf // 2, D)
        duf, ddw, ddwb = ffn_conv_gate_bwd(f"ffn_conv_bwd_{i}", L["uf"], dz, DWf, DWBf, i, seq)
        g_ffn_dw[i], g_ffn_dwb[i] = ddw.reshape(N_DEV, FFN_CONV_W, nf), ddwb.reshape(-1)
        p_fin[i] = mm_tn(f"ffn_in_wg_{i}", L["hf"][None], duf)
        dx1, g_ffn_pre[i] = mm_nt_norm_bwd(f"ffn_in_bwd_{i}", duf, Wfin, i, L["x1"], ffn_pre_g[i], dx)
        dy, g_mix_post[i], dyb = resid_norm_bwd(f"mix_post_bwd_{i}", dx1, L["y"], mix_post_g[i])
        if i >= N_A:
            j = i - N_A
            dm = mm_nt(f"attn_out_bwd_{i}", dy, Wo, j, F32)
            p_o[j] = mm_tn(f"attn_out_wg_{i}", L["mg"], dy).reshape(N_DEV, D // N_DEV, D)
            dq, dkv = attn_bwd(f"attn_bwd_{i}", L["q"], kv, dm, L["mg"], L["lse"], dkv, nb, seq)
            p_q[j] = mm_tn(f"attn_q_wg_{i}", L["h"][None], dq)
            dx, g_mix_pre[i] = mm_nt_norm_bwd(f"attn_q_bwd_{i}", dq, Wq, j, L["x_in"], mix_pre_g[i], dx1)
            if i == N_A:
                dkv_all = jnp.concatenate(dkv, axis=0)
                p_kv = mm_tn("kv_wg", hkv[None], dkv_all)
                dx, g_kvn = mm_nt_norm_bwd("kv_bwd", dkv_all, Wkv, 0, L["x_in"], kv_norm_g, dx)
        else:
            g_cbout[i] = dyb
            ds = mm_nt(f"cm_out_bwd_{i}", dy, Wcout, i, F32)
            p_cout[i] = mm_tn(f"cm_out_wg_{i}", L["s"], dy).reshape(N_DEV, D // N_DEV, D)
            dc, g_lng[i], g_lnb[i] = ln_silu_bwd(f"cm_ln_bwd_{i}", ds, L["c"], LNg[i], LNb[i])
            du, g_cdw[i], g_cdwb[i], dbi = cm_glu_conv_bwd(f"cm_conv_bwd_{i}", L["u"], dc, DWc, i, seq)
            g_cbin[i] = dbi.reshape(N_DEV, -1)
            p_cin[i] = mm_tn(f"cm_in_wg_{i}", L["h"][None], du)
            dx, g_mix_pre[i] = mm_nt_norm_bwd(f"cm_in_bwd_{i}", du, Wcin, i, L["x_in"], mix_pre_g[i], dx1)
    grad_x = dx.reshape(nb, seq, D)

    parts = p_cin + [p_kv] + p_q + p_fin + p_cout + p_o + p_fout
    recv = scatter_partials("scatter_grads", parts)
    it = iter(recv)
    take = lambda n: jnp.stack([next(it) for _ in range(n)], axis=1)
    r_cin, r_kv, r_q, r_fin, r_cout, r_o, r_fout = take(N_A), take(1), take(n_b), take(DEPTH), take(N_A), take(n_b), take(DEPTH)

    def big_update(name, w, m, v, r):
        C = w.shape[-1]
        g, d, nm, nv = adamw_sum(name, w.reshape(-1, C), m.reshape(-1, C), v.reshape(-1, C), r.reshape(N_DEV, -1, C))
        return [t.reshape(w.shape) for t in (g, d, nm, nv)]

    upd = {
        "cm_w_in": big_update("adam_cm_w_in", cm_w_in, m_cm_w_in, v_cm_w_in, r_cin),
        "w_kv": big_update("adam_w_kv", w_kv, m_w_kv, v_w_kv, r_kv),
        "w_q": big_update("adam_w_q", w_q, m_w_q, v_w_q, r_q),
        "ffn_w_in": big_update("adam_ffn_w_in", ffn_w_in, m_ffn_w_in, v_ffn_w_in, r_fin),
        "cm_w_out": big_update("adam_cm_w_out", cm_w_out, m_cm_w_out, v_cm_w_out, r_cout),
        "w_o": big_update("adam_w_o", w_o, m_w_o, v_w_o, r_o),
        "ffn_w_out": big_update("adam_ffn_w_out", ffn_w_out, m_ffn_w_out, v_ffn_w_out, r_fout),
    }

    rep_names = ["mix_pre_g", "mix_post_g", "ffn_pre_g", "ffn_post_g", "kv_norm_g", "ffn_dw_b"]
    rep_parts = [jnp.concatenate(g_mix_pre), jnp.concatenate(g_mix_post), jnp.concatenate(g_ffn_pre),
                 jnp.concatenate(g_ffn_post), g_kvn.reshape(-1), jnp.stack(g_ffn_dwb)]
    rep_w = [mix_pre_g, mix_post_g, ffn_pre_g, ffn_post_g, kv_norm_g, ffn_dw_b]
    rep_m = [m_mix_pre_g, m_mix_post_g, m_ffn_pre_g, m_ffn_post_g, m_kv_norm_g, m_ffn_dw_b]
    rep_v = [v_mix_pre_g, v_mix_post_g, v_ffn_pre_g, v_ffn_post_g, v_kv_norm_g, v_ffn_dw_b]
    sh_names = ["ffn_dw", "cm_b_in", "cm_dw", "cm_dw_b", "cm_ln_g", "cm_ln_b", "cm_b_out"]
    own = lambda per_layer, shard: jnp.stack([p.reshape((N_DEV,) + shard) for p in per_layer], axis=1)
    sh_parts = [own(g_ffn_dw, ffn_dw.shape[1:]), own(g_cbin, cm_b_in.shape[1:]), own(g_cdw, cm_dw.shape[1:]),
                own(g_cdwb, cm_dw_b.shape[1:]), own(g_lng, cm_ln_g.shape[1:]), own(g_lnb, cm_ln_b.shape[1:]),
                own(g_cbout, cm_b_out.shape[1:])]
    sh_w = [ffn_dw, cm_b_in, cm_dw, cm_dw_b, cm_ln_g, cm_ln_b, cm_b_out]
    sh_m = [m_ffn_dw, m_cm_b_in, m_cm_dw, m_cm_dw_b, m_cm_ln_g, m_cm_ln_b, m_cm_b_out]
    sh_v = [v_ffn_dw, v_cm_b_in, v_cm_dw, v_cm_dw_b, v_cm_ln_g, v_cm_ln_b, v_cm_b_out]
    rep_pack = _pack([loss_part] + rep_parts)
    sh_packs = [_pack([p[k] for p in sh_parts]) for k in range(N_DEV)]
    n_rep, n_sh = rep_pack.shape[0], sh_packs[0].shape[0]
    (gathered,) = all_gather("gather_small_grads", [jnp.concatenate([rep_pack] + sh_packs)], [False])
    total = sum_partials("sum_small_grads", gathered)
    rep_sum = total[:n_rep]
    sh_sum = lax.dynamic_slice_in_dim(total, n_rep + me * n_sh, n_sh, axis=0)
    rep_shapes = [(1, 1)] + [w.shape for w in rep_w]
    sh_shapes = [w.shape for w in sh_w]
    g_small = jnp.concatenate([rep_sum, sh_sum])
    pad1 = jnp.zeros((1, 1), F32)
    d_s, m_s, v_s = adamw_small("adam_small", jnp.concatenate([_pack([pad1] + rep_w), _pack(sh_w)]), g_small,
                                jnp.concatenate([_pack([pad1] + rep_m), _pack(sh_m)]),
                                jnp.concatenate([_pack([pad1] + rep_v), _pack(sh_v)]))
    split = lambda t: (_unpack(t[:n_rep], rep_shapes), _unpack(t[n_rep:], sh_shapes))
    for (rep_t, sh_t), slot in zip([split(g_small), split(d_s), split(m_s), split(v_s)], range(4)):
        if slot == 0:
            loss = rep_t[0].reshape(())
        for name, t in zip(rep_names, rep_t[1:]):
            upd.setdefault(name, [None] * 4)[slot] = t
        for name, t in zip(sh_names, sh_t):
            upd.setdefault(name, [None] * 4)[slot] = t

    order = ["mix_pre_g", "mix_post_g", "ffn_pre_g", "ffn_post_g", "cm_w_in", "cm_b_in", "cm_dw", "cm_dw_b", "cm_ln_g",
             "cm_ln_b", "cm_w_out", "cm_b_out", "kv_norm_g", "w_kv", "w_q", "w_o", "ffn_w_in", "ffn_dw", "ffn_dw_b",
             "ffn_w_out"]
    return (loss, grad_x, *[upd[n][0] for n in order], *[upd[n][1] for n in order],
            *[upd[n][2] for n in order], *[upd[n][3] for n in order])
```

```python
import functools

import jax
import jax.numpy as jnp
from jax import lax
from jax.experimental import pallas as pl
from jax.experimental.pallas import tpu as pltpu

N_DEV = 8
N_A = 2
DEPTH = 4
N_HEADS = 8
N_GROUPS = 3
DILATIONS = (1, 4, 16)
ATT_BLOCK = 128
CONV_W = 31
FFN_CONV_W = 3
CONV_HALO = 32
FFN_HALO = 16
EPS = 1e-6
NEG = -1e30
ADAM_LR, ADAM_B1, ADAM_B2, ADAM_EPS, ADAM_WD, ADAM_STEP = 0.001, 0.9, 0.999, 1e-08, 0.01, 10
MM = jnp.bfloat16
F32 = jnp.float32
VMEM_LIMIT_BYTES = 56 * 1024 * 1024
PACK = 1024
MESH_ID = pl.DeviceIdType.MESH

_pallas = pl.pallas_call


def _call(body, *, name, out_shape, grid=(), in_specs=None, out_specs=None, scratch=()):
    return _pallas(body, name=name, out_shape=out_shape, grid=grid, in_specs=in_specs, out_specs=out_specs,
                   scratch_shapes=list(scratch),
                   compiler_params=pltpu.CompilerParams(vmem_limit_bytes=VMEM_LIMIT_BYTES))


def _tile(n, pref):
    if n <= pref:
        return n
    t = pref - pref % 8
    while n % t:
        t -= 8
    assert t > 0, (n, pref)
    return t


def _sds(shape, dtype):
    return jax.ShapeDtypeStruct(tuple(shape), dtype)


def _dot(a, b):
    return jnp.dot(a, b, preferred_element_type=F32)


def _dot_nt(a, b):
    return lax.dot_general(a, b, (((1,), (1,)), ((), ())), preferred_element_type=F32)


def _dot_tn(a, b):
    return lax.dot_general(a, b, (((0,), (0,)), ((), ())), preferred_element_type=F32)


def _sigmoid(x):
    return 1.0 / (1.0 + jnp.exp(-x))


def _my_index():
    return 4 * lax.axis_index("x") + 2 * lax.axis_index("y") + lax.axis_index("c")


def _exchange(name, arrays, out_shapes, pieces, src_of, dst_of):
    n = len(arrays)
    base = [sum(pieces[:a]) for a in range(n)]
    total = sum(pieces)

    def body(*refs):
        ins, outs = refs[:n], refs[n:2 * n]
        send_sems, recv_sems, local_sems = refs[2 * n:]
        x, y, c = lax.axis_index("x"), lax.axis_index("y"), lax.axis_index("c")
        me = 4 * x + 2 * y + c
        copies = []
        for a in range(n):
            for k, (s, d) in enumerate(zip(src_of(a, ins[a], me), dst_of(a, outs[a], me))):
                cp = pltpu.make_async_copy(s, d, local_sems.at[base[a] + k])
                cp.start()
                copies.append(cp)
        remote = []
        for m in range(1, N_DEV):
            px, py, pc = x ^ (m >> 2), y ^ ((m >> 1) & 1), c ^ (m & 1)
            peer = 4 * px + 2 * py + pc
            for a in range(n):
                for k, (s, d) in enumerate(zip(src_of(a, ins[a], peer), dst_of(a, outs[a], me))):
                    cp = pltpu.make_async_remote_copy(src_ref=s, dst_ref=d, send_sem=send_sems.at[base[a] + k, m - 1],
                                                      recv_sem=recv_sems.at[base[a] + k, m - 1],
                                                      device_id=(px, py, pc), device_id_type=MESH_ID)
                    cp.start()
                    remote.append(cp)
        for cp in copies:
            cp.wait()
        for cp in remote:
            cp.wait_send()
        for m in range(1, N_DEV):
            px, py, pc = x ^ (m >> 2), y ^ ((m >> 1) & 1), c ^ (m & 1)
            peer = 4 * px + 2 * py + pc
            for a in range(n):
                for k, (s, d) in enumerate(zip(src_of(a, ins[a], me), dst_of(a, outs[a], peer))):
                    pltpu.make_async_remote_copy(src_ref=s, dst_ref=d, send_sem=send_sems.at[base[a] + k, m - 1],
                                                 recv_sem=recv_sems.at[base[a] + k, m - 1], device_id=(px, py, pc),
                                                 device_id_type=MESH_ID).wait_recv()

    any_spec = pl.BlockSpec(memory_space=pl.ANY)
    return _call(body, name=name, out_shape=[_sds(s, a.dtype) for s, a in zip(out_shapes, arrays)],
                 in_specs=[any_spec] * n, out_specs=[any_spec] * n,
                 scratch=[pltpu.SemaphoreType.DMA((total, N_DEV - 1)), pltpu.SemaphoreType.DMA((total, N_DEV - 1)),
                          pltpu.SemaphoreType.DMA((total,))])(*arrays)


def all_gather(name, arrays, row_sharded):
    def out_shape(a):
        s = arrays[a].shape
        return (s[0], N_DEV) + s[1:] if row_sharded[a] else (N_DEV,) + s

    def src_of(a, ref, peer):
        if row_sharded[a]:
            return [ref.at[l] for l in range(arrays[a].shape[0])]
        return [ref]

    def dst_of(a, ref, me):
        if row_sharded[a]:
            return [ref.at[l, me] for l in range(arrays[a].shape[0])]
        return [ref.at[me]]

    pieces = [arrays[a].shape[0] if row_sharded[a] else 1 for a in range(len(arrays))]
    return _exchange(name, arrays, [out_shape(a) for a in range(len(arrays))], pieces, src_of, dst_of)


def _src_view(kind, ref, peer):
    return ref if kind == "gather" else ref.at[peer]


def _peers(x, y, c):
    for m in range(1, N_DEV):
        px, py, pc = x ^ (m >> 2), y ^ ((m >> 1) & 1), c ^ (m & 1)
        yield m - 1, (px, py, pc), 4 * px + 2 * py + pc


def place_own(name, kind, srcs):
    n = len(srcs)
    shapes = [(N_DEV,) + s.shape if kind == "gather" else s.shape for s in srcs]

    def body(*refs):
        ins, outs, sems = refs[:n], refs[n:2 * n], refs[2 * n]
        me = _my_index()
        cps = [pltpu.make_async_copy(_src_view(kind, ins[a], me), outs[a].at[me], sems.at[a]) for a in range(n)]
        for cp in cps:
            cp.start()
        for cp in cps:
            cp.wait()

    any_spec = pl.BlockSpec(memory_space=pl.ANY)
    return _call(body, name=name, out_shape=[_sds(s, a.dtype) for s, a in zip(shapes, srcs)],
                 in_specs=[any_spec] * n, out_specs=[any_spec] * n, scratch=[pltpu.SemaphoreType.DMA((n,))])(*srcs)


_HBM_SPEC = pl.BlockSpec(memory_space=pltpu.HBM)
_SEM_SPEC = pl.BlockSpec(memory_space=pltpu.SEMAPHORE)
_DATAFLOW = pltpu.SideEffectType.DATAFLOW_SIDE_EFFECTING


def _remote(kind, src, land, send_sems, recv_sems, a, slot, frm, to_id, at):
    return pltpu.make_async_remote_copy(src_ref=_src_view(kind, src, frm), dst_ref=land.at[at],
                                        send_sem=send_sems.at[a * (N_DEV - 1) + slot],
                                        recv_sem=recv_sems.at[a * (N_DEV - 1) + slot],
                                        device_id=to_id, device_id_type=MESH_ID)


def exchange_begin(name, kind, srcs, lands, after):
    n = len(srcs)

    def body(*refs):
        ins, lnd = refs[:n], refs[n:2 * n]
        send_sems, recv_sems = refs[2 * n + 1], refs[2 * n + 2]
        token = refs[-1]
        x, y, c = lax.axis_index("x"), lax.axis_index("y"), lax.axis_index("c")
        me = 4 * x + 2 * y + c
        for slot, peer_id, peer in _peers(x, y, c):
            for a in range(n):
                _remote(kind, ins[a], lnd[a], send_sems, recv_sems, a, slot, peer, peer_id, me).start()
        token[...] = jnp.zeros_like(token)

    hbm = lambda t: pltpu.HBM(t.shape, t.dtype)
    outs = _pallas(
        body, name=name,
        out_shape=(pltpu.SemaphoreType.DMA((n * (N_DEV - 1),)), pltpu.SemaphoreType.DMA((n * (N_DEV - 1),)),
                   *[hbm(t) for t in srcs], *[hbm(t) for t in lands], _sds((8, 128), F32)),
        in_specs=[_HBM_SPEC] * (2 * n) + [pl.BlockSpec(memory_space=pl.ANY)],
        out_specs=(_SEM_SPEC, _SEM_SPEC, *[_HBM_SPEC] * (2 * n), pl.BlockSpec(memory_space=pltpu.VMEM)),
        input_output_aliases={i: 2 + i for i in range(2 * n)},
        compiler_params=pltpu.CompilerParams(has_side_effects=_DATAFLOW),
    )(*[pltpu.with_memory_space_constraint(t, pltpu.HBM) for t in list(srcs) + list(lands)], after)
    return (kind, outs[0], outs[1], list(outs[2:2 + n]), list(outs[2 + n:2 + 2 * n])), outs[-1]


def exchange_end(name, handle, after):
    kind, send_sems, recv_sems, srcs, lands = handle
    n = len(srcs)

    def body(*refs):
        ins, lnd = refs[:n], refs[n:2 * n]
        s_sems, r_sems = refs[2 * n], refs[2 * n + 1]
        x, y, c = lax.axis_index("x"), lax.axis_index("y"), lax.axis_index("c")
        me = 4 * x + 2 * y + c
        for slot, peer_id, peer in _peers(x, y, c):
            for a in range(n):
                _remote(kind, ins[a], lnd[a], s_sems, r_sems, a, slot, peer, peer_id, me).wait_send()
        for slot, peer_id, peer in _peers(x, y, c):
            for a in range(n):
                _remote(kind, ins[a], lnd[a], s_sems, r_sems, a, slot, me, peer_id, peer).wait_recv()

    hbm = lambda t: pltpu.HBM(t.shape, t.dtype)
    outs = _pallas(
        body, name=name, out_shape=tuple(hbm(t) for t in srcs + lands),
        in_specs=[_HBM_SPEC] * (2 * n) + [_SEM_SPEC, _SEM_SPEC, pl.BlockSpec(memory_space=pl.ANY)],
        out_specs=tuple([_HBM_SPEC] * (2 * n)), input_output_aliases={i: i for i in range(2 * n)},
        compiler_params=pltpu.CompilerParams(has_side_effects=_DATAFLOW),
    )(*srcs, *lands, send_sems, recv_sems, after)
    return list(outs[n:])


def norm_mm(name, x, gain, w, layer, bias=None):
    T, D = x.shape
    nsh, _, _, n = w.shape
    tm = _tile(T, 1024)

    def body(*refs):
        if bias is None:
            x_ref, g_ref, w_ref, u_ref, h_ref = refs
        else:
            x_ref, g_ref, w_ref, b_ref, u_ref, h_ref = refs

        @pl.when(pl.program_id(1) == 0)
        def _():
            xf = x_ref[...]
            r = lax.rsqrt(jnp.mean(xf * xf, axis=-1, keepdims=True) + EPS)
            h_ref[...] = (xf * r * g_ref[...]).astype(h_ref.dtype)

        acc = _dot(h_ref[...], w_ref[...])
        if bias is not None:
            acc = acc + b_ref[...]
        u_ref[...] = acc.astype(u_ref.dtype)

    in_specs = [pl.BlockSpec((tm, D), lambda i, j: (i, 0)),
                pl.BlockSpec((1, D), lambda i, j: (0, 0)),
                pl.BlockSpec((None, None, D, n), lambda i, j: (j, layer, 0, 0))]
    args = [x, gain.reshape(1, D), w]
    if bias is not None:
        in_specs.append(pl.BlockSpec((None, None, 1, n), lambda i, j: (j, layer, 0, 0)))
        args.append(bias)
    return _call(body, name=name, grid=(T // tm, nsh), in_specs=in_specs,
                 out_specs=[pl.BlockSpec((None, tm, n), lambda i, j: (j, i, 0)),
                            pl.BlockSpec((tm, D), lambda i, j: (i, 0))],
                 out_shape=[_sds((nsh, T, n), MM), _sds((T, D), MM)])(*args)


def mm_resid_norm(name, a, w, layer, bias, x, gain):
    nk, T, kk = a.shape
    D = x.shape[1]
    tm = _tile(T, 512)

    def body(a_ref, w_ref, b_ref, x_ref, g_ref, y_ref, xn_ref):
        y = _dot(a_ref[0], w_ref[0])
        for q in range(1, nk):
            y = y + _dot(a_ref[q], w_ref[q])
        y = y + b_ref[...]
        y_ref[...] = y
        r = lax.rsqrt(jnp.mean(y * y, axis=-1, keepdims=True) + EPS)
        xn_ref[...] = x_ref[...] + y * r * g_ref[...]

    return _call(body, name=name, grid=(T // tm,),
                 in_specs=[pl.BlockSpec((nk, tm, kk), lambda i: (0, i, 0)),
                           pl.BlockSpec((None, nk, kk, D), lambda i: (layer, 0, 0, 0)),
                           pl.BlockSpec((1, D), lambda i: (0, 0)),
                           pl.BlockSpec((tm, D), lambda i: (i, 0)),
                           pl.BlockSpec((1, D), lambda i: (0, 0))],
                 out_specs=[pl.BlockSpec((tm, D), lambda i: (i, 0))] * 2,
                 out_shape=[_sds((T, D), F32)] * 2)(a, w, bias.reshape(1, D), x, gain.reshape(1, D))


def _halo_maps(tm, hb, T):
    per = tm // hb
    last = T // hb - 1
    return (lambda i: jnp.maximum(i * per - 1, 0)), (lambda i: jnp.minimum((i + 1) * per, last))


def cm_glu_conv(name, u, dw, dwb, layer, seq):
    _, T, n = u.shape
    ct = dw.shape[-1]
    per = n // ct
    nct = 4 * per
    tm = _tile(seq, 512)
    tps = seq // tm
    hb = CONV_HALO
    prev, _ = _halo_maps(tm, hb, T)
    u4 = u.reshape(2, 4, T, n)

    def body(u_ref, uh_ref, w_ref, b_ref, o_ref, pad_ref):
        first = (pl.program_id(0) % tps) == 0
        um = u_ref[...].astype(F32)
        uh = uh_ref[...].astype(F32)
        pad_ref[pl.ds(hb, tm), :] = um[0] * _sigmoid(um[1])
        pad_ref[pl.ds(0, hb), :] = jnp.where(first, 0.0, uh[0] * _sigmoid(uh[1]))
        acc = jnp.zeros((tm, ct), F32) + b_ref[...]
        for k in range(CONV_W):
            acc = acc + w_ref[pl.ds(k, 1), :] * pad_ref[pl.ds(hb - (CONV_W - 1) + k, tm), :]
        o_ref[...] = acc

    return _call(body, name=name, grid=(T // tm, nct),
                 in_specs=[pl.BlockSpec((2, None, tm, ct), lambda i, c: (0, c // per, i, c % per)),
                           pl.BlockSpec((2, None, hb, ct), lambda i, c: (0, c // per, prev(i), c % per)),
                           pl.BlockSpec((None, None, CONV_W, ct), lambda i, c: (c, layer, 0, 0)),
                           pl.BlockSpec((None, None, 1, ct), lambda i, c: (c, layer, 0, 0))],
                 out_specs=pl.BlockSpec((tm, ct), lambda i, c: (i, c)),
                 out_shape=_sds((T, nct * ct), F32),
                 scratch=[pltpu.VMEM((tm + hb, ct), F32)])(u4, u4, dw, dwb)


def ln_silu(name, c, g, b):
    T, D = c.shape
    tm = _tile(T, 512)

    def body(c_ref, g_ref, b_ref, s_ref):
        cf = c_ref[...]
        mu = jnp.mean(cf, axis=-1, keepdims=True)
        xc = cf - mu
        r = lax.rsqrt(jnp.mean(xc * xc, axis=-1, keepdims=True) + EPS)
        t = xc * r * g_ref[...] + b_ref[...]
        s_ref[...] = (t * _sigmoid(t)).astype(s_ref.dtype)

    return _call(body, name=name, grid=(T // tm,),
                 in_specs=[pl.BlockSpec((tm, D), lambda i: (i, 0)), pl.BlockSpec((1, D), lambda i: (0, 0)),
                           pl.BlockSpec((1, D), lambda i: (0, 0))],
                 out_specs=pl.BlockSpec((None, tm, D), lambda i: (0, i, 0)),
                 out_shape=_sds((1, T, D), MM))(c, g.reshape(1, D), b.reshape(1, D))


def ffn_conv_gate(name, u, dw, dwb, layer, seq):
    _, T, n = u.shape
    tm = _tile(seq, 512)
    tps = seq // tm
    hb = FFN_HALO
    prev, _ = _halo_maps(tm, hb, T)
    u4 = u.reshape(2, 4, T, n)
    dw4 = dw.reshape(2, 4, dw.shape[1], FFN_CONV_W, n)
    dwb4 = dwb.reshape(dwb.shape[0], 2, 4, 1, n)

    def body(u_ref, uh_ref, w_ref, b_ref, z_ref, pad_ref):
        first = (pl.program_id(0) % tps) == 0
        conv = []
        for half in range(2):
            pad_ref[pl.ds(hb, tm), :] = u_ref[half].astype(F32)
            pad_ref[pl.ds(0, hb), :] = jnp.where(first, 0.0, uh_ref[half].astype(F32))
            acc = jnp.zeros((tm, n), F32) + b_ref[half]
            for k in range(FFN_CONV_W):
                acc = acc + w_ref[half, pl.ds(k, 1), :] * pad_ref[pl.ds(hb - (FFN_CONV_W - 1) + k, tm), :]
            conv.append(acc)
        a, g = conv
        z_ref[...] = (g * _sigmoid(g) * a).astype(z_ref.dtype)

    return _call(body, name=name, grid=(T // tm, 4),
                 in_specs=[pl.BlockSpec((2, None, tm, n), lambda i, j: (0, j, i, 0)),
                           pl.BlockSpec((2, None, hb, n), lambda i, j: (0, j, prev(i), 0)),
                           pl.BlockSpec((2, None, None, FFN_CONV_W, n), lambda i, j: (0, j, layer, 0, 0)),
                           pl.BlockSpec((None, 2, None, 1, n), lambda i, j: (layer, 0, j, 0, 0))],
                 out_specs=pl.BlockSpec((None, tm, n), lambda i, j: (j, i, 0)),
                 out_shape=_sds((4, T, n), MM),
                 scratch=[pltpu.VMEM((tm + hb, n), F32)])(u4, u4, dw4, dwb4)


def _head_specs(seq, dh, q_heads, kv_heads):
    def spec(per, base):
        def imap(b, h, g):
            f = base + g * N_HEADS + h
            return (f // per, b, f % per)
        return pl.BlockSpec((None, seq, dh), imap)
    return spec(q_heads, 0), spec(kv_heads, 0), spec(kv_heads, N_GROUPS * N_HEADS)


def _rows(start, d):
    return pl.ds(start, ATT_BLOCK, stride=d) if d > 1 else pl.ds(start, ATT_BLOCK)


def _band_masks():
    qi = lax.broadcasted_iota(jnp.int32, (ATT_BLOCK, ATT_BLOCK), 0)
    kj = lax.broadcasted_iota(jnp.int32, (ATT_BLOCK, ATT_BLOCK), 1)
    return kj <= qi, kj >= qi


def attn_fwd(name, q, kv, nb, seq):
    _, T, qn = q.shape
    dh = qn * N_DEV // (N_GROUPS * N_HEADS)
    scale = 1.0 / (dh ** 0.5)
    qs, ks, vs = _head_specs(seq, dh, qn // dh, kv.shape[2] // dh)

    def body(q_ref, k_ref, v_ref, m_ref, l_ref, qf, kf, vf, *branch):
        og, lg = branch[:N_GROUPS], branch[N_GROUPS:]
        cur_ok, prev_ok = _band_masks()
        qf[...] = q_ref[...].astype(F32)
        kf[...] = k_ref[...].astype(F32)
        vf[...] = v_ref[...].astype(F32)
        for g in range(N_GROUPS):
            d = DILATIONS[g]
            nblk = seq // d // ATT_BLOCK

            def block(idx, carry, d=d, nblk=nblk, g=g):
                r, n = idx // nblk, idx % nblk
                rc = _rows(r + d * ATT_BLOCK * n, d)
                qb = qf[rc, :].astype(MM)
                s_c = jnp.where(cur_ok, _dot_nt(qb, kf[rc, :].astype(MM)) * scale, NEG)
                m = jnp.max(s_c, axis=-1, keepdims=True)
                if nblk > 1:
                    rp = _rows(r + d * ATT_BLOCK * jnp.maximum(n - 1, 0), d)
                    s_p = jnp.where(prev_ok & (n > 0), _dot_nt(qb, kf[rp, :].astype(MM)) * scale, NEG)
                    m = jnp.maximum(m, jnp.max(s_p, axis=-1, keepdims=True))
                p_c = jnp.exp(s_c - m)
                den = jnp.sum(p_c, axis=-1, keepdims=True)
                o = _dot(p_c.astype(MM), vf[rc, :].astype(MM))
                if nblk > 1:
                    p_p = jnp.exp(s_p - m)
                    den = den + jnp.sum(p_p, axis=-1, keepdims=True)
                    o = o + _dot(p_p.astype(MM), vf[rp, :].astype(MM))
                og[g][rc, :] = o / den
                lg[g][rc, :] = jnp.broadcast_to(m + jnp.log(den), (ATT_BLOCK, dh))
                return carry

            @pl.when(pl.program_id(2) == g)
            def _(block=block, d=d, nblk=nblk):
                lax.fori_loop(0, d * nblk, block, 0)

        @pl.when(pl.program_id(2) == N_GROUPS - 1)
        def _():
            mx = jnp.maximum(jnp.maximum(lg[0][...], lg[1][...]), lg[2][...])
            e = [jnp.exp(lg[g][...] - mx) for g in range(N_GROUPS)]
            tot = e[0] + e[1] + e[2]
            m_ref[...] = ((e[0] * og[0][...] + e[1] * og[1][...] + e[2] * og[2][...]) / tot).astype(m_ref.dtype)
            l_ref[...] = mx + jnp.log(tot)

    return _call(body, name=name, grid=(nb, N_HEADS, N_GROUPS), in_specs=[qs, ks, vs],
                 out_specs=[pl.BlockSpec((None, seq, dh), lambda b, h, g: (0, b, h)),
                            pl.BlockSpec((seq, dh), lambda b, h, g: (b, h))],
                 out_shape=[_sds((1, T, N_HEADS * dh), MM), _sds((T, N_HEADS * dh), F32)],
                 scratch=[pltpu.VMEM((seq, dh), F32)] * (3 + 2 * N_GROUPS))(q, kv, kv)


def loss_fwd_bwd(name, x, target):
    T, D = x.shape
    tm = _tile(T, 512)

    def body(x_ref, t_ref, dx_ref, l_ref):
        @pl.when(pl.program_id(0) == 0)
        def _():
            l_ref[...] = jnp.zeros_like(l_ref)
        err = x_ref[...] - t_ref[...]
        dx_ref[...] = err * (1.0 / D)
        l_ref[...] += 0.5 * jnp.sum(jnp.mean(err * err, axis=-1, keepdims=True), axis=0, keepdims=True)

    dx, l = _call(body, name=name, grid=(T // tm,),
                  in_specs=[pl.BlockSpec((tm, D), lambda i: (i, 0))] * 2,
                  out_specs=[pl.BlockSpec((tm, D), lambda i: (i, 0)), pl.BlockSpec((1, 1), lambda i: (0, 0))],
                  out_shape=[_sds((T, D), F32), _sds((1, 1), F32)])(x, target)
    return dx, l


def resid_norm_bwd(name, dx, y, gain):
    T, D = y.shape
    tm = _tile(T, 512)

    def body(dx_ref, y_ref, g_ref, dy_ref, dg_ref, db_ref):
        @pl.when(pl.program_id(0) == 0)
        def _():
            dg_ref[...] = jnp.zeros_like(dg_ref)
            db_ref[...] = jnp.zeros_like(db_ref)
        y = y_ref[...]
        d = dx_ref[...]
        r = lax.rsqrt(jnp.mean(y * y, axis=-1, keepdims=True) + EPS)
        yh = y * r
        dyh = d * g_ref[...]
        dy = r * (dyh - yh * jnp.mean(dyh * yh, axis=-1, keepdims=True))
        dy_ref[...] = dy.astype(dy_ref.dtype)
        dg_ref[...] += jnp.sum(d * yh, axis=0, keepdims=True)
        db_ref[...] += jnp.sum(dy, axis=0, keepdims=True)

    return _call(body, name=name, grid=(T // tm,),
                 in_specs=[pl.BlockSpec((tm, D), lambda i: (i, 0))] * 2 + [pl.BlockSpec((1, D), lambda i: (0, 0))],
                 out_specs=[pl.BlockSpec((None, tm, D), lambda i: (0, i, 0))] + [pl.BlockSpec((1, D), lambda i: (0, 0))] * 2,
                 out_shape=[_sds((1, T, D), MM), _sds((1, D), F32), _sds((1, D), F32)])(dx, y, gain.reshape(1, D))


def mm_nt(name, dy, w, layer, out_dtype):
    _, T, D = dy.shape
    _, nk, kk, _ = w.shape
    tm = _tile(T, 1024)

    def body(dy_ref, w_ref, o_ref):
        o_ref[...] = _dot_nt(dy_ref[...], w_ref[...]).astype(o_ref.dtype)

    return _call(body, name=name, grid=(T // tm, nk),
                 in_specs=[pl.BlockSpec((None, tm, D), lambda i, q: (0, i, 0)),
                           pl.BlockSpec((None, None, kk, D), lambda i, q: (layer, q, 0, 0))],
                 out_specs=pl.BlockSpec((None, tm, kk), lambda i, q: (q, i, 0)),
                 out_shape=_sds((nk, T, kk), out_dtype))(dy, w)


def mm_nt_norm_bwd(name, du, w, layer, x_in, gain, dx_res):
    nsh, T, n = du.shape
    D = x_in.shape[1]
    tm = _tile(T, 512)

    def body(du_ref, w_ref, x_ref, g_ref, dr_ref, dx_ref, dg_ref, acc_ref):
        i, j = pl.program_id(0), pl.program_id(1)

        @pl.when((i == 0) & (j == 0))
        def _():
            dg_ref[...] = jnp.zeros_like(dg_ref)

        part = _dot_nt(du_ref[...], w_ref[...])

        @pl.when(j == 0)
        def _():
            acc_ref[...] = part

        @pl.when(j > 0)
        def _():
            acc_ref[...] += part

        @pl.when(j == nsh - 1)
        def _():
            x = x_ref[...]
            dh = acc_ref[...]
            r = lax.rsqrt(jnp.mean(x * x, axis=-1, keepdims=True) + EPS)
            xh = x * r
            dxh = dh * g_ref[...]
            dx_ref[...] = dr_ref[...] + r * (dxh - xh * jnp.mean(dxh * xh, axis=-1, keepdims=True))
            dg_ref[...] += jnp.sum(dh * xh, axis=0, keepdims=True)

    return _call(body, name=name, grid=(T // tm, nsh),
                 in_specs=[pl.BlockSpec((None, tm, n), lambda i, j: (j, i, 0)),
                           pl.BlockSpec((None, None, D, n), lambda i, j: (j, layer, 0, 0)),
                           pl.BlockSpec((tm, D), lambda i, j: (i, 0)),
                           pl.BlockSpec((1, D), lambda i, j: (0, 0)),
                           pl.BlockSpec((tm, D), lambda i, j: (i, 0))],
                 out_specs=[pl.BlockSpec((tm, D), lambda i, j: (i, 0)), pl.BlockSpec((1, D), lambda i, j: (0, 0))],
                 out_shape=[_sds((T, D), F32), _sds((1, D), F32)],
                 scratch=[pltpu.VMEM((tm, D), F32)])(du, w, x_in, gain.reshape(1, D), dx_res)


def mm_tn(name, a, b):
    na, T, ka = a.shape
    nb, _, kb = b.shape
    nj = max(na, nb)
    tk = _tile(T, 1024)
    nt = T // tk

    def body(a_ref, b_ref, o_ref, acc_ref):
        t = pl.program_id(1)
        part = _dot_tn(a_ref[...], b_ref[...])

        @pl.when(t == 0)
        def _():
            acc_ref[...] = part

        @pl.when(t > 0)
        def _():
            acc_ref[...] += part

        @pl.when(t == nt - 1)
        def _():
            o_ref[...] = acc_ref[...].astype(o_ref.dtype)

    return _call(body, name=name, grid=(nj, nt),
                 in_specs=[pl.BlockSpec((None, tk, ka), (lambda j, t: (j, t, 0)) if na > 1 else (lambda j, t: (0, t, 0))),
                           pl.BlockSpec((None, tk, kb), (lambda j, t: (j, t, 0)) if nb > 1 else (lambda j, t: (0, t, 0)))],
                 out_specs=pl.BlockSpec((None, ka, kb), lambda j, t: (j, 0, 0)),
                 out_shape=_sds((nj, ka, kb), MM),
                 scratch=[pltpu.VMEM((ka, kb), F32)])(a, b)


def ffn_conv_gate_bwd(name, u, dz, dw, dwb, layer, seq):
    _, T, n = u.shape
    tm = _tile(seq, 512)
    tps = seq // tm
    hb = FFN_HALO
    prev, nxt = _halo_maps(tm, hb, T)
    u4 = u.reshape(2, 4, T, n)
    dw4 = dw.reshape(2, 4, dw.shape[1], FFN_CONV_W, n)
    dwb4 = dwb.reshape(dwb.shape[0], 2, 4, 1, n)
    K = FFN_CONV_W
    te = tm + hb

    def body(u_ref, up_ref, un_ref, dz_ref, dzn_ref, w_ref, b_ref, du_ref, ddw_ref, ddb_ref, pad_ref, da_ref):
        i = pl.program_id(1)
        first = (i % tps) == 0
        last = (i % tps) == tps - 1

        @pl.when(i == 0)
        def _():
            ddw_ref[...] = jnp.zeros_like(ddw_ref)
            ddb_ref[...] = jnp.zeros_like(ddb_ref)

        for half in range(2):
            pad_ref[half, pl.ds(0, hb), :] = jnp.where(first, 0.0, up_ref[half].astype(F32))
            pad_ref[half, pl.ds(hb, tm), :] = u_ref[half].astype(F32)
            pad_ref[half, pl.ds(hb + tm, hb), :] = un_ref[half].astype(F32)
        conv = []
        for half in range(2):
            acc = jnp.zeros((te, n), F32) + b_ref[half]
            for k in range(K):
                acc = acc + w_ref[half, pl.ds(k, 1), :] * pad_ref[half, pl.ds(hb - (K - 1) + k, te), :]
            conv.append(acc)
        a, g = conv
        dzm = dz_ref[...].astype(F32)
        dzn = jnp.where(last, 0.0, dzn_ref[...].astype(F32))
        sg = _sigmoid(g)
        silu = g * sg
        dsilu = sg * (1.0 + g * (1.0 - sg))
        da_ref[0, pl.ds(0, tm), :] = dzm * silu[:tm]
        da_ref[0, pl.ds(tm, hb), :] = dzn * silu[tm:]
        da_ref[1, pl.ds(0, tm), :] = dzm * a[:tm] * dsilu[:tm]
        da_ref[1, pl.ds(tm, hb), :] = dzn * a[tm:] * dsilu[tm:]
        for half in range(2):
            dmain = da_ref[half, pl.ds(0, tm), :]
            acc = jnp.zeros((tm, n), F32)
            for k in range(K):
                acc = acc + w_ref[half, pl.ds(k, 1), :] * da_ref[half, pl.ds(K - 1 - k, tm), :]
                ddw_ref[half, pl.ds(k, 1), :] += jnp.sum(
                    dmain * pad_ref[half, pl.ds(hb - (K - 1) + k, tm), :], axis=0, keepdims=True)
            du_ref[half] = acc.astype(du_ref.dtype)
            ddb_ref[half] += jnp.sum(dmain, axis=0, keepdims=True)

    du, ddw, ddb = _call(
        body, name=name, grid=(4, T // tm),
        in_specs=[pl.BlockSpec((2, None, tm, n), lambda j, i: (0, j, i, 0)),
                  pl.BlockSpec((2, None, hb, n), lambda j, i: (0, j, prev(i), 0)),
                  pl.BlockSpec((2, None, hb, n), lambda j, i: (0, j, nxt(i), 0)),
                  pl.BlockSpec((None, tm, n), lambda j, i: (j, i, 0)),
                  pl.BlockSpec((None, hb, n), lambda j, i: (j, nxt(i), 0)),
                  pl.BlockSpec((2, None, None, K, n), lambda j, i: (0, j, layer, 0, 0)),
                  pl.BlockSpec((None, 2, None, 1, n), lambda j, i: (layer, 0, j, 0, 0))],
        out_specs=[pl.BlockSpec((2, None, tm, n), lambda j, i: (0, j, i, 0)),
                   pl.BlockSpec((2, None, K, n), lambda j, i: (0, j, 0, 0)),
                   pl.BlockSpec((2, None, 1, n), lambda j, i: (0, j, 0, 0))],
        out_shape=[_sds((2, 4, T, n), MM), _sds((2, 4, K, n), F32), _sds((2, 4, 1, n), F32)],
        scratch=[pltpu.VMEM((2, tm + 2 * hb, n), F32), pltpu.VMEM((2, te, n), F32)])(u4, u4, u4, dz, dz, dw4, dwb4)
    return du.reshape(8, T, n), ddw, ddb


def ln_silu_bwd(name, ds, c, g, b):
    T, D = c.shape
    tm = _tile(T, 512)

    def body(ds_ref, c_ref, g_ref, b_ref, dc_ref, dg_ref, db_ref):
        @pl.when(pl.program_id(0) == 0)
        def _():
            dg_ref[...] = jnp.zeros_like(dg_ref)
            db_ref[...] = jnp.zeros_like(db_ref)
        cf = c_ref[...]
        mu = jnp.mean(cf, axis=-1, keepdims=True)
        xc = cf - mu
        r = lax.rsqrt(jnp.mean(xc * xc, axis=-1, keepdims=True) + EPS)
        xh = xc * r
        t = xh * g_ref[...] + b_ref[...]
        sg = _sigmoid(t)
        dt = ds_ref[...] * (sg * (1.0 + t * (1.0 - sg)))
        dg_ref[...] += jnp.sum(dt * xh, axis=0, keepdims=True)
        db_ref[...] += jnp.sum(dt, axis=0, keepdims=True)
        dxh = dt * g_ref[...]
        dc_ref[...] = r * (dxh - jnp.mean(dxh, axis=-1, keepdims=True)
                           - xh * jnp.mean(dxh * xh, axis=-1, keepdims=True))

    vec = pl.BlockSpec((1, D), lambda i: (0, 0))
    return _call(body, name=name, grid=(T // tm,),
                 in_specs=[pl.BlockSpec((None, tm, D), lambda i: (0, i, 0)), pl.BlockSpec((tm, D), lambda i: (i, 0)),
                           vec, vec],
                 out_specs=[pl.BlockSpec((tm, D), lambda i: (i, 0)), vec, vec],
                 out_shape=[_sds((T, D), F32), _sds((1, D), F32), _sds((1, D), F32)])(
                     ds, c, g.reshape(1, D), b.reshape(1, D))


def cm_glu_conv_bwd(name, u, dc, dw, layer, seq):
    _, T, n = u.shape
    ct = dw.shape[-1]
    per = n // ct
    nct = 4 * per
    tm = _tile(seq, 512)
    tps = seq // tm
    hb = CONV_HALO
    prev, nxt = _halo_maps(tm, hb, T)
    u4 = u.reshape(2, 4, T, n)
    K = CONV_W

    def body(u_ref, up_ref, dc_ref, dcn_ref, w_ref, du_ref, ddw_ref, ddb_ref, dbi_ref, padp_ref, padd_ref):
        i = pl.program_id(1)
        first = (i % tps) == 0
        last = (i % tps) == tps - 1

        @pl.when(i == 0)
        def _():
            ddw_ref[...] = jnp.zeros_like(ddw_ref)
            ddb_ref[...] = jnp.zeros_like(ddb_ref)
            dbi_ref[...] = jnp.zeros_like(dbi_ref)

        um = u_ref[...].astype(F32)
        uh = up_ref[...].astype(F32)
        sg = _sigmoid(um[1])
        padp_ref[pl.ds(hb, tm), :] = um[0] * sg
        padp_ref[pl.ds(0, hb), :] = jnp.where(first, 0.0, uh[0] * _sigmoid(uh[1]))
        dcm = dc_ref[...]
        padd_ref[pl.ds(0, tm), :] = dcm
        padd_ref[pl.ds(tm, hb), :] = jnp.where(last, 0.0, dcn_ref[...])
        dp = jnp.zeros((tm, ct), F32)
        for k in range(K):
            dp = dp + w_ref[pl.ds(k, 1), :] * padd_ref[pl.ds(K - 1 - k, tm), :]
            ddw_ref[pl.ds(k, 1), :] += jnp.sum(dcm * padp_ref[pl.ds(hb - (K - 1) + k, tm), :], axis=0, keepdims=True)
        ddb_ref[...] += jnp.sum(dcm, axis=0, keepdims=True)
        dv = dp * sg
        dg = dp * um[0] * sg * (1.0 - sg)
        du_ref[0] = dv.astype(du_ref.dtype)
        du_ref[1] = dg.astype(du_ref.dtype)
        dbi_ref[0] += jnp.sum(dv, axis=0, keepdims=True)
        dbi_ref[1] += jnp.sum(dg, axis=0, keepdims=True)

    du, ddw, ddb, dbi = _call(
        body, name=name, grid=(nct, T // tm),
        in_specs=[pl.BlockSpec((2, None, tm, ct), lambda c, i: (0, c // per, i, c % per)),
                  pl.BlockSpec((2, None, hb, ct), lambda c, i: (0, c // per, prev(i), c % per)),
                  pl.BlockSpec((tm, ct), lambda c, i: (i, c)),
                  pl.BlockSpec((hb, ct), lambda c, i: (nxt(i), c)),
                  pl.BlockSpec((None, None, K, ct), lambda c, i: (c, layer, 0, 0))],
        out_specs=[pl.BlockSpec((2, None, tm, ct), lambda c, i: (0, c // per, i, c % per)),
                   pl.BlockSpec((None, K, ct), lambda c, i: (c, 0, 0)),
                   pl.BlockSpec((None, 1, ct), lambda c, i: (c, 0, 0)),
                   pl.BlockSpec((2, None, 1, ct), lambda c, i: (0, c // per, 0, c % per))],
        out_shape=[_sds((2, 4, T, n), MM), _sds((nct, K, ct), F32), _sds((nct, 1, ct), F32), _sds((2, 4, 1, n), F32)],
        scratch=[pltpu.VMEM((tm + hb, ct), F32), pltpu.VMEM((tm + hb, ct), F32)])(u4, u4, dc, dc, dw)
    return du.reshape(8, T, n), ddw, ddb, dbi


def attn_bwd(name, q, kv, dm, merged, lse, dkv_prev, nb, seq):
    _, T, qn = q.shape
    kn = kv.shape[2]
    dh = qn * N_DEV // (N_GROUPS * N_HEADS)
    scale = 1.0 / (dh ** 0.5)
    qs, ks, vs = _head_specs(seq, dh, qn // dh, kn // dh)
    has_prev = dkv_prev is not None
    n_in = 6 + (2 if has_prev else 0)

    def body(*refs):
        q_ref, k_ref, v_ref, dm_ref, mg_ref, l_ref = refs[:6]
        pk_ref, pv_ref = refs[6:8] if has_prev else (None, None)
        dq_ref, dk_ref, dv_ref = refs[n_in:n_in + 3]
        qf, kf, vf, dqf, dkf, dvf, dlt = refs[n_in + 3:]
        cur_ok, prev_ok = _band_masks()
        dlt[...] = jnp.broadcast_to(jnp.sum(dm_ref[...] * mg_ref[...].astype(F32), axis=-1, keepdims=True), (seq, dh))
        qf[...] = q_ref[...].astype(F32)
        kf[...] = k_ref[...].astype(F32)
        vf[...] = v_ref[...].astype(F32)
        dkf[...] = jnp.zeros_like(dkf)
        dvf[...] = jnp.zeros_like(dvf)
        for g in range(N_GROUPS):
            d = DILATIONS[g]
            nblk = seq // d // ATT_BLOCK

            def block(idx, carry, d=d, nblk=nblk):
                r, n = idx // nblk, idx % nblk
                rc = _rows(r + d * ATT_BLOCK * n, d)
                qb = qf[rc, :].astype(MM)
                dmb = dm_ref[rc, :].astype(MM)
                lse_b = l_ref[rc, :][:, :1]
                dlt_b = dlt[rc, :][:, :1]

                def side(rk, ok):
                    kb = kf[rk, :].astype(MM)
                    vb = vf[rk, :].astype(MM)
                    s = jnp.where(ok, _dot_nt(qb, kb) * scale, NEG)
                    p = jnp.exp(s - lse_b)
                    dvf[rk, :] += _dot_tn(p.astype(MM), dmb)
                    dsc = (p * (_dot_nt(dmb, vb) - dlt_b) * scale).astype(MM)
                    dkf[rk, :] += _dot_tn(dsc, qb)
                    return _dot(dsc, kb)

                dq = side(rc, cur_ok)
                if nblk > 1:
                    rp = _rows(r + d * ATT_BLOCK * jnp.maximum(n - 1, 0), d)
                    dq = dq + side(rp, prev_ok & (n > 0))
                dqf[rc, :] = dq
                return carry

            @pl.when(pl.program_id(2) == g)
            def _(block=block, d=d, nblk=nblk):
                lax.fori_loop(0, d * nblk, block, 0)

        dq_ref[...] = dqf[...].astype(dq_ref.dtype)
        if has_prev:
            dk_ref[...] = (dkf[...] + pk_ref[...].astype(F32)).astype(dk_ref.dtype)
            dv_ref[...] = (dvf[...] + pv_ref[...].astype(F32)).astype(dv_ref.dtype)
        else:
            dk_ref[...] = dkf[...].astype(dk_ref.dtype)
            dv_ref[...] = dvf[...].astype(dv_ref.dtype)

    full = pl.BlockSpec((None, seq, dh), lambda b, h, g: (0, b, h))
    in_specs = [qs, ks, vs, full, full, pl.BlockSpec((seq, dh), lambda b, h, g: (b, h))]
    args = [q, kv, kv, dm, merged, lse]
    if has_prev:
        in_specs += [ks, ks]
        args += [dkv_prev[0], dkv_prev[1]]
    dq, dk, dv = _call(body, name=name, grid=(nb, N_HEADS, N_GROUPS), in_specs=in_specs, out_specs=[qs, ks, ks],
                       out_shape=[_sds(q.shape, MM), _sds((N_DEV // 2, T, kn), MM), _sds((N_DEV // 2, T, kn), MM)],
                       scratch=[pltpu.VMEM((seq, dh), F32)] * 7)(*args)
    return dq, (dk, dv)


def _adamw_math(w, g, m, v):
    m = ADAM_B1 * m + (1.0 - ADAM_B1) * g
    v = ADAM_B2 * v + (1.0 - ADAM_B2) * (g * g)
    m_hat = m / (1.0 - ADAM_B1 ** ADAM_STEP)
    v_hat = v / (1.0 - ADAM_B2 ** ADAM_STEP)
    delta = -ADAM_LR * (m_hat / (jnp.sqrt(v_hat) + ADAM_EPS) + ADAM_WD * w)
    return delta, m, v


def adamw_sum(name, w, m, v, parts):
    L, R, C = w.shape
    tr = _tile(R, 256)

    def body(*refs):
        w_ref, m_ref, v_ref = refs[:3]
        p_refs = refs[3:3 + L]
        g_ref, d_ref, nm_ref, nv_ref = refs[3 + L:]
        for l in range(L):
            @pl.when(pl.program_id(0) == l)
            def _(p_ref=p_refs[l]):
                g = p_ref[0].astype(F32)
                for k in range(1, N_DEV):
                    g = g + p_ref[k].astype(F32)
                g_ref[...] = g
                d_ref[...], nm_ref[...], nv_ref[...] = _adamw_math(w_ref[...], g, m_ref[...], v_ref[...])

    blk = pl.BlockSpec((None, tr, C), lambda l, i: (l, i, 0))
    part = lambda k: pl.BlockSpec((N_DEV, tr, C), lambda l, i: (0, jnp.where(l == k, i, 0), 0))
    return _call(body, name=name, grid=(L, R // tr), in_specs=[blk, blk, blk] + [part(k) for k in range(L)],
                 out_specs=[blk] * 4, out_shape=[_sds((L, R, C), F32)] * 4)(w, m, v, *parts)


def sum_partials(name, parts):
    _, R, C = parts.shape
    tr = _tile(R, 512)

    def body(p_ref, o_ref):
        g = p_ref[0]
        for k in range(1, N_DEV):
            g = g + p_ref[k]
        o_ref[...] = g

    return _call(body, name=name, grid=(R // tr,),
                 in_specs=[pl.BlockSpec((N_DEV, tr, C), lambda i: (0, i, 0))],
                 out_specs=pl.BlockSpec((tr, C), lambda i: (i, 0)), out_shape=_sds((R, C), F32))(parts)


def adamw_small(name, w, g, m, v):
    R, C = w.shape
    tr = _tile(R, 512)

    def body(w_ref, g_ref, m_ref, v_ref, d_ref, nm_ref, nv_ref):
        d_ref[...], nm_ref[...], nv_ref[...] = _adamw_math(w_ref[...], g_ref[...], m_ref[...], v_ref[...])

    blk = pl.BlockSpec((tr, C), lambda i: (i, 0))
    return _call(body, name=name, grid=(R // tr,), in_specs=[blk] * 4, out_specs=[blk] * 3,
                 out_shape=[_sds((R, C), F32)] * 3)(w, g, m, v)


def _pack(arrays):
    pieces = []
    for a in arrays:
        f = a.reshape(-1).astype(F32)
        pieces.append(jnp.pad(f, (0, (-f.shape[0]) % PACK)))
    return jnp.concatenate(pieces).reshape(-1, 128)


def _unpack(flat, shapes):
    out, off = [], 0
    f = flat.reshape(-1)
    for s in shapes:
        size = 1
        for d in s:
            size *= d
        out.append(f[off:off + size].reshape(s))
        off += size + (-size) % PACK
    return out


def kernel(x, mix_pre_g, mix_post_g, ffn_pre_g, ffn_post_g, cm_w_in, cm_b_in, cm_dw, cm_dw_b, cm_ln_g, cm_ln_b, cm_w_out, cm_b_out, kv_norm_g, w_kv, w_q, w_o, ffn_w_in, ffn_dw, ffn_dw_b, ffn_w_out, loss_target, m_mix_pre_g, m_mix_post_g, m_ffn_pre_g, m_ffn_post_g, m_cm_w_in, m_cm_b_in, m_cm_dw, m_cm_dw_b, m_cm_ln_g, m_cm_ln_b, m_cm_w_out, m_cm_b_out, m_kv_norm_g, m_w_kv, m_w_q, m_w_o, m_ffn_w_in, m_ffn_dw, m_ffn_dw_b, m_ffn_w_out, v_mix_pre_g, v_mix_post_g, v_ffn_pre_g, v_ffn_post_g, v_cm_w_in, v_cm_b_in, v_cm_dw, v_cm_dw_b, v_cm_ln_g, v_cm_ln_b, v_cm_w_out, v_cm_b_out, v_kv_norm_g, v_w_kv, v_w_q, v_w_o, v_ffn_w_in, v_ffn_dw, v_ffn_dw_b, v_ffn_w_out):
    nb, seq, D = x.shape
    T = nb * seq
    me = _my_index()
    n_b = DEPTH - N_A

    nf = ffn_w_in.shape[-1]

    def layer_sources(i):
        src = {"fin": ffn_w_in[i], "fout": ffn_w_out[i]}
        if i < N_A:
            src.update(cin=cm_w_in[i], cout=cm_w_out[i])
            if i == N_A - 1:
                src.update(kv=w_kv)
        else:
            src.update(q=w_q[i - N_A], o=w_o[i - N_A])
        return {k: t.astype(MM) for k, t in src.items()}

    def begin_gather(i, after):
        src = layer_sources(i)
        names, arrays = list(src), list(src.values())
        lands = place_own(f"gather_own_{i}", "gather", arrays)
        handle, token = exchange_begin(f"gather_begin_{i}", "gather", arrays, lands, after)
        return (names, handle), token

    def end_gather(i, pending, after):
        names, handle = pending
        W = dict(zip(names, exchange_end(f"gather_end_{i}", handle, after)))
        for k in W:
            if k in ("cout", "o"):
                W[k] = W[k].reshape(1, 1, D, D)
            elif k == "fout":
                W[k] = W[k].reshape(1, 4, nf, D)
            else:
                W[k] = W[k][:, None]
        return W

    small = [cm_b_in[:, None, :], cm_dw, cm_dw_b[:, None, :], cm_ln_g, cm_ln_b, cm_b_out, ffn_dw]
    Bcin, DWc, DWBc, LNg, LNb, Bcout, DWf = all_gather("gather_small", small, [False] * len(small))
    LNg = jnp.swapaxes(LNg, 0, 1).reshape(N_A, D)
    LNb = jnp.swapaxes(LNb, 0, 1).reshape(N_A, D)
    Bcout = jnp.swapaxes(Bcout, 0, 1).reshape(N_A, D)
    DWBf = ffn_dw_b.reshape(DEPTH, N_DEV, 1, nf)
    zero_bias = jnp.zeros((D,), F32)

    xs = x.reshape(T, D)
    sv = []
    kv = hkv = None
    pending, _ = begin_gather(0, xs)
    Ws = [end_gather(0, pending, xs)] + [None] * (DEPTH - 1)
    for i in range(DEPTH):
        L = {"x_in": xs}
        W = Ws[i]
        pre_gain = mix_pre_g[i]
        if i + 1 < DEPTH:
            pending, token = begin_gather(i + 1, xs)
            pre_gain = pre_gain + token[0, 0]
        if i < N_A:
            L["u"], L["h"] = norm_mm(f"cm_in_{i}", xs, pre_gain, W["cin"], 0, Bcin[:, i:i + 1])
            L["c"] = cm_glu_conv(f"cm_conv_{i}", L["u"], DWc, DWBc, i, seq)
            L["s"] = ln_silu(f"cm_ln_{i}", L["c"], LNg[i], LNb[i])
            L["y"], x1 = mm_resid_norm(f"cm_out_{i}", L["s"], W["cout"], 0, Bcout[i], xs, mix_post_g[i])
        else:
            L["q"], L["h"] = norm_mm(f"attn_q_{i}", xs, pre_gain, W["q"], 0)
            L["mg"], L["lse"] = attn_fwd(f"attn_{i}", L["q"], kv, nb, seq)
            L["y"], x1 = mm_resid_norm(f"attn_out_{i}", L["mg"], W["o"], 0, zero_bias, xs, mix_post_g[i])
        L["x1"] = x1
        L["uf"], L["hf"] = norm_mm(f"ffn_in_{i}", x1, ffn_pre_g[i], W["fin"], 0)
        L["z"] = ffn_conv_gate(f"ffn_conv_{i}", L["uf"], DWf, DWBf, i, seq)
        L["yf"], xs = mm_resid_norm(f"ffn_out_{i}", L["z"], W["fout"], 0, zero_bias, x1, ffn_post_g[i])
        if i == N_A - 1:
            kv, hkv = norm_mm("kv_proj", xs, kv_norm_g, W["kv"], 0)
        if i + 1 < DEPTH:
            Ws[i + 1] = end_gather(i + 1, pending, xs)
        sv.append(L)
    dx, loss_part = loss_fwd_bwd("loss", xs, loss_target.reshape(T, D))

    g_mix_pre, g_mix_post, g_ffn_pre, g_ffn_post = [None] * DEPTH, [None] * DEPTH, [None] * DEPTH, [None] * DEPTH
    g_ffn_dw, g_ffn_dwb = [None] * DEPTH, [None] * DEPTH
    g_cbin, g_cdw, g_cdwb, g_lng, g_lnb, g_cbout = ([None] * N_A for _ in range(6))
    g_kvn = dkv = None
    landed = [None] * DEPTH
    in_flight = token = None
    for i in reversed(range(DEPTH)):
        L, W = sv[i], Ws[i]
        post_gain = ffn_post_g[i] if token is None else ffn_post_g[i] + token[0, 0]
        part = {}
        dyf, g_ffn_post[i], _ = resid_norm_bwd(f"ffn_post_bwd_{i}", dx, L["yf"], post_gain)
        dz = mm_nt(f"ffn_out_bwd_{i}", dyf, W["fout"], 0, MM)
        part["fout"] = mm_tn(f"ffn_out_wg_{i}", L["z"], dyf).reshape(N_DEV, nf // 2, D)
        duf, ddw, ddwb = ffn_conv_gate_bwd(f"ffn_conv_bwd_{i}", L["uf"], dz, DWf, DWBf, i, seq)
        g_ffn_dw[i], g_ffn_dwb[i] = ddw.reshape(N_DEV, FFN_CONV_W, nf), ddwb.reshape(-1)
        part["fin"] = mm_tn(f"ffn_in_wg_{i}", L["hf"][None], duf)
        dx1, g_ffn_pre[i] = mm_nt_norm_bwd(f"ffn_in_bwd_{i}", duf, W["fin"], 0, L["x1"], ffn_pre_g[i], dx)
        dy, g_mix_post[i], dyb = resid_norm_bwd(f"mix_post_bwd_{i}", dx1, L["y"], mix_post_g[i])
        if i >= N_A:
            dm = mm_nt(f"attn_out_bwd_{i}", dy, W["o"], 0, F32)
            part["o"] = mm_tn(f"attn_out_wg_{i}", L["mg"], dy).reshape(N_DEV, D // N_DEV, D)
            dq, dkv = attn_bwd(f"attn_bwd_{i}", L["q"], kv, dm, L["mg"], L["lse"], dkv, nb, seq)
            part["q"] = mm_tn(f"attn_q_wg_{i}", L["h"][None], dq)
            dx, g_mix_pre[i] = mm_nt_norm_bwd(f"attn_q_bwd_{i}", dq, W["q"], 0, L["x_in"], mix_pre_g[i], dx1)
            if i == N_A:
                dkv_all = jnp.concatenate(dkv, axis=0)
                part["kv"] = mm_tn("kv_wg", hkv[None], dkv_all)
                dx, g_kvn = mm_nt_norm_bwd("kv_bwd", dkv_all, Ws[N_A - 1]["kv"], 0, L["x_in"], kv_norm_g, dx)
        else:
            g_cbout[i] = dyb
            ds = mm_nt(f"cm_out_bwd_{i}", dy, W["cout"], 0, F32)
            part["cout"] = mm_tn(f"cm_out_wg_{i}", L["s"], dy).reshape(N_DEV, D // N_DEV, D)
            dc, g_lng[i], g_lnb[i] = ln_silu_bwd(f"cm_ln_bwd_{i}", ds, L["c"], LNg[i], LNb[i])
            du, g_cdw[i], g_cdwb[i], dbi = cm_glu_conv_bwd(f"cm_conv_bwd_{i}", L["u"], dc, DWc, i, seq)
            g_cbin[i] = dbi.reshape(N_DEV, -1)
            part["cin"] = mm_tn(f"cm_in_wg_{i}", L["h"][None], du)
            dx, g_mix_pre[i] = mm_nt_norm_bwd(f"cm_in_bwd_{i}", du, W["cin"], 0, L["x_in"], mix_pre_g[i], dx1)
        if in_flight is not None:
            j, names, handle = in_flight
            landed[j] = dict(zip(names, exchange_end(f"scatter_end_{j}", handle, dx)))
        names, arrays = list(part), list(part.values())
        lands = place_own(f"scatter_own_{i}", "scatter", arrays)
        handle, token = exchange_begin(f"scatter_begin_{i}", "scatter", arrays, lands, dx)
        in_flight = (i, names, handle)
    j, names, handle = in_flight
    landed[j] = dict(zip(names, exchange_end(f"scatter_end_{j}", handle, dx)))
    grad_x = dx.reshape(nb, seq, D)

    def big_update(name, w, m, v, key, layers):
        as3 = lambda t: t.reshape((-1,) + t.shape[-2:])
        outs = adamw_sum(name, as3(w), as3(m), as3(v), [landed[i][key] for i in layers])
        return [t.reshape(w.shape) for t in outs]

    conf, attn = range(N_A), range(N_A, DEPTH)
    upd = {
        "cm_w_in": big_update("adam_cm_w_in", cm_w_in, m_cm_w_in, v_cm_w_in, "cin", conf),
        "w_kv": big_update("adam_w_kv", w_kv, m_w_kv, v_w_kv, "kv", [N_A]),
        "w_q": big_update("adam_w_q", w_q, m_w_q, v_w_q, "q", attn),
        "ffn_w_in": big_update("adam_ffn_w_in", ffn_w_in, m_ffn_w_in, v_ffn_w_in, "fin", range(DEPTH)),
        "cm_w_out": big_update("adam_cm_w_out", cm_w_out, m_cm_w_out, v_cm_w_out, "cout", conf),
        "w_o": big_update("adam_w_o", w_o, m_w_o, v_w_o, "o", attn),
        "ffn_w_out": big_update("adam_ffn_w_out", ffn_w_out, m_ffn_w_out, v_ffn_w_out, "fout", range(DEPTH)),
    }

    rep_names = ["mix_pre_g", "mix_post_g", "ffn_pre_g", "ffn_post_g", "kv_norm_g", "ffn_dw_b"]
    rep_parts = [jnp.concatenate(g_mix_pre), jnp.concatenate(g_mix_post), jnp.concatenate(g_ffn_pre),
                 jnp.concatenate(g_ffn_post), g_kvn.reshape(-1), jnp.stack(g_ffn_dwb)]
    rep_w = [mix_pre_g, mix_post_g, ffn_pre_g, ffn_post_g, kv_norm_g, ffn_dw_b]
    rep_m = [m_mix_pre_g, m_mix_post_g, m_ffn_pre_g, m_ffn_post_g, m_kv_norm_g, m_ffn_dw_b]
    rep_v = [v_mix_pre_g, v_mix_post_g, v_ffn_pre_g, v_ffn_post_g, v_kv_norm_g, v_ffn_dw_b]
    sh_names = ["ffn_dw", "cm_b_in", "cm_dw", "cm_dw_b", "cm_ln_g", "cm_ln_b", "cm_b_out"]
    own = lambda per_layer, shard: jnp.stack([p.reshape((N_DEV,) + shard) for p in per_layer], axis=1)
    sh_parts = [own(g_ffn_dw, ffn_dw.shape[1:]), own(g_cbin, cm_b_in.shape[1:]), own(g_cdw, cm_dw.shape[1:]),
                own(g_cdwb, cm_dw_b.shape[1:]), own(g_lng, cm_ln_g.shape[1:]), own(g_lnb, cm_ln_b.shape[1:]),
                own(g_cbout, cm_b_out.shape[1:])]
    sh_w = [ffn_dw, cm_b_in, cm_dw, cm_dw_b, cm_ln_g, cm_ln_b, cm_b_out]
    sh_m = [m_ffn_dw, m_cm_b_in, m_cm_dw, m_cm_dw_b, m_cm_ln_g, m_cm_ln_b, m_cm_b_out]
    sh_v = [v_ffn_dw, v_cm_b_in, v_cm_dw, v_cm_dw_b, v_cm_ln_g, v_cm_ln_b, v_cm_b_out]
    rep_pack = _pack([loss_part] + rep_parts)
    sh_packs = [_pack([p[k] for p in sh_parts]) for k in range(N_DEV)]
    n_rep, n_sh = rep_pack.shape[0], sh_packs[0].shape[0]
    (gathered,) = all_gather("gather_small_grads", [jnp.concatenate([rep_pack] + sh_packs)], [False])
    total = sum_partials("sum_small_grads", gathered)
    rep_sum = total[:n_rep]
    sh_sum = lax.dynamic_slice_in_dim(total, n_rep + me * n_sh, n_sh, axis=0)
    rep_shapes = [(1, 1)] + [w.shape for w in rep_w]
    sh_shapes = [w.shape for w in sh_w]
    g_small = jnp.concatenate([rep_sum, sh_sum])
    pad1 = jnp.zeros((1, 1), F32)
    d_s, m_s, v_s = adamw_small("adam_small", jnp.concatenate([_pack([pad1] + rep_w), _pack(sh_w)]), g_small,
                                jnp.concatenate([_pack([pad1] + rep_m), _pack(sh_m)]),
                                jnp.concatenate([_pack([pad1] + rep_v), _pack(sh_v)]))
    split = lambda t: (_unpack(t[:n_rep], rep_shapes), _unpack(t[n_rep:], sh_shapes))
    for (rep_t, sh_t), slot in zip([split(g_small), split(d_s), split(m_s), split(v_s)], range(4)):
        if slot == 0:
            loss = rep_t[0].reshape(())
        for name, t in zip(rep_names, rep_t[1:]):
            upd.setdefault(name, [None] * 4)[slot] = t
        for name, t in zip(sh_names, sh_t):
            upd.setdefault(name, [None] * 4)[slot] = t

    order = ["mix_pre_g", "mix_post_g", "ffn_pre_g", "ffn_post_g", "cm_w_in", "cm_b_in", "cm_dw", "cm_dw_b", "cm_ln_g",
             "cm_ln_b", "cm_w_out", "cm_b_out", "kv_norm_g", "w_kv", "w_q", "w_o", "ffn_w_in", "ffn_dw", "ffn_dw_b",
             "ffn_w_out"]
    return (loss, grad_x, *[upd[n][0] for n in order], *[upd[n][1] for n in order],
            *[upd[n][2] for n in order], *[upd[n][3] for n in order])
```

```python
import functools

import jax
import jax.numpy as jnp
from jax import lax
from jax.experimental import pallas as pl
from jax.experimental.pallas import tpu as pltpu

N_DEV = 8
N_A = 2
DEPTH = 4
N_HEADS = 8
N_GROUPS = 3
DILATIONS = (1, 4, 16)
ATT_BLOCK = 128
CONV_W = 31
FFN_CONV_W = 3
CONV_HALO = 32
FFN_HALO = 16
EPS = 1e-6
NEG = -1e30
ADAM_LR, ADAM_B1, ADAM_B2, ADAM_EPS, ADAM_WD, ADAM_STEP = 0.001, 0.9, 0.999, 1e-08, 0.01, 10
MM = jnp.bfloat16
F32 = jnp.float32
VMEM_LIMIT_BYTES = 56 * 1024 * 1024
PACK = 1024
MESH_ID = pl.DeviceIdType.MESH

_pallas = pl.pallas_call


def _call(body, *, name, out_shape, grid=(), in_specs=None, out_specs=None, scratch=()):
    return _pallas(body, name=name, out_shape=out_shape, grid=grid, in_specs=in_specs, out_specs=out_specs,
                   scratch_shapes=list(scratch),
                   compiler_params=pltpu.CompilerParams(vmem_limit_bytes=VMEM_LIMIT_BYTES))


def _tile(n, pref):
    if n <= pref:
        return n
    t = pref - pref % 8
    while n % t:
        t -= 8
    assert t > 0, (n, pref)
    return t


def _sds(shape, dtype):
    return jax.ShapeDtypeStruct(tuple(shape), dtype)


def _dot(a, b):
    return jnp.dot(a, b, preferred_element_type=F32)


def _dot_nt(a, b):
    return lax.dot_general(a, b, (((1,), (1,)), ((), ())), preferred_element_type=F32)


def _dot_tn(a, b):
    return lax.dot_general(a, b, (((0,), (0,)), ((), ())), preferred_element_type=F32)


def _sigmoid(x):
    return 1.0 / (1.0 + jnp.exp(-x))


def _my_index():
    return 4 * lax.axis_index("x") + 2 * lax.axis_index("y") + lax.axis_index("c")


def _exchange(name, arrays, out_shapes, pieces, src_of, dst_of):
    n = len(arrays)
    base = [sum(pieces[:a]) for a in range(n)]
    total = sum(pieces)

    def body(*refs):
        ins, outs = refs[:n], refs[n:2 * n]
        send_sems, recv_sems, local_sems = refs[2 * n:]
        x, y, c = lax.axis_index("x"), lax.axis_index("y"), lax.axis_index("c")
        me = 4 * x + 2 * y + c
        copies = []
        for a in range(n):
            for k, (s, d) in enumerate(zip(src_of(a, ins[a], me), dst_of(a, outs[a], me))):
                cp = pltpu.make_async_copy(s, d, local_sems.at[base[a] + k])
                cp.start()
                copies.append(cp)
        remote = []
        for m in range(1, N_DEV):
            px, py, pc = x ^ (m >> 2), y ^ ((m >> 1) & 1), c ^ (m & 1)
            peer = 4 * px + 2 * py + pc
            for a in range(n):
                for k, (s, d) in enumerate(zip(src_of(a, ins[a], peer), dst_of(a, outs[a], me))):
                    cp = pltpu.make_async_remote_copy(src_ref=s, dst_ref=d, send_sem=send_sems.at[base[a] + k, m - 1],
                                                      recv_sem=recv_sems.at[base[a] + k, m - 1],
                                                      device_id=(px, py, pc), device_id_type=MESH_ID)
                    cp.start()
                    remote.append(cp)
        for cp in copies:
            cp.wait()
        for cp in remote:
            cp.wait_send()
        for m in range(1, N_DEV):
            px, py, pc = x ^ (m >> 2), y ^ ((m >> 1) & 1), c ^ (m & 1)
            peer = 4 * px + 2 * py + pc
            for a in range(n):
                for k, (s, d) in enumerate(zip(src_of(a, ins[a], me), dst_of(a, outs[a], peer))):
                    pltpu.make_async_remote_copy(src_ref=s, dst_ref=d, send_sem=send_sems.at[base[a] + k, m - 1],
                                                 recv_sem=recv_sems.at[base[a] + k, m - 1], device_id=(px, py, pc),
                                                 device_id_type=MESH_ID).wait_recv()

    any_spec = pl.BlockSpec(memory_space=pl.ANY)
    return _call(body, name=name, out_shape=[_sds(s, a.dtype) for s, a in zip(out_shapes, arrays)],
                 in_specs=[any_spec] * n, out_specs=[any_spec] * n,
                 scratch=[pltpu.SemaphoreType.DMA((total, N_DEV - 1)), pltpu.SemaphoreType.DMA((total, N_DEV - 1)),
                          pltpu.SemaphoreType.DMA((total,))])(*arrays)


def all_gather(name, arrays, row_sharded):
    def out_shape(a):
        s = arrays[a].shape
        return (s[0], N_DEV) + s[1:] if row_sharded[a] else (N_DEV,) + s

    def src_of(a, ref, peer):
        if row_sharded[a]:
            return [ref.at[l] for l in range(arrays[a].shape[0])]
        return [ref]

    def dst_of(a, ref, me):
        if row_sharded[a]:
            return [ref.at[l, me] for l in range(arrays[a].shape[0])]
        return [ref.at[me]]

    pieces = [arrays[a].shape[0] if row_sharded[a] else 1 for a in range(len(arrays))]
    return _exchange(name, arrays, [out_shape(a) for a in range(len(arrays))], pieces, src_of, dst_of)


def _src_view(kind, ref, peer):
    return ref if kind == "gather" else ref.at[peer]


def _peers(x, y, c):
    for m in range(1, N_DEV):
        px, py, pc = x ^ (m >> 2), y ^ ((m >> 1) & 1), c ^ (m & 1)
        yield m - 1, (px, py, pc), 4 * px + 2 * py + pc


def place_own(name, kind, srcs):
    n = len(srcs)
    shapes = [(N_DEV,) + s.shape if kind == "gather" else s.shape for s in srcs]
    steps = 2

    def body(*refs):
        for a in range(n):
            refs[n + a][...] = refs[a][...]

    def spec(shape, own_block):
        R, C = shape[-2:]
        tr = R // steps
        if own_block:
            return pl.BlockSpec((None, tr, C), lambda i: (_my_index(), i, 0))
        return pl.BlockSpec((tr, C), lambda i: (i, 0))

    return _call(body, name=name, grid=(steps,), out_shape=[_sds(s, a.dtype) for s, a in zip(shapes, srcs)],
                 in_specs=[spec(s.shape, kind == "scatter") for s in srcs],
                 out_specs=[spec(s, True) for s in shapes])(*srcs)


_HBM_SPEC = pl.BlockSpec(memory_space=pltpu.HBM)
_SEM_SPEC = pl.BlockSpec(memory_space=pltpu.SEMAPHORE)
_DATAFLOW = pltpu.SideEffectType.DATAFLOW_SIDE_EFFECTING


def _remote(kind, src, land, send_sems, recv_sems, a, slot, frm, to_id, at):
    return pltpu.make_async_remote_copy(src_ref=_src_view(kind, src, frm), dst_ref=land.at[at],
                                        send_sem=send_sems.at[a * (N_DEV - 1) + slot],
                                        recv_sem=recv_sems.at[a * (N_DEV - 1) + slot],
                                        device_id=to_id, device_id_type=MESH_ID)


def exchange_begin(name, kind, srcs, lands, after):
    n = len(srcs)

    def body(*refs):
        ins, lnd = refs[:n], refs[n:2 * n]
        send_sems, recv_sems = refs[2 * n + 1], refs[2 * n + 2]
        token = refs[-1]
        x, y, c = lax.axis_index("x"), lax.axis_index("y"), lax.axis_index("c")
        me = 4 * x + 2 * y + c
        for slot, peer_id, peer in _peers(x, y, c):
            for a in range(n):
                _remote(kind, ins[a], lnd[a], send_sems, recv_sems, a, slot, peer, peer_id, me).start()
        token[...] = jnp.zeros_like(token)

    hbm = lambda t: pltpu.HBM(t.shape, t.dtype)
    outs = _pallas(
        body, name=name,
        out_shape=(pltpu.SemaphoreType.DMA((n * (N_DEV - 1),)), pltpu.SemaphoreType.DMA((n * (N_DEV - 1),)),
                   *[hbm(t) for t in srcs], *[hbm(t) for t in lands], _sds((8, 128), F32)),
        in_specs=[_HBM_SPEC] * (2 * n) + [pl.BlockSpec(memory_space=pl.ANY)],
        out_specs=(_SEM_SPEC, _SEM_SPEC, *[_HBM_SPEC] * (2 * n), pl.BlockSpec(memory_space=pltpu.VMEM)),
        input_output_aliases={i: 2 + i for i in range(2 * n)},
        compiler_params=pltpu.CompilerParams(has_side_effects=_DATAFLOW),
    )(*[pltpu.with_memory_space_constraint(t, pltpu.HBM) for t in list(srcs) + list(lands)], after)
    return (kind, outs[0], outs[1], list(outs[2:2 + n]), list(outs[2 + n:2 + 2 * n])), outs[-1]


def exchange_end(name, handle, after):
    kind, send_sems, recv_sems, srcs, lands = handle
    n = len(srcs)

    def body(*refs):
        ins, lnd = refs[:n], refs[n:2 * n]
        s_sems, r_sems = refs[2 * n], refs[2 * n + 1]
        x, y, c = lax.axis_index("x"), lax.axis_index("y"), lax.axis_index("c")
        me = 4 * x + 2 * y + c
        for slot, peer_id, peer in _peers(x, y, c):
            for a in range(n):
                _remote(kind, ins[a], lnd[a], s_sems, r_sems, a, slot, peer, peer_id, me).wait_send()
        for slot, peer_id, peer in _peers(x, y, c):
            for a in range(n):
                _remote(kind, ins[a], lnd[a], s_sems, r_sems, a, slot, me, peer_id, peer).wait_recv()

    hbm = lambda t: pltpu.HBM(t.shape, t.dtype)
    outs = _pallas(
        body, name=name, out_shape=tuple(hbm(t) for t in srcs + lands),
        in_specs=[_HBM_SPEC] * (2 * n) + [_SEM_SPEC, _SEM_SPEC, pl.BlockSpec(memory_space=pl.ANY)],
        out_specs=tuple([_HBM_SPEC] * (2 * n)), input_output_aliases={i: i for i in range(2 * n)},
        compiler_params=pltpu.CompilerParams(has_side_effects=_DATAFLOW),
    )(*srcs, *lands, send_sems, recv_sems, after)
    return list(outs[n:])


def norm_mm(name, x, gain, w, layer, bias=None):
    T, D = x.shape
    nsh, _, _, n = w.shape
    tm = _tile(T, 1024)

    def body(*refs):
        if bias is None:
            x_ref, g_ref, w_ref, u_ref, h_ref = refs
        else:
            x_ref, g_ref, w_ref, b_ref, u_ref, h_ref = refs

        @pl.when(pl.program_id(1) == 0)
        def _():
            xf = x_ref[...]
            r = lax.rsqrt(jnp.mean(xf * xf, axis=-1, keepdims=True) + EPS)
            h_ref[...] = (xf * r * g_ref[...]).astype(h_ref.dtype)

        acc = _dot(h_ref[...], w_ref[...])
        if bias is not None:
            acc = acc + b_ref[...]
        u_ref[...] = acc.astype(u_ref.dtype)

    in_specs = [pl.BlockSpec((tm, D), lambda i, j: (i, 0)),
                pl.BlockSpec((1, D), lambda i, j: (0, 0)),
                pl.BlockSpec((None, None, D, n), lambda i, j: (j, layer, 0, 0))]
    args = [x, gain.reshape(1, D), w]
    if bias is not None:
        in_specs.append(pl.BlockSpec((None, None, 1, n), lambda i, j: (j, layer, 0, 0)))
        args.append(bias)
    return _call(body, name=name, grid=(T // tm, nsh), in_specs=in_specs,
                 out_specs=[pl.BlockSpec((None, tm, n), lambda i, j: (j, i, 0)),
                            pl.BlockSpec((tm, D), lambda i, j: (i, 0))],
                 out_shape=[_sds((nsh, T, n), MM), _sds((T, D), MM)])(*args)


def mm_resid_norm(name, a, w, layer, bias, x, gain):
    nk, T, kk = a.shape
    D = x.shape[1]
    tm = _tile(T, 512)

    def body(a_ref, w_ref, b_ref, x_ref, g_ref, y_ref, xn_ref):
        y = _dot(a_ref[0], w_ref[0])
        for q in range(1, nk):
            y = y + _dot(a_ref[q], w_ref[q])
        y = y + b_ref[...]
        y_ref[...] = y
        r = lax.rsqrt(jnp.mean(y * y, axis=-1, keepdims=True) + EPS)
        xn_ref[...] = x_ref[...] + y * r * g_ref[...]

    return _call(body, name=name, grid=(T // tm,),
                 in_specs=[pl.BlockSpec((nk, tm, kk), lambda i: (0, i, 0)),
                           pl.BlockSpec((None, nk, kk, D), lambda i: (layer, 0, 0, 0)),
                           pl.BlockSpec((1, D), lambda i: (0, 0)),
                           pl.BlockSpec((tm, D), lambda i: (i, 0)),
                           pl.BlockSpec((1, D), lambda i: (0, 0))],
                 out_specs=[pl.BlockSpec((tm, D), lambda i: (i, 0))] * 2,
                 out_shape=[_sds((T, D), F32)] * 2)(a, w, bias.reshape(1, D), x, gain.reshape(1, D))


def _halo_maps(tm, hb, T):
    per = tm // hb
    last = T // hb - 1
    return (lambda i: jnp.maximum(i * per - 1, 0)), (lambda i: jnp.minimum((i + 1) * per, last))


def cm_glu_conv(name, u, dw, dwb, layer, seq):
    _, T, n = u.shape
    ct = dw.shape[-1]
    per = n // ct
    nct = 4 * per
    tm = _tile(seq, 512)
    tps = seq // tm
    hb = CONV_HALO
    prev, _ = _halo_maps(tm, hb, T)
    u4 = u.reshape(2, 4, T, n)

    def body(u_ref, uh_ref, w_ref, b_ref, o_ref, pad_ref):
        first = (pl.program_id(0) % tps) == 0
        um = u_ref[...].astype(F32)
        uh = uh_ref[...].astype(F32)
        pad_ref[pl.ds(hb, tm), :] = um[0] * _sigmoid(um[1])
        pad_ref[pl.ds(0, hb), :] = jnp.where(first, 0.0, uh[0] * _sigmoid(uh[1]))
        acc = jnp.zeros((tm, ct), F32) + b_ref[...]
        for k in range(CONV_W):
            acc = acc + w_ref[pl.ds(k, 1), :] * pad_ref[pl.ds(hb - (CONV_W - 1) + k, tm), :]
        o_ref[...] = acc

    return _call(body, name=name, grid=(T // tm, nct),
                 in_specs=[pl.BlockSpec((2, None, tm, ct), lambda i, c: (0, c // per, i, c % per)),
                           pl.BlockSpec((2, None, hb, ct), lambda i, c: (0, c // per, prev(i), c % per)),
                           pl.BlockSpec((None, None, CONV_W, ct), lambda i, c: (c, layer, 0, 0)),
                           pl.BlockSpec((None, None, 1, ct), lambda i, c: (c, layer, 0, 0))],
                 out_specs=pl.BlockSpec((tm, ct), lambda i, c: (i, c)),
                 out_shape=_sds((T, nct * ct), F32),
                 scratch=[pltpu.VMEM((tm + hb, ct), F32)])(u4, u4, dw, dwb)


def ln_silu(name, c, g, b):
    T, D = c.shape
    tm = _tile(T, 512)

    def body(c_ref, g_ref, b_ref, s_ref):
        cf = c_ref[...]
        mu = jnp.mean(cf, axis=-1, keepdims=True)
        xc = cf - mu
        r = lax.rsqrt(jnp.mean(xc * xc, axis=-1, keepdims=True) + EPS)
        t = xc * r * g_ref[...] + b_ref[...]
        s_ref[...] = (t * _sigmoid(t)).astype(s_ref.dtype)

    return _call(body, name=name, grid=(T // tm,),
                 in_specs=[pl.BlockSpec((tm, D), lambda i: (i, 0)), pl.BlockSpec((1, D), lambda i: (0, 0)),
                           pl.BlockSpec((1, D), lambda i: (0, 0))],
                 out_specs=pl.BlockSpec((None, tm, D), lambda i: (0, i, 0)),
                 out_shape=_sds((1, T, D), MM))(c, g.reshape(1, D), b.reshape(1, D))


def ffn_conv_gate(name, u, dw, dwb, layer, seq):
    _, T, n = u.shape
    tm = _tile(seq, 512)
    tps = seq // tm
    hb = FFN_HALO
    prev, _ = _halo_maps(tm, hb, T)
    u4 = u.reshape(2, 4, T, n)
    dw4 = dw.reshape(2, 4, dw.shape[1], FFN_CONV_W, n)
    dwb4 = dwb.reshape(dwb.shape[0], 2, 4, 1, n)

    def body(u_ref, uh_ref, w_ref, b_ref, z_ref, pad_ref):
        first = (pl.program_id(0) % tps) == 0
        conv = []
        for half in range(2):
            pad_ref[pl.ds(hb, tm), :] = u_ref[half].astype(F32)
            pad_ref[pl.ds(0, hb), :] = jnp.where(first, 0.0, uh_ref[half].astype(F32))
            acc = jnp.zeros((tm, n), F32) + b_ref[half]
            for k in range(FFN_CONV_W):
                acc = acc + w_ref[half, pl.ds(k, 1), :] * pad_ref[pl.ds(hb - (FFN_CONV_W - 1) + k, tm), :]
            conv.append(acc)
        a, g = conv
        z_ref[...] = (g * _sigmoid(g) * a).astype(z_ref.dtype)

    return _call(body, name=name, grid=(T // tm, 4),
                 in_specs=[pl.BlockSpec((2, None, tm, n), lambda i, j: (0, j, i, 0)),
                           pl.BlockSpec((2, None, hb, n), lambda i, j: (0, j, prev(i), 0)),
                           pl.BlockSpec((2, None, None, FFN_CONV_W, n), lambda i, j: (0, j, layer, 0, 0)),
                           pl.BlockSpec((None, 2, None, 1, n), lambda i, j: (layer, 0, j, 0, 0))],
                 out_specs=pl.BlockSpec((None, tm, n), lambda i, j: (j, i, 0)),
                 out_shape=_sds((4, T, n), MM),
                 scratch=[pltpu.VMEM((tm + hb, n), F32)])(u4, u4, dw4, dwb4)


def _head_specs(seq, dh, q_heads, kv_heads):
    def spec(per, base):
        def imap(b, h, g):
            f = base + g * N_HEADS + h
            return (f // per, b, f % per)
        return pl.BlockSpec((None, seq, dh), imap)
    return spec(q_heads, 0), spec(kv_heads, 0), spec(kv_heads, N_GROUPS * N_HEADS)


def _rows(start, d, blocks=1):
    size = blocks * ATT_BLOCK
    return pl.ds(start, size, stride=d) if d > 1 else pl.ds(start, size)


def _att_pad():
    return max(ATT_BLOCK * d for d in DILATIONS[:-1])


def _band_mask(n, nblk):
    if nblk == 1:
        qi = lax.broadcasted_iota(jnp.int32, (ATT_BLOCK, ATT_BLOCK), 0)
        kj = lax.broadcasted_iota(jnp.int32, (ATT_BLOCK, ATT_BLOCK), 1)
        return kj <= qi
    qi = lax.broadcasted_iota(jnp.int32, (ATT_BLOCK, 2 * ATT_BLOCK), 0)
    kj = lax.broadcasted_iota(jnp.int32, (ATT_BLOCK, 2 * ATT_BLOCK), 1)
    return (kj >= qi) & (kj <= qi + ATT_BLOCK) & ((n > 0) | (kj >= ATT_BLOCK))


def _band_rows(r, n, d, nblk):
    if nblk == 1:
        return _rows(_att_pad() + r + d * ATT_BLOCK * n, d)
    return _rows(_att_pad() + r + d * ATT_BLOCK * (n - 1), d, blocks=2)


def attn_fwd(name, q, kv, nb, seq):
    _, T, qn = q.shape
    dh = qn * N_DEV // (N_GROUPS * N_HEADS)
    scale = 1.0 / (dh ** 0.5)
    qs, ks, vs = _head_specs(seq, dh, qn // dh, kv.shape[2] // dh)

    def body(q_ref, k_ref, v_ref, m_ref, l_ref, qf, kf, vf, *branch):
        og, lg = branch[:N_GROUPS], branch[N_GROUPS:]
        pad = _att_pad()
        qf[...] = q_ref[...].astype(F32)
        for t_ref, s_ref in ((k_ref, kf), (v_ref, vf)):
            s_ref[pl.ds(0, pad), :] = jnp.zeros((pad, dh), F32)
            s_ref[pl.ds(pad, seq), :] = t_ref[...].astype(F32)
        for g in range(N_GROUPS):
            d = DILATIONS[g]
            nblk = seq // d // ATT_BLOCK

            def block(idx, carry, d=d, nblk=nblk, g=g):
                r, n = idx // nblk, idx % nblk
                rq = _rows(r + d * ATT_BLOCK * n, d)
                rk = _band_rows(r, n, d, nblk)
                s = jnp.where(_band_mask(n, nblk), _dot_nt(qf[rq, :].astype(MM), kf[rk, :].astype(MM)) * scale, NEG)
                m = jnp.max(s, axis=-1, keepdims=True)
                p = jnp.exp(s - m)
                den = jnp.sum(p, axis=-1, keepdims=True)
                og[g][rq, :] = _dot(p.astype(MM), vf[rk, :].astype(MM)) / den
                lg[g][rq, :] = jnp.broadcast_to(m + jnp.log(den), (ATT_BLOCK, dh))
                return carry

            @pl.when(pl.program_id(2) == g)
            def _(block=block, d=d, nblk=nblk):
                lax.fori_loop(0, d * nblk, block, 0, unroll=2)

        @pl.when(pl.program_id(2) == N_GROUPS - 1)
        def _():
            mx = jnp.maximum(jnp.maximum(lg[0][...], lg[1][...]), lg[2][...])
            e = [jnp.exp(lg[g][...] - mx) for g in range(N_GROUPS)]
            tot = e[0] + e[1] + e[2]
            m_ref[...] = ((e[0] * og[0][...] + e[1] * og[1][...] + e[2] * og[2][...]) / tot).astype(m_ref.dtype)
            l_ref[...] = mx + jnp.log(tot)

    return _call(body, name=name, grid=(nb, N_HEADS, N_GROUPS), in_specs=[qs, ks, vs],
                 out_specs=[pl.BlockSpec((None, seq, dh), lambda b, h, g: (0, b, h)),
                            pl.BlockSpec((seq, dh), lambda b, h, g: (b, h))],
                 out_shape=[_sds((1, T, N_HEADS * dh), MM), _sds((T, N_HEADS * dh), F32)],
                 scratch=[pltpu.VMEM((seq, dh), F32)] + [pltpu.VMEM((_att_pad() + seq, dh), F32)] * 2
                 + [pltpu.VMEM((seq, dh), F32)] * (2 * N_GROUPS))(q, kv, kv)


def loss_fwd_bwd(name, x, target):
    T, D = x.shape
    tm = _tile(T, 512)

    def body(x_ref, t_ref, dx_ref, l_ref):
        @pl.when(pl.program_id(0) == 0)
        def _():
            l_ref[...] = jnp.zeros_like(l_ref)
        err = x_ref[...] - t_ref[...]
        dx_ref[...] = err * (1.0 / D)
        l_ref[...] += 0.5 * jnp.sum(jnp.mean(err * err, axis=-1, keepdims=True), axis=0, keepdims=True)

    dx, l = _call(body, name=name, grid=(T // tm,),
                  in_specs=[pl.BlockSpec((tm, D), lambda i: (i, 0))] * 2,
                  out_specs=[pl.BlockSpec((tm, D), lambda i: (i, 0)), pl.BlockSpec((1, 1), lambda i: (0, 0))],
                  out_shape=[_sds((T, D), F32), _sds((1, 1), F32)])(x, target)
    return dx, l


def resid_norm_bwd(name, dx, y, gain):
    T, D = y.shape
    tm = _tile(T, 512)

    def body(dx_ref, y_ref, g_ref, dy_ref, dg_ref, db_ref):
        @pl.when(pl.program_id(0) == 0)
        def _():
            dg_ref[...] = jnp.zeros_like(dg_ref)
            db_ref[...] = jnp.zeros_like(db_ref)
        y = y_ref[...]
        d = dx_ref[...]
        r = lax.rsqrt(jnp.mean(y * y, axis=-1, keepdims=True) + EPS)
        yh = y * r
        dyh = d * g_ref[...]
        dy = r * (dyh - yh * jnp.mean(dyh * yh, axis=-1, keepdims=True))
        dy_ref[...] = dy.astype(dy_ref.dtype)
        dg_ref[...] += jnp.sum(d * yh, axis=0, keepdims=True)
        db_ref[...] += jnp.sum(dy, axis=0, keepdims=True)

    return _call(body, name=name, grid=(T // tm,),
                 in_specs=[pl.BlockSpec((tm, D), lambda i: (i, 0))] * 2 + [pl.BlockSpec((1, D), lambda i: (0, 0))],
                 out_specs=[pl.BlockSpec((None, tm, D), lambda i: (0, i, 0))] + [pl.BlockSpec((1, D), lambda i: (0, 0))] * 2,
                 out_shape=[_sds((1, T, D), MM), _sds((1, D), F32), _sds((1, D), F32)])(dx, y, gain.reshape(1, D))


def mm_nt(name, dy, w, layer, out_dtype):
    _, T, D = dy.shape
    _, nk, kk, _ = w.shape
    tm = _tile(T, 1024)

    def body(dy_ref, w_ref, o_ref):
        o_ref[...] = _dot_nt(dy_ref[...], w_ref[...]).astype(o_ref.dtype)

    return _call(body, name=name, grid=(T // tm, nk),
                 in_specs=[pl.BlockSpec((None, tm, D), lambda i, q: (0, i, 0)),
                           pl.BlockSpec((None, None, kk, D), lambda i, q: (layer, q, 0, 0))],
                 out_specs=pl.BlockSpec((None, tm, kk), lambda i, q: (q, i, 0)),
                 out_shape=_sds((nk, T, kk), out_dtype))(dy, w)


def mm_nt_norm_bwd(name, du, w, layer, x_in, gain, dx_res):
    nsh, T, n = du.shape
    D = x_in.shape[1]
    tm = _tile(T, 512)

    def body(du_ref, w_ref, x_ref, g_ref, dr_ref, dx_ref, dg_ref, acc_ref):
        i, j = pl.program_id(0), pl.program_id(1)

        @pl.when((i == 0) & (j == 0))
        def _():
            dg_ref[...] = jnp.zeros_like(dg_ref)

        part = _dot_nt(du_ref[...], w_ref[...])

        @pl.when(j == 0)
        def _():
            acc_ref[...] = part

        @pl.when(j > 0)
        def _():
            acc_ref[...] += part

        @pl.when(j == nsh - 1)
        def _():
            x = x_ref[...]
            dh = acc_ref[...]
            r = lax.rsqrt(jnp.mean(x * x, axis=-1, keepdims=True) + EPS)
            xh = x * r
            dxh = dh * g_ref[...]
            dx_ref[...] = dr_ref[...] + r * (dxh - xh * jnp.mean(dxh * xh, axis=-1, keepdims=True))
            dg_ref[...] += jnp.sum(dh * xh, axis=0, keepdims=True)

    return _call(body, name=name, grid=(T // tm, nsh),
                 in_specs=[pl.BlockSpec((None, tm, n), lambda i, j: (j, i, 0)),
                           pl.BlockSpec((None, None, D, n), lambda i, j: (j, layer, 0, 0)),
                           pl.BlockSpec((tm, D), lambda i, j: (i, 0)),
                           pl.BlockSpec((1, D), lambda i, j: (0, 0)),
                           pl.BlockSpec((tm, D), lambda i, j: (i, 0))],
                 out_specs=[pl.BlockSpec((tm, D), lambda i, j: (i, 0)), pl.BlockSpec((1, D), lambda i, j: (0, 0))],
                 out_shape=[_sds((T, D), F32), _sds((1, D), F32)],
                 scratch=[pltpu.VMEM((tm, D), F32)])(du, w, x_in, gain.reshape(1, D), dx_res)


def mm_tn(name, a, b):
    na, T, ka = a.shape
    nb, _, kb = b.shape
    nj = max(na, nb)
    tk = _tile(T, 1024)
    nt = T // tk

    def body(a_ref, b_ref, o_ref, acc_ref):
        t = pl.program_id(1)
        part = _dot_tn(a_ref[...], b_ref[...])

        @pl.when(t == 0)
        def _():
            acc_ref[...] = part

        @pl.when(t > 0)
        def _():
            acc_ref[...] += part

        @pl.when(t == nt - 1)
        def _():
            o_ref[...] = acc_ref[...].astype(o_ref.dtype)

    return _call(body, name=name, grid=(nj, nt),
                 in_specs=[pl.BlockSpec((None, tk, ka), (lambda j, t: (j, t, 0)) if na > 1 else (lambda j, t: (0, t, 0))),
                           pl.BlockSpec((None, tk, kb), (lambda j, t: (j, t, 0)) if nb > 1 else (lambda j, t: (0, t, 0)))],
                 out_specs=pl.BlockSpec((None, ka, kb), lambda j, t: (j, 0, 0)),
                 out_shape=_sds((nj, ka, kb), MM),
                 scratch=[pltpu.VMEM((ka, kb), F32)])(a, b)


def ffn_conv_gate_bwd(name, u, dz, dw, dwb, layer, seq):
    _, T, n = u.shape
    tm = _tile(seq, 512)
    tps = seq // tm
    hb = FFN_HALO
    prev, nxt = _halo_maps(tm, hb, T)
    u4 = u.reshape(2, 4, T, n)
    dw4 = dw.reshape(2, 4, dw.shape[1], FFN_CONV_W, n)
    dwb4 = dwb.reshape(dwb.shape[0], 2, 4, 1, n)
    K = FFN_CONV_W
    te = tm + hb

    def body(u_ref, up_ref, un_ref, dz_ref, dzn_ref, w_ref, b_ref, du_ref, ddw_ref, ddb_ref, pad_ref, da_ref):
        i = pl.program_id(1)
        first = (i % tps) == 0
        last = (i % tps) == tps - 1

        @pl.when(i == 0)
        def _():
            ddw_ref[...] = jnp.zeros_like(ddw_ref)
            ddb_ref[...] = jnp.zeros_like(ddb_ref)

        for half in range(2):
            pad_ref[half, pl.ds(0, hb), :] = jnp.where(first, 0.0, up_ref[half].astype(F32))
            pad_ref[half, pl.ds(hb, tm), :] = u_ref[half].astype(F32)
            pad_ref[half, pl.ds(hb + tm, hb), :] = un_ref[half].astype(F32)
        conv = []
        for half in range(2):
            acc = jnp.zeros((te, n), F32) + b_ref[half]
            for k in range(K):
                acc = acc + w_ref[half, pl.ds(k, 1), :] * pad_ref[half, pl.ds(hb - (K - 1) + k, te), :]
            conv.append(acc)
        a, g = conv
        dzm = dz_ref[...].astype(F32)
        dzn = jnp.where(last, 0.0, dzn_ref[...].astype(F32))
        sg = _sigmoid(g)
        silu = g * sg
        dsilu = sg * (1.0 + g * (1.0 - sg))
        da_ref[0, pl.ds(0, tm), :] = dzm * silu[:tm]
        da_ref[0, pl.ds(tm, hb), :] = dzn * silu[tm:]
        da_ref[1, pl.ds(0, tm), :] = dzm * a[:tm] * dsilu[:tm]
        da_ref[1, pl.ds(tm, hb), :] = dzn * a[tm:] * dsilu[tm:]
        for half in range(2):
            dmain = da_ref[half, pl.ds(0, tm), :]
            acc = jnp.zeros((tm, n), F32)
            for k in range(K):
                acc = acc + w_ref[half, pl.ds(k, 1), :] * da_ref[half, pl.ds(K - 1 - k, tm), :]
                ddw_ref[half, pl.ds(k, 1), :] += jnp.sum(
                    dmain * pad_ref[half, pl.ds(hb - (K - 1) + k, tm), :], axis=0, keepdims=True)
            du_ref[half] = acc.astype(du_ref.dtype)
            ddb_ref[half] += jnp.sum(dmain, axis=0, keepdims=True)

    du, ddw, ddb = _call(
        body, name=name, grid=(4, T // tm),
        in_specs=[pl.BlockSpec((2, None, tm, n), lambda j, i: (0, j, i, 0)),
                  pl.BlockSpec((2, None, hb, n), lambda j, i: (0, j, prev(i), 0)),
                  pl.BlockSpec((2, None, hb, n), lambda j, i: (0, j, nxt(i), 0)),
                  pl.BlockSpec((None, tm, n), lambda j, i: (j, i, 0)),
                  pl.BlockSpec((None, hb, n), lambda j, i: (j, nxt(i), 0)),
                  pl.BlockSpec((2, None, None, K, n), lambda j, i: (0, j, layer, 0, 0)),
                  pl.BlockSpec((None, 2, None, 1, n), lambda j, i: (layer, 0, j, 0, 0))],
        out_specs=[pl.BlockSpec((2, None, tm, n), lambda j, i: (0, j, i, 0)),
                   pl.BlockSpec((2, None, K, n), lambda j, i: (0, j, 0, 0)),
                   pl.BlockSpec((2, None, 1, n), lambda j, i: (0, j, 0, 0))],
        out_shape=[_sds((2, 4, T, n), MM), _sds((2, 4, K, n), F32), _sds((2, 4, 1, n), F32)],
        scratch=[pltpu.VMEM((2, tm + 2 * hb, n), F32), pltpu.VMEM((2, te, n), F32)])(u4, u4, u4, dz, dz, dw4, dwb4)
    return du.reshape(8, T, n), ddw, ddb


def ln_silu_bwd(name, ds, c, g, b):
    T, D = c.shape
    tm = _tile(T, 512)

    def body(ds_ref, c_ref, g_ref, b_ref, dc_ref, dg_ref, db_ref):
        @pl.when(pl.program_id(0) == 0)
        def _():
            dg_ref[...] = jnp.zeros_like(dg_ref)
            db_ref[...] = jnp.zeros_like(db_ref)
        cf = c_ref[...]
        mu = jnp.mean(cf, axis=-1, keepdims=True)
        xc = cf - mu
        r = lax.rsqrt(jnp.mean(xc * xc, axis=-1, keepdims=True) + EPS)
        xh = xc * r
        t = xh * g_ref[...] + b_ref[...]
        sg = _sigmoid(t)
        dt = ds_ref[...] * (sg * (1.0 + t * (1.0 - sg)))
        dg_ref[...] += jnp.sum(dt * xh, axis=0, keepdims=True)
        db_ref[...] += jnp.sum(dt, axis=0, keepdims=True)
        dxh = dt * g_ref[...]
        dc_ref[...] = r * (dxh - jnp.mean(dxh, axis=-1, keepdims=True)
                           - xh * jnp.mean(dxh * xh, axis=-1, keepdims=True))

    vec = pl.BlockSpec((1, D), lambda i: (0, 0))
    return _call(body, name=name, grid=(T // tm,),
                 in_specs=[pl.BlockSpec((None, tm, D), lambda i: (0, i, 0)), pl.BlockSpec((tm, D), lambda i: (i, 0)),
                           vec, vec],
                 out_specs=[pl.BlockSpec((tm, D), lambda i: (i, 0)), vec, vec],
                 out_shape=[_sds((T, D), F32), _sds((1, D), F32), _sds((1, D), F32)])(
                     ds, c, g.reshape(1, D), b.reshape(1, D))


def cm_glu_conv_bwd(name, u, dc, dw, layer, seq):
    _, T, n = u.shape
    ct = dw.shape[-1]
    per = n // ct
    nct = 4 * per
    tm = _tile(seq, 512)
    tps = seq // tm
    hb = CONV_HALO
    prev, nxt = _halo_maps(tm, hb, T)
    u4 = u.reshape(2, 4, T, n)
    K = CONV_W

    def body(u_ref, up_ref, dc_ref, dcn_ref, w_ref, du_ref, ddw_ref, ddb_ref, dbi_ref, padp_ref, padd_ref):
        i = pl.program_id(1)
        first = (i % tps) == 0
        last = (i % tps) == tps - 1

        @pl.when(i == 0)
        def _():
            ddw_ref[...] = jnp.zeros_like(ddw_ref)
            ddb_ref[...] = jnp.zeros_like(ddb_ref)
            dbi_ref[...] = jnp.zeros_like(dbi_ref)

        um = u_ref[...].astype(F32)
        uh = up_ref[...].astype(F32)
        sg = _sigmoid(um[1])
        padp_ref[pl.ds(hb, tm), :] = um[0] * sg
        padp_ref[pl.ds(0, hb), :] = jnp.where(first, 0.0, uh[0] * _sigmoid(uh[1]))
        dcm = dc_ref[...]
        padd_ref[pl.ds(0, tm), :] = dcm
        padd_ref[pl.ds(tm, hb), :] = jnp.where(last, 0.0, dcn_ref[...])
        dp = jnp.zeros((tm, ct), F32)
        for k in range(K):
            dp = dp + w_ref[pl.ds(k, 1), :] * padd_ref[pl.ds(K - 1 - k, tm), :]
            ddw_ref[pl.ds(k, 1), :] += jnp.sum(dcm * padp_ref[pl.ds(hb - (K - 1) + k, tm), :], axis=0, keepdims=True)
        ddb_ref[...] += jnp.sum(dcm, axis=0, keepdims=True)
        dv = dp * sg
        dg = dp * um[0] * sg * (1.0 - sg)
        du_ref[0] = dv.astype(du_ref.dtype)
        du_ref[1] = dg.astype(du_ref.dtype)
        dbi_ref[0] += jnp.sum(dv, axis=0, keepdims=True)
        dbi_ref[1] += jnp.sum(dg, axis=0, keepdims=True)

    du, ddw, ddb, dbi = _call(
        body, name=name, grid=(nct, T // tm),
        in_specs=[pl.BlockSpec((2, None, tm, ct), lambda c, i: (0, c // per, i, c % per)),
                  pl.BlockSpec((2, None, hb, ct), lambda c, i: (0, c // per, prev(i), c % per)),
                  pl.BlockSpec((tm, ct), lambda c, i: (i, c)),
                  pl.BlockSpec((hb, ct), lambda c, i: (nxt(i), c)),
                  pl.BlockSpec((None, None, K, ct), lambda c, i: (c, layer, 0, 0))],
        out_specs=[pl.BlockSpec((2, None, tm, ct), lambda c, i: (0, c // per, i, c % per)),
                   pl.BlockSpec((None, K, ct), lambda c, i: (c, 0, 0)),
                   pl.BlockSpec((None, 1, ct), lambda c, i: (c, 0, 0)),
                   pl.BlockSpec((2, None, 1, ct), lambda c, i: (0, c // per, 0, c % per))],
        out_shape=[_sds((2, 4, T, n), MM), _sds((nct, K, ct), F32), _sds((nct, 1, ct), F32), _sds((2, 4, 1, n), F32)],
        scratch=[pltpu.VMEM((tm + hb, ct), F32), pltpu.VMEM((tm + hb, ct), F32)])(u4, u4, dc, dc, dw)
    return du.reshape(8, T, n), ddw, ddb, dbi


def attn_bwd(name, q, kv, dm, merged, lse, dkv_prev, nb, seq):
    _, T, qn = q.shape
    kn = kv.shape[2]
    dh = qn * N_DEV // (N_GROUPS * N_HEADS)
    scale = 1.0 / (dh ** 0.5)
    qs, ks, vs = _head_specs(seq, dh, qn // dh, kn // dh)
    has_prev = dkv_prev is not None
    n_in = 6 + (2 if has_prev else 0)

    def body(*refs):
        q_ref, k_ref, v_ref, dm_ref, mg_ref, l_ref = refs[:6]
        pk_ref, pv_ref = refs[6:8] if has_prev else (None, None)
        dq_ref, dk_ref, dv_ref = refs[n_in:n_in + 3]
        qf, kf, vf, dqf, dkf, dvf, dlt = refs[n_in + 3:]
        pad = _att_pad()
        dlt[...] = jnp.broadcast_to(jnp.sum(dm_ref[...] * mg_ref[...].astype(F32), axis=-1, keepdims=True), (seq, dh))
        qf[...] = q_ref[...].astype(F32)
        for t_ref, s_ref in ((k_ref, kf), (v_ref, vf)):
            s_ref[pl.ds(0, pad), :] = jnp.zeros((pad, dh), F32)
            s_ref[pl.ds(pad, seq), :] = t_ref[...].astype(F32)
        dkf[...] = jnp.zeros_like(dkf)
        dvf[...] = jnp.zeros_like(dvf)
        for g in range(N_GROUPS):
            d = DILATIONS[g]
            nblk = seq // d // ATT_BLOCK

            def block(idx, carry, d=d, nblk=nblk):
                r, n = idx // nblk, idx % nblk
                rq = _rows(r + d * ATT_BLOCK * n, d)
                rk = _band_rows(r, n, d, nblk)
                qb = qf[rq, :].astype(MM)
                dmb = dm_ref[rq, :].astype(MM)
                kb = kf[rk, :].astype(MM)
                s = jnp.where(_band_mask(n, nblk), _dot_nt(qb, kb) * scale, NEG)
                p = jnp.exp(s - l_ref[rq, :][:, :1])
                dvf[rk, :] += _dot_tn(p.astype(MM), dmb)
                dsc = (p * (_dot_nt(dmb, vf[rk, :].astype(MM)) - dlt[rq, :][:, :1]) * scale).astype(MM)
                dkf[rk, :] += _dot_tn(dsc, qb)
                dqf[rq, :] = _dot(dsc, kb)
                return carry

            @pl.when(pl.program_id(2) == g)
            def _(block=block, d=d, nblk=nblk):
                lax.fori_loop(0, d * nblk, block, 0, unroll=2)

        dq_ref[...] = dqf[...].astype(dq_ref.dtype)
        dk, dv = dkf[pl.ds(pad, seq), :], dvf[pl.ds(pad, seq), :]
        if has_prev:
            dk, dv = dk + pk_ref[...].astype(F32), dv + pv_ref[...].astype(F32)
        dk_ref[...] = dk.astype(dk_ref.dtype)
        dv_ref[...] = dv.astype(dv_ref.dtype)

    full = pl.BlockSpec((None, seq, dh), lambda b, h, g: (0, b, h))
    in_specs = [qs, ks, vs, full, full, pl.BlockSpec((seq, dh), lambda b, h, g: (b, h))]
    args = [q, kv, kv, dm, merged, lse]
    if has_prev:
        in_specs += [ks, ks]
        args += [dkv_prev[0], dkv_prev[1]]
    short, padded = pltpu.VMEM((seq, dh), F32), pltpu.VMEM((_att_pad() + seq, dh), F32)
    dq, dk, dv = _call(body, name=name, grid=(nb, N_HEADS, N_GROUPS), in_specs=in_specs, out_specs=[qs, ks, ks],
                       out_shape=[_sds(q.shape, MM), _sds((N_DEV // 2, T, kn), MM), _sds((N_DEV // 2, T, kn), MM)],
                       scratch=[short, padded, padded, short, padded, padded, short])(*args)
    return dq, (dk, dv)


def _adamw_math(w, g, m, v):
    m = ADAM_B1 * m + (1.0 - ADAM_B1) * g
    v = ADAM_B2 * v + (1.0 - ADAM_B2) * (g * g)
    m_hat = m / (1.0 - ADAM_B1 ** ADAM_STEP)
    v_hat = v / (1.0 - ADAM_B2 ** ADAM_STEP)
    delta = -ADAM_LR * (m_hat / (jnp.sqrt(v_hat) + ADAM_EPS) + ADAM_WD * w)
    return delta, m, v


def adamw_sum(name, w, m, v, parts):
    L, R, C = w.shape
    tr = _tile(R, 256)

    def body(*refs):
        w_ref, m_ref, v_ref = refs[:3]
        p_refs = refs[3:3 + L]
        g_ref, d_ref, nm_ref, nv_ref = refs[3 + L:]
        for l in range(L):
            @pl.when(pl.program_id(0) == l)
            def _(p_ref=p_refs[l]):
                g = p_ref[0].astype(F32)
                for k in range(1, N_DEV):
                    g = g + p_ref[k].astype(F32)
                g_ref[...] = g
                d_ref[...], nm_ref[...], nv_ref[...] = _adamw_math(w_ref[...], g, m_ref[...], v_ref[...])

    blk = pl.BlockSpec((None, tr, C), lambda l, i: (l, i, 0))
    part = lambda k: pl.BlockSpec((N_DEV, tr, C), lambda l, i: (0, jnp.where(l == k, i, 0), 0))
    return _call(body, name=name, grid=(L, R // tr), in_specs=[blk, blk, blk] + [part(k) for k in range(L)],
                 out_specs=[blk] * 4, out_shape=[_sds((L, R, C), F32)] * 4)(w, m, v, *parts)


def sum_partials(name, parts):
    _, R, C = parts.shape
    tr = _tile(R, 512)

    def body(p_ref, o_ref):
        g = p_ref[0]
        for k in range(1, N_DEV):
            g = g + p_ref[k]
        o_ref[...] = g

    return _call(body, name=name, grid=(R // tr,),
                 in_specs=[pl.BlockSpec((N_DEV, tr, C), lambda i: (0, i, 0))],
                 out_specs=pl.BlockSpec((tr, C), lambda i: (i, 0)), out_shape=_sds((R, C), F32))(parts)


def adamw_small(name, w, g, m, v):
    R, C = w.shape
    tr = _tile(R, 512)

    def body(w_ref, g_ref, m_ref, v_ref, d_ref, nm_ref, nv_ref):
        d_ref[...], nm_ref[...], nv_ref[...] = _adamw_math(w_ref[...], g_ref[...], m_ref[...], v_ref[...])

    blk = pl.BlockSpec((tr, C), lambda i: (i, 0))
    return _call(body, name=name, grid=(R // tr,), in_specs=[blk] * 4, out_specs=[blk] * 3,
                 out_shape=[_sds((R, C), F32)] * 3)(w, g, m, v)


def _pack(arrays):
    pieces = []
    for a in arrays:
        f = a.reshape(-1).astype(F32)
        pieces.append(jnp.pad(f, (0, (-f.shape[0]) % PACK)))
    return jnp.concatenate(pieces).reshape(-1, 128)


def _unpack(flat, shapes):
    out, off = [], 0
    f = flat.reshape(-1)
    for s in shapes:
        size = 1
        for d in s:
            size *= d
        out.append(f[off:off + size].reshape(s))
        off += size + (-size) % PACK
    return out


def kernel(x, mix_pre_g, mix_post_g, ffn_pre_g, ffn_post_g, cm_w_in, cm_b_in, cm_dw, cm_dw_b, cm_ln_g, cm_ln_b, cm_w_out, cm_b_out, kv_norm_g, w_kv, w_q, w_o, ffn_w_in, ffn_dw, ffn_dw_b, ffn_w_out, loss_target, m_mix_pre_g, m_mix_post_g, m_ffn_pre_g, m_ffn_post_g, m_cm_w_in, m_cm_b_in, m_cm_dw, m_cm_dw_b, m_cm_ln_g, m_cm_ln_b, m_cm_w_out, m_cm_b_out, m_kv_norm_g, m_w_kv, m_w_q, m_w_o, m_ffn_w_in, m_ffn_dw, m_ffn_dw_b, m_ffn_w_out, v_mix_pre_g, v_mix_post_g, v_ffn_pre_g, v_ffn_post_g, v_cm_w_in, v_cm_b_in, v_cm_dw, v_cm_dw_b, v_cm_ln_g, v_cm_ln_b, v_cm_w_out, v_cm_b_out, v_kv_norm_g, v_w_kv, v_w_q, v_w_o, v_ffn_w_in, v_ffn_dw, v_ffn_dw_b, v_ffn_w_out):
    nb, seq, D = x.shape
    T = nb * seq
    me = _my_index()
    n_b = DEPTH - N_A

    nf = ffn_w_in.shape[-1]

    stages = [(part, i) for i in range(DEPTH) for part in ("mix", "ffn")]

    def stage_sources(stage):
        part, i = stage
        if part == "ffn":
            src = {"fin": ffn_w_in[i], "fout": ffn_w_out[i]}
            if i == N_A - 1:
                src.update(kv=w_kv)
        elif i < N_A:
            src = {"cin": cm_w_in[i], "cout": cm_w_out[i]}
        else:
            src = {"q": w_q[i - N_A], "o": w_o[i - N_A]}
        return {k: t.astype(MM) for k, t in src.items()}

    def begin_gather(stage, after):
        src = stage_sources(stage)
        names, arrays = list(src), list(src.values())
        tag = f"{stage[0]}{stage[1]}"
        lands = place_own(f"gather_own_{tag}", "gather", arrays)
        handle, token = exchange_begin(f"gather_begin_{tag}", "gather", arrays, lands, after)
        return (names, handle), token

    def end_gather(stage, pending, after):
        names, handle = pending
        W = dict(zip(names, exchange_end(f"gather_end_{stage[0]}{stage[1]}", handle, after)))
        for k in W:
            if k in ("cout", "o"):
                W[k] = W[k].reshape(1, 1, D, D)
            elif k == "fout":
                W[k] = W[k].reshape(1, 4, nf, D)
            else:
                W[k] = W[k][:, None]
        return W

    small = [cm_b_in[:, None, :], cm_dw, cm_dw_b[:, None, :], cm_ln_g, cm_ln_b, cm_b_out, ffn_dw]
    Bcin, DWc, DWBc, LNg, LNb, Bcout, DWf = all_gather("gather_small", small, [False] * len(small))
    LNg = jnp.swapaxes(LNg, 0, 1).reshape(N_A, D)
    LNb = jnp.swapaxes(LNb, 0, 1).reshape(N_A, D)
    Bcout = jnp.swapaxes(Bcout, 0, 1).reshape(N_A, D)
    DWBf = ffn_dw_b.reshape(DEPTH, N_DEV, 1, nf)
    zero_bias = jnp.zeros((D,), F32)

    xs = x.reshape(T, D)
    sv = []
    kv = hkv = None
    pending, _ = begin_gather(stages[0], xs)
    Ws = {stages[0]: end_gather(stages[0], pending, xs)}
    sv = [{} for _ in range(DEPTH)]
    for idx, stage in enumerate(stages):
        part, i = stage
        L, W = sv[i], Ws[stage]
        gain = mix_pre_g[i] if part == "mix" else ffn_pre_g[i]
        following = stages[idx + 1] if idx + 1 < len(stages) else None
        if following is not None:
            pending, token = begin_gather(following, xs)
            gain = gain + token[0, 0]
        if part == "mix":
            L["x_in"] = xs
            if i < N_A:
                L["u"], L["h"] = norm_mm(f"cm_in_{i}", xs, gain, W["cin"], 0, Bcin[:, i:i + 1])
                L["c"] = cm_glu_conv(f"cm_conv_{i}", L["u"], DWc, DWBc, i, seq)
                L["s"] = ln_silu(f"cm_ln_{i}", L["c"], LNg[i], LNb[i])
                L["y"], xs = mm_resid_norm(f"cm_out_{i}", L["s"], W["cout"], 0, Bcout[i], xs, mix_post_g[i])
            else:
                L["q"], L["h"] = norm_mm(f"attn_q_{i}", xs, gain, W["q"], 0)
                L["mg"], L["lse"] = attn_fwd(f"attn_{i}", L["q"], kv, nb, seq)
                L["y"], xs = mm_resid_norm(f"attn_out_{i}", L["mg"], W["o"], 0, zero_bias, xs, mix_post_g[i])
            L["x1"] = xs
        else:
            L["uf"], L["hf"] = norm_mm(f"ffn_in_{i}", xs, gain, W["fin"], 0)
            L["z"] = ffn_conv_gate(f"ffn_conv_{i}", L["uf"], DWf, DWBf, i, seq)
            L["yf"], xs = mm_resid_norm(f"ffn_out_{i}", L["z"], W["fout"], 0, zero_bias, xs, ffn_post_g[i])
            if i == N_A - 1:
                kv, hkv = norm_mm("kv_proj", xs, kv_norm_g, W["kv"], 0)
        if following is not None:
            Ws[following] = end_gather(following, pending, xs)
    dx, loss_part = loss_fwd_bwd("loss", xs, loss_target.reshape(T, D))

    g_mix_pre, g_mix_post, g_ffn_pre, g_ffn_post = [None] * DEPTH, [None] * DEPTH, [None] * DEPTH, [None] * DEPTH
    g_ffn_dw, g_ffn_dwb = [None] * DEPTH, [None] * DEPTH
    g_cbin, g_cdw, g_cdwb, g_lng, g_lnb, g_cbout = ([None] * N_A for _ in range(6))
    g_kvn = dkv = None
    landed = [{} for _ in range(DEPTH)]
    in_flight = token = None
    for stage in reversed(stages):
        part, i = stage
        L, W = sv[i], Ws[stage]
        gain = ffn_post_g[i] if part == "ffn" else mix_post_g[i]
        if token is not None:
            gain = gain + token[0, 0]
        send = {}
        if part == "ffn":
            dyf, g_ffn_post[i], _ = resid_norm_bwd(f"ffn_post_bwd_{i}", dx, L["yf"], gain)
            dz = mm_nt(f"ffn_out_bwd_{i}", dyf, W["fout"], 0, MM)
            send["fout"] = mm_tn(f"ffn_out_wg_{i}", L["z"], dyf).reshape(N_DEV, nf // 2, D)
            duf, ddw, ddwb = ffn_conv_gate_bwd(f"ffn_conv_bwd_{i}", L["uf"], dz, DWf, DWBf, i, seq)
            g_ffn_dw[i], g_ffn_dwb[i] = ddw.reshape(N_DEV, FFN_CONV_W, nf), ddwb.reshape(-1)
            send["fin"] = mm_tn(f"ffn_in_wg_{i}", L["hf"][None], duf)
            dx, g_ffn_pre[i] = mm_nt_norm_bwd(f"ffn_in_bwd_{i}", duf, W["fin"], 0, L["x1"], ffn_pre_g[i], dx)
        else:
            dy, g_mix_post[i], dyb = resid_norm_bwd(f"mix_post_bwd_{i}", dx, L["y"], gain)
            if i >= N_A:
                dm = mm_nt(f"attn_out_bwd_{i}", dy, W["o"], 0, F32)
                send["o"] = mm_tn(f"attn_out_wg_{i}", L["mg"], dy).reshape(N_DEV, D // N_DEV, D)
                dq, dkv = attn_bwd(f"attn_bwd_{i}", L["q"], kv, dm, L["mg"], L["lse"], dkv, nb, seq)
                send["q"] = mm_tn(f"attn_q_wg_{i}", L["h"][None], dq)
                dx, g_mix_pre[i] = mm_nt_norm_bwd(f"attn_q_bwd_{i}", dq, W["q"], 0, L["x_in"], mix_pre_g[i], dx)
                if i == N_A:
                    dkv_all = jnp.concatenate(dkv, axis=0)
                    send["kv"] = mm_tn("kv_wg", hkv[None], dkv_all)
                    dx, g_kvn = mm_nt_norm_bwd("kv_bwd", dkv_all, Ws[("ffn", N_A - 1)]["kv"], 0, L["x_in"],
                                               kv_norm_g, dx)
            else:
                g_cbout[i] = dyb
                ds = mm_nt(f"cm_out_bwd_{i}", dy, W["cout"], 0, F32)
                send["cout"] = mm_tn(f"cm_out_wg_{i}", L["s"], dy).reshape(N_DEV, D // N_DEV, D)
                dc, g_lng[i], g_lnb[i] = ln_silu_bwd(f"cm_ln_bwd_{i}", ds, L["c"], LNg[i], LNb[i])
                du, g_cdw[i], g_cdwb[i], dbi = cm_glu_conv_bwd(f"cm_conv_bwd_{i}", L["u"], dc, DWc, i, seq)
                g_cbin[i] = dbi.reshape(N_DEV, -1)
                send["cin"] = mm_tn(f"cm_in_wg_{i}", L["h"][None], du)
                dx, g_mix_pre[i] = mm_nt_norm_bwd(f"cm_in_bwd_{i}", du, W["cin"], 0, L["x_in"], mix_pre_g[i], dx)
        if in_flight is not None:
            (p, j), names, handle = in_flight
            landed[j].update(zip(names, exchange_end(f"scatter_end_{p}{j}", handle, dx)))
        names, arrays = list(send), list(send.values())
        lands = place_own(f"scatter_own_{part}{i}", "scatter", arrays)
        handle, token = exchange_begin(f"scatter_begin_{part}{i}", "scatter", arrays, lands, dx)
        in_flight = (stage, names, handle)
    (p, j), names, handle = in_flight
    landed[j].update(zip(names, exchange_end(f"scatter_end_{p}{j}", handle, dx)))
    grad_x = dx.reshape(nb, seq, D)

    def big_update(name, w, m, v, key, layers):
        as3 = lambda t: t.reshape((-1,) + t.shape[-2:])
        outs = adamw_sum(name, as3(w), as3(m), as3(v), [landed[i][key] for i in layers])
        return [t.reshape(w.shape) for t in outs]

    conf, attn = range(N_A), range(N_A, DEPTH)
    upd = {
        "cm_w_in": big_update("adam_cm_w_in", cm_w_in, m_cm_w_in, v_cm_w_in, "cin", conf),
        "w_kv": big_update("adam_w_kv", w_kv, m_w_kv, v_w_kv, "kv", [N_A]),
        "w_q": big_update("adam_w_q", w_q, m_w_q, v_w_q, "q", attn),
        "ffn_w_in": big_update("adam_ffn_w_in", ffn_w_in, m_ffn_w_in, v_ffn_w_in, "fin", range(DEPTH)),
        "cm_w_out": big_update("adam_cm_w_out", cm_w_out, m_cm_w_out, v_cm_w_out, "cout", conf),
        "w_o": big_update("adam_w_o", w_o, m_w_o, v_w_o, "o", attn),
        "ffn_w_out": big_update("adam_ffn_w_out", ffn_w_out, m_ffn_w_out, v_ffn_w_out, "fout", range(DEPTH)),
    }

    rep_names = ["mix_pre_g", "mix_post_g", "ffn_pre_g", "ffn_post_g", "kv_norm_g", "ffn_dw_b"]
    rep_parts = [jnp.concatenate(g_mix_pre), jnp.concatenate(g_mix_post), jnp.concatenate(g_ffn_pre),
                 jnp.concatenate(g_ffn_post), g_kvn.reshape(-1), jnp.stack(g_ffn_dwb)]
    rep_w = [mix_pre_g, mix_post_g, ffn_pre_g, ffn_post_g, kv_norm_g, ffn_dw_b]
    rep_m = [m_mix_pre_g, m_mix_post_g, m_ffn_pre_g, m_ffn_post_g, m_kv_norm_g, m_ffn_dw_b]
    rep_v = [v_mix_pre_g, v_mix_post_g, v_ffn_pre_g, v_ffn_post_g, v_kv_norm_g, v_ffn_dw_b]
    sh_names = ["ffn_dw", "cm_b_in", "cm_dw", "cm_dw_b", "cm_ln_g", "cm_ln_b", "cm_b_out"]
    own = lambda per_layer, shard: jnp.stack([p.reshape((N_DEV,) + shard) for p in per_layer], axis=1)
    sh_parts = [own(g_ffn_dw, ffn_dw.shape[1:]), own(g_cbin, cm_b_in.shape[1:]), own(g_cdw, cm_dw.shape[1:]),
                own(g_cdwb, cm_dw_b.shape[1:]), own(g_lng, cm_ln_g.shape[1:]), own(g_lnb, cm_ln_b.shape[1:]),
                own(g_cbout, cm_b_out.shape[1:])]
    sh_w = [ffn_dw, cm_b_in, cm_dw, cm_dw_b, cm_ln_g, cm_ln_b, cm_b_out]
    sh_m = [m_ffn_dw, m_cm_b_in, m_cm_dw, m_cm_dw_b, m_cm_ln_g, m_cm_ln_b, m_cm_b_out]
    sh_v = [v_ffn_dw, v_cm_b_in, v_cm_dw, v_cm_dw_b, v_cm_ln_g, v_cm_ln_b, v_cm_b_out]
    rep_pack = _pack([loss_part] + rep_parts)
    sh_packs = [_pack([p[k] for p in sh_parts]) for k in range(N_DEV)]
    n_rep, n_sh = rep_pack.shape[0], sh_packs[0].shape[0]
    (gathered,) = all_gather("gather_small_grads", [jnp.concatenate([rep_pack] + sh_packs)], [False])
    total = sum_partials("sum_small_grads", gathered)
    rep_sum = total[:n_rep]
    sh_sum = lax.dynamic_slice_in_dim(total, n_rep + me * n_sh, n_sh, axis=0)
    rep_shapes = [(1, 1)] + [w.shape for w in rep_w]
    sh_shapes = [w.shape for w in sh_w]
    g_small = jnp.concatenate([rep_sum, sh_sum])
    pad1 = jnp.zeros((1, 1), F32)
    d_s, m_s, v_s = adamw_small("adam_small", jnp.concatenate([_pack([pad1] + rep_w), _pack(sh_w)]), g_small,
                                jnp.concatenate([_pack([pad1] + rep_m), _pack(sh_m)]),
                                jnp.concatenate([_pack([pad1] + rep_v), _pack(sh_v)]))
    split = lambda t: (_unpack(t[:n_rep], rep_shapes), _unpack(t[n_rep:], sh_shapes))
    for (rep_t, sh_t), slot in zip([split(g_small), split(d_s), split(m_s), split(v_s)], range(4)):
        if slot == 0:
            loss = rep_t[0].reshape(())
        for name, t in zip(rep_names, rep_t[1:]):
            upd.setdefault(name, [None] * 4)[slot] = t
        for name, t in zip(sh_names, sh_t):
            upd.setdefault(name, [None] * 4)[slot] = t

    order = ["mix_pre_g", "mix_post_g", "ffn_pre_g", "ffn_post_g", "cm_w_in", "cm_b_in", "cm_dw", "cm_dw_b", "cm_ln_g",
             "cm_ln_b", "cm_w_out", "cm_b_out", "kv_norm_g", "w_kv", "w_q", "w_o", "ffn_w_in", "ffn_dw", "ffn_dw_b",
             "ffn_w_out"]
    return (loss, grad_x, *[upd[n][0] for n in order], *[upd[n][1] for n in order],
            *[upd[n][2] for n in order], *[upd[n][3] for n in order])
```

```python
import functools

import jax
import jax.numpy as jnp
from jax import lax
from jax.experimental import pallas as pl
from jax.experimental.pallas import tpu as pltpu

N_DEV = 8
N_A = 2
DEPTH = 4
N_HEADS = 8
N_GROUPS = 3
DILATIONS = (1, 4, 16)
ATT_BLOCK = 128
CONV_W = 31
FFN_CONV_W = 3
CONV_HALO = 32
FFN_HALO = 16
ROW_CHUNK_FWD = 16
ROW_CHUNK_BWD = 8
EPS = 1e-6
NEG = -1e30
ADAM_LR, ADAM_B1, ADAM_B2, ADAM_EPS, ADAM_WD, ADAM_STEP = 0.001, 0.9, 0.999, 1e-08, 0.01, 10
MM = jnp.bfloat16
F32 = jnp.float32
VMEM_LIMIT_BYTES = 56 * 1024 * 1024
PACK = 1024
MESH_ID = pl.DeviceIdType.MESH

_pallas = pl.pallas_call


def _call(body, *, name, out_shape, grid=(), in_specs=None, out_specs=None, scratch=()):
    return _pallas(body, name=name, out_shape=out_shape, grid=grid, in_specs=in_specs, out_specs=out_specs,
                   scratch_shapes=list(scratch),
                   compiler_params=pltpu.CompilerParams(vmem_limit_bytes=VMEM_LIMIT_BYTES))


def _tile(n, pref):
    if n <= pref:
        return n
    t = pref - pref % 8
    while n % t:
        t -= 8
    assert t > 0, (n, pref)
    return t


def _sds(shape, dtype):
    return jax.ShapeDtypeStruct(tuple(shape), dtype)


def _dot(a, b):
    return jnp.dot(a, b, preferred_element_type=F32)


def _dot_nt(a, b):
    return lax.dot_general(a, b, (((1,), (1,)), ((), ())), preferred_element_type=F32)


def _dot_tn(a, b):
    return lax.dot_general(a, b, (((0,), (0,)), ((), ())), preferred_element_type=F32)


def _sigmoid(x):
    return 1.0 / (1.0 + jnp.exp(-x))


def _my_index():
    return 4 * lax.axis_index("x") + 2 * lax.axis_index("y") + lax.axis_index("c")


def _exchange(name, arrays, out_shapes, pieces, src_of, dst_of):
    n = len(arrays)
    base = [sum(pieces[:a]) for a in range(n)]
    total = sum(pieces)

    def body(*refs):
        ins, outs = refs[:n], refs[n:2 * n]
        send_sems, recv_sems, local_sems = refs[2 * n:]
        x, y, c = lax.axis_index("x"), lax.axis_index("y"), lax.axis_index("c")
        me = 4 * x + 2 * y + c
        copies = []
        for a in range(n):
            for k, (s, d) in enumerate(zip(src_of(a, ins[a], me), dst_of(a, outs[a], me))):
                cp = pltpu.make_async_copy(s, d, local_sems.at[base[a] + k])
                cp.start()
                copies.append(cp)
        remote = []
        for m in range(1, N_DEV):
            px, py, pc = x ^ (m >> 2), y ^ ((m >> 1) & 1), c ^ (m & 1)
            peer = 4 * px + 2 * py + pc
            for a in range(n):
                for k, (s, d) in enumerate(zip(src_of(a, ins[a], peer), dst_of(a, outs[a], me))):
                    cp = pltpu.make_async_remote_copy(src_ref=s, dst_ref=d, send_sem=send_sems.at[base[a] + k, m - 1],
                                                      recv_sem=recv_sems.at[base[a] + k, m - 1],
                                                      device_id=(px, py, pc), device_id_type=MESH_ID)
                    cp.start()
                    remote.append(cp)
        for cp in copies:
            cp.wait()
        for cp in remote:
            cp.wait_send()
        for m in range(1, N_DEV):
            px, py, pc = x ^ (m >> 2), y ^ ((m >> 1) & 1), c ^ (m & 1)
            peer = 4 * px + 2 * py + pc
            for a in range(n):
                for k, (s, d) in enumerate(zip(src_of(a, ins[a], me), dst_of(a, outs[a], peer))):
                    pltpu.make_async_remote_copy(src_ref=s, dst_ref=d, send_sem=send_sems.at[base[a] + k, m - 1],
                                                 recv_sem=recv_sems.at[base[a] + k, m - 1], device_id=(px, py, pc),
                                                 device_id_type=MESH_ID).wait_recv()

    any_spec = pl.BlockSpec(memory_space=pl.ANY)
    return _call(body, name=name, out_shape=[_sds(s, a.dtype) for s, a in zip(out_shapes, arrays)],
                 in_specs=[any_spec] * n, out_specs=[any_spec] * n,
                 scratch=[pltpu.SemaphoreType.DMA((total, N_DEV - 1)), pltpu.SemaphoreType.DMA((total, N_DEV - 1)),
                          pltpu.SemaphoreType.DMA((total,))])(*arrays)


def all_gather(name, arrays, row_sharded):
    def out_shape(a):
        s = arrays[a].shape
        return (s[0], N_DEV) + s[1:] if row_sharded[a] else (N_DEV,) + s

    def src_of(a, ref, peer):
        if row_sharded[a]:
            return [ref.at[l] for l in range(arrays[a].shape[0])]
        return [ref]

    def dst_of(a, ref, me):
        if row_sharded[a]:
            return [ref.at[l, me] for l in range(arrays[a].shape[0])]
        return [ref.at[me]]

    pieces = [arrays[a].shape[0] if row_sharded[a] else 1 for a in range(len(arrays))]
    return _exchange(name, arrays, [out_shape(a) for a in range(len(arrays))], pieces, src_of, dst_of)


def _src_view(kind, ref, peer):
    return ref if kind == "gather" else ref.at[peer]


def _peers(x, y, c):
    for m in range(1, N_DEV):
        px, py, pc = x ^ (m >> 2), y ^ ((m >> 1) & 1), c ^ (m & 1)
        yield m - 1, (px, py, pc), 4 * px + 2 * py + pc


def place_own(name, kind, srcs):
    n = len(srcs)
    shapes = [(N_DEV,) + s.shape if kind == "gather" else s.shape for s in srcs]
    steps = 2

    def body(*refs):
        for a in range(n):
            refs[n + a][...] = refs[a][...]

    def spec(shape, own_block):
        R, C = shape[-2:]
        tr = R // steps
        if own_block:
            return pl.BlockSpec((None, tr, C), lambda i: (_my_index(), i, 0))
        return pl.BlockSpec((tr, C), lambda i: (i, 0))

    return _call(body, name=name, grid=(steps,), out_shape=[_sds(s, a.dtype) for s, a in zip(shapes, srcs)],
                 in_specs=[spec(s.shape, kind == "scatter") for s in srcs],
                 out_specs=[spec(s, True) for s in shapes])(*srcs)


_HBM_SPEC = pl.BlockSpec(memory_space=pltpu.HBM)
_SEM_SPEC = pl.BlockSpec(memory_space=pltpu.SEMAPHORE)
_DATAFLOW = pltpu.SideEffectType.DATAFLOW_SIDE_EFFECTING


def _remote(kind, src, land, send_sems, recv_sems, a, slot, frm, to_id, at):
    return pltpu.make_async_remote_copy(src_ref=_src_view(kind, src, frm), dst_ref=land.at[at],
                                        send_sem=send_sems.at[a * (N_DEV - 1) + slot],
                                        recv_sem=recv_sems.at[a * (N_DEV - 1) + slot],
                                        device_id=to_id, device_id_type=MESH_ID)


def exchange_begin(name, kind, srcs, lands, after):
    n = len(srcs)

    def body(*refs):
        ins, lnd = refs[:n], refs[n:2 * n]
        send_sems, recv_sems = refs[2 * n + 1], refs[2 * n + 2]
        token = refs[-1]
        x, y, c = lax.axis_index("x"), lax.axis_index("y"), lax.axis_index("c")
        me = 4 * x + 2 * y + c
        for slot, peer_id, peer in _peers(x, y, c):
            for a in range(n):
                _remote(kind, ins[a], lnd[a], send_sems, recv_sems, a, slot, peer, peer_id, me).start()
        token[...] = jnp.zeros_like(token)

    hbm = lambda t: pltpu.HBM(t.shape, t.dtype)
    outs = _pallas(
        body, name=name,
        out_shape=(pltpu.SemaphoreType.DMA((n * (N_DEV - 1),)), pltpu.SemaphoreType.DMA((n * (N_DEV - 1),)),
                   *[hbm(t) for t in srcs], *[hbm(t) for t in lands], _sds((8, 128), F32)),
        in_specs=[_HBM_SPEC] * (2 * n) + [pl.BlockSpec(memory_space=pl.ANY)],
        out_specs=(_SEM_SPEC, _SEM_SPEC, *[_HBM_SPEC] * (2 * n), pl.BlockSpec(memory_space=pltpu.VMEM)),
        input_output_aliases={i: 2 + i for i in range(2 * n)},
        compiler_params=pltpu.CompilerParams(has_side_effects=_DATAFLOW),
    )(*[pltpu.with_memory_space_constraint(t, pltpu.HBM) for t in list(srcs) + list(lands)], after)
    return (kind, outs[0], outs[1], list(outs[2:2 + n]), list(outs[2 + n:2 + 2 * n])), outs[-1]


def exchange_end(name, handle, after):
    kind, send_sems, recv_sems, srcs, lands = handle
    n = len(srcs)

    def body(*refs):
        ins, lnd = refs[:n], refs[n:2 * n]
        s_sems, r_sems = refs[2 * n], refs[2 * n + 1]
        x, y, c = lax.axis_index("x"), lax.axis_index("y"), lax.axis_index("c")
        me = 4 * x + 2 * y + c
        for slot, peer_id, peer in _peers(x, y, c):
            for a in range(n):
                _remote(kind, ins[a], lnd[a], s_sems, r_sems, a, slot, peer, peer_id, me).wait_send()
        for slot, peer_id, peer in _peers(x, y, c):
            for a in range(n):
                _remote(kind, ins[a], lnd[a], s_sems, r_sems, a, slot, me, peer_id, peer).wait_recv()

    hbm = lambda t: pltpu.HBM(t.shape, t.dtype)
    outs = _pallas(
        body, name=name, out_shape=tuple(hbm(t) for t in srcs + lands),
        in_specs=[_HBM_SPEC] * (2 * n) + [_SEM_SPEC, _SEM_SPEC, pl.BlockSpec(memory_space=pl.ANY)],
        out_specs=tuple([_HBM_SPEC] * (2 * n)), input_output_aliases={i: i for i in range(2 * n)},
        compiler_params=pltpu.CompilerParams(has_side_effects=_DATAFLOW),
    )(*srcs, *lands, send_sems, recv_sems, after)
    return list(outs[n:])


def norm_mm(name, x, gain, w, layer, bias=None):
    T, D = x.shape
    nsh, _, _, n = w.shape
    tm = _tile(T, 2048)

    def body(*refs):
        if bias is None:
            x_ref, g_ref, w_ref, u_ref, h_ref = refs
        else:
            x_ref, g_ref, w_ref, b_ref, u_ref, h_ref = refs

        @pl.when(pl.program_id(1) == 0)
        def _():
            xf = x_ref[...]
            r = lax.rsqrt(jnp.mean(xf * xf, axis=-1, keepdims=True) + EPS)
            h_ref[...] = (xf * r * g_ref[...]).astype(h_ref.dtype)

        acc = _dot(h_ref[...], w_ref[...])
        if bias is not None:
            acc = acc + b_ref[...]
        u_ref[...] = acc.astype(u_ref.dtype)

    in_specs = [pl.BlockSpec((tm, D), lambda i, j: (i, 0)),
                pl.BlockSpec((1, D), lambda i, j: (0, 0)),
                pl.BlockSpec((None, None, D, n), lambda i, j: (j, layer, 0, 0))]
    args = [x, gain.reshape(1, D), w]
    if bias is not None:
        in_specs.append(pl.BlockSpec((None, None, 1, n), lambda i, j: (j, layer, 0, 0)))
        args.append(bias)
    return _call(body, name=name, grid=(T // tm, nsh), in_specs=in_specs,
                 out_specs=[pl.BlockSpec((None, tm, n), lambda i, j: (j, i, 0)),
                            pl.BlockSpec((tm, D), lambda i, j: (i, 0))],
                 out_shape=[_sds((nsh, T, n), MM), _sds((T, D), MM)])(*args)


def mm_resid_norm(name, a, w, layer, bias, x, gain):
    nk, T, kk = a.shape
    D = x.shape[1]
    tm = _tile(T, 512)

    def body(a_ref, w_ref, b_ref, x_ref, g_ref, y_ref, xn_ref):
        y = _dot(a_ref[0], w_ref[0])
        for q in range(1, nk):
            y = y + _dot(a_ref[q], w_ref[q])
        y = y + b_ref[...]
        y_ref[...] = y
        r = lax.rsqrt(jnp.mean(y * y, axis=-1, keepdims=True) + EPS)
        xn_ref[...] = x_ref[...] + y * r * g_ref[...]

    return _call(body, name=name, grid=(T // tm,),
                 in_specs=[pl.BlockSpec((nk, tm, kk), lambda i: (0, i, 0)),
                           pl.BlockSpec((None, nk, kk, D), lambda i: (layer, 0, 0, 0)),
                           pl.BlockSpec((1, D), lambda i: (0, 0)),
                           pl.BlockSpec((tm, D), lambda i: (i, 0)),
                           pl.BlockSpec((1, D), lambda i: (0, 0))],
                 out_specs=[pl.BlockSpec((tm, D), lambda i: (i, 0))] * 2,
                 out_shape=[_sds((T, D), F32)] * 2)(a, w, bias.reshape(1, D), x, gain.reshape(1, D))


def _halo_maps(tm, hb, T):
    per = tm // hb
    last = T // hb - 1
    return (lambda i: jnp.maximum(i * per - 1, 0)), (lambda i: jnp.minimum((i + 1) * per, last))


def cm_glu_conv(name, u, dw, dwb, layer, seq):
    _, T, n = u.shape
    ct = dw.shape[-1]
    per = n // ct
    nct = 4 * per
    tm = _tile(seq, 512)
    tps = seq // tm
    hb = CONV_HALO
    prev, _ = _halo_maps(tm, hb, T)
    u4 = u.reshape(2, 4, T, n)

    def body(u_ref, uh_ref, w_ref, b_ref, o_ref, pad_ref):
        first = (pl.program_id(0) % tps) == 0
        um = u_ref[...].astype(F32)
        uh = uh_ref[...].astype(F32)
        pad_ref[pl.ds(hb, tm), :] = um[0] * _sigmoid(um[1])
        pad_ref[pl.ds(0, hb), :] = jnp.where(first, 0.0, uh[0] * _sigmoid(uh[1]))
        acc = jnp.zeros((tm, ct), F32) + b_ref[...]
        for k in range(CONV_W):
            acc = acc + w_ref[pl.ds(k, 1), :] * pad_ref[pl.ds(hb - (CONV_W - 1) + k, tm), :]
        o_ref[...] = acc

    return _call(body, name=name, grid=(T // tm, nct),
                 in_specs=[pl.BlockSpec((2, None, tm, ct), lambda i, c: (0, c // per, i, c % per)),
                           pl.BlockSpec((2, None, hb, ct), lambda i, c: (0, c // per, prev(i), c % per)),
                           pl.BlockSpec((None, None, CONV_W, ct), lambda i, c: (c, layer, 0, 0)),
                           pl.BlockSpec((None, None, 1, ct), lambda i, c: (c, layer, 0, 0))],
                 out_specs=pl.BlockSpec((tm, ct), lambda i, c: (i, c)),
                 out_shape=_sds((T, nct * ct), F32),
                 scratch=[pltpu.VMEM((tm + hb, ct), F32)])(u4, u4, dw, dwb)


def ln_silu(name, c, g, b):
    T, D = c.shape
    tm = _tile(T, 512)

    def body(c_ref, g_ref, b_ref, s_ref):
        cf = c_ref[...]
        mu = jnp.mean(cf, axis=-1, keepdims=True)
        xc = cf - mu
        r = lax.rsqrt(jnp.mean(xc * xc, axis=-1, keepdims=True) + EPS)
        t = xc * r * g_ref[...] + b_ref[...]
        s_ref[...] = (t * _sigmoid(t)).astype(s_ref.dtype)

    return _call(body, name=name, grid=(T // tm,),
                 in_specs=[pl.BlockSpec((tm, D), lambda i: (i, 0)), pl.BlockSpec((1, D), lambda i: (0, 0)),
                           pl.BlockSpec((1, D), lambda i: (0, 0))],
                 out_specs=pl.BlockSpec((None, tm, D), lambda i: (0, i, 0)),
                 out_shape=_sds((1, T, D), MM))(c, g.reshape(1, D), b.reshape(1, D))


def ffn_conv_gate(name, u, dw, dwb, layer, seq):
    _, T, n = u.shape
    tm = _tile(seq, 512)
    tps = seq // tm
    hb = FFN_HALO
    prev, _ = _halo_maps(tm, hb, T)
    u4 = u.reshape(2, 4, T, n)
    dw4 = dw.reshape(2, 4, dw.shape[1], FFN_CONV_W, n)
    dwb4 = dwb.reshape(dwb.shape[0], 2, 4, 1, n)

    rows = _tile(tm, ROW_CHUNK_FWD)

    def body(u_ref, uh_ref, w_ref, b_ref, z_ref, ag_ref, pad_ref):
        first = (pl.program_id(0) % tps) == 0
        for half in range(2):
            pad_ref[half, pl.ds(hb, tm), :] = u_ref[half].astype(F32)
            pad_ref[half, pl.ds(0, hb), :] = jnp.where(first, 0.0, uh_ref[half].astype(F32))
        for r0 in range(0, tm, rows):
            a, g = (_conv_rows(pad_ref, w_ref, b_ref, half, hb + r0, rows) for half in range(2))
            z_ref[pl.ds(r0, rows), :] = (g * _sigmoid(g) * a).astype(z_ref.dtype)
            ag_ref[0, pl.ds(r0, rows), :] = a.astype(ag_ref.dtype)
            ag_ref[1, pl.ds(r0, rows), :] = g.astype(ag_ref.dtype)

    return _call(body, name=name, grid=(T // tm, 4),
                 in_specs=[pl.BlockSpec((2, None, tm, n), lambda i, j: (0, j, i, 0)),
                           pl.BlockSpec((2, None, hb, n), lambda i, j: (0, j, prev(i), 0)),
                           pl.BlockSpec((2, None, None, FFN_CONV_W, n), lambda i, j: (0, j, layer, 0, 0)),
                           pl.BlockSpec((None, 2, None, 1, n), lambda i, j: (layer, 0, j, 0, 0))],
                 out_specs=[pl.BlockSpec((None, tm, n), lambda i, j: (j, i, 0)),
                            pl.BlockSpec((2, None, tm, n), lambda i, j: (0, j, i, 0))],
                 out_shape=[_sds((4, T, n), MM), _sds((2, 4, T, n), MM)],
                 scratch=[pltpu.VMEM((2, tm + hb, n), F32)])(u4, u4, dw4, dwb4)


def _conv_rows(pad_ref, w_ref, b_ref, half, at, rows):
    acc = b_ref[half] + w_ref[half, pl.ds(FFN_CONV_W - 1, 1), :] * pad_ref[half, pl.ds(at, rows), :]
    for k in range(FFN_CONV_W - 1):
        acc = acc + w_ref[half, pl.ds(k, 1), :] * pad_ref[half, pl.ds(at - (FFN_CONV_W - 1) + k, rows), :]
    return acc


def _head_specs(seq, dh, q_heads, kv_heads):
    def spec(per, base):
        def imap(b, h, g):
            f = base + g * N_HEADS + h
            return (f // per, b, f % per)
        return pl.BlockSpec((None, seq, dh), imap)
    return spec(q_heads, 0), spec(kv_heads, 0), spec(kv_heads, N_GROUPS * N_HEADS)


def _rows(start, d, blocks=1):
    size = blocks * ATT_BLOCK
    return pl.ds(start, size, stride=d) if d > 1 else pl.ds(start, size)


def _att_pad():
    return max(ATT_BLOCK * d for d in DILATIONS[:-1])


def _band_mask(n, nblk):
    if nblk == 1:
        qi = lax.broadcasted_iota(jnp.int32, (ATT_BLOCK, ATT_BLOCK), 0)
        kj = lax.broadcasted_iota(jnp.int32, (ATT_BLOCK, ATT_BLOCK), 1)
        return kj <= qi
    qi = lax.broadcasted_iota(jnp.int32, (ATT_BLOCK, 2 * ATT_BLOCK), 0)
    kj = lax.broadcasted_iota(jnp.int32, (ATT_BLOCK, 2 * ATT_BLOCK), 1)
    return (kj >= qi) & (kj <= qi + ATT_BLOCK) & ((n > 0) | (kj >= ATT_BLOCK))


def _band_rows(r, n, d, nblk):
    if nblk == 1:
        return _rows(_att_pad() + r + d * ATT_BLOCK * n, d)
    return _rows(_att_pad() + r + d * ATT_BLOCK * (n - 1), d, blocks=2)


def attn_fwd(name, q, kv, nb, seq):
    _, T, qn = q.shape
    dh = qn * N_DEV // (N_GROUPS * N_HEADS)
    scale = 1.0 / (dh ** 0.5)
    qs, ks, vs = _head_specs(seq, dh, qn // dh, kv.shape[2] // dh)

    def body(q_ref, k_ref, v_ref, m_ref, l_ref, qf, kf, vf, *branch):
        og, lg = branch[:N_GROUPS], branch[N_GROUPS:]
        pad = _att_pad()
        qf[...] = q_ref[...].astype(F32)
        for t_ref, s_ref in ((k_ref, kf), (v_ref, vf)):
            s_ref[pl.ds(0, pad), :] = jnp.zeros((pad, dh), F32)
            s_ref[pl.ds(pad, seq), :] = t_ref[...].astype(F32)
        for g in range(N_GROUPS):
            d = DILATIONS[g]
            nblk = seq // d // ATT_BLOCK

            def block(idx, carry, d=d, nblk=nblk, g=g):
                r, n = idx // nblk, idx % nblk
                rq = _rows(r + d * ATT_BLOCK * n, d)
                rk = _band_rows(r, n, d, nblk)
                s = jnp.where(_band_mask(n, nblk), _dot_nt(qf[rq, :].astype(MM), kf[rk, :].astype(MM)) * scale, NEG)
                m = jnp.max(s, axis=-1, keepdims=True)
                p = jnp.exp(s - m)
                den = jnp.sum(p, axis=-1, keepdims=True)
                og[g][rq, :] = _dot(p.astype(MM), vf[rk, :].astype(MM)) / den
                lg[g][rq, :] = jnp.broadcast_to(m + jnp.log(den), (ATT_BLOCK, dh))
                return carry

            @pl.when(pl.program_id(2) == g)
            def _(block=block, d=d, nblk=nblk):
                lax.fori_loop(0, d * nblk, block, 0, unroll=2)

        @pl.when(pl.program_id(2) == N_GROUPS - 1)
        def _():
            mx = jnp.maximum(jnp.maximum(lg[0][...], lg[1][...]), lg[2][...])
            e = [jnp.exp(lg[g][...] - mx) for g in range(N_GROUPS)]
            tot = e[0] + e[1] + e[2]
            m_ref[...] = ((e[0] * og[0][...] + e[1] * og[1][...] + e[2] * og[2][...]) / tot).astype(m_ref.dtype)
            l_ref[...] = mx + jnp.log(tot)

    return _call(body, name=name, grid=(nb, N_HEADS, N_GROUPS), in_specs=[qs, ks, vs],
                 out_specs=[pl.BlockSpec((None, seq, dh), lambda b, h, g: (0, b, h)),
                            pl.BlockSpec((seq, dh), lambda b, h, g: (b, h))],
                 out_shape=[_sds((1, T, N_HEADS * dh), MM), _sds((T, N_HEADS * dh), F32)],
                 scratch=[pltpu.VMEM((seq, dh), F32)] + [pltpu.VMEM((_att_pad() + seq, dh), F32)] * 2
                 + [pltpu.VMEM((seq, dh), F32)] * (2 * N_GROUPS))(q, kv, kv)


def loss_fwd_bwd(name, x, target):
    T, D = x.shape
    tm = _tile(T, 512)

    def body(x_ref, t_ref, dx_ref, l_ref):
        @pl.when(pl.program_id(0) == 0)
        def _():
            l_ref[...] = jnp.zeros_like(l_ref)
        err = x_ref[...] - t_ref[...]
        dx_ref[...] = err * (1.0 / D)
        l_ref[...] += 0.5 * jnp.sum(jnp.mean(err * err, axis=-1, keepdims=True), axis=0, keepdims=True)

    dx, l = _call(body, name=name, grid=(T // tm,),
                  in_specs=[pl.BlockSpec((tm, D), lambda i: (i, 0))] * 2,
                  out_specs=[pl.BlockSpec((tm, D), lambda i: (i, 0)), pl.BlockSpec((1, 1), lambda i: (0, 0))],
                  out_shape=[_sds((T, D), F32), _sds((1, 1), F32)])(x, target)
    return dx, l


def resid_norm_bwd(name, dx, y, gain):
    T, D = y.shape
    tm = _tile(T, 512)

    def body(dx_ref, y_ref, g_ref, dy_ref, dg_ref, db_ref):
        @pl.when(pl.program_id(0) == 0)
        def _():
            dg_ref[...] = jnp.zeros_like(dg_ref)
            db_ref[...] = jnp.zeros_like(db_ref)
        y = y_ref[...]
        d = dx_ref[...]
        r = lax.rsqrt(jnp.mean(y * y, axis=-1, keepdims=True) + EPS)
        yh = y * r
        dyh = d * g_ref[...]
        dy = r * (dyh - yh * jnp.mean(dyh * yh, axis=-1, keepdims=True))
        dy_ref[...] = dy.astype(dy_ref.dtype)
        dg_ref[...] += jnp.sum(d * yh, axis=0, keepdims=True)
        db_ref[...] += jnp.sum(dy, axis=0, keepdims=True)

    return _call(body, name=name, grid=(T // tm,),
                 in_specs=[pl.BlockSpec((tm, D), lambda i: (i, 0))] * 2 + [pl.BlockSpec((1, D), lambda i: (0, 0))],
                 out_specs=[pl.BlockSpec((None, tm, D), lambda i: (0, i, 0))] + [pl.BlockSpec((1, D), lambda i: (0, 0))] * 2,
                 out_shape=[_sds((1, T, D), MM), _sds((1, D), F32), _sds((1, D), F32)])(dx, y, gain.reshape(1, D))


def mm_nt(name, dy, w, layer, out_dtype):
    _, T, D = dy.shape
    _, nk, kk, _ = w.shape
    tm = _tile(T, 1024)

    def body(dy_ref, w_ref, o_ref):
        o_ref[...] = _dot_nt(dy_ref[...], w_ref[...]).astype(o_ref.dtype)

    return _call(body, name=name, grid=(T // tm, nk),
                 in_specs=[pl.BlockSpec((None, tm, D), lambda i, q: (0, i, 0)),
                           pl.BlockSpec((None, None, kk, D), lambda i, q: (layer, q, 0, 0))],
                 out_specs=pl.BlockSpec((None, tm, kk), lambda i, q: (q, i, 0)),
                 out_shape=_sds((nk, T, kk), out_dtype))(dy, w)


def mm_nt_norm_bwd(name, du, w, layer, x_in, gain, dx_res):
    nsh, T, n = du.shape
    D = x_in.shape[1]
    tm = _tile(T, 1024)

    def body(du_ref, w_ref, x_ref, g_ref, dr_ref, dx_ref, dg_ref, acc_ref):
        i, j = pl.program_id(0), pl.program_id(1)

        @pl.when((i == 0) & (j == 0))
        def _():
            dg_ref[...] = jnp.zeros_like(dg_ref)

        part = _dot_nt(du_ref[...], w_ref[...])

        @pl.when(j == 0)
        def _():
            acc_ref[...] = part

        @pl.when(j > 0)
        def _():
            acc_ref[...] += part

        @pl.when(j == nsh - 1)
        def _():
            x = x_ref[...]
            dh = acc_ref[...]
            r = lax.rsqrt(jnp.mean(x * x, axis=-1, keepdims=True) + EPS)
            xh = x * r
            dxh = dh * g_ref[...]
            dx_ref[...] = dr_ref[...] + r * (dxh - xh * jnp.mean(dxh * xh, axis=-1, keepdims=True))
            dg_ref[...] += jnp.sum(dh * xh, axis=0, keepdims=True)

    return _call(body, name=name, grid=(T // tm, nsh),
                 in_specs=[pl.BlockSpec((None, tm, n), lambda i, j: (j, i, 0)),
                           pl.BlockSpec((None, None, D, n), lambda i, j: (j, layer, 0, 0)),
                           pl.BlockSpec((tm, D), lambda i, j: (i, 0)),
                           pl.BlockSpec((1, D), lambda i, j: (0, 0)),
                           pl.BlockSpec((tm, D), lambda i, j: (i, 0))],
                 out_specs=[pl.BlockSpec((tm, D), lambda i, j: (i, 0)), pl.BlockSpec((1, D), lambda i, j: (0, 0))],
                 out_shape=[_sds((T, D), F32), _sds((1, D), F32)],
                 scratch=[pltpu.VMEM((tm, D), F32)])(du, w, x_in, gain.reshape(1, D), dx_res)


def mm_tn(name, a, b):
    na, T, ka = a.shape
    nb, _, kb = b.shape
    nj = max(na, nb)

    def body(a_ref, b_ref, o_ref):
        o_ref[...] = _dot_tn(a_ref[...], b_ref[...]).astype(o_ref.dtype)

    return _call(body, name=name, grid=(nj,),
                 in_specs=[pl.BlockSpec((None, T, ka), (lambda j: (j, 0, 0)) if na > 1 else (lambda j: (0, 0, 0))),
                           pl.BlockSpec((None, T, kb), (lambda j: (j, 0, 0)) if nb > 1 else (lambda j: (0, 0, 0)))],
                 out_specs=pl.BlockSpec((None, ka, kb), lambda j: (j, 0, 0)),
                 out_shape=_sds((nj, ka, kb), MM))(a, b)


def ffn_conv_gate_bwd(name, u, ag, dz, dw, layer, seq):
    _, T, n = u.shape
    tm = _tile(seq, 512)
    tps = seq // tm
    hb = FFN_HALO
    _, nxt = _halo_maps(tm, hb, T)
    u4 = u.reshape(2, 4, T, n)
    dw4 = dw.reshape(2, 4, dw.shape[1], FFN_CONV_W, n)
    K = FFN_CONV_W
    te = tm + hb
    rows = _tile(hb, ROW_CHUNK_BWD)

    def body(u_ref, ag_ref, agn_ref, dz_ref, dzn_ref, w_ref, du_ref, ddw_ref, ddb_ref, uf_ref, agf_ref, dzf_ref,
             da_ref):
        i = pl.program_id(1)
        last = (i % tps) == tps - 1

        @pl.when(i == 0)
        def _():
            ddw_ref[...] = jnp.zeros_like(ddw_ref)
            ddb_ref[...] = jnp.zeros_like(ddb_ref)

        for half in range(2):
            uf_ref[half] = u_ref[half].astype(F32)
            agf_ref[half, pl.ds(0, tm), :] = ag_ref[half].astype(F32)
            agf_ref[half, pl.ds(tm, hb), :] = agn_ref[half].astype(F32)
        dzf_ref[pl.ds(0, tm), :] = dz_ref[...].astype(F32)
        dzf_ref[pl.ds(tm, hb), :] = jnp.where(last, 0.0, dzn_ref[...].astype(F32))
        for r0 in range(0, te, rows):
            a, g = agf_ref[0, pl.ds(r0, rows), :], agf_ref[1, pl.ds(r0, rows), :]
            dzc = dzf_ref[pl.ds(r0, rows), :]
            sg = _sigmoid(g)
            da_ref[0, pl.ds(r0, rows), :] = dzc * (g * sg)
            da_ref[1, pl.ds(r0, rows), :] = dzc * a * (sg * (1.0 + g * (1.0 - sg)))
        for half in range(2):
            tap_acc = [jnp.zeros((rows, n), F32) for _ in range(K)]
            bias_acc = jnp.zeros((rows, n), F32)
            for r0 in range(0, tm, rows):
                x = uf_ref[half, pl.ds(r0, rows), :]
                acc = None
                for k in range(K):
                    d = da_ref[half, pl.ds(r0 + K - 1 - k, rows), :]
                    acc = w_ref[half, pl.ds(k, 1), :] * d if acc is None else acc + w_ref[half, pl.ds(k, 1), :] * d
                    tap_acc[k] = tap_acc[k] + d * x
                    if k == K - 1:
                        bias_acc = bias_acc + d
                du_ref[half, pl.ds(r0, rows), :] = acc.astype(du_ref.dtype)
            for k in range(K):
                ddw_ref[half, pl.ds(k, 1), :] += jnp.sum(tap_acc[k], axis=0, keepdims=True)
            ddb_ref[half] += jnp.sum(bias_acc, axis=0, keepdims=True)

    du, ddw, ddb = _call(
        body, name=name, grid=(4, T // tm),
        in_specs=[pl.BlockSpec((2, None, tm, n), lambda j, i: (0, j, i, 0)),
                  pl.BlockSpec((2, None, tm, n), lambda j, i: (0, j, i, 0)),
                  pl.BlockSpec((2, None, hb, n), lambda j, i: (0, j, nxt(i), 0)),
                  pl.BlockSpec((None, tm, n), lambda j, i: (j, i, 0)),
                  pl.BlockSpec((None, hb, n), lambda j, i: (j, nxt(i), 0)),
                  pl.BlockSpec((2, None, None, K, n), lambda j, i: (0, j, layer, 0, 0))],
        out_specs=[pl.BlockSpec((2, None, tm, n), lambda j, i: (0, j, i, 0)),
                   pl.BlockSpec((2, None, K, n), lambda j, i: (0, j, 0, 0)),
                   pl.BlockSpec((2, None, 1, n), lambda j, i: (0, j, 0, 0))],
        out_shape=[_sds((2, 4, T, n), MM), _sds((2, 4, K, n), F32), _sds((2, 4, 1, n), F32)],
        scratch=[pltpu.VMEM((2, tm, n), F32), pltpu.VMEM((2, te, n), F32), pltpu.VMEM((te, n), F32),
                 pltpu.VMEM((2, te, n), F32)],
    )(u4, ag, ag, dz, dz, dw4)
    return du.reshape(8, T, n), ddw, ddb


def ln_silu_bwd(name, ds, c, g, b):
    T, D = c.shape
    tm = _tile(T, 512)

    def body(ds_ref, c_ref, g_ref, b_ref, dc_ref, dg_ref, db_ref):
        @pl.when(pl.program_id(0) == 0)
        def _():
            dg_ref[...] = jnp.zeros_like(dg_ref)
            db_ref[...] = jnp.zeros_like(db_ref)
        cf = c_ref[...]
        mu = jnp.mean(cf, axis=-1, keepdims=True)
        xc = cf - mu
        r = lax.rsqrt(jnp.mean(xc * xc, axis=-1, keepdims=True) + EPS)
        xh = xc * r
        t = xh * g_ref[...] + b_ref[...]
        sg = _sigmoid(t)
        dt = ds_ref[...] * (sg * (1.0 + t * (1.0 - sg)))
        dg_ref[...] += jnp.sum(dt * xh, axis=0, keepdims=True)
        db_ref[...] += jnp.sum(dt, axis=0, keepdims=True)
        dxh = dt * g_ref[...]
        dc_ref[...] = r * (dxh - jnp.mean(dxh, axis=-1, keepdims=True)
                           - xh * jnp.mean(dxh * xh, axis=-1, keepdims=True))

    vec = pl.BlockSpec((1, D), lambda i: (0, 0))
    return _call(body, name=name, grid=(T // tm,),
                 in_specs=[pl.BlockSpec((None, tm, D), lambda i: (0, i, 0)), pl.BlockSpec((tm, D), lambda i: (i, 0)),
                           vec, vec],
                 out_specs=[pl.BlockSpec((tm, D), lambda i: (i, 0)), vec, vec],
                 out_shape=[_sds((T, D), F32), _sds((1, D), F32), _sds((1, D), F32)])(
                     ds, c, g.reshape(1, D), b.reshape(1, D))


def cm_glu_conv_bwd(name, u, dc, dw, layer, seq):
    _, T, n = u.shape
    ct = dw.shape[-1]
    per = n // ct
    nct = 4 * per
    tm = _tile(seq, 512)
    tps = seq // tm
    hb = CONV_HALO
    prev, nxt = _halo_maps(tm, hb, T)
    u4 = u.reshape(2, 4, T, n)
    K = CONV_W

    def body(u_ref, up_ref, dc_ref, dcn_ref, w_ref, du_ref, ddw_ref, ddb_ref, dbi_ref, padp_ref, padd_ref):
        i = pl.program_id(1)
        first = (i % tps) == 0
        last = (i % tps) == tps - 1

        @pl.when(i == 0)
        def _():
            ddw_ref[...] = jnp.zeros_like(ddw_ref)
            ddb_ref[...] = jnp.zeros_like(ddb_ref)
            dbi_ref[...] = jnp.zeros_like(dbi_ref)

        um = u_ref[...].astype(F32)
        uh = up_ref[...].astype(F32)
        sg = _sigmoid(um[1])
        padp_ref[pl.ds(hb, tm), :] = um[0] * sg
        padp_ref[pl.ds(0, hb), :] = jnp.where(first, 0.0, uh[0] * _sigmoid(uh[1]))
        dcm = dc_ref[...]
        padd_ref[pl.ds(0, tm), :] = dcm
        padd_ref[pl.ds(tm, hb), :] = jnp.where(last, 0.0, dcn_ref[...])
        dp = jnp.zeros((tm, ct), F32)
        for k in range(K):
            dp = dp + w_ref[pl.ds(k, 1), :] * padd_ref[pl.ds(K - 1 - k, tm), :]
            ddw_ref[pl.ds(k, 1), :] += jnp.sum(dcm * padp_ref[pl.ds(hb - (K - 1) + k, tm), :], axis=0, keepdims=True)
        ddb_ref[...] += jnp.sum(dcm, axis=0, keepdims=True)
        dv = dp * sg
        dg = dp * um[0] * sg * (1.0 - sg)
        du_ref[0] = dv.astype(du_ref.dtype)
        du_ref[1] = dg.astype(du_ref.dtype)
        dbi_ref[0] += jnp.sum(dv, axis=0, keepdims=True)
        dbi_ref[1] += jnp.sum(dg, axis=0, keepdims=True)

    du, ddw, ddb, dbi = _call(
        body, name=name, grid=(nct, T // tm),
        in_specs=[pl.BlockSpec((2, None, tm, ct), lambda c, i: (0, c // per, i, c % per)),
                  pl.BlockSpec((2, None, hb, ct), lambda c, i: (0, c // per, prev(i), c % per)),
                  pl.BlockSpec((tm, ct), lambda c, i: (i, c)),
                  pl.BlockSpec((hb, ct), lambda c, i: (nxt(i), c)),
                  pl.BlockSpec((None, None, K, ct), lambda c, i: (c, layer, 0, 0))],
        out_specs=[pl.BlockSpec((2, None, tm, ct), lambda c, i: (0, c // per, i, c % per)),
                   pl.BlockSpec((None, K, ct), lambda c, i: (c, 0, 0)),
                   pl.BlockSpec((None, 1, ct), lambda c, i: (c, 0, 0)),
                   pl.BlockSpec((2, None, 1, ct), lambda c, i: (0, c // per, 0, c % per))],
        out_shape=[_sds((2, 4, T, n), MM), _sds((nct, K, ct), F32), _sds((nct, 1, ct), F32), _sds((2, 4, 1, n), F32)],
        scratch=[pltpu.VMEM((tm + hb, ct), F32), pltpu.VMEM((tm + hb, ct), F32)])(u4, u4, dc, dc, dw)
    return du.reshape(8, T, n), ddw, ddb, dbi


def attn_bwd(name, q, kv, dm, merged, lse, dkv_prev, nb, seq):
    _, T, qn = q.shape
    kn = kv.shape[2]
    dh = qn * N_DEV // (N_GROUPS * N_HEADS)
    scale = 1.0 / (dh ** 0.5)
    qs, ks, vs = _head_specs(seq, dh, qn // dh, kn // dh)
    has_prev = dkv_prev is not None
    n_in = 6 + (2 if has_prev else 0)

    def body(*refs):
        q_ref, k_ref, v_ref, dm_ref, mg_ref, l_ref = refs[:6]
        pk_ref, pv_ref = refs[6:8] if has_prev else (None, None)
        dq_ref, dk_ref, dv_ref = refs[n_in:n_in + 3]
        qf, kf, vf, dqf, dkf, dvf, dlt = refs[n_in + 3:]
        pad = _att_pad()
        dlt[...] = jnp.broadcast_to(jnp.sum(dm_ref[...] * mg_ref[...].astype(F32), axis=-1, keepdims=True), (seq, dh))
        qf[...] = q_ref[...].astype(F32)
        for t_ref, s_ref in ((k_ref, kf), (v_ref, vf)):
            s_ref[pl.ds(0, pad), :] = jnp.zeros((pad, dh), F32)
            s_ref[pl.ds(pad, seq), :] = t_ref[...].astype(F32)
        dkf[...] = jnp.zeros_like(dkf)
        dvf[...] = jnp.zeros_like(dvf)
        for g in range(N_GROUPS):
            d = DILATIONS[g]
            nblk = seq // d // ATT_BLOCK

            def block(idx, carry, d=d, nblk=nblk):
                r, n = idx // nblk, idx % nblk
                rq = _rows(r + d * ATT_BLOCK * n, d)
                rk = _band_rows(r, n, d, nblk)
                qb = qf[rq, :].astype(MM)
                dmb = dm_ref[rq, :].astype(MM)
                kb = kf[rk, :].astype(MM)
                s = jnp.where(_band_mask(n, nblk), _dot_nt(qb, kb) * scale, NEG)
                p = jnp.exp(s - l_ref[rq, :][:, :1])
                dvf[rk, :] += _dot_tn(p.astype(MM), dmb)
                dsc = (p * (_dot_nt(dmb, vf[rk, :].astype(MM)) - dlt[rq, :][:, :1]) * scale).astype(MM)
                dkf[rk, :] += _dot_tn(dsc, qb)
                dqf[rq, :] = _dot(dsc, kb)
                return carry

            @pl.when(pl.program_id(2) == g)
            def _(block=block, d=d, nblk=nblk):
                lax.fori_loop(0, d * nblk, block, 0, unroll=2)

        dq_ref[...] = dqf[...].astype(dq_ref.dtype)
        dk, dv = dkf[pl.ds(pad, seq), :], dvf[pl.ds(pad, seq), :]
        if has_prev:
            dk, dv = dk + pk_ref[...].astype(F32), dv + pv_ref[...].astype(F32)
        dk_ref[...] = dk.astype(dk_ref.dtype)
        dv_ref[...] = dv.astype(dv_ref.dtype)

    full = pl.BlockSpec((None, seq, dh), lambda b, h, g: (0, b, h))
    in_specs = [qs, ks, vs, full, full, pl.BlockSpec((seq, dh), lambda b, h, g: (b, h))]
    args = [q, kv, kv, dm, merged, lse]
    if has_prev:
        in_specs += [ks, ks]
        args += [dkv_prev[0], dkv_prev[1]]
    short, padded = pltpu.VMEM((seq, dh), F32), pltpu.VMEM((_att_pad() + seq, dh), F32)
    dq, dk, dv = _call(body, name=name, grid=(nb, N_HEADS, N_GROUPS), in_specs=in_specs, out_specs=[qs, ks, ks],
                       out_shape=[_sds(q.shape, MM), _sds((N_DEV // 2, T, kn), MM), _sds((N_DEV // 2, T, kn), MM)],
                       scratch=[short, padded, padded, short, padded, padded, short])(*args)
    return dq, (dk, dv)


def _adamw_math(w, g, m, v):
    m = ADAM_B1 * m + (1.0 - ADAM_B1) * g
    v = ADAM_B2 * v + (1.0 - ADAM_B2) * (g * g)
    m_hat = m / (1.0 - ADAM_B1 ** ADAM_STEP)
    v_hat = v / (1.0 - ADAM_B2 ** ADAM_STEP)
    delta = -ADAM_LR * (m_hat / (jnp.sqrt(v_hat) + ADAM_EPS) + ADAM_WD * w)
    return delta, m, v


def adamw_sum(name, w, m, v, parts):
    L, R, C = w.shape
    tr = _tile(R, 256)

    def body(*refs):
        w_ref, m_ref, v_ref = refs[:3]
        p_refs = refs[3:3 + L]
        g_ref, d_ref, nm_ref, nv_ref = refs[3 + L:]
        for l in range(L):
            @pl.when(pl.program_id(0) == l)
            def _(p_ref=p_refs[l]):
                g = p_ref[0].astype(F32)
                for k in range(1, N_DEV):
                    g = g + p_ref[k].astype(F32)
                g_ref[...] = g
                d_ref[...], nm_ref[...], nv_ref[...] = _adamw_math(w_ref[...], g, m_ref[...], v_ref[...])

    blk = pl.BlockSpec((None, tr, C), lambda l, i: (l, i, 0))
    part = lambda k: pl.BlockSpec((N_DEV, tr, C), lambda l, i: (0, jnp.where(l == k, i, 0), 0))
    return _call(body, name=name, grid=(L, R // tr), in_specs=[blk, blk, blk] + [part(k) for k in range(L)],
                 out_specs=[blk] * 4, out_shape=[_sds((L, R, C), F32)] * 4)(w, m, v, *parts)


def sum_partials(name, parts):
    _, R, C = parts.shape
    tr = _tile(R, 512)

    def body(p_ref, o_ref):
        g = p_ref[0]
        for k in range(1, N_DEV):
            g = g + p_ref[k]
        o_ref[...] = g

    return _call(body, name=name, grid=(R // tr,),
                 in_specs=[pl.BlockSpec((N_DEV, tr, C), lambda i: (0, i, 0))],
                 out_specs=pl.BlockSpec((tr, C), lambda i: (i, 0)), out_shape=_sds((R, C), F32))(parts)


def adamw_small(name, w, g, m, v):
    R, C = w.shape
    tr = _tile(R, 512)

    def body(w_ref, g_ref, m_ref, v_ref, d_ref, nm_ref, nv_ref):
        d_ref[...], nm_ref[...], nv_ref[...] = _adamw_math(w_ref[...], g_ref[...], m_ref[...], v_ref[...])

    blk = pl.BlockSpec((tr, C), lambda i: (i, 0))
    return _call(body, name=name, grid=(R // tr,), in_specs=[blk] * 4, out_specs=[blk] * 3,
                 out_shape=[_sds((R, C), F32)] * 3)(w, g, m, v)


def _pack(arrays):
    pieces = []
    for a in arrays:
        f = a.reshape(-1).astype(F32)
        pieces.append(jnp.pad(f, (0, (-f.shape[0]) % PACK)))
    return jnp.concatenate(pieces).reshape(-1, 128)


def _unpack(flat, shapes):
    out, off = [], 0
    f = flat.reshape(-1)
    for s in shapes:
        size = 1
        for d in s:
            size *= d
        out.append(f[off:off + size].reshape(s))
        off += size + (-size) % PACK
    return out


def kernel(x, mix_pre_g, mix_post_g, ffn_pre_g, ffn_post_g, cm_w_in, cm_b_in, cm_dw, cm_dw_b, cm_ln_g, cm_ln_b, cm_w_out, cm_b_out, kv_norm_g, w_kv, w_q, w_o, ffn_w_in, ffn_dw, ffn_dw_b, ffn_w_out, loss_target, m_mix_pre_g, m_mix_post_g, m_ffn_pre_g, m_ffn_post_g, m_cm_w_in, m_cm_b_in, m_cm_dw, m_cm_dw_b, m_cm_ln_g, m_cm_ln_b, m_cm_w_out, m_cm_b_out, m_kv_norm_g, m_w_kv, m_w_q, m_w_o, m_ffn_w_in, m_ffn_dw, m_ffn_dw_b, m_ffn_w_out, v_mix_pre_g, v_mix_post_g, v_ffn_pre_g, v_ffn_post_g, v_cm_w_in, v_cm_b_in, v_cm_dw, v_cm_dw_b, v_cm_ln_g, v_cm_ln_b, v_cm_w_out, v_cm_b_out, v_kv_norm_g, v_w_kv, v_w_q, v_w_o, v_ffn_w_in, v_ffn_dw, v_ffn_dw_b, v_ffn_w_out):
    nb, seq, D = x.shape
    T = nb * seq
    me = _my_index()
    n_b = DEPTH - N_A

    nf = ffn_w_in.shape[-1]

    stages = [(part, i) for i in range(DEPTH) for part in ("mix", "ffn")]

    def stage_sources(stage):
        part, i = stage
        if part == "ffn":
            src = {"fin": ffn_w_in[i], "fout": ffn_w_out[i]}
            if i == N_A - 1:
                src.update(kv=w_kv)
        elif i < N_A:
            src = {"cin": cm_w_in[i], "cout": cm_w_out[i]}
        else:
            src = {"q": w_q[i - N_A], "o": w_o[i - N_A]}
        return {k: t.astype(MM) for k, t in src.items()}

    def begin_gather(stage, after):
        src = stage_sources(stage)
        names, arrays = list(src), list(src.values())
        tag = f"{stage[0]}{stage[1]}"
        lands = place_own(f"gather_own_{tag}", "gather", arrays)
        handle, token = exchange_begin(f"gather_begin_{tag}", "gather", arrays, lands, after)
        return (names, handle), token

    def end_gather(stage, pending, after):
        names, handle = pending
        W = dict(zip(names, exchange_end(f"gather_end_{stage[0]}{stage[1]}", handle, after)))
        for k in W:
            if k in ("cout", "o"):
                W[k] = W[k].reshape(1, 1, D, D)
            elif k == "fout":
                W[k] = W[k].reshape(1, 4, nf, D)
            else:
                W[k] = W[k][:, None]
        return W

    small = [cm_b_in[:, None, :], cm_dw, cm_dw_b[:, None, :], cm_ln_g, cm_ln_b, cm_b_out, ffn_dw]
    Bcin, DWc, DWBc, LNg, LNb, Bcout, DWf = all_gather("gather_small", small, [False] * len(small))
    LNg = jnp.swapaxes(LNg, 0, 1).reshape(N_A, D)
    LNb = jnp.swapaxes(LNb, 0, 1).reshape(N_A, D)
    Bcout = jnp.swapaxes(Bcout, 0, 1).reshape(N_A, D)
    DWBf = ffn_dw_b.reshape(DEPTH, N_DEV, 1, nf)
    zero_bias = jnp.zeros((D,), F32)

    xs = x.reshape(T, D)
    sv = []
    kv = hkv = None
    pending, _ = begin_gather(stages[0], xs)
    Ws = {stages[0]: end_gather(stages[0], pending, xs)}
    sv = [{} for _ in range(DEPTH)]
    for idx, stage in enumerate(stages):
        part, i = stage
        L, W = sv[i], Ws[stage]
        gain = mix_pre_g[i] if part == "mix" else ffn_pre_g[i]
        following = stages[idx + 1] if idx + 1 < len(stages) else None
        if following is not None:
            pending, token = begin_gather(following, xs)
            gain = gain + token[0, 0]
        if part == "mix":
            L["x_in"] = xs
            if i < N_A:
                L["u"], L["h"] = norm_mm(f"cm_in_{i}", xs, gain, W["cin"], 0, Bcin[:, i:i + 1])
                L["c"] = cm_glu_conv(f"cm_conv_{i}", L["u"], DWc, DWBc, i, seq)
                L["s"] = ln_silu(f"cm_ln_{i}", L["c"], LNg[i], LNb[i])
                L["y"], xs = mm_resid_norm(f"cm_out_{i}", L["s"], W["cout"], 0, Bcout[i], xs, mix_post_g[i])
            else:
                L["q"], L["h"] = norm_mm(f"attn_q_{i}", xs, gain, W["q"], 0)
                L["mg"], L["lse"] = attn_fwd(f"attn_{i}", L["q"], kv, nb, seq)
                L["y"], xs = mm_resid_norm(f"attn_out_{i}", L["mg"], W["o"], 0, zero_bias, xs, mix_post_g[i])
            L["x1"] = xs
        else:
            L["uf"], L["hf"] = norm_mm(f"ffn_in_{i}", xs, gain, W["fin"], 0)
            L["z"], L["ag"] = ffn_conv_gate(f"ffn_conv_{i}", L["uf"], DWf, DWBf, i, seq)
            L["yf"], xs = mm_resid_norm(f"ffn_out_{i}", L["z"], W["fout"], 0, zero_bias, xs, ffn_post_g[i])
            if i == N_A - 1:
                kv, hkv = norm_mm("kv_proj", xs, kv_norm_g, W["kv"], 0)
        if following is not None:
            Ws[following] = end_gather(following, pending, xs)
    dx, loss_part = loss_fwd_bwd("loss", xs, loss_target.reshape(T, D))

    g_mix_pre, g_mix_post, g_ffn_pre, g_ffn_post = [None] * DEPTH, [None] * DEPTH, [None] * DEPTH, [None] * DEPTH
    g_ffn_dw, g_ffn_dwb = [None] * DEPTH, [None] * DEPTH
    g_cbin, g_cdw, g_cdwb, g_lng, g_lnb, g_cbout = ([None] * N_A for _ in range(6))
    g_kvn = dkv = None
    landed = [{} for _ in range(DEPTH)]
    in_flight = token = None
    for stage in reversed(stages):
        part, i = stage
        L, W = sv[i], Ws[stage]
        gain = ffn_post_g[i] if part == "ffn" else mix_post_g[i]
        if token is not None:
            gain = gain + token[0, 0]
        send = {}
        if part == "ffn":
            dyf, g_ffn_post[i], _ = resid_norm_bwd(f"ffn_post_bwd_{i}", dx, L["yf"], gain)
            dz = mm_nt(f"ffn_out_bwd_{i}", dyf, W["fout"], 0, MM)
            send["fout"] = mm_tn(f"ffn_out_wg_{i}", L["z"], dyf).reshape(N_DEV, nf // 2, D)
            duf, ddw, ddwb = ffn_conv_gate_bwd(f"ffn_conv_bwd_{i}", L["uf"], L["ag"], dz, DWf, i, seq)
            g_ffn_dw[i], g_ffn_dwb[i] = ddw.reshape(N_DEV, FFN_CONV_W, nf), ddwb.reshape(-1)
            send["fin"] = mm_tn(f"ffn_in_wg_{i}", L["hf"][None], duf)
            dx, g_ffn_pre[i] = mm_nt_norm_bwd(f"ffn_in_bwd_{i}", duf, W["fin"], 0, L["x1"], ffn_pre_g[i], dx)
        else:
            dy, g_mix_post[i], dyb = resid_norm_bwd(f"mix_post_bwd_{i}", dx, L["y"], gain)
            if i >= N_A:
                dm = mm_nt(f"attn_out_bwd_{i}", dy, W["o"], 0, F32)
                send["o"] = mm_tn(f"attn_out_wg_{i}", L["mg"], dy).reshape(N_DEV, D // N_DEV, D)
                dq, dkv = attn_bwd(f"attn_bwd_{i}", L["q"], kv, dm, L["mg"], L["lse"], dkv, nb, seq)
                send["q"] = mm_tn(f"attn_q_wg_{i}", L["h"][None], dq)
                dx, g_mix_pre[i] = mm_nt_norm_bwd(f"attn_q_bwd_{i}", dq, W["q"], 0, L["x_in"], mix_pre_g[i], dx)
                if i == N_A:
                    dkv_all = jnp.concatenate(dkv, axis=0)
                    send["kv"] = mm_tn("kv_wg", hkv[None], dkv_all)
                    dx, g_kvn = mm_nt_norm_bwd("kv_bwd", dkv_all, Ws[("ffn", N_A - 1)]["kv"], 0, L["x_in"],
                                               kv_norm_g, dx)
            else:
                g_cbout[i] = dyb
                ds = mm_nt(f"cm_out_bwd_{i}", dy, W["cout"], 0, F32)
                send["cout"] = mm_tn(f"cm_out_wg_{i}", L["s"], dy).reshape(N_DEV, D // N_DEV, D)
                dc, g_lng[i], g_lnb[i] = ln_silu_bwd(f"cm_ln_bwd_{i}", ds, L["c"], LNg[i], LNb[i])
                du, g_cdw[i], g_cdwb[i], dbi = cm_glu_conv_bwd(f"cm_conv_bwd_{i}", L["u"], dc, DWc, i, seq)
                g_cbin[i] = dbi.reshape(N_DEV, -1)
                send["cin"] = mm_tn(f"cm_in_wg_{i}", L["h"][None], du)
                dx, g_mix_pre[i] = mm_nt_norm_bwd(f"cm_in_bwd_{i}", du, W["cin"], 0, L["x_in"], mix_pre_g[i], dx)
        if in_flight is not None:
            (p, j), names, handle = in_flight
            landed[j].update(zip(names, exchange_end(f"scatter_end_{p}{j}", handle, dx)))
        names, arrays = list(send), list(send.values())
        lands = place_own(f"scatter_own_{part}{i}", "scatter", arrays)
        handle, token = exchange_begin(f"scatter_begin_{part}{i}", "scatter", arrays, lands, dx)
        in_flight = (stage, names, handle)
    (p, j), names, handle = in_flight
    landed[j].update(zip(names, exchange_end(f"scatter_end_{p}{j}", handle, dx)))
    grad_x = dx.reshape(nb, seq, D)

    def big_update(name, w, m, v, key, layers):
        as3 = lambda t: t.reshape((-1,) + t.shape[-2:])
        outs = adamw_sum(name, as3(w), as3(m), as3(v), [landed[i][key] for i in layers])
        return [t.reshape(w.shape) for t in outs]

    conf, attn = range(N_A), range(N_A, DEPTH)
    upd = {
        "cm_w_in": big_update("adam_cm_w_in", cm_w_in, m_cm_w_in, v_cm_w_in, "cin", conf),
        "w_kv": big_update("adam_w_kv", w_kv, m_w_kv, v_w_kv, "kv", [N_A]),
        "w_q": big_update("adam_w_q", w_q, m_w_q, v_w_q, "q", attn),
        "ffn_w_in": big_update("adam_ffn_w_in", ffn_w_in, m_ffn_w_in, v_ffn_w_in, "fin", range(DEPTH)),
        "cm_w_out": big_update("adam_cm_w_out", cm_w_out, m_cm_w_out, v_cm_w_out, "cout", conf),
        "w_o": big_update("adam_w_o", w_o, m_w_o, v_w_o, "o", attn),
        "ffn_w_out": big_update("adam_ffn_w_out", ffn_w_out, m_ffn_w_out, v_ffn_w_out, "fout", range(DEPTH)),
    }

    rep_names = ["mix_pre_g", "mix_post_g", "ffn_pre_g", "ffn_post_g", "kv_norm_g", "ffn_dw_b"]
    rep_parts = [jnp.concatenate(g_mix_pre), jnp.concatenate(g_mix_post), jnp.concatenate(g_ffn_pre),
                 jnp.concatenate(g_ffn_post), g_kvn.reshape(-1), jnp.stack(g_ffn_dwb)]
    rep_w = [mix_pre_g, mix_post_g, ffn_pre_g, ffn_post_g, kv_norm_g, ffn_dw_b]
    rep_m = [m_mix_pre_g, m_mix_post_g, m_ffn_pre_g, m_ffn_post_g, m_kv_norm_g, m_ffn_dw_b]
    rep_v = [v_mix_pre_g, v_mix_post_g, v_ffn_pre_g, v_ffn_post_g, v_kv_norm_g, v_ffn_dw_b]
    sh_names = ["ffn_dw", "cm_b_in", "cm_dw", "cm_dw_b", "cm_ln_g", "cm_ln_b", "cm_b_out"]
    own = lambda per_layer, shard: jnp.stack([p.reshape((N_DEV,) + shard) for p in per_layer], axis=1)
    sh_parts = [own(g_ffn_dw, ffn_dw.shape[1:]), own(g_cbin, cm_b_in.shape[1:]), own(g_cdw, cm_dw.shape[1:]),
                own(g_cdwb, cm_dw_b.shape[1:]), own(g_lng, cm_ln_g.shape[1:]), own(g_lnb, cm_ln_b.shape[1:]),
                own(g_cbout, cm_b_out.shape[1:])]
    sh_w = [ffn_dw, cm_b_in, cm_dw, cm_dw_b, cm_ln_g, cm_ln_b, cm_b_out]
    sh_m = [m_ffn_dw, m_cm_b_in, m_cm_dw, m_cm_dw_b, m_cm_ln_g, m_cm_ln_b, m_cm_b_out]
    sh_v = [v_ffn_dw, v_cm_b_in, v_cm_dw, v_cm_dw_b, v_cm_ln_g, v_cm_ln_b, v_cm_b_out]
    rep_pack = _pack([loss_part] + rep_parts)
    sh_packs = [_pack([p[k] for p in sh_parts]) for k in range(N_DEV)]
    n_rep, n_sh = rep_pack.shape[0], sh_packs[0].shape[0]
    (gathered,) = all_gather("gather_small_grads", [jnp.concatenate([rep_pack] + sh_packs)], [False])
    total = sum_partials("sum_small_grads", gathered)
    rep_sum = total[:n_rep]
    sh_sum = lax.dynamic_slice_in_dim(total, n_rep + me * n_sh, n_sh, axis=0)
    rep_shapes = [(1, 1)] + [w.shape for w in rep_w]
    sh_shapes = [w.shape for w in sh_w]
    g_small = jnp.concatenate([rep_sum, sh_sum])
    pad1 = jnp.zeros((1, 1), F32)
    d_s, m_s, v_s = adamw_small("adam_small", jnp.concatenate([_pack([pad1] + rep_w), _pack(sh_w)]), g_small,
                                jnp.concatenate([_pack([pad1] + rep_m), _pack(sh_m)]),
                                jnp.concatenate([_pack([pad1] + rep_v), _pack(sh_v)]))
    split = lambda t: (_unpack(t[:n_rep], rep_shapes), _unpack(t[n_rep:], sh_shapes))
    for (rep_t, sh_t), slot in zip([split(g_small), split(d_s), split(m_s), split(v_s)], range(4)):
        if slot == 0:
            loss = rep_t[0].reshape(())
        for name, t in zip(rep_names, rep_t[1:]):
            upd.setdefault(name, [None] * 4)[slot] = t
        for name, t in zip(sh_names, sh_t):
            upd.setdefault(name, [None] * 4)[slot] = t

    order = ["mix_pre_g", "mix_post_g", "ffn_pre_g", "ffn_post_g", "cm_w_in", "cm_b_in", "cm_dw", "cm_dw_b", "cm_ln_g",
             "cm_ln_b", "cm_w_out", "cm_b_out", "kv_norm_g", "w_kv", "w_q", "w_o", "ffn_w_in", "ffn_dw", "ffn_dw_b",
             "ffn_w_out"]
    return (loss, grad_x, *[upd[n][0] for n in order], *[upd[n][1] for n in order],
            *[upd[n][2] for n in order], *[upd[n][3] for n in order])
```

```python
import functools

import jax
import jax.numpy as jnp
from jax import lax
from jax.experimental import pallas as pl
from jax.experimental.pallas import tpu as pltpu

N_DEV = 8
N_A = 2
DEPTH = 4
N_HEADS = 8
N_GROUPS = 3
DILATIONS = (1, 4, 16)
ATT_BLOCK = 128
ATT_BATCH = 8
CONV_W = 31
FFN_CONV_W = 3
CONV_HALO = 32
FFN_HALO = 16
ROW_CHUNK_FWD = 16
ROW_CHUNK_BWD = 8
EPS = 1e-6
NEG = -1e30
ADAM_LR, ADAM_B1, ADAM_B2, ADAM_EPS, ADAM_WD, ADAM_STEP = 0.001, 0.9, 0.999, 1e-08, 0.01, 10
MM = jnp.bfloat16
F32 = jnp.float32
VMEM_LIMIT_BYTES = 56 * 1024 * 1024
PACK = 1024
MESH_ID = pl.DeviceIdType.MESH

_pallas = pl.pallas_call


def _call(body, *, name, out_shape, grid=(), in_specs=None, out_specs=None, scratch=()):
    return _pallas(body, name=name, out_shape=out_shape, grid=grid, in_specs=in_specs, out_specs=out_specs,
                   scratch_shapes=list(scratch),
                   compiler_params=pltpu.CompilerParams(vmem_limit_bytes=VMEM_LIMIT_BYTES))


def _tile(n, pref):
    if n <= pref:
        return n
    t = pref - pref % 8
    while n % t:
        t -= 8
    assert t > 0, (n, pref)
    return t


def _sds(shape, dtype):
    return jax.ShapeDtypeStruct(tuple(shape), dtype)


def _dot(a, b):
    return jnp.dot(a, b, preferred_element_type=F32)


def _dot_nt(a, b):
    return lax.dot_general(a, b, (((1,), (1,)), ((), ())), preferred_element_type=F32)


def _dot_tn(a, b):
    return lax.dot_general(a, b, (((0,), (0,)), ((), ())), preferred_element_type=F32)


def _sigmoid(x):
    return 1.0 / (1.0 + jnp.exp(-x))


def _my_index():
    return 4 * lax.axis_index("x") + 2 * lax.axis_index("y") + lax.axis_index("c")


def _exchange(name, arrays, out_shapes, pieces, src_of, dst_of):
    n = len(arrays)
    base = [sum(pieces[:a]) for a in range(n)]
    total = sum(pieces)

    def body(*refs):
        ins, outs = refs[:n], refs[n:2 * n]
        send_sems, recv_sems, local_sems = refs[2 * n:]
        x, y, c = lax.axis_index("x"), lax.axis_index("y"), lax.axis_index("c")
        me = 4 * x + 2 * y + c
        copies = []
        for a in range(n):
            for k, (s, d) in enumerate(zip(src_of(a, ins[a], me), dst_of(a, outs[a], me))):
                cp = pltpu.make_async_copy(s, d, local_sems.at[base[a] + k])
                cp.start()
                copies.append(cp)
        remote = []
        for m in range(1, N_DEV):
            px, py, pc = x ^ (m >> 2), y ^ ((m >> 1) & 1), c ^ (m & 1)
            peer = 4 * px + 2 * py + pc
            for a in range(n):
                for k, (s, d) in enumerate(zip(src_of(a, ins[a], peer), dst_of(a, outs[a], me))):
                    cp = pltpu.make_async_remote_copy(src_ref=s, dst_ref=d, send_sem=send_sems.at[base[a] + k, m - 1],
                                                      recv_sem=recv_sems.at[base[a] + k, m - 1],
                                                      device_id=(px, py, pc), device_id_type=MESH_ID)
                    cp.start()
                    remote.append(cp)
        for cp in copies:
            cp.wait()
        for cp in remote:
            cp.wait_send()
        for m in range(1, N_DEV):
            px, py, pc = x ^ (m >> 2), y ^ ((m >> 1) & 1), c ^ (m & 1)
            peer = 4 * px + 2 * py + pc
            for a in range(n):
                for k, (s, d) in enumerate(zip(src_of(a, ins[a], me), dst_of(a, outs[a], peer))):
                    pltpu.make_async_remote_copy(src_ref=s, dst_ref=d, send_sem=send_sems.at[base[a] + k, m - 1],
                                                 recv_sem=recv_sems.at[base[a] + k, m - 1], device_id=(px, py, pc),
                                                 device_id_type=MESH_ID).wait_recv()

    any_spec = pl.BlockSpec(memory_space=pl.ANY)
    return _call(body, name=name, out_shape=[_sds(s, a.dtype) for s, a in zip(out_shapes, arrays)],
                 in_specs=[any_spec] * n, out_specs=[any_spec] * n,
                 scratch=[pltpu.SemaphoreType.DMA((total, N_DEV - 1)), pltpu.SemaphoreType.DMA((total, N_DEV - 1)),
                          pltpu.SemaphoreType.DMA((total,))])(*arrays)


def all_gather(name, arrays, row_sharded):
    def out_shape(a):
        s = arrays[a].shape
        return (s[0], N_DEV) + s[1:] if row_sharded[a] else (N_DEV,) + s

    def src_of(a, ref, peer):
        if row_sharded[a]:
            return [ref.at[l] for l in range(arrays[a].shape[0])]
        return [ref]

    def dst_of(a, ref, me):
        if row_sharded[a]:
            return [ref.at[l, me] for l in range(arrays[a].shape[0])]
        return [ref.at[me]]

    pieces = [arrays[a].shape[0] if row_sharded[a] else 1 for a in range(len(arrays))]
    return _exchange(name, arrays, [out_shape(a) for a in range(len(arrays))], pieces, src_of, dst_of)


def _src_view(kind, ref, peer):
    return ref if kind == "gather" else ref.at[peer]


def _peers(x, y, c):
    for m in range(1, N_DEV):
        px, py, pc = x ^ (m >> 2), y ^ ((m >> 1) & 1), c ^ (m & 1)
        yield m - 1, (px, py, pc), 4 * px + 2 * py + pc


def place_own(name, kind, srcs):
    n = len(srcs)
    shapes = [(N_DEV,) + s.shape if kind == "gather" else s.shape for s in srcs]
    steps = 2 if all(s.shape[-2] % 32 == 0 for s in srcs) else 1

    def body(*refs):
        for a in range(n):
            refs[n + a][...] = refs[a][...]

    def spec(shape, own_block):
        R, C = shape[-2:]
        tr = R // steps
        if own_block:
            return pl.BlockSpec((None, tr, C), lambda i: (_my_index(), i, 0))
        return pl.BlockSpec((tr, C), lambda i: (i, 0))

    return _call(body, name=name, grid=(steps,), out_shape=[_sds(s, a.dtype) for s, a in zip(shapes, srcs)],
                 in_specs=[spec(s.shape, kind == "scatter") for s in srcs],
                 out_specs=[spec(s, True) for s in shapes])(*srcs)


_HBM_SPEC = pl.BlockSpec(memory_space=pltpu.HBM)
_SEM_SPEC = pl.BlockSpec(memory_space=pltpu.SEMAPHORE)
_DATAFLOW = pltpu.SideEffectType.DATAFLOW_SIDE_EFFECTING


def _remote(kind, src, land, send_sems, recv_sems, a, slot, frm, to_id, at):
    return pltpu.make_async_remote_copy(src_ref=_src_view(kind, src, frm), dst_ref=land.at[at],
                                        send_sem=send_sems.at[a * (N_DEV - 1) + slot],
                                        recv_sem=recv_sems.at[a * (N_DEV - 1) + slot],
                                        device_id=to_id, device_id_type=MESH_ID)


def exchange_begin(name, kind, srcs, lands, after):
    n = len(srcs)

    def body(*refs):
        ins, lnd = refs[:n], refs[n:2 * n]
        send_sems, recv_sems = refs[2 * n + 1], refs[2 * n + 2]
        token = refs[-1]
        x, y, c = lax.axis_index("x"), lax.axis_index("y"), lax.axis_index("c")
        me = 4 * x + 2 * y + c
        for slot, peer_id, peer in _peers(x, y, c):
            for a in range(n):
                _remote(kind, ins[a], lnd[a], send_sems, recv_sems, a, slot, peer, peer_id, me).start()
        token[...] = jnp.zeros_like(token)

    hbm = lambda t: pltpu.HBM(t.shape, t.dtype)
    outs = _pallas(
        body, name=name,
        out_shape=(pltpu.SemaphoreType.DMA((n * (N_DEV - 1),)), pltpu.SemaphoreType.DMA((n * (N_DEV - 1),)),
                   *[hbm(t) for t in srcs], *[hbm(t) for t in lands], _sds((8, 128), F32)),
        in_specs=[_HBM_SPEC] * (2 * n) + [pl.BlockSpec(memory_space=pl.ANY)],
        out_specs=(_SEM_SPEC, _SEM_SPEC, *[_HBM_SPEC] * (2 * n), pl.BlockSpec(memory_space=pltpu.VMEM)),
        input_output_aliases={i: 2 + i for i in range(2 * n)},
        compiler_params=pltpu.CompilerParams(has_side_effects=_DATAFLOW),
    )(*[pltpu.with_memory_space_constraint(t, pltpu.HBM) for t in list(srcs) + list(lands)], after)
    return (kind, outs[0], outs[1], list(outs[2:2 + n]), list(outs[2 + n:2 + 2 * n])), outs[-1]


def exchange_end(name, handle, after):
    kind, send_sems, recv_sems, srcs, lands = handle
    n = len(srcs)

    def body(*refs):
        ins, lnd = refs[:n], refs[n:2 * n]
        s_sems, r_sems = refs[2 * n], refs[2 * n + 1]
        x, y, c = lax.axis_index("x"), lax.axis_index("y"), lax.axis_index("c")
        me = 4 * x + 2 * y + c
        for slot, peer_id, peer in _peers(x, y, c):
            for a in range(n):
                _remote(kind, ins[a], lnd[a], s_sems, r_sems, a, slot, peer, peer_id, me).wait_send()
        for slot, peer_id, peer in _peers(x, y, c):
            for a in range(n):
                _remote(kind, ins[a], lnd[a], s_sems, r_sems, a, slot, me, peer_id, peer).wait_recv()

    hbm = lambda t: pltpu.HBM(t.shape, t.dtype)
    outs = _pallas(
        body, name=name, out_shape=tuple(hbm(t) for t in srcs + lands),
        in_specs=[_HBM_SPEC] * (2 * n) + [_SEM_SPEC, _SEM_SPEC, pl.BlockSpec(memory_space=pl.ANY)],
        out_specs=tuple([_HBM_SPEC] * (2 * n)), input_output_aliases={i: i for i in range(2 * n)},
        compiler_params=pltpu.CompilerParams(has_side_effects=_DATAFLOW),
    )(*srcs, *lands, send_sems, recv_sems, after)
    return list(outs[n:])


def norm_mm(name, x, gain, w, layer, bias=None):
    T, D = x.shape
    nsh, _, _, n = w.shape
    tm = _tile(T, 2048)

    def body(*refs):
        if bias is None:
            x_ref, g_ref, w_ref, u_ref, h_ref = refs
        else:
            x_ref, g_ref, w_ref, b_ref, u_ref, h_ref = refs

        @pl.when(pl.program_id(1) == 0)
        def _():
            xf = x_ref[...]
            r = lax.rsqrt(jnp.mean(xf * xf, axis=-1, keepdims=True) + EPS)
            h_ref[...] = (xf * r * g_ref[...]).astype(h_ref.dtype)

        acc = _dot(h_ref[...], w_ref[...])
        if bias is not None:
            acc = acc + b_ref[...]
        u_ref[...] = acc.astype(u_ref.dtype)

    in_specs = [pl.BlockSpec((tm, D), lambda i, j: (i, 0)),
                pl.BlockSpec((1, D), lambda i, j: (0, 0)),
                pl.BlockSpec((None, None, D, n), lambda i, j: (j, layer, 0, 0))]
    args = [x, gain.reshape(1, D), w]
    if bias is not None:
        in_specs.append(pl.BlockSpec((None, None, 1, n), lambda i, j: (j, layer, 0, 0)))
        args.append(bias)
    return _call(body, name=name, grid=(T // tm, nsh), in_specs=in_specs,
                 out_specs=[pl.BlockSpec((None, tm, n), lambda i, j: (j, i, 0)),
                            pl.BlockSpec((tm, D), lambda i, j: (i, 0))],
                 out_shape=[_sds((nsh, T, n), MM), _sds((T, D), MM)])(*args)


def mm_resid_norm(name, a, w, layer, bias, x, gain):
    nk, T, kk = a.shape
    D = x.shape[1]
    tm = _tile(T, 512)

    def body(a_ref, w_ref, b_ref, x_ref, g_ref, y_ref, xn_ref):
        y = _dot(a_ref[0], w_ref[0])
        for q in range(1, nk):
            y = y + _dot(a_ref[q], w_ref[q])
        y = y + b_ref[...]
        y_ref[...] = y
        r = lax.rsqrt(jnp.mean(y * y, axis=-1, keepdims=True) + EPS)
        xn_ref[...] = x_ref[...] + y * r * g_ref[...]

    return _call(body, name=name, grid=(T // tm,),
                 in_specs=[pl.BlockSpec((nk, tm, kk), lambda i: (0, i, 0)),
                           pl.BlockSpec((None, nk, kk, D), lambda i: (layer, 0, 0, 0)),
                           pl.BlockSpec((1, D), lambda i: (0, 0)),
                           pl.BlockSpec((tm, D), lambda i: (i, 0)),
                           pl.BlockSpec((1, D), lambda i: (0, 0))],
                 out_specs=[pl.BlockSpec((tm, D), lambda i: (i, 0))] * 2,
                 out_shape=[_sds((T, D), F32)] * 2)(a, w, bias.reshape(1, D), x, gain.reshape(1, D))


def _halo_maps(tm, hb, T):
    per = tm // hb
    last = T // hb - 1
    return (lambda i: jnp.maximum(i * per - 1, 0)), (lambda i: jnp.minimum((i + 1) * per, last))


def cm_glu_conv(name, u, dw, dwb, layer, seq):
    _, T, n = u.shape
    ct = dw.shape[-1]
    per = n // ct
    nct = 4 * per
    tm = _tile(seq, 512)
    tps = seq // tm
    hb = CONV_HALO
    prev, _ = _halo_maps(tm, hb, T)
    u4 = u.reshape(2, 4, T, n)

    def body(u_ref, uh_ref, w_ref, b_ref, o_ref, pad_ref):
        first = (pl.program_id(0) % tps) == 0
        um = u_ref[...].astype(F32)
        uh = uh_ref[...].astype(F32)
        pad_ref[pl.ds(hb, tm), :] = um[0] * _sigmoid(um[1])
        pad_ref[pl.ds(0, hb), :] = jnp.where(first, 0.0, uh[0] * _sigmoid(uh[1]))
        acc = jnp.zeros((tm, ct), F32) + b_ref[...]
        for k in range(CONV_W):
            acc = acc + w_ref[pl.ds(k, 1), :] * pad_ref[pl.ds(hb - (CONV_W - 1) + k, tm), :]
        o_ref[...] = acc

    return _call(body, name=name, grid=(T // tm, nct),
                 in_specs=[pl.BlockSpec((2, None, tm, ct), lambda i, c: (0, c // per, i, c % per)),
                           pl.BlockSpec((2, None, hb, ct), lambda i, c: (0, c // per, prev(i), c % per)),
                           pl.BlockSpec((None, None, CONV_W, ct), lambda i, c: (c, layer, 0, 0)),
                           pl.BlockSpec((None, None, 1, ct), lambda i, c: (c, layer, 0, 0))],
                 out_specs=pl.BlockSpec((tm, ct), lambda i, c: (i, c)),
                 out_shape=_sds((T, nct * ct), F32),
                 scratch=[pltpu.VMEM((tm + hb, ct), F32)])(u4, u4, dw, dwb)


def ln_silu(name, c, g, b):
    T, D = c.shape
    tm = _tile(T, 512)

    def body(c_ref, g_ref, b_ref, s_ref):
        cf = c_ref[...]
        mu = jnp.mean(cf, axis=-1, keepdims=True)
        xc = cf - mu
        r = lax.rsqrt(jnp.mean(xc * xc, axis=-1, keepdims=True) + EPS)
        t = xc * r * g_ref[...] + b_ref[...]
        s_ref[...] = (t * _sigmoid(t)).astype(s_ref.dtype)

    return _call(body, name=name, grid=(T // tm,),
                 in_specs=[pl.BlockSpec((tm, D), lambda i: (i, 0)), pl.BlockSpec((1, D), lambda i: (0, 0)),
                           pl.BlockSpec((1, D), lambda i: (0, 0))],
                 out_specs=pl.BlockSpec((None, tm, D), lambda i: (0, i, 0)),
                 out_shape=_sds((1, T, D), MM))(c, g.reshape(1, D), b.reshape(1, D))


def ffn_conv_gate(name, u, dw, dwb, layer, seq):
    _, T, n = u.shape
    tm = _tile(seq, 512)
    tps = seq // tm
    hb = FFN_HALO
    prev, _ = _halo_maps(tm, hb, T)
    u4 = u.reshape(2, 4, T, n)
    dw4 = dw.reshape(2, 4, dw.shape[1], FFN_CONV_W, n)
    dwb4 = dwb.reshape(dwb.shape[0], 2, 4, 1, n)

    rows = _tile(tm, ROW_CHUNK_FWD)

    def body(u_ref, uh_ref, w_ref, b_ref, z_ref, ag_ref, pad_ref):
        first = (pl.program_id(0) % tps) == 0
        for half in range(2):
            pad_ref[half, pl.ds(hb, tm), :] = u_ref[half].astype(F32)
            pad_ref[half, pl.ds(0, hb), :] = jnp.where(first, 0.0, uh_ref[half].astype(F32))
        for r0 in range(0, tm, rows):
            a, g = (_conv_rows(pad_ref, w_ref, b_ref, half, hb + r0, rows) for half in range(2))
            z_ref[pl.ds(r0, rows), :] = (g * _sigmoid(g) * a).astype(z_ref.dtype)
            ag_ref[0, pl.ds(r0, rows), :] = a.astype(ag_ref.dtype)
            ag_ref[1, pl.ds(r0, rows), :] = g.astype(ag_ref.dtype)

    return _call(body, name=name, grid=(T // tm, 4),
                 in_specs=[pl.BlockSpec((2, None, tm, n), lambda i, j: (0, j, i, 0)),
                           pl.BlockSpec((2, None, hb, n), lambda i, j: (0, j, prev(i), 0)),
                           pl.BlockSpec((2, None, None, FFN_CONV_W, n), lambda i, j: (0, j, layer, 0, 0)),
                           pl.BlockSpec((None, 2, None, 1, n), lambda i, j: (layer, 0, j, 0, 0))],
                 out_specs=[pl.BlockSpec((None, tm, n), lambda i, j: (j, i, 0)),
                            pl.BlockSpec((2, None, tm, n), lambda i, j: (0, j, i, 0))],
                 out_shape=[_sds((4, T, n), MM), _sds((2, 4, T, n), MM)],
                 scratch=[pltpu.VMEM((2, tm + hb, n), F32)])(u4, u4, dw4, dwb4)


def _conv_rows(pad_ref, w_ref, b_ref, half, at, rows):
    acc = b_ref[half] + w_ref[half, pl.ds(FFN_CONV_W - 1, 1), :] * pad_ref[half, pl.ds(at, rows), :]
    for k in range(FFN_CONV_W - 1):
        acc = acc + w_ref[half, pl.ds(k, 1), :] * pad_ref[half, pl.ds(at - (FFN_CONV_W - 1) + k, rows), :]
    return acc


def _head_specs(seq, dh, q_heads, kv_heads):
    def spec(per, base):
        def imap(b, h, g):
            f = base + g * N_HEADS + h
            return (f // per, b, f % per)
        return pl.BlockSpec((None, seq, dh), imap)
    return spec(q_heads, 0), spec(kv_heads, 0), spec(kv_heads, N_GROUPS * N_HEADS)


def _rows(start, d, blocks=1):
    size = blocks * ATT_BLOCK
    return pl.ds(start, size, stride=d) if d > 1 else pl.ds(start, size)


def _att_pad():
    return max(ATT_BLOCK * d for d in DILATIONS[:-1])


def _band_mask(first, nblk):
    keys = ATT_BLOCK if nblk == 1 else 2 * ATT_BLOCK
    shape = (ATT_BATCH, ATT_BLOCK, keys)
    qi = lax.broadcasted_iota(jnp.int32, shape, 1)
    kj = lax.broadcasted_iota(jnp.int32, shape, 2)
    if nblk == 1:
        return kj <= qi
    n = (first + lax.broadcasted_iota(jnp.int32, shape, 0)) % nblk
    return (kj >= qi) & (kj <= qi + ATT_BLOCK) & ((n > 0) | (kj >= ATT_BLOCK))


def _block_rows(idx, d, nblk):
    r, n = idx // nblk, idx % nblk
    rq = _rows(r + d * ATT_BLOCK * n, d)
    if nblk == 1:
        return rq, _rows(_att_pad() + r + d * ATT_BLOCK * n, d)
    return rq, _rows(_att_pad() + r + d * ATT_BLOCK * (n - 1), d, blocks=2)


def _bdot(a, b, ca, cb):
    return lax.dot_general(a, b, (((ca,), (cb,)), ((0,), (0,))), preferred_element_type=F32)


def attn_fwd(name, q, kv, nb, seq):
    _, T, qn = q.shape
    dh = qn * N_DEV // (N_GROUPS * N_HEADS)
    scale = 1.0 / (dh ** 0.5)
    qs, ks, vs = _head_specs(seq, dh, qn // dh, kv.shape[2] // dh)

    def body(q_ref, k_ref, v_ref, m_ref, l_ref, qf, kf, vf, *branch):
        og, lg = branch[:N_GROUPS], branch[N_GROUPS:]
        pad = _att_pad()
        qf[...] = q_ref[...].astype(F32)
        for t_ref, s_ref in ((k_ref, kf), (v_ref, vf)):
            s_ref[pl.ds(0, pad), :] = jnp.zeros((pad, dh), F32)
            s_ref[pl.ds(pad, seq), :] = t_ref[...].astype(F32)
        for g in range(N_GROUPS):
            d = DILATIONS[g]
            nblk = seq // d // ATT_BLOCK

            def blocks(it, carry, d=d, nblk=nblk, g=g):
                first = it * ATT_BATCH
                rows = [_block_rows(first + b, d, nblk) for b in range(ATT_BATCH)]
                qb = jnp.stack([qf[rq, :] for rq, _ in rows]).astype(MM)
                kb = jnp.stack([kf[rk, :] for _, rk in rows]).astype(MM)
                vb = jnp.stack([vf[rk, :] for _, rk in rows]).astype(MM)
                s = jnp.where(_band_mask(first, nblk), _bdot(qb, kb, 2, 2) * scale, NEG)
                m = jnp.max(s, axis=-1, keepdims=True)
                p = jnp.exp(s - m)
                den = jnp.sum(p, axis=-1, keepdims=True)
                o = _bdot(p.astype(MM), vb, 2, 1) / den
                lse = m + jnp.log(den)
                for b, (rq, _) in enumerate(rows):
                    og[g][rq, :] = o[b]
                    lg[g][rq, :] = jnp.broadcast_to(lse[b], (ATT_BLOCK, dh))
                return carry

            @pl.when(pl.program_id(2) == g)
            def _(blocks=blocks, d=d, nblk=nblk):
                lax.fori_loop(0, d * nblk // ATT_BATCH, blocks, 0)

        @pl.when(pl.program_id(2) == N_GROUPS - 1)
        def _():
            mx = jnp.maximum(jnp.maximum(lg[0][...], lg[1][...]), lg[2][...])
            e = [jnp.exp(lg[g][...] - mx) for g in range(N_GROUPS)]
            tot = e[0] + e[1] + e[2]
            m_ref[...] = ((e[0] * og[0][...] + e[1] * og[1][...] + e[2] * og[2][...]) / tot).astype(m_ref.dtype)
            l_ref[...] = mx + jnp.log(tot)

    return _call(body, name=name, grid=(nb, N_HEADS, N_GROUPS), in_specs=[qs, ks, vs],
                 out_specs=[pl.BlockSpec((None, seq, dh), lambda b, h, g: (0, b, h)),
                            pl.BlockSpec((seq, dh), lambda b, h, g: (b, h))],
                 out_shape=[_sds((1, T, N_HEADS * dh), MM), _sds((T, N_HEADS * dh), F32)],
                 scratch=[pltpu.VMEM((seq, dh), F32)] + [pltpu.VMEM((_att_pad() + seq, dh), F32)] * 2
                 + [pltpu.VMEM((seq, dh), F32)] * (2 * N_GROUPS))(q, kv, kv)


def loss_fwd_bwd(name, x, target):
    T, D = x.shape
    tm = _tile(T, 512)

    def body(x_ref, t_ref, dx_ref, l_ref):
        @pl.when(pl.program_id(0) == 0)
        def _():
            l_ref[...] = jnp.zeros_like(l_ref)
        err = x_ref[...] - t_ref[...]
        dx_ref[...] = err * (1.0 / D)
        l_ref[...] += 0.5 * jnp.sum(jnp.mean(err * err, axis=-1, keepdims=True), axis=0, keepdims=True)

    dx, l = _call(body, name=name, grid=(T // tm,),
                  in_specs=[pl.BlockSpec((tm, D), lambda i: (i, 0))] * 2,
                  out_specs=[pl.BlockSpec((tm, D), lambda i: (i, 0)), pl.BlockSpec((1, 1), lambda i: (0, 0))],
                  out_shape=[_sds((T, D), F32), _sds((1, 1), F32)])(x, target)
    return dx, l


def resid_norm_bwd(name, dx, y, gain):
    T, D = y.shape
    tm = _tile(T, 512)

    def body(dx_ref, y_ref, g_ref, dy_ref, dg_ref, db_ref):
        @pl.when(pl.program_id(0) == 0)
        def _():
            dg_ref[...] = jnp.zeros_like(dg_ref)
            db_ref[...] = jnp.zeros_like(db_ref)
        y = y_ref[...]
        d = dx_ref[...]
        r = lax.rsqrt(jnp.mean(y * y, axis=-1, keepdims=True) + EPS)
        yh = y * r
        dyh = d * g_ref[...]
        dy = r * (dyh - yh * jnp.mean(dyh * yh, axis=-1, keepdims=True))
        dy_ref[...] = dy.astype(dy_ref.dtype)
        dg_ref[...] += jnp.sum(d * yh, axis=0, keepdims=True)
        db_ref[...] += jnp.sum(dy, axis=0, keepdims=True)

    return _call(body, name=name, grid=(T // tm,),
                 in_specs=[pl.BlockSpec((tm, D), lambda i: (i, 0))] * 2 + [pl.BlockSpec((1, D), lambda i: (0, 0))],
                 out_specs=[pl.BlockSpec((None, tm, D), lambda i: (0, i, 0))] + [pl.BlockSpec((1, D), lambda i: (0, 0))] * 2,
                 out_shape=[_sds((1, T, D), MM), _sds((1, D), F32), _sds((1, D), F32)])(dx, y, gain.reshape(1, D))


def mm_nt(name, dy, w, layer, out_dtype):
    _, T, D = dy.shape
    _, nk, kk, _ = w.shape
    tm = _tile(T, 1024)

    def body(dy_ref, w_ref, o_ref):
        o_ref[...] = _dot_nt(dy_ref[...], w_ref[...]).astype(o_ref.dtype)

    return _call(body, name=name, grid=(T // tm, nk),
                 in_specs=[pl.BlockSpec((None, tm, D), lambda i, q: (0, i, 0)),
                           pl.BlockSpec((None, None, kk, D), lambda i, q: (layer, q, 0, 0))],
                 out_specs=pl.BlockSpec((None, tm, kk), lambda i, q: (q, i, 0)),
                 out_shape=_sds((nk, T, kk), out_dtype))(dy, w)


def mm_nt_norm_bwd(name, du, w, layer, x_in, gain, dx_res):
    nsh, T, n = du.shape
    D = x_in.shape[1]
    tm = _tile(T, 1024)

    def body(du_ref, w_ref, x_ref, g_ref, dr_ref, dx_ref, dg_ref, acc_ref):
        i, j = pl.program_id(0), pl.program_id(1)

        @pl.when((i == 0) & (j == 0))
        def _():
            dg_ref[...] = jnp.zeros_like(dg_ref)

        part = _dot_nt(du_ref[...], w_ref[...])

        @pl.when(j == 0)
        def _():
            acc_ref[...] = part

        @pl.when(j > 0)
        def _():
            acc_ref[...] += part

        @pl.when(j == nsh - 1)
        def _():
            x = x_ref[...]
            dh = acc_ref[...]
            r = lax.rsqrt(jnp.mean(x * x, axis=-1, keepdims=True) + EPS)
            xh = x * r
            dxh = dh * g_ref[...]
            dx_ref[...] = dr_ref[...] + r * (dxh - xh * jnp.mean(dxh * xh, axis=-1, keepdims=True))
            dg_ref[...] += jnp.sum(dh * xh, axis=0, keepdims=True)

    return _call(body, name=name, grid=(T // tm, nsh),
                 in_specs=[pl.BlockSpec((None, tm, n), lambda i, j: (j, i, 0)),
                           pl.BlockSpec((None, None, D, n), lambda i, j: (j, layer, 0, 0)),
                           pl.BlockSpec((tm, D), lambda i, j: (i, 0)),
                           pl.BlockSpec((1, D), lambda i, j: (0, 0)),
                           pl.BlockSpec((tm, D), lambda i, j: (i, 0))],
                 out_specs=[pl.BlockSpec((tm, D), lambda i, j: (i, 0)), pl.BlockSpec((1, D), lambda i, j: (0, 0))],
                 out_shape=[_sds((T, D), F32), _sds((1, D), F32)],
                 scratch=[pltpu.VMEM((tm, D), F32)])(du, w, x_in, gain.reshape(1, D), dx_res)


def mm_tn(name, a, b):
    na, T, ka = a.shape
    nb, _, kb = b.shape
    nj = max(na, nb)

    def body(a_ref, b_ref, o_ref):
        o_ref[...] = _dot_tn(a_ref[...], b_ref[...]).astype(o_ref.dtype)

    return _call(body, name=name, grid=(nj,),
                 in_specs=[pl.BlockSpec((None, T, ka), (lambda j: (j, 0, 0)) if na > 1 else (lambda j: (0, 0, 0))),
                           pl.BlockSpec((None, T, kb), (lambda j: (j, 0, 0)) if nb > 1 else (lambda j: (0, 0, 0)))],
                 out_specs=pl.BlockSpec((None, ka, kb), lambda j: (j, 0, 0)),
                 out_shape=_sds((nj, ka, kb), MM))(a, b)


def ffn_conv_gate_bwd(name, u, ag, dz, dw, layer, seq):
    _, T, n = u.shape
    tm = _tile(seq, 512)
    tps = seq // tm
    hb = FFN_HALO
    _, nxt = _halo_maps(tm, hb, T)
    u4 = u.reshape(2, 4, T, n)
    dw4 = dw.reshape(2, 4, dw.shape[1], FFN_CONV_W, n)
    K = FFN_CONV_W
    te = tm + hb
    rows = _tile(hb, ROW_CHUNK_BWD)

    def body(u_ref, ag_ref, agn_ref, dz_ref, dzn_ref, w_ref, du_ref, ddw_ref, ddb_ref, uf_ref, agf_ref, dzf_ref,
             da_ref):
        i = pl.program_id(1)
        last = (i % tps) == tps - 1

        @pl.when(i == 0)
        def _():
            ddw_ref[...] = jnp.zeros_like(ddw_ref)
            ddb_ref[...] = jnp.zeros_like(ddb_ref)

        for half in range(2):
            uf_ref[half] = u_ref[half].astype(F32)
            agf_ref[half, pl.ds(0, tm), :] = ag_ref[half].astype(F32)
            agf_ref[half, pl.ds(tm, hb), :] = agn_ref[half].astype(F32)
        dzf_ref[pl.ds(0, tm), :] = dz_ref[...].astype(F32)
        dzf_ref[pl.ds(tm, hb), :] = jnp.where(last, 0.0, dzn_ref[...].astype(F32))
        for r0 in range(0, te, rows):
            a, g = agf_ref[0, pl.ds(r0, rows), :], agf_ref[1, pl.ds(r0, rows), :]
            dzc = dzf_ref[pl.ds(r0, rows), :]
            sg = _sigmoid(g)
            da_ref[0, pl.ds(r0, rows), :] = dzc * (g * sg)
            da_ref[1, pl.ds(r0, rows), :] = dzc * a * (sg * (1.0 + g * (1.0 - sg)))
        for half in range(2):
            tap_acc = [jnp.zeros((rows, n), F32) for _ in range(K)]
            bias_acc = jnp.zeros((rows, n), F32)
            for r0 in range(0, tm, rows):
                x = uf_ref[half, pl.ds(r0, rows), :]
                acc = None
                for k in range(K):
                    d = da_ref[half, pl.ds(r0 + K - 1 - k, rows), :]
                    acc = w_ref[half, pl.ds(k, 1), :] * d if acc is None else acc + w_ref[half, pl.ds(k, 1), :] * d
                    tap_acc[k] = tap_acc[k] + d * x
                    if k == K - 1:
                        bias_acc = bias_acc + d
                du_ref[half, pl.ds(r0, rows), :] = acc.astype(du_ref.dtype)
            for k in range(K):
                ddw_ref[half, pl.ds(k, 1), :] += jnp.sum(tap_acc[k], axis=0, keepdims=True)
            ddb_ref[half] += jnp.sum(bias_acc, axis=0, keepdims=True)

    du, ddw, ddb = _call(
        body, name=name, grid=(4, T // tm),
        in_specs=[pl.BlockSpec((2, None, tm, n), lambda j, i: (0, j, i, 0)),
                  pl.BlockSpec((2, None, tm, n), lambda j, i: (0, j, i, 0)),
                  pl.BlockSpec((2, None, hb, n), lambda j, i: (0, j, nxt(i), 0)),
                  pl.BlockSpec((None, tm, n), lambda j, i: (j, i, 0)),
                  pl.BlockSpec((None, hb, n), lambda j, i: (j, nxt(i), 0)),
                  pl.BlockSpec((2, None, None, K, n), lambda j, i: (0, j, layer, 0, 0))],
        out_specs=[pl.BlockSpec((2, None, tm, n), lambda j, i: (0, j, i, 0)),
                   pl.BlockSpec((2, None, K, n), lambda j, i: (0, j, 0, 0)),
                   pl.BlockSpec((2, None, 1, n), lambda j, i: (0, j, 0, 0))],
        out_shape=[_sds((2, 4, T, n), MM), _sds((2, 4, K, n), F32), _sds((2, 4, 1, n), F32)],
        scratch=[pltpu.VMEM((2, tm, n), F32), pltpu.VMEM((2, te, n), F32), pltpu.VMEM((te, n), F32),
                 pltpu.VMEM((2, te, n), F32)],
    )(u4, ag, ag, dz, dz, dw4)
    return du.reshape(8, T, n), ddw, ddb


def ln_silu_bwd(name, ds, c, g, b):
    T, D = c.shape
    tm = _tile(T, 512)

    def body(ds_ref, c_ref, g_ref, b_ref, dc_ref, dg_ref, db_ref):
        @pl.when(pl.program_id(0) == 0)
        def _():
            dg_ref[...] = jnp.zeros_like(dg_ref)
            db_ref[...] = jnp.zeros_like(db_ref)
        cf = c_ref[...]
        mu = jnp.mean(cf, axis=-1, keepdims=True)
        xc = cf - mu
        r = lax.rsqrt(jnp.mean(xc * xc, axis=-1, keepdims=True) + EPS)
        xh = xc * r
        t = xh * g_ref[...] + b_ref[...]
        sg = _sigmoid(t)
        dt = ds_ref[...] * (sg * (1.0 + t * (1.0 - sg)))
        dg_ref[...] += jnp.sum(dt * xh, axis=0, keepdims=True)
        db_ref[...] += jnp.sum(dt, axis=0, keepdims=True)
        dxh = dt * g_ref[...]
        dc_ref[...] = r * (dxh - jnp.mean(dxh, axis=-1, keepdims=True)
                           - xh * jnp.mean(dxh * xh, axis=-1, keepdims=True))

    vec = pl.BlockSpec((1, D), lambda i: (0, 0))
    return _call(body, name=name, grid=(T // tm,),
                 in_specs=[pl.BlockSpec((None, tm, D), lambda i: (0, i, 0)), pl.BlockSpec((tm, D), lambda i: (i, 0)),
                           vec, vec],
                 out_specs=[pl.BlockSpec((tm, D), lambda i: (i, 0)), vec, vec],
                 out_shape=[_sds((T, D), F32), _sds((1, D), F32), _sds((1, D), F32)])(
                     ds, c, g.reshape(1, D), b.reshape(1, D))


def cm_glu_conv_bwd(name, u, dc, dw, layer, seq):
    _, T, n = u.shape
    ct = dw.shape[-1]
    per = n // ct
    nct = 4 * per
    tm = _tile(seq, 512)
    tps = seq // tm
    hb = CONV_HALO
    prev, nxt = _halo_maps(tm, hb, T)
    u4 = u.reshape(2, 4, T, n)
    K = CONV_W

    def body(u_ref, up_ref, dc_ref, dcn_ref, w_ref, du_ref, ddw_ref, ddb_ref, dbi_ref, padp_ref, padd_ref):
        i = pl.program_id(1)
        first = (i % tps) == 0
        last = (i % tps) == tps - 1

        @pl.when(i == 0)
        def _():
            ddw_ref[...] = jnp.zeros_like(ddw_ref)
            ddb_ref[...] = jnp.zeros_like(ddb_ref)
            dbi_ref[...] = jnp.zeros_like(dbi_ref)

        um = u_ref[...].astype(F32)
        uh = up_ref[...].astype(F32)
        sg = _sigmoid(um[1])
        padp_ref[pl.ds(hb, tm), :] = um[0] * sg
        padp_ref[pl.ds(0, hb), :] = jnp.where(first, 0.0, uh[0] * _sigmoid(uh[1]))
        dcm = dc_ref[...]
        padd_ref[pl.ds(0, tm), :] = dcm
        padd_ref[pl.ds(tm, hb), :] = jnp.where(last, 0.0, dcn_ref[...])
        dp = jnp.zeros((tm, ct), F32)
        for k in range(K):
            dp = dp + w_ref[pl.ds(k, 1), :] * padd_ref[pl.ds(K - 1 - k, tm), :]
            ddw_ref[pl.ds(k, 1), :] += jnp.sum(dcm * padp_ref[pl.ds(hb - (K - 1) + k, tm), :], axis=0, keepdims=True)
        ddb_ref[...] += jnp.sum(dcm, axis=0, keepdims=True)
        dv = dp * sg
        dg = dp * um[0] * sg * (1.0 - sg)
        du_ref[0] = dv.astype(du_ref.dtype)
        du_ref[1] = dg.astype(du_ref.dtype)
        dbi_ref[0] += jnp.sum(dv, axis=0, keepdims=True)
        dbi_ref[1] += jnp.sum(dg, axis=0, keepdims=True)

    du, ddw, ddb, dbi = _call(
        body, name=name, grid=(nct, T // tm),
        in_specs=[pl.BlockSpec((2, None, tm, ct), lambda c, i: (0, c // per, i, c % per)),
                  pl.BlockSpec((2, None, hb, ct), lambda c, i: (0, c // per, prev(i), c % per)),
                  pl.BlockSpec((tm, ct), lambda c, i: (i, c)),
                  pl.BlockSpec((hb, ct), lambda c, i: (nxt(i), c)),
                  pl.BlockSpec((None, None, K, ct), lambda c, i: (c, layer, 0, 0))],
        out_specs=[pl.BlockSpec((2, None, tm, ct), lambda c, i: (0, c // per, i, c % per)),
                   pl.BlockSpec((None, K, ct), lambda c, i: (c, 0, 0)),
                   pl.BlockSpec((None, 1, ct), lambda c, i: (c, 0, 0)),
                   pl.BlockSpec((2, None, 1, ct), lambda c, i: (0, c // per, 0, c % per))],
        out_shape=[_sds((2, 4, T, n), MM), _sds((nct, K, ct), F32), _sds((nct, 1, ct), F32), _sds((2, 4, 1, n), F32)],
        scratch=[pltpu.VMEM((tm + hb, ct), F32), pltpu.VMEM((tm + hb, ct), F32)])(u4, u4, dc, dc, dw)
    return du.reshape(8, T, n), ddw, ddb, dbi


def attn_bwd(name, q, kv, dm, merged, lse, dkv_prev, nb, seq):
    _, T, qn = q.shape
    kn = kv.shape[2]
    dh = qn * N_DEV // (N_GROUPS * N_HEADS)
    scale = 1.0 / (dh ** 0.5)
    qs, ks, vs = _head_specs(seq, dh, qn // dh, kn // dh)
    has_prev = dkv_prev is not None
    n_in = 6 + (2 if has_prev else 0)

    def body(*refs):
        q_ref, k_ref, v_ref, dm_ref, mg_ref, l_ref = refs[:6]
        pk_ref, pv_ref = refs[6:8] if has_prev else (None, None)
        dq_ref, dk_ref, dv_ref = refs[n_in:n_in + 3]
        qf, kf, vf, dqf, dkf, dvf, dlt = refs[n_in + 3:]
        pad = _att_pad()

        @pl.when(pl.program_id(2) == 0)
        def _():
            dlt[...] = jnp.broadcast_to(
                jnp.sum(dm_ref[...] * mg_ref[...].astype(F32), axis=-1, keepdims=True), (seq, dh))

        qf[...] = q_ref[...].astype(F32)
        for t_ref, s_ref in ((k_ref, kf), (v_ref, vf)):
            s_ref[pl.ds(0, pad), :] = jnp.zeros((pad, dh), F32)
            s_ref[pl.ds(pad, seq), :] = t_ref[...].astype(F32)
        dkf[...] = jnp.zeros_like(dkf)
        dvf[...] = jnp.zeros_like(dvf)
        for g in range(N_GROUPS):
            d = DILATIONS[g]
            nblk = seq // d // ATT_BLOCK

            def blocks(it, carry, d=d, nblk=nblk):
                first = it * ATT_BATCH
                rows = [_block_rows(first + b, d, nblk) for b in range(ATT_BATCH)]
                qb = jnp.stack([qf[rq, :] for rq, _ in rows]).astype(MM)
                dmb = jnp.stack([dm_ref[rq, :] for rq, _ in rows]).astype(MM)
                lse = jnp.stack([l_ref[rq, :][:, :1] for rq, _ in rows])
                delta = jnp.stack([dlt[rq, :][:, :1] for rq, _ in rows])
                kb = jnp.stack([kf[rk, :] for _, rk in rows]).astype(MM)
                vb = jnp.stack([vf[rk, :] for _, rk in rows]).astype(MM)
                s = jnp.where(_band_mask(first, nblk), _bdot(qb, kb, 2, 2) * scale, NEG)
                p = jnp.exp(s - lse)
                dsc = (p * (_bdot(dmb, vb, 2, 2) - delta) * scale).astype(MM)
                dv = _bdot(p.astype(MM), dmb, 1, 1)
                dk = _bdot(dsc, qb, 1, 1)
                dq = _bdot(dsc, kb, 2, 1)
                for b, (rq, rk) in enumerate(rows):
                    dqf[rq, :] = dq[b]
                    dkf[rk, :] += dk[b]
                    dvf[rk, :] += dv[b]
                return carry

            @pl.when(pl.program_id(2) == g)
            def _(blocks=blocks, d=d, nblk=nblk):
                lax.fori_loop(0, d * nblk // ATT_BATCH, blocks, 0)

        dq_ref[...] = dqf[...].astype(dq_ref.dtype)
        dk, dv = dkf[pl.ds(pad, seq), :], dvf[pl.ds(pad, seq), :]
        if has_prev:
            dk, dv = dk + pk_ref[...].astype(F32), dv + pv_ref[...].astype(F32)
        dk_ref[...] = dk.astype(dk_ref.dtype)
        dv_ref[...] = dv.astype(dv_ref.dtype)

    full = pl.BlockSpec((None, seq, dh), lambda b, h, g: (0, b, h))
    in_specs = [qs, ks, vs, full, full, pl.BlockSpec((seq, dh), lambda b, h, g: (b, h))]
    args = [q, kv, kv, dm, merged, lse]
    if has_prev:
        in_specs += [ks, ks]
        args += [dkv_prev[0], dkv_prev[1]]
    short, padded = pltpu.VMEM((seq, dh), F32), pltpu.VMEM((_att_pad() + seq, dh), F32)
    dq, dk, dv = _call(body, name=name, grid=(nb, N_HEADS, N_GROUPS), in_specs=in_specs, out_specs=[qs, ks, ks],
                       out_shape=[_sds(q.shape, MM), _sds((N_DEV // 2, T, kn), MM), _sds((N_DEV // 2, T, kn), MM)],
                       scratch=[short, padded, padded, short, padded, padded, short])(*args)
    return dq, (dk, dv)


def _adamw_math(w, g, m, v):
    m = ADAM_B1 * m + (1.0 - ADAM_B1) * g
    v = ADAM_B2 * v + (1.0 - ADAM_B2) * (g * g)
    m_hat = m / (1.0 - ADAM_B1 ** ADAM_STEP)
    v_hat = v / (1.0 - ADAM_B2 ** ADAM_STEP)
    delta = -ADAM_LR * (m_hat / (jnp.sqrt(v_hat) + ADAM_EPS) + ADAM_WD * w)
    return delta, m, v


def adamw_sum(name, w, m, v, parts, after):
    L, R, C = w.shape
    tr = _tile(R, 256)

    def body(*refs):
        w_ref, m_ref, v_ref = refs[:3]
        p_refs = refs[3:3 + L]
        g_ref, d_ref, nm_ref, nv_ref = refs[4 + L:]
        for l in range(L):
            @pl.when(pl.program_id(0) == l)
            def _(p_ref=p_refs[l]):
                g = p_ref[0].astype(F32)
                for k in range(1, N_DEV):
                    g = g + p_ref[k].astype(F32)
                g_ref[...] = g
                d_ref[...], nm_ref[...], nv_ref[...] = _adamw_math(w_ref[...], g, m_ref[...], v_ref[...])

    blk = pl.BlockSpec((None, tr, C), lambda l, i: (l, i, 0))
    part = lambda k: pl.BlockSpec((N_DEV, tr, C), lambda l, i: (0, jnp.where(l == k, i, 0), 0))
    return _call(body, name=name, grid=(L, R // tr),
                 in_specs=[blk, blk, blk] + [part(k) for k in range(L)] + [pl.BlockSpec(memory_space=pl.ANY)],
                 out_specs=[blk] * 4, out_shape=[_sds((L, R, C), F32)] * 4)(w, m, v, *parts, after)


def sum_partials(name, parts):
    _, R, C = parts.shape
    tr = _tile(R, 512)

    def body(p_ref, o_ref):
        g = p_ref[0]
        for k in range(1, N_DEV):
            g = g + p_ref[k]
        o_ref[...] = g

    return _call(body, name=name, grid=(R // tr,),
                 in_specs=[pl.BlockSpec((N_DEV, tr, C), lambda i: (0, i, 0))],
                 out_specs=pl.BlockSpec((tr, C), lambda i: (i, 0)), out_shape=_sds((R, C), F32))(parts)


def adamw_small(name, w, g, m, v):
    R, C = w.shape
    tr = _tile(R, 512)

    def body(w_ref, g_ref, m_ref, v_ref, d_ref, nm_ref, nv_ref):
        d_ref[...], nm_ref[...], nv_ref[...] = _adamw_math(w_ref[...], g_ref[...], m_ref[...], v_ref[...])

    blk = pl.BlockSpec((tr, C), lambda i: (i, 0))
    return _call(body, name=name, grid=(R // tr,), in_specs=[blk] * 4, out_specs=[blk] * 3,
                 out_shape=[_sds((R, C), F32)] * 3)(w, g, m, v)


def _pack(arrays):
    pieces = []
    for a in arrays:
        f = a.reshape(-1).astype(F32)
        pieces.append(jnp.pad(f, (0, (-f.shape[0]) % PACK)))
    return jnp.concatenate(pieces).reshape(-1, 128)


def _unpack(flat, shapes):
    out, off = [], 0
    f = flat.reshape(-1)
    for s in shapes:
        size = 1
        for d in s:
            size *= d
        out.append(f[off:off + size].reshape(s))
        off += size + (-size) % PACK
    return out


def kernel(x, mix_pre_g, mix_post_g, ffn_pre_g, ffn_post_g, cm_w_in, cm_b_in, cm_dw, cm_dw_b, cm_ln_g, cm_ln_b, cm_w_out, cm_b_out, kv_norm_g, w_kv, w_q, w_o, ffn_w_in, ffn_dw, ffn_dw_b, ffn_w_out, loss_target, m_mix_pre_g, m_mix_post_g, m_ffn_pre_g, m_ffn_post_g, m_cm_w_in, m_cm_b_in, m_cm_dw, m_cm_dw_b, m_cm_ln_g, m_cm_ln_b, m_cm_w_out, m_cm_b_out, m_kv_norm_g, m_w_kv, m_w_q, m_w_o, m_ffn_w_in, m_ffn_dw, m_ffn_dw_b, m_ffn_w_out, v_mix_pre_g, v_mix_post_g, v_ffn_pre_g, v_ffn_post_g, v_cm_w_in, v_cm_b_in, v_cm_dw, v_cm_dw_b, v_cm_ln_g, v_cm_ln_b, v_cm_w_out, v_cm_b_out, v_kv_norm_g, v_w_kv, v_w_q, v_w_o, v_ffn_w_in, v_ffn_dw, v_ffn_dw_b, v_ffn_w_out):
    nb, seq, D = x.shape
    T = nb * seq
    me = _my_index()
    n_b = DEPTH - N_A

    nf = ffn_w_in.shape[-1]

    stages = [(part, i) for i in range(DEPTH) for part in ("mix", "ffn")]
    stages.insert(stages.index(("ffn", N_A - 1)) + 1, ("kv", N_A - 1))

    def stage_sources(stage):
        part, i = stage
        if part == "ffn":
            src = {"fin": ffn_w_in[i], "fout": ffn_w_out[i]}
        elif part == "kv":
            src = {"kv": w_kv}
        elif i < N_A:
            src = {"cin": cm_w_in[i], "cout": cm_w_out[i]}
        else:
            src = {"q": w_q[i - N_A], "o": w_o[i - N_A]}
        return {k: t.astype(MM) for k, t in src.items()}

    def begin_gather(stage, after):
        src = stage_sources(stage)
        names, arrays = list(src), list(src.values())
        tag = f"{stage[0]}{stage[1]}"
        lands = place_own(f"gather_own_{tag}", "gather", arrays)
        handle, token = exchange_begin(f"gather_begin_{tag}", "gather", arrays, lands, after)
        return (names, handle), token

    def end_gather(stage, pending, after):
        names, handle = pending
        W = dict(zip(names, exchange_end(f"gather_end_{stage[0]}{stage[1]}", handle, after)))
        for k in W:
            if k in ("cout", "o"):
                W[k] = W[k].reshape(1, 1, D, D)
            elif k == "fout":
                W[k] = W[k].reshape(1, 4, nf, D)
            else:
                W[k] = W[k][:, None]
        return W

    small = [cm_b_in[:, None, :], cm_dw, cm_dw_b[:, None, :], cm_ln_g, cm_ln_b, cm_b_out, ffn_dw]
    Bcin, DWc, DWBc, LNg, LNb, Bcout, DWf = all_gather("gather_small", small, [False] * len(small))
    LNg = jnp.swapaxes(LNg, 0, 1).reshape(N_A, D)
    LNb = jnp.swapaxes(LNb, 0, 1).reshape(N_A, D)
    Bcout = jnp.swapaxes(Bcout, 0, 1).reshape(N_A, D)
    DWBf = ffn_dw_b.reshape(DEPTH, N_DEV, 1, nf)
    zero_bias = jnp.zeros((D,), F32)

    xs = x.reshape(T, D)
    sv = []
    kv = hkv = None
    pending, _ = begin_gather(stages[0], xs)
    Ws = {stages[0]: end_gather(stages[0], pending, xs)}
    sv = [{} for _ in range(DEPTH)]
    for idx, stage in enumerate(stages):
        part, i = stage
        L, W = sv[i], Ws[stage]
        gain = {"mix": mix_pre_g[i], "ffn": ffn_pre_g[i], "kv": kv_norm_g}[part]
        following = stages[idx + 1] if idx + 1 < len(stages) else None
        if following is not None:
            pending, token = begin_gather(following, xs)
            gain = gain + token[0, 0]
        if part == "mix":
            L["x_in"] = xs
            if i < N_A:
                L["u"], L["h"] = norm_mm(f"cm_in_{i}", xs, gain, W["cin"], 0, Bcin[:, i:i + 1])
                L["c"] = cm_glu_conv(f"cm_conv_{i}", L["u"], DWc, DWBc, i, seq)
                L["s"] = ln_silu(f"cm_ln_{i}", L["c"], LNg[i], LNb[i])
                L["y"], xs = mm_resid_norm(f"cm_out_{i}", L["s"], W["cout"], 0, Bcout[i], xs, mix_post_g[i])
            else:
                L["q"], L["h"] = norm_mm(f"attn_q_{i}", xs, gain, W["q"], 0)
                L["mg"], L["lse"] = attn_fwd(f"attn_{i}", L["q"], kv, nb, seq)
                L["y"], xs = mm_resid_norm(f"attn_out_{i}", L["mg"], W["o"], 0, zero_bias, xs, mix_post_g[i])
            L["x1"] = xs
        elif part == "ffn":
            L["uf"], L["hf"] = norm_mm(f"ffn_in_{i}", xs, gain, W["fin"], 0)
            L["z"], L["ag"] = ffn_conv_gate(f"ffn_conv_{i}", L["uf"], DWf, DWBf, i, seq)
            L["yf"], xs = mm_resid_norm(f"ffn_out_{i}", L["z"], W["fout"], 0, zero_bias, xs, ffn_post_g[i])
        else:
            kv, hkv = norm_mm("kv_proj", xs, gain, W["kv"], 0)
        if following is not None:
            Ws[following] = end_gather(following, pending, kv if part == "kv" else xs)
    dx, loss_part = loss_fwd_bwd("loss", xs, loss_target.reshape(T, D))

    g_mix_pre, g_mix_post, g_ffn_pre, g_ffn_post = [None] * DEPTH, [None] * DEPTH, [None] * DEPTH, [None] * DEPTH
    g_ffn_dw, g_ffn_dwb = [None] * DEPTH, [None] * DEPTH
    g_cbin, g_cdw, g_cdwb, g_lng, g_lnb, g_cbout = ([None] * N_A for _ in range(6))
    g_kvn = dkv = None
    landed = [{} for _ in range(DEPTH)]
    in_flight = token = None
    for stage in reversed(stages):
        part, i = stage
        L, W = sv[i], Ws[stage]
        gain = {"mix": mix_post_g[i], "ffn": ffn_post_g[i], "kv": kv_norm_g}[part]
        if token is not None:
            gain = gain + token[0, 0]
        send = {}
        if part == "kv":
            dkv_all = jnp.concatenate(dkv, axis=0)
            send["kv"] = mm_tn("kv_wg", hkv[None], dkv_all)
            dx, g_kvn = mm_nt_norm_bwd("kv_bwd", dkv_all, W["kv"], 0, sv[i + 1]["x_in"], gain, dx)
        elif part == "ffn":
            dyf, g_ffn_post[i], _ = resid_norm_bwd(f"ffn_post_bwd_{i}", dx, L["yf"], gain)
            dz = mm_nt(f"ffn_out_bwd_{i}", dyf, W["fout"], 0, MM)
            send["fout"] = mm_tn(f"ffn_out_wg_{i}", L["z"], dyf).reshape(N_DEV, nf // 2, D)
            duf, ddw, ddwb = ffn_conv_gate_bwd(f"ffn_conv_bwd_{i}", L["uf"], L["ag"], dz, DWf, i, seq)
            g_ffn_dw[i], g_ffn_dwb[i] = ddw.reshape(N_DEV, FFN_CONV_W, nf), ddwb.reshape(-1)
            send["fin"] = mm_tn(f"ffn_in_wg_{i}", L["hf"][None], duf)
            dx, g_ffn_pre[i] = mm_nt_norm_bwd(f"ffn_in_bwd_{i}", duf, W["fin"], 0, L["x1"], ffn_pre_g[i], dx)
        else:
            dy, g_mix_post[i], dyb = resid_norm_bwd(f"mix_post_bwd_{i}", dx, L["y"], gain)
            if i >= N_A:
                dm = mm_nt(f"attn_out_bwd_{i}", dy, W["o"], 0, F32)
                send["o"] = mm_tn(f"attn_out_wg_{i}", L["mg"], dy).reshape(N_DEV, D // N_DEV, D)
                dq, dkv = attn_bwd(f"attn_bwd_{i}", L["q"], kv, dm, L["mg"], L["lse"], dkv, nb, seq)
                send["q"] = mm_tn(f"attn_q_wg_{i}", L["h"][None], dq)
                dx, g_mix_pre[i] = mm_nt_norm_bwd(f"attn_q_bwd_{i}", dq, W["q"], 0, L["x_in"], mix_pre_g[i], dx)
            else:
                g_cbout[i] = dyb
                ds = mm_nt(f"cm_out_bwd_{i}", dy, W["cout"], 0, F32)
                send["cout"] = mm_tn(f"cm_out_wg_{i}", L["s"], dy).reshape(N_DEV, D // N_DEV, D)
                dc, g_lng[i], g_lnb[i] = ln_silu_bwd(f"cm_ln_bwd_{i}", ds, L["c"], LNg[i], LNb[i])
                du, g_cdw[i], g_cdwb[i], dbi = cm_glu_conv_bwd(f"cm_conv_bwd_{i}", L["u"], dc, DWc, i, seq)
                g_cbin[i] = dbi.reshape(N_DEV, -1)
                send["cin"] = mm_tn(f"cm_in_wg_{i}", L["h"][None], du)
                dx, g_mix_pre[i] = mm_nt_norm_bwd(f"cm_in_bwd_{i}", du, W["cin"], 0, L["x_in"], mix_pre_g[i], dx)
        if in_flight is not None:
            (p, j), names, handle = in_flight
            landed[j].update(zip(names, exchange_end(f"scatter_end_{p}{j}", handle, dx)))
        names, arrays = list(send), list(send.values())
        lands = place_own(f"scatter_own_{part}{i}", "scatter", arrays)
        handle, token = exchange_begin(f"scatter_begin_{part}{i}", "scatter", arrays, lands, dx)
        in_flight = (stage, names, handle)
    grad_x = dx.reshape(nb, seq, D)

    rep_names = ["mix_pre_g", "mix_post_g", "ffn_pre_g", "ffn_post_g", "kv_norm_g", "ffn_dw_b"]
    rep_parts = [jnp.concatenate(g_mix_pre), jnp.concatenate(g_mix_post), jnp.concatenate(g_ffn_pre),
                 jnp.concatenate(g_ffn_post), g_kvn.reshape(-1), jnp.stack(g_ffn_dwb)]
    rep_w = [mix_pre_g, mix_post_g, ffn_pre_g, ffn_post_g, kv_norm_g, ffn_dw_b]
    rep_m = [m_mix_pre_g, m_mix_post_g, m_ffn_pre_g, m_ffn_post_g, m_kv_norm_g, m_ffn_dw_b]
    rep_v = [v_mix_pre_g, v_mix_post_g, v_ffn_pre_g, v_ffn_post_g, v_kv_norm_g, v_ffn_dw_b]
    sh_names = ["ffn_dw", "cm_b_in", "cm_dw", "cm_dw_b", "cm_ln_g", "cm_ln_b", "cm_b_out"]
    own = lambda per_layer, shard: jnp.stack([p.reshape((N_DEV,) + shard) for p in per_layer], axis=1)
    sh_parts = [own(g_ffn_dw, ffn_dw.shape[1:]), own(g_cbin, cm_b_in.shape[1:]), own(g_cdw, cm_dw.shape[1:]),
                own(g_cdwb, cm_dw_b.shape[1:]), own(g_lng, cm_ln_g.shape[1:]), own(g_lnb, cm_ln_b.shape[1:]),
                own(g_cbout, cm_b_out.shape[1:])]
    sh_w = [ffn_dw, cm_b_in, cm_dw, cm_dw_b, cm_ln_g, cm_ln_b, cm_b_out]
    sh_m = [m_ffn_dw, m_cm_b_in, m_cm_dw, m_cm_dw_b, m_cm_ln_g, m_cm_ln_b, m_cm_b_out]
    sh_v = [v_ffn_dw, v_cm_b_in, v_cm_dw, v_cm_dw_b, v_cm_ln_g, v_cm_ln_b, v_cm_b_out]
    rep_pack = _pack([loss_part] + rep_parts)
    sh_pack = jnp.stack([_pack([p[k] for p in sh_parts]) for k in range(N_DEV)])
    n_rep = rep_pack.shape[0]
    small = []
    for kind, pack in (("gather", rep_pack), ("scatter", sh_pack)):
        lands = place_own(f"{kind}_own_small", kind, [pack])
        handle, token = exchange_begin(f"{kind}_begin_small", kind, [pack], lands, dx)
        small.append((kind, handle))

    def big_update(name, w, m, v, key, layers, after):
        as3 = lambda t: t.reshape((-1,) + t.shape[-2:])
        outs = adamw_sum(name, as3(w), as3(m), as3(v), [landed[i][key] for i in layers], after)
        return [t.reshape(w.shape) for t in outs]

    conf, attn = range(N_A), range(N_A, DEPTH)
    upd = {}
    upd["ffn_w_in"] = big_update("adam_ffn_w_in", ffn_w_in, m_ffn_w_in, v_ffn_w_in, "fin", range(DEPTH), token)
    upd["ffn_w_out"] = big_update("adam_ffn_w_out", ffn_w_out, m_ffn_w_out, v_ffn_w_out, "fout", range(DEPTH), token)
    upd["w_kv"] = big_update("adam_w_kv", w_kv, m_w_kv, v_w_kv, "kv", [N_A - 1], token)
    upd["w_q"] = big_update("adam_w_q", w_q, m_w_q, v_w_q, "q", attn, token)
    upd["w_o"] = big_update("adam_w_o", w_o, m_w_o, v_w_o, "o", attn, upd["w_q"][0])
    (p, j), names, handle = in_flight
    landed[j].update(zip(names, exchange_end(f"scatter_end_{p}{j}", handle, upd["w_o"][0])))
    upd["cm_w_in"] = big_update("adam_cm_w_in", cm_w_in, m_cm_w_in, v_cm_w_in, "cin", conf, token)
    upd["cm_w_out"] = big_update("adam_cm_w_out", cm_w_out, m_cm_w_out, v_cm_w_out, "cout", conf, upd["cm_w_in"][0])
    (rep_landed,), (sh_landed,) = (exchange_end(f"{kind}_end_small", handle, upd["cm_w_out"][0])
                                   for kind, handle in small)
    rep_sum = sum_partials("sum_small_rep", rep_landed)
    sh_sum = sum_partials("sum_small_sh", sh_landed)
    rep_shapes = [(1, 1)] + [w.shape for w in rep_w]
    sh_shapes = [w.shape for w in sh_w]
    g_small = jnp.concatenate([rep_sum, sh_sum])
    pad1 = jnp.zeros((1, 1), F32)
    d_s, m_s, v_s = adamw_small("adam_small", jnp.concatenate([_pack([pad1] + rep_w), _pack(sh_w)]), g_small,
                                jnp.concatenate([_pack([pad1] + rep_m), _pack(sh_m)]),
                                jnp.concatenate([_pack([pad1] + rep_v), _pack(sh_v)]))
    split = lambda t: (_unpack(t[:n_rep], rep_shapes), _unpack(t[n_rep:], sh_shapes))
    for (rep_t, sh_t), slot in zip([split(g_small), split(d_s), split(m_s), split(v_s)], range(4)):
        if slot == 0:
            loss = rep_t[0].reshape(())
        for name, t in zip(rep_names, rep_t[1:]):
            upd.setdefault(name, [None] * 4)[slot] = t
        for name, t in zip(sh_names, sh_t):
            upd.setdefault(name, [None] * 4)[slot] = t

    order = ["mix_pre_g", "mix_post_g", "ffn_pre_g", "ffn_post_g", "cm_w_in", "cm_b_in", "cm_dw", "cm_dw_b", "cm_ln_g",
             "cm_ln_b", "cm_w_out", "cm_b_out", "kv_norm_g", "w_kv", "w_q", "w_o", "ffn_w_in", "ffn_dw", "ffn_dw_b",
             "ffn_w_out"]
    return (loss, grad_x, *[upd[n][0] for n in order], *[upd[n][1] for n in order],
            *[upd[n][2] for n in order], *[upd[n][3] for n in order])
```

```python
import functools

import jax
import jax.numpy as jnp
from jax import lax
from jax.experimental import pallas as pl
from jax.experimental.pallas import tpu as pltpu

N_DEV = 8
N_A = 2
DEPTH = 4
N_HEADS = 8
N_GROUPS = 3
DILATIONS = (1, 4, 16)
ATT_BLOCK = 128
ATT_BATCH = 8
CONV_W = 31
FFN_CONV_W = 3
CONV_HALO = 32
FFN_HALO = 16
ROW_CHUNK_FWD = 16
ROW_CHUNK_BWD = 8
EPS = 1e-6
NEG = -1e30
ADAM_LR, ADAM_B1, ADAM_B2, ADAM_EPS, ADAM_WD, ADAM_STEP = 0.001, 0.9, 0.999, 1e-08, 0.01, 10
MM = jnp.bfloat16
F32 = jnp.float32
VMEM_LIMIT_BYTES = 56 * 1024 * 1024
PACK = 1024
MESH_ID = pl.DeviceIdType.MESH

_pallas = pl.pallas_call


def _call(body, *, name, out_shape, grid=(), in_specs=None, out_specs=None, scratch=()):
    return _pallas(body, name=name, out_shape=out_shape, grid=grid, in_specs=in_specs, out_specs=out_specs,
                   scratch_shapes=list(scratch),
                   compiler_params=pltpu.CompilerParams(vmem_limit_bytes=VMEM_LIMIT_BYTES))


def _tile(n, pref):
    if n <= pref:
        return n
    t = pref - pref % 8
    while n % t:
        t -= 8
    assert t > 0, (n, pref)
    return t


def _sds(shape, dtype):
    return jax.ShapeDtypeStruct(tuple(shape), dtype)


def _dot(a, b):
    return jnp.dot(a, b, preferred_element_type=F32)


def _dot_nt(a, b):
    return lax.dot_general(a, b, (((1,), (1,)), ((), ())), preferred_element_type=F32)


def _dot_tn(a, b):
    return lax.dot_general(a, b, (((0,), (0,)), ((), ())), preferred_element_type=F32)


def _sigmoid(x):
    return 1.0 / (1.0 + jnp.exp(-x))


def _my_index():
    return 4 * lax.axis_index("x") + 2 * lax.axis_index("y") + lax.axis_index("c")


def _exchange(name, arrays, out_shapes, pieces, src_of, dst_of):
    n = len(arrays)
    base = [sum(pieces[:a]) for a in range(n)]
    total = sum(pieces)

    def body(*refs):
        ins, outs = refs[:n], refs[n:2 * n]
        send_sems, recv_sems, local_sems = refs[2 * n:]
        x, y, c = lax.axis_index("x"), lax.axis_index("y"), lax.axis_index("c")
        me = 4 * x + 2 * y + c
        copies = []
        for a in range(n):
            for k, (s, d) in enumerate(zip(src_of(a, ins[a], me), dst_of(a, outs[a], me))):
                cp = pltpu.make_async_copy(s, d, local_sems.at[base[a] + k])
                cp.start()
                copies.append(cp)
        remote = []
        for m in range(1, N_DEV):
            px, py, pc = x ^ (m >> 2), y ^ ((m >> 1) & 1), c ^ (m & 1)
            peer = 4 * px + 2 * py + pc
            for a in range(n):
                for k, (s, d) in enumerate(zip(src_of(a, ins[a], peer), dst_of(a, outs[a], me))):
                    cp = pltpu.make_async_remote_copy(src_ref=s, dst_ref=d, send_sem=send_sems.at[base[a] + k, m - 1],
                                                      recv_sem=recv_sems.at[base[a] + k, m - 1],
                                                      device_id=(px, py, pc), device_id_type=MESH_ID)
                    cp.start()
                    remote.append(cp)
        for cp in copies:
            cp.wait()
        for cp in remote:
            cp.wait_send()
        for m in range(1, N_DEV):
            px, py, pc = x ^ (m >> 2), y ^ ((m >> 1) & 1), c ^ (m & 1)
            peer = 4 * px + 2 * py + pc
            for a in range(n):
                for k, (s, d) in enumerate(zip(src_of(a, ins[a], me), dst_of(a, outs[a], peer))):
                    pltpu.make_async_remote_copy(src_ref=s, dst_ref=d, send_sem=send_sems.at[base[a] + k, m - 1],
                                                 recv_sem=recv_sems.at[base[a] + k, m - 1], device_id=(px, py, pc),
                                                 device_id_type=MESH_ID).wait_recv()

    any_spec = pl.BlockSpec(memory_space=pl.ANY)
    return _call(body, name=name, out_shape=[_sds(s, a.dtype) for s, a in zip(out_shapes, arrays)],
                 in_specs=[any_spec] * n, out_specs=[any_spec] * n,
                 scratch=[pltpu.SemaphoreType.DMA((total, N_DEV - 1)), pltpu.SemaphoreType.DMA((total, N_DEV - 1)),
                          pltpu.SemaphoreType.DMA((total,))])(*arrays)


def all_gather(name, arrays, row_sharded):
    def out_shape(a):
        s = arrays[a].shape
        return (s[0], N_DEV) + s[1:] if row_sharded[a] else (N_DEV,) + s

    def src_of(a, ref, peer):
        if row_sharded[a]:
            return [ref.at[l] for l in range(arrays[a].shape[0])]
        return [ref]

    def dst_of(a, ref, me):
        if row_sharded[a]:
            return [ref.at[l, me] for l in range(arrays[a].shape[0])]
        return [ref.at[me]]

    pieces = [arrays[a].shape[0] if row_sharded[a] else 1 for a in range(len(arrays))]
    return _exchange(name, arrays, [out_shape(a) for a in range(len(arrays))], pieces, src_of, dst_of)


def _src_view(kind, ref, peer):
    return ref if kind == "gather" else ref.at[peer]


def _peers(x, y, c):
    for m in range(1, N_DEV):
        px, py, pc = x ^ (m >> 2), y ^ ((m >> 1) & 1), c ^ (m & 1)
        yield m - 1, (px, py, pc), 4 * px + 2 * py + pc


def place_own(name, kind, srcs):
    n = len(srcs)
    shapes = [(N_DEV,) + s.shape if kind == "gather" else s.shape for s in srcs]
    steps = 2 if all(s.shape[-2] % 32 == 0 for s in srcs) else 1

    def body(*refs):
        for a in range(n):
            refs[n + a][...] = refs[a][...]

    def spec(shape, own_block):
        R, C = shape[-2:]
        tr = R // steps
        if own_block:
            return pl.BlockSpec((None, tr, C), lambda i: (_my_index(), i, 0))
        return pl.BlockSpec((tr, C), lambda i: (i, 0))

    return _call(body, name=name, grid=(steps,), out_shape=[_sds(s, a.dtype) for s, a in zip(shapes, srcs)],
                 in_specs=[spec(s.shape, kind == "scatter") for s in srcs],
                 out_specs=[spec(s, True) for s in shapes])(*srcs)


_HBM_SPEC = pl.BlockSpec(memory_space=pltpu.HBM)
_SEM_SPEC = pl.BlockSpec(memory_space=pltpu.SEMAPHORE)
_DATAFLOW = pltpu.SideEffectType.DATAFLOW_SIDE_EFFECTING


def _remote(kind, src, land, send_sems, recv_sems, a, slot, frm, to_id, at):
    return pltpu.make_async_remote_copy(src_ref=_src_view(kind, src, frm), dst_ref=land.at[at],
                                        send_sem=send_sems.at[a * (N_DEV - 1) + slot],
                                        recv_sem=recv_sems.at[a * (N_DEV - 1) + slot],
                                        device_id=to_id, device_id_type=MESH_ID)


def exchange_begin(name, kind, srcs, lands, after):
    n = len(srcs)

    def body(*refs):
        ins, lnd = refs[:n], refs[n:2 * n]
        send_sems, recv_sems = refs[2 * n + 1], refs[2 * n + 2]
        token = refs[-1]
        x, y, c = lax.axis_index("x"), lax.axis_index("y"), lax.axis_index("c")
        me = 4 * x + 2 * y + c
        for slot, peer_id, peer in _peers(x, y, c):
            for a in range(n):
                _remote(kind, ins[a], lnd[a], send_sems, recv_sems, a, slot, peer, peer_id, me).start()
        token[...] = jnp.zeros_like(token)

    hbm = lambda t: pltpu.HBM(t.shape, t.dtype)
    outs = _pallas(
        body, name=name,
        out_shape=(pltpu.SemaphoreType.DMA((n * (N_DEV - 1),)), pltpu.SemaphoreType.DMA((n * (N_DEV - 1),)),
                   *[hbm(t) for t in srcs], *[hbm(t) for t in lands], _sds((8, 128), F32)),
        in_specs=[_HBM_SPEC] * (2 * n) + [pl.BlockSpec(memory_space=pl.ANY)],
        out_specs=(_SEM_SPEC, _SEM_SPEC, *[_HBM_SPEC] * (2 * n), pl.BlockSpec(memory_space=pltpu.VMEM)),
        input_output_aliases={i: 2 + i for i in range(2 * n)},
        compiler_params=pltpu.CompilerParams(has_side_effects=_DATAFLOW),
    )(*[pltpu.with_memory_space_constraint(t, pltpu.HBM) for t in list(srcs) + list(lands)], after)
    return (kind, outs[0], outs[1], list(outs[2:2 + n]), list(outs[2 + n:2 + 2 * n])), outs[-1]


def exchange_end(name, handle, after):
    kind, send_sems, recv_sems, srcs, lands = handle
    n = len(srcs)

    def body(*refs):
        ins, lnd = refs[:n], refs[n:2 * n]
        s_sems, r_sems = refs[2 * n], refs[2 * n + 1]
        x, y, c = lax.axis_index("x"), lax.axis_index("y"), lax.axis_index("c")
        me = 4 * x + 2 * y + c
        for slot, peer_id, peer in _peers(x, y, c):
            for a in range(n):
                _remote(kind, ins[a], lnd[a], s_sems, r_sems, a, slot, peer, peer_id, me).wait_send()
        for slot, peer_id, peer in _peers(x, y, c):
            for a in range(n):
                _remote(kind, ins[a], lnd[a], s_sems, r_sems, a, slot, me, peer_id, peer).wait_recv()

    hbm = lambda t: pltpu.HBM(t.shape, t.dtype)
    outs = _pallas(
        body, name=name, out_shape=tuple(hbm(t) for t in srcs + lands),
        in_specs=[_HBM_SPEC] * (2 * n) + [_SEM_SPEC, _SEM_SPEC, pl.BlockSpec(memory_space=pl.ANY)],
        out_specs=tuple([_HBM_SPEC] * (2 * n)), input_output_aliases={i: i for i in range(2 * n)},
        compiler_params=pltpu.CompilerParams(has_side_effects=_DATAFLOW),
    )(*srcs, *lands, send_sems, recv_sems, after)
    return list(outs[n:])


def norm_mm(name, x, gain, w, layer, bias=None, w_t=False):
    T, D = x.shape
    nsh = w.shape[0]
    n = w.shape[2] if w_t else w.shape[3]
    tm = _tile(T, 2048)

    def body(*refs):
        if bias is None:
            x_ref, g_ref, w_ref, u_ref, h_ref = refs
        else:
            x_ref, g_ref, w_ref, b_ref, u_ref, h_ref = refs

        @pl.when(pl.program_id(1) == 0)
        def _():
            xf = x_ref[...]
            r = lax.rsqrt(jnp.mean(xf * xf, axis=-1, keepdims=True) + EPS)
            h_ref[...] = (xf * r * g_ref[...]).astype(h_ref.dtype)

        acc = (_dot_nt if w_t else _dot)(h_ref[...], w_ref[...])
        if bias is not None:
            acc = acc + b_ref[...]
        u_ref[...] = acc.astype(u_ref.dtype)

    in_specs = [pl.BlockSpec((tm, D), lambda i, j: (i, 0)),
                pl.BlockSpec((1, D), lambda i, j: (0, 0)),
                pl.BlockSpec((None, None) + w.shape[2:], lambda i, j: (j, layer, 0, 0))]
    args = [x, gain.reshape(1, D), w]
    if bias is not None:
        in_specs.append(pl.BlockSpec((None, None, 1, n), lambda i, j: (j, layer, 0, 0)))
        args.append(bias)
    return _call(body, name=name, grid=(T // tm, nsh), in_specs=in_specs,
                 out_specs=[pl.BlockSpec((None, tm, n), lambda i, j: (j, i, 0)),
                            pl.BlockSpec((tm, D), lambda i, j: (i, 0))],
                 out_shape=[_sds((nsh, T, n), MM), _sds((T, D), MM)])(*args)


def mm_resid_norm(name, a, w, layer, bias, x, gain):
    nk, T, kk = a.shape
    D = x.shape[1]
    tm = _tile(T, 512)

    def body(a_ref, w_ref, b_ref, x_ref, g_ref, y_ref, xn_ref):
        y = _dot(a_ref[0], w_ref[0])
        for q in range(1, nk):
            y = y + _dot(a_ref[q], w_ref[q])
        y = y + b_ref[...]
        y_ref[...] = y
        r = lax.rsqrt(jnp.mean(y * y, axis=-1, keepdims=True) + EPS)
        xn_ref[...] = x_ref[...] + y * r * g_ref[...]

    return _call(body, name=name, grid=(T // tm,),
                 in_specs=[pl.BlockSpec((nk, tm, kk), lambda i: (0, i, 0)),
                           pl.BlockSpec((None, nk, kk, D), lambda i: (layer, 0, 0, 0)),
                           pl.BlockSpec((1, D), lambda i: (0, 0)),
                           pl.BlockSpec((tm, D), lambda i: (i, 0)),
                           pl.BlockSpec((1, D), lambda i: (0, 0))],
                 out_specs=[pl.BlockSpec((tm, D), lambda i: (i, 0))] * 2,
                 out_shape=[_sds((T, D), F32)] * 2)(a, w, bias.reshape(1, D), x, gain.reshape(1, D))


def _halo_maps(tm, hb, T):
    per = tm // hb
    last = T // hb - 1
    return (lambda i: jnp.maximum(i * per - 1, 0)), (lambda i: jnp.minimum((i + 1) * per, last))


def cm_glu_conv(name, u, dw, dwb, layer, seq):
    _, T, n = u.shape
    ct = dw.shape[-1]
    per = n // ct
    nct = 4 * per
    tm = _tile(seq, 512)
    tps = seq // tm
    hb = CONV_HALO
    prev, _ = _halo_maps(tm, hb, T)
    u4 = u.reshape(2, 4, T, n)

    def body(u_ref, uh_ref, w_ref, b_ref, o_ref, pad_ref):
        first = (pl.program_id(0) % tps) == 0
        um = u_ref[...].astype(F32)
        uh = uh_ref[...].astype(F32)
        pad_ref[pl.ds(hb, tm), :] = um[0] * _sigmoid(um[1])
        pad_ref[pl.ds(0, hb), :] = jnp.where(first, 0.0, uh[0] * _sigmoid(uh[1]))
        acc = jnp.zeros((tm, ct), F32) + b_ref[...]
        for k in range(CONV_W):
            acc = acc + w_ref[pl.ds(k, 1), :] * pad_ref[pl.ds(hb - (CONV_W - 1) + k, tm), :]
        o_ref[...] = acc

    return _call(body, name=name, grid=(T // tm, nct),
                 in_specs=[pl.BlockSpec((2, None, tm, ct), lambda i, c: (0, c // per, i, c % per)),
                           pl.BlockSpec((2, None, hb, ct), lambda i, c: (0, c // per, prev(i), c % per)),
                           pl.BlockSpec((None, None, CONV_W, ct), lambda i, c: (c, layer, 0, 0)),
                           pl.BlockSpec((None, None, 1, ct), lambda i, c: (c, layer, 0, 0))],
                 out_specs=pl.BlockSpec((tm, ct), lambda i, c: (i, c)),
                 out_shape=_sds((T, nct * ct), F32),
                 scratch=[pltpu.VMEM((tm + hb, ct), F32)])(u4, u4, dw, dwb)


def ln_silu(name, c, g, b):
    T, D = c.shape
    tm = _tile(T, 512)

    def body(c_ref, g_ref, b_ref, s_ref):
        cf = c_ref[...]
        mu = jnp.mean(cf, axis=-1, keepdims=True)
        xc = cf - mu
        r = lax.rsqrt(jnp.mean(xc * xc, axis=-1, keepdims=True) + EPS)
        t = xc * r * g_ref[...] + b_ref[...]
        s_ref[...] = (t * _sigmoid(t)).astype(s_ref.dtype)

    return _call(body, name=name, grid=(T // tm,),
                 in_specs=[pl.BlockSpec((tm, D), lambda i: (i, 0)), pl.BlockSpec((1, D), lambda i: (0, 0)),
                           pl.BlockSpec((1, D), lambda i: (0, 0))],
                 out_specs=pl.BlockSpec((None, tm, D), lambda i: (0, i, 0)),
                 out_shape=_sds((1, T, D), MM))(c, g.reshape(1, D), b.reshape(1, D))


def ffn_conv_gate(name, u, dw, dwb, layer, seq):
    _, T, n = u.shape
    tm = _tile(seq, 512)
    tps = seq // tm
    hb = FFN_HALO
    prev, _ = _halo_maps(tm, hb, T)
    u4 = u.reshape(2, 4, T, n)
    dw4 = dw.reshape(2, 4, dw.shape[1], FFN_CONV_W, n)
    dwb4 = dwb.reshape(dwb.shape[0], 2, 4, 1, n)

    rows = _tile(tm, ROW_CHUNK_FWD)

    def body(u_ref, uh_ref, w_ref, b_ref, z_ref, ag_ref, pad_ref):
        first = (pl.program_id(0) % tps) == 0
        for half in range(2):
            pad_ref[half, pl.ds(hb, tm), :] = u_ref[half].astype(F32)
            pad_ref[half, pl.ds(0, hb), :] = jnp.where(first, 0.0, uh_ref[half].astype(F32))
        for r0 in range(0, tm, rows):
            a, g = (_conv_rows(pad_ref, w_ref, b_ref, half, hb + r0, rows) for half in range(2))
            z_ref[pl.ds(r0, rows), :] = (g * _sigmoid(g) * a).astype(z_ref.dtype)
            ag_ref[0, pl.ds(r0, rows), :] = a.astype(ag_ref.dtype)
            ag_ref[1, pl.ds(r0, rows), :] = g.astype(ag_ref.dtype)

    return _call(body, name=name, grid=(T // tm, 4),
                 in_specs=[pl.BlockSpec((2, None, tm, n), lambda i, j: (0, j, i, 0)),
                           pl.BlockSpec((2, None, hb, n), lambda i, j: (0, j, prev(i), 0)),
                           pl.BlockSpec((2, None, None, FFN_CONV_W, n), lambda i, j: (0, j, layer, 0, 0)),
                           pl.BlockSpec((None, 2, None, 1, n), lambda i, j: (layer, 0, j, 0, 0))],
                 out_specs=[pl.BlockSpec((None, tm, n), lambda i, j: (j, i, 0)),
                            pl.BlockSpec((2, None, tm, n), lambda i, j: (0, j, i, 0))],
                 out_shape=[_sds((4, T, n), MM), _sds((2, 4, T, n), MM)],
                 scratch=[pltpu.VMEM((2, tm + hb, n), F32)])(u4, u4, dw4, dwb4)


def _conv_rows(pad_ref, w_ref, b_ref, half, at, rows):
    acc = b_ref[half] + w_ref[half, pl.ds(FFN_CONV_W - 1, 1), :] * pad_ref[half, pl.ds(at, rows), :]
    for k in range(FFN_CONV_W - 1):
        acc = acc + w_ref[half, pl.ds(k, 1), :] * pad_ref[half, pl.ds(at - (FFN_CONV_W - 1) + k, rows), :]
    return acc


def _head_specs(seq, dh, q_heads, kv_heads):
    def spec(per, base):
        def imap(b, h, g):
            f = base + g * N_HEADS + h
            return (f // per, b, f % per)
        return pl.BlockSpec((None, seq, dh), imap)
    return spec(q_heads, 0), spec(kv_heads, 0), spec(kv_heads, N_GROUPS * N_HEADS)


def _rows(start, d, blocks=1):
    size = blocks * ATT_BLOCK
    return pl.ds(start, size, stride=d) if d > 1 else pl.ds(start, size)


def _att_pad():
    return max(ATT_BLOCK * d for d in DILATIONS[:-1])


def _band_mask(first, nblk):
    keys = ATT_BLOCK if nblk == 1 else 2 * ATT_BLOCK
    shape = (ATT_BATCH, ATT_BLOCK, keys)
    qi = lax.broadcasted_iota(jnp.int32, shape, 1)
    kj = lax.broadcasted_iota(jnp.int32, shape, 2)
    if nblk == 1:
        return kj <= qi
    n = (first + lax.broadcasted_iota(jnp.int32, shape, 0)) % nblk
    return (kj >= qi) & (kj <= qi + ATT_BLOCK) & ((n > 0) | (kj >= ATT_BLOCK))


def _block_rows(idx, d, nblk):
    r, n = idx // nblk, idx % nblk
    rq = _rows(r + d * ATT_BLOCK * n, d)
    if nblk == 1:
        return rq, _rows(_att_pad() + r + d * ATT_BLOCK * n, d)
    return rq, _rows(_att_pad() + r + d * ATT_BLOCK * (n - 1), d, blocks=2)


def _bdot(a, b, ca, cb):
    return lax.dot_general(a, b, (((ca,), (cb,)), ((0,), (0,))), preferred_element_type=F32)


def attn_fwd(name, q, kv, nb, seq):
    _, T, qn = q.shape
    dh = qn * N_DEV // (N_GROUPS * N_HEADS)
    scale = 1.0 / (dh ** 0.5)
    qs, ks, vs = _head_specs(seq, dh, qn // dh, kv.shape[2] // dh)

    def body(q_ref, k_ref, v_ref, m_ref, l_ref, qf, kf, vf, *branch):
        og, lg = branch[:N_GROUPS], branch[N_GROUPS:]
        pad = _att_pad()
        qf[...] = q_ref[...].astype(F32)
        for t_ref, s_ref in ((k_ref, kf), (v_ref, vf)):
            s_ref[pl.ds(0, pad), :] = jnp.zeros((pad, dh), F32)
            s_ref[pl.ds(pad, seq), :] = t_ref[...].astype(F32)
        for g in range(N_GROUPS):
            d = DILATIONS[g]
            nblk = seq // d // ATT_BLOCK

            def blocks(it, carry, d=d, nblk=nblk, g=g):
                first = it * ATT_BATCH
                rows = [_block_rows(first + b, d, nblk) for b in range(ATT_BATCH)]
                qb = jnp.stack([qf[rq, :] for rq, _ in rows]).astype(MM)
                kb = jnp.stack([kf[rk, :] for _, rk in rows]).astype(MM)
                vb = jnp.stack([vf[rk, :] for _, rk in rows]).astype(MM)
                s = jnp.where(_band_mask(first, nblk), _bdot(qb, kb, 2, 2) * scale, NEG)
                m = jnp.max(s, axis=-1, keepdims=True)
                p = jnp.exp(s - m)
                den = jnp.sum(p, axis=-1, keepdims=True)
                o = _bdot(p.astype(MM), vb, 2, 1) / den
                lse = m + jnp.log(den)
                for b, (rq, _) in enumerate(rows):
                    og[g][rq, :] = o[b]
                    lg[g][rq, :] = jnp.broadcast_to(lse[b], (ATT_BLOCK, dh))
                return carry

            @pl.when(pl.program_id(2) == g)
            def _(blocks=blocks, d=d, nblk=nblk):
                lax.fori_loop(0, d * nblk // ATT_BATCH, blocks, 0)

        @pl.when(pl.program_id(2) == N_GROUPS - 1)
        def _():
            mx = jnp.maximum(jnp.maximum(lg[0][...], lg[1][...]), lg[2][...])
            e = [jnp.exp(lg[g][...] - mx) for g in range(N_GROUPS)]
            tot = e[0] + e[1] + e[2]
            m_ref[...] = ((e[0] * og[0][...] + e[1] * og[1][...] + e[2] * og[2][...]) / tot).astype(m_ref.dtype)
            l_ref[...] = mx + jnp.log(tot)

    return _call(body, name=name, grid=(nb, N_HEADS, N_GROUPS), in_specs=[qs, ks, vs],
                 out_specs=[pl.BlockSpec((None, seq, dh), lambda b, h, g: (0, b, h)),
                            pl.BlockSpec((seq, dh), lambda b, h, g: (b, h))],
                 out_shape=[_sds((1, T, N_HEADS * dh), MM), _sds((T, N_HEADS * dh), F32)],
                 scratch=[pltpu.VMEM((seq, dh), F32)] + [pltpu.VMEM((_att_pad() + seq, dh), F32)] * 2
                 + [pltpu.VMEM((seq, dh), F32)] * (2 * N_GROUPS))(q, kv, kv)


def loss_fwd_bwd(name, x, target):
    T, D = x.shape
    tm = _tile(T, 512)

    def body(x_ref, t_ref, dx_ref, l_ref):
        @pl.when(pl.program_id(0) == 0)
        def _():
            l_ref[...] = jnp.zeros_like(l_ref)
        err = x_ref[...] - t_ref[...]
        dx_ref[...] = err * (1.0 / D)
        l_ref[...] += 0.5 * jnp.sum(jnp.mean(err * err, axis=-1, keepdims=True), axis=0, keepdims=True)

    dx, l = _call(body, name=name, grid=(T // tm,),
                  in_specs=[pl.BlockSpec((tm, D), lambda i: (i, 0))] * 2,
                  out_specs=[pl.BlockSpec((tm, D), lambda i: (i, 0)), pl.BlockSpec((1, 1), lambda i: (0, 0))],
                  out_shape=[_sds((T, D), F32), _sds((1, 1), F32)])(x, target)
    return dx, l


def resid_norm_bwd(name, dx, y, gain):
    T, D = y.shape
    tm = _tile(T, 512)

    def body(dx_ref, y_ref, g_ref, dy_ref, dg_ref, db_ref):
        @pl.when(pl.program_id(0) == 0)
        def _():
            dg_ref[...] = jnp.zeros_like(dg_ref)
            db_ref[...] = jnp.zeros_like(db_ref)
        y = y_ref[...]
        d = dx_ref[...]
        r = lax.rsqrt(jnp.mean(y * y, axis=-1, keepdims=True) + EPS)
        yh = y * r
        dyh = d * g_ref[...]
        dy = r * (dyh - yh * jnp.mean(dyh * yh, axis=-1, keepdims=True))
        dy_ref[...] = dy.astype(dy_ref.dtype)
        dg_ref[...] += jnp.sum(d * yh, axis=0, keepdims=True)
        db_ref[...] += jnp.sum(dy, axis=0, keepdims=True)

    return _call(body, name=name, grid=(T // tm,),
                 in_specs=[pl.BlockSpec((tm, D), lambda i: (i, 0))] * 2 + [pl.BlockSpec((1, D), lambda i: (0, 0))],
                 out_specs=[pl.BlockSpec((None, tm, D), lambda i: (0, i, 0))] + [pl.BlockSpec((1, D), lambda i: (0, 0))] * 2,
                 out_shape=[_sds((1, T, D), MM), _sds((1, D), F32), _sds((1, D), F32)])(dx, y, gain.reshape(1, D))


def mm_nt(name, dy, w, layer, out_dtype):
    _, T, D = dy.shape
    _, nk, kk, _ = w.shape
    tm = _tile(T, 1024)

    def body(dy_ref, w_ref, o_ref):
        o_ref[...] = _dot_nt(dy_ref[...], w_ref[...]).astype(o_ref.dtype)

    return _call(body, name=name, grid=(T // tm, nk),
                 in_specs=[pl.BlockSpec((None, tm, D), lambda i, q: (0, i, 0)),
                           pl.BlockSpec((None, None, kk, D), lambda i, q: (layer, q, 0, 0))],
                 out_specs=pl.BlockSpec((None, tm, kk), lambda i, q: (q, i, 0)),
                 out_shape=_sds((nk, T, kk), out_dtype))(dy, w)


def mm_nt_norm_bwd(name, du, w, layer, x_in, gain, dx_res, w_t=False):
    nsh, T, n = du.shape
    D = x_in.shape[1]
    tm = _tile(T, 1024)

    def body(du_ref, w_ref, x_ref, g_ref, dr_ref, dx_ref, dg_ref, acc_ref):
        i, j = pl.program_id(0), pl.program_id(1)

        @pl.when((i == 0) & (j == 0))
        def _():
            dg_ref[...] = jnp.zeros_like(dg_ref)

        part = (_dot if w_t else _dot_nt)(du_ref[...], w_ref[...])

        @pl.when(j == 0)
        def _():
            acc_ref[...] = part

        @pl.when(j > 0)
        def _():
            acc_ref[...] += part

        @pl.when(j == nsh - 1)
        def _():
            x = x_ref[...]
            dh = acc_ref[...]
            r = lax.rsqrt(jnp.mean(x * x, axis=-1, keepdims=True) + EPS)
            xh = x * r
            dxh = dh * g_ref[...]
            dx_ref[...] = dr_ref[...] + r * (dxh - xh * jnp.mean(dxh * xh, axis=-1, keepdims=True))
            dg_ref[...] += jnp.sum(dh * xh, axis=0, keepdims=True)

    return _call(body, name=name, grid=(T // tm, nsh),
                 in_specs=[pl.BlockSpec((None, tm, n), lambda i, j: (j, i, 0)),
                           pl.BlockSpec((None, None) + w.shape[2:], lambda i, j: (j, layer, 0, 0)),
                           pl.BlockSpec((tm, D), lambda i, j: (i, 0)),
                           pl.BlockSpec((1, D), lambda i, j: (0, 0)),
                           pl.BlockSpec((tm, D), lambda i, j: (i, 0))],
                 out_specs=[pl.BlockSpec((tm, D), lambda i, j: (i, 0)), pl.BlockSpec((1, D), lambda i, j: (0, 0))],
                 out_shape=[_sds((T, D), F32), _sds((1, D), F32)],
                 scratch=[pltpu.VMEM((tm, D), F32)])(du, w, x_in, gain.reshape(1, D), dx_res)


def mm_tn(name, a, b):
    na, T, ka = a.shape
    nb, _, kb = b.shape
    nj = max(na, nb)

    def body(a_ref, b_ref, o_ref):
        o_ref[...] = _dot_tn(a_ref[...], b_ref[...]).astype(o_ref.dtype)

    return _call(body, name=name, grid=(nj,),
                 in_specs=[pl.BlockSpec((None, T, ka), (lambda j: (j, 0, 0)) if na > 1 else (lambda j: (0, 0, 0))),
                           pl.BlockSpec((None, T, kb), (lambda j: (j, 0, 0)) if nb > 1 else (lambda j: (0, 0, 0)))],
                 out_specs=pl.BlockSpec((None, ka, kb), lambda j: (j, 0, 0)),
                 out_shape=_sds((nj, ka, kb), MM))(a, b)


def ffn_conv_gate_bwd(name, u, ag, dz, dw, layer, seq):
    _, T, n = u.shape
    tm = _tile(seq, 512)
    tps = seq // tm
    hb = FFN_HALO
    _, nxt = _halo_maps(tm, hb, T)
    u4 = u.reshape(2, 4, T, n)
    dw4 = dw.reshape(2, 4, dw.shape[1], FFN_CONV_W, n)
    K = FFN_CONV_W
    te = tm + hb
    rows = _tile(hb, ROW_CHUNK_BWD)

    def body(u_ref, ag_ref, agn_ref, dz_ref, dzn_ref, w_ref, du_ref, ddw_ref, ddb_ref, uf_ref, agf_ref, dzf_ref,
             da_ref):
        i = pl.program_id(1)
        last = (i % tps) == tps - 1

        @pl.when(i == 0)
        def _():
            ddw_ref[...] = jnp.zeros_like(ddw_ref)
            ddb_ref[...] = jnp.zeros_like(ddb_ref)

        for half in range(2):
            uf_ref[half] = u_ref[half].astype(F32)
            agf_ref[half, pl.ds(0, tm), :] = ag_ref[half].astype(F32)
            agf_ref[half, pl.ds(tm, hb), :] = agn_ref[half].astype(F32)
        dzf_ref[pl.ds(0, tm), :] = dz_ref[...].astype(F32)
        dzf_ref[pl.ds(tm, hb), :] = jnp.where(last, 0.0, dzn_ref[...].astype(F32))
        for r0 in range(0, te, rows):
            a, g = agf_ref[0, pl.ds(r0, rows), :], agf_ref[1, pl.ds(r0, rows), :]
            dzc = dzf_ref[pl.ds(r0, rows), :]
            sg = _sigmoid(g)
            da_ref[0, pl.ds(r0, rows), :] = dzc * (g * sg)
            da_ref[1, pl.ds(r0, rows), :] = dzc * a * (sg * (1.0 + g * (1.0 - sg)))
        for half in range(2):
            tap_acc = [jnp.zeros((rows, n), F32) for _ in range(K)]
            bias_acc = jnp.zeros((rows, n), F32)
            for r0 in range(0, tm, rows):
                x = uf_ref[half, pl.ds(r0, rows), :]
                acc = None
                for k in range(K):
                    d = da_ref[half, pl.ds(r0 + K - 1 - k, rows), :]
                    acc = w_ref[half, pl.ds(k, 1), :] * d if acc is None else acc + w_ref[half, pl.ds(k, 1), :] * d
                    tap_acc[k] = tap_acc[k] + d * x
                    if k == K - 1:
                        bias_acc = bias_acc + d
                du_ref[half, pl.ds(r0, rows), :] = acc.astype(du_ref.dtype)
            for k in range(K):
                ddw_ref[half, pl.ds(k, 1), :] += jnp.sum(tap_acc[k], axis=0, keepdims=True)
            ddb_ref[half] += jnp.sum(bias_acc, axis=0, keepdims=True)

    du, ddw, ddb = _call(
        body, name=name, grid=(4, T // tm),
        in_specs=[pl.BlockSpec((2, None, tm, n), lambda j, i: (0, j, i, 0)),
                  pl.BlockSpec((2, None, tm, n), lambda j, i: (0, j, i, 0)),
                  pl.BlockSpec((2, None, hb, n), lambda j, i: (0, j, nxt(i), 0)),
                  pl.BlockSpec((None, tm, n), lambda j, i: (j, i, 0)),
                  pl.BlockSpec((None, hb, n), lambda j, i: (j, nxt(i), 0)),
                  pl.BlockSpec((2, None, None, K, n), lambda j, i: (0, j, layer, 0, 0))],
        out_specs=[pl.BlockSpec((2, None, tm, n), lambda j, i: (0, j, i, 0)),
                   pl.BlockSpec((2, None, K, n), lambda j, i: (0, j, 0, 0)),
                   pl.BlockSpec((2, None, 1, n), lambda j, i: (0, j, 0, 0))],
        out_shape=[_sds((2, 4, T, n), MM), _sds((2, 4, K, n), F32), _sds((2, 4, 1, n), F32)],
        scratch=[pltpu.VMEM((2, tm, n), F32), pltpu.VMEM((2, te, n), F32), pltpu.VMEM((te, n), F32),
                 pltpu.VMEM((2, te, n), F32)],
    )(u4, ag, ag, dz, dz, dw4)
    return du.reshape(8, T, n), ddw, ddb


def ln_silu_bwd(name, ds, c, g, b):
    T, D = c.shape
    tm = _tile(T, 512)

    def body(ds_ref, c_ref, g_ref, b_ref, dc_ref, dg_ref, db_ref):
        @pl.when(pl.program_id(0) == 0)
        def _():
            dg_ref[...] = jnp.zeros_like(dg_ref)
            db_ref[...] = jnp.zeros_like(db_ref)
        cf = c_ref[...]
        mu = jnp.mean(cf, axis=-1, keepdims=True)
        xc = cf - mu
        r = lax.rsqrt(jnp.mean(xc * xc, axis=-1, keepdims=True) + EPS)
        xh = xc * r
        t = xh * g_ref[...] + b_ref[...]
        sg = _sigmoid(t)
        dt = ds_ref[...] * (sg * (1.0 + t * (1.0 - sg)))
        dg_ref[...] += jnp.sum(dt * xh, axis=0, keepdims=True)
        db_ref[...] += jnp.sum(dt, axis=0, keepdims=True)
        dxh = dt * g_ref[...]
        dc_ref[...] = r * (dxh - jnp.mean(dxh, axis=-1, keepdims=True)
                           - xh * jnp.mean(dxh * xh, axis=-1, keepdims=True))

    vec = pl.BlockSpec((1, D), lambda i: (0, 0))
    return _call(body, name=name, grid=(T // tm,),
                 in_specs=[pl.BlockSpec((None, tm, D), lambda i: (0, i, 0)), pl.BlockSpec((tm, D), lambda i: (i, 0)),
                           vec, vec],
                 out_specs=[pl.BlockSpec((tm, D), lambda i: (i, 0)), vec, vec],
                 out_shape=[_sds((T, D), F32), _sds((1, D), F32), _sds((1, D), F32)])(
                     ds, c, g.reshape(1, D), b.reshape(1, D))


def cm_glu_conv_bwd(name, u, dc, dw, layer, seq):
    _, T, n = u.shape
    ct = dw.shape[-1]
    per = n // ct
    nct = 4 * per
    tm = _tile(seq, 512)
    tps = seq // tm
    hb = CONV_HALO
    prev, nxt = _halo_maps(tm, hb, T)
    u4 = u.reshape(2, 4, T, n)
    K = CONV_W

    def body(u_ref, up_ref, dc_ref, dcn_ref, w_ref, du_ref, ddw_ref, ddb_ref, dbi_ref, padp_ref, padd_ref):
        i = pl.program_id(1)
        first = (i % tps) == 0
        last = (i % tps) == tps - 1

        @pl.when(i == 0)
        def _():
            ddw_ref[...] = jnp.zeros_like(ddw_ref)
            ddb_ref[...] = jnp.zeros_like(ddb_ref)
            dbi_ref[...] = jnp.zeros_like(dbi_ref)

        um = u_ref[...].astype(F32)
        uh = up_ref[...].astype(F32)
        sg = _sigmoid(um[1])
        padp_ref[pl.ds(hb, tm), :] = um[0] * sg
        padp_ref[pl.ds(0, hb), :] = jnp.where(first, 0.0, uh[0] * _sigmoid(uh[1]))
        dcm = dc_ref[...]
        padd_ref[pl.ds(0, tm), :] = dcm
        padd_ref[pl.ds(tm, hb), :] = jnp.where(last, 0.0, dcn_ref[...])
        dp = jnp.zeros((tm, ct), F32)
        for k in range(K):
            dp = dp + w_ref[pl.ds(k, 1), :] * padd_ref[pl.ds(K - 1 - k, tm), :]
            ddw_ref[pl.ds(k, 1), :] += jnp.sum(dcm * padp_ref[pl.ds(hb - (K - 1) + k, tm), :], axis=0, keepdims=True)
        ddb_ref[...] += jnp.sum(dcm, axis=0, keepdims=True)
        dv = dp * sg
        dg = dp * um[0] * sg * (1.0 - sg)
        du_ref[0] = dv.astype(du_ref.dtype)
        du_ref[1] = dg.astype(du_ref.dtype)
        dbi_ref[0] += jnp.sum(dv, axis=0, keepdims=True)
        dbi_ref[1] += jnp.sum(dg, axis=0, keepdims=True)

    du, ddw, ddb, dbi = _call(
        body, name=name, grid=(nct, T // tm),
        in_specs=[pl.BlockSpec((2, None, tm, ct), lambda c, i: (0, c // per, i, c % per)),
                  pl.BlockSpec((2, None, hb, ct), lambda c, i: (0, c // per, prev(i), c % per)),
                  pl.BlockSpec((tm, ct), lambda c, i: (i, c)),
                  pl.BlockSpec((hb, ct), lambda c, i: (nxt(i), c)),
                  pl.BlockSpec((None, None, K, ct), lambda c, i: (c, layer, 0, 0))],
        out_specs=[pl.BlockSpec((2, None, tm, ct), lambda c, i: (0, c // per, i, c % per)),
                   pl.BlockSpec((None, K, ct), lambda c, i: (c, 0, 0)),
                   pl.BlockSpec((None, 1, ct), lambda c, i: (c, 0, 0)),
                   pl.BlockSpec((2, None, 1, ct), lambda c, i: (0, c // per, 0, c % per))],
        out_shape=[_sds((2, 4, T, n), MM), _sds((nct, K, ct), F32), _sds((nct, 1, ct), F32), _sds((2, 4, 1, n), F32)],
        scratch=[pltpu.VMEM((tm + hb, ct), F32), pltpu.VMEM((tm + hb, ct), F32)])(u4, u4, dc, dc, dw)
    return du.reshape(8, T, n), ddw, ddb, dbi


def attn_bwd(name, q, kv, dm, merged, lse, dkv_prev, nb, seq):
    _, T, qn = q.shape
    kn = kv.shape[2]
    dh = qn * N_DEV // (N_GROUPS * N_HEADS)
    scale = 1.0 / (dh ** 0.5)
    qs, ks, vs = _head_specs(seq, dh, qn // dh, kn // dh)
    has_prev = dkv_prev is not None
    n_in = 6 + (1 if has_prev else 0)

    def body(*refs):
        q_ref, k_ref, v_ref, dm_ref, mg_ref, l_ref = refs[:6]
        pkv_ref = refs[6] if has_prev else None
        dq_ref, dkv_ref = refs[n_in:n_in + 2]
        qf, kf, vf, dqf, dkf, dvf, dlt = refs[n_in + 2:]
        pad = _att_pad()

        @pl.when(pl.program_id(2) == 0)
        def _():
            dlt[...] = jnp.broadcast_to(
                jnp.sum(dm_ref[...] * mg_ref[...].astype(F32), axis=-1, keepdims=True), (seq, dh))

        qf[...] = q_ref[...].astype(F32)
        for t_ref, s_ref in ((k_ref, kf), (v_ref, vf)):
            s_ref[pl.ds(0, pad), :] = jnp.zeros((pad, dh), F32)
            s_ref[pl.ds(pad, seq), :] = t_ref[...].astype(F32)
        dkf[...] = jnp.zeros_like(dkf)
        dvf[...] = jnp.zeros_like(dvf)
        for g in range(N_GROUPS):
            d = DILATIONS[g]
            nblk = seq // d // ATT_BLOCK

            def blocks(it, carry, d=d, nblk=nblk):
                first = it * ATT_BATCH
                rows = [_block_rows(first + b, d, nblk) for b in range(ATT_BATCH)]
                qb = jnp.stack([qf[rq, :] for rq, _ in rows]).astype(MM)
                dmb = jnp.stack([dm_ref[rq, :] for rq, _ in rows]).astype(MM)
                lse = jnp.stack([l_ref[rq, :][:, :1] for rq, _ in rows])
                delta = jnp.stack([dlt[rq, :][:, :1] for rq, _ in rows])
                kb = jnp.stack([kf[rk, :] for _, rk in rows]).astype(MM)
                vb = jnp.stack([vf[rk, :] for _, rk in rows]).astype(MM)
                s = jnp.where(_band_mask(first, nblk), _bdot(qb, kb, 2, 2) * scale, NEG)
                p = jnp.exp(s - lse)
                dsc = (p * (_bdot(dmb, vb, 2, 2) - delta) * scale).astype(MM)
                dv = _bdot(p.astype(MM), dmb, 1, 1)
                dk = _bdot(dsc, qb, 1, 1)
                dq = _bdot(dsc, kb, 2, 1)
                for b, (rq, rk) in enumerate(rows):
                    dqf[rq, :] = dq[b]
                    dkf[rk, :] += dk[b]
                    dvf[rk, :] += dv[b]
                return carry

            @pl.when(pl.program_id(2) == g)
            def _(blocks=blocks, d=d, nblk=nblk):
                lax.fori_loop(0, d * nblk // ATT_BATCH, blocks, 0)

        dq_ref[...] = dqf[...].astype(dq_ref.dtype)
        dk, dv = dkf[pl.ds(pad, seq), :], dvf[pl.ds(pad, seq), :]
        if has_prev:
            dk, dv = dk + pkv_ref[0].astype(F32), dv + pkv_ref[1].astype(F32)
        dkv_ref[0] = dk.astype(dkv_ref.dtype)
        dkv_ref[1] = dv.astype(dkv_ref.dtype)

    per = kn // dh

    def both(b, h, g):
        f = g * N_HEADS + h
        return (0, f // per, b, f % per)

    kv_spec = pl.BlockSpec((2, None, seq, dh), both)
    full = pl.BlockSpec((None, seq, dh), lambda b, h, g: (0, b, h))
    in_specs = [qs, ks, vs, full, full, pl.BlockSpec((seq, dh), lambda b, h, g: (b, h))]
    args = [q, kv, kv, dm, merged, lse]
    if has_prev:
        in_specs.append(kv_spec)
        args.append(dkv_prev.reshape(2, N_DEV // 2, T, kn))
    short, padded = pltpu.VMEM((seq, dh), F32), pltpu.VMEM((_att_pad() + seq, dh), F32)
    dq, dkv = _call(body, name=name, grid=(nb, N_HEADS, N_GROUPS), in_specs=in_specs, out_specs=[qs, kv_spec],
                    out_shape=[_sds(q.shape, MM), _sds((2, N_DEV // 2, T, kn), MM)],
                    scratch=[short, padded, padded, short, padded, padded, short])(*args)
    return dq, dkv.reshape(N_DEV, T, kn)


def _adamw_math(w, g, m, v):
    m = ADAM_B1 * m + (1.0 - ADAM_B1) * g
    v = ADAM_B2 * v + (1.0 - ADAM_B2) * (g * g)
    m_hat = m / (1.0 - ADAM_B1 ** ADAM_STEP)
    v_hat = v / (1.0 - ADAM_B2 ** ADAM_STEP)
    delta = -ADAM_LR * (m_hat / (jnp.sqrt(v_hat) + ADAM_EPS) + ADAM_WD * w)
    return delta, m, v


def adamw_sum(name, w, m, v, parts, after):
    L, R, C = w.shape
    tr = _tile(R, 256)

    def body(*refs):
        w_ref, m_ref, v_ref = refs[:3]
        p_refs = refs[3:3 + L]
        g_ref, d_ref, nm_ref, nv_ref = refs[4 + L:]
        for l in range(L):
            @pl.when(pl.program_id(0) == l)
            def _(p_ref=p_refs[l]):
                g = p_ref[0].astype(F32)
                for k in range(1, N_DEV):
                    g = g + p_ref[k].astype(F32)
                g_ref[...] = g
                d_ref[...], nm_ref[...], nv_ref[...] = _adamw_math(w_ref[...], g, m_ref[...], v_ref[...])

    blk = pl.BlockSpec((None, tr, C), lambda l, i: (l, i, 0))
    part = lambda k: pl.BlockSpec((N_DEV, tr, C), lambda l, i: (0, jnp.where(l == k, i, 0), 0))
    return _call(body, name=name, grid=(L, R // tr),
                 in_specs=[blk, blk, blk] + [part(k) for k in range(L)] + [pl.BlockSpec(memory_space=pl.ANY)],
                 out_specs=[blk] * 4, out_shape=[_sds((L, R, C), F32)] * 4)(w, m, v, *parts, after)


def sum_partials(name, parts):
    _, R, C = parts.shape
    tr = _tile(R, 512)

    def body(p_ref, o_ref):
        g = p_ref[0]
        for k in range(1, N_DEV):
            g = g + p_ref[k]
        o_ref[...] = g

    return _call(body, name=name, grid=(R // tr,),
                 in_specs=[pl.BlockSpec((N_DEV, tr, C), lambda i: (0, i, 0))],
                 out_specs=pl.BlockSpec((tr, C), lambda i: (i, 0)), out_shape=_sds((R, C), F32))(parts)


def adamw_small(name, w, g, m, v):
    R, C = w.shape
    tr = _tile(R, 512)

    def body(w_ref, g_ref, m_ref, v_ref, d_ref, nm_ref, nv_ref):
        d_ref[...], nm_ref[...], nv_ref[...] = _adamw_math(w_ref[...], g_ref[...], m_ref[...], v_ref[...])

    blk = pl.BlockSpec((tr, C), lambda i: (i, 0))
    return _call(body, name=name, grid=(R // tr,), in_specs=[blk] * 4, out_specs=[blk] * 3,
                 out_shape=[_sds((R, C), F32)] * 3)(w, g, m, v)


def _pack(arrays):
    pieces = []
    for a in arrays:
        f = a.reshape(-1).astype(F32)
        pieces.append(jnp.pad(f, (0, (-f.shape[0]) % PACK)))
    return jnp.concatenate(pieces).reshape(-1, 128)


def _unpack(flat, shapes):
    out, off = [], 0
    f = flat.reshape(-1)
    for s in shapes:
        size = 1
        for d in s:
            size *= d
        out.append(f[off:off + size].reshape(s))
        off += size + (-size) % PACK
    return out


def kernel(x, mix_pre_g, mix_post_g, ffn_pre_g, ffn_post_g, cm_w_in, cm_b_in, cm_dw, cm_dw_b, cm_ln_g, cm_ln_b, cm_w_out, cm_b_out, kv_norm_g, w_kv, w_q, w_o, ffn_w_in, ffn_dw, ffn_dw_b, ffn_w_out, loss_target, m_mix_pre_g, m_mix_post_g, m_ffn_pre_g, m_ffn_post_g, m_cm_w_in, m_cm_b_in, m_cm_dw, m_cm_dw_b, m_cm_ln_g, m_cm_ln_b, m_cm_w_out, m_cm_b_out, m_kv_norm_g, m_w_kv, m_w_q, m_w_o, m_ffn_w_in, m_ffn_dw, m_ffn_dw_b, m_ffn_w_out, v_mix_pre_g, v_mix_post_g, v_ffn_pre_g, v_ffn_post_g, v_cm_w_in, v_cm_b_in, v_cm_dw, v_cm_dw_b, v_cm_ln_g, v_cm_ln_b, v_cm_w_out, v_cm_b_out, v_kv_norm_g, v_w_kv, v_w_q, v_w_o, v_ffn_w_in, v_ffn_dw, v_ffn_dw_b, v_ffn_w_out):
    nb, seq, D = x.shape
    T = nb * seq
    me = _my_index()
    n_b = DEPTH - N_A

    nf = ffn_w_in.shape[-1]
    t_ = lambda t: jnp.swapaxes(t, 1, 2)
    fin_t, m_fin_t, v_fin_t = t_(ffn_w_in), t_(m_ffn_w_in), t_(v_ffn_w_in)

    stages = [(part, i) for i in range(DEPTH) for part in ("mix", "ffn")]
    stages.insert(stages.index(("ffn", N_A - 1)) + 1, ("kv", N_A - 1))

    def stage_sources(stage):
        part, i = stage
        if part == "ffn":
            src = {"fin": fin_t[i], "fout": ffn_w_out[i]}
        elif part == "kv":
            src = {"kv": w_kv}
        elif i < N_A:
            src = {"cin": cm_w_in[i], "cout": cm_w_out[i]}
        else:
            src = {"q": w_q[i - N_A], "o": w_o[i - N_A]}
        return {k: t.astype(MM) for k, t in src.items()}

    def begin_gather(stage, after):
        src = stage_sources(stage)
        names, arrays = list(src), list(src.values())
        tag = f"{stage[0]}{stage[1]}"
        lands = place_own(f"gather_own_{tag}", "gather", arrays)
        handle, token = exchange_begin(f"gather_begin_{tag}", "gather", arrays, lands, after)
        return (names, handle), token

    def end_gather(stage, pending, after):
        names, handle = pending
        W = dict(zip(names, exchange_end(f"gather_end_{stage[0]}{stage[1]}", handle, after)))
        for k in W:
            if k in ("cout", "o"):
                W[k] = W[k].reshape(1, 1, D, D)
            elif k == "fout":
                W[k] = W[k].reshape(1, 4, nf, D)
            else:
                W[k] = W[k][:, None]
        return W

    small = [cm_b_in[:, None, :], cm_dw, cm_dw_b[:, None, :], cm_ln_g, cm_ln_b, cm_b_out, ffn_dw]
    Bcin, DWc, DWBc, LNg, LNb, Bcout, DWf = all_gather("gather_small", small, [False] * len(small))
    LNg = jnp.swapaxes(LNg, 0, 1).reshape(N_A, D)
    LNb = jnp.swapaxes(LNb, 0, 1).reshape(N_A, D)
    Bcout = jnp.swapaxes(Bcout, 0, 1).reshape(N_A, D)
    DWBf = ffn_dw_b.reshape(DEPTH, N_DEV, 1, nf)
    zero_bias = jnp.zeros((D,), F32)

    xs = x.reshape(T, D)
    sv = []
    kv = hkv = None
    pending, _ = begin_gather(stages[0], xs)
    Ws = {stages[0]: end_gather(stages[0], pending, xs)}
    sv = [{} for _ in range(DEPTH)]
    for idx, stage in enumerate(stages):
        part, i = stage
        L, W = sv[i], Ws[stage]
        gain = {"mix": mix_pre_g[i], "ffn": ffn_pre_g[i], "kv": kv_norm_g}[part]
        following = stages[idx + 1] if idx + 1 < len(stages) else None
        if following is not None:
            pending, token = begin_gather(following, xs)
            gain = gain + token[0, 0]
        if part == "mix":
            L["x_in"] = xs
            if i < N_A:
                L["u"], L["h"] = norm_mm(f"cm_in_{i}", xs, gain, W["cin"], 0, Bcin[:, i:i + 1])
                L["c"] = cm_glu_conv(f"cm_conv_{i}", L["u"], DWc, DWBc, i, seq)
                L["s"] = ln_silu(f"cm_ln_{i}", L["c"], LNg[i], LNb[i])
                L["y"], xs = mm_resid_norm(f"cm_out_{i}", L["s"], W["cout"], 0, Bcout[i], xs, mix_post_g[i])
            else:
                L["q"], L["h"] = norm_mm(f"attn_q_{i}", xs, gain, W["q"], 0)
                L["mg"], L["lse"] = attn_fwd(f"attn_{i}", L["q"], kv, nb, seq)
                L["y"], xs = mm_resid_norm(f"attn_out_{i}", L["mg"], W["o"], 0, zero_bias, xs, mix_post_g[i])
            L["x1"] = xs
        elif part == "ffn":
            L["uf"], L["hf"] = norm_mm(f"ffn_in_{i}", xs, gain, W["fin"], 0, w_t=True)
            L["z"], L["ag"] = ffn_conv_gate(f"ffn_conv_{i}", L["uf"], DWf, DWBf, i, seq)
            L["yf"], xs = mm_resid_norm(f"ffn_out_{i}", L["z"], W["fout"], 0, zero_bias, xs, ffn_post_g[i])
        else:
            kv, hkv = norm_mm("kv_proj", xs, gain, W["kv"], 0)
        if following is not None:
            Ws[following] = end_gather(following, pending, kv if part == "kv" else xs)
    dx, loss_part = loss_fwd_bwd("loss", xs, loss_target.reshape(T, D))

    g_mix_pre, g_mix_post, g_ffn_pre, g_ffn_post = [None] * DEPTH, [None] * DEPTH, [None] * DEPTH, [None] * DEPTH
    g_ffn_dw, g_ffn_dwb = [None] * DEPTH, [None] * DEPTH
    g_cbin, g_cdw, g_cdwb, g_lng, g_lnb, g_cbout = ([None] * N_A for _ in range(6))
    g_kvn = dkv = None
    landed = [{} for _ in range(DEPTH)]
    in_flight = token = None
    for stage in reversed(stages):
        part, i = stage
        L, W = sv[i], Ws[stage]
        gain = {"mix": mix_post_g[i], "ffn": ffn_post_g[i], "kv": kv_norm_g}[part]
        if token is not None:
            gain = gain + token[0, 0]
        send = {}
        if part == "kv":
            send["kv"] = mm_tn("kv_wg", hkv[None], dkv)
            dx, g_kvn = mm_nt_norm_bwd("kv_bwd", dkv, W["kv"], 0, sv[i + 1]["x_in"], gain, dx)
        elif part == "ffn":
            dyf, g_ffn_post[i], _ = resid_norm_bwd(f"ffn_post_bwd_{i}", dx, L["yf"], gain)
            dz = mm_nt(f"ffn_out_bwd_{i}", dyf, W["fout"], 0, MM)
            send["fout"] = mm_tn(f"ffn_out_wg_{i}", L["z"], dyf).reshape(N_DEV, nf // 2, D)
            duf, ddw, ddwb = ffn_conv_gate_bwd(f"ffn_conv_bwd_{i}", L["uf"], L["ag"], dz, DWf, i, seq)
            g_ffn_dw[i], g_ffn_dwb[i] = ddw.reshape(N_DEV, FFN_CONV_W, nf), ddwb.reshape(-1)
            send["fin"] = mm_tn(f"ffn_in_wg_{i}", duf, L["hf"][None])
            dx, g_ffn_pre[i] = mm_nt_norm_bwd(f"ffn_in_bwd_{i}", duf, W["fin"], 0, L["x1"], ffn_pre_g[i], dx, w_t=True)
        else:
            dy, g_mix_post[i], dyb = resid_norm_bwd(f"mix_post_bwd_{i}", dx, L["y"], gain)
            if i >= N_A:
                dm = mm_nt(f"attn_out_bwd_{i}", dy, W["o"], 0, F32)
                send["o"] = mm_tn(f"attn_out_wg_{i}", L["mg"], dy).reshape(N_DEV, D // N_DEV, D)
                dq, dkv = attn_bwd(f"attn_bwd_{i}", L["q"], kv, dm, L["mg"], L["lse"], dkv, nb, seq)
                send["q"] = mm_tn(f"attn_q_wg_{i}", L["h"][None], dq)
                dx, g_mix_pre[i] = mm_nt_norm_bwd(f"attn_q_bwd_{i}", dq, W["q"], 0, L["x_in"], mix_pre_g[i], dx)
            else:
                g_cbout[i] = dyb
                ds = mm_nt(f"cm_out_bwd_{i}", dy, W["cout"], 0, F32)
                send["cout"] = mm_tn(f"cm_out_wg_{i}", L["s"], dy).reshape(N_DEV, D // N_DEV, D)
                dc, g_lng[i], g_lnb[i] = ln_silu_bwd(f"cm_ln_bwd_{i}", ds, L["c"], LNg[i], LNb[i])
                du, g_cdw[i], g_cdwb[i], dbi = cm_glu_conv_bwd(f"cm_conv_bwd_{i}", L["u"], dc, DWc, i, seq)
                g_cbin[i] = dbi.reshape(N_DEV, -1)
                send["cin"] = mm_tn(f"cm_in_wg_{i}", L["h"][None], du)
                dx, g_mix_pre[i] = mm_nt_norm_bwd(f"cm_in_bwd_{i}", du, W["cin"], 0, L["x_in"], mix_pre_g[i], dx)
        if in_flight is not None:
            (p, j), names, handle = in_flight
            landed[j].update(zip(names, exchange_end(f"scatter_end_{p}{j}", handle, dx)))
        names, arrays = list(send), list(send.values())
        lands = place_own(f"scatter_own_{part}{i}", "scatter", arrays)
        handle, token = exchange_begin(f"scatter_begin_{part}{i}", "scatter", arrays, lands, dx)
        in_flight = (stage, names, handle)
    grad_x = dx.reshape(nb, seq, D)

    rep_names = ["mix_pre_g", "mix_post_g", "ffn_pre_g", "ffn_post_g", "kv_norm_g", "ffn_dw_b"]
    rep_parts = [jnp.concatenate(g_mix_pre), jnp.concatenate(g_mix_post), jnp.concatenate(g_ffn_pre),
                 jnp.concatenate(g_ffn_post), g_kvn.reshape(-1), jnp.stack(g_ffn_dwb)]
    rep_w = [mix_pre_g, mix_post_g, ffn_pre_g, ffn_post_g, kv_norm_g, ffn_dw_b]
    rep_m = [m_mix_pre_g, m_mix_post_g, m_ffn_pre_g, m_ffn_post_g, m_kv_norm_g, m_ffn_dw_b]
    rep_v = [v_mix_pre_g, v_mix_post_g, v_ffn_pre_g, v_ffn_post_g, v_kv_norm_g, v_ffn_dw_b]
    sh_names = ["ffn_dw", "cm_b_in", "cm_dw", "cm_dw_b", "cm_ln_g", "cm_ln_b", "cm_b_out"]
    own = lambda per_layer, shard: jnp.stack([p.reshape((N_DEV,) + shard) for p in per_layer], axis=1)
    sh_parts = [own(g_ffn_dw, ffn_dw.shape[1:]), own(g_cbin, cm_b_in.shape[1:]), own(g_cdw, cm_dw.shape[1:]),
                own(g_cdwb, cm_dw_b.shape[1:]), own(g_lng, cm_ln_g.shape[1:]), own(g_lnb, cm_ln_b.shape[1:]),
                own(g_cbout, cm_b_out.shape[1:])]
    sh_w = [ffn_dw, cm_b_in, cm_dw, cm_dw_b, cm_ln_g, cm_ln_b, cm_b_out]
    sh_m = [m_ffn_dw, m_cm_b_in, m_cm_dw, m_cm_dw_b, m_cm_ln_g, m_cm_ln_b, m_cm_b_out]
    sh_v = [v_ffn_dw, v_cm_b_in, v_cm_dw, v_cm_dw_b, v_cm_ln_g, v_cm_ln_b, v_cm_b_out]
    rep_pack = _pack([loss_part] + rep_parts)
    sh_pack = jnp.stack([_pack([p[k] for p in sh_parts]) for k in range(N_DEV)])
    n_rep = rep_pack.shape[0]
    small = []
    for kind, pack in (("gather", rep_pack), ("scatter", sh_pack)):
        lands = place_own(f"{kind}_own_small", kind, [pack])
        handle, token = exchange_begin(f"{kind}_begin_small", kind, [pack], lands, dx)
        small.append((kind, handle))

    def big_update(name, w, m, v, key, layers, after):
        as3 = lambda t: t.reshape((-1,) + t.shape[-2:])
        outs = adamw_sum(name, as3(w), as3(m), as3(v), [landed[i][key] for i in layers], after)
        return [t.reshape(w.shape) for t in outs]

    conf, attn = range(N_A), range(N_A, DEPTH)
    upd = {}
    upd["ffn_w_in"] = [t_(t) for t in big_update("adam_ffn_w_in", fin_t, m_fin_t, v_fin_t, "fin", range(DEPTH), token)]
    upd["ffn_w_out"] = big_update("adam_ffn_w_out", ffn_w_out, m_ffn_w_out, v_ffn_w_out, "fout", range(DEPTH), token)
    upd["w_kv"] = big_update("adam_w_kv", w_kv, m_w_kv, v_w_kv, "kv", [N_A - 1], token)
    upd["w_q"] = big_update("adam_w_q", w_q, m_w_q, v_w_q, "q", attn, token)
    upd["w_o"] = big_update("adam_w_o", w_o, m_w_o, v_w_o, "o", attn, upd["w_q"][0])
    (p, j), names, handle = in_flight
    landed[j].update(zip(names, exchange_end(f"scatter_end_{p}{j}", handle, upd["w_o"][0])))
    upd["cm_w_in"] = big_update("adam_cm_w_in", cm_w_in, m_cm_w_in, v_cm_w_in, "cin", conf, token)
    upd["cm_w_out"] = big_update("adam_cm_w_out", cm_w_out, m_cm_w_out, v_cm_w_out, "cout", conf, upd["cm_w_in"][0])
    (rep_landed,), (sh_landed,) = (exchange_end(f"{kind}_end_small", handle, upd["cm_w_out"][0])
                                   for kind, handle in small)
    rep_sum = sum_partials("sum_small_rep", rep_landed)
    sh_sum = sum_partials("sum_small_sh", sh_landed)
    rep_shapes = [(1, 1)] + [w.shape for w in rep_w]
    sh_shapes = [w.shape for w in sh_w]
    g_small = jnp.concatenate([rep_sum, sh_sum])
    pad1 = jnp.zeros((1, 1), F32)
    d_s, m_s, v_s = adamw_small("adam_small", jnp.concatenate([_pack([pad1] + rep_w), _pack(sh_w)]), g_small,
                                jnp.concatenate([_pack([pad1] + rep_m), _pack(sh_m)]),
                                jnp.concatenate([_pack([pad1] + rep_v), _pack(sh_v)]))
    split = lambda t: (_unpack(t[:n_rep], rep_shapes), _unpack(t[n_rep:], sh_shapes))
    for (rep_t, sh_t), slot in zip([split(g_small), split(d_s), split(m_s), split(v_s)], range(4)):
        if slot == 0:
            loss = rep_t[0].reshape(())
        for name, t in zip(rep_names, rep_t[1:]):
            upd.setdefault(name, [None] * 4)[slot] = t
        for name, t in zip(sh_names, sh_t):
            upd.setdefault(name, [None] * 4)[slot] = t

    order = ["mix_pre_g", "mix_post_g", "ffn_pre_g", "ffn_post_g", "cm_w_in", "cm_b_in", "cm_dw", "cm_dw_b", "cm_ln_g",
             "cm_ln_b", "cm_w_out", "cm_b_out", "kv_norm_g", "w_kv", "w_q", "w_o", "ffn_w_in", "ffn_dw", "ffn_dw_b",
             "ffn_w_out"]
    return (loss, grad_x, *[upd[n][0] for n in order], *[upd[n][1] for n in order],
            *[upd[n][2] for n in order], *[upd[n][3] for n in order])
```

```python
import functools

import jax
import jax.numpy as jnp
from jax import lax
from jax.experimental import pallas as pl
from jax.experimental.pallas import tpu as pltpu

N_DEV = 8
N_A = 2
DEPTH = 4
N_HEADS = 8
N_GROUPS = 3
DILATIONS = (1, 4, 16)
ATT_BLOCK = 128
ATT_BATCH = 8
CONV_W = 31
FFN_CONV_W = 3
CONV_HALO = 32
FFN_HALO = 16
ROW_CHUNK_FWD = 16
ROW_CHUNK_BWD = 8
EPS = 1e-6
NEG = -1e30
ADAM_LR, ADAM_B1, ADAM_B2, ADAM_EPS, ADAM_WD, ADAM_STEP = 0.001, 0.9, 0.999, 1e-08, 0.01, 10
MM = jnp.bfloat16
F32 = jnp.float32
VMEM_LIMIT_BYTES = 56 * 1024 * 1024
PACK = 1024
MESH_ID = pl.DeviceIdType.MESH

_pallas = pl.pallas_call


def _call(body, *, name, out_shape, grid=(), in_specs=None, out_specs=None, scratch=()):
    return _pallas(body, name=name, out_shape=out_shape, grid=grid, in_specs=in_specs, out_specs=out_specs,
                   scratch_shapes=list(scratch),
                   compiler_params=pltpu.CompilerParams(vmem_limit_bytes=VMEM_LIMIT_BYTES))


def _tile(n, pref):
    if n <= pref:
        return n
    t = pref - pref % 8
    while n % t:
        t -= 8
    assert t > 0, (n, pref)
    return t


def _sds(shape, dtype):
    return jax.ShapeDtypeStruct(tuple(shape), dtype)


def _dot(a, b):
    return jnp.dot(a, b, preferred_element_type=F32)


def _dot_nt(a, b):
    return lax.dot_general(a, b, (((1,), (1,)), ((), ())), preferred_element_type=F32)


def _dot_tn(a, b):
    return lax.dot_general(a, b, (((0,), (0,)), ((), ())), preferred_element_type=F32)


def _sigmoid(x):
    return 1.0 / (1.0 + jnp.exp(-x))


def _my_index():
    return 4 * lax.axis_index("x") + 2 * lax.axis_index("y") + lax.axis_index("c")


def _exchange(name, arrays, out_shapes, pieces, src_of, dst_of):
    n = len(arrays)
    base = [sum(pieces[:a]) for a in range(n)]
    total = sum(pieces)

    def body(*refs):
        ins, outs = refs[:n], refs[n:2 * n]
        send_sems, recv_sems, local_sems = refs[2 * n:]
        x, y, c = lax.axis_index("x"), lax.axis_index("y"), lax.axis_index("c")
        me = 4 * x + 2 * y + c
        copies = []
        for a in range(n):
            for k, (s, d) in enumerate(zip(src_of(a, ins[a], me), dst_of(a, outs[a], me))):
                cp = pltpu.make_async_copy(s, d, local_sems.at[base[a] + k])
                cp.start()
                copies.append(cp)
        remote = []
        for m in range(1, N_DEV):
            px, py, pc = x ^ (m >> 2), y ^ ((m >> 1) & 1), c ^ (m & 1)
            peer = 4 * px + 2 * py + pc
            for a in range(n):
                for k, (s, d) in enumerate(zip(src_of(a, ins[a], peer), dst_of(a, outs[a], me))):
                    cp = pltpu.make_async_remote_copy(src_ref=s, dst_ref=d, send_sem=send_sems.at[base[a] + k, m - 1],
                                                      recv_sem=recv_sems.at[base[a] + k, m - 1],
                                                      device_id=(px, py, pc), device_id_type=MESH_ID)
                    cp.start()
                    remote.append(cp)
        for cp in copies:
            cp.wait()
        for cp in remote:
            cp.wait_send()
        for m in range(1, N_DEV):
            px, py, pc = x ^ (m >> 2), y ^ ((m >> 1) & 1), c ^ (m & 1)
            peer = 4 * px + 2 * py + pc
            for a in range(n):
                for k, (s, d) in enumerate(zip(src_of(a, ins[a], me), dst_of(a, outs[a], peer))):
                    pltpu.make_async_remote_copy(src_ref=s, dst_ref=d, send_sem=send_sems.at[base[a] + k, m - 1],
                                                 recv_sem=recv_sems.at[base[a] + k, m - 1], device_id=(px, py, pc),
                                                 device_id_type=MESH_ID).wait_recv()

    any_spec = pl.BlockSpec(memory_space=pl.ANY)
    return _call(body, name=name, out_shape=[_sds(s, a.dtype) for s, a in zip(out_shapes, arrays)],
                 in_specs=[any_spec] * n, out_specs=[any_spec] * n,
                 scratch=[pltpu.SemaphoreType.DMA((total, N_DEV - 1)), pltpu.SemaphoreType.DMA((total, N_DEV - 1)),
                          pltpu.SemaphoreType.DMA((total,))])(*arrays)


def all_gather(name, arrays, row_sharded):
    def out_shape(a):
        s = arrays[a].shape
        return (s[0], N_DEV) + s[1:] if row_sharded[a] else (N_DEV,) + s

    def src_of(a, ref, peer):
        if row_sharded[a]:
            return [ref.at[l] for l in range(arrays[a].shape[0])]
        return [ref]

    def dst_of(a, ref, me):
        if row_sharded[a]:
            return [ref.at[l, me] for l in range(arrays[a].shape[0])]
        return [ref.at[me]]

    pieces = [arrays[a].shape[0] if row_sharded[a] else 1 for a in range(len(arrays))]
    return _exchange(name, arrays, [out_shape(a) for a in range(len(arrays))], pieces, src_of, dst_of)


def _src_view(kind, ref, peer):
    return ref if kind == "gather" else ref.at[peer]


def _peers(x, y, c):
    for m in range(1, N_DEV):
        px, py, pc = x ^ (m >> 2), y ^ ((m >> 1) & 1), c ^ (m & 1)
        yield m - 1, (px, py, pc), 4 * px + 2 * py + pc


def place_own(name, kind, srcs):
    n = len(srcs)
    shapes = [(N_DEV,) + s.shape if kind == "gather" else s.shape for s in srcs]
    steps = 2 if all(s.shape[-2] % 32 == 0 for s in srcs) else 1

    def body(*refs):
        for a in range(n):
            refs[n + a][...] = refs[a][...]

    def spec(shape, own_block):
        R, C = shape[-2:]
        tr = R // steps
        if own_block:
            return pl.BlockSpec((None, tr, C), lambda i: (_my_index(), i, 0))
        return pl.BlockSpec((tr, C), lambda i: (i, 0))

    return _call(body, name=name, grid=(steps,), out_shape=[_sds(s, a.dtype) for s, a in zip(shapes, srcs)],
                 in_specs=[spec(s.shape, kind == "scatter") for s in srcs],
                 out_specs=[spec(s, True) for s in shapes])(*srcs)


_HBM_SPEC = pl.BlockSpec(memory_space=pltpu.HBM)
_SEM_SPEC = pl.BlockSpec(memory_space=pltpu.SEMAPHORE)
_DATAFLOW = pltpu.SideEffectType.DATAFLOW_SIDE_EFFECTING


def _remote(kind, src, land, send_sems, recv_sems, a, slot, frm, to_id, at):
    return pltpu.make_async_remote_copy(src_ref=_src_view(kind, src, frm), dst_ref=land.at[at],
                                        send_sem=send_sems.at[a * (N_DEV - 1) + slot],
                                        recv_sem=recv_sems.at[a * (N_DEV - 1) + slot],
                                        device_id=to_id, device_id_type=MESH_ID)


def exchange_begin(name, kind, srcs, lands, after):
    n = len(srcs)

    def body(*refs):
        ins, lnd = refs[:n], refs[n:2 * n]
        send_sems, recv_sems = refs[2 * n + 1], refs[2 * n + 2]
        token = refs[-1]
        x, y, c = lax.axis_index("x"), lax.axis_index("y"), lax.axis_index("c")
        me = 4 * x + 2 * y + c
        for slot, peer_id, peer in _peers(x, y, c):
            for a in range(n):
                _remote(kind, ins[a], lnd[a], send_sems, recv_sems, a, slot, peer, peer_id, me).start()
        token[...] = jnp.zeros_like(token)

    hbm = lambda t: pltpu.HBM(t.shape, t.dtype)
    outs = _pallas(
        body, name=name,
        out_shape=(pltpu.SemaphoreType.DMA((n * (N_DEV - 1),)), pltpu.SemaphoreType.DMA((n * (N_DEV - 1),)),
                   *[hbm(t) for t in srcs], *[hbm(t) for t in lands], _sds((8, 128), F32)),
        in_specs=[_HBM_SPEC] * (2 * n) + [pl.BlockSpec(memory_space=pl.ANY)],
        out_specs=(_SEM_SPEC, _SEM_SPEC, *[_HBM_SPEC] * (2 * n), pl.BlockSpec(memory_space=pltpu.VMEM)),
        input_output_aliases={i: 2 + i for i in range(2 * n)},
        compiler_params=pltpu.CompilerParams(has_side_effects=_DATAFLOW),
    )(*[pltpu.with_memory_space_constraint(t, pltpu.HBM) for t in list(srcs) + list(lands)], after)
    return (kind, outs[0], outs[1], list(outs[2:2 + n]), list(outs[2 + n:2 + 2 * n])), outs[-1]


def exchange_end(name, handle, after):
    kind, send_sems, recv_sems, srcs, lands = handle
    n = len(srcs)

    def body(*refs):
        ins, lnd = refs[:n], refs[n:2 * n]
        s_sems, r_sems = refs[2 * n], refs[2 * n + 1]
        x, y, c = lax.axis_index("x"), lax.axis_index("y"), lax.axis_index("c")
        me = 4 * x + 2 * y + c
        for slot, peer_id, peer in _peers(x, y, c):
            for a in range(n):
                _remote(kind, ins[a], lnd[a], s_sems, r_sems, a, slot, peer, peer_id, me).wait_send()
        for slot, peer_id, peer in _peers(x, y, c):
            for a in range(n):
                _remote(kind, ins[a], lnd[a], s_sems, r_sems, a, slot, me, peer_id, peer).wait_recv()

    hbm = lambda t: pltpu.HBM(t.shape, t.dtype)
    outs = _pallas(
        body, name=name, out_shape=tuple(hbm(t) for t in srcs + lands),
        in_specs=[_HBM_SPEC] * (2 * n) + [_SEM_SPEC, _SEM_SPEC, pl.BlockSpec(memory_space=pl.ANY)],
        out_specs=tuple([_HBM_SPEC] * (2 * n)), input_output_aliases={i: i for i in range(2 * n)},
        compiler_params=pltpu.CompilerParams(has_side_effects=_DATAFLOW),
    )(*srcs, *lands, send_sems, recv_sems, after)
    return list(outs[n:])


def norm_mm(name, x, gain, w, layer, bias=None, w_t=False):
    T, D = x.shape
    nsh = w.shape[0]
    n = w.shape[2] if w_t else w.shape[3]
    tm = _tile(T, 2048)

    def body(*refs):
        if bias is None:
            x_ref, g_ref, w_ref, u_ref, h_ref = refs
        else:
            x_ref, g_ref, w_ref, b_ref, u_ref, h_ref = refs

        @pl.when(pl.program_id(1) == 0)
        def _():
            xf = x_ref[...]
            r = lax.rsqrt(jnp.mean(xf * xf, axis=-1, keepdims=True) + EPS)
            h_ref[...] = (xf * r * g_ref[...]).astype(h_ref.dtype)

        acc = (_dot_nt if w_t else _dot)(h_ref[...], w_ref[...])
        if bias is not None:
            acc = acc + b_ref[...]
        u_ref[...] = acc.astype(u_ref.dtype)

    in_specs = [pl.BlockSpec((tm, D), lambda i, j: (i, 0)),
                pl.BlockSpec((1, D), lambda i, j: (0, 0)),
                pl.BlockSpec((None, None) + w.shape[2:], lambda i, j: (j, layer, 0, 0))]
    args = [x, gain.reshape(1, D), w]
    if bias is not None:
        in_specs.append(pl.BlockSpec((None, None, 1, n), lambda i, j: (j, layer, 0, 0)))
        args.append(bias)
    return _call(body, name=name, grid=(T // tm, nsh), in_specs=in_specs,
                 out_specs=[pl.BlockSpec((None, tm, n), lambda i, j: (j, i, 0)),
                            pl.BlockSpec((tm, D), lambda i, j: (i, 0))],
                 out_shape=[_sds((nsh, T, n), MM), _sds((T, D), MM)])(*args)


def mm_resid_norm(name, a, w, layer, bias, x, gain):
    nk, T, kk = a.shape
    D = x.shape[1]
    tm = _tile(T, 512)

    def body(a_ref, w_ref, b_ref, x_ref, g_ref, y_ref, xn_ref):
        y = _dot(a_ref[0], w_ref[0])
        for q in range(1, nk):
            y = y + _dot(a_ref[q], w_ref[q])
        y = y + b_ref[...]
        y_ref[...] = y
        r = lax.rsqrt(jnp.mean(y * y, axis=-1, keepdims=True) + EPS)
        xn_ref[...] = x_ref[...] + y * r * g_ref[...]

    return _call(body, name=name, grid=(T // tm,),
                 in_specs=[pl.BlockSpec((nk, tm, kk), lambda i: (0, i, 0)),
                           pl.BlockSpec((None, nk, kk, D), lambda i: (layer, 0, 0, 0)),
                           pl.BlockSpec((1, D), lambda i: (0, 0)),
                           pl.BlockSpec((tm, D), lambda i: (i, 0)),
                           pl.BlockSpec((1, D), lambda i: (0, 0))],
                 out_specs=[pl.BlockSpec((tm, D), lambda i: (i, 0))] * 2,
                 out_shape=[_sds((T, D), F32)] * 2)(a, w, bias.reshape(1, D), x, gain.reshape(1, D))


def _halo_maps(tm, hb, T):
    per = tm // hb
    last = T // hb - 1
    return (lambda i: jnp.maximum(i * per - 1, 0)), (lambda i: jnp.minimum((i + 1) * per, last))


def cm_glu_conv(name, u, dw, dwb, layer, seq):
    _, T, n = u.shape
    ct = dw.shape[-1]
    per = n // ct
    nct = 4 * per
    tm = _tile(seq, 512)
    tps = seq // tm
    hb = CONV_HALO
    prev, _ = _halo_maps(tm, hb, T)
    u4 = u.reshape(2, 4, T, n)

    def body(u_ref, uh_ref, w_ref, b_ref, o_ref, pad_ref):
        first = (pl.program_id(0) % tps) == 0
        um = u_ref[...].astype(F32)
        uh = uh_ref[...].astype(F32)
        pad_ref[pl.ds(hb, tm), :] = um[0] * _sigmoid(um[1])
        pad_ref[pl.ds(0, hb), :] = jnp.where(first, 0.0, uh[0] * _sigmoid(uh[1]))
        acc = jnp.zeros((tm, ct), F32) + b_ref[...]
        for k in range(CONV_W):
            acc = acc + w_ref[pl.ds(k, 1), :] * pad_ref[pl.ds(hb - (CONV_W - 1) + k, tm), :]
        o_ref[...] = acc

    return _call(body, name=name, grid=(T // tm, nct),
                 in_specs=[pl.BlockSpec((2, None, tm, ct), lambda i, c: (0, c // per, i, c % per)),
                           pl.BlockSpec((2, None, hb, ct), lambda i, c: (0, c // per, prev(i), c % per)),
                           pl.BlockSpec((None, None, CONV_W, ct), lambda i, c: (c, layer, 0, 0)),
                           pl.BlockSpec((None, None, 1, ct), lambda i, c: (c, layer, 0, 0))],
                 out_specs=pl.BlockSpec((tm, ct), lambda i, c: (i, c)),
                 out_shape=_sds((T, nct * ct), F32),
                 scratch=[pltpu.VMEM((tm + hb, ct), F32)])(u4, u4, dw, dwb)


def ln_silu(name, c, g, b):
    T, D = c.shape
    tm = _tile(T, 512)

    def body(c_ref, g_ref, b_ref, s_ref):
        cf = c_ref[...]
        mu = jnp.mean(cf, axis=-1, keepdims=True)
        xc = cf - mu
        r = lax.rsqrt(jnp.mean(xc * xc, axis=-1, keepdims=True) + EPS)
        t = xc * r * g_ref[...] + b_ref[...]
        s_ref[...] = (t * _sigmoid(t)).astype(s_ref.dtype)

    return _call(body, name=name, grid=(T // tm,),
                 in_specs=[pl.BlockSpec((tm, D), lambda i: (i, 0)), pl.BlockSpec((1, D), lambda i: (0, 0)),
                           pl.BlockSpec((1, D), lambda i: (0, 0))],
                 out_specs=pl.BlockSpec((None, tm, D), lambda i: (0, i, 0)),
                 out_shape=_sds((1, T, D), MM))(c, g.reshape(1, D), b.reshape(1, D))


def _col_tiles(n, width):
    return [(c0, min(width, n - c0)) for c0 in range(0, n, width)]


def ffn_in_conv(name, x, gain, wt, dw, dwb, layer, seq):
    T, D = x.shape
    n = wt.shape[2]
    tm = _tile(seq, 1024)
    tps = seq // tm
    hb = 8
    rows = _tile(tm, 2 * ROW_CHUNK_FWD)
    K = FFN_CONV_W
    w5 = wt.reshape(2, 4, n, D)
    dw4 = dw.reshape(2, 4, dw.shape[1], K, n)
    dwb4 = dwb.reshape(dwb.shape[0], 2, 4, 1, n)

    def body(x_ref, g_ref, w_ref, cw_ref, cb_ref, u_ref, h_ref, z_ref, ag_ref, pad_ref, carry_ref):
        i, j = pl.program_id(0), pl.program_id(1)
        first = (i % tps) == 0

        @pl.when(j == 0)
        def _():
            xf = x_ref[...]
            r = lax.rsqrt(jnp.mean(xf * xf, axis=-1, keepdims=True) + EPS)
            h_ref[...] = (xf * r * g_ref[...]).astype(h_ref.dtype)

        h = h_ref[...]
        for c0, wc in _col_tiles(n, 256):
            cols = pl.ds(c0, wc)
            for half in range(2):
                res = _dot_nt(h, w_ref[half, cols, :])
                u_ref[half, :, cols] = res.astype(u_ref.dtype)
                pad_ref[half, pl.ds(hb, tm), cols] = res
                pad_ref[half, pl.ds(0, hb), cols] = jnp.where(first, 0.0, carry_ref[j, half, :, cols])
            for r0 in range(0, tm, rows):
                conv = []
                for half in range(2):
                    acc = cb_ref[half, :, cols] + cw_ref[half, pl.ds(K - 1, 1), cols] * pad_ref[
                        half, pl.ds(hb + r0, rows), cols]
                    for k in range(K - 1):
                        acc = acc + cw_ref[half, pl.ds(k, 1), cols] * pad_ref[
                            half, pl.ds(hb + r0 - (K - 1) + k, rows), cols]
                    conv.append(acc)
                a, g = conv
                z_ref[pl.ds(r0, rows), cols] = (g * _sigmoid(g) * a).astype(z_ref.dtype)
                ag_ref[0, pl.ds(r0, rows), cols] = a.astype(ag_ref.dtype)
                ag_ref[1, pl.ds(r0, rows), cols] = g.astype(ag_ref.dtype)
            for half in range(2):
                carry_ref[j, half, :, cols] = pad_ref[half, pl.ds(tm, hb), cols]

    u, h, z, ag = _call(
        body, name=name, grid=(T // tm, 4),
        in_specs=[pl.BlockSpec((tm, D), lambda i, j: (i, 0)),
                  pl.BlockSpec((1, D), lambda i, j: (0, 0)),
                  pl.BlockSpec((2, None, n, D), lambda i, j: (0, j, 0, 0)),
                  pl.BlockSpec((2, None, None, K, n), lambda i, j: (0, j, layer, 0, 0)),
                  pl.BlockSpec((None, 2, None, 1, n), lambda i, j: (layer, 0, j, 0, 0))],
        out_specs=[pl.BlockSpec((2, None, tm, n), lambda i, j: (0, j, i, 0)),
                   pl.BlockSpec((tm, D), lambda i, j: (i, 0)),
                   pl.BlockSpec((None, tm, n), lambda i, j: (j, i, 0)),
                   pl.BlockSpec((2, None, tm, n), lambda i, j: (0, j, i, 0))],
        out_shape=[_sds((2, 4, T, n), MM), _sds((T, D), MM), _sds((4, T, n), MM), _sds((2, 4, T, n), MM)],
        scratch=[pltpu.VMEM((2, tm + hb, n), F32), pltpu.VMEM((4, 2, hb, n), F32)])(
            x, gain.reshape(1, D), w5, dw4, dwb4)
    return u.reshape(8, T, n), h, z, ag


def _head_specs(seq, dh, q_heads, kv_heads):
    def spec(per, base):
        def imap(b, h, g):
            f = base + g * N_HEADS + h
            return (f // per, b, f % per)
        return pl.BlockSpec((None, seq, dh), imap)
    return spec(q_heads, 0), spec(kv_heads, 0), spec(kv_heads, N_GROUPS * N_HEADS)


def _rows(start, d, blocks=1):
    size = blocks * ATT_BLOCK
    return pl.ds(start, size, stride=d) if d > 1 else pl.ds(start, size)


def _att_pad():
    return max(ATT_BLOCK * d for d in DILATIONS[:-1])


def _band_mask(first, nblk):
    keys = ATT_BLOCK if nblk == 1 else 2 * ATT_BLOCK
    shape = (ATT_BATCH, ATT_BLOCK, keys)
    qi = lax.broadcasted_iota(jnp.int32, shape, 1)
    kj = lax.broadcasted_iota(jnp.int32, shape, 2)
    if nblk == 1:
        return kj <= qi
    n = (first + lax.broadcasted_iota(jnp.int32, shape, 0)) % nblk
    return (kj >= qi) & (kj <= qi + ATT_BLOCK) & ((n > 0) | (kj >= ATT_BLOCK))


def _block_rows(idx, d, nblk):
    r, n = idx // nblk, idx % nblk
    rq = _rows(r + d * ATT_BLOCK * n, d)
    if nblk == 1:
        return rq, _rows(_att_pad() + r + d * ATT_BLOCK * n, d)
    return rq, _rows(_att_pad() + r + d * ATT_BLOCK * (n - 1), d, blocks=2)


def _bdot(a, b, ca, cb):
    return lax.dot_general(a, b, (((ca,), (cb,)), ((0,), (0,))), preferred_element_type=F32)


def attn_fwd(name, q, kv, nb, seq):
    _, T, qn = q.shape
    dh = qn * N_DEV // (N_GROUPS * N_HEADS)
    scale = 1.0 / (dh ** 0.5)
    qs, ks, vs = _head_specs(seq, dh, qn // dh, kv.shape[2] // dh)

    def body(q_ref, k_ref, v_ref, m_ref, l_ref, qf, kf, vf, *branch):
        og, lg = branch[:N_GROUPS], branch[N_GROUPS:]
        pad = _att_pad()
        qf[...] = q_ref[...].astype(F32)
        for t_ref, s_ref in ((k_ref, kf), (v_ref, vf)):
            s_ref[pl.ds(0, pad), :] = jnp.zeros((pad, dh), F32)
            s_ref[pl.ds(pad, seq), :] = t_ref[...].astype(F32)
        for g in range(N_GROUPS):
            d = DILATIONS[g]
            nblk = seq // d // ATT_BLOCK

            def blocks(it, carry, d=d, nblk=nblk, g=g):
                first = it * ATT_BATCH
                rows = [_block_rows(first + b, d, nblk) for b in range(ATT_BATCH)]
                qb = jnp.stack([qf[rq, :] for rq, _ in rows]).astype(MM)
                kb = jnp.stack([kf[rk, :] for _, rk in rows]).astype(MM)
                vb = jnp.stack([vf[rk, :] for _, rk in rows]).astype(MM)
                s = jnp.where(_band_mask(first, nblk), _bdot(qb, kb, 2, 2) * scale, NEG)
                m = jnp.max(s, axis=-1, keepdims=True)
                p = jnp.exp(s - m)
                den = jnp.sum(p, axis=-1, keepdims=True)
                o = _bdot(p.astype(MM), vb, 2, 1) / den
                lse = m + jnp.log(den)
                for b, (rq, _) in enumerate(rows):
                    og[g][rq, :] = o[b]
                    lg[g][rq, :] = jnp.broadcast_to(lse[b], (ATT_BLOCK, dh))
                return carry

            @pl.when(pl.program_id(2) == g)
            def _(blocks=blocks, d=d, nblk=nblk):
                lax.fori_loop(0, d * nblk // ATT_BATCH, blocks, 0)

        @pl.when(pl.program_id(2) == N_GROUPS - 1)
        def _():
            mx = jnp.maximum(jnp.maximum(lg[0][...], lg[1][...]), lg[2][...])
            e = [jnp.exp(lg[g][...] - mx) for g in range(N_GROUPS)]
            tot = e[0] + e[1] + e[2]
            m_ref[...] = ((e[0] * og[0][...] + e[1] * og[1][...] + e[2] * og[2][...]) / tot).astype(m_ref.dtype)
            l_ref[...] = mx + jnp.log(tot)

    return _call(body, name=name, grid=(nb, N_HEADS, N_GROUPS), in_specs=[qs, ks, vs],
                 out_specs=[pl.BlockSpec((None, seq, dh), lambda b, h, g: (0, b, h)),
                            pl.BlockSpec((seq, dh), lambda b, h, g: (b, h))],
                 out_shape=[_sds((1, T, N_HEADS * dh), MM), _sds((T, N_HEADS * dh), F32)],
                 scratch=[pltpu.VMEM((seq, dh), F32)] + [pltpu.VMEM((_att_pad() + seq, dh), F32)] * 2
                 + [pltpu.VMEM((seq, dh), F32)] * (2 * N_GROUPS))(q, kv, kv)


def loss_fwd_bwd(name, x, target):
    T, D = x.shape
    tm = _tile(T, 512)

    def body(x_ref, t_ref, dx_ref, l_ref):
        @pl.when(pl.program_id(0) == 0)
        def _():
            l_ref[...] = jnp.zeros_like(l_ref)
        err = x_ref[...] - t_ref[...]
        dx_ref[...] = err * (1.0 / D)
        l_ref[...] += 0.5 * jnp.sum(jnp.mean(err * err, axis=-1, keepdims=True), axis=0, keepdims=True)

    dx, l = _call(body, name=name, grid=(T // tm,),
                  in_specs=[pl.BlockSpec((tm, D), lambda i: (i, 0))] * 2,
                  out_specs=[pl.BlockSpec((tm, D), lambda i: (i, 0)), pl.BlockSpec((1, 1), lambda i: (0, 0))],
                  out_shape=[_sds((T, D), F32), _sds((1, 1), F32)])(x, target)
    return dx, l


def resid_norm_bwd(name, dx, y, gain):
    T, D = y.shape
    tm = _tile(T, 512)

    def body(dx_ref, y_ref, g_ref, dy_ref, dg_ref, db_ref):
        @pl.when(pl.program_id(0) == 0)
        def _():
            dg_ref[...] = jnp.zeros_like(dg_ref)
            db_ref[...] = jnp.zeros_like(db_ref)
        y = y_ref[...]
        d = dx_ref[...]
        r = lax.rsqrt(jnp.mean(y * y, axis=-1, keepdims=True) + EPS)
        yh = y * r
        dyh = d * g_ref[...]
        dy = r * (dyh - yh * jnp.mean(dyh * yh, axis=-1, keepdims=True))
        dy_ref[...] = dy.astype(dy_ref.dtype)
        dg_ref[...] += jnp.sum(d * yh, axis=0, keepdims=True)
        db_ref[...] += jnp.sum(dy, axis=0, keepdims=True)

    return _call(body, name=name, grid=(T // tm,),
                 in_specs=[pl.BlockSpec((tm, D), lambda i: (i, 0))] * 2 + [pl.BlockSpec((1, D), lambda i: (0, 0))],
                 out_specs=[pl.BlockSpec((None, tm, D), lambda i: (0, i, 0))] + [pl.BlockSpec((1, D), lambda i: (0, 0))] * 2,
                 out_shape=[_sds((1, T, D), MM), _sds((1, D), F32), _sds((1, D), F32)])(dx, y, gain.reshape(1, D))


def mm_nt(name, dy, w, layer, out_dtype):
    _, T, D = dy.shape
    _, nk, kk, _ = w.shape
    tm = _tile(T, 1024)

    def body(dy_ref, w_ref, o_ref):
        o_ref[...] = _dot_nt(dy_ref[...], w_ref[...]).astype(o_ref.dtype)

    return _call(body, name=name, grid=(T // tm, nk),
                 in_specs=[pl.BlockSpec((None, tm, D), lambda i, q: (0, i, 0)),
                           pl.BlockSpec((None, None, kk, D), lambda i, q: (layer, q, 0, 0))],
                 out_specs=pl.BlockSpec((None, tm, kk), lambda i, q: (q, i, 0)),
                 out_shape=_sds((nk, T, kk), out_dtype))(dy, w)


def mm_nt_norm_bwd(name, du, w, layer, x_in, gain, dx_res, w_t=False):
    nsh, T, n = du.shape
    D = x_in.shape[1]
    tm = _tile(T, 1024)

    def body(du_ref, w_ref, x_ref, g_ref, dr_ref, dx_ref, dg_ref, acc_ref):
        i, j = pl.program_id(0), pl.program_id(1)

        @pl.when((i == 0) & (j == 0))
        def _():
            dg_ref[...] = jnp.zeros_like(dg_ref)

        part = (_dot if w_t else _dot_nt)(du_ref[...], w_ref[...])

        @pl.when(j == 0)
        def _():
            acc_ref[...] = part

        @pl.when(j > 0)
        def _():
            acc_ref[...] += part

        @pl.when(j == nsh - 1)
        def _():
            x = x_ref[...]
            dh = acc_ref[...]
            r = lax.rsqrt(jnp.mean(x * x, axis=-1, keepdims=True) + EPS)
            xh = x * r
            dxh = dh * g_ref[...]
            dx_ref[...] = dr_ref[...] + r * (dxh - xh * jnp.mean(dxh * xh, axis=-1, keepdims=True))
            dg_ref[...] += jnp.sum(dh * xh, axis=0, keepdims=True)

    return _call(body, name=name, grid=(T // tm, nsh),
                 in_specs=[pl.BlockSpec((None, tm, n), lambda i, j: (j, i, 0)),
                           pl.BlockSpec((None, None) + w.shape[2:], lambda i, j: (j, layer, 0, 0)),
                           pl.BlockSpec((tm, D), lambda i, j: (i, 0)),
                           pl.BlockSpec((1, D), lambda i, j: (0, 0)),
                           pl.BlockSpec((tm, D), lambda i, j: (i, 0))],
                 out_specs=[pl.BlockSpec((tm, D), lambda i, j: (i, 0)), pl.BlockSpec((1, D), lambda i, j: (0, 0))],
                 out_shape=[_sds((T, D), F32), _sds((1, D), F32)],
                 scratch=[pltpu.VMEM((tm, D), F32)])(du, w, x_in, gain.reshape(1, D), dx_res)


def mm_tn(name, a, b):
    na, T, ka = a.shape
    nb, _, kb = b.shape
    nj = max(na, nb)

    def body(a_ref, b_ref, o_ref):
        o_ref[...] = _dot_tn(a_ref[...], b_ref[...]).astype(o_ref.dtype)

    return _call(body, name=name, grid=(nj,),
                 in_specs=[pl.BlockSpec((None, T, ka), (lambda j: (j, 0, 0)) if na > 1 else (lambda j: (0, 0, 0))),
                           pl.BlockSpec((None, T, kb), (lambda j: (j, 0, 0)) if nb > 1 else (lambda j: (0, 0, 0)))],
                 out_specs=pl.BlockSpec((None, ka, kb), lambda j: (j, 0, 0)),
                 out_shape=_sds((nj, ka, kb), MM))(a, b)


def ffn_conv_gate_bwd(name, u, ag, dz, dw, layer, seq):
    _, T, n = u.shape
    tm = _tile(seq, 512)
    tps = seq // tm
    hb = FFN_HALO
    _, nxt = _halo_maps(tm, hb, T)
    u4 = u.reshape(2, 4, T, n)
    dw4 = dw.reshape(2, 4, dw.shape[1], FFN_CONV_W, n)
    K = FFN_CONV_W
    te = tm + hb
    rows = _tile(hb, ROW_CHUNK_BWD)

    def body(u_ref, ag_ref, agn_ref, dz_ref, dzn_ref, w_ref, du_ref, ddw_ref, ddb_ref, uf_ref, agf_ref, dzf_ref,
             da_ref):
        i = pl.program_id(1)
        last = (i % tps) == tps - 1

        @pl.when(i == 0)
        def _():
            ddw_ref[...] = jnp.zeros_like(ddw_ref)
            ddb_ref[...] = jnp.zeros_like(ddb_ref)

        for half in range(2):
            uf_ref[half] = u_ref[half].astype(F32)
            agf_ref[half, pl.ds(0, tm), :] = ag_ref[half].astype(F32)
            agf_ref[half, pl.ds(tm, hb), :] = agn_ref[half].astype(F32)
        dzf_ref[pl.ds(0, tm), :] = dz_ref[...].astype(F32)
        dzf_ref[pl.ds(tm, hb), :] = jnp.where(last, 0.0, dzn_ref[...].astype(F32))
        for r0 in range(0, te, rows):
            a, g = agf_ref[0, pl.ds(r0, rows), :], agf_ref[1, pl.ds(r0, rows), :]
            dzc = dzf_ref[pl.ds(r0, rows), :]
            sg = _sigmoid(g)
            da_ref[0, pl.ds(r0, rows), :] = dzc * (g * sg)
            da_ref[1, pl.ds(r0, rows), :] = dzc * a * (sg * (1.0 + g * (1.0 - sg)))
        for half in range(2):
            tap_acc = [jnp.zeros((rows, n), F32) for _ in range(K)]
            bias_acc = jnp.zeros((rows, n), F32)
            for r0 in range(0, tm, rows):
                x = uf_ref[half, pl.ds(r0, rows), :]
                acc = None
                for k in range(K):
                    d = da_ref[half, pl.ds(r0 + K - 1 - k, rows), :]
                    acc = w_ref[half, pl.ds(k, 1), :] * d if acc is None else acc + w_ref[half, pl.ds(k, 1), :] * d
                    tap_acc[k] = tap_acc[k] + d * x
                    if k == K - 1:
                        bias_acc = bias_acc + d
                du_ref[half, pl.ds(r0, rows), :] = acc.astype(du_ref.dtype)
            for k in range(K):
                ddw_ref[half, pl.ds(k, 1), :] += jnp.sum(tap_acc[k], axis=0, keepdims=True)
            ddb_ref[half] += jnp.sum(bias_acc, axis=0, keepdims=True)

    du, ddw, ddb = _call(
        body, name=name, grid=(4, T // tm),
        in_specs=[pl.BlockSpec((2, None, tm, n), lambda j, i: (0, j, i, 0)),
                  pl.BlockSpec((2, None, tm, n), lambda j, i: (0, j, i, 0)),
                  pl.BlockSpec((2, None, hb, n), lambda j, i: (0, j, nxt(i), 0)),
                  pl.BlockSpec((None, tm, n), lambda j, i: (j, i, 0)),
                  pl.BlockSpec((None, hb, n), lambda j, i: (j, nxt(i), 0)),
                  pl.BlockSpec((2, None, None, K, n), lambda j, i: (0, j, layer, 0, 0))],
        out_specs=[pl.BlockSpec((2, None, tm, n), lambda j, i: (0, j, i, 0)),
                   pl.BlockSpec((2, None, K, n), lambda j, i: (0, j, 0, 0)),
                   pl.BlockSpec((2, None, 1, n), lambda j, i: (0, j, 0, 0))],
        out_shape=[_sds((2, 4, T, n), MM), _sds((2, 4, K, n), F32), _sds((2, 4, 1, n), F32)],
        scratch=[pltpu.VMEM((2, tm, n), F32), pltpu.VMEM((2, te, n), F32), pltpu.VMEM((te, n), F32),
                 pltpu.VMEM((2, te, n), F32)],
    )(u4, ag, ag, dz, dz, dw4)
    return du.reshape(8, T, n), ddw, ddb


def ln_silu_bwd(name, ds, c, g, b):
    T, D = c.shape
    tm = _tile(T, 512)

    def body(ds_ref, c_ref, g_ref, b_ref, dc_ref, dg_ref, db_ref):
        @pl.when(pl.program_id(0) == 0)
        def _():
            dg_ref[...] = jnp.zeros_like(dg_ref)
            db_ref[...] = jnp.zeros_like(db_ref)
        cf = c_ref[...]
        mu = jnp.mean(cf, axis=-1, keepdims=True)
        xc = cf - mu
        r = lax.rsqrt(jnp.mean(xc * xc, axis=-1, keepdims=True) + EPS)
        xh = xc * r
        t = xh * g_ref[...] + b_ref[...]
        sg = _sigmoid(t)
        dt = ds_ref[...] * (sg * (1.0 + t * (1.0 - sg)))
        dg_ref[...] += jnp.sum(dt * xh, axis=0, keepdims=True)
        db_ref[...] += jnp.sum(dt, axis=0, keepdims=True)
        dxh = dt * g_ref[...]
        dc_ref[...] = r * (dxh - jnp.mean(dxh, axis=-1, keepdims=True)
                           - xh * jnp.mean(dxh * xh, axis=-1, keepdims=True))

    vec = pl.BlockSpec((1, D), lambda i: (0, 0))
    return _call(body, name=name, grid=(T // tm,),
                 in_specs=[pl.BlockSpec((None, tm, D), lambda i: (0, i, 0)), pl.BlockSpec((tm, D), lambda i: (i, 0)),
                           vec, vec],
                 out_specs=[pl.BlockSpec((tm, D), lambda i: (i, 0)), vec, vec],
                 out_shape=[_sds((T, D), F32), _sds((1, D), F32), _sds((1, D), F32)])(
                     ds, c, g.reshape(1, D), b.reshape(1, D))


def cm_glu_conv_bwd(name, u, dc, dw, layer, seq):
    _, T, n = u.shape
    ct = dw.shape[-1]
    per = n // ct
    nct = 4 * per
    tm = _tile(seq, 512)
    tps = seq // tm
    hb = CONV_HALO
    prev, nxt = _halo_maps(tm, hb, T)
    u4 = u.reshape(2, 4, T, n)
    K = CONV_W

    def body(u_ref, up_ref, dc_ref, dcn_ref, w_ref, du_ref, ddw_ref, ddb_ref, dbi_ref, padp_ref, padd_ref):
        i = pl.program_id(1)
        first = (i % tps) == 0
        last = (i % tps) == tps - 1

        @pl.when(i == 0)
        def _():
            ddw_ref[...] = jnp.zeros_like(ddw_ref)
            ddb_ref[...] = jnp.zeros_like(ddb_ref)
            dbi_ref[...] = jnp.zeros_like(dbi_ref)

        um = u_ref[...].astype(F32)
        uh = up_ref[...].astype(F32)
        sg = _sigmoid(um[1])
        padp_ref[pl.ds(hb, tm), :] = um[0] * sg
        padp_ref[pl.ds(0, hb), :] = jnp.where(first, 0.0, uh[0] * _sigmoid(uh[1]))
        dcm = dc_ref[...]
        padd_ref[pl.ds(0, tm), :] = dcm
        padd_ref[pl.ds(tm, hb), :] = jnp.where(last, 0.0, dcn_ref[...])
        dp = jnp.zeros((tm, ct), F32)
        for k in range(K):
            dp = dp + w_ref[pl.ds(k, 1), :] * padd_ref[pl.ds(K - 1 - k, tm), :]
            ddw_ref[pl.ds(k, 1), :] += jnp.sum(dcm * padp_ref[pl.ds(hb - (K - 1) + k, tm), :], axis=0, keepdims=True)
        ddb_ref[...] += jnp.sum(dcm, axis=0, keepdims=True)
        dv = dp * sg
        dg = dp * um[0] * sg * (1.0 - sg)
        du_ref[0] = dv.astype(du_ref.dtype)
        du_ref[1] = dg.astype(du_ref.dtype)
        dbi_ref[0] += jnp.sum(dv, axis=0, keepdims=True)
        dbi_ref[1] += jnp.sum(dg, axis=0, keepdims=True)

    du, ddw, ddb, dbi = _call(
        body, name=name, grid=(nct, T // tm),
        in_specs=[pl.BlockSpec((2, None, tm, ct), lambda c, i: (0, c // per, i, c % per)),
                  pl.BlockSpec((2, None, hb, ct), lambda c, i: (0, c // per, prev(i), c % per)),
                  pl.BlockSpec((tm, ct), lambda c, i: (i, c)),
                  pl.BlockSpec((hb, ct), lambda c, i: (nxt(i), c)),
                  pl.BlockSpec((None, None, K, ct), lambda c, i: (c, layer, 0, 0))],
        out_specs=[pl.BlockSpec((2, None, tm, ct), lambda c, i: (0, c // per, i, c % per)),
                   pl.BlockSpec((None, K, ct), lambda c, i: (c, 0, 0)),
                   pl.BlockSpec((None, 1, ct), lambda c, i: (c, 0, 0)),
                   pl.BlockSpec((2, None, 1, ct), lambda c, i: (0, c // per, 0, c % per))],
        out_shape=[_sds((2, 4, T, n), MM), _sds((nct, K, ct), F32), _sds((nct, 1, ct), F32), _sds((2, 4, 1, n), F32)],
        scratch=[pltpu.VMEM((tm + hb, ct), F32), pltpu.VMEM((tm + hb, ct), F32)])(u4, u4, dc, dc, dw)
    return du.reshape(8, T, n), ddw, ddb, dbi


def attn_bwd(name, q, kv, dm, merged, lse, dkv_prev, nb, seq):
    _, T, qn = q.shape
    kn = kv.shape[2]
    dh = qn * N_DEV // (N_GROUPS * N_HEADS)
    scale = 1.0 / (dh ** 0.5)
    qs, ks, vs = _head_specs(seq, dh, qn // dh, kn // dh)
    has_prev = dkv_prev is not None
    n_in = 6 + (1 if has_prev else 0)

    def body(*refs):
        q_ref, k_ref, v_ref, dm_ref, mg_ref, l_ref = refs[:6]
        pkv_ref = refs[6] if has_prev else None
        dq_ref, dkv_ref = refs[n_in:n_in + 2]
        qf, kf, vf, dqf, dkf, dvf, dlt = refs[n_in + 2:]
        pad = _att_pad()

        @pl.when(pl.program_id(2) == 0)
        def _():
            dlt[...] = jnp.broadcast_to(
                jnp.sum(dm_ref[...] * mg_ref[...].astype(F32), axis=-1, keepdims=True), (seq, dh))

        qf[...] = q_ref[...].astype(F32)
        for t_ref, s_ref in ((k_ref, kf), (v_ref, vf)):
            s_ref[pl.ds(0, pad), :] = jnp.zeros((pad, dh), F32)
            s_ref[pl.ds(pad, seq), :] = t_ref[...].astype(F32)
        dkf[...] = jnp.zeros_like(dkf)
        dvf[...] = jnp.zeros_like(dvf)
        for g in range(N_GROUPS):
            d = DILATIONS[g]
            nblk = seq // d // ATT_BLOCK

            def blocks(it, carry, d=d, nblk=nblk):
                first = it * ATT_BATCH
                rows = [_block_rows(first + b, d, nblk) for b in range(ATT_BATCH)]
                qb = jnp.stack([qf[rq, :] for rq, _ in rows]).astype(MM)
                dmb = jnp.stack([dm_ref[rq, :] for rq, _ in rows]).astype(MM)
                lse = jnp.stack([l_ref[rq, :][:, :1] for rq, _ in rows])
                delta = jnp.stack([dlt[rq, :][:, :1] for rq, _ in rows])
                kb = jnp.stack([kf[rk, :] for _, rk in rows]).astype(MM)
                vb = jnp.stack([vf[rk, :] for _, rk in rows]).astype(MM)
                s = jnp.where(_band_mask(first, nblk), _bdot(qb, kb, 2, 2) * scale, NEG)
                p = jnp.exp(s - lse)
                dsc = (p * (_bdot(dmb, vb, 2, 2) - delta) * scale).astype(MM)
                dv = _bdot(p.astype(MM), dmb, 1, 1)
                dk = _bdot(dsc, qb, 1, 1)
                dq = _bdot(dsc, kb, 2, 1)
                for b, (rq, rk) in enumerate(rows):
                    dqf[rq, :] = dq[b]
                    dkf[rk, :] += dk[b]
                    dvf[rk, :] += dv[b]
                return carry

            @pl.when(pl.program_id(2) == g)
            def _(blocks=blocks, d=d, nblk=nblk):
                lax.fori_loop(0, d * nblk // ATT_BATCH, blocks, 0)

        dq_ref[...] = dqf[...].astype(dq_ref.dtype)
        dk, dv = dkf[pl.ds(pad, seq), :], dvf[pl.ds(pad, seq), :]
        if has_prev:
            dk, dv = dk + pkv_ref[0].astype(F32), dv + pkv_ref[1].astype(F32)
        dkv_ref[0] = dk.astype(dkv_ref.dtype)
        dkv_ref[1] = dv.astype(dkv_ref.dtype)

    per = kn // dh

    def both(b, h, g):
        f = g * N_HEADS + h
        return (0, f // per, b, f % per)

    kv_spec = pl.BlockSpec((2, None, seq, dh), both)
    full = pl.BlockSpec((None, seq, dh), lambda b, h, g: (0, b, h))
    in_specs = [qs, ks, vs, full, full, pl.BlockSpec((seq, dh), lambda b, h, g: (b, h))]
    args = [q, kv, kv, dm, merged, lse]
    if has_prev:
        in_specs.append(kv_spec)
        args.append(dkv_prev.reshape(2, N_DEV // 2, T, kn))
    short, padded = pltpu.VMEM((seq, dh), F32), pltpu.VMEM((_att_pad() + seq, dh), F32)
    dq, dkv = _call(body, name=name, grid=(nb, N_HEADS, N_GROUPS), in_specs=in_specs, out_specs=[qs, kv_spec],
                    out_shape=[_sds(q.shape, MM), _sds((2, N_DEV // 2, T, kn), MM)],
                    scratch=[short, padded, padded, short, padded, padded, short])(*args)
    return dq, dkv.reshape(N_DEV, T, kn)


def _adamw_math(w, g, m, v):
    m = ADAM_B1 * m + (1.0 - ADAM_B1) * g
    v = ADAM_B2 * v + (1.0 - ADAM_B2) * (g * g)
    m_hat = m / (1.0 - ADAM_B1 ** ADAM_STEP)
    v_hat = v / (1.0 - ADAM_B2 ** ADAM_STEP)
    delta = -ADAM_LR * (m_hat / (jnp.sqrt(v_hat) + ADAM_EPS) + ADAM_WD * w)
    return delta, m, v


def adamw_sum(name, w, m, v, parts, after):
    L, R, C = w.shape
    tr = _tile(R, 256)

    def body(*refs):
        w_ref, m_ref, v_ref = refs[:3]
        p_refs = refs[3:3 + L]
        g_ref, d_ref, nm_ref, nv_ref = refs[4 + L:]
        for l in range(L):
            @pl.when(pl.program_id(0) == l)
            def _(p_ref=p_refs[l]):
                g = p_ref[0].astype(F32)
                for k in range(1, N_DEV):
                    g = g + p_ref[k].astype(F32)
                g_ref[...] = g
                d_ref[...], nm_ref[...], nv_ref[...] = _adamw_math(w_ref[...], g, m_ref[...], v_ref[...])

    blk = pl.BlockSpec((None, tr, C), lambda l, i: (l, i, 0))
    part = lambda k: pl.BlockSpec((N_DEV, tr, C), lambda l, i: (0, jnp.where(l == k, i, 0), 0))
    return _call(body, name=name, grid=(L, R // tr),
                 in_specs=[blk, blk, blk] + [part(k) for k in range(L)] + [pl.BlockSpec(memory_space=pl.ANY)],
                 out_specs=[blk] * 4, out_shape=[_sds((L, R, C), F32)] * 4)(w, m, v, *parts, after)


def sum_partials(name, parts):
    _, R, C = parts.shape
    tr = _tile(R, 512)

    def body(p_ref, o_ref):
        g = p_ref[0]
        for k in range(1, N_DEV):
            g = g + p_ref[k]
        o_ref[...] = g

    return _call(body, name=name, grid=(R // tr,),
                 in_specs=[pl.BlockSpec((N_DEV, tr, C), lambda i: (0, i, 0))],
                 out_specs=pl.BlockSpec((tr, C), lambda i: (i, 0)), out_shape=_sds((R, C), F32))(parts)


def adamw_small(name, w, g, m, v):
    R, C = w.shape
    tr = _tile(R, 512)

    def body(w_ref, g_ref, m_ref, v_ref, d_ref, nm_ref, nv_ref):
        d_ref[...], nm_ref[...], nv_ref[...] = _adamw_math(w_ref[...], g_ref[...], m_ref[...], v_ref[...])

    blk = pl.BlockSpec((tr, C), lambda i: (i, 0))
    return _call(body, name=name, grid=(R // tr,), in_specs=[blk] * 4, out_specs=[blk] * 3,
                 out_shape=[_sds((R, C), F32)] * 3)(w, g, m, v)


def _pack(arrays):
    pieces = []
    for a in arrays:
        f = a.reshape(-1).astype(F32)
        pieces.append(jnp.pad(f, (0, (-f.shape[0]) % PACK)))
    return jnp.concatenate(pieces).reshape(-1, 128)


def _unpack(flat, shapes):
    out, off = [], 0
    f = flat.reshape(-1)
    for s in shapes:
        size = 1
        for d in s:
            size *= d
        out.append(f[off:off + size].reshape(s))
        off += size + (-size) % PACK
    return out


def kernel(x, mix_pre_g, mix_post_g, ffn_pre_g, ffn_post_g, cm_w_in, cm_b_in, cm_dw, cm_dw_b, cm_ln_g, cm_ln_b, cm_w_out, cm_b_out, kv_norm_g, w_kv, w_q, w_o, ffn_w_in, ffn_dw, ffn_dw_b, ffn_w_out, loss_target, m_mix_pre_g, m_mix_post_g, m_ffn_pre_g, m_ffn_post_g, m_cm_w_in, m_cm_b_in, m_cm_dw, m_cm_dw_b, m_cm_ln_g, m_cm_ln_b, m_cm_w_out, m_cm_b_out, m_kv_norm_g, m_w_kv, m_w_q, m_w_o, m_ffn_w_in, m_ffn_dw, m_ffn_dw_b, m_ffn_w_out, v_mix_pre_g, v_mix_post_g, v_ffn_pre_g, v_ffn_post_g, v_cm_w_in, v_cm_b_in, v_cm_dw, v_cm_dw_b, v_cm_ln_g, v_cm_ln_b, v_cm_w_out, v_cm_b_out, v_kv_norm_g, v_w_kv, v_w_q, v_w_o, v_ffn_w_in, v_ffn_dw, v_ffn_dw_b, v_ffn_w_out):
    nb, seq, D = x.shape
    T = nb * seq
    me = _my_index()
    n_b = DEPTH - N_A

    nf = ffn_w_in.shape[-1]
    t_ = lambda t: jnp.swapaxes(t, 1, 2)
    fin_t, m_fin_t, v_fin_t = t_(ffn_w_in), t_(m_ffn_w_in), t_(v_ffn_w_in)

    stages = [(part, i) for i in range(DEPTH) for part in ("mix", "ffn")]
    stages.insert(stages.index(("ffn", N_A - 1)) + 1, ("kv", N_A - 1))

    def stage_sources(stage):
        part, i = stage
        if part == "ffn":
            src = {"fin": fin_t[i], "fout": ffn_w_out[i]}
        elif part == "kv":
            src = {"kv": w_kv}
        elif i < N_A:
            src = {"cin": cm_w_in[i], "cout": cm_w_out[i]}
        else:
            src = {"q": w_q[i - N_A], "o": w_o[i - N_A]}
        return {k: t.astype(MM) for k, t in src.items()}

    def begin_gather(stage, after):
        src = stage_sources(stage)
        names, arrays = list(src), list(src.values())
        tag = f"{stage[0]}{stage[1]}"
        lands = place_own(f"gather_own_{tag}", "gather", arrays)
        handle, token = exchange_begin(f"gather_begin_{tag}", "gather", arrays, lands, after)
        return (names, handle), token

    def end_gather(stage, pending, after):
        names, handle = pending
        W = dict(zip(names, exchange_end(f"gather_end_{stage[0]}{stage[1]}", handle, after)))
        for k in W:
            if k in ("cout", "o"):
                W[k] = W[k].reshape(1, 1, D, D)
            elif k == "fout":
                W[k] = W[k].reshape(1, 4, nf, D)
            else:
                W[k] = W[k][:, None]
        return W

    small = [cm_b_in[:, None, :], cm_dw, cm_dw_b[:, None, :], cm_ln_g, cm_ln_b, cm_b_out, ffn_dw]
    Bcin, DWc, DWBc, LNg, LNb, Bcout, DWf = all_gather("gather_small", small, [False] * len(small))
    LNg = jnp.swapaxes(LNg, 0, 1).reshape(N_A, D)
    LNb = jnp.swapaxes(LNb, 0, 1).reshape(N_A, D)
    Bcout = jnp.swapaxes(Bcout, 0, 1).reshape(N_A, D)
    DWBf = ffn_dw_b.reshape(DEPTH, N_DEV, 1, nf)
    zero_bias = jnp.zeros((D,), F32)

    xs = x.reshape(T, D)
    sv = []
    kv = hkv = None
    pending, _ = begin_gather(stages[0], xs)
    Ws = {stages[0]: end_gather(stages[0], pending, xs)}
    sv = [{} for _ in range(DEPTH)]
    for idx, stage in enumerate(stages):
        part, i = stage
        L, W = sv[i], Ws[stage]
        gain = {"mix": mix_pre_g[i], "ffn": ffn_pre_g[i], "kv": kv_norm_g}[part]
        following = stages[idx + 1] if idx + 1 < len(stages) else None
        if following is not None:
            pending, token = begin_gather(following, xs)
            gain = gain + token[0, 0]
        if part == "mix":
            L["x_in"] = xs
            if i < N_A:
                L["u"], L["h"] = norm_mm(f"cm_in_{i}", xs, gain, W["cin"], 0, Bcin[:, i:i + 1])
                L["c"] = cm_glu_conv(f"cm_conv_{i}", L["u"], DWc, DWBc, i, seq)
                L["s"] = ln_silu(f"cm_ln_{i}", L["c"], LNg[i], LNb[i])
                L["y"], xs = mm_resid_norm(f"cm_out_{i}", L["s"], W["cout"], 0, Bcout[i], xs, mix_post_g[i])
            else:
                L["q"], L["h"] = norm_mm(f"attn_q_{i}", xs, gain, W["q"], 0)
                L["mg"], L["lse"] = attn_fwd(f"attn_{i}", L["q"], kv, nb, seq)
                L["y"], xs = mm_resid_norm(f"attn_out_{i}", L["mg"], W["o"], 0, zero_bias, xs, mix_post_g[i])
            L["x1"] = xs
        elif part == "ffn":
            L["uf"], L["hf"], L["z"], L["ag"] = ffn_in_conv(f"ffn_in_{i}", xs, gain, W["fin"], DWf, DWBf, i, seq)
            L["yf"], xs = mm_resid_norm(f"ffn_out_{i}", L["z"], W["fout"], 0, zero_bias, xs, ffn_post_g[i])
        else:
            kv, hkv = norm_mm("kv_proj", xs, gain, W["kv"], 0)
        if following is not None:
            Ws[following] = end_gather(following, pending, kv if part == "kv" else xs)
    dx, loss_part = loss_fwd_bwd("loss", xs, loss_target.reshape(T, D))

    g_mix_pre, g_mix_post, g_ffn_pre, g_ffn_post = [None] * DEPTH, [None] * DEPTH, [None] * DEPTH, [None] * DEPTH
    g_ffn_dw, g_ffn_dwb = [None] * DEPTH, [None] * DEPTH
    g_cbin, g_cdw, g_cdwb, g_lng, g_lnb, g_cbout = ([None] * N_A for _ in range(6))
    g_kvn = dkv = None
    landed = [{} for _ in range(DEPTH)]
    in_flight = token = None
    for stage in reversed(stages):
        part, i = stage
        L, W = sv[i], Ws[stage]
        gain = {"mix": mix_post_g[i], "ffn": ffn_post_g[i], "kv": kv_norm_g}[part]
        if token is not None:
            gain = gain + token[0, 0]
        send = {}
        if part == "kv":
            send["kv"] = mm_tn("kv_wg", hkv[None], dkv)
            dx, g_kvn = mm_nt_norm_bwd("kv_bwd", dkv, W["kv"], 0, sv[i + 1]["x_in"], gain, dx)
        elif part == "ffn":
            dyf, g_ffn_post[i], _ = resid_norm_bwd(f"ffn_post_bwd_{i}", dx, L["yf"], gain)
            dz = mm_nt(f"ffn_out_bwd_{i}", dyf, W["fout"], 0, MM)
            send["fout"] = mm_tn(f"ffn_out_wg_{i}", L["z"], dyf).reshape(N_DEV, nf // 2, D)
            duf, ddw, ddwb = ffn_conv_gate_bwd(f"ffn_conv_bwd_{i}", L["uf"], L["ag"], dz, DWf, i, seq)
            g_ffn_dw[i], g_ffn_dwb[i] = ddw.reshape(N_DEV, FFN_CONV_W, nf), ddwb.reshape(-1)
            send["fin"] = mm_tn(f"ffn_in_wg_{i}", duf, L["hf"][None])
            dx, g_ffn_pre[i] = mm_nt_norm_bwd(f"ffn_in_bwd_{i}", duf, W["fin"], 0, L["x1"], ffn_pre_g[i], dx, w_t=True)
        else:
            dy, g_mix_post[i], dyb = resid_norm_bwd(f"mix_post_bwd_{i}", dx, L["y"], gain)
            if i >= N_A:
                dm = mm_nt(f"attn_out_bwd_{i}", dy, W["o"], 0, F32)
                send["o"] = mm_tn(f"attn_out_wg_{i}", L["mg"], dy).reshape(N_DEV, D // N_DEV, D)
                dq, dkv = attn_bwd(f"attn_bwd_{i}", L["q"], kv, dm, L["mg"], L["lse"], dkv, nb, seq)
                send["q"] = mm_tn(f"attn_q_wg_{i}", L["h"][None], dq)
                dx, g_mix_pre[i] = mm_nt_norm_bwd(f"attn_q_bwd_{i}", dq, W["q"], 0, L["x_in"], mix_pre_g[i], dx)
            else:
                g_cbout[i] = dyb
                ds = mm_nt(f"cm_out_bwd_{i}", dy, W["cout"], 0, F32)
                send["cout"] = mm_tn(f"cm_out_wg_{i}", L["s"], dy).reshape(N_DEV, D // N_DEV, D)
                dc, g_lng[i], g_lnb[i] = ln_silu_bwd(f"cm_ln_bwd_{i}", ds, L["c"], LNg[i], LNb[i])
                du, g_cdw[i], g_cdwb[i], dbi = cm_glu_conv_bwd(f"cm_conv_bwd_{i}", L["u"], dc, DWc, i, seq)
                g_cbin[i] = dbi.reshape(N_DEV, -1)
                send["cin"] = mm_tn(f"cm_in_wg_{i}", L["h"][None], du)
                dx, g_mix_pre[i] = mm_nt_norm_bwd(f"cm_in_bwd_{i}", du, W["cin"], 0, L["x_in"], mix_pre_g[i], dx)
        if in_flight is not None:
            (p, j), names, handle = in_flight
            landed[j].update(zip(names, exchange_end(f"scatter_end_{p}{j}", handle, dx)))
        names, arrays = list(send), list(send.values())
        lands = place_own(f"scatter_own_{part}{i}", "scatter", arrays)
        handle, token = exchange_begin(f"scatter_begin_{part}{i}", "scatter", arrays, lands, dx)
        in_flight = (stage, names, handle)
    grad_x = dx.reshape(nb, seq, D)

    rep_names = ["mix_pre_g", "mix_post_g", "ffn_pre_g", "ffn_post_g", "kv_norm_g", "ffn_dw_b"]
    rep_parts = [jnp.concatenate(g_mix_pre), jnp.concatenate(g_mix_post), jnp.concatenate(g_ffn_pre),
                 jnp.concatenate(g_ffn_post), g_kvn.reshape(-1), jnp.stack(g_ffn_dwb)]
    rep_w = [mix_pre_g, mix_post_g, ffn_pre_g, ffn_post_g, kv_norm_g, ffn_dw_b]
    rep_m = [m_mix_pre_g, m_mix_post_g, m_ffn_pre_g, m_ffn_post_g, m_kv_norm_g, m_ffn_dw_b]
    rep_v = [v_mix_pre_g, v_mix_post_g, v_ffn_pre_g, v_ffn_post_g, v_kv_norm_g, v_ffn_dw_b]
    sh_names = ["ffn_dw", "cm_b_in", "cm_dw", "cm_dw_b", "cm_ln_g", "cm_ln_b", "cm_b_out"]
    own = lambda per_layer, shard: jnp.stack([p.reshape((N_DEV,) + shard) for p in per_layer], axis=1)
    sh_parts = [own(g_ffn_dw, ffn_dw.shape[1:]), own(g_cbin, cm_b_in.shape[1:]), own(g_cdw, cm_dw.shape[1:]),
                own(g_cdwb, cm_dw_b.shape[1:]), own(g_lng, cm_ln_g.shape[1:]), own(g_lnb, cm_ln_b.shape[1:]),
                own(g_cbout, cm_b_out.shape[1:])]
    sh_w = [ffn_dw, cm_b_in, cm_dw, cm_dw_b, cm_ln_g, cm_ln_b, cm_b_out]
    sh_m = [m_ffn_dw, m_cm_b_in, m_cm_dw, m_cm_dw_b, m_cm_ln_g, m_cm_ln_b, m_cm_b_out]
    sh_v = [v_ffn_dw, v_cm_b_in, v_cm_dw, v_cm_dw_b, v_cm_ln_g, v_cm_ln_b, v_cm_b_out]
    rep_pack = _pack([loss_part] + rep_parts)
    sh_pack = jnp.stack([_pack([p[k] for p in sh_parts]) for k in range(N_DEV)])
    n_rep = rep_pack.shape[0]
    small = []
    for kind, pack in (("gather", rep_pack), ("scatter", sh_pack)):
        lands = place_own(f"{kind}_own_small", kind, [pack])
        handle, token = exchange_begin(f"{kind}_begin_small", kind, [pack], lands, token)
        small.append((kind, handle))

    def big_update(name, w, m, v, key, layers, after):
        as3 = lambda t: t.reshape((-1,) + t.shape[-2:])
        outs = adamw_sum(name, as3(w), as3(m), as3(v), [landed[i][key] for i in layers], after)
        return [t.reshape(w.shape) for t in outs]

    conf, attn = range(N_A), range(N_A, DEPTH)
    upd = {}
    upd["ffn_w_in"] = [t_(t) for t in big_update("adam_ffn_w_in", fin_t, m_fin_t, v_fin_t, "fin", range(DEPTH), token)]
    upd["ffn_w_out"] = big_update("adam_ffn_w_out", ffn_w_out, m_ffn_w_out, v_ffn_w_out, "fout", range(DEPTH), token)
    upd["w_kv"] = big_update("adam_w_kv", w_kv, m_w_kv, v_w_kv, "kv", [N_A - 1], token)
    upd["w_q"] = big_update("adam_w_q", w_q, m_w_q, v_w_q, "q", attn, token)
    upd["w_o"] = big_update("adam_w_o", w_o, m_w_o, v_w_o, "o", attn, upd["w_q"][0])
    (p, j), names, handle = in_flight
    landed[j].update(zip(names, exchange_end(f"scatter_end_{p}{j}", handle, upd["w_o"][0])))
    upd["cm_w_in"] = big_update("adam_cm_w_in", cm_w_in, m_cm_w_in, v_cm_w_in, "cin", conf, token)
    upd["cm_w_out"] = big_update("adam_cm_w_out", cm_w_out, m_cm_w_out, v_cm_w_out, "cout", conf, upd["cm_w_in"][0])
    (rep_landed,), (sh_landed,) = (exchange_end(f"{kind}_end_small", handle, upd["cm_w_out"][0])
                                   for kind, handle in small)
    rep_sum = sum_partials("sum_small_rep", rep_landed)
    sh_sum = sum_partials("sum_small_sh", sh_landed)
    rep_shapes = [(1, 1)] + [w.shape for w in rep_w]
    sh_shapes = [w.shape for w in sh_w]
    g_small = jnp.concatenate([rep_sum, sh_sum])
    pad1 = jnp.zeros((1, 1), F32)
    d_s, m_s, v_s = adamw_small("adam_small", jnp.concatenate([_pack([pad1] + rep_w), _pack(sh_w)]), g_small,
                                jnp.concatenate([_pack([pad1] + rep_m), _pack(sh_m)]),
                                jnp.concatenate([_pack([pad1] + rep_v), _pack(sh_v)]))
    split = lambda t: (_unpack(t[:n_rep], rep_shapes), _unpack(t[n_rep:], sh_shapes))
    for (rep_t, sh_t), slot in zip([split(g_small), split(d_s), split(m_s), split(v_s)], range(4)):
        if slot == 0:
            loss = rep_t[0].reshape(())
        for name, t in zip(rep_names, rep_t[1:]):
            upd.setdefault(name, [None] * 4)[slot] = t
        for name, t in zip(sh_names, sh_t):
            upd.setdefault(name, [None] * 4)[slot] = t

    order = ["mix_pre_g", "mix_post_g", "ffn_pre_g", "ffn_post_g", "cm_w_in", "cm_b_in", "cm_dw", "cm_dw_b", "cm_ln_g",
             "cm_ln_b", "cm_w_out", "cm_b_out", "kv_norm_g", "w_kv", "w_q", "w_o", "ffn_w_in", "ffn_dw", "ffn_dw_b",
             "ffn_w_out"]
    return (loss, grad_x, *[upd[n][0] for n in order], *[upd[n][1] for n in order],
            *[upd[n][2] for n in order], *[upd[n][3] for n in order])
```

```python
import functools

import jax
import jax.numpy as jnp
from jax import lax
from jax.experimental import pallas as pl
from jax.experimental.pallas import tpu as pltpu

N_DEV = 8
N_A = 2
DEPTH = 4
N_HEADS = 8
N_GROUPS = 3
DILATIONS = (1, 4, 16)
ATT_BLOCK = 128
ATT_BATCH = 8
CONV_W = 31
FFN_CONV_W = 3
CONV_HALO = 32
FFN_HALO = 16
ROW_CHUNK_FWD = 16
ROW_CHUNK_BWD = 8
EPS = 1e-6
NEG = -1e30
ADAM_LR, ADAM_B1, ADAM_B2, ADAM_EPS, ADAM_WD, ADAM_STEP = 0.001, 0.9, 0.999, 1e-08, 0.01, 10
MM = jnp.bfloat16
F32 = jnp.float32
VMEM_LIMIT_BYTES = 56 * 1024 * 1024
PACK = 1024
MESH_ID = pl.DeviceIdType.MESH

_pallas = pl.pallas_call


def _call(body, *, name, out_shape, grid=(), in_specs=None, out_specs=None, scratch=()):
    return _pallas(body, name=name, out_shape=out_shape, grid=grid, in_specs=in_specs, out_specs=out_specs,
                   scratch_shapes=list(scratch),
                   compiler_params=pltpu.CompilerParams(vmem_limit_bytes=VMEM_LIMIT_BYTES))


def _tile(n, pref):
    if n <= pref:
        return n
    t = pref - pref % 8
    while n % t:
        t -= 8
    assert t > 0, (n, pref)
    return t


def _sds(shape, dtype):
    return jax.ShapeDtypeStruct(tuple(shape), dtype)


def _dot(a, b):
    return jnp.dot(a, b, preferred_element_type=F32)


def _dot_nt(a, b):
    return lax.dot_general(a, b, (((1,), (1,)), ((), ())), preferred_element_type=F32)


def _dot_tn(a, b):
    return lax.dot_general(a, b, (((0,), (0,)), ((), ())), preferred_element_type=F32)


def _sigmoid(x):
    return 1.0 / (1.0 + jnp.exp(-x))


def _my_index():
    return 4 * lax.axis_index("x") + 2 * lax.axis_index("y") + lax.axis_index("c")


def _exchange(name, arrays, out_shapes, pieces, src_of, dst_of):
    n = len(arrays)
    base = [sum(pieces[:a]) for a in range(n)]
    total = sum(pieces)

    def body(*refs):
        ins, outs = refs[:n], refs[n:2 * n]
        send_sems, recv_sems, local_sems = refs[2 * n:]
        x, y, c = lax.axis_index("x"), lax.axis_index("y"), lax.axis_index("c")
        me = 4 * x + 2 * y + c
        copies = []
        for a in range(n):
            for k, (s, d) in enumerate(zip(src_of(a, ins[a], me), dst_of(a, outs[a], me))):
                cp = pltpu.make_async_copy(s, d, local_sems.at[base[a] + k])
                cp.start()
                copies.append(cp)
        remote = []
        for m in range(1, N_DEV):
            px, py, pc = x ^ (m >> 2), y ^ ((m >> 1) & 1), c ^ (m & 1)
            peer = 4 * px + 2 * py + pc
            for a in range(n):
                for k, (s, d) in enumerate(zip(src_of(a, ins[a], peer), dst_of(a, outs[a], me))):
                    cp = pltpu.make_async_remote_copy(src_ref=s, dst_ref=d, send_sem=send_sems.at[base[a] + k, m - 1],
                                                      recv_sem=recv_sems.at[base[a] + k, m - 1],
                                                      device_id=(px, py, pc), device_id_type=MESH_ID)
                    cp.start()
                    remote.append(cp)
        for cp in copies:
            cp.wait()
        for cp in remote:
            cp.wait_send()
        for m in range(1, N_DEV):
            px, py, pc = x ^ (m >> 2), y ^ ((m >> 1) & 1), c ^ (m & 1)
            peer = 4 * px + 2 * py + pc
            for a in range(n):
                for k, (s, d) in enumerate(zip(src_of(a, ins[a], me), dst_of(a, outs[a], peer))):
                    pltpu.make_async_remote_copy(src_ref=s, dst_ref=d, send_sem=send_sems.at[base[a] + k, m - 1],
                                                 recv_sem=recv_sems.at[base[a] + k, m - 1], device_id=(px, py, pc),
                                                 device_id_type=MESH_ID).wait_recv()

    any_spec = pl.BlockSpec(memory_space=pl.ANY)
    return _call(body, name=name, out_shape=[_sds(s, a.dtype) for s, a in zip(out_shapes, arrays)],
                 in_specs=[any_spec] * n, out_specs=[any_spec] * n,
                 scratch=[pltpu.SemaphoreType.DMA((total, N_DEV - 1)), pltpu.SemaphoreType.DMA((total, N_DEV - 1)),
                          pltpu.SemaphoreType.DMA((total,))])(*arrays)


def all_gather(name, arrays, row_sharded):
    def out_shape(a):
        s = arrays[a].shape
        return (s[0], N_DEV) + s[1:] if row_sharded[a] else (N_DEV,) + s

    def src_of(a, ref, peer):
        if row_sharded[a]:
            return [ref.at[l] for l in range(arrays[a].shape[0])]
        return [ref]

    def dst_of(a, ref, me):
        if row_sharded[a]:
            return [ref.at[l, me] for l in range(arrays[a].shape[0])]
        return [ref.at[me]]

    pieces = [arrays[a].shape[0] if row_sharded[a] else 1 for a in range(len(arrays))]
    return _exchange(name, arrays, [out_shape(a) for a in range(len(arrays))], pieces, src_of, dst_of)


def _src_view(kind, ref, peer):
    return ref if kind == "gather" else ref.at[peer]


def _peers(x, y, c):
    for m in range(1, N_DEV):
        px, py, pc = x ^ (m >> 2), y ^ ((m >> 1) & 1), c ^ (m & 1)
        yield m - 1, (px, py, pc), 4 * px + 2 * py + pc


def place_own(name, kind, srcs):
    n = len(srcs)
    shapes = [(N_DEV,) + s.shape if kind == "gather" else s.shape for s in srcs]
    steps = 2 if all(s.shape[-2] % 32 == 0 for s in srcs) else 1

    def body(*refs):
        for a in range(n):
            refs[n + a][...] = refs[a][...]

    def spec(shape, own_block):
        R, C = shape[-2:]
        tr = R // steps
        if own_block:
            return pl.BlockSpec((None, tr, C), lambda i: (_my_index(), i, 0))
        return pl.BlockSpec((tr, C), lambda i: (i, 0))

    return _call(body, name=name, grid=(steps,), out_shape=[_sds(s, a.dtype) for s, a in zip(shapes, srcs)],
                 in_specs=[spec(s.shape, kind == "scatter") for s in srcs],
                 out_specs=[spec(s, True) for s in shapes])(*srcs)


_HBM_SPEC = pl.BlockSpec(memory_space=pltpu.HBM)
_SEM_SPEC = pl.BlockSpec(memory_space=pltpu.SEMAPHORE)
_DATAFLOW = pltpu.SideEffectType.DATAFLOW_SIDE_EFFECTING


def _remote(kind, src, land, send_sems, recv_sems, a, slot, frm, to_id, at):
    return pltpu.make_async_remote_copy(src_ref=_src_view(kind, src, frm), dst_ref=land.at[at],
                                        send_sem=send_sems.at[a * (N_DEV - 1) + slot],
                                        recv_sem=recv_sems.at[a * (N_DEV - 1) + slot],
                                        device_id=to_id, device_id_type=MESH_ID)


def exchange_begin(name, kind, srcs, lands, after):
    n = len(srcs)

    def body(*refs):
        ins, lnd = refs[:n], refs[n:2 * n]
        send_sems, recv_sems = refs[2 * n + 1], refs[2 * n + 2]
        token = refs[-1]
        x, y, c = lax.axis_index("x"), lax.axis_index("y"), lax.axis_index("c")
        me = 4 * x + 2 * y + c
        for slot, peer_id, peer in _peers(x, y, c):
            for a in range(n):
                _remote(kind, ins[a], lnd[a], send_sems, recv_sems, a, slot, peer, peer_id, me).start()
        token[...] = jnp.zeros_like(token)

    hbm = lambda t: pltpu.HBM(t.shape, t.dtype)
    outs = _pallas(
        body, name=name,
        out_shape=(pltpu.SemaphoreType.DMA((n * (N_DEV - 1),)), pltpu.SemaphoreType.DMA((n * (N_DEV - 1),)),
                   *[hbm(t) for t in srcs], *[hbm(t) for t in lands], _sds((8, 128), F32)),
        in_specs=[_HBM_SPEC] * (2 * n) + [pl.BlockSpec(memory_space=pl.ANY)],
        out_specs=(_SEM_SPEC, _SEM_SPEC, *[_HBM_SPEC] * (2 * n), pl.BlockSpec(memory_space=pltpu.VMEM)),
        input_output_aliases={i: 2 + i for i in range(2 * n)},
        compiler_params=pltpu.CompilerParams(has_side_effects=_DATAFLOW),
    )(*[pltpu.with_memory_space_constraint(t, pltpu.HBM) for t in list(srcs) + list(lands)], after)
    return (kind, outs[0], outs[1], list(outs[2:2 + n]), list(outs[2 + n:2 + 2 * n])), outs[-1]


def exchange_end(name, handle, after):
    kind, send_sems, recv_sems, srcs, lands = handle
    n = len(srcs)

    def body(*refs):
        ins, lnd = refs[:n], refs[n:2 * n]
        s_sems, r_sems = refs[2 * n], refs[2 * n + 1]
        x, y, c = lax.axis_index("x"), lax.axis_index("y"), lax.axis_index("c")
        me = 4 * x + 2 * y + c
        for slot, peer_id, peer in _peers(x, y, c):
            for a in range(n):
                _remote(kind, ins[a], lnd[a], s_sems, r_sems, a, slot, peer, peer_id, me).wait_send()
        for slot, peer_id, peer in _peers(x, y, c):
            for a in range(n):
                _remote(kind, ins[a], lnd[a], s_sems, r_sems, a, slot, me, peer_id, peer).wait_recv()

    hbm = lambda t: pltpu.HBM(t.shape, t.dtype)
    outs = _pallas(
        body, name=name, out_shape=tuple(hbm(t) for t in srcs + lands),
        in_specs=[_HBM_SPEC] * (2 * n) + [_SEM_SPEC, _SEM_SPEC, pl.BlockSpec(memory_space=pl.ANY)],
        out_specs=tuple([_HBM_SPEC] * (2 * n)), input_output_aliases={i: i for i in range(2 * n)},
        compiler_params=pltpu.CompilerParams(has_side_effects=_DATAFLOW),
    )(*srcs, *lands, send_sems, recv_sems, after)
    return list(outs[n:])


def norm_mm(name, x, gain, w, layer, bias=None, w_t=False):
    T, D = x.shape
    nsh = w.shape[0]
    n = w.shape[2] if w_t else w.shape[3]
    tm = _tile(T, 2048)

    def body(*refs):
        if bias is None:
            x_ref, g_ref, w_ref, u_ref, h_ref = refs
        else:
            x_ref, g_ref, w_ref, b_ref, u_ref, h_ref = refs

        @pl.when(pl.program_id(1) == 0)
        def _():
            xf = x_ref[...]
            r = lax.rsqrt(jnp.mean(xf * xf, axis=-1, keepdims=True) + EPS)
            h_ref[...] = (xf * r * g_ref[...]).astype(h_ref.dtype)

        acc = (_dot_nt if w_t else _dot)(h_ref[...], w_ref[...])
        if bias is not None:
            acc = acc + b_ref[...]
        u_ref[...] = acc.astype(u_ref.dtype)

    in_specs = [pl.BlockSpec((tm, D), lambda i, j: (i, 0)),
                pl.BlockSpec((1, D), lambda i, j: (0, 0)),
                pl.BlockSpec((None, None) + w.shape[2:], lambda i, j: (j, layer, 0, 0))]
    args = [x, gain.reshape(1, D), w]
    if bias is not None:
        in_specs.append(pl.BlockSpec((None, None, 1, n), lambda i, j: (j, layer, 0, 0)))
        args.append(bias)
    return _call(body, name=name, grid=(T // tm, nsh), in_specs=in_specs,
                 out_specs=[pl.BlockSpec((None, tm, n), lambda i, j: (j, i, 0)),
                            pl.BlockSpec((tm, D), lambda i, j: (i, 0))],
                 out_shape=[_sds((nsh, T, n), MM), _sds((T, D), MM)])(*args)


def mm_resid_norm(name, a, w, layer, bias, x, gain):
    nk, T, kk = a.shape
    D = x.shape[1]
    tm = _tile(T, 512)

    def body(a_ref, w_ref, b_ref, x_ref, g_ref, y_ref, xn_ref):
        y = _dot(a_ref[0], w_ref[0])
        for q in range(1, nk):
            y = y + _dot(a_ref[q], w_ref[q])
        y = y + b_ref[...]
        y_ref[...] = y
        r = lax.rsqrt(jnp.mean(y * y, axis=-1, keepdims=True) + EPS)
        xn_ref[...] = x_ref[...] + y * r * g_ref[...]

    return _call(body, name=name, grid=(T // tm,),
                 in_specs=[pl.BlockSpec((nk, tm, kk), lambda i: (0, i, 0)),
                           pl.BlockSpec((None, nk, kk, D), lambda i: (layer, 0, 0, 0)),
                           pl.BlockSpec((1, D), lambda i: (0, 0)),
                           pl.BlockSpec((tm, D), lambda i: (i, 0)),
                           pl.BlockSpec((1, D), lambda i: (0, 0))],
                 out_specs=[pl.BlockSpec((tm, D), lambda i: (i, 0))] * 2,
                 out_shape=[_sds((T, D), F32)] * 2)(a, w, bias.reshape(1, D), x, gain.reshape(1, D))


def _halo_maps(tm, hb, T):
    per = tm // hb
    last = T // hb - 1
    return (lambda i: jnp.maximum(i * per - 1, 0)), (lambda i: jnp.minimum((i + 1) * per, last))


def cm_glu_conv(name, u, dw, dwb, layer, seq):
    _, T, n = u.shape
    ct = dw.shape[-1]
    per = n // ct
    nct = 4 * per
    tm = _tile(seq, 512)
    tps = seq // tm
    hb = CONV_HALO
    prev, _ = _halo_maps(tm, hb, T)
    u4 = u.reshape(2, 4, T, n)

    def body(u_ref, uh_ref, w_ref, b_ref, o_ref, pad_ref):
        first = (pl.program_id(0) % tps) == 0
        um = u_ref[...].astype(F32)
        uh = uh_ref[...].astype(F32)
        pad_ref[pl.ds(hb, tm), :] = um[0] * _sigmoid(um[1])
        pad_ref[pl.ds(0, hb), :] = jnp.where(first, 0.0, uh[0] * _sigmoid(uh[1]))
        acc = jnp.zeros((tm, ct), F32) + b_ref[...]
        for k in range(CONV_W):
            acc = acc + w_ref[pl.ds(k, 1), :] * pad_ref[pl.ds(hb - (CONV_W - 1) + k, tm), :]
        o_ref[...] = acc

    return _call(body, name=name, grid=(T // tm, nct),
                 in_specs=[pl.BlockSpec((2, None, tm, ct), lambda i, c: (0, c // per, i, c % per)),
                           pl.BlockSpec((2, None, hb, ct), lambda i, c: (0, c // per, prev(i), c % per)),
                           pl.BlockSpec((None, None, CONV_W, ct), lambda i, c: (c, layer, 0, 0)),
                           pl.BlockSpec((None, None, 1, ct), lambda i, c: (c, layer, 0, 0))],
                 out_specs=pl.BlockSpec((tm, ct), lambda i, c: (i, c)),
                 out_shape=_sds((T, nct * ct), F32),
                 scratch=[pltpu.VMEM((tm + hb, ct), F32)])(u4, u4, dw, dwb)


def ln_silu(name, c, g, b):
    T, D = c.shape
    tm = _tile(T, 512)

    def body(c_ref, g_ref, b_ref, s_ref):
        cf = c_ref[...]
        mu = jnp.mean(cf, axis=-1, keepdims=True)
        xc = cf - mu
        r = lax.rsqrt(jnp.mean(xc * xc, axis=-1, keepdims=True) + EPS)
        t = xc * r * g_ref[...] + b_ref[...]
        s_ref[...] = (t * _sigmoid(t)).astype(s_ref.dtype)

    return _call(body, name=name, grid=(T // tm,),
                 in_specs=[pl.BlockSpec((tm, D), lambda i: (i, 0)), pl.BlockSpec((1, D), lambda i: (0, 0)),
                           pl.BlockSpec((1, D), lambda i: (0, 0))],
                 out_specs=pl.BlockSpec((None, tm, D), lambda i: (0, i, 0)),
                 out_shape=_sds((1, T, D), MM))(c, g.reshape(1, D), b.reshape(1, D))


def _col_tiles(n, width):
    return [(c0, min(width, n - c0)) for c0 in range(0, n, width)]


def ffn_in_conv(name, x, gain, wt, dw, dwb, layer, seq):
    T, D = x.shape
    n = wt.shape[2]
    tm = _tile(seq, 1024)
    tps = seq // tm
    hb = 8
    rows = _tile(tm, 2 * ROW_CHUNK_FWD)
    K = FFN_CONV_W
    w5 = wt.reshape(2, 4, n, D)
    dw4 = dw.reshape(2, 4, dw.shape[1], K, n)
    dwb4 = dwb.reshape(dwb.shape[0], 2, 4, 1, n)

    def body(x_ref, g_ref, w_ref, cw_ref, cb_ref, u_ref, h_ref, z_ref, ag_ref, pad_ref, carry_ref):
        i, j = pl.program_id(0), pl.program_id(1)
        first = (i % tps) == 0

        @pl.when(j == 0)
        def _():
            xf = x_ref[...]
            r = lax.rsqrt(jnp.mean(xf * xf, axis=-1, keepdims=True) + EPS)
            h_ref[...] = (xf * r * g_ref[...]).astype(h_ref.dtype)

        h = h_ref[...]
        for c0, wc in _col_tiles(n, 256):
            cols = pl.ds(c0, wc)
            for half in range(2):
                res = _dot_nt(h, w_ref[half, cols, :])
                u_ref[half, :, cols] = res.astype(u_ref.dtype)
                pad_ref[half, pl.ds(hb, tm), cols] = res
                pad_ref[half, pl.ds(0, hb), cols] = jnp.where(first, 0.0, carry_ref[j, half, :, cols])
            for r0 in range(0, tm, rows):
                conv = []
                for half in range(2):
                    acc = cb_ref[half, :, cols] + cw_ref[half, pl.ds(K - 1, 1), cols] * pad_ref[
                        half, pl.ds(hb + r0, rows), cols]
                    for k in range(K - 1):
                        acc = acc + cw_ref[half, pl.ds(k, 1), cols] * pad_ref[
                            half, pl.ds(hb + r0 - (K - 1) + k, rows), cols]
                    conv.append(acc)
                a, g = conv
                z_ref[pl.ds(r0, rows), cols] = (g * _sigmoid(g) * a).astype(z_ref.dtype)
                ag_ref[0, pl.ds(r0, rows), cols] = a.astype(ag_ref.dtype)
                ag_ref[1, pl.ds(r0, rows), cols] = g.astype(ag_ref.dtype)
            for half in range(2):
                carry_ref[j, half, :, cols] = pad_ref[half, pl.ds(tm, hb), cols]

    u, h, z, ag = _call(
        body, name=name, grid=(T // tm, 4),
        in_specs=[pl.BlockSpec((tm, D), lambda i, j: (i, 0)),
                  pl.BlockSpec((1, D), lambda i, j: (0, 0)),
                  pl.BlockSpec((2, None, n, D), lambda i, j: (0, j, 0, 0)),
                  pl.BlockSpec((2, None, None, K, n), lambda i, j: (0, j, layer, 0, 0)),
                  pl.BlockSpec((None, 2, None, 1, n), lambda i, j: (layer, 0, j, 0, 0))],
        out_specs=[pl.BlockSpec((2, None, tm, n), lambda i, j: (0, j, i, 0)),
                   pl.BlockSpec((tm, D), lambda i, j: (i, 0)),
                   pl.BlockSpec((None, tm, n), lambda i, j: (j, i, 0)),
                   pl.BlockSpec((2, None, tm, n), lambda i, j: (0, j, i, 0))],
        out_shape=[_sds((2, 4, T, n), MM), _sds((T, D), MM), _sds((4, T, n), MM), _sds((2, 4, T, n), MM)],
        scratch=[pltpu.VMEM((2, tm + hb, n), F32), pltpu.VMEM((4, 2, hb, n), F32)])(
            x, gain.reshape(1, D), w5, dw4, dwb4)
    return u.reshape(8, T, n), h, z, ag


def _head_specs(seq, dh, q_heads, kv_heads):
    def spec(per, base):
        def imap(b, h, g):
            f = base + g * N_HEADS + h
            return (f // per, b, f % per)
        return pl.BlockSpec((None, seq, dh), imap)
    return spec(q_heads, 0), spec(kv_heads, 0), spec(kv_heads, N_GROUPS * N_HEADS)


def _rows(start, d, blocks=1):
    size = blocks * ATT_BLOCK
    return pl.ds(start, size, stride=d) if d > 1 else pl.ds(start, size)


def _att_pad():
    return max(ATT_BLOCK * d for d in DILATIONS[:-1])


def _band_mask(first, nblk):
    keys = ATT_BLOCK if nblk == 1 else 2 * ATT_BLOCK
    shape = (ATT_BATCH, ATT_BLOCK, keys)
    qi = lax.broadcasted_iota(jnp.int32, shape, 1)
    kj = lax.broadcasted_iota(jnp.int32, shape, 2)
    if nblk == 1:
        return kj <= qi
    n = (first + lax.broadcasted_iota(jnp.int32, shape, 0)) % nblk
    return (kj >= qi) & (kj <= qi + ATT_BLOCK) & ((n > 0) | (kj >= ATT_BLOCK))


def _block_rows(idx, d, nblk):
    r, n = idx // nblk, idx % nblk
    rq = _rows(r + d * ATT_BLOCK * n, d)
    if nblk == 1:
        return rq, _rows(_att_pad() + r + d * ATT_BLOCK * n, d)
    return rq, _rows(_att_pad() + r + d * ATT_BLOCK * (n - 1), d, blocks=2)


def _bdot(a, b, ca, cb):
    return lax.dot_general(a, b, (((ca,), (cb,)), ((0,), (0,))), preferred_element_type=F32)


def attn_fwd(name, q, kv, nb, seq):
    _, T, qn = q.shape
    dh = qn * N_DEV // (N_GROUPS * N_HEADS)
    scale = 1.0 / (dh ** 0.5)
    qs, ks, vs = _head_specs(seq, dh, qn // dh, kv.shape[2] // dh)

    def body(q_ref, k_ref, v_ref, m_ref, l_ref, qf, kf, vf, *branch):
        og, lg = branch[:N_GROUPS], branch[N_GROUPS:]
        pad = _att_pad()
        qf[...] = q_ref[...].astype(F32)
        for t_ref, s_ref in ((k_ref, kf), (v_ref, vf)):
            s_ref[pl.ds(0, pad), :] = jnp.zeros((pad, dh), F32)
            s_ref[pl.ds(pad, seq), :] = t_ref[...].astype(F32)
        for g in range(N_GROUPS):
            d = DILATIONS[g]
            nblk = seq // d // ATT_BLOCK

            def blocks(it, carry, d=d, nblk=nblk, g=g):
                first = it * ATT_BATCH
                rows = [_block_rows(first + b, d, nblk) for b in range(ATT_BATCH)]
                qb = jnp.stack([qf[rq, :] for rq, _ in rows]).astype(MM)
                kb = jnp.stack([kf[rk, :] for _, rk in rows]).astype(MM)
                vb = jnp.stack([vf[rk, :] for _, rk in rows]).astype(MM)
                s = jnp.where(_band_mask(first, nblk), _bdot(qb, kb, 2, 2) * scale, NEG)
                m = jnp.max(s, axis=-1, keepdims=True)
                p = jnp.exp(s - m)
                den = jnp.sum(p, axis=-1, keepdims=True)
                o = _bdot(p.astype(MM), vb, 2, 1) / den
                lse = m + jnp.log(den)
                for b, (rq, _) in enumerate(rows):
                    og[g][rq, :] = o[b]
                    lg[g][rq, :] = jnp.broadcast_to(lse[b], (ATT_BLOCK, dh))
                return carry

            @pl.when(pl.program_id(2) == g)
            def _(blocks=blocks, d=d, nblk=nblk):
                lax.fori_loop(0, d * nblk // ATT_BATCH, blocks, 0)

        @pl.when(pl.program_id(2) == N_GROUPS - 1)
        def _():
            mx = jnp.maximum(jnp.maximum(lg[0][...], lg[1][...]), lg[2][...])
            e = [jnp.exp(lg[g][...] - mx) for g in range(N_GROUPS)]
            tot = e[0] + e[1] + e[2]
            m_ref[...] = ((e[0] * og[0][...] + e[1] * og[1][...] + e[2] * og[2][...]) / tot).astype(m_ref.dtype)
            l_ref[...] = mx + jnp.log(tot)

    return _call(body, name=name, grid=(nb, N_HEADS, N_GROUPS), in_specs=[qs, ks, vs],
                 out_specs=[pl.BlockSpec((None, seq, dh), lambda b, h, g: (0, b, h)),
                            pl.BlockSpec((seq, dh), lambda b, h, g: (b, h))],
                 out_shape=[_sds((1, T, N_HEADS * dh), MM), _sds((T, N_HEADS * dh), F32)],
                 scratch=[pltpu.VMEM((seq, dh), F32)] + [pltpu.VMEM((_att_pad() + seq, dh), F32)] * 2
                 + [pltpu.VMEM((seq, dh), F32)] * (2 * N_GROUPS))(q, kv, kv)


def loss_fwd_bwd(name, x, target):
    T, D = x.shape
    tm = _tile(T, 512)

    def body(x_ref, t_ref, dx_ref, l_ref):
        @pl.when(pl.program_id(0) == 0)
        def _():
            l_ref[...] = jnp.zeros_like(l_ref)
        err = x_ref[...] - t_ref[...]
        dx_ref[...] = err * (1.0 / D)
        l_ref[...] += 0.5 * jnp.sum(jnp.mean(err * err, axis=-1, keepdims=True), axis=0, keepdims=True)

    dx, l = _call(body, name=name, grid=(T // tm,),
                  in_specs=[pl.BlockSpec((tm, D), lambda i: (i, 0))] * 2,
                  out_specs=[pl.BlockSpec((tm, D), lambda i: (i, 0)), pl.BlockSpec((1, 1), lambda i: (0, 0))],
                  out_shape=[_sds((T, D), F32), _sds((1, 1), F32)])(x, target)
    return dx, l


def resid_norm_bwd(name, dx, y, gain):
    T, D = y.shape
    tm = _tile(T, 512)

    def body(dx_ref, y_ref, g_ref, dy_ref, dg_ref, db_ref):
        @pl.when(pl.program_id(0) == 0)
        def _():
            dg_ref[...] = jnp.zeros_like(dg_ref)
            db_ref[...] = jnp.zeros_like(db_ref)
        y = y_ref[...]
        d = dx_ref[...]
        r = lax.rsqrt(jnp.mean(y * y, axis=-1, keepdims=True) + EPS)
        yh = y * r
        dyh = d * g_ref[...]
        dy = r * (dyh - yh * jnp.mean(dyh * yh, axis=-1, keepdims=True))
        dy_ref[...] = dy.astype(dy_ref.dtype)
        dg_ref[...] += jnp.sum(d * yh, axis=0, keepdims=True)
        db_ref[...] += jnp.sum(dy, axis=0, keepdims=True)

    return _call(body, name=name, grid=(T // tm,),
                 in_specs=[pl.BlockSpec((tm, D), lambda i: (i, 0))] * 2 + [pl.BlockSpec((1, D), lambda i: (0, 0))],
                 out_specs=[pl.BlockSpec((None, tm, D), lambda i: (0, i, 0))] + [pl.BlockSpec((1, D), lambda i: (0, 0))] * 2,
                 out_shape=[_sds((1, T, D), MM), _sds((1, D), F32), _sds((1, D), F32)])(dx, y, gain.reshape(1, D))


def mm_nt(name, dy, w, layer, out_dtype):
    _, T, D = dy.shape
    _, nk, kk, _ = w.shape
    tm = _tile(T, 1024)

    def body(dy_ref, w_ref, o_ref):
        o_ref[...] = _dot_nt(dy_ref[...], w_ref[...]).astype(o_ref.dtype)

    return _call(body, name=name, grid=(T // tm, nk),
                 in_specs=[pl.BlockSpec((None, tm, D), lambda i, q: (0, i, 0)),
                           pl.BlockSpec((None, None, kk, D), lambda i, q: (layer, q, 0, 0))],
                 out_specs=pl.BlockSpec((None, tm, kk), lambda i, q: (q, i, 0)),
                 out_shape=_sds((nk, T, kk), out_dtype))(dy, w)


def mm_nt_norm_bwd(name, du, w, layer, x_in, gain, dx_res, w_t=False):
    nsh, T, n = du.shape
    D = x_in.shape[1]
    tm = _tile(T, 1024)

    def body(du_ref, w_ref, x_ref, g_ref, dr_ref, dx_ref, dg_ref, acc_ref):
        i, j = pl.program_id(0), pl.program_id(1)

        @pl.when((i == 0) & (j == 0))
        def _():
            dg_ref[...] = jnp.zeros_like(dg_ref)

        part = (_dot if w_t else _dot_nt)(du_ref[...], w_ref[...])

        @pl.when(j == 0)
        def _():
            acc_ref[...] = part

        @pl.when(j > 0)
        def _():
            acc_ref[...] += part

        @pl.when(j == nsh - 1)
        def _():
            x = x_ref[...]
            dh = acc_ref[...]
            r = lax.rsqrt(jnp.mean(x * x, axis=-1, keepdims=True) + EPS)
            xh = x * r
            dxh = dh * g_ref[...]
            dx_ref[...] = dr_ref[...] + r * (dxh - xh * jnp.mean(dxh * xh, axis=-1, keepdims=True))
            dg_ref[...] += jnp.sum(dh * xh, axis=0, keepdims=True)

    return _call(body, name=name, grid=(T // tm, nsh),
                 in_specs=[pl.BlockSpec((None, tm, n), lambda i, j: (j, i, 0)),
                           pl.BlockSpec((None, None) + w.shape[2:], lambda i, j: (j, layer, 0, 0)),
                           pl.BlockSpec((tm, D), lambda i, j: (i, 0)),
                           pl.BlockSpec((1, D), lambda i, j: (0, 0)),
                           pl.BlockSpec((tm, D), lambda i, j: (i, 0))],
                 out_specs=[pl.BlockSpec((tm, D), lambda i, j: (i, 0)), pl.BlockSpec((1, D), lambda i, j: (0, 0))],
                 out_shape=[_sds((T, D), F32), _sds((1, D), F32)],
                 scratch=[pltpu.VMEM((tm, D), F32)])(du, w, x_in, gain.reshape(1, D), dx_res)


def mm_tn(name, a, b):
    na, T, ka = a.shape
    nb, _, kb = b.shape
    nj = max(na, nb)

    def body(a_ref, b_ref, o_ref):
        o_ref[...] = _dot_tn(a_ref[...], b_ref[...]).astype(o_ref.dtype)

    return _call(body, name=name, grid=(nj,),
                 in_specs=[pl.BlockSpec((None, T, ka), (lambda j: (j, 0, 0)) if na > 1 else (lambda j: (0, 0, 0))),
                           pl.BlockSpec((None, T, kb), (lambda j: (j, 0, 0)) if nb > 1 else (lambda j: (0, 0, 0)))],
                 out_specs=pl.BlockSpec((None, ka, kb), lambda j: (j, 0, 0)),
                 out_shape=_sds((nj, ka, kb), MM))(a, b)


def ffn_conv_gate_bwd(name, u, ag, dz, dw, layer, seq):
    _, T, n = u.shape
    tm = _tile(seq, 512)
    tps = seq // tm
    hb = FFN_HALO
    _, nxt = _halo_maps(tm, hb, T)
    u4 = u.reshape(2, 4, T, n)
    dw4 = dw.reshape(2, 4, dw.shape[1], FFN_CONV_W, n)
    K = FFN_CONV_W
    te = tm + hb
    rows = _tile(hb, ROW_CHUNK_BWD)

    def body(u_ref, ag_ref, agn_ref, dz_ref, dzn_ref, w_ref, du_ref, ddw_ref, ddb_ref, uf_ref, agf_ref, dzf_ref,
             da_ref):
        i = pl.program_id(1)
        last = (i % tps) == tps - 1

        @pl.when(i == 0)
        def _():
            ddw_ref[...] = jnp.zeros_like(ddw_ref)
            ddb_ref[...] = jnp.zeros_like(ddb_ref)

        for half in range(2):
            uf_ref[half] = u_ref[half].astype(F32)
            agf_ref[half, pl.ds(0, tm), :] = ag_ref[half].astype(F32)
            agf_ref[half, pl.ds(tm, hb), :] = agn_ref[half].astype(F32)
        dzf_ref[pl.ds(0, tm), :] = dz_ref[...].astype(F32)
        dzf_ref[pl.ds(tm, hb), :] = jnp.where(last, 0.0, dzn_ref[...].astype(F32))
        for r0 in range(0, te, rows):
            a, g = agf_ref[0, pl.ds(r0, rows), :], agf_ref[1, pl.ds(r0, rows), :]
            dzc = dzf_ref[pl.ds(r0, rows), :]
            sg = _sigmoid(g)
            da_ref[0, pl.ds(r0, rows), :] = dzc * (g * sg)
            da_ref[1, pl.ds(r0, rows), :] = dzc * a * (sg * (1.0 + g * (1.0 - sg)))
        for half in range(2):
            tap_acc = [jnp.zeros((rows, n), F32) for _ in range(K)]
            bias_acc = jnp.zeros((rows, n), F32)
            for r0 in range(0, tm, rows):
                x = uf_ref[half, pl.ds(r0, rows), :]
                acc = None
                for k in range(K):
                    d = da_ref[half, pl.ds(r0 + K - 1 - k, rows), :]
                    acc = w_ref[half, pl.ds(k, 1), :] * d if acc is None else acc + w_ref[half, pl.ds(k, 1), :] * d
                    tap_acc[k] = tap_acc[k] + d * x
                    if k == K - 1:
                        bias_acc = bias_acc + d
                du_ref[half, pl.ds(r0, rows), :] = acc.astype(du_ref.dtype)
            for k in range(K):
                ddw_ref[half, pl.ds(k, 1), :] += jnp.sum(tap_acc[k], axis=0, keepdims=True)
            ddb_ref[half] += jnp.sum(bias_acc, axis=0, keepdims=True)

    du, ddw, ddb = _call(
        body, name=name, grid=(4, T // tm),
        in_specs=[pl.BlockSpec((2, None, tm, n), lambda j, i: (0, j, i, 0)),
                  pl.BlockSpec((2, None, tm, n), lambda j, i: (0, j, i, 0)),
                  pl.BlockSpec((2, None, hb, n), lambda j, i: (0, j, nxt(i), 0)),
                  pl.BlockSpec((None, tm, n), lambda j, i: (j, i, 0)),
                  pl.BlockSpec((None, hb, n), lambda j, i: (j, nxt(i), 0)),
                  pl.BlockSpec((2, None, None, K, n), lambda j, i: (0, j, layer, 0, 0))],
        out_specs=[pl.BlockSpec((2, None, tm, n), lambda j, i: (0, j, i, 0)),
                   pl.BlockSpec((2, None, K, n), lambda j, i: (0, j, 0, 0)),
                   pl.BlockSpec((2, None, 1, n), lambda j, i: (0, j, 0, 0))],
        out_shape=[_sds((2, 4, T, n), MM), _sds((2, 4, K, n), F32), _sds((2, 4, 1, n), F32)],
        scratch=[pltpu.VMEM((2, tm, n), F32), pltpu.VMEM((2, te, n), F32), pltpu.VMEM((te, n), F32),
                 pltpu.VMEM((2, te, n), F32)],
    )(u4, ag, ag, dz, dz, dw4)
    return du.reshape(8, T, n), ddw, ddb


def ffn_bwd(name, dy, wout, u, ag, dw, wt, x_in, gain, dx_res, layer, seq):
    _, T, D = dy.shape
    n = wt.shape[2]
    tm = _tile(seq, 512)
    tps = seq // tm
    hb = FFN_HALO
    _, nxt = _halo_maps(tm, hb, T)
    K = FFN_CONV_W
    te = tm + hb
    rows = _tile(hb, ROW_CHUNK_BWD)
    u4, w5 = u.reshape(2, 4, T, n), wt.reshape(2, 4, n, D)
    dw4 = dw.reshape(2, 4, dw.shape[1], K, n)
    tiles = _col_tiles(n, 256)

    def body(dy_ref, dyn_ref, wo_ref, u_ref, ag_ref, agn_ref, cw_ref, wt_ref, x_ref, g_ref, dr_ref,
             du_ref, ddw_ref, ddb_ref, dx_ref, dg_ref,
             uf_ref, agf_ref, dzf_ref, da_ref, acc_ref):
        i, j = pl.program_id(0), pl.program_id(1)
        last = (i % tps) == tps - 1

        @pl.when((i == 0) & (j == 0))
        def _():
            dg_ref[...] = jnp.zeros_like(dg_ref)

        @pl.when(i == 0)
        def _():
            ddw_ref[j] = jnp.zeros((2, K, n), F32)
            ddb_ref[j] = jnp.zeros((2, 1, n), F32)

        dyt, dyn = dy_ref[...], dyn_ref[...]
        for ci, (c0, wc) in enumerate(tiles):
            cols = pl.ds(c0, wc)
            dzf_ref[pl.ds(0, tm), cols] = _dot_nt(dyt, wo_ref[cols, :])
            dzf_ref[pl.ds(tm, hb), cols] = jnp.where(last, 0.0, _dot_nt(dyn, wo_ref[cols, :]))
            for half in range(2):
                uf_ref[half, :, cols] = u_ref[half, :, cols].astype(F32)
                agf_ref[half, pl.ds(0, tm), cols] = ag_ref[half, :, cols].astype(F32)
                agf_ref[half, pl.ds(tm, hb), cols] = agn_ref[half, :, cols].astype(F32)
            for r0 in range(0, te, rows):
                a, g = agf_ref[0, pl.ds(r0, rows), cols], agf_ref[1, pl.ds(r0, rows), cols]
                dzc = dzf_ref[pl.ds(r0, rows), cols]
                sg = _sigmoid(g)
                da_ref[0, pl.ds(r0, rows), cols] = dzc * (g * sg)
                da_ref[1, pl.ds(r0, rows), cols] = dzc * a * (sg * (1.0 + g * (1.0 - sg)))
            for half in range(2):
                tap_acc = [jnp.zeros((rows, wc), F32) for _ in range(K)]
                bias_acc = jnp.zeros((rows, wc), F32)
                for r0 in range(0, tm, rows):
                    xr = uf_ref[half, pl.ds(r0, rows), cols]
                    acc = None
                    for k in range(K):
                        d = da_ref[half, pl.ds(r0 + K - 1 - k, rows), cols]
                        term = cw_ref[half, pl.ds(k, 1), cols] * d
                        acc = term if acc is None else acc + term
                        tap_acc[k] = tap_acc[k] + d * xr
                        if k == K - 1:
                            bias_acc = bias_acc + d
                    du_ref[half, pl.ds(r0, rows), cols] = acc.astype(du_ref.dtype)
                for k in range(K):
                    ddw_ref[j, half, pl.ds(k, 1), cols] += jnp.sum(tap_acc[k], axis=0, keepdims=True)
                ddb_ref[j, half, :, cols] += jnp.sum(bias_acc, axis=0, keepdims=True)
            part = _dot(du_ref[0, :, cols], wt_ref[0, cols, :]) + _dot(du_ref[1, :, cols], wt_ref[1, cols, :])
            if ci == 0:
                acc_ref[...] = part + jnp.where(j == 0, 0.0, acc_ref[...])
            else:
                acc_ref[...] += part
        @pl.when(j == 3)
        def _():
            x = x_ref[...]
            dh = acc_ref[...]
            r = lax.rsqrt(jnp.mean(x * x, axis=-1, keepdims=True) + EPS)
            xh = x * r
            dxh = dh * g_ref[...]
            dx_ref[...] = dr_ref[...] + r * (dxh - xh * jnp.mean(dxh * xh, axis=-1, keepdims=True))
            dg_ref[...] += jnp.sum(dh * xh, axis=0, keepdims=True)

    f32 = lambda *shape: pltpu.VMEM(shape, F32)
    du, ddw, ddb, dx, dg = _call(
        body, name=name, grid=(T // tm, 4),
        in_specs=[pl.BlockSpec((None, tm, D), lambda i, j: (0, i, 0)),
                  pl.BlockSpec((None, hb, D), lambda i, j: (0, nxt(i), 0)),
                  pl.BlockSpec((None, None, n, D), lambda i, j: (0, j, 0, 0)),
                  pl.BlockSpec((2, None, tm, n), lambda i, j: (0, j, i, 0)),
                  pl.BlockSpec((2, None, tm, n), lambda i, j: (0, j, i, 0)),
                  pl.BlockSpec((2, None, hb, n), lambda i, j: (0, j, nxt(i), 0)),
                  pl.BlockSpec((2, None, None, K, n), lambda i, j: (0, j, layer, 0, 0)),
                  pl.BlockSpec((2, None, n, D), lambda i, j: (0, j, 0, 0)),
                  pl.BlockSpec((tm, D), lambda i, j: (i, 0)),
                  pl.BlockSpec((1, D), lambda i, j: (0, 0)),
                  pl.BlockSpec((tm, D), lambda i, j: (i, 0))],
        out_specs=[pl.BlockSpec((2, None, tm, n), lambda i, j: (0, j, i, 0)),
                   pl.BlockSpec((4, 2, K, n), lambda i, j: (0, 0, 0, 0)),
                   pl.BlockSpec((4, 2, 1, n), lambda i, j: (0, 0, 0, 0)),
                   pl.BlockSpec((tm, D), lambda i, j: (i, 0)),
                   pl.BlockSpec((1, D), lambda i, j: (0, 0))],
        out_shape=[_sds((2, 4, T, n), MM), _sds((4, 2, K, n), F32), _sds((4, 2, 1, n), F32), _sds((T, D), F32),
                   _sds((1, D), F32)],
        scratch=[f32(2, tm, n), f32(2, te, n), f32(te, n), f32(2, te, n), f32(tm, D)],
    )(dy, dy, wout, u4, ag, ag, dw4, w5, x_in, gain.reshape(1, D), dx_res)
    return du.reshape(8, T, n), jnp.swapaxes(ddw, 0, 1), jnp.swapaxes(ddb, 0, 1), dx, dg


def ln_silu_bwd(name, ds, c, g, b):
    T, D = c.shape
    tm = _tile(T, 512)

    def body(ds_ref, c_ref, g_ref, b_ref, dc_ref, dg_ref, db_ref):
        @pl.when(pl.program_id(0) == 0)
        def _():
            dg_ref[...] = jnp.zeros_like(dg_ref)
            db_ref[...] = jnp.zeros_like(db_ref)
        cf = c_ref[...]
        mu = jnp.mean(cf, axis=-1, keepdims=True)
        xc = cf - mu
        r = lax.rsqrt(jnp.mean(xc * xc, axis=-1, keepdims=True) + EPS)
        xh = xc * r
        t = xh * g_ref[...] + b_ref[...]
        sg = _sigmoid(t)
        dt = ds_ref[...] * (sg * (1.0 + t * (1.0 - sg)))
        dg_ref[...] += jnp.sum(dt * xh, axis=0, keepdims=True)
        db_ref[...] += jnp.sum(dt, axis=0, keepdims=True)
        dxh = dt * g_ref[...]
        dc_ref[...] = r * (dxh - jnp.mean(dxh, axis=-1, keepdims=True)
                           - xh * jnp.mean(dxh * xh, axis=-1, keepdims=True))

    vec = pl.BlockSpec((1, D), lambda i: (0, 0))
    return _call(body, name=name, grid=(T // tm,),
                 in_specs=[pl.BlockSpec((None, tm, D), lambda i: (0, i, 0)), pl.BlockSpec((tm, D), lambda i: (i, 0)),
                           vec, vec],
                 out_specs=[pl.BlockSpec((tm, D), lambda i: (i, 0)), vec, vec],
                 out_shape=[_sds((T, D), F32), _sds((1, D), F32), _sds((1, D), F32)])(
                     ds, c, g.reshape(1, D), b.reshape(1, D))


def cm_glu_conv_bwd(name, u, dc, dw, layer, seq):
    _, T, n = u.shape
    ct = dw.shape[-1]
    per = n // ct
    nct = 4 * per
    tm = _tile(seq, 512)
    tps = seq // tm
    hb = CONV_HALO
    prev, nxt = _halo_maps(tm, hb, T)
    u4 = u.reshape(2, 4, T, n)
    K = CONV_W

    def body(u_ref, up_ref, dc_ref, dcn_ref, w_ref, du_ref, ddw_ref, ddb_ref, dbi_ref, padp_ref, padd_ref):
        i = pl.program_id(1)
        first = (i % tps) == 0
        last = (i % tps) == tps - 1

        @pl.when(i == 0)
        def _():
            ddw_ref[...] = jnp.zeros_like(ddw_ref)
            ddb_ref[...] = jnp.zeros_like(ddb_ref)
            dbi_ref[...] = jnp.zeros_like(dbi_ref)

        um = u_ref[...].astype(F32)
        uh = up_ref[...].astype(F32)
        sg = _sigmoid(um[1])
        padp_ref[pl.ds(hb, tm), :] = um[0] * sg
        padp_ref[pl.ds(0, hb), :] = jnp.where(first, 0.0, uh[0] * _sigmoid(uh[1]))
        dcm = dc_ref[...]
        padd_ref[pl.ds(0, tm), :] = dcm
        padd_ref[pl.ds(tm, hb), :] = jnp.where(last, 0.0, dcn_ref[...])
        dp = jnp.zeros((tm, ct), F32)
        for k in range(K):
            dp = dp + w_ref[pl.ds(k, 1), :] * padd_ref[pl.ds(K - 1 - k, tm), :]
            ddw_ref[pl.ds(k, 1), :] += jnp.sum(dcm * padp_ref[pl.ds(hb - (K - 1) + k, tm), :], axis=0, keepdims=True)
        ddb_ref[...] += jnp.sum(dcm, axis=0, keepdims=True)
        dv = dp * sg
        dg = dp * um[0] * sg * (1.0 - sg)
        du_ref[0] = dv.astype(du_ref.dtype)
        du_ref[1] = dg.astype(du_ref.dtype)
        dbi_ref[0] += jnp.sum(dv, axis=0, keepdims=True)
        dbi_ref[1] += jnp.sum(dg, axis=0, keepdims=True)

    du, ddw, ddb, dbi = _call(
        body, name=name, grid=(nct, T // tm),
        in_specs=[pl.BlockSpec((2, None, tm, ct), lambda c, i: (0, c // per, i, c % per)),
                  pl.BlockSpec((2, None, hb, ct), lambda c, i: (0, c // per, prev(i), c % per)),
                  pl.BlockSpec((tm, ct), lambda c, i: (i, c)),
                  pl.BlockSpec((hb, ct), lambda c, i: (nxt(i), c)),
                  pl.BlockSpec((None, None, K, ct), lambda c, i: (c, layer, 0, 0))],
        out_specs=[pl.BlockSpec((2, None, tm, ct), lambda c, i: (0, c // per, i, c % per)),
                   pl.BlockSpec((None, K, ct), lambda c, i: (c, 0, 0)),
                   pl.BlockSpec((None, 1, ct), lambda c, i: (c, 0, 0)),
                   pl.BlockSpec((2, None, 1, ct), lambda c, i: (0, c // per, 0, c % per))],
        out_shape=[_sds((2, 4, T, n), MM), _sds((nct, K, ct), F32), _sds((nct, 1, ct), F32), _sds((2, 4, 1, n), F32)],
        scratch=[pltpu.VMEM((tm + hb, ct), F32), pltpu.VMEM((tm + hb, ct), F32)])(u4, u4, dc, dc, dw)
    return du.reshape(8, T, n), ddw, ddb, dbi


def attn_bwd(name, q, kv, dm, merged, lse, dkv_prev, nb, seq):
    _, T, qn = q.shape
    kn = kv.shape[2]
    dh = qn * N_DEV // (N_GROUPS * N_HEADS)
    scale = 1.0 / (dh ** 0.5)
    qs, ks, vs = _head_specs(seq, dh, qn // dh, kn // dh)
    has_prev = dkv_prev is not None
    n_in = 6 + (1 if has_prev else 0)

    def body(*refs):
        q_ref, k_ref, v_ref, dm_ref, mg_ref, l_ref = refs[:6]
        pkv_ref = refs[6] if has_prev else None
        dq_ref, dkv_ref = refs[n_in:n_in + 2]
        qf, kf, vf, dqf, dkf, dvf, dlt = refs[n_in + 2:]
        pad = _att_pad()

        @pl.when(pl.program_id(2) == 0)
        def _():
            dlt[...] = jnp.broadcast_to(
                jnp.sum(dm_ref[...] * mg_ref[...].astype(F32), axis=-1, keepdims=True), (seq, dh))

        qf[...] = q_ref[...].astype(F32)
        for t_ref, s_ref in ((k_ref, kf), (v_ref, vf)):
            s_ref[pl.ds(0, pad), :] = jnp.zeros((pad, dh), F32)
            s_ref[pl.ds(pad, seq), :] = t_ref[...].astype(F32)
        dkf[...] = jnp.zeros_like(dkf)
        dvf[...] = jnp.zeros_like(dvf)
        for g in range(N_GROUPS):
            d = DILATIONS[g]
            nblk = seq // d // ATT_BLOCK

            def blocks(it, carry, d=d, nblk=nblk):
                first = it * ATT_BATCH
                rows = [_block_rows(first + b, d, nblk) for b in range(ATT_BATCH)]
                qb = jnp.stack([qf[rq, :] for rq, _ in rows]).astype(MM)
                dmb = jnp.stack([dm_ref[rq, :] for rq, _ in rows]).astype(MM)
                lse = jnp.stack([l_ref[rq, :][:, :1] for rq, _ in rows])
                delta = jnp.stack([dlt[rq, :][:, :1] for rq, _ in rows])
                kb = jnp.stack([kf[rk, :] for _, rk in rows]).astype(MM)
                vb = jnp.stack([vf[rk, :] for _, rk in rows]).astype(MM)
                s = jnp.where(_band_mask(first, nblk), _bdot(qb, kb, 2, 2) * scale, NEG)
                p = jnp.exp(s - lse)
                dsc = (p * (_bdot(dmb, vb, 2, 2) - delta) * scale).astype(MM)
                dv = _bdot(p.astype(MM), dmb, 1, 1)
                dk = _bdot(dsc, qb, 1, 1)
                dq = _bdot(dsc, kb, 2, 1)
                for b, (rq, rk) in enumerate(rows):
                    dqf[rq, :] = dq[b]
                    dkf[rk, :] += dk[b]
                    dvf[rk, :] += dv[b]
                return carry

            @pl.when(pl.program_id(2) == g)
            def _(blocks=blocks, d=d, nblk=nblk):
                lax.fori_loop(0, d * nblk // ATT_BATCH, blocks, 0)

        dq_ref[...] = dqf[...].astype(dq_ref.dtype)
        dk, dv = dkf[pl.ds(pad, seq), :], dvf[pl.ds(pad, seq), :]
        if has_prev:
            dk, dv = dk + pkv_ref[0].astype(F32), dv + pkv_ref[1].astype(F32)
        dkv_ref[0] = dk.astype(dkv_ref.dtype)
        dkv_ref[1] = dv.astype(dkv_ref.dtype)

    per = kn // dh

    def both(b, h, g):
        f = g * N_HEADS + h
        return (0, f // per, b, f % per)

    kv_spec = pl.BlockSpec((2, None, seq, dh), both)
    full = pl.BlockSpec((None, seq, dh), lambda b, h, g: (0, b, h))
    in_specs = [qs, ks, vs, full, full, pl.BlockSpec((seq, dh), lambda b, h, g: (b, h))]
    args = [q, kv, kv, dm, merged, lse]
    if has_prev:
        in_specs.append(kv_spec)
        args.append(dkv_prev.reshape(2, N_DEV // 2, T, kn))
    short, padded = pltpu.VMEM((seq, dh), F32), pltpu.VMEM((_att_pad() + seq, dh), F32)
    dq, dkv = _call(body, name=name, grid=(nb, N_HEADS, N_GROUPS), in_specs=in_specs, out_specs=[qs, kv_spec],
                    out_shape=[_sds(q.shape, MM), _sds((2, N_DEV // 2, T, kn), MM)],
                    scratch=[short, padded, padded, short, padded, padded, short])(*args)
    return dq, dkv.reshape(N_DEV, T, kn)


def _adamw_math(w, g, m, v):
    m = ADAM_B1 * m + (1.0 - ADAM_B1) * g
    v = ADAM_B2 * v + (1.0 - ADAM_B2) * (g * g)
    m_hat = m / (1.0 - ADAM_B1 ** ADAM_STEP)
    v_hat = v / (1.0 - ADAM_B2 ** ADAM_STEP)
    delta = -ADAM_LR * (m_hat / (jnp.sqrt(v_hat) + ADAM_EPS) + ADAM_WD * w)
    return delta, m, v


def adamw_sum(name, w, m, v, parts, after):
    L, R, C = w.shape
    tr = _tile(R, 256)

    def body(*refs):
        w_ref, m_ref, v_ref = refs[:3]
        p_refs = refs[3:3 + L]
        g_ref, d_ref, nm_ref, nv_ref = refs[4 + L:]
        for l in range(L):
            @pl.when(pl.program_id(0) == l)
            def _(p_ref=p_refs[l]):
                g = p_ref[0].astype(F32)
                for k in range(1, N_DEV):
                    g = g + p_ref[k].astype(F32)
                g_ref[...] = g
                d_ref[...], nm_ref[...], nv_ref[...] = _adamw_math(w_ref[...], g, m_ref[...], v_ref[...])

    blk = pl.BlockSpec((None, tr, C), lambda l, i: (l, i, 0))
    part = lambda k: pl.BlockSpec((N_DEV, tr, C), lambda l, i: (0, jnp.where(l == k, i, 0), 0))
    return _call(body, name=name, grid=(L, R // tr),
                 in_specs=[blk, blk, blk] + [part(k) for k in range(L)] + [pl.BlockSpec(memory_space=pl.ANY)],
                 out_specs=[blk] * 4, out_shape=[_sds((L, R, C), F32)] * 4)(w, m, v, *parts, after)


def sum_partials(name, parts):
    _, R, C = parts.shape
    tr = _tile(R, 512)

    def body(p_ref, o_ref):
        g = p_ref[0]
        for k in range(1, N_DEV):
            g = g + p_ref[k]
        o_ref[...] = g

    return _call(body, name=name, grid=(R // tr,),
                 in_specs=[pl.BlockSpec((N_DEV, tr, C), lambda i: (0, i, 0))],
                 out_specs=pl.BlockSpec((tr, C), lambda i: (i, 0)), out_shape=_sds((R, C), F32))(parts)


def adamw_small(name, w, g, m, v):
    R, C = w.shape
    tr = _tile(R, 512)

    def body(w_ref, g_ref, m_ref, v_ref, d_ref, nm_ref, nv_ref):
        d_ref[...], nm_ref[...], nv_ref[...] = _adamw_math(w_ref[...], g_ref[...], m_ref[...], v_ref[...])

    blk = pl.BlockSpec((tr, C), lambda i: (i, 0))
    return _call(body, name=name, grid=(R // tr,), in_specs=[blk] * 4, out_specs=[blk] * 3,
                 out_shape=[_sds((R, C), F32)] * 3)(w, g, m, v)


def _pack(arrays):
    pieces = []
    for a in arrays:
        f = a.reshape(-1).astype(F32)
        pieces.append(jnp.pad(f, (0, (-f.shape[0]) % PACK)))
    return jnp.concatenate(pieces).reshape(-1, 128)


def _unpack(flat, shapes):
    out, off = [], 0
    f = flat.reshape(-1)
    for s in shapes:
        size = 1
        for d in s:
            size *= d
        out.append(f[off:off + size].reshape(s))
        off += size + (-size) % PACK
    return out


def kernel(x, mix_pre_g, mix_post_g, ffn_pre_g, ffn_post_g, cm_w_in, cm_b_in, cm_dw, cm_dw_b, cm_ln_g, cm_ln_b, cm_w_out, cm_b_out, kv_norm_g, w_kv, w_q, w_o, ffn_w_in, ffn_dw, ffn_dw_b, ffn_w_out, loss_target, m_mix_pre_g, m_mix_post_g, m_ffn_pre_g, m_ffn_post_g, m_cm_w_in, m_cm_b_in, m_cm_dw, m_cm_dw_b, m_cm_ln_g, m_cm_ln_b, m_cm_w_out, m_cm_b_out, m_kv_norm_g, m_w_kv, m_w_q, m_w_o, m_ffn_w_in, m_ffn_dw, m_ffn_dw_b, m_ffn_w_out, v_mix_pre_g, v_mix_post_g, v_ffn_pre_g, v_ffn_post_g, v_cm_w_in, v_cm_b_in, v_cm_dw, v_cm_dw_b, v_cm_ln_g, v_cm_ln_b, v_cm_w_out, v_cm_b_out, v_kv_norm_g, v_w_kv, v_w_q, v_w_o, v_ffn_w_in, v_ffn_dw, v_ffn_dw_b, v_ffn_w_out):
    nb, seq, D = x.shape
    T = nb * seq
    me = _my_index()
    n_b = DEPTH - N_A

    nf = ffn_w_in.shape[-1]
    t_ = lambda t: jnp.swapaxes(t, 1, 2)
    fin_t, m_fin_t, v_fin_t = t_(ffn_w_in), t_(m_ffn_w_in), t_(v_ffn_w_in)

    stages = [(part, i) for i in range(DEPTH) for part in ("mix", "ffn")]
    stages.insert(stages.index(("ffn", N_A - 1)) + 1, ("kv", N_A - 1))

    def stage_sources(stage):
        part, i = stage
        if part == "ffn":
            src = {"fin": fin_t[i], "fout": ffn_w_out[i]}
        elif part == "kv":
            src = {"kv": w_kv}
        elif i < N_A:
            src = {"cin": cm_w_in[i], "cout": cm_w_out[i]}
        else:
            src = {"q": w_q[i - N_A], "o": w_o[i - N_A]}
        return {k: t.astype(MM) for k, t in src.items()}

    def begin_gather(stage, after):
        src = stage_sources(stage)
        names, arrays = list(src), list(src.values())
        tag = f"{stage[0]}{stage[1]}"
        lands = place_own(f"gather_own_{tag}", "gather", arrays)
        handle, token = exchange_begin(f"gather_begin_{tag}", "gather", arrays, lands, after)
        return (names, handle), token

    def end_gather(stage, pending, after):
        names, handle = pending
        W = dict(zip(names, exchange_end(f"gather_end_{stage[0]}{stage[1]}", handle, after)))
        for k in W:
            if k in ("cout", "o"):
                W[k] = W[k].reshape(1, 1, D, D)
            elif k == "fout":
                W[k] = W[k].reshape(1, 4, nf, D)
            else:
                W[k] = W[k][:, None]
        return W

    small = [cm_b_in[:, None, :], cm_dw, cm_dw_b[:, None, :], cm_ln_g, cm_ln_b, cm_b_out, ffn_dw]
    Bcin, DWc, DWBc, LNg, LNb, Bcout, DWf = all_gather("gather_small", small, [False] * len(small))
    LNg = jnp.swapaxes(LNg, 0, 1).reshape(N_A, D)
    LNb = jnp.swapaxes(LNb, 0, 1).reshape(N_A, D)
    Bcout = jnp.swapaxes(Bcout, 0, 1).reshape(N_A, D)
    DWBf = ffn_dw_b.reshape(DEPTH, N_DEV, 1, nf)
    zero_bias = jnp.zeros((D,), F32)

    xs = x.reshape(T, D)
    sv = []
    kv = hkv = None
    pending, _ = begin_gather(stages[0], xs)
    Ws = {stages[0]: end_gather(stages[0], pending, xs)}
    sv = [{} for _ in range(DEPTH)]
    for idx, stage in enumerate(stages):
        part, i = stage
        L, W = sv[i], Ws[stage]
        gain = {"mix": mix_pre_g[i], "ffn": ffn_pre_g[i], "kv": kv_norm_g}[part]
        following = stages[idx + 1] if idx + 1 < len(stages) else None
        if following is not None:
            pending, token = begin_gather(following, xs)
            gain = gain + token[0, 0]
        if part == "mix":
            L["x_in"] = xs
            if i < N_A:
                L["u"], L["h"] = norm_mm(f"cm_in_{i}", xs, gain, W["cin"], 0, Bcin[:, i:i + 1])
                L["c"] = cm_glu_conv(f"cm_conv_{i}", L["u"], DWc, DWBc, i, seq)
                L["s"] = ln_silu(f"cm_ln_{i}", L["c"], LNg[i], LNb[i])
                L["y"], xs = mm_resid_norm(f"cm_out_{i}", L["s"], W["cout"], 0, Bcout[i], xs, mix_post_g[i])
            else:
                L["q"], L["h"] = norm_mm(f"attn_q_{i}", xs, gain, W["q"], 0)
                L["mg"], L["lse"] = attn_fwd(f"attn_{i}", L["q"], kv, nb, seq)
                L["y"], xs = mm_resid_norm(f"attn_out_{i}", L["mg"], W["o"], 0, zero_bias, xs, mix_post_g[i])
            L["x1"] = xs
        elif part == "ffn":
            L["uf"], L["hf"], L["z"], L["ag"] = ffn_in_conv(f"ffn_in_{i}", xs, gain, W["fin"], DWf, DWBf, i, seq)
            L["yf"], xs = mm_resid_norm(f"ffn_out_{i}", L["z"], W["fout"], 0, zero_bias, xs, ffn_post_g[i])
        else:
            kv, hkv = norm_mm("kv_proj", xs, gain, W["kv"], 0)
        if following is not None:
            Ws[following] = end_gather(following, pending, kv if part == "kv" else xs)
    dx, loss_part = loss_fwd_bwd("loss", xs, loss_target.reshape(T, D))

    g_mix_pre, g_mix_post, g_ffn_pre, g_ffn_post = [None] * DEPTH, [None] * DEPTH, [None] * DEPTH, [None] * DEPTH
    g_ffn_dw, g_ffn_dwb = [None] * DEPTH, [None] * DEPTH
    g_cbin, g_cdw, g_cdwb, g_lng, g_lnb, g_cbout = ([None] * N_A for _ in range(6))
    g_kvn = dkv = None
    landed = [{} for _ in range(DEPTH)]
    in_flight = token = None
    for stage in reversed(stages):
        part, i = stage
        L, W = sv[i], Ws[stage]
        gain = {"mix": mix_post_g[i], "ffn": ffn_post_g[i], "kv": kv_norm_g}[part]
        if token is not None:
            gain = gain + token[0, 0]
        send = {}
        if part == "kv":
            send["kv"] = mm_tn("kv_wg", hkv[None], dkv)
            dx, g_kvn = mm_nt_norm_bwd("kv_bwd", dkv, W["kv"], 0, sv[i + 1]["x_in"], gain, dx)
        elif part == "ffn":
            dyf, g_ffn_post[i], _ = resid_norm_bwd(f"ffn_post_bwd_{i}", dx, L["yf"], gain)
            send["fout"] = mm_tn(f"ffn_out_wg_{i}", L["z"], dyf).reshape(N_DEV, nf // 2, D)
            duf, ddw, ddwb, dx, g_ffn_pre[i] = ffn_bwd(f"ffn_bwd_{i}", dyf, W["fout"], L["uf"], L["ag"], DWf, W["fin"],
                                                       L["x1"], ffn_pre_g[i], dx, i, seq)
            g_ffn_dw[i], g_ffn_dwb[i] = ddw.reshape(N_DEV, FFN_CONV_W, nf), ddwb.reshape(-1)
            send["fin"] = mm_tn(f"ffn_in_wg_{i}", duf, L["hf"][None])
        else:
            dy, g_mix_post[i], dyb = resid_norm_bwd(f"mix_post_bwd_{i}", dx, L["y"], gain)
            if i >= N_A:
                dm = mm_nt(f"attn_out_bwd_{i}", dy, W["o"], 0, F32)
                send["o"] = mm_tn(f"attn_out_wg_{i}", L["mg"], dy).reshape(N_DEV, D // N_DEV, D)
                dq, dkv = attn_bwd(f"attn_bwd_{i}", L["q"], kv, dm, L["mg"], L["lse"], dkv, nb, seq)
                send["q"] = mm_tn(f"attn_q_wg_{i}", L["h"][None], dq)
                dx, g_mix_pre[i] = mm_nt_norm_bwd(f"attn_q_bwd_{i}", dq, W["q"], 0, L["x_in"], mix_pre_g[i], dx)
            else:
                g_cbout[i] = dyb
                ds = mm_nt(f"cm_out_bwd_{i}", dy, W["cout"], 0, F32)
                send["cout"] = mm_tn(f"cm_out_wg_{i}", L["s"], dy).reshape(N_DEV, D // N_DEV, D)
                dc, g_lng[i], g_lnb[i] = ln_silu_bwd(f"cm_ln_bwd_{i}", ds, L["c"], LNg[i], LNb[i])
                du, g_cdw[i], g_cdwb[i], dbi = cm_glu_conv_bwd(f"cm_conv_bwd_{i}", L["u"], dc, DWc, i, seq)
                g_cbin[i] = dbi.reshape(N_DEV, -1)
                send["cin"] = mm_tn(f"cm_in_wg_{i}", L["h"][None], du)
                dx, g_mix_pre[i] = mm_nt_norm_bwd(f"cm_in_bwd_{i}", du, W["cin"], 0, L["x_in"], mix_pre_g[i], dx)
        if in_flight is not None:
            (p, j), names, handle = in_flight
            landed[j].update(zip(names, exchange_end(f"scatter_end_{p}{j}", handle, dx)))
        names, arrays = list(send), list(send.values())
        lands = place_own(f"scatter_own_{part}{i}", "scatter", arrays)
        handle, token = exchange_begin(f"scatter_begin_{part}{i}", "scatter", arrays, lands, dx)
        in_flight = (stage, names, handle)
    grad_x = dx.reshape(nb, seq, D)

    rep_names = ["mix_pre_g", "mix_post_g", "ffn_pre_g", "ffn_post_g", "kv_norm_g", "ffn_dw_b"]
    rep_parts = [jnp.concatenate(g_mix_pre), jnp.concatenate(g_mix_post), jnp.concatenate(g_ffn_pre),
                 jnp.concatenate(g_ffn_post), g_kvn.reshape(-1), jnp.stack(g_ffn_dwb)]
    rep_w = [mix_pre_g, mix_post_g, ffn_pre_g, ffn_post_g, kv_norm_g, ffn_dw_b]
    rep_m = [m_mix_pre_g, m_mix_post_g, m_ffn_pre_g, m_ffn_post_g, m_kv_norm_g, m_ffn_dw_b]
    rep_v = [v_mix_pre_g, v_mix_post_g, v_ffn_pre_g, v_ffn_post_g, v_kv_norm_g, v_ffn_dw_b]
    sh_names = ["ffn_dw", "cm_b_in", "cm_dw", "cm_dw_b", "cm_ln_g", "cm_ln_b", "cm_b_out"]
    own = lambda per_layer, shard: jnp.stack([p.reshape((N_DEV,) + shard) for p in per_layer], axis=1)
    sh_parts = [own(g_ffn_dw, ffn_dw.shape[1:]), own(g_cbin, cm_b_in.shape[1:]), own(g_cdw, cm_dw.shape[1:]),
                own(g_cdwb, cm_dw_b.shape[1:]), own(g_lng, cm_ln_g.shape[1:]), own(g_lnb, cm_ln_b.shape[1:]),
                own(g_cbout, cm_b_out.shape[1:])]
    sh_w = [ffn_dw, cm_b_in, cm_dw, cm_dw_b, cm_ln_g, cm_ln_b, cm_b_out]
    sh_m = [m_ffn_dw, m_cm_b_in, m_cm_dw, m_cm_dw_b, m_cm_ln_g, m_cm_ln_b, m_cm_b_out]
    sh_v = [v_ffn_dw, v_cm_b_in, v_cm_dw, v_cm_dw_b, v_cm_ln_g, v_cm_ln_b, v_cm_b_out]
    rep_pack = _pack([loss_part] + rep_parts)
    sh_pack = jnp.stack([_pack([p[k] for p in sh_parts]) for k in range(N_DEV)])
    n_rep = rep_pack.shape[0]
    small = []
    for kind, pack in (("gather", rep_pack), ("scatter", sh_pack)):
        lands = place_own(f"{kind}_own_small", kind, [pack])
        handle, token = exchange_begin(f"{kind}_begin_small", kind, [pack], lands, token)
        small.append((kind, handle))

    def big_update(name, w, m, v, key, layers, after):
        as3 = lambda t: t.reshape((-1,) + t.shape[-2:])
        outs = adamw_sum(name, as3(w), as3(m), as3(v), [landed[i][key] for i in layers], after)
        return [t.reshape(w.shape) for t in outs]

    conf, attn = range(N_A), range(N_A, DEPTH)
    upd = {}
    upd["ffn_w_in"] = [t_(t) for t in big_update("adam_ffn_w_in", fin_t, m_fin_t, v_fin_t, "fin", range(DEPTH), token)]
    upd["ffn_w_out"] = big_update("adam_ffn_w_out", ffn_w_out, m_ffn_w_out, v_ffn_w_out, "fout", range(DEPTH), token)
    upd["w_kv"] = big_update("adam_w_kv", w_kv, m_w_kv, v_w_kv, "kv", [N_A - 1], token)
    upd["w_q"] = big_update("adam_w_q", w_q, m_w_q, v_w_q, "q", attn, token)
    upd["w_o"] = big_update("adam_w_o", w_o, m_w_o, v_w_o, "o", attn, upd["w_q"][0])
    (p, j), names, handle = in_flight
    landed[j].update(zip(names, exchange_end(f"scatter_end_{p}{j}", handle, upd["w_o"][0])))
    upd["cm_w_in"] = big_update("adam_cm_w_in", cm_w_in, m_cm_w_in, v_cm_w_in, "cin", conf, token)
    upd["cm_w_out"] = big_update("adam_cm_w_out", cm_w_out, m_cm_w_out, v_cm_w_out, "cout", conf, upd["cm_w_in"][0])
    (rep_landed,), (sh_landed,) = (exchange_end(f"{kind}_end_small", handle, upd["cm_w_out"][0])
                                   for kind, handle in small)
    rep_sum = sum_partials("sum_small_rep", rep_landed)
    sh_sum = sum_partials("sum_small_sh", sh_landed)
    rep_shapes = [(1, 1)] + [w.shape for w in rep_w]
    sh_shapes = [w.shape for w in sh_w]
    g_small = jnp.concatenate([rep_sum, sh_sum])
    pad1 = jnp.zeros((1, 1), F32)
    d_s, m_s, v_s = adamw_small("adam_small", jnp.concatenate([_pack([pad1] + rep_w), _pack(sh_w)]), g_small,
                                jnp.concatenate([_pack([pad1] + rep_m), _pack(sh_m)]),
                                jnp.concatenate([_pack([pad1] + rep_v), _pack(sh_v)]))
    split = lambda t: (_unpack(t[:n_rep], rep_shapes), _unpack(t[n_rep:], sh_shapes))
    for (rep_t, sh_t), slot in zip([split(g_small), split(d_s), split(m_s), split(v_s)], range(4)):
        if slot == 0:
            loss = rep_t[0].reshape(())
        for name, t in zip(rep_names, rep_t[1:]):
            upd.setdefault(name, [None] * 4)[slot] = t
        for name, t in zip(sh_names, sh_t):
            upd.setdefault(name, [None] * 4)[slot] = t

    order = ["mix_pre_g", "mix_post_g", "ffn_pre_g", "ffn_post_g", "cm_w_in", "cm_b_in", "cm_dw", "cm_dw_b", "cm_ln_g",
             "cm_ln_b", "cm_w_out", "cm_b_out", "kv_norm_g", "w_kv", "w_q", "w_o", "ffn_w_in", "ffn_dw", "ffn_dw_b",
             "ffn_w_out"]
    return (loss, grad_x, *[upd[n][0] for n in order], *[upd[n][1] for n in order],
            *[upd[n][2] for n in order], *[upd[n][3] for n in order])
```

```python
import functools

import jax
import jax.numpy as jnp
from jax import lax
from jax.experimental import pallas as pl
from jax.experimental.pallas import tpu as pltpu

N_DEV = 8
N_A = 2
DEPTH = 4
N_HEADS = 8
N_GROUPS = 3
DILATIONS = (1, 4, 16)
ATT_BLOCK = 128
ATT_BATCH = 8
CONV_W = 31
FFN_CONV_W = 3
CONV_HALO = 32
FFN_HALO = 16
ROW_CHUNK_FWD = 16
ROW_CHUNK_BWD = 8
EPS = 1e-6
NEG = -1e30
ADAM_LR, ADAM_B1, ADAM_B2, ADAM_EPS, ADAM_WD, ADAM_STEP = 0.001, 0.9, 0.999, 1e-08, 0.01, 10
MM = jnp.bfloat16
F32 = jnp.float32
VMEM_LIMIT_BYTES = 56 * 1024 * 1024
PACK = 1024
MESH_ID = pl.DeviceIdType.MESH

_pallas = pl.pallas_call


def _call(body, *, name, out_shape, grid=(), in_specs=None, out_specs=None, scratch=()):
    return _pallas(body, name=name, out_shape=out_shape, grid=grid, in_specs=in_specs, out_specs=out_specs,
                   scratch_shapes=list(scratch),
                   compiler_params=pltpu.CompilerParams(vmem_limit_bytes=VMEM_LIMIT_BYTES))


def _tile(n, pref):
    if n <= pref:
        return n
    t = pref - pref % 8
    while n % t:
        t -= 8
    assert t > 0, (n, pref)
    return t


def _sds(shape, dtype):
    return jax.ShapeDtypeStruct(tuple(shape), dtype)


def _dot(a, b):
    return jnp.dot(a, b, preferred_element_type=F32)


def _dot_nt(a, b):
    return lax.dot_general(a, b, (((1,), (1,)), ((), ())), preferred_element_type=F32)


def _dot_tn(a, b):
    return lax.dot_general(a, b, (((0,), (0,)), ((), ())), preferred_element_type=F32)


def _sigmoid(x):
    return 1.0 / (1.0 + jnp.exp(-x))


def _my_index():
    return 4 * lax.axis_index("x") + 2 * lax.axis_index("y") + lax.axis_index("c")


def _exchange(name, arrays, out_shapes, pieces, src_of, dst_of):
    n = len(arrays)
    base = [sum(pieces[:a]) for a in range(n)]
    total = sum(pieces)

    def body(*refs):
        ins, outs = refs[:n], refs[n:2 * n]
        send_sems, recv_sems, local_sems = refs[2 * n:]
        x, y, c = lax.axis_index("x"), lax.axis_index("y"), lax.axis_index("c")
        me = 4 * x + 2 * y + c
        copies = []
        for a in range(n):
            for k, (s, d) in enumerate(zip(src_of(a, ins[a], me), dst_of(a, outs[a], me))):
                cp = pltpu.make_async_copy(s, d, local_sems.at[base[a] + k])
                cp.start()
                copies.append(cp)
        remote = []
        for m in range(1, N_DEV):
            px, py, pc = x ^ (m >> 2), y ^ ((m >> 1) & 1), c ^ (m & 1)
            peer = 4 * px + 2 * py + pc
            for a in range(n):
                for k, (s, d) in enumerate(zip(src_of(a, ins[a], peer), dst_of(a, outs[a], me))):
                    cp = pltpu.make_async_remote_copy(src_ref=s, dst_ref=d, send_sem=send_sems.at[base[a] + k, m - 1],
                                                      recv_sem=recv_sems.at[base[a] + k, m - 1],
                                                      device_id=(px, py, pc), device_id_type=MESH_ID)
                    cp.start()
                    remote.append(cp)
        for cp in copies:
            cp.wait()
        for cp in remote:
            cp.wait_send()
        for m in range(1, N_DEV):
            px, py, pc = x ^ (m >> 2), y ^ ((m >> 1) & 1), c ^ (m & 1)
            peer = 4 * px + 2 * py + pc
            for a in range(n):
                for k, (s, d) in enumerate(zip(src_of(a, ins[a], me), dst_of(a, outs[a], peer))):
                    pltpu.make_async_remote_copy(src_ref=s, dst_ref=d, send_sem=send_sems.at[base[a] + k, m - 1],
                                                 recv_sem=recv_sems.at[base[a] + k, m - 1], device_id=(px, py, pc),
                                                 device_id_type=MESH_ID).wait_recv()

    any_spec = pl.BlockSpec(memory_space=pl.ANY)
    return _call(body, name=name, out_shape=[_sds(s, a.dtype) for s, a in zip(out_shapes, arrays)],
                 in_specs=[any_spec] * n, out_specs=[any_spec] * n,
                 scratch=[pltpu.SemaphoreType.DMA((total, N_DEV - 1)), pltpu.SemaphoreType.DMA((total, N_DEV - 1)),
                          pltpu.SemaphoreType.DMA((total,))])(*arrays)


def all_gather(name, arrays, row_sharded):
    def out_shape(a):
        s = arrays[a].shape
        return (s[0], N_DEV) + s[1:] if row_sharded[a] else (N_DEV,) + s

    def src_of(a, ref, peer):
        if row_sharded[a]:
            return [ref.at[l] for l in range(arrays[a].shape[0])]
        return [ref]

    def dst_of(a, ref, me):
        if row_sharded[a]:
            return [ref.at[l, me] for l in range(arrays[a].shape[0])]
        return [ref.at[me]]

    pieces = [arrays[a].shape[0] if row_sharded[a] else 1 for a in range(len(arrays))]
    return _exchange(name, arrays, [out_shape(a) for a in range(len(arrays))], pieces, src_of, dst_of)


def _src_view(kind, ref, peer):
    return ref if kind == "gather" else ref.at[peer]


def _peers(x, y, c):
    for m in range(1, N_DEV):
        px, py, pc = x ^ (m >> 2), y ^ ((m >> 1) & 1), c ^ (m & 1)
        yield m - 1, (px, py, pc), 4 * px + 2 * py + pc


def place_own(name, kind, srcs):
    n = len(srcs)
    shapes = [(N_DEV,) + s.shape if kind == "gather" else s.shape for s in srcs]
    steps = 2 if all(s.shape[-2] % 32 == 0 for s in srcs) else 1

    def body(*refs):
        for a in range(n):
            refs[n + a][...] = refs[a][...]

    def spec(shape, own_block):
        R, C = shape[-2:]
        tr = R // steps
        if own_block:
            return pl.BlockSpec((None, tr, C), lambda i: (_my_index(), i, 0))
        return pl.BlockSpec((tr, C), lambda i: (i, 0))

    return _call(body, name=name, grid=(steps,), out_shape=[_sds(s, a.dtype) for s, a in zip(shapes, srcs)],
                 in_specs=[spec(s.shape, kind == "scatter") for s in srcs],
                 out_specs=[spec(s, True) for s in shapes])(*srcs)


_HBM_SPEC = pl.BlockSpec(memory_space=pltpu.HBM)
_SEM_SPEC = pl.BlockSpec(memory_space=pltpu.SEMAPHORE)
_DATAFLOW = pltpu.SideEffectType.DATAFLOW_SIDE_EFFECTING


def _remote(kind, src, land, send_sems, recv_sems, a, slot, frm, to_id, at):
    return pltpu.make_async_remote_copy(src_ref=_src_view(kind, src, frm), dst_ref=land.at[at],
                                        send_sem=send_sems.at[a * (N_DEV - 1) + slot],
                                        recv_sem=recv_sems.at[a * (N_DEV - 1) + slot],
                                        device_id=to_id, device_id_type=MESH_ID)


def exchange_begin(name, kind, srcs, lands, after):
    n = len(srcs)

    def body(*refs):
        ins, lnd = refs[:n], refs[n:2 * n]
        send_sems, recv_sems = refs[2 * n + 1], refs[2 * n + 2]
        token = refs[-1]
        x, y, c = lax.axis_index("x"), lax.axis_index("y"), lax.axis_index("c")
        me = 4 * x + 2 * y + c
        for slot, peer_id, peer in _peers(x, y, c):
            for a in range(n):
                _remote(kind, ins[a], lnd[a], send_sems, recv_sems, a, slot, peer, peer_id, me).start()
        token[...] = jnp.zeros_like(token)

    hbm = lambda t: pltpu.HBM(t.shape, t.dtype)
    outs = _pallas(
        body, name=name,
        out_shape=(pltpu.SemaphoreType.DMA((n * (N_DEV - 1),)), pltpu.SemaphoreType.DMA((n * (N_DEV - 1),)),
                   *[hbm(t) for t in srcs], *[hbm(t) for t in lands], _sds((8, 128), F32)),
        in_specs=[_HBM_SPEC] * (2 * n) + [pl.BlockSpec(memory_space=pl.ANY)],
        out_specs=(_SEM_SPEC, _SEM_SPEC, *[_HBM_SPEC] * (2 * n), pl.BlockSpec(memory_space=pltpu.VMEM)),
        input_output_aliases={i: 2 + i for i in range(2 * n)},
        compiler_params=pltpu.CompilerParams(has_side_effects=_DATAFLOW),
    )(*[pltpu.with_memory_space_constraint(t, pltpu.HBM) for t in list(srcs) + list(lands)], after)
    return (kind, outs[0], outs[1], list(outs[2:2 + n]), list(outs[2 + n:2 + 2 * n])), outs[-1]


def exchange_end(name, handle, after):
    kind, send_sems, recv_sems, srcs, lands = handle
    n = len(srcs)

    def body(*refs):
        ins, lnd = refs[:n], refs[n:2 * n]
        s_sems, r_sems = refs[2 * n], refs[2 * n + 1]
        x, y, c = lax.axis_index("x"), lax.axis_index("y"), lax.axis_index("c")
        me = 4 * x + 2 * y + c
        for slot, peer_id, peer in _peers(x, y, c):
            for a in range(n):
                _remote(kind, ins[a], lnd[a], s_sems, r_sems, a, slot, peer, peer_id, me).wait_send()
        for slot, peer_id, peer in _peers(x, y, c):
            for a in range(n):
                _remote(kind, ins[a], lnd[a], s_sems, r_sems, a, slot, me, peer_id, peer).wait_recv()

    hbm = lambda t: pltpu.HBM(t.shape, t.dtype)
    outs = _pallas(
        body, name=name, out_shape=tuple(hbm(t) for t in srcs + lands),
        in_specs=[_HBM_SPEC] * (2 * n) + [_SEM_SPEC, _SEM_SPEC, pl.BlockSpec(memory_space=pl.ANY)],
        out_specs=tuple([_HBM_SPEC] * (2 * n)), input_output_aliases={i: i for i in range(2 * n)},
        compiler_params=pltpu.CompilerParams(has_side_effects=_DATAFLOW),
    )(*srcs, *lands, send_sems, recv_sems, after)
    return list(outs[n:])


def norm_mm(name, x, gain, w, layer, bias=None, w_t=False):
    T, D = x.shape
    nsh = w.shape[0]
    n = w.shape[2] if w_t else w.shape[3]
    tm = _tile(T, 2048)

    def body(*refs):
        if bias is None:
            x_ref, g_ref, w_ref, u_ref, h_ref = refs
        else:
            x_ref, g_ref, w_ref, b_ref, u_ref, h_ref = refs

        @pl.when(pl.program_id(1) == 0)
        def _():
            xf = x_ref[...]
            r = lax.rsqrt(jnp.mean(xf * xf, axis=-1, keepdims=True) + EPS)
            h_ref[...] = (xf * r * g_ref[...]).astype(h_ref.dtype)

        acc = (_dot_nt if w_t else _dot)(h_ref[...], w_ref[...])
        if bias is not None:
            acc = acc + b_ref[...]
        u_ref[...] = acc.astype(u_ref.dtype)

    in_specs = [pl.BlockSpec((tm, D), lambda i, j: (i, 0)),
                pl.BlockSpec((1, D), lambda i, j: (0, 0)),
                pl.BlockSpec((None, None) + w.shape[2:], lambda i, j: (j, layer, 0, 0))]
    args = [x, gain.reshape(1, D), w]
    if bias is not None:
        in_specs.append(pl.BlockSpec((None, None, 1, n), lambda i, j: (j, layer, 0, 0)))
        args.append(bias)
    return _call(body, name=name, grid=(T // tm, nsh), in_specs=in_specs,
                 out_specs=[pl.BlockSpec((None, tm, n), lambda i, j: (j, i, 0)),
                            pl.BlockSpec((tm, D), lambda i, j: (i, 0))],
                 out_shape=[_sds((nsh, T, n), MM), _sds((T, D), MM)])(*args)


def mm_resid_norm(name, a, w, layer, bias, x, gain):
    nk, T, kk = a.shape
    D = x.shape[1]
    tm = _tile(T, 512)

    def body(a_ref, w_ref, b_ref, x_ref, g_ref, y_ref, xn_ref):
        y = _dot(a_ref[0], w_ref[0])
        for q in range(1, nk):
            y = y + _dot(a_ref[q], w_ref[q])
        y = y + b_ref[...]
        y_ref[...] = y
        r = lax.rsqrt(jnp.mean(y * y, axis=-1, keepdims=True) + EPS)
        xn_ref[...] = x_ref[...] + y * r * g_ref[...]

    return _call(body, name=name, grid=(T // tm,),
                 in_specs=[pl.BlockSpec((nk, tm, kk), lambda i: (0, i, 0)),
                           pl.BlockSpec((None, nk, kk, D), lambda i: (layer, 0, 0, 0)),
                           pl.BlockSpec((1, D), lambda i: (0, 0)),
                           pl.BlockSpec((tm, D), lambda i: (i, 0)),
                           pl.BlockSpec((1, D), lambda i: (0, 0))],
                 out_specs=[pl.BlockSpec((tm, D), lambda i: (i, 0))] * 2,
                 out_shape=[_sds((T, D), F32)] * 2)(a, w, bias.reshape(1, D), x, gain.reshape(1, D))


def _halo_maps(tm, hb, T):
    per = tm // hb
    last = T // hb - 1
    return (lambda i: jnp.maximum(i * per - 1, 0)), (lambda i: jnp.minimum((i + 1) * per, last))


def cm_glu_conv(name, u, dw, dwb, layer, seq):
    _, T, n = u.shape
    ct = dw.shape[-1]
    per = n // ct
    nct = 4 * per
    tm = _tile(seq, 512)
    tps = seq // tm
    hb = CONV_HALO
    prev, _ = _halo_maps(tm, hb, T)
    u4 = u.reshape(2, 4, T, n)

    def body(u_ref, uh_ref, w_ref, b_ref, o_ref, pad_ref):
        first = (pl.program_id(0) % tps) == 0
        um = u_ref[...].astype(F32)
        uh = uh_ref[...].astype(F32)
        pad_ref[pl.ds(hb, tm), :] = um[0] * _sigmoid(um[1])
        pad_ref[pl.ds(0, hb), :] = jnp.where(first, 0.0, uh[0] * _sigmoid(uh[1]))
        acc = jnp.zeros((tm, ct), F32) + b_ref[...]
        for k in range(CONV_W):
            acc = acc + w_ref[pl.ds(k, 1), :] * pad_ref[pl.ds(hb - (CONV_W - 1) + k, tm), :]
        o_ref[...] = acc

    return _call(body, name=name, grid=(T // tm, nct),
                 in_specs=[pl.BlockSpec((2, None, tm, ct), lambda i, c: (0, c // per, i, c % per)),
                           pl.BlockSpec((2, None, hb, ct), lambda i, c: (0, c // per, prev(i), c % per)),
                           pl.BlockSpec((None, None, CONV_W, ct), lambda i, c: (c, layer, 0, 0)),
                           pl.BlockSpec((None, None, 1, ct), lambda i, c: (c, layer, 0, 0))],
                 out_specs=pl.BlockSpec((tm, ct), lambda i, c: (i, c)),
                 out_shape=_sds((T, nct * ct), F32),
                 scratch=[pltpu.VMEM((tm + hb, ct), F32)])(u4, u4, dw, dwb)


def ln_silu(name, c, g, b):
    T, D = c.shape
    tm = _tile(T, 512)

    def body(c_ref, g_ref, b_ref, s_ref):
        cf = c_ref[...]
        mu = jnp.mean(cf, axis=-1, keepdims=True)
        xc = cf - mu
        r = lax.rsqrt(jnp.mean(xc * xc, axis=-1, keepdims=True) + EPS)
        t = xc * r * g_ref[...] + b_ref[...]
        s_ref[...] = (t * _sigmoid(t)).astype(s_ref.dtype)

    return _call(body, name=name, grid=(T // tm,),
                 in_specs=[pl.BlockSpec((tm, D), lambda i: (i, 0)), pl.BlockSpec((1, D), lambda i: (0, 0)),
                           pl.BlockSpec((1, D), lambda i: (0, 0))],
                 out_specs=pl.BlockSpec((None, tm, D), lambda i: (0, i, 0)),
                 out_shape=_sds((1, T, D), MM))(c, g.reshape(1, D), b.reshape(1, D))


def _col_tiles(n, width):
    return [(c0, min(width, n - c0)) for c0 in range(0, n, width)]


def ffn_in_conv(name, x, gain, wt, dw, dwb, layer, seq):
    T, D = x.shape
    n = wt.shape[2]
    tm = _tile(seq, 1024)
    tps = seq // tm
    hb = 8
    rows = _tile(tm, 2 * ROW_CHUNK_FWD)
    K = FFN_CONV_W
    w5 = wt.reshape(2, 4, n, D)
    dw4 = dw.reshape(2, 4, dw.shape[1], K, n)
    dwb4 = dwb.reshape(dwb.shape[0], 2, 4, 1, n)

    def body(x_ref, g_ref, w_ref, cw_ref, cb_ref, u_ref, h_ref, z_ref, ag_ref, pad_ref, carry_ref):
        i, j = pl.program_id(0), pl.program_id(1)
        first = (i % tps) == 0

        @pl.when(j == 0)
        def _():
            xf = x_ref[...]
            r = lax.rsqrt(jnp.mean(xf * xf, axis=-1, keepdims=True) + EPS)
            h_ref[...] = (xf * r * g_ref[...]).astype(h_ref.dtype)

        h = h_ref[...]
        for c0, wc in _col_tiles(n, 256):
            cols = pl.ds(c0, wc)
            for half in range(2):
                res = _dot_nt(h, w_ref[half, cols, :])
                u_ref[half, :, cols] = res.astype(u_ref.dtype)
                pad_ref[half, pl.ds(hb, tm), cols] = res
                pad_ref[half, pl.ds(0, hb), cols] = jnp.where(first, 0.0, carry_ref[j, half, :, cols])
            for r0 in range(0, tm, rows):
                conv = []
                for half in range(2):
                    acc = cb_ref[half, :, cols] + cw_ref[half, pl.ds(K - 1, 1), cols] * pad_ref[
                        half, pl.ds(hb + r0, rows), cols]
                    for k in range(K - 1):
                        acc = acc + cw_ref[half, pl.ds(k, 1), cols] * pad_ref[
                            half, pl.ds(hb + r0 - (K - 1) + k, rows), cols]
                    conv.append(acc)
                a, g = conv
                z_ref[pl.ds(r0, rows), cols] = (g * _sigmoid(g) * a).astype(z_ref.dtype)
                ag_ref[0, pl.ds(r0, rows), cols] = a.astype(ag_ref.dtype)
                ag_ref[1, pl.ds(r0, rows), cols] = g.astype(ag_ref.dtype)
            for half in range(2):
                carry_ref[j, half, :, cols] = pad_ref[half, pl.ds(tm, hb), cols]

    u, h, z, ag = _call(
        body, name=name, grid=(T // tm, 4),
        in_specs=[pl.BlockSpec((tm, D), lambda i, j: (i, 0)),
                  pl.BlockSpec((1, D), lambda i, j: (0, 0)),
                  pl.BlockSpec((2, None, n, D), lambda i, j: (0, j, 0, 0)),
                  pl.BlockSpec((2, None, None, K, n), lambda i, j: (0, j, layer, 0, 0)),
                  pl.BlockSpec((None, 2, None, 1, n), lambda i, j: (layer, 0, j, 0, 0))],
        out_specs=[pl.BlockSpec((2, None, tm, n), lambda i, j: (0, j, i, 0)),
                   pl.BlockSpec((tm, D), lambda i, j: (i, 0)),
                   pl.BlockSpec((None, tm, n), lambda i, j: (j, i, 0)),
                   pl.BlockSpec((2, None, tm, n), lambda i, j: (0, j, i, 0))],
        out_shape=[_sds((2, 4, T, n), MM), _sds((T, D), MM), _sds((4, T, n), MM), _sds((2, 4, T, n), MM)],
        scratch=[pltpu.VMEM((2, tm + hb, n), F32), pltpu.VMEM((4, 2, hb, n), F32)])(
            x, gain.reshape(1, D), w5, dw4, dwb4)
    return u.reshape(8, T, n), h, z, ag


def _head_specs(seq, dh, q_heads, kv_heads):
    def spec(per, base):
        def imap(b, h, g):
            f = base + g * N_HEADS + h
            return (f // per, b, f % per)
        return pl.BlockSpec((None, seq, dh), imap)
    return spec(q_heads, 0), spec(kv_heads, 0), spec(kv_heads, N_GROUPS * N_HEADS)


def _rows(start, d, blocks=1):
    size = blocks * ATT_BLOCK
    return pl.ds(start, size, stride=d) if d > 1 else pl.ds(start, size)


def _att_pad():
    return max(ATT_BLOCK * d for d in DILATIONS[:-1])


def _band_mask(first, nblk):
    keys = ATT_BLOCK if nblk == 1 else 2 * ATT_BLOCK
    shape = (ATT_BATCH, ATT_BLOCK, keys)
    qi = lax.broadcasted_iota(jnp.int32, shape, 1)
    kj = lax.broadcasted_iota(jnp.int32, shape, 2)
    if nblk == 1:
        return kj <= qi
    n = (first + lax.broadcasted_iota(jnp.int32, shape, 0)) % nblk
    return (kj >= qi) & (kj <= qi + ATT_BLOCK) & ((n > 0) | (kj >= ATT_BLOCK))


def _block_rows(idx, d, nblk):
    r, n = idx // nblk, idx % nblk
    rq = _rows(r + d * ATT_BLOCK * n, d)
    if nblk == 1:
        return rq, _rows(_att_pad() + r + d * ATT_BLOCK * n, d)
    return rq, _rows(_att_pad() + r + d * ATT_BLOCK * (n - 1), d, blocks=2)


def _bdot(a, b, ca, cb):
    return lax.dot_general(a, b, (((ca,), (cb,)), ((0,), (0,))), preferred_element_type=F32)


def attn_fwd(name, q, kv, nb, seq):
    _, T, qn = q.shape
    dh = qn * N_DEV // (N_GROUPS * N_HEADS)
    scale = 1.0 / (dh ** 0.5)
    qs, ks, vs = _head_specs(seq, dh, qn // dh, kv.shape[2] // dh)

    def body(q_ref, k_ref, v_ref, m_ref, l_ref, qf, kf, vf, *branch):
        og, lg = branch[:N_GROUPS], branch[N_GROUPS:]
        pad = _att_pad()
        qf[...] = q_ref[...].astype(F32)
        for t_ref, s_ref in ((k_ref, kf), (v_ref, vf)):
            s_ref[pl.ds(0, pad), :] = jnp.zeros((pad, dh), F32)
            s_ref[pl.ds(pad, seq), :] = t_ref[...].astype(F32)
        for g in range(N_GROUPS):
            d = DILATIONS[g]
            nblk = seq // d // ATT_BLOCK

            def blocks(it, carry, d=d, nblk=nblk, g=g):
                first = it * ATT_BATCH
                rows = [_block_rows(first + b, d, nblk) for b in range(ATT_BATCH)]
                qb = jnp.stack([qf[rq, :] for rq, _ in rows]).astype(MM)
                kb = jnp.stack([kf[rk, :] for _, rk in rows]).astype(MM)
                vb = jnp.stack([vf[rk, :] for _, rk in rows]).astype(MM)
                s = jnp.where(_band_mask(first, nblk), _bdot(qb, kb, 2, 2) * scale, NEG)
                m = jnp.max(s, axis=-1, keepdims=True)
                p = jnp.exp(s - m)
                den = jnp.sum(p, axis=-1, keepdims=True)
                o = _bdot(p.astype(MM), vb, 2, 1) / den
                lse = m + jnp.log(den)
                for b, (rq, _) in enumerate(rows):
                    og[g][rq, :] = o[b]
                    lg[g][rq, :] = jnp.broadcast_to(lse[b], (ATT_BLOCK, dh))
                return carry

            @pl.when(pl.program_id(2) == g)
            def _(blocks=blocks, d=d, nblk=nblk):
                lax.fori_loop(0, d * nblk // ATT_BATCH, blocks, 0)

        @pl.when(pl.program_id(2) == N_GROUPS - 1)
        def _():
            mx = jnp.maximum(jnp.maximum(lg[0][...], lg[1][...]), lg[2][...])
            e = [jnp.exp(lg[g][...] - mx) for g in range(N_GROUPS)]
            tot = e[0] + e[1] + e[2]
            m_ref[...] = ((e[0] * og[0][...] + e[1] * og[1][...] + e[2] * og[2][...]) / tot).astype(m_ref.dtype)
            l_ref[...] = mx + jnp.log(tot)

    return _call(body, name=name, grid=(nb, N_HEADS, N_GROUPS), in_specs=[qs, ks, vs],
                 out_specs=[pl.BlockSpec((None, seq, dh), lambda b, h, g: (0, b, h)),
                            pl.BlockSpec((seq, dh), lambda b, h, g: (b, h))],
                 out_shape=[_sds((1, T, N_HEADS * dh), MM), _sds((T, N_HEADS * dh), F32)],
                 scratch=[pltpu.VMEM((seq, dh), F32)] + [pltpu.VMEM((_att_pad() + seq, dh), F32)] * 2
                 + [pltpu.VMEM((seq, dh), F32)] * (2 * N_GROUPS))(q, kv, kv)


def loss_fwd_bwd(name, x, target):
    T, D = x.shape
    tm = _tile(T, 512)

    def body(x_ref, t_ref, dx_ref, l_ref):
        @pl.when(pl.program_id(0) == 0)
        def _():
            l_ref[...] = jnp.zeros_like(l_ref)
        err = x_ref[...] - t_ref[...]
        dx_ref[...] = err * (1.0 / D)
        l_ref[...] += 0.5 * jnp.sum(jnp.mean(err * err, axis=-1, keepdims=True), axis=0, keepdims=True)

    dx, l = _call(body, name=name, grid=(T // tm,),
                  in_specs=[pl.BlockSpec((tm, D), lambda i: (i, 0))] * 2,
                  out_specs=[pl.BlockSpec((tm, D), lambda i: (i, 0)), pl.BlockSpec((1, 1), lambda i: (0, 0))],
                  out_shape=[_sds((T, D), F32), _sds((1, 1), F32)])(x, target)
    return dx, l


def resid_norm_bwd(name, dx, y, gain):
    T, D = y.shape
    tm = _tile(T, 512)

    def body(dx_ref, y_ref, g_ref, dy_ref, dg_ref, db_ref):
        @pl.when(pl.program_id(0) == 0)
        def _():
            dg_ref[...] = jnp.zeros_like(dg_ref)
            db_ref[...] = jnp.zeros_like(db_ref)
        y = y_ref[...]
        d = dx_ref[...]
        r = lax.rsqrt(jnp.mean(y * y, axis=-1, keepdims=True) + EPS)
        yh = y * r
        dyh = d * g_ref[...]
        dy = r * (dyh - yh * jnp.mean(dyh * yh, axis=-1, keepdims=True))
        dy_ref[...] = dy.astype(dy_ref.dtype)
        dg_ref[...] += jnp.sum(d * yh, axis=0, keepdims=True)
        db_ref[...] += jnp.sum(dy, axis=0, keepdims=True)

    return _call(body, name=name, grid=(T // tm,),
                 in_specs=[pl.BlockSpec((tm, D), lambda i: (i, 0))] * 2 + [pl.BlockSpec((1, D), lambda i: (0, 0))],
                 out_specs=[pl.BlockSpec((None, tm, D), lambda i: (0, i, 0))] + [pl.BlockSpec((1, D), lambda i: (0, 0))] * 2,
                 out_shape=[_sds((1, T, D), MM), _sds((1, D), F32), _sds((1, D), F32)])(dx, y, gain.reshape(1, D))


def mm_nt(name, dy, w, layer, out_dtype):
    _, T, D = dy.shape
    _, nk, kk, _ = w.shape
    tm = _tile(T, 1024)

    def body(dy_ref, w_ref, o_ref):
        o_ref[...] = _dot_nt(dy_ref[...], w_ref[...]).astype(o_ref.dtype)

    return _call(body, name=name, grid=(T // tm, nk),
                 in_specs=[pl.BlockSpec((None, tm, D), lambda i, q: (0, i, 0)),
                           pl.BlockSpec((None, None, kk, D), lambda i, q: (layer, q, 0, 0))],
                 out_specs=pl.BlockSpec((None, tm, kk), lambda i, q: (q, i, 0)),
                 out_shape=_sds((nk, T, kk), out_dtype))(dy, w)


def mm_nt_norm_bwd(name, du, w, layer, x_in, gain, dx_res, w_t=False):
    nsh, T, n = du.shape
    D = x_in.shape[1]
    tm = _tile(T, 1024)

    def body(du_ref, w_ref, x_ref, g_ref, dr_ref, dx_ref, dg_ref, acc_ref):
        i, j = pl.program_id(0), pl.program_id(1)

        @pl.when((i == 0) & (j == 0))
        def _():
            dg_ref[...] = jnp.zeros_like(dg_ref)

        part = (_dot if w_t else _dot_nt)(du_ref[...], w_ref[...])

        @pl.when(j == 0)
        def _():
            acc_ref[...] = part

        @pl.when(j > 0)
        def _():
            acc_ref[...] += part

        @pl.when(j == nsh - 1)
        def _():
            x = x_ref[...]
            dh = acc_ref[...]
            r = lax.rsqrt(jnp.mean(x * x, axis=-1, keepdims=True) + EPS)
            xh = x * r
            dxh = dh * g_ref[...]
            dx_ref[...] = dr_ref[...] + r * (dxh - xh * jnp.mean(dxh * xh, axis=-1, keepdims=True))
            dg_ref[...] += jnp.sum(dh * xh, axis=0, keepdims=True)

    return _call(body, name=name, grid=(T // tm, nsh),
                 in_specs=[pl.BlockSpec((None, tm, n), lambda i, j: (j, i, 0)),
                           pl.BlockSpec((None, None) + w.shape[2:], lambda i, j: (j, layer, 0, 0)),
                           pl.BlockSpec((tm, D), lambda i, j: (i, 0)),
                           pl.BlockSpec((1, D), lambda i, j: (0, 0)),
                           pl.BlockSpec((tm, D), lambda i, j: (i, 0))],
                 out_specs=[pl.BlockSpec((tm, D), lambda i, j: (i, 0)), pl.BlockSpec((1, D), lambda i, j: (0, 0))],
                 out_shape=[_sds((T, D), F32), _sds((1, D), F32)],
                 scratch=[pltpu.VMEM((tm, D), F32)])(du, w, x_in, gain.reshape(1, D), dx_res)


def mm_tn(name, a, b):
    na, T, ka = a.shape
    nb, _, kb = b.shape
    nj = max(na, nb)

    def body(a_ref, b_ref, o_ref):
        o_ref[...] = _dot_tn(a_ref[...], b_ref[...]).astype(o_ref.dtype)

    return _call(body, name=name, grid=(nj,),
                 in_specs=[pl.BlockSpec((None, T, ka), (lambda j: (j, 0, 0)) if na > 1 else (lambda j: (0, 0, 0))),
                           pl.BlockSpec((None, T, kb), (lambda j: (j, 0, 0)) if nb > 1 else (lambda j: (0, 0, 0)))],
                 out_specs=pl.BlockSpec((None, ka, kb), lambda j: (j, 0, 0)),
                 out_shape=_sds((nj, ka, kb), MM))(a, b)


def ffn_conv_gate_bwd(name, u, ag, dz, dw, layer, seq):
    _, T, n = u.shape
    tm = _tile(seq, 512)
    tps = seq // tm
    hb = FFN_HALO
    _, nxt = _halo_maps(tm, hb, T)
    u4 = u.reshape(2, 4, T, n)
    dw4 = dw.reshape(2, 4, dw.shape[1], FFN_CONV_W, n)
    K = FFN_CONV_W
    te = tm + hb
    rows = _tile(hb, ROW_CHUNK_BWD)

    def body(u_ref, ag_ref, agn_ref, dz_ref, dzn_ref, w_ref, du_ref, ddw_ref, ddb_ref, uf_ref, agf_ref, dzf_ref,
             da_ref):
        i = pl.program_id(1)
        last = (i % tps) == tps - 1

        @pl.when(i == 0)
        def _():
            ddw_ref[...] = jnp.zeros_like(ddw_ref)
            ddb_ref[...] = jnp.zeros_like(ddb_ref)

        for half in range(2):
            uf_ref[half] = u_ref[half].astype(F32)
            agf_ref[half, pl.ds(0, tm), :] = ag_ref[half].astype(F32)
            agf_ref[half, pl.ds(tm, hb), :] = agn_ref[half].astype(F32)
        dzf_ref[pl.ds(0, tm), :] = dz_ref[...].astype(F32)
        dzf_ref[pl.ds(tm, hb), :] = jnp.where(last, 0.0, dzn_ref[...].astype(F32))
        for r0 in range(0, te, rows):
            a, g = agf_ref[0, pl.ds(r0, rows), :], agf_ref[1, pl.ds(r0, rows), :]
            dzc = dzf_ref[pl.ds(r0, rows), :]
            sg = _sigmoid(g)
            da_ref[0, pl.ds(r0, rows), :] = dzc * (g * sg)
            da_ref[1, pl.ds(r0, rows), :] = dzc * a * (sg * (1.0 + g * (1.0 - sg)))
        for half in range(2):
            tap_acc = [jnp.zeros((rows, n), F32) for _ in range(K)]
            bias_acc = jnp.zeros((rows, n), F32)
            for r0 in range(0, tm, rows):
                x = uf_ref[half, pl.ds(r0, rows), :]
                acc = None
                for k in range(K):
                    d = da_ref[half, pl.ds(r0 + K - 1 - k, rows), :]
                    acc = w_ref[half, pl.ds(k, 1), :] * d if acc is None else acc + w_ref[half, pl.ds(k, 1), :] * d
                    tap_acc[k] = tap_acc[k] + d * x
                    if k == K - 1:
                        bias_acc = bias_acc + d
                du_ref[half, pl.ds(r0, rows), :] = acc.astype(du_ref.dtype)
            for k in range(K):
                ddw_ref[half, pl.ds(k, 1), :] += jnp.sum(tap_acc[k], axis=0, keepdims=True)
            ddb_ref[half] += jnp.sum(bias_acc, axis=0, keepdims=True)

    du, ddw, ddb = _call(
        body, name=name, grid=(4, T // tm),
        in_specs=[pl.BlockSpec((2, None, tm, n), lambda j, i: (0, j, i, 0)),
                  pl.BlockSpec((2, None, tm, n), lambda j, i: (0, j, i, 0)),
                  pl.BlockSpec((2, None, hb, n), lambda j, i: (0, j, nxt(i), 0)),
                  pl.BlockSpec((None, tm, n), lambda j, i: (j, i, 0)),
                  pl.BlockSpec((None, hb, n), lambda j, i: (j, nxt(i), 0)),
                  pl.BlockSpec((2, None, None, K, n), lambda j, i: (0, j, layer, 0, 0))],
        out_specs=[pl.BlockSpec((2, None, tm, n), lambda j, i: (0, j, i, 0)),
                   pl.BlockSpec((2, None, K, n), lambda j, i: (0, j, 0, 0)),
                   pl.BlockSpec((2, None, 1, n), lambda j, i: (0, j, 0, 0))],
        out_shape=[_sds((2, 4, T, n), MM), _sds((2, 4, K, n), F32), _sds((2, 4, 1, n), F32)],
        scratch=[pltpu.VMEM((2, tm, n), F32), pltpu.VMEM((2, te, n), F32), pltpu.VMEM((te, n), F32),
                 pltpu.VMEM((2, te, n), F32)],
    )(u4, ag, ag, dz, dz, dw4)
    return du.reshape(8, T, n), ddw, ddb


def ffn_bwd(name, dy, wout, u, ag, dw, wt, x_in, gain, dx_res, layer, seq):
    _, T, D = dy.shape
    n = wt.shape[2]
    tm = _tile(seq, 512)
    tps = seq // tm
    hb = FFN_HALO
    _, nxt = _halo_maps(tm, hb, T)
    K = FFN_CONV_W
    te = tm + hb
    rows = _tile(hb, ROW_CHUNK_BWD)
    u4, w5 = u.reshape(2, 4, T, n), wt.reshape(2, 4, n, D)
    dw4 = dw.reshape(2, 4, dw.shape[1], K, n)
    tiles = _col_tiles(n, 256)

    def body(dy_ref, dyn_ref, wo_ref, u_ref, ag_ref, agn_ref, cw_ref, wt_ref, x_ref, g_ref, dr_ref,
             du_ref, ddw_ref, ddb_ref, dx_ref, dg_ref,
             uf_ref, agf_ref, dzf_ref, da_ref, acc_ref):
        i, j = pl.program_id(0), pl.program_id(1)
        last = (i % tps) == tps - 1

        @pl.when((i == 0) & (j == 0))
        def _():
            dg_ref[...] = jnp.zeros_like(dg_ref)

        @pl.when(i == 0)
        def _():
            ddw_ref[j] = jnp.zeros((2, K, n), F32)
            ddb_ref[j] = jnp.zeros((2, 1, n), F32)

        dyt, dyn = dy_ref[...], dyn_ref[...]
        for ci, (c0, wc) in enumerate(tiles):
            cols = pl.ds(c0, wc)
            dzf_ref[pl.ds(0, tm), cols] = _dot_nt(dyt, wo_ref[cols, :])
            dzf_ref[pl.ds(tm, hb), cols] = jnp.where(last, 0.0, _dot_nt(dyn, wo_ref[cols, :]))
            for half in range(2):
                uf_ref[half, :, cols] = u_ref[half, :, cols].astype(F32)
                agf_ref[half, pl.ds(0, tm), cols] = ag_ref[half, :, cols].astype(F32)
                agf_ref[half, pl.ds(tm, hb), cols] = agn_ref[half, :, cols].astype(F32)
            for r0 in range(0, te, rows):
                a, g = agf_ref[0, pl.ds(r0, rows), cols], agf_ref[1, pl.ds(r0, rows), cols]
                dzc = dzf_ref[pl.ds(r0, rows), cols]
                sg = _sigmoid(g)
                da_ref[0, pl.ds(r0, rows), cols] = dzc * (g * sg)
                da_ref[1, pl.ds(r0, rows), cols] = dzc * a * (sg * (1.0 + g * (1.0 - sg)))
            for half in range(2):
                tap_acc = [jnp.zeros((rows, wc), F32) for _ in range(K)]
                bias_acc = jnp.zeros((rows, wc), F32)
                for r0 in range(0, tm, rows):
                    xr = uf_ref[half, pl.ds(r0, rows), cols]
                    acc = None
                    for k in range(K):
                        d = da_ref[half, pl.ds(r0 + K - 1 - k, rows), cols]
                        term = cw_ref[half, pl.ds(k, 1), cols] * d
                        acc = term if acc is None else acc + term
                        tap_acc[k] = tap_acc[k] + d * xr
                        if k == K - 1:
                            bias_acc = bias_acc + d
                    du_ref[half, pl.ds(r0, rows), cols] = acc.astype(du_ref.dtype)
                for k in range(K):
                    ddw_ref[j, half, pl.ds(k, 1), cols] += jnp.sum(tap_acc[k], axis=0, keepdims=True)
                ddb_ref[j, half, :, cols] += jnp.sum(bias_acc, axis=0, keepdims=True)
            part = _dot(du_ref[0, :, cols], wt_ref[0, cols, :]) + _dot(du_ref[1, :, cols], wt_ref[1, cols, :])
            if ci == 0:
                acc_ref[...] = part + jnp.where(j == 0, 0.0, acc_ref[...])
            else:
                acc_ref[...] += part
        @pl.when(j == 3)
        def _():
            x = x_ref[...]
            dh = acc_ref[...]
            r = lax.rsqrt(jnp.mean(x * x, axis=-1, keepdims=True) + EPS)
            xh = x * r
            dxh = dh * g_ref[...]
            dx_ref[...] = dr_ref[...] + r * (dxh - xh * jnp.mean(dxh * xh, axis=-1, keepdims=True))
            dg_ref[...] += jnp.sum(dh * xh, axis=0, keepdims=True)

    f32 = lambda *shape: pltpu.VMEM(shape, F32)
    du, ddw, ddb, dx, dg = _call(
        body, name=name, grid=(T // tm, 4),
        in_specs=[pl.BlockSpec((None, tm, D), lambda i, j: (0, i, 0)),
                  pl.BlockSpec((None, hb, D), lambda i, j: (0, nxt(i), 0)),
                  pl.BlockSpec((None, None, n, D), lambda i, j: (0, j, 0, 0)),
                  pl.BlockSpec((2, None, tm, n), lambda i, j: (0, j, i, 0)),
                  pl.BlockSpec((2, None, tm, n), lambda i, j: (0, j, i, 0)),
                  pl.BlockSpec((2, None, hb, n), lambda i, j: (0, j, nxt(i), 0)),
                  pl.BlockSpec((2, None, None, K, n), lambda i, j: (0, j, layer, 0, 0)),
                  pl.BlockSpec((2, None, n, D), lambda i, j: (0, j, 0, 0)),
                  pl.BlockSpec((tm, D), lambda i, j: (i, 0)),
                  pl.BlockSpec((1, D), lambda i, j: (0, 0)),
                  pl.BlockSpec((tm, D), lambda i, j: (i, 0))],
        out_specs=[pl.BlockSpec((2, None, tm, n), lambda i, j: (0, j, i, 0)),
                   pl.BlockSpec((4, 2, K, n), lambda i, j: (0, 0, 0, 0)),
                   pl.BlockSpec((4, 2, 1, n), lambda i, j: (0, 0, 0, 0)),
                   pl.BlockSpec((tm, D), lambda i, j: (i, 0)),
                   pl.BlockSpec((1, D), lambda i, j: (0, 0))],
        out_shape=[_sds((2, 4, T, n), MM), _sds((4, 2, K, n), F32), _sds((4, 2, 1, n), F32), _sds((T, D), F32),
                   _sds((1, D), F32)],
        scratch=[f32(2, tm, n), f32(2, te, n), f32(te, n), f32(2, te, n), f32(tm, D)],
    )(dy, dy, wout, u4, ag, ag, dw4, w5, x_in, gain.reshape(1, D), dx_res)
    return du.reshape(8, T, n), jnp.swapaxes(ddw, 0, 1), jnp.swapaxes(ddb, 0, 1), dx, dg


def ln_silu_bwd(name, ds, c, g, b):
    T, D = c.shape
    tm = _tile(T, 512)

    def body(ds_ref, c_ref, g_ref, b_ref, dc_ref, dg_ref, db_ref):
        @pl.when(pl.program_id(0) == 0)
        def _():
            dg_ref[...] = jnp.zeros_like(dg_ref)
            db_ref[...] = jnp.zeros_like(db_ref)
        cf = c_ref[...]
        mu = jnp.mean(cf, axis=-1, keepdims=True)
        xc = cf - mu
        r = lax.rsqrt(jnp.mean(xc * xc, axis=-1, keepdims=True) + EPS)
        xh = xc * r
        t = xh * g_ref[...] + b_ref[...]
        sg = _sigmoid(t)
        dt = ds_ref[...] * (sg * (1.0 + t * (1.0 - sg)))
        dg_ref[...] += jnp.sum(dt * xh, axis=0, keepdims=True)
        db_ref[...] += jnp.sum(dt, axis=0, keepdims=True)
        dxh = dt * g_ref[...]
        dc_ref[...] = r * (dxh - jnp.mean(dxh, axis=-1, keepdims=True)
                           - xh * jnp.mean(dxh * xh, axis=-1, keepdims=True))

    vec = pl.BlockSpec((1, D), lambda i: (0, 0))
    return _call(body, name=name, grid=(T // tm,),
                 in_specs=[pl.BlockSpec((None, tm, D), lambda i: (0, i, 0)), pl.BlockSpec((tm, D), lambda i: (i, 0)),
                           vec, vec],
                 out_specs=[pl.BlockSpec((tm, D), lambda i: (i, 0)), vec, vec],
                 out_shape=[_sds((T, D), F32), _sds((1, D), F32), _sds((1, D), F32)])(
                     ds, c, g.reshape(1, D), b.reshape(1, D))


def cm_glu_conv_bwd(name, u, dc, dw, layer, seq):
    _, T, n = u.shape
    ct = dw.shape[-1]
    per = n // ct
    nct = 4 * per
    tm = _tile(seq, 512)
    tps = seq // tm
    hb = CONV_HALO
    prev, nxt = _halo_maps(tm, hb, T)
    u4 = u.reshape(2, 4, T, n)
    K = CONV_W

    def body(u_ref, up_ref, dc_ref, dcn_ref, w_ref, du_ref, ddw_ref, ddb_ref, dbi_ref, padp_ref, padd_ref):
        i = pl.program_id(1)
        first = (i % tps) == 0
        last = (i % tps) == tps - 1

        @pl.when(i == 0)
        def _():
            ddw_ref[...] = jnp.zeros_like(ddw_ref)
            ddb_ref[...] = jnp.zeros_like(ddb_ref)
            dbi_ref[...] = jnp.zeros_like(dbi_ref)

        um = u_ref[...].astype(F32)
        uh = up_ref[...].astype(F32)
        sg = _sigmoid(um[1])
        padp_ref[pl.ds(hb, tm), :] = um[0] * sg
        padp_ref[pl.ds(0, hb), :] = jnp.where(first, 0.0, uh[0] * _sigmoid(uh[1]))
        dcm = dc_ref[...]
        padd_ref[pl.ds(0, tm), :] = dcm
        padd_ref[pl.ds(tm, hb), :] = jnp.where(last, 0.0, dcn_ref[...])
        dp = jnp.zeros((tm, ct), F32)
        for k in range(K):
            dp = dp + w_ref[pl.ds(k, 1), :] * padd_ref[pl.ds(K - 1 - k, tm), :]
            ddw_ref[pl.ds(k, 1), :] += jnp.sum(dcm * padp_ref[pl.ds(hb - (K - 1) + k, tm), :], axis=0, keepdims=True)
        ddb_ref[...] += jnp.sum(dcm, axis=0, keepdims=True)
        dv = dp * sg
        dg = dp * um[0] * sg * (1.0 - sg)
        du_ref[0] = dv.astype(du_ref.dtype)
        du_ref[1] = dg.astype(du_ref.dtype)
        dbi_ref[0] += jnp.sum(dv, axis=0, keepdims=True)
        dbi_ref[1] += jnp.sum(dg, axis=0, keepdims=True)

    du, ddw, ddb, dbi = _call(
        body, name=name, grid=(nct, T // tm),
        in_specs=[pl.BlockSpec((2, None, tm, ct), lambda c, i: (0, c // per, i, c % per)),
                  pl.BlockSpec((2, None, hb, ct), lambda c, i: (0, c // per, prev(i), c % per)),
                  pl.BlockSpec((tm, ct), lambda c, i: (i, c)),
                  pl.BlockSpec((hb, ct), lambda c, i: (nxt(i), c)),
                  pl.BlockSpec((None, None, K, ct), lambda c, i: (c, layer, 0, 0))],
        out_specs=[pl.BlockSpec((2, None, tm, ct), lambda c, i: (0, c // per, i, c % per)),
                   pl.BlockSpec((None, K, ct), lambda c, i: (c, 0, 0)),
                   pl.BlockSpec((None, 1, ct), lambda c, i: (c, 0, 0)),
                   pl.BlockSpec((2, None, 1, ct), lambda c, i: (0, c // per, 0, c % per))],
        out_shape=[_sds((2, 4, T, n), MM), _sds((nct, K, ct), F32), _sds((nct, 1, ct), F32), _sds((2, 4, 1, n), F32)],
        scratch=[pltpu.VMEM((tm + hb, ct), F32), pltpu.VMEM((tm + hb, ct), F32)])(u4, u4, dc, dc, dw)
    return du.reshape(8, T, n), ddw, ddb, dbi


def attn_bwd(name, q, kv, dm, merged, lse, dkv_prev, nb, seq):
    _, T, qn = q.shape
    kn = kv.shape[2]
    dh = qn * N_DEV // (N_GROUPS * N_HEADS)
    scale = 1.0 / (dh ** 0.5)
    qs, ks, vs = _head_specs(seq, dh, qn // dh, kn // dh)
    has_prev = dkv_prev is not None
    n_in = 6 + (1 if has_prev else 0)

    def body(*refs):
        q_ref, k_ref, v_ref, dm_ref, mg_ref, l_ref = refs[:6]
        pkv_ref = refs[6] if has_prev else None
        dq_ref, dkv_ref = refs[n_in:n_in + 2]
        qf, kf, vf, dqf, dkf, dvf, dlt = refs[n_in + 2:]
        pad = _att_pad()

        @pl.when(pl.program_id(2) == 0)
        def _():
            dlt[...] = jnp.broadcast_to(
                jnp.sum(dm_ref[...] * mg_ref[...].astype(F32), axis=-1, keepdims=True), (seq, dh))

        qf[...] = q_ref[...].astype(F32)
        for t_ref, s_ref in ((k_ref, kf), (v_ref, vf)):
            s_ref[pl.ds(0, pad), :] = jnp.zeros((pad, dh), F32)
            s_ref[pl.ds(pad, seq), :] = t_ref[...].astype(F32)
        dkf[...] = jnp.zeros_like(dkf)
        dvf[...] = jnp.zeros_like(dvf)
        for g in range(N_GROUPS):
            d = DILATIONS[g]
            nblk = seq // d // ATT_BLOCK

            def blocks(it, carry, d=d, nblk=nblk):
                first = it * ATT_BATCH
                rows = [_block_rows(first + b, d, nblk) for b in range(ATT_BATCH)]
                qb = jnp.stack([qf[rq, :] for rq, _ in rows]).astype(MM)
                dmb = jnp.stack([dm_ref[rq, :] for rq, _ in rows]).astype(MM)
                lse = jnp.stack([l_ref[rq, :][:, :1] for rq, _ in rows])
                delta = jnp.stack([dlt[rq, :][:, :1] for rq, _ in rows])
                kb = jnp.stack([kf[rk, :] for _, rk in rows]).astype(MM)
                vb = jnp.stack([vf[rk, :] for _, rk in rows]).astype(MM)
                s = jnp.where(_band_mask(first, nblk), _bdot(qb, kb, 2, 2) * scale, NEG)
                p = jnp.exp(s - lse)
                dsc = (p * (_bdot(dmb, vb, 2, 2) - delta) * scale).astype(MM)
                dv = _bdot(p.astype(MM), dmb, 1, 1)
                dk = _bdot(dsc, qb, 1, 1)
                dq = _bdot(dsc, kb, 2, 1)
                for b, (rq, rk) in enumerate(rows):
                    dqf[rq, :] = dq[b]
                    dkf[rk, :] += dk[b]
                    dvf[rk, :] += dv[b]
                return carry

            @pl.when(pl.program_id(2) == g)
            def _(blocks=blocks, d=d, nblk=nblk):
                lax.fori_loop(0, d * nblk // ATT_BATCH, blocks, 0)

        dq_ref[...] = dqf[...].astype(dq_ref.dtype)
        dk, dv = dkf[pl.ds(pad, seq), :], dvf[pl.ds(pad, seq), :]
        if has_prev:
            dk, dv = dk + pkv_ref[0].astype(F32), dv + pkv_ref[1].astype(F32)
        dkv_ref[0] = dk.astype(dkv_ref.dtype)
        dkv_ref[1] = dv.astype(dkv_ref.dtype)

    per = kn // dh

    def both(b, h, g):
        f = g * N_HEADS + h
        return (0, f // per, b, f % per)

    kv_spec = pl.BlockSpec((2, None, seq, dh), both)
    full = pl.BlockSpec((None, seq, dh), lambda b, h, g: (0, b, h))
    in_specs = [qs, ks, vs, full, full, pl.BlockSpec((seq, dh), lambda b, h, g: (b, h))]
    args = [q, kv, kv, dm, merged, lse]
    if has_prev:
        in_specs.append(kv_spec)
        args.append(dkv_prev.reshape(2, N_DEV // 2, T, kn))
    short, padded = pltpu.VMEM((seq, dh), F32), pltpu.VMEM((_att_pad() + seq, dh), F32)
    dq, dkv = _call(body, name=name, grid=(nb, N_HEADS, N_GROUPS), in_specs=in_specs, out_specs=[qs, kv_spec],
                    out_shape=[_sds(q.shape, MM), _sds((2, N_DEV // 2, T, kn), MM)],
                    scratch=[short, padded, padded, short, padded, padded, short])(*args)
    return dq, dkv.reshape(N_DEV, T, kn)


def _adamw_math(w, g, m, v):
    m = ADAM_B1 * m + (1.0 - ADAM_B1) * g
    v = ADAM_B2 * v + (1.0 - ADAM_B2) * (g * g)
    m_hat = m / (1.0 - ADAM_B1 ** ADAM_STEP)
    v_hat = v / (1.0 - ADAM_B2 ** ADAM_STEP)
    delta = -ADAM_LR * (m_hat / (jnp.sqrt(v_hat) + ADAM_EPS) + ADAM_WD * w)
    return delta, m, v


def adamw_sum(name, w, m, v, parts, after):
    L, R, C = w.shape
    tr = _tile(R, 256)

    def body(*refs):
        w_ref, m_ref, v_ref = refs[:3]
        p_refs = refs[3:3 + L]
        g_ref, d_ref, nm_ref, nv_ref = refs[4 + L:]
        for l in range(L):
            @pl.when(pl.program_id(0) == l)
            def _(p_ref=p_refs[l]):
                g = p_ref[0].astype(F32)
                for k in range(1, N_DEV):
                    g = g + p_ref[k].astype(F32)
                g_ref[...] = g
                d_ref[...], nm_ref[...], nv_ref[...] = _adamw_math(w_ref[...], g, m_ref[...], v_ref[...])

    blk = pl.BlockSpec((None, tr, C), lambda l, i: (l, i, 0))
    part = lambda k: pl.BlockSpec((N_DEV, tr, C), lambda l, i: (0, jnp.where(l == k, i, 0), 0))
    return _call(body, name=name, grid=(L, R // tr),
                 in_specs=[blk, blk, blk] + [part(k) for k in range(L)] + [pl.BlockSpec(memory_space=pl.ANY)],
                 out_specs=[blk] * 4, out_shape=[_sds((L, R, C), F32)] * 4)(w, m, v, *parts, after)


def sum_partials(name, parts):
    _, R, C = parts.shape
    tr = _tile(R, 512)

    def body(p_ref, o_ref):
        g = p_ref[0]
        for k in range(1, N_DEV):
            g = g + p_ref[k]
        o_ref[...] = g

    return _call(body, name=name, grid=(R // tr,),
                 in_specs=[pl.BlockSpec((N_DEV, tr, C), lambda i: (0, i, 0))],
                 out_specs=pl.BlockSpec((tr, C), lambda i: (i, 0)), out_shape=_sds((R, C), F32))(parts)


def adamw_small(name, w, g, m, v):
    R, C = w.shape
    tr = _tile(R, 512)

    def body(w_ref, g_ref, m_ref, v_ref, d_ref, nm_ref, nv_ref):
        d_ref[...], nm_ref[...], nv_ref[...] = _adamw_math(w_ref[...], g_ref[...], m_ref[...], v_ref[...])

    blk = pl.BlockSpec((tr, C), lambda i: (i, 0))
    return _call(body, name=name, grid=(R // tr,), in_specs=[blk] * 4, out_specs=[blk] * 3,
                 out_shape=[_sds((R, C), F32)] * 3)(w, g, m, v)


def _pack(arrays):
    pieces = []
    for a in arrays:
        f = a.reshape(-1).astype(F32)
        pieces.append(jnp.pad(f, (0, (-f.shape[0]) % PACK)))
    return jnp.concatenate(pieces).reshape(-1, 128)


def _unpack(flat, shapes):
    out, off = [], 0
    f = flat.reshape(-1)
    for s in shapes:
        size = 1
        for d in s:
            size *= d
        out.append(f[off:off + size].reshape(s))
        off += size + (-size) % PACK
    return out


def kernel(x, mix_pre_g, mix_post_g, ffn_pre_g, ffn_post_g, cm_w_in, cm_b_in, cm_dw, cm_dw_b, cm_ln_g, cm_ln_b, cm_w_out, cm_b_out, kv_norm_g, w_kv, w_q, w_o, ffn_w_in, ffn_dw, ffn_dw_b, ffn_w_out, loss_target, m_mix_pre_g, m_mix_post_g, m_ffn_pre_g, m_ffn_post_g, m_cm_w_in, m_cm_b_in, m_cm_dw, m_cm_dw_b, m_cm_ln_g, m_cm_ln_b, m_cm_w_out, m_cm_b_out, m_kv_norm_g, m_w_kv, m_w_q, m_w_o, m_ffn_w_in, m_ffn_dw, m_ffn_dw_b, m_ffn_w_out, v_mix_pre_g, v_mix_post_g, v_ffn_pre_g, v_ffn_post_g, v_cm_w_in, v_cm_b_in, v_cm_dw, v_cm_dw_b, v_cm_ln_g, v_cm_ln_b, v_cm_w_out, v_cm_b_out, v_kv_norm_g, v_w_kv, v_w_q, v_w_o, v_ffn_w_in, v_ffn_dw, v_ffn_dw_b, v_ffn_w_out):
    nb, seq, D = x.shape
    T = nb * seq
    me = _my_index()
    n_b = DEPTH - N_A

    nf = ffn_w_in.shape[-1]
    t_ = lambda t: jnp.swapaxes(t, 1, 2)
    fin_t, m_fin_t, v_fin_t = t_(ffn_w_in), t_(m_ffn_w_in), t_(v_ffn_w_in)

    stages = [(part, i) for i in range(DEPTH) for part in ("mix", "ffn")]
    stages.insert(stages.index(("ffn", N_A - 1)) + 1, ("kv", N_A - 1))

    def stage_sources(stage):
        part, i = stage
        if part == "ffn":
            src = {"fin": fin_t[i], "fout": ffn_w_out[i]}
        elif part == "kv":
            src = {"kv": w_kv}
        elif i < N_A:
            src = {"cin": cm_w_in[i], "cout": cm_w_out[i]}
        else:
            src = {"q": w_q[i - N_A], "o": w_o[i - N_A]}
        return {k: t.astype(MM) for k, t in src.items()}

    def begin_gather(stage, after):
        src = stage_sources(stage)
        names, arrays = list(src), list(src.values())
        tag = f"{stage[0]}{stage[1]}"
        lands = place_own(f"gather_own_{tag}", "gather", arrays)
        handle, token = exchange_begin(f"gather_begin_{tag}", "gather", arrays, lands, after)
        return (names, handle), token

    def end_gather(stage, pending, after):
        names, handle = pending
        W = dict(zip(names, exchange_end(f"gather_end_{stage[0]}{stage[1]}", handle, after)))
        for k in W:
            if k in ("cout", "o"):
                W[k] = W[k].reshape(1, 1, D, D)
            elif k == "fout":
                W[k] = W[k].reshape(1, 4, nf, D)
            else:
                W[k] = W[k][:, None]
        return W

    small = [cm_b_in[:, None, :], cm_dw, cm_dw_b[:, None, :], cm_ln_g, cm_ln_b, cm_b_out, ffn_dw]
    Bcin, DWc, DWBc, LNg, LNb, Bcout, DWf = all_gather("gather_small", small, [False] * len(small))
    LNg = jnp.swapaxes(LNg, 0, 1).reshape(N_A, D)
    LNb = jnp.swapaxes(LNb, 0, 1).reshape(N_A, D)
    Bcout = jnp.swapaxes(Bcout, 0, 1).reshape(N_A, D)
    DWBf = ffn_dw_b.reshape(DEPTH, N_DEV, 1, nf)
    zero_bias = jnp.zeros((D,), F32)

    xs = x.reshape(T, D)
    sv = []
    kv = hkv = None
    pending, _ = begin_gather(stages[0], DWf)
    Ws = {stages[0]: end_gather(stages[0], pending, xs)}
    sv = [{} for _ in range(DEPTH)]
    for idx, stage in enumerate(stages):
        part, i = stage
        L, W = sv[i], Ws[stage]
        gain = {"mix": mix_pre_g[i], "ffn": ffn_pre_g[i], "kv": kv_norm_g}[part]
        following = stages[idx + 1] if idx + 1 < len(stages) else None
        if following is not None:
            pending, token = begin_gather(following, next(iter(W.values())))
            gain = gain + token[0, 0]
        if part == "mix":
            L["x_in"] = xs
            if i < N_A:
                L["u"], L["h"] = norm_mm(f"cm_in_{i}", xs, gain, W["cin"], 0, Bcin[:, i:i + 1])
                L["c"] = cm_glu_conv(f"cm_conv_{i}", L["u"], DWc, DWBc, i, seq)
                L["s"] = ln_silu(f"cm_ln_{i}", L["c"], LNg[i], LNb[i])
                L["y"], xs = mm_resid_norm(f"cm_out_{i}", L["s"], W["cout"], 0, Bcout[i], xs, mix_post_g[i])
            else:
                L["q"], L["h"] = norm_mm(f"attn_q_{i}", xs, gain, W["q"], 0)
                L["mg"], L["lse"] = attn_fwd(f"attn_{i}", L["q"], kv, nb, seq)
                L["y"], xs = mm_resid_norm(f"attn_out_{i}", L["mg"], W["o"], 0, zero_bias, xs, mix_post_g[i])
            L["x1"] = xs
        elif part == "ffn":
            L["uf"], L["hf"], L["z"], L["ag"] = ffn_in_conv(f"ffn_in_{i}", xs, gain, W["fin"], DWf, DWBf, i, seq)
            L["yf"], xs = mm_resid_norm(f"ffn_out_{i}", L["z"], W["fout"], 0, zero_bias, xs, ffn_post_g[i])
        else:
            kv, hkv = norm_mm("kv_proj", xs, gain, W["kv"], 0)
        if following is not None:
            Ws[following] = end_gather(following, pending, kv if part == "kv" else xs)
    dx, loss_part = loss_fwd_bwd("loss", xs, loss_target.reshape(T, D))

    g_mix_pre, g_mix_post, g_ffn_pre, g_ffn_post = [None] * DEPTH, [None] * DEPTH, [None] * DEPTH, [None] * DEPTH
    g_ffn_dw, g_ffn_dwb = [None] * DEPTH, [None] * DEPTH
    g_cbin, g_cdw, g_cdwb, g_lng, g_lnb, g_cbout = ([None] * N_A for _ in range(6))
    g_kvn = dkv = None
    landed = [{} for _ in range(DEPTH)]
    in_flight = token = None
    for stage in reversed(stages):
        part, i = stage
        L, W = sv[i], Ws[stage]
        gain = {"mix": mix_post_g[i], "ffn": ffn_post_g[i], "kv": kv_norm_g}[part]
        if token is not None:
            gain = gain + token[0, 0]
        send = {}
        if part == "kv":
            send["kv"] = mm_tn("kv_wg", hkv[None], dkv)
            dx, g_kvn = mm_nt_norm_bwd("kv_bwd", dkv, W["kv"], 0, sv[i + 1]["x_in"], gain, dx)
        elif part == "ffn":
            dyf, g_ffn_post[i], _ = resid_norm_bwd(f"ffn_post_bwd_{i}", dx, L["yf"], gain)
            send["fout"] = mm_tn(f"ffn_out_wg_{i}", L["z"], dyf).reshape(N_DEV, nf // 2, D)
            duf, ddw, ddwb, dx, g_ffn_pre[i] = ffn_bwd(f"ffn_bwd_{i}", dyf, W["fout"], L["uf"], L["ag"], DWf, W["fin"],
                                                       L["x1"], ffn_pre_g[i], dx, i, seq)
            g_ffn_dw[i], g_ffn_dwb[i] = ddw.reshape(N_DEV, FFN_CONV_W, nf), ddwb.reshape(-1)
            send["fin"] = mm_tn(f"ffn_in_wg_{i}", duf, L["hf"][None])
        else:
            dy, g_mix_post[i], dyb = resid_norm_bwd(f"mix_post_bwd_{i}", dx, L["y"], gain)
            if i >= N_A:
                dm = mm_nt(f"attn_out_bwd_{i}", dy, W["o"], 0, F32)
                send["o"] = mm_tn(f"attn_out_wg_{i}", L["mg"], dy).reshape(N_DEV, D // N_DEV, D)
                dq, dkv = attn_bwd(f"attn_bwd_{i}", L["q"], kv, dm, L["mg"], L["lse"], dkv, nb, seq)
                send["q"] = mm_tn(f"attn_q_wg_{i}", L["h"][None], dq)
                dx, g_mix_pre[i] = mm_nt_norm_bwd(f"attn_q_bwd_{i}", dq, W["q"], 0, L["x_in"], mix_pre_g[i], dx)
            else:
                g_cbout[i] = dyb
                ds = mm_nt(f"cm_out_bwd_{i}", dy, W["cout"], 0, F32)
                send["cout"] = mm_tn(f"cm_out_wg_{i}", L["s"], dy).reshape(N_DEV, D // N_DEV, D)
                dc, g_lng[i], g_lnb[i] = ln_silu_bwd(f"cm_ln_bwd_{i}", ds, L["c"], LNg[i], LNb[i])
                du, g_cdw[i], g_cdwb[i], dbi = cm_glu_conv_bwd(f"cm_conv_bwd_{i}", L["u"], dc, DWc, i, seq)
                g_cbin[i] = dbi.reshape(N_DEV, -1)
                send["cin"] = mm_tn(f"cm_in_wg_{i}", L["h"][None], du)
                dx, g_mix_pre[i] = mm_nt_norm_bwd(f"cm_in_bwd_{i}", du, W["cin"], 0, L["x_in"], mix_pre_g[i], dx)
        if in_flight is not None:
            (p, j), names, handle = in_flight
            landed[j].update(zip(names, exchange_end(f"scatter_end_{p}{j}", handle, dx)))
        names, arrays = list(send), list(send.values())
        lands = place_own(f"scatter_own_{part}{i}", "scatter", arrays)
        handle, token = exchange_begin(f"scatter_begin_{part}{i}", "scatter", arrays, lands, dx)
        in_flight = (stage, names, handle)
    grad_x = dx.reshape(nb, seq, D)

    rep_names = ["mix_pre_g", "mix_post_g", "ffn_pre_g", "ffn_post_g", "kv_norm_g", "ffn_dw_b"]
    rep_parts = [jnp.concatenate(g_mix_pre), jnp.concatenate(g_mix_post), jnp.concatenate(g_ffn_pre),
                 jnp.concatenate(g_ffn_post), g_kvn.reshape(-1), jnp.stack(g_ffn_dwb)]
    rep_w = [mix_pre_g, mix_post_g, ffn_pre_g, ffn_post_g, kv_norm_g, ffn_dw_b]
    rep_m = [m_mix_pre_g, m_mix_post_g, m_ffn_pre_g, m_ffn_post_g, m_kv_norm_g, m_ffn_dw_b]
    rep_v = [v_mix_pre_g, v_mix_post_g, v_ffn_pre_g, v_ffn_post_g, v_kv_norm_g, v_ffn_dw_b]
    sh_names = ["ffn_dw", "cm_b_in", "cm_dw", "cm_dw_b", "cm_ln_g", "cm_ln_b", "cm_b_out"]
    own = lambda per_layer, shard: jnp.stack([p.reshape((N_DEV,) + shard) for p in per_layer], axis=1)
    sh_parts = [own(g_ffn_dw, ffn_dw.shape[1:]), own(g_cbin, cm_b_in.shape[1:]), own(g_cdw, cm_dw.shape[1:]),
                own(g_cdwb, cm_dw_b.shape[1:]), own(g_lng, cm_ln_g.shape[1:]), own(g_lnb, cm_ln_b.shape[1:]),
                own(g_cbout, cm_b_out.shape[1:])]
    sh_w = [ffn_dw, cm_b_in, cm_dw, cm_dw_b, cm_ln_g, cm_ln_b, cm_b_out]
    sh_m = [m_ffn_dw, m_cm_b_in, m_cm_dw, m_cm_dw_b, m_cm_ln_g, m_cm_ln_b, m_cm_b_out]
    sh_v = [v_ffn_dw, v_cm_b_in, v_cm_dw, v_cm_dw_b, v_cm_ln_g, v_cm_ln_b, v_cm_b_out]
    rep_pack = _pack([loss_part] + rep_parts)
    sh_pack = jnp.stack([_pack([p[k] for p in sh_parts]) for k in range(N_DEV)])
    n_rep = rep_pack.shape[0]
    small = []
    for kind, pack in (("gather", rep_pack), ("scatter", sh_pack)):
        lands = place_own(f"{kind}_own_small", kind, [pack])
        handle, token = exchange_begin(f"{kind}_begin_small", kind, [pack], lands, token)
        small.append((kind, handle))

    def big_update(name, w, m, v, key, layers, after):
        as3 = lambda t: t.reshape((-1,) + t.shape[-2:])
        outs = adamw_sum(name, as3(w), as3(m), as3(v), [landed[i][key] for i in layers], after)
        return [t.reshape(w.shape) for t in outs]

    conf, attn = range(N_A), range(N_A, DEPTH)
    upd = {}
    upd["ffn_w_in"] = [t_(t) for t in big_update("adam_ffn_w_in", fin_t, m_fin_t, v_fin_t, "fin", range(DEPTH), token)]
    upd["ffn_w_out"] = big_update("adam_ffn_w_out", ffn_w_out, m_ffn_w_out, v_ffn_w_out, "fout", range(DEPTH), token)
    upd["w_kv"] = big_update("adam_w_kv", w_kv, m_w_kv, v_w_kv, "kv", [N_A - 1], token)
    upd["w_q"] = big_update("adam_w_q", w_q, m_w_q, v_w_q, "q", attn, token)
    upd["w_o"] = big_update("adam_w_o", w_o, m_w_o, v_w_o, "o", attn, upd["w_q"][0])
    (p, j), names, handle = in_flight
    landed[j].update(zip(names, exchange_end(f"scatter_end_{p}{j}", handle, upd["w_o"][0])))
    upd["cm_w_in"] = big_update("adam_cm_w_in", cm_w_in, m_cm_w_in, v_cm_w_in, "cin", conf, token)
    upd["cm_w_out"] = big_update("adam_cm_w_out", cm_w_out, m_cm_w_out, v_cm_w_out, "cout", conf, upd["cm_w_in"][0])
    (rep_landed,), (sh_landed,) = (exchange_end(f"{kind}_end_small", handle, upd["cm_w_out"][0])
                                   for kind, handle in small)
    rep_sum = sum_partials("sum_small_rep", rep_landed)
    sh_sum = sum_partials("sum_small_sh", sh_landed)
    rep_shapes = [(1, 1)] + [w.shape for w in rep_w]
    sh_shapes = [w.shape for w in sh_w]
    g_small = jnp.concatenate([rep_sum, sh_sum])
    pad1 = jnp.zeros((1, 1), F32)
    d_s, m_s, v_s = adamw_small("adam_small", jnp.concatenate([_pack([pad1] + rep_w), _pack(sh_w)]), g_small,
                                jnp.concatenate([_pack([pad1] + rep_m), _pack(sh_m)]),
                                jnp.concatenate([_pack([pad1] + rep_v), _pack(sh_v)]))
    split = lambda t: (_unpack(t[:n_rep], rep_shapes), _unpack(t[n_rep:], sh_shapes))
    for (rep_t, sh_t), slot in zip([split(g_small), split(d_s), split(m_s), split(v_s)], range(4)):
        if slot == 0:
            loss = rep_t[0].reshape(())
        for name, t in zip(rep_names, rep_t[1:]):
            upd.setdefault(name, [None] * 4)[slot] = t
        for name, t in zip(sh_names, sh_t):
            upd.setdefault(name, [None] * 4)[slot] = t

    order = ["mix_pre_g", "mix_post_g", "ffn_pre_g", "ffn_post_g", "cm_w_in", "cm_b_in", "cm_dw", "cm_dw_b", "cm_ln_g",
             "cm_ln_b", "cm_w_out", "cm_b_out", "kv_norm_g", "w_kv", "w_q", "w_o", "ffn_w_in", "ffn_dw", "ffn_dw_b",
             "ffn_w_out"]
    return (loss, grad_x, *[upd[n][0] for n in order], *[upd[n][1] for n in order],
            *[upd[n][2] for n in order], *[upd[n][3] for n in order])
```

```python
import functools

import jax
import jax.numpy as jnp
from jax import lax
from jax.experimental import pallas as pl
from jax.experimental.pallas import tpu as pltpu

N_DEV = 8
N_A = 2
DEPTH = 4
N_HEADS = 8
N_GROUPS = 3
DILATIONS = (1, 4, 16)
ATT_BLOCK = 128
ATT_BATCH = 8
CONV_W = 31
FFN_CONV_W = 3
CONV_HALO = 32
FFN_HALO = 16
ROW_CHUNK_FWD = 16
ROW_CHUNK_BWD = 16
DGRAD_GROUP_ALL_MAX_N = 384
FFN_COL_TILE = 1024
EPS = 1e-6
NEG = -1e30
ADAM_LR, ADAM_B1, ADAM_B2, ADAM_EPS, ADAM_WD, ADAM_STEP = 0.001, 0.9, 0.999, 1e-08, 0.01, 10
MM = jnp.bfloat16
F32 = jnp.float32
VMEM_LIMIT_BYTES = 56 * 1024 * 1024
PACK = 1024
MESH_ID = pl.DeviceIdType.MESH

_pallas = pl.pallas_call


def _call(body, *, name, out_shape, grid=(), in_specs=None, out_specs=None, scratch=()):
    return _pallas(body, name=name, out_shape=out_shape, grid=grid, in_specs=in_specs, out_specs=out_specs,
                   scratch_shapes=list(scratch),
                   compiler_params=pltpu.CompilerParams(vmem_limit_bytes=VMEM_LIMIT_BYTES))


def _tile(n, pref):
    if n <= pref:
        return n
    t = pref - pref % 8
    while n % t:
        t -= 8
    assert t > 0, (n, pref)
    return t


def _sds(shape, dtype):
    return jax.ShapeDtypeStruct(tuple(shape), dtype)


def _dot(a, b):
    return jnp.dot(a, b, preferred_element_type=F32)


def _dot_nt(a, b):
    return lax.dot_general(a, b, (((1,), (1,)), ((), ())), preferred_element_type=F32)


def _dot_tn(a, b):
    return lax.dot_general(a, b, (((0,), (0,)), ((), ())), preferred_element_type=F32)


def _sigmoid(x):
    return 1.0 / (1.0 + jnp.exp(-x))


def _my_index():
    return 4 * lax.axis_index("x") + 2 * lax.axis_index("y") + lax.axis_index("c")


def _exchange(name, arrays, out_shapes, pieces, src_of, dst_of):
    n = len(arrays)
    base = [sum(pieces[:a]) for a in range(n)]
    total = sum(pieces)

    def body(*refs):
        ins, outs = refs[:n], refs[n:2 * n]
        send_sems, recv_sems, local_sems = refs[2 * n:]
        x, y, c = lax.axis_index("x"), lax.axis_index("y"), lax.axis_index("c")
        me = 4 * x + 2 * y + c
        copies = []
        for a in range(n):
            for k, (s, d) in enumerate(zip(src_of(a, ins[a], me), dst_of(a, outs[a], me))):
                cp = pltpu.make_async_copy(s, d, local_sems.at[base[a] + k])
                cp.start()
                copies.append(cp)
        remote = []
        for m in range(1, N_DEV):
            px, py, pc = x ^ (m >> 2), y ^ ((m >> 1) & 1), c ^ (m & 1)
            peer = 4 * px + 2 * py + pc
            for a in range(n):
                for k, (s, d) in enumerate(zip(src_of(a, ins[a], peer), dst_of(a, outs[a], me))):
                    cp = pltpu.make_async_remote_copy(src_ref=s, dst_ref=d, send_sem=send_sems.at[base[a] + k, m - 1],
                                                      recv_sem=recv_sems.at[base[a] + k, m - 1],
                                                      device_id=(px, py, pc), device_id_type=MESH_ID)
                    cp.start()
                    remote.append(cp)
        for cp in copies:
            cp.wait()
        for cp in remote:
            cp.wait_send()
        for m in range(1, N_DEV):
            px, py, pc = x ^ (m >> 2), y ^ ((m >> 1) & 1), c ^ (m & 1)
            peer = 4 * px + 2 * py + pc
            for a in range(n):
                for k, (s, d) in enumerate(zip(src_of(a, ins[a], me), dst_of(a, outs[a], peer))):
                    pltpu.make_async_remote_copy(src_ref=s, dst_ref=d, send_sem=send_sems.at[base[a] + k, m - 1],
                                                 recv_sem=recv_sems.at[base[a] + k, m - 1], device_id=(px, py, pc),
                                                 device_id_type=MESH_ID).wait_recv()

    any_spec = pl.BlockSpec(memory_space=pl.ANY)
    return _call(body, name=name, out_shape=[_sds(s, a.dtype) for s, a in zip(out_shapes, arrays)],
                 in_specs=[any_spec] * n, out_specs=[any_spec] * n,
                 scratch=[pltpu.SemaphoreType.DMA((total, N_DEV - 1)), pltpu.SemaphoreType.DMA((total, N_DEV - 1)),
                          pltpu.SemaphoreType.DMA((total,))])(*arrays)


def all_gather(name, arrays, row_sharded):
    def out_shape(a):
        s = arrays[a].shape
        return (s[0], N_DEV) + s[1:] if row_sharded[a] else (N_DEV,) + s

    def src_of(a, ref, peer):
        if row_sharded[a]:
            return [ref.at[l] for l in range(arrays[a].shape[0])]
        return [ref]

    def dst_of(a, ref, me):
        if row_sharded[a]:
            return [ref.at[l, me] for l in range(arrays[a].shape[0])]
        return [ref.at[me]]

    pieces = [arrays[a].shape[0] if row_sharded[a] else 1 for a in range(len(arrays))]
    return _exchange(name, arrays, [out_shape(a) for a in range(len(arrays))], pieces, src_of, dst_of)


def _src_view(kind, ref, peer):
    return ref if kind == "gather" else ref.at[peer]


def _peers(x, y, c):
    for m in range(1, N_DEV):
        px, py, pc = x ^ (m >> 2), y ^ ((m >> 1) & 1), c ^ (m & 1)
        yield m - 1, (px, py, pc), 4 * px + 2 * py + pc


def place_own(name, kind, srcs):
    n = len(srcs)
    shapes = [(N_DEV,) + s.shape if kind == "gather" else s.shape for s in srcs]
    steps = 2 if all(s.shape[-2] % 32 == 0 for s in srcs) else 1

    def body(*refs):
        for a in range(n):
            refs[n + a][...] = refs[a][...]

    def spec(shape, own_block):
        R, C = shape[-2:]
        tr = R // steps
        if own_block:
            return pl.BlockSpec((None, tr, C), lambda i: (_my_index(), i, 0))
        return pl.BlockSpec((tr, C), lambda i: (i, 0))

    return _call(body, name=name, grid=(steps,), out_shape=[_sds(s, a.dtype) for s, a in zip(shapes, srcs)],
                 in_specs=[spec(s.shape, kind == "scatter") for s in srcs],
                 out_specs=[spec(s, True) for s in shapes])(*srcs)


_HBM_SPEC = pl.BlockSpec(memory_space=pltpu.HBM)
_SEM_SPEC = pl.BlockSpec(memory_space=pltpu.SEMAPHORE)
_DATAFLOW = pltpu.SideEffectType.DATAFLOW_SIDE_EFFECTING


def _remote(kind, src, land, send_sems, recv_sems, a, slot, frm, to_id, at):
    return pltpu.make_async_remote_copy(src_ref=_src_view(kind, src, frm), dst_ref=land.at[at],
                                        send_sem=send_sems.at[a * (N_DEV - 1) + slot],
                                        recv_sem=recv_sems.at[a * (N_DEV - 1) + slot],
                                        device_id=to_id, device_id_type=MESH_ID)


def exchange_begin(name, kind, srcs, lands, after):
    n = len(srcs)

    def body(*refs):
        ins, lnd = refs[:n], refs[n:2 * n]
        send_sems, recv_sems = refs[2 * n + 1], refs[2 * n + 2]
        token = refs[-1]
        x, y, c = lax.axis_index("x"), lax.axis_index("y"), lax.axis_index("c")
        me = 4 * x + 2 * y + c
        for slot, peer_id, peer in _peers(x, y, c):
            for a in range(n):
                _remote(kind, ins[a], lnd[a], send_sems, recv_sems, a, slot, peer, peer_id, me).start()
        token[...] = jnp.zeros_like(token)

    hbm = lambda t: pltpu.HBM(t.shape, t.dtype)
    outs = _pallas(
        body, name=name,
        out_shape=(pltpu.SemaphoreType.DMA((n * (N_DEV - 1),)), pltpu.SemaphoreType.DMA((n * (N_DEV - 1),)),
                   *[hbm(t) for t in srcs], *[hbm(t) for t in lands], _sds((8, 128), F32)),
        in_specs=[_HBM_SPEC] * (2 * n) + [pl.BlockSpec(memory_space=pl.ANY)],
        out_specs=(_SEM_SPEC, _SEM_SPEC, *[_HBM_SPEC] * (2 * n), pl.BlockSpec(memory_space=pltpu.VMEM)),
        input_output_aliases={i: 2 + i for i in range(2 * n)},
        compiler_params=pltpu.CompilerParams(has_side_effects=_DATAFLOW),
    )(*[pltpu.with_memory_space_constraint(t, pltpu.HBM) for t in list(srcs) + list(lands)], after)
    return (kind, outs[0], outs[1], list(outs[2:2 + n]), list(outs[2 + n:2 + 2 * n])), outs[-1]


def exchange_end(name, handle, after):
    kind, send_sems, recv_sems, srcs, lands = handle
    n = len(srcs)

    def body(*refs):
        ins, lnd = refs[:n], refs[n:2 * n]
        s_sems, r_sems = refs[2 * n], refs[2 * n + 1]
        x, y, c = lax.axis_index("x"), lax.axis_index("y"), lax.axis_index("c")
        me = 4 * x + 2 * y + c
        for slot, peer_id, peer in _peers(x, y, c):
            for a in range(n):
                _remote(kind, ins[a], lnd[a], s_sems, r_sems, a, slot, peer, peer_id, me).wait_send()
        for slot, peer_id, peer in _peers(x, y, c):
            for a in range(n):
                _remote(kind, ins[a], lnd[a], s_sems, r_sems, a, slot, me, peer_id, peer).wait_recv()

    hbm = lambda t: pltpu.HBM(t.shape, t.dtype)
    outs = _pallas(
        body, name=name, out_shape=tuple(hbm(t) for t in srcs + lands),
        in_specs=[_HBM_SPEC] * (2 * n) + [_SEM_SPEC, _SEM_SPEC, pl.BlockSpec(memory_space=pl.ANY)],
        out_specs=tuple([_HBM_SPEC] * (2 * n)), input_output_aliases={i: i for i in range(2 * n)},
        compiler_params=pltpu.CompilerParams(has_side_effects=_DATAFLOW),
    )(*srcs, *lands, send_sems, recv_sems, after)
    return list(outs[n:])


def norm_mm(name, x, gain, w, layer, bias=None, w_t=False):
    T, D = x.shape
    nsh = w.shape[0]
    n = w.shape[2] if w_t else w.shape[3]
    tm = _tile(T, 2048)

    def body(*refs):
        if bias is None:
            x_ref, g_ref, w_ref, u_ref, h_ref = refs
        else:
            x_ref, g_ref, w_ref, b_ref, u_ref, h_ref = refs

        @pl.when(pl.program_id(1) == 0)
        def _():
            xf = x_ref[...]
            r = lax.rsqrt(jnp.mean(xf * xf, axis=-1, keepdims=True) + EPS)
            h_ref[...] = (xf * r * g_ref[...]).astype(h_ref.dtype)

        acc = (_dot_nt if w_t else _dot)(h_ref[...], w_ref[...])
        if bias is not None:
            acc = acc + b_ref[...]
        u_ref[...] = acc.astype(u_ref.dtype)

    in_specs = [pl.BlockSpec((tm, D), lambda i, j: (i, 0)),
                pl.BlockSpec((1, D), lambda i, j: (0, 0)),
                pl.BlockSpec((None, None) + w.shape[2:], lambda i, j: (j, layer, 0, 0))]
    args = [x, gain.reshape(1, D), w]
    if bias is not None:
        in_specs.append(pl.BlockSpec((None, None, 1, n), lambda i, j: (j, layer, 0, 0)))
        args.append(bias)
    return _call(body, name=name, grid=(T // tm, nsh), in_specs=in_specs,
                 out_specs=[pl.BlockSpec((None, tm, n), lambda i, j: (j, i, 0)),
                            pl.BlockSpec((tm, D), lambda i, j: (i, 0))],
                 out_shape=[_sds((nsh, T, n), MM), _sds((T, D), MM)])(*args)


def mm_resid_norm(name, a, w, layer, bias, x, gain):
    nk, T, kk = a.shape
    D = x.shape[1]
    tm = _tile(T, 512)

    def body(a_ref, w_ref, b_ref, x_ref, g_ref, y_ref, xn_ref):
        y = _dot(a_ref[0], w_ref[0])
        for q in range(1, nk):
            y = y + _dot(a_ref[q], w_ref[q])
        y = y + b_ref[...]
        y_ref[...] = y
        r = lax.rsqrt(jnp.mean(y * y, axis=-1, keepdims=True) + EPS)
        xn_ref[...] = x_ref[...] + y * r * g_ref[...]

    return _call(body, name=name, grid=(T // tm,),
                 in_specs=[pl.BlockSpec((nk, tm, kk), lambda i: (0, i, 0)),
                           pl.BlockSpec((None, nk, kk, D), lambda i: (layer, 0, 0, 0)),
                           pl.BlockSpec((1, D), lambda i: (0, 0)),
                           pl.BlockSpec((tm, D), lambda i: (i, 0)),
                           pl.BlockSpec((1, D), lambda i: (0, 0))],
                 out_specs=[pl.BlockSpec((tm, D), lambda i: (i, 0))] * 2,
                 out_shape=[_sds((T, D), F32)] * 2)(a, w, bias.reshape(1, D), x, gain.reshape(1, D))


def _halo_maps(tm, hb, T):
    per = tm // hb
    last = T // hb - 1
    return (lambda i: jnp.maximum(i * per - 1, 0)), (lambda i: jnp.minimum((i + 1) * per, last))


def cm_glu_conv(name, u, dw, dwb, layer, seq):
    _, T, n = u.shape
    ct = dw.shape[-1]
    per = n // ct
    nct = 4 * per
    tm = _tile(seq, 512)
    tps = seq // tm
    hb = CONV_HALO
    prev, _ = _halo_maps(tm, hb, T)
    u4 = u.reshape(2, 4, T, n)

    def body(u_ref, uh_ref, w_ref, b_ref, o_ref, pad_ref):
        first = (pl.program_id(0) % tps) == 0
        um = u_ref[...].astype(F32)
        uh = uh_ref[...].astype(F32)
        pad_ref[pl.ds(hb, tm), :] = um[0] * _sigmoid(um[1])
        pad_ref[pl.ds(0, hb), :] = jnp.where(first, 0.0, uh[0] * _sigmoid(uh[1]))
        acc = jnp.zeros((tm, ct), F32) + b_ref[...]
        for k in range(CONV_W):
            acc = acc + w_ref[pl.ds(k, 1), :] * pad_ref[pl.ds(hb - (CONV_W - 1) + k, tm), :]
        o_ref[...] = acc

    return _call(body, name=name, grid=(T // tm, nct),
                 in_specs=[pl.BlockSpec((2, None, tm, ct), lambda i, c: (0, c // per, i, c % per)),
                           pl.BlockSpec((2, None, hb, ct), lambda i, c: (0, c // per, prev(i), c % per)),
                           pl.BlockSpec((None, None, CONV_W, ct), lambda i, c: (c, layer, 0, 0)),
                           pl.BlockSpec((None, None, 1, ct), lambda i, c: (c, layer, 0, 0))],
                 out_specs=pl.BlockSpec((tm, ct), lambda i, c: (i, c)),
                 out_shape=_sds((T, nct * ct), F32),
                 scratch=[pltpu.VMEM((tm + hb, ct), F32)])(u4, u4, dw, dwb)


def ln_silu(name, c, g, b):
    T, D = c.shape
    tm = _tile(T, 512)

    def body(c_ref, g_ref, b_ref, s_ref):
        cf = c_ref[...]
        mu = jnp.mean(cf, axis=-1, keepdims=True)
        xc = cf - mu
        r = lax.rsqrt(jnp.mean(xc * xc, axis=-1, keepdims=True) + EPS)
        t = xc * r * g_ref[...] + b_ref[...]
        s_ref[...] = (t * _sigmoid(t)).astype(s_ref.dtype)

    return _call(body, name=name, grid=(T // tm,),
                 in_specs=[pl.BlockSpec((tm, D), lambda i: (i, 0)), pl.BlockSpec((1, D), lambda i: (0, 0)),
                           pl.BlockSpec((1, D), lambda i: (0, 0))],
                 out_specs=pl.BlockSpec((None, tm, D), lambda i: (0, i, 0)),
                 out_shape=_sds((1, T, D), MM))(c, g.reshape(1, D), b.reshape(1, D))


def _col_tiles(n, width):
    return [(c0, min(width, n - c0)) for c0 in range(0, n, width)]


def ffn_in_conv(name, x, gain, wt, dw, dwb, layer, seq):
    T, D = x.shape
    n = wt.shape[2]
    tm = _tile(seq, 1024)
    tps = seq // tm
    hb = 8
    rows = _tile(tm, ROW_CHUNK_FWD)
    K = FFN_CONV_W
    w5 = wt.reshape(2, 4, n, D)
    dw4 = dw.reshape(2, 4, dw.shape[1], K, n)
    dwb4 = dwb.reshape(dwb.shape[0], 2, 4, 1, n)

    def body(x_ref, g_ref, w_ref, cw_ref, cb_ref, u_ref, h_ref, z_ref, ag_ref, pad_ref, carry_ref):
        i, j = pl.program_id(0), pl.program_id(1)
        first = (i % tps) == 0

        @pl.when(j == 0)
        def _():
            xf = x_ref[...]
            r = lax.rsqrt(jnp.mean(xf * xf, axis=-1, keepdims=True) + EPS)
            h_ref[...] = (xf * r * g_ref[...]).astype(h_ref.dtype)

        h = h_ref[...]
        for c0, wc in _col_tiles(n, FFN_COL_TILE):
            cols = pl.ds(c0, wc)
            for half in range(2):
                res = _dot_nt(h, w_ref[half, cols, :])
                u_ref[half, :, cols] = res.astype(u_ref.dtype)
                pad_ref[half, pl.ds(hb, tm), cols] = res
                pad_ref[half, pl.ds(0, hb), cols] = jnp.where(first, 0.0, carry_ref[j, half, :, cols])
            for r0 in range(0, tm, rows):
                conv = []
                for half in range(2):
                    acc = cb_ref[half, :, cols] + cw_ref[half, pl.ds(K - 1, 1), cols] * pad_ref[
                        half, pl.ds(hb + r0, rows), cols]
                    for k in range(K - 1):
                        acc = acc + cw_ref[half, pl.ds(k, 1), cols] * pad_ref[
                            half, pl.ds(hb + r0 - (K - 1) + k, rows), cols]
                    conv.append(acc)
                a, g = conv
                z_ref[pl.ds(r0, rows), cols] = (g * _sigmoid(g) * a).astype(z_ref.dtype)
                ag_ref[0, pl.ds(r0, rows), cols] = a.astype(ag_ref.dtype)
                ag_ref[1, pl.ds(r0, rows), cols] = g.astype(ag_ref.dtype)
            for half in range(2):
                carry_ref[j, half, :, cols] = pad_ref[half, pl.ds(tm, hb), cols]

    u, h, z, ag = _call(
        body, name=name, grid=(T // tm, 4),
        in_specs=[pl.BlockSpec((tm, D), lambda i, j: (i, 0)),
                  pl.BlockSpec((1, D), lambda i, j: (0, 0)),
                  pl.BlockSpec((2, None, n, D), lambda i, j: (0, j, 0, 0)),
                  pl.BlockSpec((2, None, None, K, n), lambda i, j: (0, j, layer, 0, 0)),
                  pl.BlockSpec((None, 2, None, 1, n), lambda i, j: (layer, 0, j, 0, 0))],
        out_specs=[pl.BlockSpec((2, None, tm, n), lambda i, j: (0, j, i, 0)),
                   pl.BlockSpec((tm, D), lambda i, j: (i, 0)),
                   pl.BlockSpec((None, tm, n), lambda i, j: (j, i, 0)),
                   pl.BlockSpec((2, None, tm, n), lambda i, j: (0, j, i, 0))],
        out_shape=[_sds((2, 4, T, n), MM), _sds((T, D), MM), _sds((4, T, n), MM), _sds((2, 4, T, n), MM)],
        scratch=[pltpu.VMEM((2, tm + hb, n), F32), pltpu.VMEM((4, 2, hb, n), F32)])(
            x, gain.reshape(1, D), w5, dw4, dwb4)
    return u.reshape(8, T, n), h, z, ag


def _head_specs(seq, dh, q_heads, kv_heads):
    def spec(per, base):
        def imap(b, h, g):
            f = base + g * N_HEADS + h
            return (f // per, b, f % per)
        return pl.BlockSpec((None, seq, dh), imap)
    return spec(q_heads, 0), spec(kv_heads, 0), spec(kv_heads, N_GROUPS * N_HEADS)


def _rows(start, d, blocks=1):
    size = blocks * ATT_BLOCK
    return pl.ds(start, size, stride=d) if d > 1 else pl.ds(start, size)


def _att_pad():
    return max(ATT_BLOCK * d for d in DILATIONS[:-1])


def _band_mask(first, nblk):
    keys = ATT_BLOCK if nblk == 1 else 2 * ATT_BLOCK
    shape = (ATT_BATCH, ATT_BLOCK, keys)
    qi = lax.broadcasted_iota(jnp.int32, shape, 1)
    kj = lax.broadcasted_iota(jnp.int32, shape, 2)
    if nblk == 1:
        return kj <= qi
    n = (first + lax.broadcasted_iota(jnp.int32, shape, 0)) % nblk
    return (kj >= qi) & (kj <= qi + ATT_BLOCK) & ((n > 0) | (kj >= ATT_BLOCK))


def _block_rows(idx, d, nblk):
    r, n = idx // nblk, idx % nblk
    rq = _rows(r + d * ATT_BLOCK * n, d)
    if nblk == 1:
        return rq, _rows(_att_pad() + r + d * ATT_BLOCK * n, d)
    return rq, _rows(_att_pad() + r + d * ATT_BLOCK * (n - 1), d, blocks=2)


def _bdot(a, b, ca, cb):
    return lax.dot_general(a, b, (((ca,), (cb,)), ((0,), (0,))), preferred_element_type=F32)


def attn_fwd(name, q, kv, nb, seq):
    _, T, qn = q.shape
    dh = qn * N_DEV // (N_GROUPS * N_HEADS)
    scale = 1.0 / (dh ** 0.5)
    qs, ks, vs = _head_specs(seq, dh, qn // dh, kv.shape[2] // dh)

    def body(q_ref, k_ref, v_ref, m_ref, l_ref, qf, kf, vf, *branch):
        og, lg = branch[:N_GROUPS], branch[N_GROUPS:]
        pad = _att_pad()
        qf[...] = q_ref[...].astype(F32)
        for t_ref, s_ref in ((k_ref, kf), (v_ref, vf)):
            s_ref[pl.ds(0, pad), :] = jnp.zeros((pad, dh), F32)
            s_ref[pl.ds(pad, seq), :] = t_ref[...].astype(F32)
        for g in range(N_GROUPS):
            d = DILATIONS[g]
            nblk = seq // d // ATT_BLOCK

            def blocks(it, carry, d=d, nblk=nblk, g=g):
                first = it * ATT_BATCH
                rows = [_block_rows(first + b, d, nblk) for b in range(ATT_BATCH)]
                qb = jnp.stack([qf[rq, :] for rq, _ in rows]).astype(MM)
                kb = jnp.stack([kf[rk, :] for _, rk in rows]).astype(MM)
                vb = jnp.stack([vf[rk, :] for _, rk in rows]).astype(MM)
                s = jnp.where(_band_mask(first, nblk), _bdot(qb, kb, 2, 2) * scale, NEG)
                m = jnp.max(s, axis=-1, keepdims=True)
                p = jnp.exp(s - m)
                den = jnp.sum(p, axis=-1, keepdims=True)
                o = _bdot(p.astype(MM), vb, 2, 1) / den
                lse = m + jnp.log(den)
                for b, (rq, _) in enumerate(rows):
                    og[g][rq, :] = o[b]
                    lg[g][rq, :] = jnp.broadcast_to(lse[b], (ATT_BLOCK, dh))
                return carry

            @pl.when(pl.program_id(2) == g)
            def _(blocks=blocks, d=d, nblk=nblk):
                lax.fori_loop(0, d * nblk // ATT_BATCH, blocks, 0)

        @pl.when(pl.program_id(2) == N_GROUPS - 1)
        def _():
            mx = jnp.maximum(jnp.maximum(lg[0][...], lg[1][...]), lg[2][...])
            e = [jnp.exp(lg[g][...] - mx) for g in range(N_GROUPS)]
            tot = e[0] + e[1] + e[2]
            m_ref[...] = ((e[0] * og[0][...] + e[1] * og[1][...] + e[2] * og[2][...]) / tot).astype(m_ref.dtype)
            l_ref[...] = mx + jnp.log(tot)

    return _call(body, name=name, grid=(nb, N_HEADS, N_GROUPS), in_specs=[qs, ks, vs],
                 out_specs=[pl.BlockSpec((None, seq, dh), lambda b, h, g: (0, b, h)),
                            pl.BlockSpec((seq, dh), lambda b, h, g: (b, h))],
                 out_shape=[_sds((1, T, N_HEADS * dh), MM), _sds((T, N_HEADS * dh), F32)],
                 scratch=[pltpu.VMEM((seq, dh), F32)] + [pltpu.VMEM((_att_pad() + seq, dh), F32)] * 2
                 + [pltpu.VMEM((seq, dh), F32)] * (2 * N_GROUPS))(q, kv, kv)


def loss_fwd_bwd(name, x, target):
    T, D = x.shape
    tm = _tile(T, 512)

    def body(x_ref, t_ref, dx_ref, l_ref):
        @pl.when(pl.program_id(0) == 0)
        def _():
            l_ref[...] = jnp.zeros_like(l_ref)
        err = x_ref[...] - t_ref[...]
        dx_ref[...] = err * (1.0 / D)
        l_ref[...] += 0.5 * jnp.sum(jnp.mean(err * err, axis=-1, keepdims=True), axis=0, keepdims=True)

    dx, l = _call(body, name=name, grid=(T // tm,),
                  in_specs=[pl.BlockSpec((tm, D), lambda i: (i, 0))] * 2,
                  out_specs=[pl.BlockSpec((tm, D), lambda i: (i, 0)), pl.BlockSpec((1, 1), lambda i: (0, 0))],
                  out_shape=[_sds((T, D), F32), _sds((1, 1), F32)])(x, target)
    return dx, l


def resid_norm_bwd(name, dx, y, gain):
    T, D = y.shape
    tm = _tile(T, 512)

    def body(dx_ref, y_ref, g_ref, dy_ref, dg_ref, db_ref):
        @pl.when(pl.program_id(0) == 0)
        def _():
            dg_ref[...] = jnp.zeros_like(dg_ref)
            db_ref[...] = jnp.zeros_like(db_ref)
        y = y_ref[...]
        d = dx_ref[...]
        r = lax.rsqrt(jnp.mean(y * y, axis=-1, keepdims=True) + EPS)
        yh = y * r
        dyh = d * g_ref[...]
        dy = r * (dyh - yh * jnp.mean(dyh * yh, axis=-1, keepdims=True))
        dy_ref[...] = dy.astype(dy_ref.dtype)
        dg_ref[...] += jnp.sum(d * yh, axis=0, keepdims=True)
        db_ref[...] += jnp.sum(dy, axis=0, keepdims=True)

    return _call(body, name=name, grid=(T // tm,),
                 in_specs=[pl.BlockSpec((tm, D), lambda i: (i, 0))] * 2 + [pl.BlockSpec((1, D), lambda i: (0, 0))],
                 out_specs=[pl.BlockSpec((None, tm, D), lambda i: (0, i, 0))] + [pl.BlockSpec((1, D), lambda i: (0, 0))] * 2,
                 out_shape=[_sds((1, T, D), MM), _sds((1, D), F32), _sds((1, D), F32)])(dx, y, gain.reshape(1, D))


def mm_nt(name, dy, w, layer, out_dtype):
    _, T, D = dy.shape
    _, nk, kk, _ = w.shape
    tm = _tile(T, 1024)

    def body(dy_ref, w_ref, o_ref):
        o_ref[...] = _dot_nt(dy_ref[...], w_ref[...]).astype(o_ref.dtype)

    return _call(body, name=name, grid=(T // tm, nk),
                 in_specs=[pl.BlockSpec((None, tm, D), lambda i, q: (0, i, 0)),
                           pl.BlockSpec((None, None, kk, D), lambda i, q: (layer, q, 0, 0))],
                 out_specs=pl.BlockSpec((None, tm, kk), lambda i, q: (q, i, 0)),
                 out_shape=_sds((nk, T, kk), out_dtype))(dy, w)


def mm_nt_norm_bwd(name, du, w, layer, x_in, gain, dx_res):
    nsh, T, n = du.shape
    D = x_in.shape[1]
    tm = _tile(T, 512)
    grp = nsh if n <= DGRAD_GROUP_ALL_MAX_N else nsh // 2
    steps = nsh // grp

    def body(du_ref, w_ref, x_ref, g_ref, dr_ref, dx_ref, dg_ref, acc_ref):
        i, j = pl.program_id(0), pl.program_id(1)

        @pl.when((i == 0) & (j == 0))
        def _():
            dg_ref[...] = jnp.zeros_like(dg_ref)

        part = _dot_nt(du_ref[0], w_ref[0])
        for k in range(1, grp):
            part = part + _dot_nt(du_ref[k], w_ref[k])

        @pl.when(j == 0)
        def _():
            acc_ref[...] = part

        @pl.when(j > 0)
        def _():
            acc_ref[...] += part

        @pl.when(j == steps - 1)
        def _():
            x = x_ref[...]
            dh = acc_ref[...]
            r = lax.rsqrt(jnp.mean(x * x, axis=-1, keepdims=True) + EPS)
            xh = x * r
            dxh = dh * g_ref[...]
            dx_ref[...] = dr_ref[...] + r * (dxh - xh * jnp.mean(dxh * xh, axis=-1, keepdims=True))
            dg_ref[...] += jnp.sum(dh * xh, axis=0, keepdims=True)

    return _call(body, name=name, grid=(T // tm, steps),
                 in_specs=[pl.BlockSpec((grp, tm, n), lambda i, j: (j, i, 0)),
                           pl.BlockSpec((grp, None, D, n), lambda i, j: (j, layer, 0, 0)),
                           pl.BlockSpec((tm, D), lambda i, j: (i, 0)),
                           pl.BlockSpec((1, D), lambda i, j: (0, 0)),
                           pl.BlockSpec((tm, D), lambda i, j: (i, 0))],
                 out_specs=[pl.BlockSpec((tm, D), lambda i, j: (i, 0)), pl.BlockSpec((1, D), lambda i, j: (0, 0))],
                 out_shape=[_sds((T, D), F32), _sds((1, D), F32)],
                 scratch=[pltpu.VMEM((tm, D), F32)])(du, w, x_in, gain.reshape(1, D), dx_res)


def mm_tn(name, a, b):
    na, T, ka = a.shape
    nb, _, kb = b.shape
    nj = max(na, nb)

    def body(a_ref, b_ref, o_ref):
        o_ref[...] = _dot_tn(a_ref[...], b_ref[...]).astype(o_ref.dtype)

    return _call(body, name=name, grid=(nj,),
                 in_specs=[pl.BlockSpec((None, T, ka), (lambda j: (j, 0, 0)) if na > 1 else (lambda j: (0, 0, 0))),
                           pl.BlockSpec((None, T, kb), (lambda j: (j, 0, 0)) if nb > 1 else (lambda j: (0, 0, 0)))],
                 out_specs=pl.BlockSpec((None, ka, kb), lambda j: (j, 0, 0)),
                 out_shape=_sds((nj, ka, kb), MM))(a, b)


def ffn_bwd(name, dy, wout, u, ag, dw, wt, x_in, gain, dx_res, layer, seq):
    _, T, D = dy.shape
    n = wt.shape[2]
    tm = _tile(seq, 512)
    tps = seq // tm
    hb = FFN_HALO
    _, nxt = _halo_maps(tm, hb, T)
    K = FFN_CONV_W
    te = tm + hb
    rows = _tile(hb, ROW_CHUNK_BWD)
    u4, w5 = u.reshape(2, 4, T, n), wt.reshape(2, 4, n, D)
    dw4 = dw.reshape(2, 4, dw.shape[1], K, n)
    tiles = _col_tiles(n, FFN_COL_TILE)

    def body(dy_ref, dyn_ref, wo_ref, u_ref, ag_ref, agn_ref, cw_ref, wt_ref, x_ref, g_ref, dr_ref,
             du_ref, ddw_ref, ddb_ref, dx_ref, dg_ref,
             uf_ref, agf_ref, dzf_ref, da_ref, acc_ref):
        i, j = pl.program_id(0), pl.program_id(1)
        last = (i % tps) == tps - 1

        @pl.when((i == 0) & (j == 0))
        def _():
            dg_ref[...] = jnp.zeros_like(dg_ref)

        @pl.when(i == 0)
        def _():
            ddw_ref[j] = jnp.zeros((2, K, n), F32)
            ddb_ref[j] = jnp.zeros((2, 1, n), F32)

        dyt, dyn = dy_ref[...], dyn_ref[...]
        for ci, (c0, wc) in enumerate(tiles):
            cols = pl.ds(c0, wc)
            dzf_ref[pl.ds(0, tm), cols] = _dot_nt(dyt, wo_ref[cols, :])
            dzf_ref[pl.ds(tm, hb), cols] = jnp.where(last, 0.0, _dot_nt(dyn, wo_ref[cols, :]))
            for half in range(2):
                uf_ref[half, :, cols] = u_ref[half, :, cols].astype(F32)
                agf_ref[half, pl.ds(0, tm), cols] = ag_ref[half, :, cols].astype(F32)
                agf_ref[half, pl.ds(tm, hb), cols] = agn_ref[half, :, cols].astype(F32)
            for r0 in range(0, te, rows):
                a, g = agf_ref[0, pl.ds(r0, rows), cols], agf_ref[1, pl.ds(r0, rows), cols]
                dzc = dzf_ref[pl.ds(r0, rows), cols]
                sg = _sigmoid(g)
                da_ref[0, pl.ds(r0, rows), cols] = dzc * (g * sg)
                da_ref[1, pl.ds(r0, rows), cols] = dzc * a * (sg * (1.0 + g * (1.0 - sg)))
            for half in range(2):
                tap_acc = [jnp.zeros((rows, wc), F32) for _ in range(K)]
                bias_acc = jnp.zeros((rows, wc), F32)
                for r0 in range(0, tm, rows):
                    xr = uf_ref[half, pl.ds(r0, rows), cols]
                    acc = None
                    for k in range(K):
                        d = da_ref[half, pl.ds(r0 + K - 1 - k, rows), cols]
                        term = cw_ref[half, pl.ds(k, 1), cols] * d
                        acc = term if acc is None else acc + term
                        tap_acc[k] = tap_acc[k] + d * xr
                        if k == K - 1:
                            bias_acc = bias_acc + d
                    du_ref[half, pl.ds(r0, rows), cols] = acc.astype(du_ref.dtype)
                for k in range(K):
                    ddw_ref[j, half, pl.ds(k, 1), cols] += jnp.sum(tap_acc[k], axis=0, keepdims=True)
                ddb_ref[j, half, :, cols] += jnp.sum(bias_acc, axis=0, keepdims=True)
            part = _dot(du_ref[0, :, cols], wt_ref[0, cols, :]) + _dot(du_ref[1, :, cols], wt_ref[1, cols, :])
            if ci == 0:
                acc_ref[...] = part + jnp.where(j == 0, 0.0, acc_ref[...])
            else:
                acc_ref[...] += part
        @pl.when(j == 3)
        def _():
            x = x_ref[...]
            dh = acc_ref[...]
            r = lax.rsqrt(jnp.mean(x * x, axis=-1, keepdims=True) + EPS)
            xh = x * r
            dxh = dh * g_ref[...]
            dx_ref[...] = dr_ref[...] + r * (dxh - xh * jnp.mean(dxh * xh, axis=-1, keepdims=True))
            dg_ref[...] += jnp.sum(dh * xh, axis=0, keepdims=True)

    f32 = lambda *shape: pltpu.VMEM(shape, F32)
    du, ddw, ddb, dx, dg = _call(
        body, name=name, grid=(T // tm, 4),
        in_specs=[pl.BlockSpec((None, tm, D), lambda i, j: (0, i, 0)),
                  pl.BlockSpec((None, hb, D), lambda i, j: (0, nxt(i), 0)),
                  pl.BlockSpec((None, None, n, D), lambda i, j: (0, j, 0, 0)),
                  pl.BlockSpec((2, None, tm, n), lambda i, j: (0, j, i, 0)),
                  pl.BlockSpec((2, None, tm, n), lambda i, j: (0, j, i, 0)),
                  pl.BlockSpec((2, None, hb, n), lambda i, j: (0, j, nxt(i), 0)),
                  pl.BlockSpec((2, None, None, K, n), lambda i, j: (0, j, layer, 0, 0)),
                  pl.BlockSpec((2, None, n, D), lambda i, j: (0, j, 0, 0)),
                  pl.BlockSpec((tm, D), lambda i, j: (i, 0)),
                  pl.BlockSpec((1, D), lambda i, j: (0, 0)),
                  pl.BlockSpec((tm, D), lambda i, j: (i, 0))],
        out_specs=[pl.BlockSpec((2, None, tm, n), lambda i, j: (0, j, i, 0)),
                   pl.BlockSpec((4, 2, K, n), lambda i, j: (0, 0, 0, 0)),
                   pl.BlockSpec((4, 2, 1, n), lambda i, j: (0, 0, 0, 0)),
                   pl.BlockSpec((tm, D), lambda i, j: (i, 0)),
                   pl.BlockSpec((1, D), lambda i, j: (0, 0))],
        out_shape=[_sds((2, 4, T, n), MM), _sds((4, 2, K, n), F32), _sds((4, 2, 1, n), F32), _sds((T, D), F32),
                   _sds((1, D), F32)],
        scratch=[f32(2, tm, n), f32(2, te, n), f32(te, n), f32(2, te, n), f32(tm, D)],
    )(dy, dy, wout, u4, ag, ag, dw4, w5, x_in, gain.reshape(1, D), dx_res)
    return du.reshape(8, T, n), jnp.swapaxes(ddw, 0, 1), jnp.swapaxes(ddb, 0, 1), dx, dg


def ln_silu_bwd(name, ds, c, g, b):
    T, D = c.shape
    tm = _tile(T, 512)

    def body(ds_ref, c_ref, g_ref, b_ref, dc_ref, dg_ref, db_ref):
        @pl.when(pl.program_id(0) == 0)
        def _():
            dg_ref[...] = jnp.zeros_like(dg_ref)
            db_ref[...] = jnp.zeros_like(db_ref)
        cf = c_ref[...]
        mu = jnp.mean(cf, axis=-1, keepdims=True)
        xc = cf - mu
        r = lax.rsqrt(jnp.mean(xc * xc, axis=-1, keepdims=True) + EPS)
        xh = xc * r
        t = xh * g_ref[...] + b_ref[...]
        sg = _sigmoid(t)
        dt = ds_ref[...] * (sg * (1.0 + t * (1.0 - sg)))
        dg_ref[...] += jnp.sum(dt * xh, axis=0, keepdims=True)
        db_ref[...] += jnp.sum(dt, axis=0, keepdims=True)
        dxh = dt * g_ref[...]
        dc_ref[...] = r * (dxh - jnp.mean(dxh, axis=-1, keepdims=True)
                           - xh * jnp.mean(dxh * xh, axis=-1, keepdims=True))

    vec = pl.BlockSpec((1, D), lambda i: (0, 0))
    return _call(body, name=name, grid=(T // tm,),
                 in_specs=[pl.BlockSpec((None, tm, D), lambda i: (0, i, 0)), pl.BlockSpec((tm, D), lambda i: (i, 0)),
                           vec, vec],
                 out_specs=[pl.BlockSpec((tm, D), lambda i: (i, 0)), vec, vec],
                 out_shape=[_sds((T, D), F32), _sds((1, D), F32), _sds((1, D), F32)])(
                     ds, c, g.reshape(1, D), b.reshape(1, D))


def cm_glu_conv_bwd(name, u, dc, dw, layer, seq):
    _, T, n = u.shape
    ct = dw.shape[-1]
    per = n // ct
    nct = 4 * per
    tm = _tile(seq, 512)
    tps = seq // tm
    hb = CONV_HALO
    prev, nxt = _halo_maps(tm, hb, T)
    u4 = u.reshape(2, 4, T, n)
    K = CONV_W

    def body(u_ref, up_ref, dc_ref, dcn_ref, w_ref, du_ref, ddw_ref, ddb_ref, dbi_ref, padp_ref, padd_ref):
        i = pl.program_id(1)
        first = (i % tps) == 0
        last = (i % tps) == tps - 1

        @pl.when(i == 0)
        def _():
            ddw_ref[...] = jnp.zeros_like(ddw_ref)
            ddb_ref[...] = jnp.zeros_like(ddb_ref)
            dbi_ref[...] = jnp.zeros_like(dbi_ref)

        um = u_ref[...].astype(F32)
        uh = up_ref[...].astype(F32)
        sg = _sigmoid(um[1])
        padp_ref[pl.ds(hb, tm), :] = um[0] * sg
        padp_ref[pl.ds(0, hb), :] = jnp.where(first, 0.0, uh[0] * _sigmoid(uh[1]))
        dcm = dc_ref[...]
        padd_ref[pl.ds(0, tm), :] = dcm
        padd_ref[pl.ds(tm, hb), :] = jnp.where(last, 0.0, dcn_ref[...])
        dp = jnp.zeros((tm, ct), F32)
        for k in range(K):
            dp = dp + w_ref[pl.ds(k, 1), :] * padd_ref[pl.ds(K - 1 - k, tm), :]
            ddw_ref[pl.ds(k, 1), :] += jnp.sum(dcm * padp_ref[pl.ds(hb - (K - 1) + k, tm), :], axis=0, keepdims=True)
        ddb_ref[...] += jnp.sum(dcm, axis=0, keepdims=True)
        dv = dp * sg
        dg = dp * um[0] * sg * (1.0 - sg)
        du_ref[0] = dv.astype(du_ref.dtype)
        du_ref[1] = dg.astype(du_ref.dtype)
        dbi_ref[0] += jnp.sum(dv, axis=0, keepdims=True)
        dbi_ref[1] += jnp.sum(dg, axis=0, keepdims=True)

    du, ddw, ddb, dbi = _call(
        body, name=name, grid=(nct, T // tm),
        in_specs=[pl.BlockSpec((2, None, tm, ct), lambda c, i: (0, c // per, i, c % per)),
                  pl.BlockSpec((2, None, hb, ct), lambda c, i: (0, c // per, prev(i), c % per)),
                  pl.BlockSpec((tm, ct), lambda c, i: (i, c)),
                  pl.BlockSpec((hb, ct), lambda c, i: (nxt(i), c)),
                  pl.BlockSpec((None, None, K, ct), lambda c, i: (c, layer, 0, 0))],
        out_specs=[pl.BlockSpec((2, None, tm, ct), lambda c, i: (0, c // per, i, c % per)),
                   pl.BlockSpec((None, K, ct), lambda c, i: (c, 0, 0)),
                   pl.BlockSpec((None, 1, ct), lambda c, i: (c, 0, 0)),
                   pl.BlockSpec((2, None, 1, ct), lambda c, i: (0, c // per, 0, c % per))],
        out_shape=[_sds((2, 4, T, n), MM), _sds((nct, K, ct), F32), _sds((nct, 1, ct), F32), _sds((2, 4, 1, n), F32)],
        scratch=[pltpu.VMEM((tm + hb, ct), F32), pltpu.VMEM((tm + hb, ct), F32)])(u4, u4, dc, dc, dw)
    return du.reshape(8, T, n), ddw, ddb, dbi


def attn_bwd(name, q, kv, dm, merged, lse, dkv_prev, nb, seq):
    _, T, qn = q.shape
    kn = kv.shape[2]
    dh = qn * N_DEV // (N_GROUPS * N_HEADS)
    scale = 1.0 / (dh ** 0.5)
    qs, ks, vs = _head_specs(seq, dh, qn // dh, kn // dh)
    has_prev = dkv_prev is not None
    n_in = 6 + (1 if has_prev else 0)

    def body(*refs):
        q_ref, k_ref, v_ref, dm_ref, mg_ref, l_ref = refs[:6]
        pkv_ref = refs[6] if has_prev else None
        dq_ref, dkv_ref = refs[n_in:n_in + 2]
        qf, kf, vf, dqf, dkf, dvf, dlt = refs[n_in + 2:]
        pad = _att_pad()

        @pl.when(pl.program_id(2) == 0)
        def _():
            dlt[...] = jnp.broadcast_to(
                jnp.sum(dm_ref[...] * mg_ref[...].astype(F32), axis=-1, keepdims=True), (seq, dh))

        qf[...] = q_ref[...].astype(F32)
        for t_ref, s_ref in ((k_ref, kf), (v_ref, vf)):
            s_ref[pl.ds(0, pad), :] = jnp.zeros((pad, dh), F32)
            s_ref[pl.ds(pad, seq), :] = t_ref[...].astype(F32)
        dkf[...] = jnp.zeros_like(dkf)
        dvf[...] = jnp.zeros_like(dvf)
        for g in range(N_GROUPS):
            d = DILATIONS[g]
            nblk = seq // d // ATT_BLOCK

            def blocks(it, carry, d=d, nblk=nblk):
                first = it * ATT_BATCH
                rows = [_block_rows(first + b, d, nblk) for b in range(ATT_BATCH)]
                qb = jnp.stack([qf[rq, :] for rq, _ in rows]).astype(MM)
                dmb = jnp.stack([dm_ref[rq, :] for rq, _ in rows]).astype(MM)
                lse = jnp.stack([l_ref[rq, :][:, :1] for rq, _ in rows])
                delta = jnp.stack([dlt[rq, :][:, :1] for rq, _ in rows])
                kb = jnp.stack([kf[rk, :] for _, rk in rows]).astype(MM)
                vb = jnp.stack([vf[rk, :] for _, rk in rows]).astype(MM)
                s = jnp.where(_band_mask(first, nblk), _bdot(qb, kb, 2, 2) * scale, NEG)
                p = jnp.exp(s - lse)
                dsc = (p * (_bdot(dmb, vb, 2, 2) - delta) * scale).astype(MM)
                dv = _bdot(p.astype(MM), dmb, 1, 1)
                dk = _bdot(dsc, qb, 1, 1)
                dq = _bdot(dsc, kb, 2, 1)
                for b, (rq, rk) in enumerate(rows):
                    dqf[rq, :] = dq[b]
                    dkf[rk, :] += dk[b]
                    dvf[rk, :] += dv[b]
                return carry

            @pl.when(pl.program_id(2) == g)
            def _(blocks=blocks, d=d, nblk=nblk):
                lax.fori_loop(0, d * nblk // ATT_BATCH, blocks, 0)

        dq_ref[...] = dqf[...].astype(dq_ref.dtype)
        dk, dv = dkf[pl.ds(pad, seq), :], dvf[pl.ds(pad, seq), :]
        if has_prev:
            dk, dv = dk + pkv_ref[0].astype(F32), dv + pkv_ref[1].astype(F32)
        dkv_ref[0] = dk.astype(dkv_ref.dtype)
        dkv_ref[1] = dv.astype(dkv_ref.dtype)

    per = kn // dh

    def both(b, h, g):
        f = g * N_HEADS + h
        return (0, f // per, b, f % per)

    kv_spec = pl.BlockSpec((2, None, seq, dh), both)
    full = pl.BlockSpec((None, seq, dh), lambda b, h, g: (0, b, h))
    in_specs = [qs, ks, vs, full, full, pl.BlockSpec((seq, dh), lambda b, h, g: (b, h))]
    args = [q, kv, kv, dm, merged, lse]
    if has_prev:
        in_specs.append(kv_spec)
        args.append(dkv_prev.reshape(2, N_DEV // 2, T, kn))
    short, padded = pltpu.VMEM((seq, dh), F32), pltpu.VMEM((_att_pad() + seq, dh), F32)
    dq, dkv = _call(body, name=name, grid=(nb, N_HEADS, N_GROUPS), in_specs=in_specs, out_specs=[qs, kv_spec],
                    out_shape=[_sds(q.shape, MM), _sds((2, N_DEV // 2, T, kn), MM)],
                    scratch=[short, padded, padded, short, padded, padded, short])(*args)
    return dq, dkv.reshape(N_DEV, T, kn)


def _adamw_math(w, g, m, v):
    m = ADAM_B1 * m + (1.0 - ADAM_B1) * g
    v = ADAM_B2 * v + (1.0 - ADAM_B2) * (g * g)
    m_hat = m / (1.0 - ADAM_B1 ** ADAM_STEP)
    v_hat = v / (1.0 - ADAM_B2 ** ADAM_STEP)
    delta = -ADAM_LR * (m_hat / (jnp.sqrt(v_hat) + ADAM_EPS) + ADAM_WD * w)
    return delta, m, v


def adamw_sum(name, w, m, v, parts, after):
    L, R, C = w.shape
    tr = _tile(R, 256)

    def body(*refs):
        w_ref, m_ref, v_ref = refs[:3]
        p_refs = refs[3:3 + L]
        g_ref, d_ref, nm_ref, nv_ref = refs[4 + L:]
        for l in range(L):
            @pl.when(pl.program_id(0) == l)
            def _(p_ref=p_refs[l]):
                g = p_ref[0].astype(F32)
                for k in range(1, N_DEV):
                    g = g + p_ref[k].astype(F32)
                g_ref[...] = g
                d_ref[...], nm_ref[...], nv_ref[...] = _adamw_math(w_ref[...], g, m_ref[...], v_ref[...])

    blk = pl.BlockSpec((None, tr, C), lambda l, i: (l, i, 0))
    part = lambda k: pl.BlockSpec((N_DEV, tr, C), lambda l, i: (0, jnp.where(l == k, i, 0), 0))
    return _call(body, name=name, grid=(L, R // tr),
                 in_specs=[blk, blk, blk] + [part(k) for k in range(L)] + [pl.BlockSpec(memory_space=pl.ANY)],
                 out_specs=[blk] * 4, out_shape=[_sds((L, R, C), F32)] * 4)(w, m, v, *parts, after)


def sum_partials(name, parts):
    _, R, C = parts.shape
    tr = _tile(R, 512)

    def body(p_ref, o_ref):
        g = p_ref[0]
        for k in range(1, N_DEV):
            g = g + p_ref[k]
        o_ref[...] = g

    return _call(body, name=name, grid=(R // tr,),
                 in_specs=[pl.BlockSpec((N_DEV, tr, C), lambda i: (0, i, 0))],
                 out_specs=pl.BlockSpec((tr, C), lambda i: (i, 0)), out_shape=_sds((R, C), F32))(parts)


def adamw_small(name, w, g, m, v):
    R, C = w.shape
    tr = _tile(R, 512)

    def body(w_ref, g_ref, m_ref, v_ref, d_ref, nm_ref, nv_ref):
        d_ref[...], nm_ref[...], nv_ref[...] = _adamw_math(w_ref[...], g_ref[...], m_ref[...], v_ref[...])

    blk = pl.BlockSpec((tr, C), lambda i: (i, 0))
    return _call(body, name=name, grid=(R // tr,), in_specs=[blk] * 4, out_specs=[blk] * 3,
                 out_shape=[_sds((R, C), F32)] * 3)(w, g, m, v)


def _pack(arrays):
    pieces = []
    for a in arrays:
        f = a.reshape(-1).astype(F32)
        pieces.append(jnp.pad(f, (0, (-f.shape[0]) % PACK)))
    return jnp.concatenate(pieces).reshape(-1, 128)


def _unpack(flat, shapes):
    out, off = [], 0
    f = flat.reshape(-1)
    for s in shapes:
        size = 1
        for d in s:
            size *= d
        out.append(f[off:off + size].reshape(s))
        off += size + (-size) % PACK
    return out


def kernel(x, mix_pre_g, mix_post_g, ffn_pre_g, ffn_post_g, cm_w_in, cm_b_in, cm_dw, cm_dw_b, cm_ln_g, cm_ln_b, cm_w_out, cm_b_out, kv_norm_g, w_kv, w_q, w_o, ffn_w_in, ffn_dw, ffn_dw_b, ffn_w_out, loss_target, m_mix_pre_g, m_mix_post_g, m_ffn_pre_g, m_ffn_post_g, m_cm_w_in, m_cm_b_in, m_cm_dw, m_cm_dw_b, m_cm_ln_g, m_cm_ln_b, m_cm_w_out, m_cm_b_out, m_kv_norm_g, m_w_kv, m_w_q, m_w_o, m_ffn_w_in, m_ffn_dw, m_ffn_dw_b, m_ffn_w_out, v_mix_pre_g, v_mix_post_g, v_ffn_pre_g, v_ffn_post_g, v_cm_w_in, v_cm_b_in, v_cm_dw, v_cm_dw_b, v_cm_ln_g, v_cm_ln_b, v_cm_w_out, v_cm_b_out, v_kv_norm_g, v_w_kv, v_w_q, v_w_o, v_ffn_w_in, v_ffn_dw, v_ffn_dw_b, v_ffn_w_out):
    nb, seq, D = x.shape
    T = nb * seq
    me = _my_index()
    n_b = DEPTH - N_A

    nf = ffn_w_in.shape[-1]
    t_ = lambda t: jnp.swapaxes(t, 1, 2)
    fin_t, m_fin_t, v_fin_t = t_(ffn_w_in), t_(m_ffn_w_in), t_(v_ffn_w_in)

    stages = [(part, i) for i in range(DEPTH) for part in ("mix", "ffn")]
    stages.insert(stages.index(("ffn", N_A - 1)) + 1, ("kv", N_A - 1))

    def stage_sources(stage):
        part, i = stage
        if part == "ffn":
            src = {"fin": fin_t[i], "fout": ffn_w_out[i]}
        elif part == "kv":
            src = {"kv": w_kv}
        elif i < N_A:
            src = {"cin": cm_w_in[i], "cout": cm_w_out[i]}
        else:
            src = {"q": w_q[i - N_A], "o": w_o[i - N_A]}
        return {k: t.astype(MM) for k, t in src.items()}

    def begin_gather(stage, after):
        src = stage_sources(stage)
        names, arrays = list(src), list(src.values())
        tag = f"{stage[0]}{stage[1]}"
        lands = place_own(f"gather_own_{tag}", "gather", arrays)
        handle, token = exchange_begin(f"gather_begin_{tag}", "gather", arrays, lands, after)
        return (names, handle), token

    def end_gather(stage, pending, after):
        names, handle = pending
        W = dict(zip(names, exchange_end(f"gather_end_{stage[0]}{stage[1]}", handle, after)))
        for k in W:
            if k in ("cout", "o"):
                W[k] = W[k].reshape(1, 1, D, D)
            elif k == "fout":
                W[k] = W[k].reshape(1, 4, nf, D)
            else:
                W[k] = W[k][:, None]
        return W

    small = [cm_b_in[:, None, :], cm_dw, cm_dw_b[:, None, :], cm_ln_g, cm_ln_b, cm_b_out, ffn_dw]
    Bcin, DWc, DWBc, LNg, LNb, Bcout, DWf = all_gather("gather_small", small, [False] * len(small))
    LNg = jnp.swapaxes(LNg, 0, 1).reshape(N_A, D)
    LNb = jnp.swapaxes(LNb, 0, 1).reshape(N_A, D)
    Bcout = jnp.swapaxes(Bcout, 0, 1).reshape(N_A, D)
    DWBf = ffn_dw_b.reshape(DEPTH, N_DEV, 1, nf)
    zero_bias = jnp.zeros((D,), F32)

    xs = x.reshape(T, D)
    sv = []
    kv = hkv = None
    pending, _ = begin_gather(stages[0], DWf)
    Ws = {stages[0]: end_gather(stages[0], pending, xs)}
    sv = [{} for _ in range(DEPTH)]
    for idx, stage in enumerate(stages):
        part, i = stage
        L, W = sv[i], Ws[stage]
        gain = {"mix": mix_pre_g[i], "ffn": ffn_pre_g[i], "kv": kv_norm_g}[part]
        following = stages[idx + 1] if idx + 1 < len(stages) else None
        if following is not None:
            pending, token = begin_gather(following, next(iter(W.values())))
            gain = gain + token[0, 0]
        if part == "mix":
            L["x_in"] = xs
            if i < N_A:
                L["u"], L["h"] = norm_mm(f"cm_in_{i}", xs, gain, W["cin"], 0, Bcin[:, i:i + 1])
                L["c"] = cm_glu_conv(f"cm_conv_{i}", L["u"], DWc, DWBc, i, seq)
                L["s"] = ln_silu(f"cm_ln_{i}", L["c"], LNg[i], LNb[i])
                L["y"], xs = mm_resid_norm(f"cm_out_{i}", L["s"], W["cout"], 0, Bcout[i], xs, mix_post_g[i])
            else:
                L["q"], L["h"] = norm_mm(f"attn_q_{i}", xs, gain, W["q"], 0)
                L["mg"], L["lse"] = attn_fwd(f"attn_{i}", L["q"], kv, nb, seq)
                L["y"], xs = mm_resid_norm(f"attn_out_{i}", L["mg"], W["o"], 0, zero_bias, xs, mix_post_g[i])
            L["x1"] = xs
        elif part == "ffn":
            L["uf"], L["hf"], L["z"], L["ag"] = ffn_in_conv(f"ffn_in_{i}", xs, gain, W["fin"], DWf, DWBf, i, seq)
            L["yf"], xs = mm_resid_norm(f"ffn_out_{i}", L["z"], W["fout"], 0, zero_bias, xs, ffn_post_g[i])
        else:
            kv, hkv = norm_mm("kv_proj", xs, gain, W["kv"], 0)
        if following is not None:
            Ws[following] = end_gather(following, pending, kv if part == "kv" else xs)
    dx, loss_part = loss_fwd_bwd("loss", xs, loss_target.reshape(T, D))

    g_mix_pre, g_mix_post, g_ffn_pre, g_ffn_post = [None] * DEPTH, [None] * DEPTH, [None] * DEPTH, [None] * DEPTH
    g_ffn_dw, g_ffn_dwb = [None] * DEPTH, [None] * DEPTH
    g_cbin, g_cdw, g_cdwb, g_lng, g_lnb, g_cbout = ([None] * N_A for _ in range(6))
    g_kvn = dkv = None
    landed = [{} for _ in range(DEPTH)]
    in_flight = token = None
    for stage in reversed(stages):
        part, i = stage
        L, W = sv[i], Ws[stage]
        gain = {"mix": mix_post_g[i], "ffn": ffn_post_g[i], "kv": kv_norm_g}[part]
        if token is not None:
            gain = gain + token[0, 0]
        send = {}
        if part == "kv":
            send["kv"] = mm_tn("kv_wg", hkv[None], dkv)
            dx, g_kvn = mm_nt_norm_bwd("kv_bwd", dkv, W["kv"], 0, sv[i + 1]["x_in"], gain, dx)
        elif part == "ffn":
            dyf, g_ffn_post[i], _ = resid_norm_bwd(f"ffn_post_bwd_{i}", dx, L["yf"], gain)
            send["fout"] = mm_tn(f"ffn_out_wg_{i}", L["z"], dyf).reshape(N_DEV, nf // 2, D)
            duf, ddw, ddwb, dx, g_ffn_pre[i] = ffn_bwd(f"ffn_bwd_{i}", dyf, W["fout"], L["uf"], L["ag"], DWf, W["fin"],
                                                       L["x1"], ffn_pre_g[i], dx, i, seq)
            g_ffn_dw[i], g_ffn_dwb[i] = ddw.reshape(N_DEV, FFN_CONV_W, nf), ddwb.reshape(-1)
            send["fin"] = mm_tn(f"ffn_in_wg_{i}", duf, L["hf"][None])
        else:
            dy, g_mix_post[i], dyb = resid_norm_bwd(f"mix_post_bwd_{i}", dx, L["y"], gain)
            if i >= N_A:
                dm = mm_nt(f"attn_out_bwd_{i}", dy, W["o"], 0, F32)
                send["o"] = mm_tn(f"attn_out_wg_{i}", L["mg"], dy).reshape(N_DEV, D // N_DEV, D)
                dq, dkv = attn_bwd(f"attn_bwd_{i}", L["q"], kv, dm, L["mg"], L["lse"], dkv, nb, seq)
                send["q"] = mm_tn(f"attn_q_wg_{i}", L["h"][None], dq)
                dx, g_mix_pre[i] = mm_nt_norm_bwd(f"attn_q_bwd_{i}", dq, W["q"], 0, L["x_in"], mix_pre_g[i], dx)
            else:
                g_cbout[i] = dyb
                ds = mm_nt(f"cm_out_bwd_{i}", dy, W["cout"], 0, F32)
                send["cout"] = mm_tn(f"cm_out_wg_{i}", L["s"], dy).reshape(N_DEV, D // N_DEV, D)
                dc, g_lng[i], g_lnb[i] = ln_silu_bwd(f"cm_ln_bwd_{i}", ds, L["c"], LNg[i], LNb[i])
                du, g_cdw[i], g_cdwb[i], dbi = cm_glu_conv_bwd(f"cm_conv_bwd_{i}", L["u"], dc, DWc, i, seq)
                g_cbin[i] = dbi.reshape(N_DEV, -1)
                send["cin"] = mm_tn(f"cm_in_wg_{i}", L["h"][None], du)
                dx, g_mix_pre[i] = mm_nt_norm_bwd(f"cm_in_bwd_{i}", du, W["cin"], 0, L["x_in"], mix_pre_g[i], dx)
        if in_flight is not None:
            (p, j), names, handle = in_flight
            landed[j].update(zip(names, exchange_end(f"scatter_end_{p}{j}", handle, dx)))
        names, arrays = list(send), list(send.values())
        lands = place_own(f"scatter_own_{part}{i}", "scatter", arrays)
        handle, token = exchange_begin(f"scatter_begin_{part}{i}", "scatter", arrays, lands, dx)
        in_flight = (stage, names, handle)
    grad_x = dx.reshape(nb, seq, D)

    rep_names = ["mix_pre_g", "mix_post_g", "ffn_pre_g", "ffn_post_g", "kv_norm_g", "ffn_dw_b"]
    rep_parts = [jnp.concatenate(g_mix_pre), jnp.concatenate(g_mix_post), jnp.concatenate(g_ffn_pre),
                 jnp.concatenate(g_ffn_post), g_kvn.reshape(-1), jnp.stack(g_ffn_dwb)]
    rep_w = [mix_pre_g, mix_post_g, ffn_pre_g, ffn_post_g, kv_norm_g, ffn_dw_b]
    rep_m = [m_mix_pre_g, m_mix_post_g, m_ffn_pre_g, m_ffn_post_g, m_kv_norm_g, m_ffn_dw_b]
    rep_v = [v_mix_pre_g, v_mix_post_g, v_ffn_pre_g, v_ffn_post_g, v_kv_norm_g, v_ffn_dw_b]
    sh_names = ["ffn_dw", "cm_b_in", "cm_dw", "cm_dw_b", "cm_ln_g", "cm_ln_b", "cm_b_out"]
    own = lambda per_layer, shard: jnp.stack([p.reshape((N_DEV,) + shard) for p in per_layer], axis=1)
    sh_parts = [own(g_ffn_dw, ffn_dw.shape[1:]), own(g_cbin, cm_b_in.shape[1:]), own(g_cdw, cm_dw.shape[1:]),
                own(g_cdwb, cm_dw_b.shape[1:]), own(g_lng, cm_ln_g.shape[1:]), own(g_lnb, cm_ln_b.shape[1:]),
                own(g_cbout, cm_b_out.shape[1:])]
    sh_w = [ffn_dw, cm_b_in, cm_dw, cm_dw_b, cm_ln_g, cm_ln_b, cm_b_out]
    sh_m = [m_ffn_dw, m_cm_b_in, m_cm_dw, m_cm_dw_b, m_cm_ln_g, m_cm_ln_b, m_cm_b_out]
    sh_v = [v_ffn_dw, v_cm_b_in, v_cm_dw, v_cm_dw_b, v_cm_ln_g, v_cm_ln_b, v_cm_b_out]
    rep_pack = _pack([loss_part] + rep_parts)
    sh_pack = jnp.stack([_pack([p[k] for p in sh_parts]) for k in range(N_DEV)])
    n_rep = rep_pack.shape[0]
    small = []
    for kind, pack in (("gather", rep_pack), ("scatter", sh_pack)):
        lands = place_own(f"{kind}_own_small", kind, [pack])
        handle, token = exchange_begin(f"{kind}_begin_small", kind, [pack], lands, token)
        small.append((kind, handle))

    def big_update(name, w, m, v, key, layers, after):
        as3 = lambda t: t.reshape((-1,) + t.shape[-2:])
        outs = adamw_sum(name, as3(w), as3(m), as3(v), [landed[i][key] for i in layers], after)
        return [t.reshape(w.shape) for t in outs]

    conf, attn = range(N_A), range(N_A, DEPTH)
    upd = {}
    upd["ffn_w_in"] = [t_(t) for t in big_update("adam_ffn_w_in", fin_t, m_fin_t, v_fin_t, "fin", range(DEPTH), token)]
    upd["ffn_w_out"] = big_update("adam_ffn_w_out", ffn_w_out, m_ffn_w_out, v_ffn_w_out, "fout", range(DEPTH), token)
    upd["w_kv"] = big_update("adam_w_kv", w_kv, m_w_kv, v_w_kv, "kv", [N_A - 1], token)
    upd["w_q"] = big_update("adam_w_q", w_q, m_w_q, v_w_q, "q", attn, token)
    upd["w_o"] = big_update("adam_w_o", w_o, m_w_o, v_w_o, "o", attn, upd["w_q"][0])
    (p, j), names, handle = in_flight
    landed[j].update(zip(names, exchange_end(f"scatter_end_{p}{j}", handle, upd["w_o"][0])))
    upd["cm_w_in"] = big_update("adam_cm_w_in", cm_w_in, m_cm_w_in, v_cm_w_in, "cin", conf, token)
    upd["cm_w_out"] = big_update("adam_cm_w_out", cm_w_out, m_cm_w_out, v_cm_w_out, "cout", conf, upd["cm_w_in"][0])
    (rep_landed,), (sh_landed,) = (exchange_end(f"{kind}_end_small", handle, upd["cm_w_out"][0])
                                   for kind, handle in small)
    rep_sum = sum_partials("sum_small_rep", rep_landed)
    sh_sum = sum_partials("sum_small_sh", sh_landed)
    rep_shapes = [(1, 1)] + [w.shape for w in rep_w]
    sh_shapes = [w.shape for w in sh_w]
    g_small = jnp.concatenate([rep_sum, sh_sum])
    pad1 = jnp.zeros((1, 1), F32)
    d_s, m_s, v_s = adamw_small("adam_small", jnp.concatenate([_pack([pad1] + rep_w), _pack(sh_w)]), g_small,
                                jnp.concatenate([_pack([pad1] + rep_m), _pack(sh_m)]),
                                jnp.concatenate([_pack([pad1] + rep_v), _pack(sh_v)]))
    split = lambda t: (_unpack(t[:n_rep], rep_shapes), _unpack(t[n_rep:], sh_shapes))
    for (rep_t, sh_t), slot in zip([split(g_small), split(d_s), split(m_s), split(v_s)], range(4)):
        if slot == 0:
            loss = rep_t[0].reshape(())
        for name, t in zip(rep_names, rep_t[1:]):
            upd.setdefault(name, [None] * 4)[slot] = t
        for name, t in zip(sh_names, sh_t):
            upd.setdefault(name, [None] * 4)[slot] = t

    order = ["mix_pre_g", "mix_post_g", "ffn_pre_g", "ffn_post_g", "cm_w_in", "cm_b_in", "cm_dw", "cm_dw_b", "cm_ln_g",
             "cm_ln_b", "cm_w_out", "cm_b_out", "kv_norm_g", "w_kv", "w_q", "w_o", "ffn_w_in", "ffn_dw", "ffn_dw_b",
             "ffn_w_out"]
    return (loss, grad_x, *[upd[n][0] for n in order], *[upd[n][1] for n in order],
            *[upd[n][2] for n in order], *[upd[n][3] for n in order])
```

```python
import functools

import jax
import jax.numpy as jnp
from jax import lax
from jax.experimental import pallas as pl
from jax.experimental.pallas import tpu as pltpu

N_DEV = 8
N_A = 2
DEPTH = 4
N_HEADS = 8
N_GROUPS = 3
DILATIONS = (1, 4, 16)
ATT_BLOCK = 128
ATT_BATCH = 8
CONV_W = 31
FFN_CONV_W = 3
CONV_HALO = 32
FFN_HALO = 16
ROW_CHUNK_FWD = 16
ROW_CHUNK_BWD = 16
DGRAD_GROUP_ALL_MAX_N = 384
FFN_COL_TILE = 1024
EPS = 1e-6
NEG = -1e30
ADAM_LR, ADAM_B1, ADAM_B2, ADAM_EPS, ADAM_WD, ADAM_STEP = 0.001, 0.9, 0.999, 1e-08, 0.01, 10
MM = jnp.bfloat16
F32 = jnp.float32
VMEM_LIMIT_BYTES = 56 * 1024 * 1024
PACK = 1024
MESH_ID = pl.DeviceIdType.MESH

_pallas = pl.pallas_call


def _call(body, *, name, out_shape, grid=(), in_specs=None, out_specs=None, scratch=()):
    return _pallas(body, name=name, out_shape=out_shape, grid=grid, in_specs=in_specs, out_specs=out_specs,
                   scratch_shapes=list(scratch),
                   compiler_params=pltpu.CompilerParams(vmem_limit_bytes=VMEM_LIMIT_BYTES))


def _tile(n, pref):
    if n <= pref:
        return n
    t = pref - pref % 8
    while n % t:
        t -= 8
    assert t > 0, (n, pref)
    return t


def _sds(shape, dtype):
    return jax.ShapeDtypeStruct(tuple(shape), dtype)


def _dot(a, b):
    return jnp.dot(a, b, preferred_element_type=F32)


def _dot_nt(a, b):
    return lax.dot_general(a, b, (((1,), (1,)), ((), ())), preferred_element_type=F32)


def _dot_tn(a, b):
    return lax.dot_general(a, b, (((0,), (0,)), ((), ())), preferred_element_type=F32)


def _sigmoid(x):
    return 0.5 * jnp.tanh(0.5 * x) + 0.5


def _my_index():
    return 4 * lax.axis_index("x") + 2 * lax.axis_index("y") + lax.axis_index("c")


def _exchange(name, arrays, out_shapes, pieces, src_of, dst_of):
    n = len(arrays)
    base = [sum(pieces[:a]) for a in range(n)]
    total = sum(pieces)

    def body(*refs):
        ins, outs = refs[:n], refs[n:2 * n]
        send_sems, recv_sems, local_sems = refs[2 * n:]
        x, y, c = lax.axis_index("x"), lax.axis_index("y"), lax.axis_index("c")
        me = 4 * x + 2 * y + c
        copies = []
        for a in range(n):
            for k, (s, d) in enumerate(zip(src_of(a, ins[a], me), dst_of(a, outs[a], me))):
                cp = pltpu.make_async_copy(s, d, local_sems.at[base[a] + k])
                cp.start()
                copies.append(cp)
        remote = []
        for m in range(1, N_DEV):
            px, py, pc = x ^ (m >> 2), y ^ ((m >> 1) & 1), c ^ (m & 1)
            peer = 4 * px + 2 * py + pc
            for a in range(n):
                for k, (s, d) in enumerate(zip(src_of(a, ins[a], peer), dst_of(a, outs[a], me))):
                    cp = pltpu.make_async_remote_copy(src_ref=s, dst_ref=d, send_sem=send_sems.at[base[a] + k, m - 1],
                                                      recv_sem=recv_sems.at[base[a] + k, m - 1],
                                                      device_id=(px, py, pc), device_id_type=MESH_ID)
                    cp.start()
                    remote.append(cp)
        for cp in copies:
            cp.wait()
        for cp in remote:
            cp.wait_send()
        for m in range(1, N_DEV):
            px, py, pc = x ^ (m >> 2), y ^ ((m >> 1) & 1), c ^ (m & 1)
            peer = 4 * px + 2 * py + pc
            for a in range(n):
                for k, (s, d) in enumerate(zip(src_of(a, ins[a], me), dst_of(a, outs[a], peer))):
                    pltpu.make_async_remote_copy(src_ref=s, dst_ref=d, send_sem=send_sems.at[base[a] + k, m - 1],
                                                 recv_sem=recv_sems.at[base[a] + k, m - 1], device_id=(px, py, pc),
                                                 device_id_type=MESH_ID).wait_recv()

    any_spec = pl.BlockSpec(memory_space=pl.ANY)
    return _call(body, name=name, out_shape=[_sds(s, a.dtype) for s, a in zip(out_shapes, arrays)],
                 in_specs=[any_spec] * n, out_specs=[any_spec] * n,
                 scratch=[pltpu.SemaphoreType.DMA((total, N_DEV - 1)), pltpu.SemaphoreType.DMA((total, N_DEV - 1)),
                          pltpu.SemaphoreType.DMA((total,))])(*arrays)


def all_gather(name, arrays, row_sharded):
    def out_shape(a):
        s = arrays[a].shape
        return (s[0], N_DEV) + s[1:] if row_sharded[a] else (N_DEV,) + s

    def src_of(a, ref, peer):
        if row_sharded[a]:
            return [ref.at[l] for l in range(arrays[a].shape[0])]
        return [ref]

    def dst_of(a, ref, me):
        if row_sharded[a]:
            return [ref.at[l, me] for l in range(arrays[a].shape[0])]
        return [ref.at[me]]

    pieces = [arrays[a].shape[0] if row_sharded[a] else 1 for a in range(len(arrays))]
    return _exchange(name, arrays, [out_shape(a) for a in range(len(arrays))], pieces, src_of, dst_of)


def _src_view(kind, ref, peer):
    return ref if kind == "gather" else ref.at[peer]


def _peers(x, y, c):
    for m in range(1, N_DEV):
        px, py, pc = x ^ (m >> 2), y ^ ((m >> 1) & 1), c ^ (m & 1)
        yield m - 1, (px, py, pc), 4 * px + 2 * py + pc


def place_own(name, kind, srcs):
    n = len(srcs)
    shapes = [(N_DEV,) + s.shape if kind == "gather" else s.shape for s in srcs]
    steps = 2 if all(s.shape[-2] % 32 == 0 for s in srcs) else 1

    def body(*refs):
        for a in range(n):
            refs[n + a][...] = refs[a][...]

    def spec(shape, own_block):
        R, C = shape[-2:]
        tr = R // steps
        if own_block:
            return pl.BlockSpec((None, tr, C), lambda i: (_my_index(), i, 0))
        return pl.BlockSpec((tr, C), lambda i: (i, 0))

    return _call(body, name=name, grid=(steps,), out_shape=[_sds(s, a.dtype) for s, a in zip(shapes, srcs)],
                 in_specs=[spec(s.shape, kind == "scatter") for s in srcs],
                 out_specs=[spec(s, True) for s in shapes])(*srcs)


_HBM_SPEC = pl.BlockSpec(memory_space=pltpu.HBM)
_SEM_SPEC = pl.BlockSpec(memory_space=pltpu.SEMAPHORE)
_DATAFLOW = pltpu.SideEffectType.DATAFLOW_SIDE_EFFECTING


def _remote(kind, src, land, send_sems, recv_sems, a, slot, frm, to_id, at):
    return pltpu.make_async_remote_copy(src_ref=_src_view(kind, src, frm), dst_ref=land.at[at],
                                        send_sem=send_sems.at[a * (N_DEV - 1) + slot],
                                        recv_sem=recv_sems.at[a * (N_DEV - 1) + slot],
                                        device_id=to_id, device_id_type=MESH_ID)


def exchange_begin(name, kind, srcs, lands, after):
    n = len(srcs)

    def body(*refs):
        ins, lnd = refs[:n], refs[n:2 * n]
        send_sems, recv_sems = refs[2 * n + 1], refs[2 * n + 2]
        token = refs[-1]
        x, y, c = lax.axis_index("x"), lax.axis_index("y"), lax.axis_index("c")
        me = 4 * x + 2 * y + c
        for slot, peer_id, peer in _peers(x, y, c):
            for a in range(n):
                _remote(kind, ins[a], lnd[a], send_sems, recv_sems, a, slot, peer, peer_id, me).start()
        token[...] = jnp.zeros_like(token)

    hbm = lambda t: pltpu.HBM(t.shape, t.dtype)
    outs = _pallas(
        body, name=name,
        out_shape=(pltpu.SemaphoreType.DMA((n * (N_DEV - 1),)), pltpu.SemaphoreType.DMA((n * (N_DEV - 1),)),
                   *[hbm(t) for t in srcs], *[hbm(t) for t in lands], _sds((8, 128), F32)),
        in_specs=[_HBM_SPEC] * (2 * n) + [pl.BlockSpec(memory_space=pl.ANY)],
        out_specs=(_SEM_SPEC, _SEM_SPEC, *[_HBM_SPEC] * (2 * n), pl.BlockSpec(memory_space=pltpu.VMEM)),
        input_output_aliases={i: 2 + i for i in range(2 * n)},
        compiler_params=pltpu.CompilerParams(has_side_effects=_DATAFLOW),
    )(*[pltpu.with_memory_space_constraint(t, pltpu.HBM) for t in list(srcs) + list(lands)], after)
    return (kind, outs[0], outs[1], list(outs[2:2 + n]), list(outs[2 + n:2 + 2 * n])), outs[-1]


def exchange_end(name, handle, after):
    kind, send_sems, recv_sems, srcs, lands = handle
    n = len(srcs)

    def body(*refs):
        ins, lnd = refs[:n], refs[n:2 * n]
        s_sems, r_sems = refs[2 * n], refs[2 * n + 1]
        x, y, c = lax.axis_index("x"), lax.axis_index("y"), lax.axis_index("c")
        me = 4 * x + 2 * y + c
        for slot, peer_id, peer in _peers(x, y, c):
            for a in range(n):
                _remote(kind, ins[a], lnd[a], s_sems, r_sems, a, slot, peer, peer_id, me).wait_send()
        for slot, peer_id, peer in _peers(x, y, c):
            for a in range(n):
                _remote(kind, ins[a], lnd[a], s_sems, r_sems, a, slot, me, peer_id, peer).wait_recv()

    hbm = lambda t: pltpu.HBM(t.shape, t.dtype)
    outs = _pallas(
        body, name=name, out_shape=tuple(hbm(t) for t in srcs + lands),
        in_specs=[_HBM_SPEC] * (2 * n) + [_SEM_SPEC, _SEM_SPEC, pl.BlockSpec(memory_space=pl.ANY)],
        out_specs=tuple([_HBM_SPEC] * (2 * n)), input_output_aliases={i: i for i in range(2 * n)},
        compiler_params=pltpu.CompilerParams(has_side_effects=_DATAFLOW),
    )(*srcs, *lands, send_sems, recv_sems, after)
    return list(outs[n:])


def norm_mm(name, x, gain, w, layer, bias=None, w_t=False):
    T, D = x.shape
    nsh = w.shape[0]
    n = w.shape[2] if w_t else w.shape[3]
    tm = _tile(T, 2048)

    def body(*refs):
        if bias is None:
            x_ref, g_ref, w_ref, u_ref, h_ref = refs
        else:
            x_ref, g_ref, w_ref, b_ref, u_ref, h_ref = refs

        @pl.when(pl.program_id(1) == 0)
        def _():
            xf = x_ref[...]
            r = lax.rsqrt(jnp.mean(xf * xf, axis=-1, keepdims=True) + EPS)
            h_ref[...] = (xf * r * g_ref[...]).astype(h_ref.dtype)

        acc = (_dot_nt if w_t else _dot)(h_ref[...], w_ref[...])
        if bias is not None:
            acc = acc + b_ref[...]
        u_ref[...] = acc.astype(u_ref.dtype)

    in_specs = [pl.BlockSpec((tm, D), lambda i, j: (i, 0)),
                pl.BlockSpec((1, D), lambda i, j: (0, 0)),
                pl.BlockSpec((None, None) + w.shape[2:], lambda i, j: (j, layer, 0, 0))]
    args = [x, gain.reshape(1, D), w]
    if bias is not None:
        in_specs.append(pl.BlockSpec((None, None, 1, n), lambda i, j: (j, layer, 0, 0)))
        args.append(bias)
    return _call(body, name=name, grid=(T // tm, nsh), in_specs=in_specs,
                 out_specs=[pl.BlockSpec((None, tm, n), lambda i, j: (j, i, 0)),
                            pl.BlockSpec((tm, D), lambda i, j: (i, 0))],
                 out_shape=[_sds((nsh, T, n), MM), _sds((T, D), MM)])(*args)


def mm_resid_norm(name, a, w, layer, bias, x, gain):
    nk, T, kk = a.shape
    D = x.shape[1]
    tm = _tile(T, 512)

    def body(a_ref, w_ref, b_ref, x_ref, g_ref, y_ref, xn_ref):
        y = _dot(a_ref[0], w_ref[0])
        for q in range(1, nk):
            y = y + _dot(a_ref[q], w_ref[q])
        y = y + b_ref[...]
        y_ref[...] = y
        r = lax.rsqrt(jnp.mean(y * y, axis=-1, keepdims=True) + EPS)
        xn_ref[...] = x_ref[...] + y * r * g_ref[...]

    return _call(body, name=name, grid=(T // tm,),
                 in_specs=[pl.BlockSpec((nk, tm, kk), lambda i: (0, i, 0)),
                           pl.BlockSpec((None, nk, kk, D), lambda i: (layer, 0, 0, 0)),
                           pl.BlockSpec((1, D), lambda i: (0, 0)),
                           pl.BlockSpec((tm, D), lambda i: (i, 0)),
                           pl.BlockSpec((1, D), lambda i: (0, 0))],
                 out_specs=[pl.BlockSpec((tm, D), lambda i: (i, 0))] * 2,
                 out_shape=[_sds((T, D), F32)] * 2)(a, w, bias.reshape(1, D), x, gain.reshape(1, D))


def _halo_maps(tm, hb, T):
    per = tm // hb
    last = T // hb - 1
    return (lambda i: jnp.maximum(i * per - 1, 0)), (lambda i: jnp.minimum((i + 1) * per, last))


def cm_glu_conv(name, u, dw, dwb, layer, seq):
    _, T, n = u.shape
    ct = dw.shape[-1]
    per = n // ct
    nct = 4 * per
    tm = _tile(seq, 512)
    tps = seq // tm
    hb = CONV_HALO
    prev, _ = _halo_maps(tm, hb, T)
    u4 = u.reshape(2, 4, T, n)

    rows = _tile(tm, 2 * ROW_CHUNK_FWD)

    def body(u_ref, uh_ref, w_ref, b_ref, o_ref, z_ref):
        first = (pl.program_id(0) % tps) == 0
        um = u_ref[...].astype(F32)
        uh = uh_ref[...].astype(F32)
        z_ref[0, pl.ds(hb, tm), :] = um[0] * _sigmoid(um[1])
        z_ref[0, pl.ds(0, hb), :] = jnp.where(first, 0.0, uh[0] * _sigmoid(uh[1]))
        _stage_shifts(z_ref, hb + tm, back=True)
        for r0 in range(0, tm, rows):
            acc = b_ref[...]
            for j in range(CONV_W):
                acc = acc + w_ref[pl.ds(CONV_W - 1 - j, 1), :] * z_ref[j % 8, pl.ds(hb + r0 - (j - j % 8), rows), :]
            o_ref[pl.ds(r0, rows), :] = acc

    return _call(body, name=name, grid=(T // tm, nct),
                 in_specs=[pl.BlockSpec((2, None, tm, ct), lambda i, c: (0, c // per, i, c % per)),
                           pl.BlockSpec((2, None, hb, ct), lambda i, c: (0, c // per, prev(i), c % per)),
                           pl.BlockSpec((None, None, CONV_W, ct), lambda i, c: (c, layer, 0, 0)),
                           pl.BlockSpec((None, None, 1, ct), lambda i, c: (c, layer, 0, 0))],
                 out_specs=pl.BlockSpec((tm, ct), lambda i, c: (i, c)),
                 out_shape=_sds((T, nct * ct), F32),
                 scratch=[pltpu.VMEM((8, tm + hb, ct), F32)])(u4, u4, dw, dwb)


def _stage_shifts(z_ref, n_rows, back):
    for r in range(1, 8):
        if back:
            z_ref[r, pl.ds(8, n_rows - 8), :] = z_ref[0, pl.ds(8 - r, n_rows - 8), :]
        else:
            z_ref[r, pl.ds(0, n_rows - 8), :] = z_ref[0, pl.ds(r, n_rows - 8), :]


def ln_silu(name, c, g, b):
    T, D = c.shape
    tm = _tile(T, 512)

    def body(c_ref, g_ref, b_ref, s_ref):
        cf = c_ref[...]
        mu = jnp.mean(cf, axis=-1, keepdims=True)
        xc = cf - mu
        r = lax.rsqrt(jnp.mean(xc * xc, axis=-1, keepdims=True) + EPS)
        t = xc * r * g_ref[...] + b_ref[...]
        s_ref[...] = (t * _sigmoid(t)).astype(s_ref.dtype)

    return _call(body, name=name, grid=(T // tm,),
                 in_specs=[pl.BlockSpec((tm, D), lambda i: (i, 0)), pl.BlockSpec((1, D), lambda i: (0, 0)),
                           pl.BlockSpec((1, D), lambda i: (0, 0))],
                 out_specs=pl.BlockSpec((None, tm, D), lambda i: (0, i, 0)),
                 out_shape=_sds((1, T, D), MM))(c, g.reshape(1, D), b.reshape(1, D))


def _col_tiles(n, width):
    return [(c0, min(width, n - c0)) for c0 in range(0, n, width)]


def ffn_in_conv(name, x, gain, wt, dw, dwb, layer, seq):
    T, D = x.shape
    n = wt.shape[2]
    tm = _tile(seq, 1024)
    tps = seq // tm
    hb = 8
    rows = _tile(tm, ROW_CHUNK_FWD)
    K = FFN_CONV_W
    w5 = wt.reshape(2, 4, n, D)
    dw4 = dw.reshape(2, 4, dw.shape[1], K, n)
    dwb4 = dwb.reshape(dwb.shape[0], 2, 4, 1, n)

    def body(x_ref, g_ref, w_ref, cw_ref, cb_ref, u_ref, h_ref, z_ref, ag_ref, pad_ref, carry_ref):
        i, j = pl.program_id(0), pl.program_id(1)
        first = (i % tps) == 0

        @pl.when(j == 0)
        def _():
            xf = x_ref[...]
            r = lax.rsqrt(jnp.mean(xf * xf, axis=-1, keepdims=True) + EPS)
            h_ref[...] = (xf * r * g_ref[...]).astype(h_ref.dtype)

        h = h_ref[...]
        for c0, wc in _col_tiles(n, FFN_COL_TILE):
            cols = pl.ds(c0, wc)
            for half in range(2):
                res = _dot_nt(h, w_ref[half, cols, :])
                u_ref[half, :, cols] = res.astype(u_ref.dtype)
                pad_ref[half, pl.ds(hb, tm), cols] = res
                pad_ref[half, pl.ds(0, hb), cols] = jnp.where(first, 0.0, carry_ref[j, half, :, cols])
            for r0 in range(0, tm, rows):
                conv = []
                for half in range(2):
                    acc = cb_ref[half, :, cols] + cw_ref[half, pl.ds(K - 1, 1), cols] * pad_ref[
                        half, pl.ds(hb + r0, rows), cols]
                    for k in range(K - 1):
                        acc = acc + cw_ref[half, pl.ds(k, 1), cols] * pad_ref[
                            half, pl.ds(hb + r0 - (K - 1) + k, rows), cols]
                    conv.append(acc)
                a, g = conv
                sg = _sigmoid(g)
                silu = g * sg
                z_ref[pl.ds(r0, rows), cols] = (silu * a).astype(z_ref.dtype)
                ag_ref[0, pl.ds(r0, rows), cols] = silu.astype(ag_ref.dtype)
                ag_ref[1, pl.ds(r0, rows), cols] = (a * (sg * (1.0 + g * (1.0 - sg)))).astype(ag_ref.dtype)
            for half in range(2):
                carry_ref[j, half, :, cols] = pad_ref[half, pl.ds(tm, hb), cols]

    u, h, z, ag = _call(
        body, name=name, grid=(T // tm, 4),
        in_specs=[pl.BlockSpec((tm, D), lambda i, j: (i, 0)),
                  pl.BlockSpec((1, D), lambda i, j: (0, 0)),
                  pl.BlockSpec((2, None, n, D), lambda i, j: (0, j, 0, 0)),
                  pl.BlockSpec((2, None, None, K, n), lambda i, j: (0, j, layer, 0, 0)),
                  pl.BlockSpec((None, 2, None, 1, n), lambda i, j: (layer, 0, j, 0, 0))],
        out_specs=[pl.BlockSpec((2, None, tm, n), lambda i, j: (0, j, i, 0)),
                   pl.BlockSpec((tm, D), lambda i, j: (i, 0)),
                   pl.BlockSpec((None, tm, n), lambda i, j: (j, i, 0)),
                   pl.BlockSpec((2, None, tm, n), lambda i, j: (0, j, i, 0))],
        out_shape=[_sds((2, 4, T, n), MM), _sds((T, D), MM), _sds((4, T, n), MM), _sds((2, 4, T, n), MM)],
        scratch=[pltpu.VMEM((2, tm + hb, n), F32), pltpu.VMEM((4, 2, hb, n), F32)])(
            x, gain.reshape(1, D), w5, dw4, dwb4)
    return u.reshape(8, T, n), h, z, ag


def _head_specs(seq, dh, q_heads, kv_heads):
    def spec(per, base):
        def imap(b, h, g):
            f = base + g * N_HEADS + h
            return (f // per, b, f % per)
        return pl.BlockSpec((None, seq, dh), imap)
    return spec(q_heads, 0), spec(kv_heads, 0), spec(kv_heads, N_GROUPS * N_HEADS)


def _rows(start, d, blocks=1):
    size = blocks * ATT_BLOCK
    return pl.ds(start, size, stride=d) if d > 1 else pl.ds(start, size)


def _att_pad():
    return max(ATT_BLOCK * d for d in DILATIONS[:-1])


def _band_mask(first, nblk):
    keys = ATT_BLOCK if nblk == 1 else 2 * ATT_BLOCK
    shape = (ATT_BATCH, ATT_BLOCK, keys)
    qi = lax.broadcasted_iota(jnp.int32, shape, 1)
    kj = lax.broadcasted_iota(jnp.int32, shape, 2)
    if nblk == 1:
        return kj <= qi
    n = (first + lax.broadcasted_iota(jnp.int32, shape, 0)) % nblk
    return (kj >= qi) & (kj <= qi + ATT_BLOCK) & ((n > 0) | (kj >= ATT_BLOCK))


def _block_rows(idx, d, nblk):
    r, n = idx // nblk, idx % nblk
    rq = _rows(r + d * ATT_BLOCK * n, d)
    if nblk == 1:
        return rq, _rows(_att_pad() + r + d * ATT_BLOCK * n, d)
    return rq, _rows(_att_pad() + r + d * ATT_BLOCK * (n - 1), d, blocks=2)


def _bdot(a, b, ca, cb):
    return lax.dot_general(a, b, (((ca,), (cb,)), ((0,), (0,))), preferred_element_type=F32)


def attn_fwd(name, q, kv, nb, seq):
    _, T, qn = q.shape
    dh = qn * N_DEV // (N_GROUPS * N_HEADS)
    scale = 1.0 / (dh ** 0.5)
    qs, ks, vs = _head_specs(seq, dh, qn // dh, kv.shape[2] // dh)

    def body(q_ref, k_ref, v_ref, m_ref, l_ref, qf, kf, vf, *branch):
        og, lg = branch[:N_GROUPS], branch[N_GROUPS:]
        pad = _att_pad()
        qf[...] = q_ref[...].astype(F32)
        for t_ref, s_ref in ((k_ref, kf), (v_ref, vf)):
            s_ref[pl.ds(0, pad), :] = jnp.zeros((pad, dh), F32)
            s_ref[pl.ds(pad, seq), :] = t_ref[...].astype(F32)
        for g in range(N_GROUPS):
            d = DILATIONS[g]
            nblk = seq // d // ATT_BLOCK

            def blocks(it, carry, d=d, nblk=nblk, g=g):
                first = it * ATT_BATCH
                rows = [_block_rows(first + b, d, nblk) for b in range(ATT_BATCH)]
                qb = jnp.stack([qf[rq, :] for rq, _ in rows]).astype(MM)
                kb = jnp.stack([kf[rk, :] for _, rk in rows]).astype(MM)
                vb = jnp.stack([vf[rk, :] for _, rk in rows]).astype(MM)
                s = jnp.where(_band_mask(first, nblk), _bdot(qb, kb, 2, 2) * scale, NEG)
                m = jnp.max(s, axis=-1, keepdims=True)
                p = jnp.exp(s - m)
                den = jnp.sum(p, axis=-1, keepdims=True)
                o = _bdot(p.astype(MM), vb, 2, 1) / den
                lse = m + jnp.log(den)
                for b, (rq, _) in enumerate(rows):
                    og[g][rq, :] = o[b]
                    lg[g][rq, :] = jnp.broadcast_to(lse[b], (ATT_BLOCK, dh))
                return carry

            @pl.when(pl.program_id(2) == g)
            def _(blocks=blocks, d=d, nblk=nblk):
                lax.fori_loop(0, d * nblk // ATT_BATCH, blocks, 0)

        @pl.when(pl.program_id(2) == N_GROUPS - 1)
        def _():
            mx = jnp.maximum(jnp.maximum(lg[0][...], lg[1][...]), lg[2][...])
            e = [jnp.exp(lg[g][...] - mx) for g in range(N_GROUPS)]
            tot = e[0] + e[1] + e[2]
            m_ref[...] = ((e[0] * og[0][...] + e[1] * og[1][...] + e[2] * og[2][...]) / tot).astype(m_ref.dtype)
            l_ref[...] = mx + jnp.log(tot)

    return _call(body, name=name, grid=(nb, N_HEADS, N_GROUPS), in_specs=[qs, ks, vs],
                 out_specs=[pl.BlockSpec((None, seq, dh), lambda b, h, g: (0, b, h)),
                            pl.BlockSpec((seq, dh), lambda b, h, g: (b, h))],
                 out_shape=[_sds((1, T, N_HEADS * dh), MM), _sds((T, N_HEADS * dh), F32)],
                 scratch=[pltpu.VMEM((seq, dh), F32)] + [pltpu.VMEM((_att_pad() + seq, dh), F32)] * 2
                 + [pltpu.VMEM((seq, dh), F32)] * (2 * N_GROUPS))(q, kv, kv)


def loss_fwd_bwd(name, x, target):
    T, D = x.shape
    tm = _tile(T, 512)

    def body(x_ref, t_ref, dx_ref, l_ref):
        @pl.when(pl.program_id(0) == 0)
        def _():
            l_ref[...] = jnp.zeros_like(l_ref)
        err = x_ref[...] - t_ref[...]
        dx_ref[...] = err * (1.0 / D)
        l_ref[...] += 0.5 * jnp.sum(jnp.mean(err * err, axis=-1, keepdims=True), axis=0, keepdims=True)

    dx, l = _call(body, name=name, grid=(T // tm,),
                  in_specs=[pl.BlockSpec((tm, D), lambda i: (i, 0))] * 2,
                  out_specs=[pl.BlockSpec((tm, D), lambda i: (i, 0)), pl.BlockSpec((1, 1), lambda i: (0, 0))],
                  out_shape=[_sds((T, D), F32), _sds((1, 1), F32)])(x, target)
    return dx, l


def resid_norm_bwd(name, dx, y, gain):
    T, D = y.shape
    tm = _tile(T, 512)

    def body(dx_ref, y_ref, g_ref, dy_ref, dg_ref, db_ref):
        @pl.when(pl.program_id(0) == 0)
        def _():
            dg_ref[...] = jnp.zeros_like(dg_ref)
            db_ref[...] = jnp.zeros_like(db_ref)
        y = y_ref[...]
        d = dx_ref[...]
        r = lax.rsqrt(jnp.mean(y * y, axis=-1, keepdims=True) + EPS)
        yh = y * r
        dyh = d * g_ref[...]
        dy = r * (dyh - yh * jnp.mean(dyh * yh, axis=-1, keepdims=True))
        dy_ref[...] = dy.astype(dy_ref.dtype)
        dg_ref[...] += jnp.sum(d * yh, axis=0, keepdims=True)
        db_ref[...] += jnp.sum(dy, axis=0, keepdims=True)

    return _call(body, name=name, grid=(T // tm,),
                 in_specs=[pl.BlockSpec((tm, D), lambda i: (i, 0))] * 2 + [pl.BlockSpec((1, D), lambda i: (0, 0))],
                 out_specs=[pl.BlockSpec((None, tm, D), lambda i: (0, i, 0))] + [pl.BlockSpec((1, D), lambda i: (0, 0))] * 2,
                 out_shape=[_sds((1, T, D), MM), _sds((1, D), F32), _sds((1, D), F32)])(dx, y, gain.reshape(1, D))


def mm_nt(name, dy, w, layer, out_dtype):
    _, T, D = dy.shape
    _, nk, kk, _ = w.shape
    tm = _tile(T, 1024)

    def body(dy_ref, w_ref, o_ref):
        o_ref[...] = _dot_nt(dy_ref[...], w_ref[...]).astype(o_ref.dtype)

    return _call(body, name=name, grid=(T // tm, nk),
                 in_specs=[pl.BlockSpec((None, tm, D), lambda i, q: (0, i, 0)),
                           pl.BlockSpec((None, None, kk, D), lambda i, q: (layer, q, 0, 0))],
                 out_specs=pl.BlockSpec((None, tm, kk), lambda i, q: (q, i, 0)),
                 out_shape=_sds((nk, T, kk), out_dtype))(dy, w)


def mm_nt_norm_bwd(name, du, w, layer, x_in, gain, dx_res):
    nsh, T, n = du.shape
    D = x_in.shape[1]
    tm = _tile(T, 512)
    grp = nsh if n <= DGRAD_GROUP_ALL_MAX_N else nsh // 2
    steps = nsh // grp

    def body(du_ref, w_ref, x_ref, g_ref, dr_ref, dx_ref, dg_ref, acc_ref):
        i, j = pl.program_id(0), pl.program_id(1)

        @pl.when((i == 0) & (j == 0))
        def _():
            dg_ref[...] = jnp.zeros_like(dg_ref)

        part = _dot_nt(du_ref[0], w_ref[0])
        for k in range(1, grp):
            part = part + _dot_nt(du_ref[k], w_ref[k])

        @pl.when(j == 0)
        def _():
            acc_ref[...] = part

        @pl.when(j > 0)
        def _():
            acc_ref[...] += part

        @pl.when(j == steps - 1)
        def _():
            x = x_ref[...]
            dh = acc_ref[...]
            r = lax.rsqrt(jnp.mean(x * x, axis=-1, keepdims=True) + EPS)
            xh = x * r
            dxh = dh * g_ref[...]
            dx_ref[...] = dr_ref[...] + r * (dxh - xh * jnp.mean(dxh * xh, axis=-1, keepdims=True))
            dg_ref[...] += jnp.sum(dh * xh, axis=0, keepdims=True)

    return _call(body, name=name, grid=(T // tm, steps),
                 in_specs=[pl.BlockSpec((grp, tm, n), lambda i, j: (j, i, 0)),
                           pl.BlockSpec((grp, None, D, n), lambda i, j: (j, layer, 0, 0)),
                           pl.BlockSpec((tm, D), lambda i, j: (i, 0)),
                           pl.BlockSpec((1, D), lambda i, j: (0, 0)),
                           pl.BlockSpec((tm, D), lambda i, j: (i, 0))],
                 out_specs=[pl.BlockSpec((tm, D), lambda i, j: (i, 0)), pl.BlockSpec((1, D), lambda i, j: (0, 0))],
                 out_shape=[_sds((T, D), F32), _sds((1, D), F32)],
                 scratch=[pltpu.VMEM((tm, D), F32)])(du, w, x_in, gain.reshape(1, D), dx_res)


def mm_tn(name, a, b):
    na, T, ka = a.shape
    nb, _, kb = b.shape
    nj = max(na, nb)

    def body(a_ref, b_ref, o_ref):
        o_ref[...] = _dot_tn(a_ref[...], b_ref[...]).astype(o_ref.dtype)

    return _call(body, name=name, grid=(nj,),
                 in_specs=[pl.BlockSpec((None, T, ka), (lambda j: (j, 0, 0)) if na > 1 else (lambda j: (0, 0, 0))),
                           pl.BlockSpec((None, T, kb), (lambda j: (j, 0, 0)) if nb > 1 else (lambda j: (0, 0, 0)))],
                 out_specs=pl.BlockSpec((None, ka, kb), lambda j: (j, 0, 0)),
                 out_shape=_sds((nj, ka, kb), MM))(a, b)


def ffn_bwd(name, dy, wout, u, ag, dw, wt, x_in, gain, dx_res, layer, seq):
    _, T, D = dy.shape
    n = wt.shape[2]
    tm = _tile(seq, 512)
    tps = seq // tm
    hb = FFN_HALO
    _, nxt = _halo_maps(tm, hb, T)
    K = FFN_CONV_W
    te = tm + hb
    rows = _tile(hb, ROW_CHUNK_BWD)
    u4, w5 = u.reshape(2, 4, T, n), wt.reshape(2, 4, n, D)
    dw4 = dw.reshape(2, 4, dw.shape[1], K, n)
    tiles = _col_tiles(n, FFN_COL_TILE)

    def body(dy_ref, dyn_ref, wo_ref, u_ref, ag_ref, agn_ref, cw_ref, wt_ref, x_ref, g_ref, dr_ref,
             du_ref, ddw_ref, ddb_ref, dx_ref, dg_ref,
             uf_ref, agf_ref, dzf_ref, da_ref, acc_ref):
        i, j = pl.program_id(0), pl.program_id(1)
        last = (i % tps) == tps - 1

        @pl.when((i == 0) & (j == 0))
        def _():
            dg_ref[...] = jnp.zeros_like(dg_ref)

        @pl.when(i == 0)
        def _():
            ddw_ref[j] = jnp.zeros((2, K, n), F32)
            ddb_ref[j] = jnp.zeros((2, 1, n), F32)

        dyt, dyn = dy_ref[...], dyn_ref[...]
        for ci, (c0, wc) in enumerate(tiles):
            cols = pl.ds(c0, wc)
            dzf_ref[pl.ds(0, tm), cols] = _dot_nt(dyt, wo_ref[cols, :])
            dzf_ref[pl.ds(tm, hb), cols] = jnp.where(last, 0.0, _dot_nt(dyn, wo_ref[cols, :]))
            for half in range(2):
                uf_ref[half, :, cols] = u_ref[half, :, cols].astype(F32)
                agf_ref[half, pl.ds(0, tm), cols] = ag_ref[half, :, cols].astype(F32)
                agf_ref[half, pl.ds(tm, hb), cols] = agn_ref[half, :, cols].astype(F32)
            for r0 in range(0, te, rows):
                dzc = dzf_ref[pl.ds(r0, rows), cols]
                da_ref[0, pl.ds(r0, rows), cols] = dzc * agf_ref[0, pl.ds(r0, rows), cols]
                da_ref[1, pl.ds(r0, rows), cols] = dzc * agf_ref[1, pl.ds(r0, rows), cols]
            for half in range(2):
                tap_acc = [jnp.zeros((rows, wc), F32) for _ in range(K)]
                bias_acc = jnp.zeros((rows, wc), F32)
                for r0 in range(0, tm, rows):
                    xr = uf_ref[half, pl.ds(r0, rows), cols]
                    acc = None
                    for k in range(K):
                        d = da_ref[half, pl.ds(r0 + K - 1 - k, rows), cols]
                        term = cw_ref[half, pl.ds(k, 1), cols] * d
                        acc = term if acc is None else acc + term
                        tap_acc[k] = tap_acc[k] + d * xr
                        if k == K - 1:
                            bias_acc = bias_acc + d
                    du_ref[half, pl.ds(r0, rows), cols] = acc.astype(du_ref.dtype)
                for k in range(K):
                    ddw_ref[j, half, pl.ds(k, 1), cols] += jnp.sum(tap_acc[k], axis=0, keepdims=True)
                ddb_ref[j, half, :, cols] += jnp.sum(bias_acc, axis=0, keepdims=True)
            part = _dot(du_ref[0, :, cols], wt_ref[0, cols, :]) + _dot(du_ref[1, :, cols], wt_ref[1, cols, :])
            if ci == 0:
                acc_ref[...] = part + jnp.where(j == 0, 0.0, acc_ref[...])
            else:
                acc_ref[...] += part
        @pl.when(j == 3)
        def _():
            x = x_ref[...]
            dh = acc_ref[...]
            r = lax.rsqrt(jnp.mean(x * x, axis=-1, keepdims=True) + EPS)
            xh = x * r
            dxh = dh * g_ref[...]
            dx_ref[...] = dr_ref[...] + r * (dxh - xh * jnp.mean(dxh * xh, axis=-1, keepdims=True))
            dg_ref[...] += jnp.sum(dh * xh, axis=0, keepdims=True)

    f32 = lambda *shape: pltpu.VMEM(shape, F32)
    du, ddw, ddb, dx, dg = _call(
        body, name=name, grid=(T // tm, 4),
        in_specs=[pl.BlockSpec((None, tm, D), lambda i, j: (0, i, 0)),
                  pl.BlockSpec((None, hb, D), lambda i, j: (0, nxt(i), 0)),
                  pl.BlockSpec((None, None, n, D), lambda i, j: (0, j, 0, 0)),
                  pl.BlockSpec((2, None, tm, n), lambda i, j: (0, j, i, 0)),
                  pl.BlockSpec((2, None, tm, n), lambda i, j: (0, j, i, 0)),
                  pl.BlockSpec((2, None, hb, n), lambda i, j: (0, j, nxt(i), 0)),
                  pl.BlockSpec((2, None, None, K, n), lambda i, j: (0, j, layer, 0, 0)),
                  pl.BlockSpec((2, None, n, D), lambda i, j: (0, j, 0, 0)),
                  pl.BlockSpec((tm, D), lambda i, j: (i, 0)),
                  pl.BlockSpec((1, D), lambda i, j: (0, 0)),
                  pl.BlockSpec((tm, D), lambda i, j: (i, 0))],
        out_specs=[pl.BlockSpec((2, None, tm, n), lambda i, j: (0, j, i, 0)),
                   pl.BlockSpec((4, 2, K, n), lambda i, j: (0, 0, 0, 0)),
                   pl.BlockSpec((4, 2, 1, n), lambda i, j: (0, 0, 0, 0)),
                   pl.BlockSpec((tm, D), lambda i, j: (i, 0)),
                   pl.BlockSpec((1, D), lambda i, j: (0, 0))],
        out_shape=[_sds((2, 4, T, n), MM), _sds((4, 2, K, n), F32), _sds((4, 2, 1, n), F32), _sds((T, D), F32),
                   _sds((1, D), F32)],
        scratch=[f32(2, tm, n), f32(2, te, n), f32(te, n), f32(2, te, n), f32(tm, D)],
    )(dy, dy, wout, u4, ag, ag, dw4, w5, x_in, gain.reshape(1, D), dx_res)
    return du.reshape(8, T, n), jnp.swapaxes(ddw, 0, 1), jnp.swapaxes(ddb, 0, 1), dx, dg


def ln_silu_bwd(name, ds, c, g, b):
    T, D = c.shape
    tm = _tile(T, 512)

    def body(ds_ref, c_ref, g_ref, b_ref, dc_ref, dg_ref, db_ref):
        @pl.when(pl.program_id(0) == 0)
        def _():
            dg_ref[...] = jnp.zeros_like(dg_ref)
            db_ref[...] = jnp.zeros_like(db_ref)
        cf = c_ref[...]
        mu = jnp.mean(cf, axis=-1, keepdims=True)
        xc = cf - mu
        r = lax.rsqrt(jnp.mean(xc * xc, axis=-1, keepdims=True) + EPS)
        xh = xc * r
        t = xh * g_ref[...] + b_ref[...]
        sg = _sigmoid(t)
        dt = ds_ref[...] * (sg * (1.0 + t * (1.0 - sg)))
        dg_ref[...] += jnp.sum(dt * xh, axis=0, keepdims=True)
        db_ref[...] += jnp.sum(dt, axis=0, keepdims=True)
        dxh = dt * g_ref[...]
        dc_ref[...] = r * (dxh - jnp.mean(dxh, axis=-1, keepdims=True)
                           - xh * jnp.mean(dxh * xh, axis=-1, keepdims=True))

    vec = pl.BlockSpec((1, D), lambda i: (0, 0))
    return _call(body, name=name, grid=(T // tm,),
                 in_specs=[pl.BlockSpec((None, tm, D), lambda i: (0, i, 0)), pl.BlockSpec((tm, D), lambda i: (i, 0)),
                           vec, vec],
                 out_specs=[pl.BlockSpec((tm, D), lambda i: (i, 0)), vec, vec],
                 out_shape=[_sds((T, D), F32), _sds((1, D), F32), _sds((1, D), F32)])(
                     ds, c, g.reshape(1, D), b.reshape(1, D))


def cm_glu_conv_bwd(name, u, dc, dw, layer, seq):
    _, T, n = u.shape
    ct = dw.shape[-1]
    per = n // ct
    nct = 4 * per
    tm = _tile(seq, 512)
    tps = seq // tm
    hb = CONV_HALO
    prev, nxt = _halo_maps(tm, hb, T)
    u4 = u.reshape(2, 4, T, n)
    K = CONV_W

    rows = _tile(tm, 2 * ROW_CHUNK_FWD)
    wrows = 8

    def body(u_ref, up_ref, dc_ref, dcn_ref, w_ref, du_ref, ddw_ref, ddb_ref, dbi_ref, zp_ref, zd_ref):
        i = pl.program_id(1)
        first = (i % tps) == 0
        last = (i % tps) == tps - 1

        @pl.when(i == 0)
        def _():
            ddw_ref[...] = jnp.zeros_like(ddw_ref)
            ddb_ref[...] = jnp.zeros_like(ddb_ref)
            dbi_ref[...] = jnp.zeros_like(dbi_ref)

        uh = up_ref[...].astype(F32)
        zp_ref[0, pl.ds(hb, tm), :] = u_ref[0].astype(F32) * _sigmoid(u_ref[1].astype(F32))
        zp_ref[0, pl.ds(0, hb), :] = jnp.where(first, 0.0, uh[0] * _sigmoid(uh[1]))
        zd_ref[0, pl.ds(0, tm), :] = dc_ref[...]
        zd_ref[0, pl.ds(tm, hb), :] = jnp.where(last, 0.0, dcn_ref[...])
        _stage_shifts(zp_ref, hb + tm, back=True)
        _stage_shifts(zd_ref, tm + hb, back=False)
        for r0 in range(0, tm, rows):
            dp = None
            for j in range(K):
                term = w_ref[pl.ds(K - 1 - j, 1), :] * zd_ref[j % 8, pl.ds(r0 + (j - j % 8), rows), :]
                dp = term if dp is None else dp + term
            v = u_ref[0, pl.ds(r0, rows), :].astype(F32)
            sg = _sigmoid(u_ref[1, pl.ds(r0, rows), :].astype(F32))
            dv = dp * sg
            dg = dp * v * sg * (1.0 - sg)
            du_ref[0, pl.ds(r0, rows), :] = dv.astype(du_ref.dtype)
            du_ref[1, pl.ds(r0, rows), :] = dg.astype(du_ref.dtype)
            dbi_ref[0] += jnp.sum(dv, axis=0, keepdims=True)
            dbi_ref[1] += jnp.sum(dg, axis=0, keepdims=True)
        tap_acc = [jnp.zeros((wrows, ct), F32) for _ in range(K)]
        bias_acc = jnp.zeros((wrows, ct), F32)
        for r0 in range(0, tm, wrows):
            d = zd_ref[0, pl.ds(r0, wrows), :]
            for j in range(K):
                tap_acc[j] = tap_acc[j] + d * zp_ref[j % 8, pl.ds(hb + r0 - (j - j % 8), wrows), :]
            bias_acc = bias_acc + d
        for j in range(K):
            ddw_ref[pl.ds(K - 1 - j, 1), :] += jnp.sum(tap_acc[j], axis=0, keepdims=True)
        ddb_ref[...] += jnp.sum(bias_acc, axis=0, keepdims=True)

    du, ddw, ddb, dbi = _call(
        body, name=name, grid=(nct, T // tm),
        in_specs=[pl.BlockSpec((2, None, tm, ct), lambda c, i: (0, c // per, i, c % per)),
                  pl.BlockSpec((2, None, hb, ct), lambda c, i: (0, c // per, prev(i), c % per)),
                  pl.BlockSpec((tm, ct), lambda c, i: (i, c)),
                  pl.BlockSpec((hb, ct), lambda c, i: (nxt(i), c)),
                  pl.BlockSpec((None, None, K, ct), lambda c, i: (c, layer, 0, 0))],
        out_specs=[pl.BlockSpec((2, None, tm, ct), lambda c, i: (0, c // per, i, c % per)),
                   pl.BlockSpec((None, K, ct), lambda c, i: (c, 0, 0)),
                   pl.BlockSpec((None, 1, ct), lambda c, i: (c, 0, 0)),
                   pl.BlockSpec((2, None, 1, ct), lambda c, i: (0, c // per, 0, c % per))],
        out_shape=[_sds((2, 4, T, n), MM), _sds((nct, K, ct), F32), _sds((nct, 1, ct), F32), _sds((2, 4, 1, n), F32)],
        scratch=[pltpu.VMEM((8, tm + hb, ct), F32), pltpu.VMEM((8, tm + hb, ct), F32)])(u4, u4, dc, dc, dw)
    return du.reshape(8, T, n), ddw, ddb, dbi


def attn_bwd(name, q, kv, dm, merged, lse, dkv_prev, nb, seq):
    _, T, qn = q.shape
    kn = kv.shape[2]
    dh = qn * N_DEV // (N_GROUPS * N_HEADS)
    scale = 1.0 / (dh ** 0.5)
    qs, ks, vs = _head_specs(seq, dh, qn // dh, kn // dh)
    has_prev = dkv_prev is not None
    n_in = 6 + (1 if has_prev else 0)

    def body(*refs):
        q_ref, k_ref, v_ref, dm_ref, mg_ref, l_ref = refs[:6]
        pkv_ref = refs[6] if has_prev else None
        dq_ref, dkv_ref = refs[n_in:n_in + 2]
        qf, kf, vf, dqf, dkf, dvf, dlt = refs[n_in + 2:]
        pad = _att_pad()

        @pl.when(pl.program_id(2) == 0)
        def _():
            dlt[...] = jnp.broadcast_to(
                jnp.sum(dm_ref[...] * mg_ref[...].astype(F32), axis=-1, keepdims=True), (seq, dh))

        qf[...] = q_ref[...].astype(F32)
        for t_ref, s_ref in ((k_ref, kf), (v_ref, vf)):
            s_ref[pl.ds(0, pad), :] = jnp.zeros((pad, dh), F32)
            s_ref[pl.ds(pad, seq), :] = t_ref[...].astype(F32)
        dkf[...] = jnp.zeros_like(dkf)
        dvf[...] = jnp.zeros_like(dvf)
        for g in range(N_GROUPS):
            d = DILATIONS[g]
            nblk = seq // d // ATT_BLOCK

            def blocks(it, carry, d=d, nblk=nblk):
                first = it * ATT_BATCH
                rows = [_block_rows(first + b, d, nblk) for b in range(ATT_BATCH)]
                qb = jnp.stack([qf[rq, :] for rq, _ in rows]).astype(MM)
                dmb = jnp.stack([dm_ref[rq, :] for rq, _ in rows]).astype(MM)
                lse = jnp.stack([l_ref[rq, :][:, :1] for rq, _ in rows])
                delta = jnp.stack([dlt[rq, :][:, :1] for rq, _ in rows])
                kb = jnp.stack([kf[rk, :] for _, rk in rows]).astype(MM)
                vb = jnp.stack([vf[rk, :] for _, rk in rows]).astype(MM)
                s = jnp.where(_band_mask(first, nblk), _bdot(qb, kb, 2, 2) * scale, NEG)
                p = jnp.exp(s - lse)
                dsc = (p * (_bdot(dmb, vb, 2, 2) - delta) * scale).astype(MM)
                dv = _bdot(p.astype(MM), dmb, 1, 1)
                dk = _bdot(dsc, qb, 1, 1)
                dq = _bdot(dsc, kb, 2, 1)
                for b, (rq, rk) in enumerate(rows):
                    dqf[rq, :] = dq[b]
                    dkf[rk, :] += dk[b]
                    dvf[rk, :] += dv[b]
                return carry

            @pl.when(pl.program_id(2) == g)
            def _(blocks=blocks, d=d, nblk=nblk):
                lax.fori_loop(0, d * nblk // ATT_BATCH, blocks, 0)

        dq_ref[...] = dqf[...].astype(dq_ref.dtype)
        dk, dv = dkf[pl.ds(pad, seq), :], dvf[pl.ds(pad, seq), :]
        if has_prev:
            dk, dv = dk + pkv_ref[0].astype(F32), dv + pkv_ref[1].astype(F32)
        dkv_ref[0] = dk.astype(dkv_ref.dtype)
        dkv_ref[1] = dv.astype(dkv_ref.dtype)

    per = kn // dh

    def both(b, h, g):
        f = g * N_HEADS + h
        return (0, f // per, b, f % per)

    kv_spec = pl.BlockSpec((2, None, seq, dh), both)
    full = pl.BlockSpec((None, seq, dh), lambda b, h, g: (0, b, h))
    in_specs = [qs, ks, vs, full, full, pl.BlockSpec((seq, dh), lambda b, h, g: (b, h))]
    args = [q, kv, kv, dm, merged, lse]
    if has_prev:
        in_specs.append(kv_spec)
        args.append(dkv_prev.reshape(2, N_DEV // 2, T, kn))
    short, padded = pltpu.VMEM((seq, dh), F32), pltpu.VMEM((_att_pad() + seq, dh), F32)
    dq, dkv = _call(body, name=name, grid=(nb, N_HEADS, N_GROUPS), in_specs=in_specs, out_specs=[qs, kv_spec],
                    out_shape=[_sds(q.shape, MM), _sds((2, N_DEV // 2, T, kn), MM)],
                    scratch=[short, padded, padded, short, padded, padded, short])(*args)
    return dq, dkv.reshape(N_DEV, T, kn)


def _adamw_math(w, g, m, v):
    m = ADAM_B1 * m + (1.0 - ADAM_B1) * g
    v = ADAM_B2 * v + (1.0 - ADAM_B2) * (g * g)
    m_hat = m / (1.0 - ADAM_B1 ** ADAM_STEP)
    v_hat = v / (1.0 - ADAM_B2 ** ADAM_STEP)
    delta = -ADAM_LR * (m_hat / (jnp.sqrt(v_hat) + ADAM_EPS) + ADAM_WD * w)
    return delta, m, v


def adamw_sum(name, w, m, v, parts, after):
    L, R, C = w.shape
    tr = _tile(R, 256)

    def body(*refs):
        w_ref, m_ref, v_ref = refs[:3]
        p_refs = refs[3:3 + L]
        g_ref, d_ref, nm_ref, nv_ref = refs[4 + L:]
        for l in range(L):
            @pl.when(pl.program_id(0) == l)
            def _(p_ref=p_refs[l]):
                g = p_ref[0].astype(F32)
                for k in range(1, N_DEV):
                    g = g + p_ref[k].astype(F32)
                g_ref[...] = g
                d_ref[...], nm_ref[...], nv_ref[...] = _adamw_math(w_ref[...], g, m_ref[...], v_ref[...])

    blk = pl.BlockSpec((None, tr, C), lambda l, i: (l, i, 0))
    part = lambda k: pl.BlockSpec((N_DEV, tr, C), lambda l, i: (0, jnp.where(l == k, i, 0), 0))
    return _call(body, name=name, grid=(L, R // tr),
                 in_specs=[blk, blk, blk] + [part(k) for k in range(L)] + [pl.BlockSpec(memory_space=pl.ANY)],
                 out_specs=[blk] * 4, out_shape=[_sds((L, R, C), F32)] * 4)(w, m, v, *parts, after)


def sum_partials(name, parts):
    _, R, C = parts.shape
    tr = _tile(R, 512)

    def body(p_ref, o_ref):
        g = p_ref[0]
        for k in range(1, N_DEV):
            g = g + p_ref[k]
        o_ref[...] = g

    return _call(body, name=name, grid=(R // tr,),
                 in_specs=[pl.BlockSpec((N_DEV, tr, C), lambda i: (0, i, 0))],
                 out_specs=pl.BlockSpec((tr, C), lambda i: (i, 0)), out_shape=_sds((R, C), F32))(parts)


def adamw_small(name, w, g, m, v):
    R, C = w.shape
    tr = _tile(R, 512)

    def body(w_ref, g_ref, m_ref, v_ref, d_ref, nm_ref, nv_ref):
        d_ref[...], nm_ref[...], nv_ref[...] = _adamw_math(w_ref[...], g_ref[...], m_ref[...], v_ref[...])

    blk = pl.BlockSpec((tr, C), lambda i: (i, 0))
    return _call(body, name=name, grid=(R // tr,), in_specs=[blk] * 4, out_specs=[blk] * 3,
                 out_shape=[_sds((R, C), F32)] * 3)(w, g, m, v)


def _pack(arrays):
    pieces = []
    for a in arrays:
        f = a.reshape(-1).astype(F32)
        pieces.append(jnp.pad(f, (0, (-f.shape[0]) % PACK)))
    return jnp.concatenate(pieces).reshape(-1, 128)


def _unpack(flat, shapes):
    out, off = [], 0
    f = flat.reshape(-1)
    for s in shapes:
        size = 1
        for d in s:
            size *= d
        out.append(f[off:off + size].reshape(s))
        off += size + (-size) % PACK
    return out


def kernel(x, mix_pre_g, mix_post_g, ffn_pre_g, ffn_post_g, cm_w_in, cm_b_in, cm_dw, cm_dw_b, cm_ln_g, cm_ln_b, cm_w_out, cm_b_out, kv_norm_g, w_kv, w_q, w_o, ffn_w_in, ffn_dw, ffn_dw_b, ffn_w_out, loss_target, m_mix_pre_g, m_mix_post_g, m_ffn_pre_g, m_ffn_post_g, m_cm_w_in, m_cm_b_in, m_cm_dw, m_cm_dw_b, m_cm_ln_g, m_cm_ln_b, m_cm_w_out, m_cm_b_out, m_kv_norm_g, m_w_kv, m_w_q, m_w_o, m_ffn_w_in, m_ffn_dw, m_ffn_dw_b, m_ffn_w_out, v_mix_pre_g, v_mix_post_g, v_ffn_pre_g, v_ffn_post_g, v_cm_w_in, v_cm_b_in, v_cm_dw, v_cm_dw_b, v_cm_ln_g, v_cm_ln_b, v_cm_w_out, v_cm_b_out, v_kv_norm_g, v_w_kv, v_w_q, v_w_o, v_ffn_w_in, v_ffn_dw, v_ffn_dw_b, v_ffn_w_out):
    nb, seq, D = x.shape
    T = nb * seq
    me = _my_index()
    n_b = DEPTH - N_A

    nf = ffn_w_in.shape[-1]
    t_ = lambda t: jnp.swapaxes(t, 1, 2)
    fin_t, m_fin_t, v_fin_t = t_(ffn_w_in), t_(m_ffn_w_in), t_(v_ffn_w_in)

    stages = [(part, i) for i in range(DEPTH) for part in ("mix", "ffn")]
    stages.insert(stages.index(("ffn", N_A - 1)) + 1, ("kv", N_A - 1))

    def stage_sources(stage):
        part, i = stage
        if part == "ffn":
            src = {"fin": fin_t[i], "fout": ffn_w_out[i]}
        elif part == "kv":
            src = {"kv": w_kv}
        elif i < N_A:
            src = {"cin": cm_w_in[i], "cout": cm_w_out[i]}
        else:
            src = {"q": w_q[i - N_A], "o": w_o[i - N_A]}
        return {k: t.astype(MM) for k, t in src.items()}

    def begin_gather(stage, after):
        src = stage_sources(stage)
        names, arrays = list(src), list(src.values())
        tag = f"{stage[0]}{stage[1]}"
        lands = place_own(f"gather_own_{tag}", "gather", arrays)
        handle, token = exchange_begin(f"gather_begin_{tag}", "gather", arrays, lands, after)
        return (names, handle), token

    def end_gather(stage, pending, after):
        names, handle = pending
        W = dict(zip(names, exchange_end(f"gather_end_{stage[0]}{stage[1]}", handle, after)))
        for k in W:
            if k in ("cout", "o"):
                W[k] = W[k].reshape(1, 1, D, D)
            elif k == "fout":
                W[k] = W[k].reshape(1, 4, nf, D)
            else:
                W[k] = W[k][:, None]
        return W

    small = [cm_b_in[:, None, :], cm_dw, cm_dw_b[:, None, :], cm_ln_g, cm_ln_b, cm_b_out, ffn_dw]
    Bcin, DWc, DWBc, LNg, LNb, Bcout, DWf = all_gather("gather_small", small, [False] * len(small))
    LNg = jnp.swapaxes(LNg, 0, 1).reshape(N_A, D)
    LNb = jnp.swapaxes(LNb, 0, 1).reshape(N_A, D)
    Bcout = jnp.swapaxes(Bcout, 0, 1).reshape(N_A, D)
    DWBf = ffn_dw_b.reshape(DEPTH, N_DEV, 1, nf)
    zero_bias = jnp.zeros((D,), F32)

    xs = x.reshape(T, D)
    sv = []
    kv = hkv = None
    pending, _ = begin_gather(stages[0], DWf)
    Ws = {stages[0]: end_gather(stages[0], pending, xs)}
    sv = [{} for _ in range(DEPTH)]
    for idx, stage in enumerate(stages):
        part, i = stage
        L, W = sv[i], Ws[stage]
        gain = {"mix": mix_pre_g[i], "ffn": ffn_pre_g[i], "kv": kv_norm_g}[part]
        following = stages[idx + 1] if idx + 1 < len(stages) else None
        if following is not None:
            pending, token = begin_gather(following, next(iter(W.values())))
            gain = gain + token[0, 0]
        if part == "mix":
            L["x_in"] = xs
            if i < N_A:
                L["u"], L["h"] = norm_mm(f"cm_in_{i}", xs, gain, W["cin"], 0, Bcin[:, i:i + 1])
                L["c"] = cm_glu_conv(f"cm_conv_{i}", L["u"], DWc, DWBc, i, seq)
                L["s"] = ln_silu(f"cm_ln_{i}", L["c"], LNg[i], LNb[i])
                L["y"], xs = mm_resid_norm(f"cm_out_{i}", L["s"], W["cout"], 0, Bcout[i], xs, mix_post_g[i])
            else:
                L["q"], L["h"] = norm_mm(f"attn_q_{i}", xs, gain, W["q"], 0)
                L["mg"], L["lse"] = attn_fwd(f"attn_{i}", L["q"], kv, nb, seq)
                L["y"], xs = mm_resid_norm(f"attn_out_{i}", L["mg"], W["o"], 0, zero_bias, xs, mix_post_g[i])
            L["x1"] = xs
        elif part == "ffn":
            L["uf"], L["hf"], L["z"], L["ag"] = ffn_in_conv(f"ffn_in_{i}", xs, gain, W["fin"], DWf, DWBf, i, seq)
            L["yf"], xs = mm_resid_norm(f"ffn_out_{i}", L["z"], W["fout"], 0, zero_bias, xs, ffn_post_g[i])
        else:
            kv, hkv = norm_mm("kv_proj", xs, gain, W["kv"], 0)
        if following is not None:
            Ws[following] = end_gather(following, pending, kv if part == "kv" else xs)
    dx, loss_part = loss_fwd_bwd("loss", xs, loss_target.reshape(T, D))

    g_mix_pre, g_mix_post, g_ffn_pre, g_ffn_post = [None] * DEPTH, [None] * DEPTH, [None] * DEPTH, [None] * DEPTH
    g_ffn_dw, g_ffn_dwb = [None] * DEPTH, [None] * DEPTH
    g_cbin, g_cdw, g_cdwb, g_lng, g_lnb, g_cbout = ([None] * N_A for _ in range(6))
    g_kvn = dkv = None
    landed = [{} for _ in range(DEPTH)]
    in_flight = token = None
    for stage in reversed(stages):
        part, i = stage
        L, W = sv[i], Ws[stage]
        gain = {"mix": mix_post_g[i], "ffn": ffn_post_g[i], "kv": kv_norm_g}[part]
        if token is not None:
            gain = gain + token[0, 0]
        send = {}
        if part == "kv":
            send["kv"] = mm_tn("kv_wg", hkv[None], dkv)
            dx, g_kvn = mm_nt_norm_bwd("kv_bwd", dkv, W["kv"], 0, sv[i + 1]["x_in"], gain, dx)
        elif part == "ffn":
            dyf, g_ffn_post[i], _ = resid_norm_bwd(f"ffn_post_bwd_{i}", dx, L["yf"], gain)
            send["fout"] = mm_tn(f"ffn_out_wg_{i}", L["z"], dyf).reshape(N_DEV, nf // 2, D)
            duf, ddw, ddwb, dx, g_ffn_pre[i] = ffn_bwd(f"ffn_bwd_{i}", dyf, W["fout"], L["uf"], L["ag"], DWf, W["fin"],
                                                       L["x1"], ffn_pre_g[i], dx, i, seq)
            g_ffn_dw[i], g_ffn_dwb[i] = ddw.reshape(N_DEV, FFN_CONV_W, nf), ddwb.reshape(-1)
            send["fin"] = mm_tn(f"ffn_in_wg_{i}", duf, L["hf"][None])
        else:
            dy, g_mix_post[i], dyb = resid_norm_bwd(f"mix_post_bwd_{i}", dx, L["y"], gain)
            if i >= N_A:
                dm = mm_nt(f"attn_out_bwd_{i}", dy, W["o"], 0, F32)
                send["o"] = mm_tn(f"attn_out_wg_{i}", L["mg"], dy).reshape(N_DEV, D // N_DEV, D)
                dq, dkv = attn_bwd(f"attn_bwd_{i}", L["q"], kv, dm, L["mg"], L["lse"], dkv, nb, seq)
                send["q"] = mm_tn(f"attn_q_wg_{i}", L["h"][None], dq)
                dx, g_mix_pre[i] = mm_nt_norm_bwd(f"attn_q_bwd_{i}", dq, W["q"], 0, L["x_in"], mix_pre_g[i], dx)
            else:
                g_cbout[i] = dyb
                ds = mm_nt(f"cm_out_bwd_{i}", dy, W["cout"], 0, F32)
                send["cout"] = mm_tn(f"cm_out_wg_{i}", L["s"], dy).reshape(N_DEV, D // N_DEV, D)
                dc, g_lng[i], g_lnb[i] = ln_silu_bwd(f"cm_ln_bwd_{i}", ds, L["c"], LNg[i], LNb[i])
                du, g_cdw[i], g_cdwb[i], dbi = cm_glu_conv_bwd(f"cm_conv_bwd_{i}", L["u"], dc, DWc, i, seq)
                g_cbin[i] = dbi.reshape(N_DEV, -1)
                send["cin"] = mm_tn(f"cm_in_wg_{i}", L["h"][None], du)
                dx, g_mix_pre[i] = mm_nt_norm_bwd(f"cm_in_bwd_{i}", du, W["cin"], 0, L["x_in"], mix_pre_g[i], dx)
        if in_flight is not None:
            (p, j), names, handle = in_flight
            landed[j].update(zip(names, exchange_end(f"scatter_end_{p}{j}", handle, dx)))
        names, arrays = list(send), list(send.values())
        lands = place_own(f"scatter_own_{part}{i}", "scatter", arrays)
        handle, token = exchange_begin(f"scatter_begin_{part}{i}", "scatter", arrays, lands, dx)
        in_flight = (stage, names, handle)
    grad_x = dx.reshape(nb, seq, D)

    rep_names = ["mix_pre_g", "mix_post_g", "ffn_pre_g", "ffn_post_g", "kv_norm_g", "ffn_dw_b"]
    rep_parts = [jnp.concatenate(g_mix_pre), jnp.concatenate(g_mix_post), jnp.concatenate(g_ffn_pre),
                 jnp.concatenate(g_ffn_post), g_kvn.reshape(-1), jnp.stack(g_ffn_dwb)]
    rep_w = [mix_pre_g, mix_post_g, ffn_pre_g, ffn_post_g, kv_norm_g, ffn_dw_b]
    rep_m = [m_mix_pre_g, m_mix_post_g, m_ffn_pre_g, m_ffn_post_g, m_kv_norm_g, m_ffn_dw_b]
    rep_v = [v_mix_pre_g, v_mix_post_g, v_ffn_pre_g, v_ffn_post_g, v_kv_norm_g, v_ffn_dw_b]
    sh_names = ["ffn_dw", "cm_b_in", "cm_dw", "cm_dw_b", "cm_ln_g", "cm_ln_b", "cm_b_out"]
    own = lambda per_layer, shard: jnp.stack([p.reshape((N_DEV,) + shard) for p in per_layer], axis=1)
    sh_parts = [own(g_ffn_dw, ffn_dw.shape[1:]), own(g_cbin, cm_b_in.shape[1:]), own(g_cdw, cm_dw.shape[1:]),
                own(g_cdwb, cm_dw_b.shape[1:]), own(g_lng, cm_ln_g.shape[1:]), own(g_lnb, cm_ln_b.shape[1:]),
                own(g_cbout, cm_b_out.shape[1:])]
    sh_w = [ffn_dw, cm_b_in, cm_dw, cm_dw_b, cm_ln_g, cm_ln_b, cm_b_out]
    sh_m = [m_ffn_dw, m_cm_b_in, m_cm_dw, m_cm_dw_b, m_cm_ln_g, m_cm_ln_b, m_cm_b_out]
    sh_v = [v_ffn_dw, v_cm_b_in, v_cm_dw, v_cm_dw_b, v_cm_ln_g, v_cm_ln_b, v_cm_b_out]
    rep_pack = _pack([loss_part] + rep_parts)
    sh_pack = jnp.stack([_pack([p[k] for p in sh_parts]) for k in range(N_DEV)])
    n_rep = rep_pack.shape[0]
    small = []
    for kind, pack in (("gather", rep_pack), ("scatter", sh_pack)):
        lands = place_own(f"{kind}_own_small", kind, [pack])
        handle, token = exchange_begin(f"{kind}_begin_small", kind, [pack], lands, token)
        small.append((kind, handle))

    def big_update(name, w, m, v, key, layers, after):
        as3 = lambda t: t.reshape((-1,) + t.shape[-2:])
        outs = adamw_sum(name, as3(w), as3(m), as3(v), [landed[i][key] for i in layers], after)
        return [t.reshape(w.shape) for t in outs]

    conf, attn = range(N_A), range(N_A, DEPTH)
    upd = {}
    upd["ffn_w_in"] = [t_(t) for t in big_update("adam_ffn_w_in", fin_t, m_fin_t, v_fin_t, "fin", range(DEPTH), token)]
    upd["ffn_w_out"] = big_update("adam_ffn_w_out", ffn_w_out, m_ffn_w_out, v_ffn_w_out, "fout", range(DEPTH), token)
    upd["w_kv"] = big_update("adam_w_kv", w_kv, m_w_kv, v_w_kv, "kv", [N_A - 1], token)
    upd["w_q"] = big_update("adam_w_q", w_q, m_w_q, v_w_q, "q", attn, token)
    upd["w_o"] = big_update("adam_w_o", w_o, m_w_o, v_w_o, "o", attn, upd["w_q"][0])
    (p, j), names, handle = in_flight
    landed[j].update(zip(names, exchange_end(f"scatter_end_{p}{j}", handle, upd["w_o"][0])))
    upd["cm_w_in"] = big_update("adam_cm_w_in", cm_w_in, m_cm_w_in, v_cm_w_in, "cin", conf, token)
    upd["cm_w_out"] = big_update("adam_cm_w_out", cm_w_out, m_cm_w_out, v_cm_w_out, "cout", conf, upd["cm_w_in"][0])
    (rep_landed,), (sh_landed,) = (exchange_end(f"{kind}_end_small", handle, upd["cm_w_out"][0])
                                   for kind, handle in small)
    rep_sum = sum_partials("sum_small_rep", rep_landed)
    sh_sum = sum_partials("sum_small_sh", sh_landed)
    rep_shapes = [(1, 1)] + [w.shape for w in rep_w]
    sh_shapes = [w.shape for w in sh_w]
    g_small = jnp.concatenate([rep_sum, sh_sum])
    pad1 = jnp.zeros((1, 1), F32)
    d_s, m_s, v_s = adamw_small("adam_small", jnp.concatenate([_pack([pad1] + rep_w), _pack(sh_w)]), g_small,
                                jnp.concatenate([_pack([pad1] + rep_m), _pack(sh_m)]),
                                jnp.concatenate([_pack([pad1] + rep_v), _pack(sh_v)]))
    split = lambda t: (_unpack(t[:n_rep], rep_shapes), _unpack(t[n_rep:], sh_shapes))
    for (rep_t, sh_t), slot in zip([split(g_small), split(d_s), split(m_s), split(v_s)], range(4)):
        if slot == 0:
            loss = rep_t[0].reshape(())
        for name, t in zip(rep_names, rep_t[1:]):
            upd.setdefault(name, [None] * 4)[slot] = t
        for name, t in zip(sh_names, sh_t):
            upd.setdefault(name, [None] * 4)[slot] = t

    order = ["mix_pre_g", "mix_post_g", "ffn_pre_g", "ffn_post_g", "cm_w_in", "cm_b_in", "cm_dw", "cm_dw_b", "cm_ln_g",
             "cm_ln_b", "cm_w_out", "cm_b_out", "kv_norm_g", "w_kv", "w_q", "w_o", "ffn_w_in", "ffn_dw", "ffn_dw_b",
             "ffn_w_out"]
    return (loss, grad_x, *[upd[n][0] for n in order], *[upd[n][1] for n in order],
            *[upd[n][2] for n in order], *[upd[n][3] for n in order])
```

```python
import functools

import jax
import jax.numpy as jnp
from jax import lax
from jax.experimental import pallas as pl
from jax.experimental.pallas import tpu as pltpu

N_DEV = 8
N_A = 2
DEPTH = 4
N_HEADS = 8
N_GROUPS = 3
DILATIONS = (1, 4, 16)
ATT_BLOCK = 128
ATT_BATCH = 8
CONV_W = 31
FFN_CONV_W = 3
CONV_HALO = 32
FFN_HALO = 16
ROW_CHUNK_FWD = 16
ROW_CHUNK_BWD = 16
DGRAD_GROUP_ALL_MAX_N = 384
FFN_COL_TILE = 1024
EPS = 1e-6
NEG = -1e30
ADAM_LR, ADAM_B1, ADAM_B2, ADAM_EPS, ADAM_WD, ADAM_STEP = 0.001, 0.9, 0.999, 1e-08, 0.01, 10
MM = jnp.bfloat16
F32 = jnp.float32
VMEM_LIMIT_BYTES = 56 * 1024 * 1024
PACK = 1024
MESH_ID = pl.DeviceIdType.MESH

_pallas = pl.pallas_call


def _call(body, *, name, out_shape, grid=(), in_specs=None, out_specs=None, scratch=()):
    return _pallas(body, name=name, out_shape=out_shape, grid=grid, in_specs=in_specs, out_specs=out_specs,
                   scratch_shapes=list(scratch),
                   compiler_params=pltpu.CompilerParams(vmem_limit_bytes=VMEM_LIMIT_BYTES))


def _tile(n, pref):
    if n <= pref:
        return n
    t = pref - pref % 8
    while n % t:
        t -= 8
    assert t > 0, (n, pref)
    return t


def _sds(shape, dtype):
    return jax.ShapeDtypeStruct(tuple(shape), dtype)


def _dot(a, b):
    return jnp.dot(a, b, preferred_element_type=F32)


def _dot_nt(a, b):
    return lax.dot_general(a, b, (((1,), (1,)), ((), ())), preferred_element_type=F32)


def _dot_tn(a, b):
    return lax.dot_general(a, b, (((0,), (0,)), ((), ())), preferred_element_type=F32)


def _sigmoid(x):
    return 0.5 * jnp.tanh(0.5 * x) + 0.5


def _my_index():
    return 4 * lax.axis_index("x") + 2 * lax.axis_index("y") + lax.axis_index("c")


def _exchange(name, arrays, out_shapes, pieces, src_of, dst_of):
    n = len(arrays)
    base = [sum(pieces[:a]) for a in range(n)]
    total = sum(pieces)

    def body(*refs):
        ins, outs = refs[:n], refs[n:2 * n]
        send_sems, recv_sems, local_sems = refs[2 * n:]
        x, y, c = lax.axis_index("x"), lax.axis_index("y"), lax.axis_index("c")
        me = 4 * x + 2 * y + c
        copies = []
        for a in range(n):
            for k, (s, d) in enumerate(zip(src_of(a, ins[a], me), dst_of(a, outs[a], me))):
                cp = pltpu.make_async_copy(s, d, local_sems.at[base[a] + k])
                cp.start()
                copies.append(cp)
        remote = []
        for m in range(1, N_DEV):
            px, py, pc = x ^ (m >> 2), y ^ ((m >> 1) & 1), c ^ (m & 1)
            peer = 4 * px + 2 * py + pc
            for a in range(n):
                for k, (s, d) in enumerate(zip(src_of(a, ins[a], peer), dst_of(a, outs[a], me))):
                    cp = pltpu.make_async_remote_copy(src_ref=s, dst_ref=d, send_sem=send_sems.at[base[a] + k, m - 1],
                                                      recv_sem=recv_sems.at[base[a] + k, m - 1],
                                                      device_id=(px, py, pc), device_id_type=MESH_ID)
                    cp.start()
                    remote.append(cp)
        for cp in copies:
            cp.wait()
        for cp in remote:
            cp.wait_send()
        for m in range(1, N_DEV):
            px, py, pc = x ^ (m >> 2), y ^ ((m >> 1) & 1), c ^ (m & 1)
            peer = 4 * px + 2 * py + pc
            for a in range(n):
                for k, (s, d) in enumerate(zip(src_of(a, ins[a], me), dst_of(a, outs[a], peer))):
                    pltpu.make_async_remote_copy(src_ref=s, dst_ref=d, send_sem=send_sems.at[base[a] + k, m - 1],
                                                 recv_sem=recv_sems.at[base[a] + k, m - 1], device_id=(px, py, pc),
                                                 device_id_type=MESH_ID).wait_recv()

    any_spec = pl.BlockSpec(memory_space=pl.ANY)
    return _call(body, name=name, out_shape=[_sds(s, a.dtype) for s, a in zip(out_shapes, arrays)],
                 in_specs=[any_spec] * n, out_specs=[any_spec] * n,
                 scratch=[pltpu.SemaphoreType.DMA((total, N_DEV - 1)), pltpu.SemaphoreType.DMA((total, N_DEV - 1)),
                          pltpu.SemaphoreType.DMA((total,))])(*arrays)


def all_gather(name, arrays, row_sharded):
    def out_shape(a):
        s = arrays[a].shape
        return (s[0], N_DEV) + s[1:] if row_sharded[a] else (N_DEV,) + s

    def src_of(a, ref, peer):
        if row_sharded[a]:
            return [ref.at[l] for l in range(arrays[a].shape[0])]
        return [ref]

    def dst_of(a, ref, me):
        if row_sharded[a]:
            return [ref.at[l, me] for l in range(arrays[a].shape[0])]
        return [ref.at[me]]

    pieces = [arrays[a].shape[0] if row_sharded[a] else 1 for a in range(len(arrays))]
    return _exchange(name, arrays, [out_shape(a) for a in range(len(arrays))], pieces, src_of, dst_of)


def _src_view(kind, ref, peer):
    return ref if kind == "gather" else ref.at[peer]


def _peers(x, y, c):
    for m in range(1, N_DEV):
        px, py, pc = x ^ (m >> 2), y ^ ((m >> 1) & 1), c ^ (m & 1)
        yield m - 1, (px, py, pc), 4 * px + 2 * py + pc


def place_own(name, kind, srcs):
    n = len(srcs)
    shapes = [(N_DEV,) + s.shape if kind == "gather" else s.shape for s in srcs]
    steps = 2 if all(s.shape[-2] % 32 == 0 for s in srcs) else 1

    def body(*refs):
        for a in range(n):
            refs[n + a][...] = refs[a][...]

    def spec(shape, own_block):
        R, C = shape[-2:]
        tr = R // steps
        if own_block:
            return pl.BlockSpec((None, tr, C), lambda i: (_my_index(), i, 0))
        return pl.BlockSpec((tr, C), lambda i: (i, 0))

    return _call(body, name=name, grid=(steps,), out_shape=[_sds(s, a.dtype) for s, a in zip(shapes, srcs)],
                 in_specs=[spec(s.shape, kind == "scatter") for s in srcs],
                 out_specs=[spec(s, True) for s in shapes])(*srcs)


_HBM_SPEC = pl.BlockSpec(memory_space=pltpu.HBM)
_SEM_SPEC = pl.BlockSpec(memory_space=pltpu.SEMAPHORE)
_DATAFLOW = pltpu.SideEffectType.DATAFLOW_SIDE_EFFECTING


def _remote(kind, src, land, send_sems, recv_sems, a, slot, frm, to_id, at):
    return pltpu.make_async_remote_copy(src_ref=_src_view(kind, src, frm), dst_ref=land.at[at],
                                        send_sem=send_sems.at[a * (N_DEV - 1) + slot],
                                        recv_sem=recv_sems.at[a * (N_DEV - 1) + slot],
                                        device_id=to_id, device_id_type=MESH_ID)


def exchange_begin(name, kind, srcs, lands, after):
    n = len(srcs)

    def body(*refs):
        ins, lnd = refs[:n], refs[n:2 * n]
        send_sems, recv_sems = refs[2 * n + 1], refs[2 * n + 2]
        token = refs[-1]
        x, y, c = lax.axis_index("x"), lax.axis_index("y"), lax.axis_index("c")
        me = 4 * x + 2 * y + c
        for slot, peer_id, peer in _peers(x, y, c):
            for a in range(n):
                _remote(kind, ins[a], lnd[a], send_sems, recv_sems, a, slot, peer, peer_id, me).start()
        token[...] = jnp.zeros_like(token)

    hbm = lambda t: pltpu.HBM(t.shape, t.dtype)
    outs = _pallas(
        body, name=name,
        out_shape=(pltpu.SemaphoreType.DMA((n * (N_DEV - 1),)), pltpu.SemaphoreType.DMA((n * (N_DEV - 1),)),
                   *[hbm(t) for t in srcs], *[hbm(t) for t in lands], _sds((8, 128), F32)),
        in_specs=[_HBM_SPEC] * (2 * n) + [pl.BlockSpec(memory_space=pl.ANY)],
        out_specs=(_SEM_SPEC, _SEM_SPEC, *[_HBM_SPEC] * (2 * n), pl.BlockSpec(memory_space=pltpu.VMEM)),
        input_output_aliases={i: 2 + i for i in range(2 * n)},
        compiler_params=pltpu.CompilerParams(has_side_effects=_DATAFLOW),
    )(*[pltpu.with_memory_space_constraint(t, pltpu.HBM) for t in list(srcs) + list(lands)], after)
    return (kind, outs[0], outs[1], list(outs[2:2 + n]), list(outs[2 + n:2 + 2 * n])), outs[-1]


def exchange_end(name, handle, after):
    kind, send_sems, recv_sems, srcs, lands = handle
    n = len(srcs)

    def body(*refs):
        ins, lnd = refs[:n], refs[n:2 * n]
        s_sems, r_sems = refs[2 * n], refs[2 * n + 1]
        x, y, c = lax.axis_index("x"), lax.axis_index("y"), lax.axis_index("c")
        me = 4 * x + 2 * y + c
        for slot, peer_id, peer in _peers(x, y, c):
            for a in range(n):
                _remote(kind, ins[a], lnd[a], s_sems, r_sems, a, slot, peer, peer_id, me).wait_send()
        for slot, peer_id, peer in _peers(x, y, c):
            for a in range(n):
                _remote(kind, ins[a], lnd[a], s_sems, r_sems, a, slot, me, peer_id, peer).wait_recv()

    hbm = lambda t: pltpu.HBM(t.shape, t.dtype)
    outs = _pallas(
        body, name=name, out_shape=tuple(hbm(t) for t in srcs + lands),
        in_specs=[_HBM_SPEC] * (2 * n) + [_SEM_SPEC, _SEM_SPEC, pl.BlockSpec(memory_space=pl.ANY)],
        out_specs=tuple([_HBM_SPEC] * (2 * n)), input_output_aliases={i: i for i in range(2 * n)},
        compiler_params=pltpu.CompilerParams(has_side_effects=_DATAFLOW),
    )(*srcs, *lands, send_sems, recv_sems, after)
    return list(outs[n:])


def norm_mm(name, x, gain, w, layer, bias=None, w_t=False):
    T, D = x.shape
    nsh = w.shape[0]
    n = w.shape[2] if w_t else w.shape[3]
    tm = _tile(T, 2048)

    def body(*refs):
        if bias is None:
            x_ref, g_ref, w_ref, u_ref, h_ref = refs
        else:
            x_ref, g_ref, w_ref, b_ref, u_ref, h_ref = refs

        @pl.when(pl.program_id(1) == 0)
        def _():
            xf = x_ref[...]
            r = lax.rsqrt(jnp.mean(xf * xf, axis=-1, keepdims=True) + EPS)
            h_ref[...] = (xf * r * g_ref[...]).astype(h_ref.dtype)

        acc = (_dot_nt if w_t else _dot)(h_ref[...], w_ref[...])
        if bias is not None:
            acc = acc + b_ref[...]
        u_ref[...] = acc.astype(u_ref.dtype)

    in_specs = [pl.BlockSpec((tm, D), lambda i, j: (i, 0)),
                pl.BlockSpec((1, D), lambda i, j: (0, 0)),
                pl.BlockSpec((None, None) + w.shape[2:], lambda i, j: (j, layer, 0, 0))]
    args = [x, gain.reshape(1, D), w]
    if bias is not None:
        in_specs.append(pl.BlockSpec((None, None, 1, n), lambda i, j: (j, layer, 0, 0)))
        args.append(bias)
    return _call(body, name=name, grid=(T // tm, nsh), in_specs=in_specs,
                 out_specs=[pl.BlockSpec((None, tm, n), lambda i, j: (j, i, 0)),
                            pl.BlockSpec((tm, D), lambda i, j: (i, 0))],
                 out_shape=[_sds((nsh, T, n), MM), _sds((T, D), MM)])(*args)


def mm_resid_norm(name, a, w, layer, bias, x, gain):
    nk, T, kk = a.shape
    D = x.shape[1]
    tm = _tile(T, 512)

    def body(a_ref, w_ref, b_ref, x_ref, g_ref, y_ref, xn_ref):
        y = _dot(a_ref[0], w_ref[0])
        for q in range(1, nk):
            y = y + _dot(a_ref[q], w_ref[q])
        y = y + b_ref[...]
        y_ref[...] = y
        r = lax.rsqrt(jnp.mean(y * y, axis=-1, keepdims=True) + EPS)
        xn_ref[...] = x_ref[...] + y * r * g_ref[...]

    return _call(body, name=name, grid=(T // tm,),
                 in_specs=[pl.BlockSpec((nk, tm, kk), lambda i: (0, i, 0)),
                           pl.BlockSpec((None, nk, kk, D), lambda i: (layer, 0, 0, 0)),
                           pl.BlockSpec((1, D), lambda i: (0, 0)),
                           pl.BlockSpec((tm, D), lambda i: (i, 0)),
                           pl.BlockSpec((1, D), lambda i: (0, 0))],
                 out_specs=[pl.BlockSpec((tm, D), lambda i: (i, 0))] * 2,
                 out_shape=[_sds((T, D), F32)] * 2)(a, w, bias.reshape(1, D), x, gain.reshape(1, D))


def _halo_maps(tm, hb, T):
    per = tm // hb
    last = T // hb - 1
    return (lambda i: jnp.maximum(i * per - 1, 0)), (lambda i: jnp.minimum((i + 1) * per, last))


def cm_glu_conv(name, u, dw, dwb, layer, seq):
    _, T, n = u.shape
    ct = dw.shape[-1]
    per = n // ct
    nct = 4 * per
    tm = _tile(seq, 512)
    tps = seq // tm
    hb = CONV_HALO
    prev, _ = _halo_maps(tm, hb, T)
    u4 = u.reshape(2, 4, T, n)

    rows = _tile(tm, 2 * ROW_CHUNK_FWD)

    def body(u_ref, uh_ref, w_ref, b_ref, o_ref, z_ref):
        first = (pl.program_id(0) % tps) == 0
        um = u_ref[...].astype(F32)
        uh = uh_ref[...].astype(F32)
        z_ref[0, pl.ds(hb, tm), :] = um[0] * _sigmoid(um[1])
        z_ref[0, pl.ds(0, hb), :] = jnp.where(first, 0.0, uh[0] * _sigmoid(uh[1]))
        _stage_shifts(z_ref, hb + tm, back=True)
        for r0 in range(0, tm, rows):
            acc = b_ref[...]
            for j in range(CONV_W):
                acc = acc + w_ref[pl.ds(CONV_W - 1 - j, 1), :] * z_ref[j % 8, pl.ds(hb + r0 - (j - j % 8), rows), :]
            o_ref[pl.ds(r0, rows), :] = acc

    return _call(body, name=name, grid=(T // tm, nct),
                 in_specs=[pl.BlockSpec((2, None, tm, ct), lambda i, c: (0, c // per, i, c % per)),
                           pl.BlockSpec((2, None, hb, ct), lambda i, c: (0, c // per, prev(i), c % per)),
                           pl.BlockSpec((None, None, CONV_W, ct), lambda i, c: (c, layer, 0, 0)),
                           pl.BlockSpec((None, None, 1, ct), lambda i, c: (c, layer, 0, 0))],
                 out_specs=pl.BlockSpec((tm, ct), lambda i, c: (i, c)),
                 out_shape=_sds((T, nct * ct), F32),
                 scratch=[pltpu.VMEM((8, tm + hb, ct), F32)])(u4, u4, dw, dwb)


def _stage_shifts(z_ref, n_rows, back):
    for r in range(1, 8):
        if back:
            z_ref[r, pl.ds(8, n_rows - 8), :] = z_ref[0, pl.ds(8 - r, n_rows - 8), :]
        else:
            z_ref[r, pl.ds(0, n_rows - 8), :] = z_ref[0, pl.ds(r, n_rows - 8), :]


def ln_silu(name, c, g, b):
    T, D = c.shape
    tm = _tile(T, 512)

    def body(c_ref, g_ref, b_ref, s_ref):
        cf = c_ref[...]
        mu = jnp.mean(cf, axis=-1, keepdims=True)
        xc = cf - mu
        r = lax.rsqrt(jnp.mean(xc * xc, axis=-1, keepdims=True) + EPS)
        t = xc * r * g_ref[...] + b_ref[...]
        s_ref[...] = (t * _sigmoid(t)).astype(s_ref.dtype)

    return _call(body, name=name, grid=(T // tm,),
                 in_specs=[pl.BlockSpec((tm, D), lambda i: (i, 0)), pl.BlockSpec((1, D), lambda i: (0, 0)),
                           pl.BlockSpec((1, D), lambda i: (0, 0))],
                 out_specs=pl.BlockSpec((None, tm, D), lambda i: (0, i, 0)),
                 out_shape=_sds((1, T, D), MM))(c, g.reshape(1, D), b.reshape(1, D))


def _col_tiles(n, width):
    return [(c0, min(width, n - c0)) for c0 in range(0, n, width)]


def ffn_in_conv(name, x, gain, wt, dw, dwb, layer, seq):
    T, D = x.shape
    n = wt.shape[2]
    tm = _tile(seq, 1024)
    tps = seq // tm
    hb = 8
    rows = _tile(tm, ROW_CHUNK_FWD)
    K = FFN_CONV_W
    w5 = wt.reshape(2, 4, n, D)
    dw4 = dw.reshape(2, 4, dw.shape[1], K, n)
    dwb4 = dwb.reshape(dwb.shape[0], 2, 4, 1, n)

    def body(x_ref, g_ref, w_ref, cw_ref, cb_ref, u_ref, h_ref, z_ref, ag_ref, pad_ref, carry_ref):
        i, j = pl.program_id(0), pl.program_id(1)
        first = (i % tps) == 0

        @pl.when(j == 0)
        def _():
            xf = x_ref[...]
            r = lax.rsqrt(jnp.mean(xf * xf, axis=-1, keepdims=True) + EPS)
            h_ref[...] = (xf * r * g_ref[...]).astype(h_ref.dtype)

        h = h_ref[...]
        for c0, wc in _col_tiles(n, FFN_COL_TILE):
            cols = pl.ds(c0, wc)
            for half in range(2):
                res = _dot_nt(h, w_ref[half, cols, :])
                u_ref[half, :, cols] = res.astype(u_ref.dtype)
                pad_ref[half, pl.ds(hb, tm), cols] = res
                pad_ref[half, pl.ds(0, hb), cols] = jnp.where(first, 0.0, carry_ref[j, half, :, cols])
            for r0 in range(0, tm, rows):
                conv = []
                for half in range(2):
                    acc = cb_ref[half, :, cols] + cw_ref[half, pl.ds(K - 1, 1), cols] * pad_ref[
                        half, pl.ds(hb + r0, rows), cols]
                    for k in range(K - 1):
                        acc = acc + cw_ref[half, pl.ds(k, 1), cols] * pad_ref[
                            half, pl.ds(hb + r0 - (K - 1) + k, rows), cols]
                    conv.append(acc)
                a, g = conv
                sg = _sigmoid(g)
                silu = g * sg
                z_ref[pl.ds(r0, rows), cols] = (silu * a).astype(z_ref.dtype)
                ag_ref[0, pl.ds(r0, rows), cols] = silu.astype(ag_ref.dtype)
                ag_ref[1, pl.ds(r0, rows), cols] = (a * (sg * (1.0 + g * (1.0 - sg)))).astype(ag_ref.dtype)
            for half in range(2):
                carry_ref[j, half, :, cols] = pad_ref[half, pl.ds(tm, hb), cols]

    u, h, z, ag = _call(
        body, name=name, grid=(T // tm, 4),
        in_specs=[pl.BlockSpec((tm, D), lambda i, j: (i, 0)),
                  pl.BlockSpec((1, D), lambda i, j: (0, 0)),
                  pl.BlockSpec((2, None, n, D), lambda i, j: (0, j, 0, 0)),
                  pl.BlockSpec((2, None, None, K, n), lambda i, j: (0, j, layer, 0, 0)),
                  pl.BlockSpec((None, 2, None, 1, n), lambda i, j: (layer, 0, j, 0, 0))],
        out_specs=[pl.BlockSpec((2, None, tm, n), lambda i, j: (0, j, i, 0)),
                   pl.BlockSpec((tm, D), lambda i, j: (i, 0)),
                   pl.BlockSpec((None, tm, n), lambda i, j: (j, i, 0)),
                   pl.BlockSpec((2, None, tm, n), lambda i, j: (0, j, i, 0))],
        out_shape=[_sds((2, 4, T, n), MM), _sds((T, D), MM), _sds((4, T, n), MM), _sds((2, 4, T, n), MM)],
        scratch=[pltpu.VMEM((2, tm + hb, n), F32), pltpu.VMEM((4, 2, hb, n), F32)])(
            x, gain.reshape(1, D), w5, dw4, dwb4)
    return u.reshape(8, T, n), h, z, ag


def _head_specs(seq, dh, q_heads, kv_heads):
    def spec(per, base):
        def imap(b, h, g):
            f = base + g * N_HEADS + h
            return (f // per, b, f % per)
        return pl.BlockSpec((None, seq, dh), imap)
    return spec(q_heads, 0), spec(kv_heads, 0), spec(kv_heads, N_GROUPS * N_HEADS)


def _rows(start, d, blocks=1):
    size = blocks * ATT_BLOCK
    return pl.ds(start, size, stride=d) if d > 1 else pl.ds(start, size)


def _att_pad():
    return max(ATT_BLOCK * d for d in DILATIONS[:-1])


def _band_mask(first, nblk):
    keys = ATT_BLOCK if nblk == 1 else 2 * ATT_BLOCK
    shape = (ATT_BATCH, ATT_BLOCK, keys)
    qi = lax.broadcasted_iota(jnp.int32, shape, 1)
    kj = lax.broadcasted_iota(jnp.int32, shape, 2)
    if nblk == 1:
        return kj <= qi
    n = (first + lax.broadcasted_iota(jnp.int32, shape, 0)) % nblk
    return (kj >= qi) & (kj <= qi + ATT_BLOCK) & ((n > 0) | (kj >= ATT_BLOCK))


def _block_rows(idx, d, nblk):
    r, n = idx // nblk, idx % nblk
    rq = _rows(r + d * ATT_BLOCK * n, d)
    if nblk == 1:
        return rq, _rows(_att_pad() + r + d * ATT_BLOCK * n, d)
    return rq, _rows(_att_pad() + r + d * ATT_BLOCK * (n - 1), d, blocks=2)


def _bdot(a, b, ca, cb):
    return lax.dot_general(a, b, (((ca,), (cb,)), ((0,), (0,))), preferred_element_type=F32)


def attn_fwd(name, q, kv, nb, seq):
    _, T, qn = q.shape
    dh = qn * N_DEV // (N_GROUPS * N_HEADS)
    scale = 1.0 / (dh ** 0.5)
    qs, ks, vs = _head_specs(seq, dh, qn // dh, kv.shape[2] // dh)

    def body(q_ref, k_ref, v_ref, m_ref, l_ref, qf, kf, vf, *branch):
        og, lg = branch[:N_GROUPS], branch[N_GROUPS:]
        pad = _att_pad()
        qf[...] = q_ref[...].astype(F32)
        for t_ref, s_ref in ((k_ref, kf), (v_ref, vf)):
            s_ref[pl.ds(0, pad), :] = jnp.zeros((pad, dh), F32)
            s_ref[pl.ds(pad, seq), :] = t_ref[...].astype(F32)
        for g in range(N_GROUPS):
            d = DILATIONS[g]
            nblk = seq // d // ATT_BLOCK

            def blocks(it, carry, d=d, nblk=nblk, g=g):
                first = it * ATT_BATCH
                rows = [_block_rows(first + b, d, nblk) for b in range(ATT_BATCH)]
                qb = jnp.stack([qf[rq, :] for rq, _ in rows]).astype(MM)
                kb = jnp.stack([kf[rk, :] for _, rk in rows]).astype(MM)
                vb = jnp.stack([vf[rk, :] for _, rk in rows]).astype(MM)
                s = jnp.where(_band_mask(first, nblk), _bdot(qb, kb, 2, 2) * scale, NEG)
                m = jnp.max(s, axis=-1, keepdims=True)
                p = jnp.exp(s - m)
                den = jnp.sum(p, axis=-1, keepdims=True)
                o = _bdot(p.astype(MM), vb, 2, 1) / den
                lse = m + jnp.log(den)
                for b, (rq, _) in enumerate(rows):
                    og[g][rq, :] = o[b]
                    lg[g][rq, :] = jnp.broadcast_to(lse[b], (ATT_BLOCK, dh))
                return carry

            @pl.when(pl.program_id(2) == g)
            def _(blocks=blocks, d=d, nblk=nblk):
                lax.fori_loop(0, d * nblk // ATT_BATCH, blocks, 0)

        @pl.when(pl.program_id(2) == N_GROUPS - 1)
        def _():
            mx = jnp.maximum(jnp.maximum(lg[0][...], lg[1][...]), lg[2][...])
            e = [jnp.exp(lg[g][...] - mx) for g in range(N_GROUPS)]
            tot = e[0] + e[1] + e[2]
            m_ref[...] = ((e[0] * og[0][...] + e[1] * og[1][...] + e[2] * og[2][...]) / tot).astype(m_ref.dtype)
            l_ref[...] = mx + jnp.log(tot)

    return _call(body, name=name, grid=(nb, N_HEADS, N_GROUPS), in_specs=[qs, ks, vs],
                 out_specs=[pl.BlockSpec((None, seq, dh), lambda b, h, g: (0, b, h)),
                            pl.BlockSpec((seq, dh), lambda b, h, g: (b, h))],
                 out_shape=[_sds((1, T, N_HEADS * dh), MM), _sds((T, N_HEADS * dh), F32)],
                 scratch=[pltpu.VMEM((seq, dh), F32)] + [pltpu.VMEM((_att_pad() + seq, dh), F32)] * 2
                 + [pltpu.VMEM((seq, dh), F32)] * (2 * N_GROUPS))(q, kv, kv)


def loss_fwd_bwd(name, x, target):
    T, D = x.shape
    tm = _tile(T, 512)

    def body(x_ref, t_ref, dx_ref, l_ref):
        @pl.when(pl.program_id(0) == 0)
        def _():
            l_ref[...] = jnp.zeros_like(l_ref)
        err = x_ref[...] - t_ref[...]
        dx_ref[...] = err * (1.0 / D)
        l_ref[...] += 0.5 * jnp.sum(jnp.mean(err * err, axis=-1, keepdims=True), axis=0, keepdims=True)

    dx, l = _call(body, name=name, grid=(T // tm,),
                  in_specs=[pl.BlockSpec((tm, D), lambda i: (i, 0))] * 2,
                  out_specs=[pl.BlockSpec((tm, D), lambda i: (i, 0)), pl.BlockSpec((1, 1), lambda i: (0, 0))],
                  out_shape=[_sds((T, D), F32), _sds((1, 1), F32)])(x, target)
    return dx, l


def resid_norm_bwd(name, dx, y, gain):
    T, D = y.shape
    tm = _tile(T, 512)

    def body(dx_ref, y_ref, g_ref, dy_ref, dg_ref, db_ref):
        @pl.when(pl.program_id(0) == 0)
        def _():
            dg_ref[...] = jnp.zeros_like(dg_ref)
            db_ref[...] = jnp.zeros_like(db_ref)
        y = y_ref[...]
        d = dx_ref[...]
        r = lax.rsqrt(jnp.mean(y * y, axis=-1, keepdims=True) + EPS)
        yh = y * r
        dyh = d * g_ref[...]
        dy = r * (dyh - yh * jnp.mean(dyh * yh, axis=-1, keepdims=True))
        dy_ref[...] = dy.astype(dy_ref.dtype)
        dg_ref[...] += jnp.sum(d * yh, axis=0, keepdims=True)
        db_ref[...] += jnp.sum(dy, axis=0, keepdims=True)

    return _call(body, name=name, grid=(T // tm,),
                 in_specs=[pl.BlockSpec((tm, D), lambda i: (i, 0))] * 2 + [pl.BlockSpec((1, D), lambda i: (0, 0))],
                 out_specs=[pl.BlockSpec((None, tm, D), lambda i: (0, i, 0))] + [pl.BlockSpec((1, D), lambda i: (0, 0))] * 2,
                 out_shape=[_sds((1, T, D), MM), _sds((1, D), F32), _sds((1, D), F32)])(dx, y, gain.reshape(1, D))


def mm_nt(name, dy, w, layer, out_dtype):
    _, T, D = dy.shape
    _, nk, kk, _ = w.shape
    tm = _tile(T, 1024)

    def body(dy_ref, w_ref, o_ref):
        o_ref[...] = _dot_nt(dy_ref[...], w_ref[...]).astype(o_ref.dtype)

    return _call(body, name=name, grid=(T // tm, nk),
                 in_specs=[pl.BlockSpec((None, tm, D), lambda i, q: (0, i, 0)),
                           pl.BlockSpec((None, None, kk, D), lambda i, q: (layer, q, 0, 0))],
                 out_specs=pl.BlockSpec((None, tm, kk), lambda i, q: (q, i, 0)),
                 out_shape=_sds((nk, T, kk), out_dtype))(dy, w)


def mm_nt_norm_bwd(name, du, w, layer, x_in, gain, dx_res):
    nsh, T, n = du.shape
    D = x_in.shape[1]
    tm = _tile(T, 512)
    grp = nsh if n <= DGRAD_GROUP_ALL_MAX_N else nsh // 2
    steps = nsh // grp

    def body(du_ref, w_ref, x_ref, g_ref, dr_ref, dx_ref, dg_ref, acc_ref):
        i, j = pl.program_id(0), pl.program_id(1)

        @pl.when((i == 0) & (j == 0))
        def _():
            dg_ref[...] = jnp.zeros_like(dg_ref)

        part = _dot_nt(du_ref[0], w_ref[0])
        for k in range(1, grp):
            part = part + _dot_nt(du_ref[k], w_ref[k])

        @pl.when(j == 0)
        def _():
            acc_ref[...] = part

        @pl.when(j > 0)
        def _():
            acc_ref[...] += part

        @pl.when(j == steps - 1)
        def _():
            x = x_ref[...]
            dh = acc_ref[...]
            r = lax.rsqrt(jnp.mean(x * x, axis=-1, keepdims=True) + EPS)
            xh = x * r
            dxh = dh * g_ref[...]
            dx_ref[...] = dr_ref[...] + r * (dxh - xh * jnp.mean(dxh * xh, axis=-1, keepdims=True))
            dg_ref[...] += jnp.sum(dh * xh, axis=0, keepdims=True)

    return _call(body, name=name, grid=(T // tm, steps),
                 in_specs=[pl.BlockSpec((grp, tm, n), lambda i, j: (j, i, 0)),
                           pl.BlockSpec((grp, None, D, n), lambda i, j: (j, layer, 0, 0)),
                           pl.BlockSpec((tm, D), lambda i, j: (i, 0)),
                           pl.BlockSpec((1, D), lambda i, j: (0, 0)),
                           pl.BlockSpec((tm, D), lambda i, j: (i, 0))],
                 out_specs=[pl.BlockSpec((tm, D), lambda i, j: (i, 0)), pl.BlockSpec((1, D), lambda i, j: (0, 0))],
                 out_shape=[_sds((T, D), F32), _sds((1, D), F32)],
                 scratch=[pltpu.VMEM((tm, D), F32)])(du, w, x_in, gain.reshape(1, D), dx_res)


def mm_tn(name, a, b):
    na, T, ka = a.shape
    nb, _, kb = b.shape
    nj = max(na, nb)

    def body(a_ref, b_ref, o_ref):
        o_ref[...] = _dot_tn(a_ref[...], b_ref[...]).astype(o_ref.dtype)

    return _call(body, name=name, grid=(nj,),
                 in_specs=[pl.BlockSpec((None, T, ka), (lambda j: (j, 0, 0)) if na > 1 else (lambda j: (0, 0, 0))),
                           pl.BlockSpec((None, T, kb), (lambda j: (j, 0, 0)) if nb > 1 else (lambda j: (0, 0, 0)))],
                 out_specs=pl.BlockSpec((None, ka, kb), lambda j: (j, 0, 0)),
                 out_shape=_sds((nj, ka, kb), MM))(a, b)


def ffn_bwd(name, dy, wout, u, ag, dw, wt, x_in, gain, dx_res, layer, seq):
    _, T, D = dy.shape
    n = wt.shape[2]
    tm = _tile(seq, 512)
    tps = seq // tm
    hb = FFN_HALO
    _, nxt = _halo_maps(tm, hb, T)
    K = FFN_CONV_W
    te = tm + hb
    rows = _tile(hb, ROW_CHUNK_BWD)
    u4, w5 = u.reshape(2, 4, T, n), wt.reshape(2, 4, n, D)
    dw4 = dw.reshape(2, 4, dw.shape[1], K, n)
    tiles = _col_tiles(n, FFN_COL_TILE)

    def body(dy_ref, dyn_ref, wo_ref, u_ref, ag_ref, agn_ref, cw_ref, wt_ref, x_ref, g_ref, dr_ref,
             du_ref, ddw_ref, ddb_ref, dx_ref, dg_ref,
             uf_ref, agf_ref, dzf_ref, da_ref, acc_ref):
        i, j = pl.program_id(0), pl.program_id(1)
        last = (i % tps) == tps - 1

        @pl.when((i == 0) & (j == 0))
        def _():
            dg_ref[...] = jnp.zeros_like(dg_ref)

        @pl.when(i == 0)
        def _():
            ddw_ref[j] = jnp.zeros((2, K, n), F32)
            ddb_ref[j] = jnp.zeros((2, 1, n), F32)

        dyt, dyn = dy_ref[...], dyn_ref[...]
        for ci, (c0, wc) in enumerate(tiles):
            cols = pl.ds(c0, wc)
            dzf_ref[pl.ds(0, tm), cols] = _dot_nt(dyt, wo_ref[cols, :])
            dzf_ref[pl.ds(tm, hb), cols] = jnp.where(last, 0.0, _dot_nt(dyn, wo_ref[cols, :]))
            for half in range(2):
                uf_ref[half, :, cols] = u_ref[half, :, cols].astype(F32)
                agf_ref[half, pl.ds(0, tm), cols] = ag_ref[half, :, cols].astype(F32)
                agf_ref[half, pl.ds(tm, hb), cols] = agn_ref[half, :, cols].astype(F32)
            for r0 in range(0, te, rows):
                dzc = dzf_ref[pl.ds(r0, rows), cols]
                da_ref[0, pl.ds(r0, rows), cols] = dzc * agf_ref[0, pl.ds(r0, rows), cols]
                da_ref[1, pl.ds(r0, rows), cols] = dzc * agf_ref[1, pl.ds(r0, rows), cols]
            for half in range(2):
                tap_acc = [jnp.zeros((rows, wc), F32) for _ in range(K)]
                bias_acc = jnp.zeros((rows, wc), F32)
                for r0 in range(0, tm, rows):
                    xr = uf_ref[half, pl.ds(r0, rows), cols]
                    acc = None
                    for k in range(K):
                        d = da_ref[half, pl.ds(r0 + K - 1 - k, rows), cols]
                        term = cw_ref[half, pl.ds(k, 1), cols] * d
                        acc = term if acc is None else acc + term
                        tap_acc[k] = tap_acc[k] + d * xr
                        if k == K - 1:
                            bias_acc = bias_acc + d
                    du_ref[half, pl.ds(r0, rows), cols] = acc.astype(du_ref.dtype)
                for k in range(K):
                    ddw_ref[j, half, pl.ds(k, 1), cols] += jnp.sum(tap_acc[k], axis=0, keepdims=True)
                ddb_ref[j, half, :, cols] += jnp.sum(bias_acc, axis=0, keepdims=True)
            part = _dot(du_ref[0, :, cols], wt_ref[0, cols, :]) + _dot(du_ref[1, :, cols], wt_ref[1, cols, :])
            if ci == 0:
                acc_ref[...] = part + jnp.where(j == 0, 0.0, acc_ref[...])
            else:
                acc_ref[...] += part
        @pl.when(j == 3)
        def _():
            x = x_ref[...]
            dh = acc_ref[...]
            r = lax.rsqrt(jnp.mean(x * x, axis=-1, keepdims=True) + EPS)
            xh = x * r
            dxh = dh * g_ref[...]
            dx_ref[...] = dr_ref[...] + r * (dxh - xh * jnp.mean(dxh * xh, axis=-1, keepdims=True))
            dg_ref[...] += jnp.sum(dh * xh, axis=0, keepdims=True)

    f32 = lambda *shape: pltpu.VMEM(shape, F32)
    du, ddw, ddb, dx, dg = _call(
        body, name=name, grid=(T // tm, 4),
        in_specs=[pl.BlockSpec((None, tm, D), lambda i, j: (0, i, 0)),
                  pl.BlockSpec((None, hb, D), lambda i, j: (0, nxt(i), 0)),
                  pl.BlockSpec((None, None, n, D), lambda i, j: (0, j, 0, 0)),
                  pl.BlockSpec((2, None, tm, n), lambda i, j: (0, j, i, 0)),
                  pl.BlockSpec((2, None, tm, n), lambda i, j: (0, j, i, 0)),
                  pl.BlockSpec((2, None, hb, n), lambda i, j: (0, j, nxt(i), 0)),
                  pl.BlockSpec((2, None, None, K, n), lambda i, j: (0, j, layer, 0, 0)),
                  pl.BlockSpec((2, None, n, D), lambda i, j: (0, j, 0, 0)),
                  pl.BlockSpec((tm, D), lambda i, j: (i, 0)),
                  pl.BlockSpec((1, D), lambda i, j: (0, 0)),
                  pl.BlockSpec((tm, D), lambda i, j: (i, 0))],
        out_specs=[pl.BlockSpec((2, None, tm, n), lambda i, j: (0, j, i, 0)),
                   pl.BlockSpec((4, 2, K, n), lambda i, j: (0, 0, 0, 0)),
                   pl.BlockSpec((4, 2, 1, n), lambda i, j: (0, 0, 0, 0)),
                   pl.BlockSpec((tm, D), lambda i, j: (i, 0)),
                   pl.BlockSpec((1, D), lambda i, j: (0, 0))],
        out_shape=[_sds((2, 4, T, n), MM), _sds((4, 2, K, n), F32), _sds((4, 2, 1, n), F32), _sds((T, D), F32),
                   _sds((1, D), F32)],
        scratch=[f32(2, tm, n), f32(2, te, n), f32(te, n), f32(2, te, n), f32(tm, D)],
    )(dy, dy, wout, u4, ag, ag, dw4, w5, x_in, gain.reshape(1, D), dx_res)
    return du.reshape(8, T, n), jnp.swapaxes(ddw, 0, 1), jnp.swapaxes(ddb, 0, 1), dx, dg


def ln_silu_bwd(name, ds, c, g, b):
    T, D = c.shape
    tm = _tile(T, 512)

    def body(ds_ref, c_ref, g_ref, b_ref, dc_ref, dg_ref, db_ref):
        @pl.when(pl.program_id(0) == 0)
        def _():
            dg_ref[...] = jnp.zeros_like(dg_ref)
            db_ref[...] = jnp.zeros_like(db_ref)
        cf = c_ref[...]
        mu = jnp.mean(cf, axis=-1, keepdims=True)
        xc = cf - mu
        r = lax.rsqrt(jnp.mean(xc * xc, axis=-1, keepdims=True) + EPS)
        xh = xc * r
        t = xh * g_ref[...] + b_ref[...]
        sg = _sigmoid(t)
        dt = ds_ref[...] * (sg * (1.0 + t * (1.0 - sg)))
        dg_ref[...] += jnp.sum(dt * xh, axis=0, keepdims=True)
        db_ref[...] += jnp.sum(dt, axis=0, keepdims=True)
        dxh = dt * g_ref[...]
        dc_ref[...] = r * (dxh - jnp.mean(dxh, axis=-1, keepdims=True)
                           - xh * jnp.mean(dxh * xh, axis=-1, keepdims=True))

    vec = pl.BlockSpec((1, D), lambda i: (0, 0))
    return _call(body, name=name, grid=(T // tm,),
                 in_specs=[pl.BlockSpec((None, tm, D), lambda i: (0, i, 0)), pl.BlockSpec((tm, D), lambda i: (i, 0)),
                           vec, vec],
                 out_specs=[pl.BlockSpec((tm, D), lambda i: (i, 0)), vec, vec],
                 out_shape=[_sds((T, D), F32), _sds((1, D), F32), _sds((1, D), F32)])(
                     ds, c, g.reshape(1, D), b.reshape(1, D))


def cm_glu_conv_bwd(name, u, dc, dw, layer, seq):
    _, T, n = u.shape
    ct = dw.shape[-1]
    per = n // ct
    nct = 4 * per
    tm = _tile(seq, 512)
    tps = seq // tm
    hb = CONV_HALO
    prev, nxt = _halo_maps(tm, hb, T)
    u4 = u.reshape(2, 4, T, n)
    K = CONV_W

    rows = _tile(tm, 2 * ROW_CHUNK_FWD)
    wrows = 8

    def body(u_ref, up_ref, dc_ref, dcn_ref, w_ref, du_ref, ddw_ref, ddb_ref, dbi_ref, zp_ref, zd_ref):
        i = pl.program_id(1)
        first = (i % tps) == 0
        last = (i % tps) == tps - 1

        @pl.when(i == 0)
        def _():
            ddw_ref[...] = jnp.zeros_like(ddw_ref)
            ddb_ref[...] = jnp.zeros_like(ddb_ref)
            dbi_ref[...] = jnp.zeros_like(dbi_ref)

        uh = up_ref[...].astype(F32)
        zp_ref[0, pl.ds(hb, tm), :] = u_ref[0].astype(F32) * _sigmoid(u_ref[1].astype(F32))
        zp_ref[0, pl.ds(0, hb), :] = jnp.where(first, 0.0, uh[0] * _sigmoid(uh[1]))
        zd_ref[0, pl.ds(0, tm), :] = dc_ref[...]
        zd_ref[0, pl.ds(tm, hb), :] = jnp.where(last, 0.0, dcn_ref[...])
        _stage_shifts(zp_ref, hb + tm, back=True)
        _stage_shifts(zd_ref, tm + hb, back=False)
        for r0 in range(0, tm, rows):
            dp = None
            for j in range(K):
                term = w_ref[pl.ds(K - 1 - j, 1), :] * zd_ref[j % 8, pl.ds(r0 + (j - j % 8), rows), :]
                dp = term if dp is None else dp + term
            v = u_ref[0, pl.ds(r0, rows), :].astype(F32)
            sg = _sigmoid(u_ref[1, pl.ds(r0, rows), :].astype(F32))
            dv = dp * sg
            dg = dp * v * sg * (1.0 - sg)
            du_ref[0, pl.ds(r0, rows), :] = dv.astype(du_ref.dtype)
            du_ref[1, pl.ds(r0, rows), :] = dg.astype(du_ref.dtype)
            dbi_ref[0] += jnp.sum(dv, axis=0, keepdims=True)
            dbi_ref[1] += jnp.sum(dg, axis=0, keepdims=True)
        tap_acc = [jnp.zeros((wrows, ct), F32) for _ in range(K)]
        bias_acc = jnp.zeros((wrows, ct), F32)
        for r0 in range(0, tm, wrows):
            d = zd_ref[0, pl.ds(r0, wrows), :]
            for j in range(K):
                tap_acc[j] = tap_acc[j] + d * zp_ref[j % 8, pl.ds(hb + r0 - (j - j % 8), wrows), :]
            bias_acc = bias_acc + d
        for j in range(K):
            ddw_ref[pl.ds(K - 1 - j, 1), :] += jnp.sum(tap_acc[j], axis=0, keepdims=True)
        ddb_ref[...] += jnp.sum(bias_acc, axis=0, keepdims=True)

    du, ddw, ddb, dbi = _call(
        body, name=name, grid=(nct, T // tm),
        in_specs=[pl.BlockSpec((2, None, tm, ct), lambda c, i: (0, c // per, i, c % per)),
                  pl.BlockSpec((2, None, hb, ct), lambda c, i: (0, c // per, prev(i), c % per)),
                  pl.BlockSpec((tm, ct), lambda c, i: (i, c)),
                  pl.BlockSpec((hb, ct), lambda c, i: (nxt(i), c)),
                  pl.BlockSpec((None, None, K, ct), lambda c, i: (c, layer, 0, 0))],
        out_specs=[pl.BlockSpec((2, None, tm, ct), lambda c, i: (0, c // per, i, c % per)),
                   pl.BlockSpec((None, K, ct), lambda c, i: (c, 0, 0)),
                   pl.BlockSpec((None, 1, ct), lambda c, i: (c, 0, 0)),
                   pl.BlockSpec((2, None, 1, ct), lambda c, i: (0, c // per, 0, c % per))],
        out_shape=[_sds((2, 4, T, n), MM), _sds((nct, K, ct), F32), _sds((nct, 1, ct), F32), _sds((2, 4, 1, n), F32)],
        scratch=[pltpu.VMEM((8, tm + hb, ct), F32), pltpu.VMEM((8, tm + hb, ct), F32)])(u4, u4, dc, dc, dw)
    return du.reshape(8, T, n), ddw, ddb, dbi


def attn_bwd(name, q, kv, dm, merged, lse, dkv_prev, nb, seq):
    _, T, qn = q.shape
    kn = kv.shape[2]
    dh = qn * N_DEV // (N_GROUPS * N_HEADS)
    scale = 1.0 / (dh ** 0.5)
    qs, ks, vs = _head_specs(seq, dh, qn // dh, kn // dh)
    has_prev = dkv_prev is not None
    n_in = 6 + (1 if has_prev else 0)

    def body(*refs):
        q_ref, k_ref, v_ref, dm_ref, mg_ref, l_ref = refs[:6]
        pkv_ref = refs[6] if has_prev else None
        dq_ref, dkv_ref = refs[n_in:n_in + 2]
        qf, kf, vf, dqf, dkf, dvf, dlt = refs[n_in + 2:]
        pad = _att_pad()

        @pl.when(pl.program_id(2) == 0)
        def _():
            dlt[...] = jnp.broadcast_to(
                jnp.sum(dm_ref[...] * mg_ref[...].astype(F32), axis=-1, keepdims=True), (seq, dh))

        qf[...] = q_ref[...].astype(F32)
        for t_ref, s_ref in ((k_ref, kf), (v_ref, vf)):
            s_ref[pl.ds(0, pad), :] = jnp.zeros((pad, dh), F32)
            s_ref[pl.ds(pad, seq), :] = t_ref[...].astype(F32)
        dkf[...] = jnp.zeros_like(dkf)
        dvf[...] = jnp.zeros_like(dvf)
        for g in range(N_GROUPS):
            d = DILATIONS[g]
            nblk = seq // d // ATT_BLOCK

            def blocks(it, carry, d=d, nblk=nblk):
                first = it * ATT_BATCH
                rows = [_block_rows(first + b, d, nblk) for b in range(ATT_BATCH)]
                qb = jnp.stack([qf[rq, :] for rq, _ in rows]).astype(MM)
                dmb = jnp.stack([dm_ref[rq, :] for rq, _ in rows]).astype(MM)
                lse = jnp.stack([l_ref[rq, :][:, :1] for rq, _ in rows])
                delta = jnp.stack([dlt[rq, :][:, :1] for rq, _ in rows])
                kb = jnp.stack([kf[rk, :] for _, rk in rows]).astype(MM)
                vb = jnp.stack([vf[rk, :] for _, rk in rows]).astype(MM)
                s = jnp.where(_band_mask(first, nblk), _bdot(qb, kb, 2, 2) * scale, NEG)
                p = jnp.exp(s - lse)
                dsc = (p * (_bdot(dmb, vb, 2, 2) - delta) * scale).astype(MM)
                dv = _bdot(p.astype(MM), dmb, 1, 1)
                dk = _bdot(dsc, qb, 1, 1)
                dq = _bdot(dsc, kb, 2, 1)
                for b, (rq, rk) in enumerate(rows):
                    dqf[rq, :] = dq[b]
                    dkf[rk, :] += dk[b]
                    dvf[rk, :] += dv[b]
                return carry

            @pl.when(pl.program_id(2) == g)
            def _(blocks=blocks, d=d, nblk=nblk):
                lax.fori_loop(0, d * nblk // ATT_BATCH, blocks, 0)

        dq_ref[...] = dqf[...].astype(dq_ref.dtype)
        dk, dv = dkf[pl.ds(pad, seq), :], dvf[pl.ds(pad, seq), :]
        if has_prev:
            dk, dv = dk + pkv_ref[0].astype(F32), dv + pkv_ref[1].astype(F32)
        dkv_ref[0] = dk.astype(dkv_ref.dtype)
        dkv_ref[1] = dv.astype(dkv_ref.dtype)

    per = kn // dh

    def both(b, h, g):
        f = g * N_HEADS + h
        return (0, f // per, b, f % per)

    kv_spec = pl.BlockSpec((2, None, seq, dh), both)
    full = pl.BlockSpec((None, seq, dh), lambda b, h, g: (0, b, h))
    in_specs = [qs, ks, vs, full, full, pl.BlockSpec((seq, dh), lambda b, h, g: (b, h))]
    args = [q, kv, kv, dm, merged, lse]
    if has_prev:
        in_specs.append(kv_spec)
        args.append(dkv_prev.reshape(2, N_DEV // 2, T, kn))
    short, padded = pltpu.VMEM((seq, dh), F32), pltpu.VMEM((_att_pad() + seq, dh), F32)
    dq, dkv = _call(body, name=name, grid=(nb, N_HEADS, N_GROUPS), in_specs=in_specs, out_specs=[qs, kv_spec],
                    out_shape=[_sds(q.shape, MM), _sds((2, N_DEV // 2, T, kn), MM)],
                    scratch=[short, padded, padded, short, padded, padded, short])(*args)
    return dq, dkv.reshape(N_DEV, T, kn)


def _adamw_math(w, g, m, v):
    m = ADAM_B1 * m + (1.0 - ADAM_B1) * g
    v = ADAM_B2 * v + (1.0 - ADAM_B2) * (g * g)
    m_hat = m / (1.0 - ADAM_B1 ** ADAM_STEP)
    v_hat = v / (1.0 - ADAM_B2 ** ADAM_STEP)
    delta = -ADAM_LR * (m_hat / (jnp.sqrt(v_hat) + ADAM_EPS) + ADAM_WD * w)
    return delta, m, v


def adamw_sum(name, w, m, v, parts, after):
    L, R, C = w.shape
    tr = _tile(R, 256)

    def body(*refs):
        w_ref, m_ref, v_ref = refs[:3]
        p_refs = refs[3:3 + L]
        g_ref, d_ref, nm_ref, nv_ref = refs[4 + L:]
        for l in range(L):
            @pl.when(pl.program_id(0) == l)
            def _(p_ref=p_refs[l]):
                g = p_ref[0].astype(F32)
                for k in range(1, N_DEV):
                    g = g + p_ref[k].astype(F32)
                g_ref[...] = g
                d_ref[...], nm_ref[...], nv_ref[...] = _adamw_math(w_ref[...], g, m_ref[...], v_ref[...])

    blk = pl.BlockSpec((None, tr, C), lambda l, i: (l, i, 0))
    part = lambda k: pl.BlockSpec((N_DEV, tr, C), lambda l, i: (0, jnp.where(l == k, i, 0), 0))
    return _call(body, name=name, grid=(L, R // tr),
                 in_specs=[blk, blk, blk] + [part(k) for k in range(L)] + [pl.BlockSpec(memory_space=pl.ANY)],
                 out_specs=[blk] * 4, out_shape=[_sds((L, R, C), F32)] * 4)(w, m, v, *parts, after)


def sum_partials(name, parts):
    _, R, C = parts.shape
    tr = _tile(R, 512)

    def body(p_ref, o_ref):
        g = p_ref[0]
        for k in range(1, N_DEV):
            g = g + p_ref[k]
        o_ref[...] = g

    return _call(body, name=name, grid=(R // tr,),
                 in_specs=[pl.BlockSpec((N_DEV, tr, C), lambda i: (0, i, 0))],
                 out_specs=pl.BlockSpec((tr, C), lambda i: (i, 0)), out_shape=_sds((R, C), F32))(parts)


def adamw_small(name, w, g, m, v):
    R, C = w.shape
    tr = _tile(R, 512)

    def body(w_ref, g_ref, m_ref, v_ref, d_ref, nm_ref, nv_ref):
        d_ref[...], nm_ref[...], nv_ref[...] = _adamw_math(w_ref[...], g_ref[...], m_ref[...], v_ref[...])

    blk = pl.BlockSpec((tr, C), lambda i: (i, 0))
    return _call(body, name=name, grid=(R // tr,), in_specs=[blk] * 4, out_specs=[blk] * 3,
                 out_shape=[_sds((R, C), F32)] * 3)(w, g, m, v)


def _pack(arrays):
    pieces = []
    for a in arrays:
        f = a.reshape(-1).astype(F32)
        pieces.append(jnp.pad(f, (0, (-f.shape[0]) % PACK)))
    return jnp.concatenate(pieces).reshape(-1, 128)


def _unpack(flat, shapes):
    out, off = [], 0
    f = flat.reshape(-1)
    for s in shapes:
        size = 1
        for d in s:
            size *= d
        out.append(f[off:off + size].reshape(s))
        off += size + (-size) % PACK
    return out


def kernel(x, mix_pre_g, mix_post_g, ffn_pre_g, ffn_post_g, cm_w_in, cm_b_in, cm_dw, cm_dw_b, cm_ln_g, cm_ln_b, cm_w_out, cm_b_out, kv_norm_g, w_kv, w_q, w_o, ffn_w_in, ffn_dw, ffn_dw_b, ffn_w_out, loss_target, m_mix_pre_g, m_mix_post_g, m_ffn_pre_g, m_ffn_post_g, m_cm_w_in, m_cm_b_in, m_cm_dw, m_cm_dw_b, m_cm_ln_g, m_cm_ln_b, m_cm_w_out, m_cm_b_out, m_kv_norm_g, m_w_kv, m_w_q, m_w_o, m_ffn_w_in, m_ffn_dw, m_ffn_dw_b, m_ffn_w_out, v_mix_pre_g, v_mix_post_g, v_ffn_pre_g, v_ffn_post_g, v_cm_w_in, v_cm_b_in, v_cm_dw, v_cm_dw_b, v_cm_ln_g, v_cm_ln_b, v_cm_w_out, v_cm_b_out, v_kv_norm_g, v_w_kv, v_w_q, v_w_o, v_ffn_w_in, v_ffn_dw, v_ffn_dw_b, v_ffn_w_out):
    nb, seq, D = x.shape
    T = nb * seq
    me = _my_index()
    n_b = DEPTH - N_A

    nf = ffn_w_in.shape[-1]
    t_ = lambda t: jnp.swapaxes(t, 1, 2)
    fin_t, m_fin_t, v_fin_t = t_(ffn_w_in), t_(m_ffn_w_in), t_(v_ffn_w_in)

    stages = [(part, i) for i in range(DEPTH) for part in ("mix", "ffn")]
    stages.insert(stages.index(("ffn", N_A - 1)) + 1, ("kv", N_A - 1))

    def stage_sources(stage):
        part, i = stage
        if part == "ffn":
            src = {"fin": fin_t[i], "fout": ffn_w_out[i]}
        elif part == "kv":
            src = {"kv": w_kv}
        elif i < N_A:
            src = {"cin": cm_w_in[i], "cout": cm_w_out[i]}
        else:
            src = {"q": w_q[i - N_A], "o": w_o[i - N_A]}
        return {k: t.astype(MM) for k, t in src.items()}

    def begin_gather(stage, after):
        src = stage_sources(stage)
        names, arrays = list(src), list(src.values())
        tag = f"{stage[0]}{stage[1]}"
        lands = place_own(f"gather_own_{tag}", "gather", arrays)
        handle, token = exchange_begin(f"gather_begin_{tag}", "gather", arrays, lands, after)
        return (names, handle), token

    def end_gather(stage, pending, after):
        names, handle = pending
        W = dict(zip(names, exchange_end(f"gather_end_{stage[0]}{stage[1]}", handle, after)))
        for k in W:
            if k in ("cout", "o"):
                W[k] = W[k].reshape(1, 1, D, D)
            elif k == "fout":
                W[k] = W[k].reshape(1, 4, nf, D)
            else:
                W[k] = W[k][:, None]
        return W

    small = [cm_b_in[:, None, :], cm_dw, cm_dw_b[:, None, :], cm_ln_g, cm_ln_b, cm_b_out, ffn_dw]
    Bcin, DWc, DWBc, LNg, LNb, Bcout, DWf = all_gather("gather_small", small, [False] * len(small))
    LNg = jnp.swapaxes(LNg, 0, 1).reshape(N_A, D)
    LNb = jnp.swapaxes(LNb, 0, 1).reshape(N_A, D)
    Bcout = jnp.swapaxes(Bcout, 0, 1).reshape(N_A, D)
    DWBf = ffn_dw_b.reshape(DEPTH, N_DEV, 1, nf)
    zero_bias = jnp.zeros((D,), F32)

    xs = x.reshape(T, D)
    sv = []
    kv = hkv = None
    pending = {}
    pending[stages[0]], _ = begin_gather(stages[0], DWf)
    Ws = {stages[0]: end_gather(stages[0], pending[stages[0]], xs)}
    pending[stages[1]], token = begin_gather(stages[1], next(iter(Ws[stages[0]].values())))
    sv = [{} for _ in range(DEPTH)]
    for idx, stage in enumerate(stages):
        part, i = stage
        L, W = sv[i], Ws[stage]
        gain = {"mix": mix_pre_g[i], "ffn": ffn_pre_g[i], "kv": kv_norm_g}[part]
        following = stages[idx + 1] if idx + 1 < len(stages) else None
        if idx + 2 < len(stages):
            after = token if idx == 0 else next(iter(W.values()))
            pending[stages[idx + 2]], token = begin_gather(stages[idx + 2], after)
        if idx + 2 < len(stages) or idx == 0:
            gain = gain + token[0, 0]
        if part == "mix":
            L["x_in"] = xs
            if i < N_A:
                L["u"], L["h"] = norm_mm(f"cm_in_{i}", xs, gain, W["cin"], 0, Bcin[:, i:i + 1])
                L["c"] = cm_glu_conv(f"cm_conv_{i}", L["u"], DWc, DWBc, i, seq)
                L["s"] = ln_silu(f"cm_ln_{i}", L["c"], LNg[i], LNb[i])
                L["y"], xs = mm_resid_norm(f"cm_out_{i}", L["s"], W["cout"], 0, Bcout[i], xs, mix_post_g[i])
            else:
                L["q"], L["h"] = norm_mm(f"attn_q_{i}", xs, gain, W["q"], 0)
                L["mg"], L["lse"] = attn_fwd(f"attn_{i}", L["q"], kv, nb, seq)
                L["y"], xs = mm_resid_norm(f"attn_out_{i}", L["mg"], W["o"], 0, zero_bias, xs, mix_post_g[i])
            L["x1"] = xs
        elif part == "ffn":
            L["uf"], L["hf"], L["z"], L["ag"] = ffn_in_conv(f"ffn_in_{i}", xs, gain, W["fin"], DWf, DWBf, i, seq)
            L["yf"], xs = mm_resid_norm(f"ffn_out_{i}", L["z"], W["fout"], 0, zero_bias, xs, ffn_post_g[i])
        else:
            kv, hkv = norm_mm("kv_proj", xs, gain, W["kv"], 0)
        if following is not None:
            Ws[following] = end_gather(following, pending[following], kv if part == "kv" else xs)
    dx, loss_part = loss_fwd_bwd("loss", xs, loss_target.reshape(T, D))

    g_mix_pre, g_mix_post, g_ffn_pre, g_ffn_post = [None] * DEPTH, [None] * DEPTH, [None] * DEPTH, [None] * DEPTH
    g_ffn_dw, g_ffn_dwb = [None] * DEPTH, [None] * DEPTH
    g_cbin, g_cdw, g_cdwb, g_lng, g_lnb, g_cbout = ([None] * N_A for _ in range(6))
    g_kvn = dkv = None
    landed = [{} for _ in range(DEPTH)]
    in_flight = token = None
    for stage in reversed(stages):
        part, i = stage
        L, W = sv[i], Ws[stage]
        gain = {"mix": mix_post_g[i], "ffn": ffn_post_g[i], "kv": kv_norm_g}[part]
        if token is not None:
            gain = gain + token[0, 0]
        send = {}
        if part == "kv":
            send["kv"] = mm_tn("kv_wg", hkv[None], dkv)
            dx, g_kvn = mm_nt_norm_bwd("kv_bwd", dkv, W["kv"], 0, sv[i + 1]["x_in"], gain, dx)
        elif part == "ffn":
            dyf, g_ffn_post[i], _ = resid_norm_bwd(f"ffn_post_bwd_{i}", dx, L["yf"], gain)
            send["fout"] = mm_tn(f"ffn_out_wg_{i}", L["z"], dyf).reshape(N_DEV, nf // 2, D)
            duf, ddw, ddwb, dx, g_ffn_pre[i] = ffn_bwd(f"ffn_bwd_{i}", dyf, W["fout"], L["uf"], L["ag"], DWf, W["fin"],
                                                       L["x1"], ffn_pre_g[i], dx, i, seq)
            g_ffn_dw[i], g_ffn_dwb[i] = ddw.reshape(N_DEV, FFN_CONV_W, nf), ddwb.reshape(-1)
            send["fin"] = mm_tn(f"ffn_in_wg_{i}", duf, L["hf"][None])
        else:
            dy, g_mix_post[i], dyb = resid_norm_bwd(f"mix_post_bwd_{i}", dx, L["y"], gain)
            if i >= N_A:
                dm = mm_nt(f"attn_out_bwd_{i}", dy, W["o"], 0, F32)
                send["o"] = mm_tn(f"attn_out_wg_{i}", L["mg"], dy).reshape(N_DEV, D // N_DEV, D)
                dq, dkv = attn_bwd(f"attn_bwd_{i}", L["q"], kv, dm, L["mg"], L["lse"], dkv, nb, seq)
                send["q"] = mm_tn(f"attn_q_wg_{i}", L["h"][None], dq)
                dx, g_mix_pre[i] = mm_nt_norm_bwd(f"attn_q_bwd_{i}", dq, W["q"], 0, L["x_in"], mix_pre_g[i], dx)
            else:
                g_cbout[i] = dyb
                ds = mm_nt(f"cm_out_bwd_{i}", dy, W["cout"], 0, F32)
                send["cout"] = mm_tn(f"cm_out_wg_{i}", L["s"], dy).reshape(N_DEV, D // N_DEV, D)
                ln_gain = LNg[i]
                if stage == stages[0]:
                    early = [send.pop("cout")]
                    lands = place_own("scatter_own_cout0", "scatter", early)
                    early_handle, early_token = exchange_begin("scatter_begin_cout0", "scatter", early, lands, ds)
                    ln_gain = ln_gain + early_token[0, 0]
                dc, g_lng[i], g_lnb[i] = ln_silu_bwd(f"cm_ln_bwd_{i}", ds, L["c"], ln_gain, LNb[i])
                du, g_cdw[i], g_cdwb[i], dbi = cm_glu_conv_bwd(f"cm_conv_bwd_{i}", L["u"], dc, DWc, i, seq)
                g_cbin[i] = dbi.reshape(N_DEV, -1)
                send["cin"] = mm_tn(f"cm_in_wg_{i}", L["h"][None], du)
                dx, g_mix_pre[i] = mm_nt_norm_bwd(f"cm_in_bwd_{i}", du, W["cin"], 0, L["x_in"], mix_pre_g[i], dx)
        if in_flight is not None:
            (p, j), names, handle = in_flight
            landed[j].update(zip(names, exchange_end(f"scatter_end_{p}{j}", handle, dx)))
        names, arrays = list(send), list(send.values())
        lands = place_own(f"scatter_own_{part}{i}", "scatter", arrays)
        handle, token = exchange_begin(f"scatter_begin_{part}{i}", "scatter", arrays, lands, dx)
        in_flight = (stage, names, handle)
    grad_x = dx.reshape(nb, seq, D)

    rep_names = ["mix_pre_g", "mix_post_g", "ffn_pre_g", "ffn_post_g", "kv_norm_g", "ffn_dw_b"]
    rep_parts = [jnp.concatenate(g_mix_pre), jnp.concatenate(g_mix_post), jnp.concatenate(g_ffn_pre),
                 jnp.concatenate(g_ffn_post), g_kvn.reshape(-1), jnp.stack(g_ffn_dwb)]
    rep_w = [mix_pre_g, mix_post_g, ffn_pre_g, ffn_post_g, kv_norm_g, ffn_dw_b]
    rep_m = [m_mix_pre_g, m_mix_post_g, m_ffn_pre_g, m_ffn_post_g, m_kv_norm_g, m_ffn_dw_b]
    rep_v = [v_mix_pre_g, v_mix_post_g, v_ffn_pre_g, v_ffn_post_g, v_kv_norm_g, v_ffn_dw_b]
    sh_names = ["ffn_dw", "cm_b_in", "cm_dw", "cm_dw_b", "cm_ln_g", "cm_ln_b", "cm_b_out"]
    own = lambda per_layer, shard: jnp.stack([p.reshape((N_DEV,) + shard) for p in per_layer], axis=1)
    sh_parts = [own(g_ffn_dw, ffn_dw.shape[1:]), own(g_cbin, cm_b_in.shape[1:]), own(g_cdw, cm_dw.shape[1:]),
                own(g_cdwb, cm_dw_b.shape[1:]), own(g_lng, cm_ln_g.shape[1:]), own(g_lnb, cm_ln_b.shape[1:]),
                own(g_cbout, cm_b_out.shape[1:])]
    sh_w = [ffn_dw, cm_b_in, cm_dw, cm_dw_b, cm_ln_g, cm_ln_b, cm_b_out]
    sh_m = [m_ffn_dw, m_cm_b_in, m_cm_dw, m_cm_dw_b, m_cm_ln_g, m_cm_ln_b, m_cm_b_out]
    sh_v = [v_ffn_dw, v_cm_b_in, v_cm_dw, v_cm_dw_b, v_cm_ln_g, v_cm_ln_b, v_cm_b_out]
    rep_pack = _pack([loss_part] + rep_parts)
    sh_pack = jnp.stack([_pack([p[k] for p in sh_parts]) for k in range(N_DEV)])
    n_rep = rep_pack.shape[0]
    small = []
    for kind, pack in (("gather", rep_pack), ("scatter", sh_pack)):
        lands = place_own(f"{kind}_own_small", kind, [pack])
        handle, token = exchange_begin(f"{kind}_begin_small", kind, [pack], lands, token)
        small.append((kind, handle))

    def big_update(name, w, m, v, key, layers, after):
        as3 = lambda t: t.reshape((-1,) + t.shape[-2:])
        outs = adamw_sum(name, as3(w), as3(m), as3(v), [landed[i][key] for i in layers], after)
        return [t.reshape(w.shape) for t in outs]

    conf, attn = range(N_A), range(N_A, DEPTH)
    upd = {}
    upd["ffn_w_in"] = [t_(t) for t in big_update("adam_ffn_w_in", fin_t, m_fin_t, v_fin_t, "fin", range(DEPTH), token)]
    upd["ffn_w_out"] = big_update("adam_ffn_w_out", ffn_w_out, m_ffn_w_out, v_ffn_w_out, "fout", range(DEPTH), token)
    upd["w_kv"] = big_update("adam_w_kv", w_kv, m_w_kv, v_w_kv, "kv", [N_A - 1], token)
    upd["w_q"] = big_update("adam_w_q", w_q, m_w_q, v_w_q, "q", attn, token)
    upd["w_o"] = big_update("adam_w_o", w_o, m_w_o, v_w_o, "o", attn, upd["w_q"][0])
    (p, j), names, handle = in_flight
    landed[j].update(zip(names, exchange_end(f"scatter_end_{p}{j}", handle, upd["w_o"][0])))
    landed[0]["cout"] = exchange_end("scatter_end_cout0", early_handle, upd["w_o"][0])[0]
    upd["cm_w_in"] = big_update("adam_cm_w_in", cm_w_in, m_cm_w_in, v_cm_w_in, "cin", conf, token)
    upd["cm_w_out"] = big_update("adam_cm_w_out", cm_w_out, m_cm_w_out, v_cm_w_out, "cout", conf, upd["cm_w_in"][0])
    (rep_landed,), (sh_landed,) = (exchange_end(f"{kind}_end_small", handle, upd["cm_w_out"][0])
                                   for kind, handle in small)
    rep_sum = sum_partials("sum_small_rep", rep_landed)
    sh_sum = sum_partials("sum_small_sh", sh_landed)
    rep_shapes = [(1, 1)] + [w.shape for w in rep_w]
    sh_shapes = [w.shape for w in sh_w]
    g_small = jnp.concatenate([rep_sum, sh_sum])
    pad1 = jnp.zeros((1, 1), F32)
    d_s, m_s, v_s = adamw_small("adam_small", jnp.concatenate([_pack([pad1] + rep_w), _pack(sh_w)]), g_small,
                                jnp.concatenate([_pack([pad1] + rep_m), _pack(sh_m)]),
                                jnp.concatenate([_pack([pad1] + rep_v), _pack(sh_v)]))
    split = lambda t: (_unpack(t[:n_rep], rep_shapes), _unpack(t[n_rep:], sh_shapes))
    for (rep_t, sh_t), slot in zip([split(g_small), split(d_s), split(m_s), split(v_s)], range(4)):
        if slot == 0:
            loss = rep_t[0].reshape(())
        for name, t in zip(rep_names, rep_t[1:]):
            upd.setdefault(name, [None] * 4)[slot] = t
        for name, t in zip(sh_names, sh_t):
            upd.setdefault(name, [None] * 4)[slot] = t

    order = ["mix_pre_g", "mix_post_g", "ffn_pre_g", "ffn_post_g", "cm_w_in", "cm_b_in", "cm_dw", "cm_dw_b", "cm_ln_g",
             "cm_ln_b", "cm_w_out", "cm_b_out", "kv_norm_g", "w_kv", "w_q", "w_o", "ffn_w_in", "ffn_dw", "ffn_dw_b",
             "ffn_w_out"]
    return (loss, grad_x, *[upd[n][0] for n in order], *[upd[n][1] for n in order],
            *[upd[n][2] for n in order], *[upd[n][3] for n in order])
```

```python
import functools

import jax
import jax.numpy as jnp
from jax import lax
from jax.experimental import pallas as pl
from jax.experimental.pallas import tpu as pltpu

N_DEV = 8
N_A = 2
DEPTH = 4
N_HEADS = 8
N_GROUPS = 3
DILATIONS = (1, 4, 16)
ATT_BLOCK = 128
ATT_BATCH = 8
CONV_W = 31
FFN_CONV_W = 3
CONV_HALO = 32
FFN_HALO = 16
ROW_CHUNK_FWD = 16
ROW_CHUNK_BWD = 16
DGRAD_GROUP_ALL_MAX_N = 384
FFN_COL_TILE = 1024
EPS = 1e-6
NEG = -1e30
ADAM_LR, ADAM_B1, ADAM_B2, ADAM_EPS, ADAM_WD, ADAM_STEP = 0.001, 0.9, 0.999, 1e-08, 0.01, 10
MM = jnp.bfloat16
F32 = jnp.float32
VMEM_LIMIT_BYTES = 56 * 1024 * 1024
PACK = 1024
MESH_ID = pl.DeviceIdType.MESH

_pallas = pl.pallas_call


def _call(body, *, name, out_shape, grid=(), in_specs=None, out_specs=None, scratch=()):
    return _pallas(body, name=name, out_shape=out_shape, grid=grid, in_specs=in_specs, out_specs=out_specs,
                   scratch_shapes=list(scratch),
                   compiler_params=pltpu.CompilerParams(vmem_limit_bytes=VMEM_LIMIT_BYTES))


def _tile(n, pref):
    if n <= pref:
        return n
    t = pref - pref % 8
    while n % t:
        t -= 8
    assert t > 0, (n, pref)
    return t


def _sds(shape, dtype):
    return jax.ShapeDtypeStruct(tuple(shape), dtype)


def _dot(a, b):
    return jnp.dot(a, b, preferred_element_type=F32)


def _dot_nt(a, b):
    return lax.dot_general(a, b, (((1,), (1,)), ((), ())), preferred_element_type=F32)


def _dot_tn(a, b):
    return lax.dot_general(a, b, (((0,), (0,)), ((), ())), preferred_element_type=F32)


def _sigmoid(x):
    return 0.5 * jnp.tanh(0.5 * x) + 0.5


def _my_index():
    return 4 * lax.axis_index("x") + 2 * lax.axis_index("y") + lax.axis_index("c")


def _exchange(name, arrays, out_shapes, pieces, src_of, dst_of):
    n = len(arrays)
    base = [sum(pieces[:a]) for a in range(n)]
    total = sum(pieces)

    def body(*refs):
        ins, outs = refs[:n], refs[n:2 * n]
        send_sems, recv_sems, local_sems = refs[2 * n:]
        x, y, c = lax.axis_index("x"), lax.axis_index("y"), lax.axis_index("c")
        me = 4 * x + 2 * y + c
        copies = []
        for a in range(n):
            for k, (s, d) in enumerate(zip(src_of(a, ins[a], me), dst_of(a, outs[a], me))):
                cp = pltpu.make_async_copy(s, d, local_sems.at[base[a] + k])
                cp.start()
                copies.append(cp)
        remote = []
        for m in range(1, N_DEV):
            px, py, pc = x ^ (m >> 2), y ^ ((m >> 1) & 1), c ^ (m & 1)
            peer = 4 * px + 2 * py + pc
            for a in range(n):
                for k, (s, d) in enumerate(zip(src_of(a, ins[a], peer), dst_of(a, outs[a], me))):
                    cp = pltpu.make_async_remote_copy(src_ref=s, dst_ref=d, send_sem=send_sems.at[base[a] + k, m - 1],
                                                      recv_sem=recv_sems.at[base[a] + k, m - 1],
                                                      device_id=(px, py, pc), device_id_type=MESH_ID)
                    cp.start()
                    remote.append(cp)
        for cp in copies:
            cp.wait()
        for cp in remote:
            cp.wait_send()
        for m in range(1, N_DEV):
            px, py, pc = x ^ (m >> 2), y ^ ((m >> 1) & 1), c ^ (m & 1)
            peer = 4 * px + 2 * py + pc
            for a in range(n):
                for k, (s, d) in enumerate(zip(src_of(a, ins[a], me), dst_of(a, outs[a], peer))):
                    pltpu.make_async_remote_copy(src_ref=s, dst_ref=d, send_sem=send_sems.at[base[a] + k, m - 1],
                                                 recv_sem=recv_sems.at[base[a] + k, m - 1], device_id=(px, py, pc),
                                                 device_id_type=MESH_ID).wait_recv()

    any_spec = pl.BlockSpec(memory_space=pl.ANY)
    return _call(body, name=name, out_shape=[_sds(s, a.dtype) for s, a in zip(out_shapes, arrays)],
                 in_specs=[any_spec] * n, out_specs=[any_spec] * n,
                 scratch=[pltpu.SemaphoreType.DMA((total, N_DEV - 1)), pltpu.SemaphoreType.DMA((total, N_DEV - 1)),
                          pltpu.SemaphoreType.DMA((total,))])(*arrays)


def all_gather(name, arrays, row_sharded):
    def out_shape(a):
        s = arrays[a].shape
        return (s[0], N_DEV) + s[1:] if row_sharded[a] else (N_DEV,) + s

    def src_of(a, ref, peer):
        if row_sharded[a]:
            return [ref.at[l] for l in range(arrays[a].shape[0])]
        return [ref]

    def dst_of(a, ref, me):
        if row_sharded[a]:
            return [ref.at[l, me] for l in range(arrays[a].shape[0])]
        return [ref.at[me]]

    pieces = [arrays[a].shape[0] if row_sharded[a] else 1 for a in range(len(arrays))]
    return _exchange(name, arrays, [out_shape(a) for a in range(len(arrays))], pieces, src_of, dst_of)


def _src_view(kind, ref, peer):
    return ref if kind == "gather" else ref.at[peer]


def _peers(x, y, c):
    for m in range(1, N_DEV):
        px, py, pc = x ^ (m >> 2), y ^ ((m >> 1) & 1), c ^ (m & 1)
        yield m - 1, (px, py, pc), 4 * px + 2 * py + pc


def place_own(name, kind, srcs):
    n = len(srcs)
    shapes = [(N_DEV,) + s.shape if kind == "gather" else s.shape for s in srcs]
    steps = 2 if all(s.shape[-2] % 32 == 0 for s in srcs) else 1

    def body(*refs):
        for a in range(n):
            refs[n + a][...] = refs[a][...]

    def spec(shape, own_block):
        R, C = shape[-2:]
        tr = R // steps
        if own_block:
            return pl.BlockSpec((None, tr, C), lambda i: (_my_index(), i, 0))
        return pl.BlockSpec((tr, C), lambda i: (i, 0))

    return _call(body, name=name, grid=(steps,), out_shape=[_sds(s, a.dtype) for s, a in zip(shapes, srcs)],
                 in_specs=[spec(s.shape, kind == "scatter") for s in srcs],
                 out_specs=[spec(s, True) for s in shapes])(*srcs)


_HBM_SPEC = pl.BlockSpec(memory_space=pltpu.HBM)
_SEM_SPEC = pl.BlockSpec(memory_space=pltpu.SEMAPHORE)
_DATAFLOW = pltpu.SideEffectType.DATAFLOW_SIDE_EFFECTING


def _remote(kind, src, land, send_sems, recv_sems, a, slot, frm, to_id, at):
    return pltpu.make_async_remote_copy(src_ref=_src_view(kind, src, frm), dst_ref=land.at[at],
                                        send_sem=send_sems.at[a * (N_DEV - 1) + slot],
                                        recv_sem=recv_sems.at[a * (N_DEV - 1) + slot],
                                        device_id=to_id, device_id_type=MESH_ID)


def exchange_begin(name, kind, srcs, lands, after):
    n = len(srcs)

    def body(*refs):
        ins, lnd = refs[:n], refs[n:2 * n]
        send_sems, recv_sems = refs[2 * n + 1], refs[2 * n + 2]
        token = refs[-1]
        x, y, c = lax.axis_index("x"), lax.axis_index("y"), lax.axis_index("c")
        me = 4 * x + 2 * y + c
        for slot, peer_id, peer in _peers(x, y, c):
            for a in range(n):
                _remote(kind, ins[a], lnd[a], send_sems, recv_sems, a, slot, peer, peer_id, me).start()
        token[...] = jnp.zeros_like(token)

    hbm = lambda t: pltpu.HBM(t.shape, t.dtype)
    outs = _pallas(
        body, name=name,
        out_shape=(pltpu.SemaphoreType.DMA((n * (N_DEV - 1),)), pltpu.SemaphoreType.DMA((n * (N_DEV - 1),)),
                   *[hbm(t) for t in srcs], *[hbm(t) for t in lands], _sds((8, 128), F32)),
        in_specs=[_HBM_SPEC] * (2 * n) + [pl.BlockSpec(memory_space=pl.ANY)],
        out_specs=(_SEM_SPEC, _SEM_SPEC, *[_HBM_SPEC] * (2 * n), pl.BlockSpec(memory_space=pltpu.VMEM)),
        input_output_aliases={i: 2 + i for i in range(2 * n)},
        compiler_params=pltpu.CompilerParams(has_side_effects=_DATAFLOW),
    )(*[pltpu.with_memory_space_constraint(t, pltpu.HBM) for t in list(srcs) + list(lands)], after)
    return (kind, outs[0], outs[1], list(outs[2:2 + n]), list(outs[2 + n:2 + 2 * n])), outs[-1]


def exchange_end(name, handle, after):
    kind, send_sems, recv_sems, srcs, lands = handle
    n = len(srcs)

    def body(*refs):
        ins, lnd = refs[:n], refs[n:2 * n]
        s_sems, r_sems = refs[2 * n], refs[2 * n + 1]
        x, y, c = lax.axis_index("x"), lax.axis_index("y"), lax.axis_index("c")
        me = 4 * x + 2 * y + c
        for slot, peer_id, peer in _peers(x, y, c):
            for a in range(n):
                _remote(kind, ins[a], lnd[a], s_sems, r_sems, a, slot, peer, peer_id, me).wait_send()
        for slot, peer_id, peer in _peers(x, y, c):
            for a in range(n):
                _remote(kind, ins[a], lnd[a], s_sems, r_sems, a, slot, me, peer_id, peer).wait_recv()

    hbm = lambda t: pltpu.HBM(t.shape, t.dtype)
    outs = _pallas(
        body, name=name, out_shape=tuple(hbm(t) for t in srcs + lands),
        in_specs=[_HBM_SPEC] * (2 * n) + [_SEM_SPEC, _SEM_SPEC, pl.BlockSpec(memory_space=pl.ANY)],
        out_specs=tuple([_HBM_SPEC] * (2 * n)), input_output_aliases={i: i for i in range(2 * n)},
        compiler_params=pltpu.CompilerParams(has_side_effects=_DATAFLOW),
    )(*srcs, *lands, send_sems, recv_sems, after)
    return list(outs[n:])


def norm_mm(name, x, gain, w, layer, bias=None, w_t=False):
    T, D = x.shape
    nsh = w.shape[0]
    n = w.shape[2] if w_t else w.shape[3]
    tm = _tile(T, 2048)

    def body(*refs):
        if bias is None:
            x_ref, g_ref, w_ref, u_ref, h_ref = refs
        else:
            x_ref, g_ref, w_ref, b_ref, u_ref, h_ref = refs

        @pl.when(pl.program_id(1) == 0)
        def _():
            xf = x_ref[...]
            r = lax.rsqrt(jnp.mean(xf * xf, axis=-1, keepdims=True) + EPS)
            h_ref[...] = (xf * r * g_ref[...]).astype(h_ref.dtype)

        acc = (_dot_nt if w_t else _dot)(h_ref[...], w_ref[...])
        if bias is not None:
            acc = acc + b_ref[...]
        u_ref[...] = acc.astype(u_ref.dtype)

    in_specs = [pl.BlockSpec((tm, D), lambda i, j: (i, 0)),
                pl.BlockSpec((1, D), lambda i, j: (0, 0)),
                pl.BlockSpec((None, None) + w.shape[2:], lambda i, j: (j, layer, 0, 0))]
    args = [x, gain.reshape(1, D), w]
    if bias is not None:
        in_specs.append(pl.BlockSpec((None, None, 1, n), lambda i, j: (j, layer, 0, 0)))
        args.append(bias)
    return _call(body, name=name, grid=(T // tm, nsh), in_specs=in_specs,
                 out_specs=[pl.BlockSpec((None, tm, n), lambda i, j: (j, i, 0)),
                            pl.BlockSpec((tm, D), lambda i, j: (i, 0))],
                 out_shape=[_sds((nsh, T, n), MM), _sds((T, D), MM)])(*args)


def mm_resid_norm(name, a, w, layer, bias, x, gain):
    nk, T, kk = a.shape
    D = x.shape[1]
    tm = _tile(T, 512)

    def body(a_ref, w_ref, b_ref, x_ref, g_ref, y_ref, xn_ref):
        y = _dot(a_ref[0], w_ref[0])
        for q in range(1, nk):
            y = y + _dot(a_ref[q], w_ref[q])
        y = y + b_ref[...]
        y_ref[...] = y
        r = lax.rsqrt(jnp.mean(y * y, axis=-1, keepdims=True) + EPS)
        xn_ref[...] = x_ref[...] + y * r * g_ref[...]

    return _call(body, name=name, grid=(T // tm,),
                 in_specs=[pl.BlockSpec((nk, tm, kk), lambda i: (0, i, 0)),
                           pl.BlockSpec((None, nk, kk, D), lambda i: (layer, 0, 0, 0)),
                           pl.BlockSpec((1, D), lambda i: (0, 0)),
                           pl.BlockSpec((tm, D), lambda i: (i, 0)),
                           pl.BlockSpec((1, D), lambda i: (0, 0))],
                 out_specs=[pl.BlockSpec((tm, D), lambda i: (i, 0))] * 2,
                 out_shape=[_sds((T, D), F32)] * 2)(a, w, bias.reshape(1, D), x, gain.reshape(1, D))


def _halo_maps(tm, hb, T):
    per = tm // hb
    last = T // hb - 1
    return (lambda i: jnp.maximum(i * per - 1, 0)), (lambda i: jnp.minimum((i + 1) * per, last))


def cm_glu_conv(name, u, dw, dwb, layer, seq):
    _, T, n = u.shape
    ct = dw.shape[-1]
    per = n // ct
    nct = 4 * per
    tm = _tile(seq, 512)
    tps = seq // tm
    hb = CONV_HALO
    prev, _ = _halo_maps(tm, hb, T)
    u4 = u.reshape(2, 4, T, n)

    rows = _tile(tm, 2 * ROW_CHUNK_FWD)

    def body(u_ref, uh_ref, w_ref, b_ref, o_ref, z_ref):
        first = (pl.program_id(0) % tps) == 0
        um = u_ref[...].astype(F32)
        uh = uh_ref[...].astype(F32)
        z_ref[0, pl.ds(hb, tm), :] = um[0] * _sigmoid(um[1])
        z_ref[0, pl.ds(0, hb), :] = jnp.where(first, 0.0, uh[0] * _sigmoid(uh[1]))
        _stage_shifts(z_ref, hb + tm, back=True)
        for r0 in range(0, tm, rows):
            acc = b_ref[...]
            for j in range(CONV_W):
                acc = acc + w_ref[pl.ds(CONV_W - 1 - j, 1), :] * z_ref[j % 8, pl.ds(hb + r0 - (j - j % 8), rows), :]
            o_ref[pl.ds(r0, rows), :] = acc

    return _call(body, name=name, grid=(T // tm, nct),
                 in_specs=[pl.BlockSpec((2, None, tm, ct), lambda i, c: (0, c // per, i, c % per)),
                           pl.BlockSpec((2, None, hb, ct), lambda i, c: (0, c // per, prev(i), c % per)),
                           pl.BlockSpec((None, None, CONV_W, ct), lambda i, c: (c, layer, 0, 0)),
                           pl.BlockSpec((None, None, 1, ct), lambda i, c: (c, layer, 0, 0))],
                 out_specs=pl.BlockSpec((tm, ct), lambda i, c: (i, c)),
                 out_shape=_sds((T, nct * ct), F32),
                 scratch=[pltpu.VMEM((8, tm + hb, ct), F32)])(u4, u4, dw, dwb)


def _stage_shifts(z_ref, n_rows, back):
    for r in range(1, 8):
        if back:
            z_ref[r, pl.ds(8, n_rows - 8), :] = z_ref[0, pl.ds(8 - r, n_rows - 8), :]
        else:
            z_ref[r, pl.ds(0, n_rows - 8), :] = z_ref[0, pl.ds(r, n_rows - 8), :]


def ln_silu(name, c, g, b):
    T, D = c.shape
    tm = _tile(T, 512)

    def body(c_ref, g_ref, b_ref, s_ref):
        cf = c_ref[...]
        mu = jnp.mean(cf, axis=-1, keepdims=True)
        xc = cf - mu
        r = lax.rsqrt(jnp.mean(xc * xc, axis=-1, keepdims=True) + EPS)
        t = xc * r * g_ref[...] + b_ref[...]
        s_ref[...] = (t * _sigmoid(t)).astype(s_ref.dtype)

    return _call(body, name=name, grid=(T // tm,),
                 in_specs=[pl.BlockSpec((tm, D), lambda i: (i, 0)), pl.BlockSpec((1, D), lambda i: (0, 0)),
                           pl.BlockSpec((1, D), lambda i: (0, 0))],
                 out_specs=pl.BlockSpec((None, tm, D), lambda i: (0, i, 0)),
                 out_shape=_sds((1, T, D), MM))(c, g.reshape(1, D), b.reshape(1, D))


def _col_tiles(n, width):
    return [(c0, min(width, n - c0)) for c0 in range(0, n, width)]


def ffn_in_conv(name, x, gain, wt, dw, dwb, layer, seq):
    T, D = x.shape
    n = wt.shape[2]
    tm = _tile(seq, 1024)
    tps = seq // tm
    hb = 8
    rows = _tile(tm, ROW_CHUNK_FWD)
    K = FFN_CONV_W
    w5 = wt.reshape(2, 4, n, D)
    dw4 = dw.reshape(2, 4, dw.shape[1], K, n)
    dwb4 = dwb.reshape(dwb.shape[0], 2, 4, 1, n)

    def body(x_ref, g_ref, w_ref, cw_ref, cb_ref, u_ref, h_ref, z_ref, ag_ref, pad_ref, carry_ref):
        i, j = pl.program_id(0), pl.program_id(1)
        first = (i % tps) == 0

        @pl.when(j == 0)
        def _():
            xf = x_ref[...]
            r = lax.rsqrt(jnp.mean(xf * xf, axis=-1, keepdims=True) + EPS)
            h_ref[...] = (xf * r * g_ref[...]).astype(h_ref.dtype)

        h = h_ref[...]
        for c0, wc in _col_tiles(n, FFN_COL_TILE):
            cols = pl.ds(c0, wc)
            for half in range(2):
                res = _dot_nt(h, w_ref[half, cols, :])
                u_ref[half, :, cols] = res.astype(u_ref.dtype)
                pad_ref[half, pl.ds(hb, tm), cols] = res
                pad_ref[half, pl.ds(0, hb), cols] = jnp.where(first, 0.0, carry_ref[j, half, :, cols])
            for r0 in range(0, tm, rows):
                conv = []
                for half in range(2):
                    acc = cb_ref[half, :, cols] + cw_ref[half, pl.ds(K - 1, 1), cols] * pad_ref[
                        half, pl.ds(hb + r0, rows), cols]
                    for k in range(K - 1):
                        acc = acc + cw_ref[half, pl.ds(k, 1), cols] * pad_ref[
                            half, pl.ds(hb + r0 - (K - 1) + k, rows), cols]
                    conv.append(acc)
                a, g = conv
                sg = _sigmoid(g)
                silu = g * sg
                z_ref[pl.ds(r0, rows), cols] = (silu * a).astype(z_ref.dtype)
                ag_ref[0, pl.ds(r0, rows), cols] = silu.astype(ag_ref.dtype)
                ag_ref[1, pl.ds(r0, rows), cols] = (a * (sg * (1.0 + g * (1.0 - sg)))).astype(ag_ref.dtype)
            for half in range(2):
                carry_ref[j, half, :, cols] = pad_ref[half, pl.ds(tm, hb), cols]

    u, h, z, ag = _call(
        body, name=name, grid=(T // tm, 4),
        in_specs=[pl.BlockSpec((tm, D), lambda i, j: (i, 0)),
                  pl.BlockSpec((1, D), lambda i, j: (0, 0)),
                  pl.BlockSpec((2, None, n, D), lambda i, j: (0, j, 0, 0)),
                  pl.BlockSpec((2, None, None, K, n), lambda i, j: (0, j, layer, 0, 0)),
                  pl.BlockSpec((None, 2, None, 1, n), lambda i, j: (layer, 0, j, 0, 0))],
        out_specs=[pl.BlockSpec((2, None, tm, n), lambda i, j: (0, j, i, 0)),
                   pl.BlockSpec((tm, D), lambda i, j: (i, 0)),
                   pl.BlockSpec((None, tm, n), lambda i, j: (j, i, 0)),
                   pl.BlockSpec((2, None, tm, n), lambda i, j: (0, j, i, 0))],
        out_shape=[_sds((2, 4, T, n), MM), _sds((T, D), MM), _sds((4, T, n), MM), _sds((2, 4, T, n), MM)],
        scratch=[pltpu.VMEM((2, tm + hb, n), F32), pltpu.VMEM((4, 2, hb, n), F32)])(
            x, gain.reshape(1, D), w5, dw4, dwb4)
    return u.reshape(8, T, n), h, z, ag


def _head_specs(seq, dh, q_heads, kv_heads):
    def spec(per, base):
        def imap(b, h, g):
            f = base + g * N_HEADS + h
            return (f // per, b, f % per)
        return pl.BlockSpec((None, seq, dh), imap)
    return spec(q_heads, 0), spec(kv_heads, 0), spec(kv_heads, N_GROUPS * N_HEADS)


def _rows(start, d, blocks=1):
    size = blocks * ATT_BLOCK
    return pl.ds(start, size, stride=d) if d > 1 else pl.ds(start, size)


def _att_pad():
    return max(ATT_BLOCK * d for d in DILATIONS[:-1])


def _band_mask(first, nblk):
    keys = ATT_BLOCK if nblk == 1 else 2 * ATT_BLOCK
    shape = (ATT_BATCH, ATT_BLOCK, keys)
    qi = lax.broadcasted_iota(jnp.int32, shape, 1)
    kj = lax.broadcasted_iota(jnp.int32, shape, 2)
    if nblk == 1:
        return kj <= qi
    n = (first + lax.broadcasted_iota(jnp.int32, shape, 0)) % nblk
    return (kj >= qi) & (kj <= qi + ATT_BLOCK) & ((n > 0) | (kj >= ATT_BLOCK))


def _block_rows(idx, d, nblk):
    r, n = idx // nblk, idx % nblk
    rq = _rows(r + d * ATT_BLOCK * n, d)
    if nblk == 1:
        return rq, _rows(_att_pad() + r + d * ATT_BLOCK * n, d)
    return rq, _rows(_att_pad() + r + d * ATT_BLOCK * (n - 1), d, blocks=2)


def _bdot(a, b, ca, cb):
    return lax.dot_general(a, b, (((ca,), (cb,)), ((0,), (0,))), preferred_element_type=F32)


def attn_fwd(name, q, kv, nb, seq):
    _, T, qn = q.shape
    dh = qn * N_DEV // (N_GROUPS * N_HEADS)
    scale = 1.0 / (dh ** 0.5)
    qs, ks, vs = _head_specs(seq, dh, qn // dh, kv.shape[2] // dh)

    def body(q_ref, k_ref, v_ref, m_ref, l_ref, qf, kf, vf, *branch):
        og, lg = branch[:N_GROUPS], branch[N_GROUPS:]
        pad = _att_pad()
        qf[...] = q_ref[...].astype(F32)
        for t_ref, s_ref in ((k_ref, kf), (v_ref, vf)):
            s_ref[pl.ds(0, pad), :] = jnp.zeros((pad, dh), F32)
            s_ref[pl.ds(pad, seq), :] = t_ref[...].astype(F32)
        for g in range(N_GROUPS):
            d = DILATIONS[g]
            nblk = seq // d // ATT_BLOCK

            def blocks(it, carry, d=d, nblk=nblk, g=g):
                first = it * ATT_BATCH
                rows = [_block_rows(first + b, d, nblk) for b in range(ATT_BATCH)]
                qb = jnp.stack([qf[rq, :] for rq, _ in rows]).astype(MM)
                kb = jnp.stack([kf[rk, :] for _, rk in rows]).astype(MM)
                vb = jnp.stack([vf[rk, :] for _, rk in rows]).astype(MM)
                s = jnp.where(_band_mask(first, nblk), _bdot(qb, kb, 2, 2) * scale, NEG)
                m = jnp.max(s, axis=-1, keepdims=True)
                p = jnp.exp(s - m)
                den = jnp.sum(p, axis=-1, keepdims=True)
                o = _bdot(p.astype(MM), vb, 2, 1) / den
                lse = m + jnp.log(den)
                for b, (rq, _) in enumerate(rows):
                    og[g][rq, :] = o[b]
                    lg[g][rq, :] = jnp.broadcast_to(lse[b], (ATT_BLOCK, dh))
                return carry

            @pl.when(pl.program_id(2) == g)
            def _(blocks=blocks, d=d, nblk=nblk):
                lax.fori_loop(0, d * nblk // ATT_BATCH, blocks, 0)

        @pl.when(pl.program_id(2) == N_GROUPS - 1)
        def _():
            mx = jnp.maximum(jnp.maximum(lg[0][...], lg[1][...]), lg[2][...])
            e = [jnp.exp(lg[g][...] - mx) for g in range(N_GROUPS)]
            tot = e[0] + e[1] + e[2]
            m_ref[...] = ((e[0] * og[0][...] + e[1] * og[1][...] + e[2] * og[2][...]) / tot).astype(m_ref.dtype)
            l_ref[...] = mx + jnp.log(tot)

    return _call(body, name=name, grid=(nb, N_HEADS, N_GROUPS), in_specs=[qs, ks, vs],
                 out_specs=[pl.BlockSpec((None, seq, dh), lambda b, h, g: (0, b, h)),
                            pl.BlockSpec((seq, dh), lambda b, h, g: (b, h))],
                 out_shape=[_sds((1, T, N_HEADS * dh), MM), _sds((T, N_HEADS * dh), F32)],
                 scratch=[pltpu.VMEM((seq, dh), F32)] + [pltpu.VMEM((_att_pad() + seq, dh), F32)] * 2
                 + [pltpu.VMEM((seq, dh), F32)] * (2 * N_GROUPS))(q, kv, kv)


def loss_fwd_bwd(name, x, target):
    T, D = x.shape
    tm = _tile(T, 512)

    def body(x_ref, t_ref, dx_ref, l_ref):
        @pl.when(pl.program_id(0) == 0)
        def _():
            l_ref[...] = jnp.zeros_like(l_ref)
        err = x_ref[...] - t_ref[...]
        dx_ref[...] = err * (1.0 / D)
        l_ref[...] += 0.5 * jnp.sum(jnp.mean(err * err, axis=-1, keepdims=True), axis=0, keepdims=True)

    dx, l = _call(body, name=name, grid=(T // tm,),
                  in_specs=[pl.BlockSpec((tm, D), lambda i: (i, 0))] * 2,
                  out_specs=[pl.BlockSpec((tm, D), lambda i: (i, 0)), pl.BlockSpec((1, 1), lambda i: (0, 0))],
                  out_shape=[_sds((T, D), F32), _sds((1, 1), F32)])(x, target)
    return dx, l


def resid_norm_bwd(name, dx, y, gain):
    T, D = y.shape
    tm = _tile(T, 512)

    def body(dx_ref, y_ref, g_ref, dy_ref, dg_ref, db_ref):
        @pl.when(pl.program_id(0) == 0)
        def _():
            dg_ref[...] = jnp.zeros_like(dg_ref)
            db_ref[...] = jnp.zeros_like(db_ref)
        y = y_ref[...]
        d = dx_ref[...]
        r = lax.rsqrt(jnp.mean(y * y, axis=-1, keepdims=True) + EPS)
        yh = y * r
        dyh = d * g_ref[...]
        dy = r * (dyh - yh * jnp.mean(dyh * yh, axis=-1, keepdims=True))
        dy_ref[...] = dy.astype(dy_ref.dtype)
        dg_ref[...] += jnp.sum(d * yh, axis=0, keepdims=True)
        db_ref[...] += jnp.sum(dy, axis=0, keepdims=True)

    return _call(body, name=name, grid=(T // tm,),
                 in_specs=[pl.BlockSpec((tm, D), lambda i: (i, 0))] * 2 + [pl.BlockSpec((1, D), lambda i: (0, 0))],
                 out_specs=[pl.BlockSpec((None, tm, D), lambda i: (0, i, 0))] + [pl.BlockSpec((1, D), lambda i: (0, 0))] * 2,
                 out_shape=[_sds((1, T, D), MM), _sds((1, D), F32), _sds((1, D), F32)])(dx, y, gain.reshape(1, D))


def mm_nt(name, dy, w, layer, out_dtype):
    _, T, D = dy.shape
    _, nk, kk, _ = w.shape
    tm = _tile(T, 1024)

    def body(dy_ref, w_ref, o_ref):
        o_ref[...] = _dot_nt(dy_ref[...], w_ref[...]).astype(o_ref.dtype)

    return _call(body, name=name, grid=(T // tm, nk),
                 in_specs=[pl.BlockSpec((None, tm, D), lambda i, q: (0, i, 0)),
                           pl.BlockSpec((None, None, kk, D), lambda i, q: (layer, q, 0, 0))],
                 out_specs=pl.BlockSpec((None, tm, kk), lambda i, q: (q, i, 0)),
                 out_shape=_sds((nk, T, kk), out_dtype))(dy, w)


def mm_nt_norm_bwd(name, du, w, layer, x_in, gain, dx_res):
    nsh, T, n = du.shape
    D = x_in.shape[1]
    tm = _tile(T, 512)
    grp = nsh if n <= DGRAD_GROUP_ALL_MAX_N else nsh // 2
    steps = nsh // grp

    def body(du_ref, w_ref, x_ref, g_ref, dr_ref, dx_ref, dg_ref, acc_ref):
        i, j = pl.program_id(0), pl.program_id(1)

        @pl.when((i == 0) & (j == 0))
        def _():
            dg_ref[...] = jnp.zeros_like(dg_ref)

        part = _dot_nt(du_ref[0], w_ref[0])
        for k in range(1, grp):
            part = part + _dot_nt(du_ref[k], w_ref[k])

        @pl.when(j == 0)
        def _():
            acc_ref[...] = part

        @pl.when(j > 0)
        def _():
            acc_ref[...] += part

        @pl.when(j == steps - 1)
        def _():
            x = x_ref[...]
            dh = acc_ref[...]
            r = lax.rsqrt(jnp.mean(x * x, axis=-1, keepdims=True) + EPS)
            xh = x * r
            dxh = dh * g_ref[...]
            dx_ref[...] = dr_ref[...] + r * (dxh - xh * jnp.mean(dxh * xh, axis=-1, keepdims=True))
            dg_ref[...] += jnp.sum(dh * xh, axis=0, keepdims=True)

    return _call(body, name=name, grid=(T // tm, steps),
                 in_specs=[pl.BlockSpec((grp, tm, n), lambda i, j: (j, i, 0)),
                           pl.BlockSpec((grp, None, D, n), lambda i, j: (j, layer, 0, 0)),
                           pl.BlockSpec((tm, D), lambda i, j: (i, 0)),
                           pl.BlockSpec((1, D), lambda i, j: (0, 0)),
                           pl.BlockSpec((tm, D), lambda i, j: (i, 0))],
                 out_specs=[pl.BlockSpec((tm, D), lambda i, j: (i, 0)), pl.BlockSpec((1, D), lambda i, j: (0, 0))],
                 out_shape=[_sds((T, D), F32), _sds((1, D), F32)],
                 scratch=[pltpu.VMEM((tm, D), F32)])(du, w, x_in, gain.reshape(1, D), dx_res)


def mm_tn(name, a, b):
    na, T, ka = a.shape
    nb, _, kb = b.shape
    nj = max(na, nb)

    def body(a_ref, b_ref, o_ref):
        o_ref[...] = _dot_tn(a_ref[...], b_ref[...]).astype(o_ref.dtype)

    return _call(body, name=name, grid=(nj,),
                 in_specs=[pl.BlockSpec((None, T, ka), (lambda j: (j, 0, 0)) if na > 1 else (lambda j: (0, 0, 0))),
                           pl.BlockSpec((None, T, kb), (lambda j: (j, 0, 0)) if nb > 1 else (lambda j: (0, 0, 0)))],
                 out_specs=pl.BlockSpec((None, ka, kb), lambda j: (j, 0, 0)),
                 out_shape=_sds((nj, ka, kb), MM))(a, b)


def ffn_bwd(name, dy, wout, u, ag, dw, wt, x_in, gain, dx_res, layer, seq):
    _, T, D = dy.shape
    n = wt.shape[2]
    tm = _tile(seq, 512)
    tps = seq // tm
    hb = FFN_HALO
    _, nxt = _halo_maps(tm, hb, T)
    K = FFN_CONV_W
    te = tm + hb
    rows = _tile(hb, ROW_CHUNK_BWD)
    u4, w5 = u.reshape(2, 4, T, n), wt.reshape(2, 4, n, D)
    dw4 = dw.reshape(2, 4, dw.shape[1], K, n)
    tiles = _col_tiles(n, FFN_COL_TILE)

    def body(dy_ref, dyn_ref, wo_ref, u_ref, ag_ref, agn_ref, cw_ref, wt_ref, x_ref, g_ref, dr_ref,
             du_ref, ddw_ref, ddb_ref, dx_ref, dg_ref,
             dzf_ref, da_ref, acc_ref):
        i, j = pl.program_id(0), pl.program_id(1)
        last = (i % tps) == tps - 1

        @pl.when((i == 0) & (j == 0))
        def _():
            dg_ref[...] = jnp.zeros_like(dg_ref)

        @pl.when(i == 0)
        def _():
            ddw_ref[j] = jnp.zeros((2, K, n), F32)
            ddb_ref[j] = jnp.zeros((2, 1, n), F32)

        dyt, dyn = dy_ref[...], dyn_ref[...]
        for ci, (c0, wc) in enumerate(tiles):
            cols = pl.ds(c0, wc)
            dzf_ref[pl.ds(0, tm), cols] = _dot_nt(dyt, wo_ref[cols, :])
            dzf_ref[pl.ds(tm, hb), cols] = jnp.where(last, 0.0, _dot_nt(dyn, wo_ref[cols, :]))
            for r0 in range(0, te, rows):
                dzc = dzf_ref[pl.ds(r0, rows), cols]
                for half in range(2):
                    src = ag_ref[half, pl.ds(r0, rows), cols] if r0 < tm else agn_ref[half, pl.ds(r0 - tm, rows), cols]
                    da_ref[half, 0, pl.ds(r0, rows), cols] = dzc * src.astype(F32)
            for half in range(2):
                for s in range(1, K):
                    da_ref[half, s, pl.ds(0, tm), cols] = da_ref[half, 0, pl.ds(s, tm), cols]
                tap_acc = [jnp.zeros((rows, wc), F32) for _ in range(K)]
                bias_acc = jnp.zeros((rows, wc), F32)
                for r0 in range(0, tm, rows):
                    xr = u_ref[half, pl.ds(r0, rows), cols].astype(F32)
                    acc = None
                    for k in range(K):
                        d = da_ref[half, K - 1 - k, pl.ds(r0, rows), cols]
                        term = cw_ref[half, pl.ds(k, 1), cols] * d
                        acc = term if acc is None else acc + term
                        tap_acc[k] = tap_acc[k] + d * xr
                        if k == K - 1:
                            bias_acc = bias_acc + d
                    du_ref[half, pl.ds(r0, rows), cols] = acc.astype(du_ref.dtype)
                for k in range(K):
                    ddw_ref[j, half, pl.ds(k, 1), cols] += jnp.sum(tap_acc[k], axis=0, keepdims=True)
                ddb_ref[j, half, :, cols] += jnp.sum(bias_acc, axis=0, keepdims=True)
            part = _dot(du_ref[0, :, cols], wt_ref[0, cols, :]) + _dot(du_ref[1, :, cols], wt_ref[1, cols, :])
            if ci == 0:
                acc_ref[...] = part + jnp.where(j == 0, 0.0, acc_ref[...])
            else:
                acc_ref[...] += part
        @pl.when(j == 3)
        def _():
            x = x_ref[...]
            dh = acc_ref[...]
            r = lax.rsqrt(jnp.mean(x * x, axis=-1, keepdims=True) + EPS)
            xh = x * r
            dxh = dh * g_ref[...]
            dx_ref[...] = dr_ref[...] + r * (dxh - xh * jnp.mean(dxh * xh, axis=-1, keepdims=True))
            dg_ref[...] += jnp.sum(dh * xh, axis=0, keepdims=True)

    f32 = lambda *shape: pltpu.VMEM(shape, F32)
    du, ddw, ddb, dx, dg = _call(
        body, name=name, grid=(T // tm, 4),
        in_specs=[pl.BlockSpec((None, tm, D), lambda i, j: (0, i, 0)),
                  pl.BlockSpec((None, hb, D), lambda i, j: (0, nxt(i), 0)),
                  pl.BlockSpec((None, None, n, D), lambda i, j: (0, j, 0, 0)),
                  pl.BlockSpec((2, None, tm, n), lambda i, j: (0, j, i, 0)),
                  pl.BlockSpec((2, None, tm, n), lambda i, j: (0, j, i, 0)),
                  pl.BlockSpec((2, None, hb, n), lambda i, j: (0, j, nxt(i), 0)),
                  pl.BlockSpec((2, None, None, K, n), lambda i, j: (0, j, layer, 0, 0)),
                  pl.BlockSpec((2, None, n, D), lambda i, j: (0, j, 0, 0)),
                  pl.BlockSpec((tm, D), lambda i, j: (i, 0)),
                  pl.BlockSpec((1, D), lambda i, j: (0, 0)),
                  pl.BlockSpec((tm, D), lambda i, j: (i, 0))],
        out_specs=[pl.BlockSpec((2, None, tm, n), lambda i, j: (0, j, i, 0)),
                   pl.BlockSpec((4, 2, K, n), lambda i, j: (0, 0, 0, 0)),
                   pl.BlockSpec((4, 2, 1, n), lambda i, j: (0, 0, 0, 0)),
                   pl.BlockSpec((tm, D), lambda i, j: (i, 0)),
                   pl.BlockSpec((1, D), lambda i, j: (0, 0))],
        out_shape=[_sds((2, 4, T, n), MM), _sds((4, 2, K, n), F32), _sds((4, 2, 1, n), F32), _sds((T, D), F32),
                   _sds((1, D), F32)],
        scratch=[f32(te, n), f32(2, K, te, n), f32(tm, D)],
    )(dy, dy, wout, u4, ag, ag, dw4, w5, x_in, gain.reshape(1, D), dx_res)
    return du.reshape(8, T, n), jnp.swapaxes(ddw, 0, 1), jnp.swapaxes(ddb, 0, 1), dx, dg


def ln_silu_bwd(name, ds, c, g, b):
    T, D = c.shape
    tm = _tile(T, 512)

    def body(ds_ref, c_ref, g_ref, b_ref, dc_ref, dg_ref, db_ref):
        @pl.when(pl.program_id(0) == 0)
        def _():
            dg_ref[...] = jnp.zeros_like(dg_ref)
            db_ref[...] = jnp.zeros_like(db_ref)
        cf = c_ref[...]
        mu = jnp.mean(cf, axis=-1, keepdims=True)
        xc = cf - mu
        r = lax.rsqrt(jnp.mean(xc * xc, axis=-1, keepdims=True) + EPS)
        xh = xc * r
        t = xh * g_ref[...] + b_ref[...]
        sg = _sigmoid(t)
        dt = ds_ref[...] * (sg * (1.0 + t * (1.0 - sg)))
        dg_ref[...] += jnp.sum(dt * xh, axis=0, keepdims=True)
        db_ref[...] += jnp.sum(dt, axis=0, keepdims=True)
        dxh = dt * g_ref[...]
        dc_ref[...] = r * (dxh - jnp.mean(dxh, axis=-1, keepdims=True)
                           - xh * jnp.mean(dxh * xh, axis=-1, keepdims=True))

    vec = pl.BlockSpec((1, D), lambda i: (0, 0))
    return _call(body, name=name, grid=(T // tm,),
                 in_specs=[pl.BlockSpec((None, tm, D), lambda i: (0, i, 0)), pl.BlockSpec((tm, D), lambda i: (i, 0)),
                           vec, vec],
                 out_specs=[pl.BlockSpec((tm, D), lambda i: (i, 0)), vec, vec],
                 out_shape=[_sds((T, D), F32), _sds((1, D), F32), _sds((1, D), F32)])(
                     ds, c, g.reshape(1, D), b.reshape(1, D))


def cm_glu_conv_bwd(name, u, dc, dw, layer, seq):
    _, T, n = u.shape
    ct = dw.shape[-1]
    per = n // ct
    nct = 4 * per
    tm = _tile(seq, 512)
    tps = seq // tm
    hb = CONV_HALO
    prev, nxt = _halo_maps(tm, hb, T)
    u4 = u.reshape(2, 4, T, n)
    K = CONV_W

    rows = _tile(tm, 2 * ROW_CHUNK_FWD)
    wrows = 8

    def body(u_ref, up_ref, dc_ref, dcn_ref, w_ref, du_ref, ddw_ref, ddb_ref, dbi_ref, zp_ref, zd_ref):
        i = pl.program_id(1)
        first = (i % tps) == 0
        last = (i % tps) == tps - 1

        @pl.when(i == 0)
        def _():
            ddw_ref[...] = jnp.zeros_like(ddw_ref)
            ddb_ref[...] = jnp.zeros_like(ddb_ref)
            dbi_ref[...] = jnp.zeros_like(dbi_ref)

        uh = up_ref[...].astype(F32)
        zp_ref[0, pl.ds(hb, tm), :] = u_ref[0].astype(F32) * _sigmoid(u_ref[1].astype(F32))
        zp_ref[0, pl.ds(0, hb), :] = jnp.where(first, 0.0, uh[0] * _sigmoid(uh[1]))
        zd_ref[0, pl.ds(0, tm), :] = dc_ref[...]
        zd_ref[0, pl.ds(tm, hb), :] = jnp.where(last, 0.0, dcn_ref[...])
        _stage_shifts(zp_ref, hb + tm, back=True)
        _stage_shifts(zd_ref, tm + hb, back=False)
        for r0 in range(0, tm, rows):
            dp = None
            for j in range(K):
                term = w_ref[pl.ds(K - 1 - j, 1), :] * zd_ref[j % 8, pl.ds(r0 + (j - j % 8), rows), :]
                dp = term if dp is None else dp + term
            v = u_ref[0, pl.ds(r0, rows), :].astype(F32)
            sg = _sigmoid(u_ref[1, pl.ds(r0, rows), :].astype(F32))
            dv = dp * sg
            dg = dp * v * sg * (1.0 - sg)
            du_ref[0, pl.ds(r0, rows), :] = dv.astype(du_ref.dtype)
            du_ref[1, pl.ds(r0, rows), :] = dg.astype(du_ref.dtype)
            dbi_ref[0] += jnp.sum(dv, axis=0, keepdims=True)
            dbi_ref[1] += jnp.sum(dg, axis=0, keepdims=True)
        tap_acc = [jnp.zeros((wrows, ct), F32) for _ in range(K)]
        bias_acc = jnp.zeros((wrows, ct), F32)
        for r0 in range(0, tm, wrows):
            d = zd_ref[0, pl.ds(r0, wrows), :]
            for j in range(K):
                tap_acc[j] = tap_acc[j] + d * zp_ref[j % 8, pl.ds(hb + r0 - (j - j % 8), wrows), :]
            bias_acc = bias_acc + d
        for j in range(K):
            ddw_ref[pl.ds(K - 1 - j, 1), :] += jnp.sum(tap_acc[j], axis=0, keepdims=True)
        ddb_ref[...] += jnp.sum(bias_acc, axis=0, keepdims=True)

    du, ddw, ddb, dbi = _call(
        body, name=name, grid=(nct, T // tm),
        in_specs=[pl.BlockSpec((2, None, tm, ct), lambda c, i: (0, c // per, i, c % per)),
                  pl.BlockSpec((2, None, hb, ct), lambda c, i: (0, c // per, prev(i), c % per)),
                  pl.BlockSpec((tm, ct), lambda c, i: (i, c)),
                  pl.BlockSpec((hb, ct), lambda c, i: (nxt(i), c)),
                  pl.BlockSpec((None, None, K, ct), lambda c, i: (c, layer, 0, 0))],
        out_specs=[pl.BlockSpec((2, None, tm, ct), lambda c, i: (0, c // per, i, c % per)),
                   pl.BlockSpec((None, K, ct), lambda c, i: (c, 0, 0)),
                   pl.BlockSpec((None, 1, ct), lambda c, i: (c, 0, 0)),
                   pl.BlockSpec((2, None, 1, ct), lambda c, i: (0, c // per, 0, c % per))],
        out_shape=[_sds((2, 4, T, n), MM), _sds((nct, K, ct), F32), _sds((nct, 1, ct), F32), _sds((2, 4, 1, n), F32)],
        scratch=[pltpu.VMEM((8, tm + hb, ct), F32), pltpu.VMEM((8, tm + hb, ct), F32)])(u4, u4, dc, dc, dw)
    return du.reshape(8, T, n), ddw, ddb, dbi


def attn_bwd(name, q, kv, dm, merged, lse, dkv_prev, nb, seq):
    _, T, qn = q.shape
    kn = kv.shape[2]
    dh = qn * N_DEV // (N_GROUPS * N_HEADS)
    scale = 1.0 / (dh ** 0.5)
    qs, ks, vs = _head_specs(seq, dh, qn // dh, kn // dh)
    has_prev = dkv_prev is not None
    n_in = 6 + (1 if has_prev else 0)

    def body(*refs):
        q_ref, k_ref, v_ref, dm_ref, mg_ref, l_ref = refs[:6]
        pkv_ref = refs[6] if has_prev else None
        dq_ref, dkv_ref = refs[n_in:n_in + 2]
        qf, kf, vf, dqf, dkf, dvf, dlt = refs[n_in + 2:]
        pad = _att_pad()

        @pl.when(pl.program_id(2) == 0)
        def _():
            dlt[...] = jnp.broadcast_to(
                jnp.sum(dm_ref[...] * mg_ref[...].astype(F32), axis=-1, keepdims=True), (seq, dh))

        qf[...] = q_ref[...].astype(F32)
        for t_ref, s_ref in ((k_ref, kf), (v_ref, vf)):
            s_ref[pl.ds(0, pad), :] = jnp.zeros((pad, dh), F32)
            s_ref[pl.ds(pad, seq), :] = t_ref[...].astype(F32)
        dkf[...] = jnp.zeros_like(dkf)
        dvf[...] = jnp.zeros_like(dvf)
        for g in range(N_GROUPS):
            d = DILATIONS[g]
            nblk = seq // d // ATT_BLOCK

            def blocks(it, carry, d=d, nblk=nblk):
                first = it * ATT_BATCH
                rows = [_block_rows(first + b, d, nblk) for b in range(ATT_BATCH)]
                qb = jnp.stack([qf[rq, :] for rq, _ in rows]).astype(MM)
                dmb = jnp.stack([dm_ref[rq, :] for rq, _ in rows]).astype(MM)
                lse = jnp.stack([l_ref[rq, :][:, :1] for rq, _ in rows])
                delta = jnp.stack([dlt[rq, :][:, :1] for rq, _ in rows])
                kb = jnp.stack([kf[rk, :] for _, rk in rows]).astype(MM)
                vb = jnp.stack([vf[rk, :] for _, rk in rows]).astype(MM)
                s = jnp.where(_band_mask(first, nblk), _bdot(qb, kb, 2, 2) * scale, NEG)
                p = jnp.exp(s - lse)
                dsc = (p * (_bdot(dmb, vb, 2, 2) - delta) * scale).astype(MM)
                dv = _bdot(p.astype(MM), dmb, 1, 1)
                dk = _bdot(dsc, qb, 1, 1)
                dq = _bdot(dsc, kb, 2, 1)
                for b, (rq, rk) in enumerate(rows):
                    dqf[rq, :] = dq[b]
                    dkf[rk, :] += dk[b]
                    dvf[rk, :] += dv[b]
                return carry

            @pl.when(pl.program_id(2) == g)
            def _(blocks=blocks, d=d, nblk=nblk):
                lax.fori_loop(0, d * nblk // ATT_BATCH, blocks, 0)

        dq_ref[...] = dqf[...].astype(dq_ref.dtype)
        dk, dv = dkf[pl.ds(pad, seq), :], dvf[pl.ds(pad, seq), :]
        if has_prev:
            dk, dv = dk + pkv_ref[0].astype(F32), dv + pkv_ref[1].astype(F32)
        dkv_ref[0] = dk.astype(dkv_ref.dtype)
        dkv_ref[1] = dv.astype(dkv_ref.dtype)

    per = kn // dh

    def both(b, h, g):
        f = g * N_HEADS + h
        return (0, f // per, b, f % per)

    kv_spec = pl.BlockSpec((2, None, seq, dh), both)
    full = pl.BlockSpec((None, seq, dh), lambda b, h, g: (0, b, h))
    in_specs = [qs, ks, vs, full, full, pl.BlockSpec((seq, dh), lambda b, h, g: (b, h))]
    args = [q, kv, kv, dm, merged, lse]
    if has_prev:
        in_specs.append(kv_spec)
        args.append(dkv_prev.reshape(2, N_DEV // 2, T, kn))
    short, padded = pltpu.VMEM((seq, dh), F32), pltpu.VMEM((_att_pad() + seq, dh), F32)
    dq, dkv = _call(body, name=name, grid=(nb, N_HEADS, N_GROUPS), in_specs=in_specs, out_specs=[qs, kv_spec],
                    out_shape=[_sds(q.shape, MM), _sds((2, N_DEV // 2, T, kn), MM)],
                    scratch=[short, padded, padded, short, padded, padded, short])(*args)
    return dq, dkv.reshape(N_DEV, T, kn)


def _adamw_math(w, g, m, v):
    m = ADAM_B1 * m + (1.0 - ADAM_B1) * g
    v = ADAM_B2 * v + (1.0 - ADAM_B2) * (g * g)
    m_hat = m / (1.0 - ADAM_B1 ** ADAM_STEP)
    v_hat = v / (1.0 - ADAM_B2 ** ADAM_STEP)
    delta = -ADAM_LR * (m_hat / (jnp.sqrt(v_hat) + ADAM_EPS) + ADAM_WD * w)
    return delta, m, v


def adamw_sum(name, w, m, v, parts, after):
    L, R, C = w.shape
    tr = _tile(R, 256)

    def body(*refs):
        w_ref, m_ref, v_ref = refs[:3]
        p_refs = refs[3:3 + L]
        g_ref, d_ref, nm_ref, nv_ref = refs[4 + L:]
        for l in range(L):
            @pl.when(pl.program_id(0) == l)
            def _(p_ref=p_refs[l]):
                g = p_ref[0].astype(F32)
                for k in range(1, N_DEV):
                    g = g + p_ref[k].astype(F32)
                g_ref[...] = g
                d_ref[...], nm_ref[...], nv_ref[...] = _adamw_math(w_ref[...], g, m_ref[...], v_ref[...])

    blk = pl.BlockSpec((None, tr, C), lambda l, i: (l, i, 0))
    part = lambda k: pl.BlockSpec((N_DEV, tr, C), lambda l, i: (0, jnp.where(l == k, i, 0), 0))
    return _call(body, name=name, grid=(L, R // tr),
                 in_specs=[blk, blk, blk] + [part(k) for k in range(L)] + [pl.BlockSpec(memory_space=pl.ANY)],
                 out_specs=[blk] * 4, out_shape=[_sds((L, R, C), F32)] * 4)(w, m, v, *parts, after)


def sum_partials(name, parts):
    _, R, C = parts.shape
    tr = _tile(R, 512)

    def body(p_ref, o_ref):
        g = p_ref[0]
        for k in range(1, N_DEV):
            g = g + p_ref[k]
        o_ref[...] = g

    return _call(body, name=name, grid=(R // tr,),
                 in_specs=[pl.BlockSpec((N_DEV, tr, C), lambda i: (0, i, 0))],
                 out_specs=pl.BlockSpec((tr, C), lambda i: (i, 0)), out_shape=_sds((R, C), F32))(parts)


def adamw_small(name, w, g, m, v):
    R, C = w.shape
    tr = _tile(R, 512)

    def body(w_ref, g_ref, m_ref, v_ref, d_ref, nm_ref, nv_ref):
        d_ref[...], nm_ref[...], nv_ref[...] = _adamw_math(w_ref[...], g_ref[...], m_ref[...], v_ref[...])

    blk = pl.BlockSpec((tr, C), lambda i: (i, 0))
    return _call(body, name=name, grid=(R // tr,), in_specs=[blk] * 4, out_specs=[blk] * 3,
                 out_shape=[_sds((R, C), F32)] * 3)(w, g, m, v)


def _pack(arrays):
    pieces = []
    for a in arrays:
        f = a.reshape(-1).astype(F32)
        pieces.append(jnp.pad(f, (0, (-f.shape[0]) % PACK)))
    return jnp.concatenate(pieces).reshape(-1, 128)


def _unpack(flat, shapes):
    out, off = [], 0
    f = flat.reshape(-1)
    for s in shapes:
        size = 1
        for d in s:
            size *= d
        out.append(f[off:off + size].reshape(s))
        off += size + (-size) % PACK
    return out


def kernel(x, mix_pre_g, mix_post_g, ffn_pre_g, ffn_post_g, cm_w_in, cm_b_in, cm_dw, cm_dw_b, cm_ln_g, cm_ln_b, cm_w_out, cm_b_out, kv_norm_g, w_kv, w_q, w_o, ffn_w_in, ffn_dw, ffn_dw_b, ffn_w_out, loss_target, m_mix_pre_g, m_mix_post_g, m_ffn_pre_g, m_ffn_post_g, m_cm_w_in, m_cm_b_in, m_cm_dw, m_cm_dw_b, m_cm_ln_g, m_cm_ln_b, m_cm_w_out, m_cm_b_out, m_kv_norm_g, m_w_kv, m_w_q, m_w_o, m_ffn_w_in, m_ffn_dw, m_ffn_dw_b, m_ffn_w_out, v_mix_pre_g, v_mix_post_g, v_ffn_pre_g, v_ffn_post_g, v_cm_w_in, v_cm_b_in, v_cm_dw, v_cm_dw_b, v_cm_ln_g, v_cm_ln_b, v_cm_w_out, v_cm_b_out, v_kv_norm_g, v_w_kv, v_w_q, v_w_o, v_ffn_w_in, v_ffn_dw, v_ffn_dw_b, v_ffn_w_out):
    nb, seq, D = x.shape
    T = nb * seq
    me = _my_index()
    n_b = DEPTH - N_A

    nf = ffn_w_in.shape[-1]
    t_ = lambda t: jnp.swapaxes(t, 1, 2)
    fin_t, m_fin_t, v_fin_t = t_(ffn_w_in), t_(m_ffn_w_in), t_(v_ffn_w_in)

    stages = [(part, i) for i in range(DEPTH) for part in ("mix", "ffn")]
    stages.insert(stages.index(("ffn", N_A - 1)) + 1, ("kv", N_A - 1))

    def stage_sources(stage):
        part, i = stage
        if part == "ffn":
            src = {"fin": fin_t[i], "fout": ffn_w_out[i]}
        elif part == "kv":
            src = {"kv": w_kv}
        elif i < N_A:
            src = {"cin": cm_w_in[i], "cout": cm_w_out[i]}
        else:
            src = {"q": w_q[i - N_A], "o": w_o[i - N_A]}
        return {k: t.astype(MM) for k, t in src.items()}

    def begin_gather(stage, after):
        src = stage_sources(stage)
        names, arrays = list(src), list(src.values())
        tag = f"{stage[0]}{stage[1]}"
        lands = place_own(f"gather_own_{tag}", "gather", arrays)
        handle, token = exchange_begin(f"gather_begin_{tag}", "gather", arrays, lands, after)
        return (names, handle), token

    def end_gather(stage, pending, after):
        names, handle = pending
        W = dict(zip(names, exchange_end(f"gather_end_{stage[0]}{stage[1]}", handle, after)))
        for k in W:
            if k in ("cout", "o"):
                W[k] = W[k].reshape(1, 1, D, D)
            elif k == "fout":
                W[k] = W[k].reshape(1, 4, nf, D)
            else:
                W[k] = W[k][:, None]
        return W

    small = [cm_b_in[:, None, :], cm_dw, cm_dw_b[:, None, :], cm_ln_g, cm_ln_b, cm_b_out, ffn_dw]
    Bcin, DWc, DWBc, LNg, LNb, Bcout, DWf = all_gather("gather_small", small, [False] * len(small))
    LNg = jnp.swapaxes(LNg, 0, 1).reshape(N_A, D)
    LNb = jnp.swapaxes(LNb, 0, 1).reshape(N_A, D)
    Bcout = jnp.swapaxes(Bcout, 0, 1).reshape(N_A, D)
    DWBf = ffn_dw_b.reshape(DEPTH, N_DEV, 1, nf)
    zero_bias = jnp.zeros((D,), F32)

    xs = x.reshape(T, D)
    sv = []
    kv = hkv = None
    pending = {}
    pending[stages[0]], _ = begin_gather(stages[0], DWf)
    Ws = {stages[0]: end_gather(stages[0], pending[stages[0]], xs)}
    pending[stages[1]], token = begin_gather(stages[1], next(iter(Ws[stages[0]].values())))
    sv = [{} for _ in range(DEPTH)]
    for idx, stage in enumerate(stages):
        part, i = stage
        L, W = sv[i], Ws[stage]
        gain = {"mix": mix_pre_g[i], "ffn": ffn_pre_g[i], "kv": kv_norm_g}[part]
        following = stages[idx + 1] if idx + 1 < len(stages) else None
        if idx + 2 < len(stages):
            after = token if idx == 0 else next(iter(W.values()))
            pending[stages[idx + 2]], token = begin_gather(stages[idx + 2], after)
        if idx + 2 < len(stages) or idx == 0:
            gain = gain + token[0, 0]
        if part == "mix":
            L["x_in"] = xs
            if i < N_A:
                L["u"], L["h"] = norm_mm(f"cm_in_{i}", xs, gain, W["cin"], 0, Bcin[:, i:i + 1])
                L["c"] = cm_glu_conv(f"cm_conv_{i}", L["u"], DWc, DWBc, i, seq)
                L["s"] = ln_silu(f"cm_ln_{i}", L["c"], LNg[i], LNb[i])
                L["y"], xs = mm_resid_norm(f"cm_out_{i}", L["s"], W["cout"], 0, Bcout[i], xs, mix_post_g[i])
            else:
                L["q"], L["h"] = norm_mm(f"attn_q_{i}", xs, gain, W["q"], 0)
                L["mg"], L["lse"] = attn_fwd(f"attn_{i}", L["q"], kv, nb, seq)
                L["y"], xs = mm_resid_norm(f"attn_out_{i}", L["mg"], W["o"], 0, zero_bias, xs, mix_post_g[i])
            L["x1"] = xs
        elif part == "ffn":
            L["uf"], L["hf"], L["z"], L["ag"] = ffn_in_conv(f"ffn_in_{i}", xs, gain, W["fin"], DWf, DWBf, i, seq)
            L["yf"], xs = mm_resid_norm(f"ffn_out_{i}", L["z"], W["fout"], 0, zero_bias, xs, ffn_post_g[i])
        else:
            kv, hkv = norm_mm("kv_proj", xs, gain, W["kv"], 0)
        if following is not None:
            Ws[following] = end_gather(following, pending[following], kv if part == "kv" else xs)
    dx, loss_part = loss_fwd_bwd("loss", xs, loss_target.reshape(T, D))

    g_mix_pre, g_mix_post, g_ffn_pre, g_ffn_post = [None] * DEPTH, [None] * DEPTH, [None] * DEPTH, [None] * DEPTH
    g_ffn_dw, g_ffn_dwb = [None] * DEPTH, [None] * DEPTH
    g_cbin, g_cdw, g_cdwb, g_lng, g_lnb, g_cbout = ([None] * N_A for _ in range(6))
    g_kvn = dkv = None
    landed = [{} for _ in range(DEPTH)]
    in_flight = token = None
    for stage in reversed(stages):
        part, i = stage
        L, W = sv[i], Ws[stage]
        gain = {"mix": mix_post_g[i], "ffn": ffn_post_g[i], "kv": kv_norm_g}[part]
        if token is not None:
            gain = gain + token[0, 0]
        send = {}
        if part == "kv":
            send["kv"] = mm_tn("kv_wg", hkv[None], dkv)
            dx, g_kvn = mm_nt_norm_bwd("kv_bwd", dkv, W["kv"], 0, sv[i + 1]["x_in"], gain, dx)
        elif part == "ffn":
            dyf, g_ffn_post[i], _ = resid_norm_bwd(f"ffn_post_bwd_{i}", dx, L["yf"], gain)
            send["fout"] = mm_tn(f"ffn_out_wg_{i}", L["z"], dyf).reshape(N_DEV, nf // 2, D)
            duf, ddw, ddwb, dx, g_ffn_pre[i] = ffn_bwd(f"ffn_bwd_{i}", dyf, W["fout"], L["uf"], L["ag"], DWf, W["fin"],
                                                       L["x1"], ffn_pre_g[i], dx, i, seq)
            g_ffn_dw[i], g_ffn_dwb[i] = ddw.reshape(N_DEV, FFN_CONV_W, nf), ddwb.reshape(-1)
            send["fin"] = mm_tn(f"ffn_in_wg_{i}", duf, L["hf"][None])
        else:
            dy, g_mix_post[i], dyb = resid_norm_bwd(f"mix_post_bwd_{i}", dx, L["y"], gain)
            if i >= N_A:
                dm = mm_nt(f"attn_out_bwd_{i}", dy, W["o"], 0, F32)
                send["o"] = mm_tn(f"attn_out_wg_{i}", L["mg"], dy).reshape(N_DEV, D // N_DEV, D)
                dq, dkv = attn_bwd(f"attn_bwd_{i}", L["q"], kv, dm, L["mg"], L["lse"], dkv, nb, seq)
                send["q"] = mm_tn(f"attn_q_wg_{i}", L["h"][None], dq)
                dx, g_mix_pre[i] = mm_nt_norm_bwd(f"attn_q_bwd_{i}", dq, W["q"], 0, L["x_in"], mix_pre_g[i], dx)
            else:
                g_cbout[i] = dyb
                ds = mm_nt(f"cm_out_bwd_{i}", dy, W["cout"], 0, F32)
                send["cout"] = mm_tn(f"cm_out_wg_{i}", L["s"], dy).reshape(N_DEV, D // N_DEV, D)
                ln_gain = LNg[i]
                if stage == stages[0]:
                    early = [send.pop("cout")]
                    lands = place_own("scatter_own_cout0", "scatter", early)
                    early_handle, early_token = exchange_begin("scatter_begin_cout0", "scatter", early, lands, ds)
                    ln_gain = ln_gain + early_token[0, 0]
                dc, g_lng[i], g_lnb[i] = ln_silu_bwd(f"cm_ln_bwd_{i}", ds, L["c"], ln_gain, LNb[i])
                du, g_cdw[i], g_cdwb[i], dbi = cm_glu_conv_bwd(f"cm_conv_bwd_{i}", L["u"], dc, DWc, i, seq)
                g_cbin[i] = dbi.reshape(N_DEV, -1)
                send["cin"] = mm_tn(f"cm_in_wg_{i}", L["h"][None], du)
                dx, g_mix_pre[i] = mm_nt_norm_bwd(f"cm_in_bwd_{i}", du, W["cin"], 0, L["x_in"], mix_pre_g[i], dx)
        if in_flight is not None:
            (p, j), names, handle = in_flight
            landed[j].update(zip(names, exchange_end(f"scatter_end_{p}{j}", handle, dx)))
        names, arrays = list(send), list(send.values())
        lands = place_own(f"scatter_own_{part}{i}", "scatter", arrays)
        handle, token = exchange_begin(f"scatter_begin_{part}{i}", "scatter", arrays, lands, dx)
        in_flight = (stage, names, handle)
    grad_x = dx.reshape(nb, seq, D)

    rep_names = ["mix_pre_g", "mix_post_g", "ffn_pre_g", "ffn_post_g", "kv_norm_g", "ffn_dw_b"]
    rep_parts = [jnp.concatenate(g_mix_pre), jnp.concatenate(g_mix_post), jnp.concatenate(g_ffn_pre),
                 jnp.concatenate(g_ffn_post), g_kvn.reshape(-1), jnp.stack(g_ffn_dwb)]
    rep_w = [mix_pre_g, mix_post_g, ffn_pre_g, ffn_post_g, kv_norm_g, ffn_dw_b]
    rep_m = [m_mix_pre_g, m_mix_post_g, m_ffn_pre_g, m_ffn_post_g, m_kv_norm_g, m_ffn_dw_b]
    rep_v = [v_mix_pre_g, v_mix_post_g, v_ffn_pre_g, v_ffn_post_g, v_kv_norm_g, v_ffn_dw_b]
    sh_names = ["ffn_dw", "cm_b_in", "cm_dw", "cm_dw_b", "cm_ln_g", "cm_ln_b", "cm_b_out"]
    own = lambda per_layer, shard: jnp.stack([p.reshape((N_DEV,) + shard) for p in per_layer], axis=1)
    sh_parts = [own(g_ffn_dw, ffn_dw.shape[1:]), own(g_cbin, cm_b_in.shape[1:]), own(g_cdw, cm_dw.shape[1:]),
                own(g_cdwb, cm_dw_b.shape[1:]), own(g_lng, cm_ln_g.shape[1:]), own(g_lnb, cm_ln_b.shape[1:]),
                own(g_cbout, cm_b_out.shape[1:])]
    sh_w = [ffn_dw, cm_b_in, cm_dw, cm_dw_b, cm_ln_g, cm_ln_b, cm_b_out]
    sh_m = [m_ffn_dw, m_cm_b_in, m_cm_dw, m_cm_dw_b, m_cm_ln_g, m_cm_ln_b, m_cm_b_out]
    sh_v = [v_ffn_dw, v_cm_b_in, v_cm_dw, v_cm_dw_b, v_cm_ln_g, v_cm_ln_b, v_cm_b_out]
    rep_pack = _pack([loss_part] + rep_parts)
    sh_pack = jnp.stack([_pack([p[k] for p in sh_parts]) for k in range(N_DEV)])
    n_rep = rep_pack.shape[0]
    small = []
    for kind, pack in (("gather", rep_pack), ("scatter", sh_pack)):
        lands = place_own(f"{kind}_own_small", kind, [pack])
        handle, token = exchange_begin(f"{kind}_begin_small", kind, [pack], lands, token)
        small.append((kind, handle))

    def big_update(name, w, m, v, key, layers, after):
        as3 = lambda t: t.reshape((-1,) + t.shape[-2:])
        outs = adamw_sum(name, as3(w), as3(m), as3(v), [landed[i][key] for i in layers], after)
        return [t.reshape(w.shape) for t in outs]

    conf, attn = range(N_A), range(N_A, DEPTH)
    upd = {}
    upd["ffn_w_in"] = [t_(t) for t in big_update("adam_ffn_w_in", fin_t, m_fin_t, v_fin_t, "fin", range(DEPTH), token)]
    upd["ffn_w_out"] = big_update("adam_ffn_w_out", ffn_w_out, m_ffn_w_out, v_ffn_w_out, "fout", range(DEPTH), token)
    upd["w_kv"] = big_update("adam_w_kv", w_kv, m_w_kv, v_w_kv, "kv", [N_A - 1], token)
    upd["w_q"] = big_update("adam_w_q", w_q, m_w_q, v_w_q, "q", attn, token)
    upd["w_o"] = big_update("adam_w_o", w_o, m_w_o, v_w_o, "o", attn, upd["w_q"][0])
    (p, j), names, handle = in_flight
    landed[j].update(zip(names, exchange_end(f"scatter_end_{p}{j}", handle, upd["w_o"][0])))
    landed[0]["cout"] = exchange_end("scatter_end_cout0", early_handle, upd["w_o"][0])[0]
    upd["cm_w_in"] = big_update("adam_cm_w_in", cm_w_in, m_cm_w_in, v_cm_w_in, "cin", conf, token)
    upd["cm_w_out"] = big_update("adam_cm_w_out", cm_w_out, m_cm_w_out, v_cm_w_out, "cout", conf, upd["cm_w_in"][0])
    (rep_landed,), (sh_landed,) = (exchange_end(f"{kind}_end_small", handle, upd["cm_w_out"][0])
                                   for kind, handle in small)
    rep_sum = sum_partials("sum_small_rep", rep_landed)
    sh_sum = sum_partials("sum_small_sh", sh_landed)
    rep_shapes = [(1, 1)] + [w.shape for w in rep_w]
    sh_shapes = [w.shape for w in sh_w]
    g_small = jnp.concatenate([rep_sum, sh_sum])
    pad1 = jnp.zeros((1, 1), F32)
    d_s, m_s, v_s = adamw_small("adam_small", jnp.concatenate([_pack([pad1] + rep_w), _pack(sh_w)]), g_small,
                                jnp.concatenate([_pack([pad1] + rep_m), _pack(sh_m)]),
                                jnp.concatenate([_pack([pad1] + rep_v), _pack(sh_v)]))
    split = lambda t: (_unpack(t[:n_rep], rep_shapes), _unpack(t[n_rep:], sh_shapes))
    for (rep_t, sh_t), slot in zip([split(g_small), split(d_s), split(m_s), split(v_s)], range(4)):
        if slot == 0:
            loss = rep_t[0].reshape(())
        for name, t in zip(rep_names, rep_t[1:]):
            upd.setdefault(name, [None] * 4)[slot] = t
        for name, t in zip(sh_names, sh_t):
            upd.setdefault(name, [None] * 4)[slot] = t

    order = ["mix_pre_g", "mix_post_g", "ffn_pre_g", "ffn_post_g", "cm_w_in", "cm_b_in", "cm_dw", "cm_dw_b", "cm_ln_g",
             "cm_ln_b", "cm_w_out", "cm_b_out", "kv_norm_g", "w_kv", "w_q", "w_o", "ffn_w_in", "ffn_dw", "ffn_dw_b",
             "ffn_w_out"]
    return (loss, grad_x, *[upd[n][0] for n in order], *[upd[n][1] for n in order],
            *[upd[n][2] for n in order], *[upd[n][3] for n in order])
```

```python
import functools

import jax
import jax.numpy as jnp
from jax import lax
from jax.experimental import pallas as pl
from jax.experimental.pallas import tpu as pltpu

N_DEV = 8
N_A = 2
DEPTH = 4
N_HEADS = 8
N_GROUPS = 3
DILATIONS = (1, 4, 16)
ATT_BLOCK = 128
ATT_BATCH = 8
CONV_W = 31
FFN_CONV_W = 3
CONV_HALO = 32
FFN_HALO = 16
ROW_CHUNK_FWD = 16
ROW_CHUNK_BWD = 16
GROUP_ALL_MAX_N = 384
FFN_COL_TILE = 1024
EPS = 1e-6
NEG = -1e30
ADAM_LR, ADAM_B1, ADAM_B2, ADAM_EPS, ADAM_WD, ADAM_STEP = 0.001, 0.9, 0.999, 1e-08, 0.01, 10
MM = jnp.bfloat16
F32 = jnp.float32
VMEM_LIMIT_BYTES = 56 * 1024 * 1024
PACK = 1024
MESH_ID = pl.DeviceIdType.MESH

_pallas = pl.pallas_call


def _call(body, *, name, out_shape, grid=(), in_specs=None, out_specs=None, scratch=()):
    return _pallas(body, name=name, out_shape=out_shape, grid=grid, in_specs=in_specs, out_specs=out_specs,
                   scratch_shapes=list(scratch),
                   compiler_params=pltpu.CompilerParams(vmem_limit_bytes=VMEM_LIMIT_BYTES))


def _tile(n, pref):
    if n <= pref:
        return n
    t = pref - pref % 8
    while n % t:
        t -= 8
    assert t > 0, (n, pref)
    return t


def _sds(shape, dtype):
    return jax.ShapeDtypeStruct(tuple(shape), dtype)


def _dot(a, b):
    return jnp.dot(a, b, preferred_element_type=F32)


def _dot_nt(a, b):
    return lax.dot_general(a, b, (((1,), (1,)), ((), ())), preferred_element_type=F32)


def _dot_tn(a, b):
    return lax.dot_general(a, b, (((0,), (0,)), ((), ())), preferred_element_type=F32)


def _sigmoid(x):
    return 0.5 * jnp.tanh(0.5 * x) + 0.5


def _my_index():
    return 4 * lax.axis_index("x") + 2 * lax.axis_index("y") + lax.axis_index("c")


def _exchange(name, arrays, out_shapes, pieces, src_of, dst_of):
    n = len(arrays)
    base = [sum(pieces[:a]) for a in range(n)]
    total = sum(pieces)

    def body(*refs):
        ins, outs = refs[:n], refs[n:2 * n]
        send_sems, recv_sems, local_sems = refs[2 * n:]
        x, y, c = lax.axis_index("x"), lax.axis_index("y"), lax.axis_index("c")
        me = 4 * x + 2 * y + c
        copies = []
        for a in range(n):
            for k, (s, d) in enumerate(zip(src_of(a, ins[a], me), dst_of(a, outs[a], me))):
                cp = pltpu.make_async_copy(s, d, local_sems.at[base[a] + k])
                cp.start()
                copies.append(cp)
        remote = []
        for m in range(1, N_DEV):
            px, py, pc = x ^ (m >> 2), y ^ ((m >> 1) & 1), c ^ (m & 1)
            peer = 4 * px + 2 * py + pc
            for a in range(n):
                for k, (s, d) in enumerate(zip(src_of(a, ins[a], peer), dst_of(a, outs[a], me))):
                    cp = pltpu.make_async_remote_copy(src_ref=s, dst_ref=d, send_sem=send_sems.at[base[a] + k, m - 1],
                                                      recv_sem=recv_sems.at[base[a] + k, m - 1],
                                                      device_id=(px, py, pc), device_id_type=MESH_ID)
                    cp.start()
                    remote.append(cp)
        for cp in copies:
            cp.wait()
        for cp in remote:
            cp.wait_send()
        for m in range(1, N_DEV):
            px, py, pc = x ^ (m >> 2), y ^ ((m >> 1) & 1), c ^ (m & 1)
            peer = 4 * px + 2 * py + pc
            for a in range(n):
                for k, (s, d) in enumerate(zip(src_of(a, ins[a], me), dst_of(a, outs[a], peer))):
                    pltpu.make_async_remote_copy(src_ref=s, dst_ref=d, send_sem=send_sems.at[base[a] + k, m - 1],
                                                 recv_sem=recv_sems.at[base[a] + k, m - 1], device_id=(px, py, pc),
                                                 device_id_type=MESH_ID).wait_recv()

    any_spec = pl.BlockSpec(memory_space=pl.ANY)
    return _call(body, name=name, out_shape=[_sds(s, a.dtype) for s, a in zip(out_shapes, arrays)],
                 in_specs=[any_spec] * n, out_specs=[any_spec] * n,
                 scratch=[pltpu.SemaphoreType.DMA((total, N_DEV - 1)), pltpu.SemaphoreType.DMA((total, N_DEV - 1)),
                          pltpu.SemaphoreType.DMA((total,))])(*arrays)


def all_gather(name, arrays, row_sharded):
    def out_shape(a):
        s = arrays[a].shape
        return (s[0], N_DEV) + s[1:] if row_sharded[a] else (N_DEV,) + s

    def src_of(a, ref, peer):
        if row_sharded[a]:
            return [ref.at[l] for l in range(arrays[a].shape[0])]
        return [ref]

    def dst_of(a, ref, me):
        if row_sharded[a]:
            return [ref.at[l, me] for l in range(arrays[a].shape[0])]
        return [ref.at[me]]

    pieces = [arrays[a].shape[0] if row_sharded[a] else 1 for a in range(len(arrays))]
    return _exchange(name, arrays, [out_shape(a) for a in range(len(arrays))], pieces, src_of, dst_of)


def _src_view(kind, ref, peer):
    return ref if kind == "gather" else ref.at[peer]


def _peers(x, y, c):
    for m in range(1, N_DEV):
        px, py, pc = x ^ (m >> 2), y ^ ((m >> 1) & 1), c ^ (m & 1)
        yield m - 1, (px, py, pc), 4 * px + 2 * py + pc


def place_own(name, kind, srcs):
    n = len(srcs)
    shapes = [(N_DEV,) + s.shape if kind == "gather" else s.shape for s in srcs]
    steps = 2 if all(s.shape[-2] % 32 == 0 for s in srcs) else 1

    def body(*refs):
        for a in range(n):
            refs[n + a][...] = refs[a][...]

    def spec(shape, own_block):
        R, C = shape[-2:]
        tr = R // steps
        if own_block:
            return pl.BlockSpec((None, tr, C), lambda i: (_my_index(), i, 0))
        return pl.BlockSpec((tr, C), lambda i: (i, 0))

    return _call(body, name=name, grid=(steps,), out_shape=[_sds(s, a.dtype) for s, a in zip(shapes, srcs)],
                 in_specs=[spec(s.shape, kind == "scatter") for s in srcs],
                 out_specs=[spec(s, True) for s in shapes])(*srcs)


_HBM_SPEC = pl.BlockSpec(memory_space=pltpu.HBM)
_SEM_SPEC = pl.BlockSpec(memory_space=pltpu.SEMAPHORE)
_DATAFLOW = pltpu.SideEffectType.DATAFLOW_SIDE_EFFECTING


def _remote(kind, src, land, send_sems, recv_sems, a, slot, frm, to_id, at):
    return pltpu.make_async_remote_copy(src_ref=_src_view(kind, src, frm), dst_ref=land.at[at],
                                        send_sem=send_sems.at[a * (N_DEV - 1) + slot],
                                        recv_sem=recv_sems.at[a * (N_DEV - 1) + slot],
                                        device_id=to_id, device_id_type=MESH_ID)


def exchange_begin(name, kind, srcs, lands, after):
    n = len(srcs)

    def body(*refs):
        ins, lnd = refs[:n], refs[n:2 * n]
        send_sems, recv_sems = refs[2 * n + 1], refs[2 * n + 2]
        token = refs[-1]
        x, y, c = lax.axis_index("x"), lax.axis_index("y"), lax.axis_index("c")
        me = 4 * x + 2 * y + c
        for slot, peer_id, peer in _peers(x, y, c):
            for a in range(n):
                _remote(kind, ins[a], lnd[a], send_sems, recv_sems, a, slot, peer, peer_id, me).start()
        token[...] = jnp.zeros_like(token)

    hbm = lambda t: pltpu.HBM(t.shape, t.dtype)
    outs = _pallas(
        body, name=name,
        out_shape=(pltpu.SemaphoreType.DMA((n * (N_DEV - 1),)), pltpu.SemaphoreType.DMA((n * (N_DEV - 1),)),
                   *[hbm(t) for t in srcs], *[hbm(t) for t in lands], _sds((8, 128), F32)),
        in_specs=[_HBM_SPEC] * (2 * n) + [pl.BlockSpec(memory_space=pl.ANY)],
        out_specs=(_SEM_SPEC, _SEM_SPEC, *[_HBM_SPEC] * (2 * n), pl.BlockSpec(memory_space=pltpu.VMEM)),
        input_output_aliases={i: 2 + i for i in range(2 * n)},
        compiler_params=pltpu.CompilerParams(has_side_effects=_DATAFLOW),
    )(*[pltpu.with_memory_space_constraint(t, pltpu.HBM) for t in list(srcs) + list(lands)], after)
    return (kind, outs[0], outs[1], list(outs[2:2 + n]), list(outs[2 + n:2 + 2 * n])), outs[-1]


def exchange_end(name, handle, after):
    kind, send_sems, recv_sems, srcs, lands = handle
    n = len(srcs)

    def body(*refs):
        ins, lnd = refs[:n], refs[n:2 * n]
        s_sems, r_sems = refs[2 * n], refs[2 * n + 1]
        x, y, c = lax.axis_index("x"), lax.axis_index("y"), lax.axis_index("c")
        me = 4 * x + 2 * y + c
        for slot, peer_id, peer in _peers(x, y, c):
            for a in range(n):
                _remote(kind, ins[a], lnd[a], s_sems, r_sems, a, slot, peer, peer_id, me).wait_send()
        for slot, peer_id, peer in _peers(x, y, c):
            for a in range(n):
                _remote(kind, ins[a], lnd[a], s_sems, r_sems, a, slot, me, peer_id, peer).wait_recv()

    hbm = lambda t: pltpu.HBM(t.shape, t.dtype)
    outs = _pallas(
        body, name=name, out_shape=tuple(hbm(t) for t in srcs + lands),
        in_specs=[_HBM_SPEC] * (2 * n) + [_SEM_SPEC, _SEM_SPEC, pl.BlockSpec(memory_space=pl.ANY)],
        out_specs=tuple([_HBM_SPEC] * (2 * n)), input_output_aliases={i: i for i in range(2 * n)},
        compiler_params=pltpu.CompilerParams(has_side_effects=_DATAFLOW),
    )(*srcs, *lands, send_sems, recv_sems, after)
    return list(outs[n:])


def norm_mm(name, x, gain, w, layer, bias=None):
    T, D = x.shape
    nsh, _, _, n = w.shape
    tm = _tile(T, 1024)
    grp = nsh if n <= GROUP_ALL_MAX_N else nsh // 2
    steps = nsh // grp

    def body(*refs):
        if bias is None:
            x_ref, g_ref, w_ref, u_ref, h_ref = refs
        else:
            x_ref, g_ref, w_ref, b_ref, u_ref, h_ref = refs

        @pl.when(pl.program_id(1) == 0)
        def _():
            xf = x_ref[...]
            r = lax.rsqrt(jnp.mean(xf * xf, axis=-1, keepdims=True) + EPS)
            h_ref[...] = (xf * r * g_ref[...]).astype(h_ref.dtype)

        h = h_ref[...]
        for k in range(grp):
            acc = _dot(h, w_ref[k])
            if bias is not None:
                acc = acc + b_ref[k]
            u_ref[k] = acc.astype(u_ref.dtype)

    in_specs = [pl.BlockSpec((tm, D), lambda i, j: (i, 0)),
                pl.BlockSpec((1, D), lambda i, j: (0, 0)),
                pl.BlockSpec((grp, None, D, n), lambda i, j: (j, layer, 0, 0))]
    args = [x, gain.reshape(1, D), w]
    if bias is not None:
        in_specs.append(pl.BlockSpec((grp, None, 1, n), lambda i, j: (j, layer, 0, 0)))
        args.append(bias)
    return _call(body, name=name, grid=(T // tm, steps), in_specs=in_specs,
                 out_specs=[pl.BlockSpec((grp, tm, n), lambda i, j: (j, i, 0)),
                            pl.BlockSpec((tm, D), lambda i, j: (i, 0))],
                 out_shape=[_sds((nsh, T, n), MM), _sds((T, D), MM)])(*args)


def mm_resid_norm(name, a, w, layer, bias, x, gain):
    nk, T, kk = a.shape
    D = x.shape[1]
    tm = _tile(T, 512)

    def body(a_ref, w_ref, b_ref, x_ref, g_ref, y_ref, xn_ref):
        y = _dot(a_ref[0], w_ref[0])
        for q in range(1, nk):
            y = y + _dot(a_ref[q], w_ref[q])
        y = y + b_ref[...]
        y_ref[...] = y
        r = lax.rsqrt(jnp.mean(y * y, axis=-1, keepdims=True) + EPS)
        xn_ref[...] = x_ref[...] + y * r * g_ref[...]

    return _call(body, name=name, grid=(T // tm,),
                 in_specs=[pl.BlockSpec((nk, tm, kk), lambda i: (0, i, 0)),
                           pl.BlockSpec((None, nk, kk, D), lambda i: (layer, 0, 0, 0)),
                           pl.BlockSpec((1, D), lambda i: (0, 0)),
                           pl.BlockSpec((tm, D), lambda i: (i, 0)),
                           pl.BlockSpec((1, D), lambda i: (0, 0))],
                 out_specs=[pl.BlockSpec((tm, D), lambda i: (i, 0))] * 2,
                 out_shape=[_sds((T, D), F32)] * 2)(a, w, bias.reshape(1, D), x, gain.reshape(1, D))


def _halo_maps(tm, hb, T):
    per = tm // hb
    last = T // hb - 1
    return (lambda i: jnp.maximum(i * per - 1, 0)), (lambda i: jnp.minimum((i + 1) * per, last))


def cm_glu_conv(name, u, dw, dwb, layer, seq):
    _, T, n = u.shape
    ct = dw.shape[-1]
    per = n // ct
    nct = 4 * per
    tm = _tile(seq, 512)
    tps = seq // tm
    hb = CONV_HALO
    prev, _ = _halo_maps(tm, hb, T)
    u4 = u.reshape(2, 4, T, n)

    rows = _tile(tm, 2 * ROW_CHUNK_FWD)

    def body(u_ref, uh_ref, w_ref, b_ref, o_ref, z_ref):
        first = (pl.program_id(0) % tps) == 0
        um = u_ref[...].astype(F32)
        uh = uh_ref[...].astype(F32)
        z_ref[0, pl.ds(hb, tm), :] = um[0] * _sigmoid(um[1])
        z_ref[0, pl.ds(0, hb), :] = jnp.where(first, 0.0, uh[0] * _sigmoid(uh[1]))
        _stage_shifts(z_ref, hb + tm, back=True)
        for r0 in range(0, tm, rows):
            acc = b_ref[...]
            for j in range(CONV_W):
                acc = acc + w_ref[pl.ds(CONV_W - 1 - j, 1), :] * z_ref[j % 8, pl.ds(hb + r0 - (j - j % 8), rows), :]
            o_ref[pl.ds(r0, rows), :] = acc

    return _call(body, name=name, grid=(T // tm, nct),
                 in_specs=[pl.BlockSpec((2, None, tm, ct), lambda i, c: (0, c // per, i, c % per)),
                           pl.BlockSpec((2, None, hb, ct), lambda i, c: (0, c // per, prev(i), c % per)),
                           pl.BlockSpec((None, None, CONV_W, ct), lambda i, c: (c, layer, 0, 0)),
                           pl.BlockSpec((None, None, 1, ct), lambda i, c: (c, layer, 0, 0))],
                 out_specs=pl.BlockSpec((tm, ct), lambda i, c: (i, c)),
                 out_shape=_sds((T, nct * ct), F32),
                 scratch=[pltpu.VMEM((8, tm + hb, ct), F32)])(u4, u4, dw, dwb)


def _stage_shifts(z_ref, n_rows, back):
    for r in range(1, 8):
        if back:
            z_ref[r, pl.ds(8, n_rows - 8), :] = z_ref[0, pl.ds(8 - r, n_rows - 8), :]
        else:
            z_ref[r, pl.ds(0, n_rows - 8), :] = z_ref[0, pl.ds(r, n_rows - 8), :]


def ln_silu(name, c, g, b):
    T, D = c.shape
    tm = _tile(T, 512)

    def body(c_ref, g_ref, b_ref, s_ref):
        cf = c_ref[...]
        mu = jnp.mean(cf, axis=-1, keepdims=True)
        xc = cf - mu
        r = lax.rsqrt(jnp.mean(xc * xc, axis=-1, keepdims=True) + EPS)
        t = xc * r * g_ref[...] + b_ref[...]
        s_ref[...] = (t * _sigmoid(t)).astype(s_ref.dtype)

    return _call(body, name=name, grid=(T // tm,),
                 in_specs=[pl.BlockSpec((tm, D), lambda i: (i, 0)), pl.BlockSpec((1, D), lambda i: (0, 0)),
                           pl.BlockSpec((1, D), lambda i: (0, 0))],
                 out_specs=pl.BlockSpec((None, tm, D), lambda i: (0, i, 0)),
                 out_shape=_sds((1, T, D), MM))(c, g.reshape(1, D), b.reshape(1, D))


def _col_tiles(n, width):
    return [(c0, min(width, n - c0)) for c0 in range(0, n, width)]


def ffn_in_conv(name, x, gain, wt, dw, dwb, layer, seq):
    T, D = x.shape
    n = wt.shape[2]
    tm = _tile(seq, 1024)
    tps = seq // tm
    hb = 8
    rows = _tile(tm, ROW_CHUNK_FWD)
    K = FFN_CONV_W
    w5 = wt.reshape(2, 4, n, D)
    dw4 = dw.reshape(2, 4, dw.shape[1], K, n)
    dwb4 = dwb.reshape(dwb.shape[0], 2, 4, 1, n)

    def body(x_ref, g_ref, w_ref, cw_ref, cb_ref, u_ref, h_ref, z_ref, ag_ref, pad_ref, carry_ref):
        i, j = pl.program_id(0), pl.program_id(1)
        first = (i % tps) == 0

        @pl.when(j == 0)
        def _():
            xf = x_ref[...]
            r = lax.rsqrt(jnp.mean(xf * xf, axis=-1, keepdims=True) + EPS)
            h_ref[...] = (xf * r * g_ref[...]).astype(h_ref.dtype)

        h = h_ref[...]
        for c0, wc in _col_tiles(n, FFN_COL_TILE):
            cols = pl.ds(c0, wc)
            for half in range(2):
                res = _dot_nt(h, w_ref[half, cols, :])
                u_ref[half, :, cols] = res.astype(u_ref.dtype)
                pad_ref[half, pl.ds(hb, tm), cols] = res
                pad_ref[half, pl.ds(0, hb), cols] = jnp.where(first, 0.0, carry_ref[j, half, :, cols])
            for r0 in range(0, tm, rows):
                conv = []
                for half in range(2):
                    acc = cb_ref[half, :, cols] + cw_ref[half, pl.ds(K - 1, 1), cols] * pad_ref[
                        half, pl.ds(hb + r0, rows), cols]
                    for k in range(K - 1):
                        acc = acc + cw_ref[half, pl.ds(k, 1), cols] * pad_ref[
                            half, pl.ds(hb + r0 - (K - 1) + k, rows), cols]
                    conv.append(acc)
                a, g = conv
                sg = _sigmoid(g)
                silu = g * sg
                z_ref[pl.ds(r0, rows), cols] = (silu * a).astype(z_ref.dtype)
                ag_ref[0, pl.ds(r0, rows), cols] = silu.astype(ag_ref.dtype)
                ag_ref[1, pl.ds(r0, rows), cols] = (a * (sg * (1.0 + g * (1.0 - sg)))).astype(ag_ref.dtype)
            for half in range(2):
                carry_ref[j, half, :, cols] = pad_ref[half, pl.ds(tm, hb), cols]

    u, h, z, ag = _call(
        body, name=name, grid=(T // tm, 4),
        in_specs=[pl.BlockSpec((tm, D), lambda i, j: (i, 0)),
                  pl.BlockSpec((1, D), lambda i, j: (0, 0)),
                  pl.BlockSpec((2, None, n, D), lambda i, j: (0, j, 0, 0)),
                  pl.BlockSpec((2, None, None, K, n), lambda i, j: (0, j, layer, 0, 0)),
                  pl.BlockSpec((None, 2, None, 1, n), lambda i, j: (layer, 0, j, 0, 0))],
        out_specs=[pl.BlockSpec((2, None, tm, n), lambda i, j: (0, j, i, 0)),
                   pl.BlockSpec((tm, D), lambda i, j: (i, 0)),
                   pl.BlockSpec((None, tm, n), lambda i, j: (j, i, 0)),
                   pl.BlockSpec((2, None, tm, n), lambda i, j: (0, j, i, 0))],
        out_shape=[_sds((2, 4, T, n), MM), _sds((T, D), MM), _sds((4, T, n), MM), _sds((2, 4, T, n), MM)],
        scratch=[pltpu.VMEM((2, tm + hb, n), F32), pltpu.VMEM((4, 2, hb, n), F32)])(
            x, gain.reshape(1, D), w5, dw4, dwb4)
    return u.reshape(8, T, n), h, z, ag


def _head_specs(seq, dh, q_heads, kv_heads):
    def spec(per, base):
        def imap(b, h, g):
            f = base + g * N_HEADS + h
            return (f // per, b, f % per)
        return pl.BlockSpec((None, seq, dh), imap)
    return spec(q_heads, 0), spec(kv_heads, 0), spec(kv_heads, N_GROUPS * N_HEADS)


def _rows(start, d, blocks=1):
    size = blocks * ATT_BLOCK
    return pl.ds(start, size, stride=d) if d > 1 else pl.ds(start, size)


def _att_pad():
    return max(ATT_BLOCK * d for d in DILATIONS[:-1])


def _band_mask(first, nblk):
    keys = ATT_BLOCK if nblk == 1 else 2 * ATT_BLOCK
    shape = (ATT_BATCH, ATT_BLOCK, keys)
    qi = lax.broadcasted_iota(jnp.int32, shape, 1)
    kj = lax.broadcasted_iota(jnp.int32, shape, 2)
    if nblk == 1:
        return kj <= qi
    n = (first + lax.broadcasted_iota(jnp.int32, shape, 0)) % nblk
    return (kj >= qi) & (kj <= qi + ATT_BLOCK) & ((n > 0) | (kj >= ATT_BLOCK))


def _block_rows(idx, d, nblk):
    r, n = idx // nblk, idx % nblk
    rq = _rows(r + d * ATT_BLOCK * n, d)
    if nblk == 1:
        return rq, _rows(_att_pad() + r + d * ATT_BLOCK * n, d)
    return rq, _rows(_att_pad() + r + d * ATT_BLOCK * (n - 1), d, blocks=2)


def _bdot(a, b, ca, cb):
    return lax.dot_general(a, b, (((ca,), (cb,)), ((0,), (0,))), preferred_element_type=F32)


def attn_fwd(name, q, kv, nb, seq):
    _, T, qn = q.shape
    dh = qn * N_DEV // (N_GROUPS * N_HEADS)
    scale = 1.0 / (dh ** 0.5)
    qs, ks, vs = _head_specs(seq, dh, qn // dh, kv.shape[2] // dh)

    def body(q_ref, k_ref, v_ref, m_ref, l_ref, qf, kf, vf, *branch):
        og, lg = branch[:N_GROUPS], branch[N_GROUPS:]
        pad = _att_pad()
        qf[...] = q_ref[...].astype(F32)
        for t_ref, s_ref in ((k_ref, kf), (v_ref, vf)):
            s_ref[pl.ds(0, pad), :] = jnp.zeros((pad, dh), F32)
            s_ref[pl.ds(pad, seq), :] = t_ref[...].astype(F32)
        for g in range(N_GROUPS):
            d = DILATIONS[g]
            nblk = seq // d // ATT_BLOCK

            def blocks(it, carry, d=d, nblk=nblk, g=g):
                first = it * ATT_BATCH
                rows = [_block_rows(first + b, d, nblk) for b in range(ATT_BATCH)]
                qb = jnp.stack([qf[rq, :] for rq, _ in rows]).astype(MM)
                kb = jnp.stack([kf[rk, :] for _, rk in rows]).astype(MM)
                vb = jnp.stack([vf[rk, :] for _, rk in rows]).astype(MM)
                s = jnp.where(_band_mask(first, nblk), _bdot(qb, kb, 2, 2) * scale, NEG)
                m = jnp.max(s, axis=-1, keepdims=True)
                p = jnp.exp(s - m)
                den = jnp.sum(p, axis=-1, keepdims=True)
                o = _bdot(p.astype(MM), vb, 2, 1) / den
                lse = m + jnp.log(den)
                for b, (rq, _) in enumerate(rows):
                    og[g][rq, :] = o[b]
                    lg[g][rq, :] = jnp.broadcast_to(lse[b], (ATT_BLOCK, dh))
                return carry

            @pl.when(pl.program_id(2) == g)
            def _(blocks=blocks, d=d, nblk=nblk):
                lax.fori_loop(0, d * nblk // ATT_BATCH, blocks, 0)

        @pl.when(pl.program_id(2) == N_GROUPS - 1)
        def _():
            mx = jnp.maximum(jnp.maximum(lg[0][...], lg[1][...]), lg[2][...])
            e = [jnp.exp(lg[g][...] - mx) for g in range(N_GROUPS)]
            tot = e[0] + e[1] + e[2]
            m_ref[...] = ((e[0] * og[0][...] + e[1] * og[1][...] + e[2] * og[2][...]) / tot).astype(m_ref.dtype)
            l_ref[...] = mx + jnp.log(tot)

    return _call(body, name=name, grid=(nb, N_HEADS, N_GROUPS), in_specs=[qs, ks, vs],
                 out_specs=[pl.BlockSpec((None, seq, dh), lambda b, h, g: (0, b, h)),
                            pl.BlockSpec((seq, dh), lambda b, h, g: (b, h))],
                 out_shape=[_sds((1, T, N_HEADS * dh), MM), _sds((T, N_HEADS * dh), F32)],
                 scratch=[pltpu.VMEM((seq, dh), F32)] + [pltpu.VMEM((_att_pad() + seq, dh), F32)] * 2
                 + [pltpu.VMEM((seq, dh), F32)] * (2 * N_GROUPS))(q, kv, kv)


def loss_fwd_bwd(name, x, target):
    T, D = x.shape
    tm = _tile(T, 512)

    def body(x_ref, t_ref, dx_ref, l_ref):
        @pl.when(pl.program_id(0) == 0)
        def _():
            l_ref[...] = jnp.zeros_like(l_ref)
        err = x_ref[...] - t_ref[...]
        dx_ref[...] = err * (1.0 / D)
        l_ref[...] += 0.5 * jnp.sum(jnp.mean(err * err, axis=-1, keepdims=True), axis=0, keepdims=True)

    dx, l = _call(body, name=name, grid=(T // tm,),
                  in_specs=[pl.BlockSpec((tm, D), lambda i: (i, 0))] * 2,
                  out_specs=[pl.BlockSpec((tm, D), lambda i: (i, 0)), pl.BlockSpec((1, 1), lambda i: (0, 0))],
                  out_shape=[_sds((T, D), F32), _sds((1, 1), F32)])(x, target)
    return dx, l


def resid_norm_bwd(name, dx, y, gain):
    T, D = y.shape
    tm = _tile(T, 1024)

    def body(dx_ref, y_ref, g_ref, dy_ref, dg_ref, db_ref):
        @pl.when(pl.program_id(0) == 0)
        def _():
            dg_ref[...] = jnp.zeros_like(dg_ref)
            db_ref[...] = jnp.zeros_like(db_ref)
        y = y_ref[...]
        d = dx_ref[...]
        r = lax.rsqrt(jnp.mean(y * y, axis=-1, keepdims=True) + EPS)
        yh = y * r
        dyh = d * g_ref[...]
        dy = r * (dyh - yh * jnp.mean(dyh * yh, axis=-1, keepdims=True))
        dy_ref[...] = dy.astype(dy_ref.dtype)
        dg_ref[...] += jnp.sum(d * yh, axis=0, keepdims=True)
        db_ref[...] += jnp.sum(dy, axis=0, keepdims=True)

    return _call(body, name=name, grid=(T // tm,),
                 in_specs=[pl.BlockSpec((tm, D), lambda i: (i, 0))] * 2 + [pl.BlockSpec((1, D), lambda i: (0, 0))],
                 out_specs=[pl.BlockSpec((None, tm, D), lambda i: (0, i, 0))] + [pl.BlockSpec((1, D), lambda i: (0, 0))] * 2,
                 out_shape=[_sds((1, T, D), MM), _sds((1, D), F32), _sds((1, D), F32)])(dx, y, gain.reshape(1, D))


def mm_nt(name, dy, w, layer, out_dtype):
    _, T, D = dy.shape
    _, nk, kk, _ = w.shape
    tm = _tile(T, 1024)

    def body(dy_ref, w_ref, o_ref):
        o_ref[...] = _dot_nt(dy_ref[...], w_ref[...]).astype(o_ref.dtype)

    return _call(body, name=name, grid=(T // tm, nk),
                 in_specs=[pl.BlockSpec((None, tm, D), lambda i, q: (0, i, 0)),
                           pl.BlockSpec((None, None, kk, D), lambda i, q: (layer, q, 0, 0))],
                 out_specs=pl.BlockSpec((None, tm, kk), lambda i, q: (q, i, 0)),
                 out_shape=_sds((nk, T, kk), out_dtype))(dy, w)


def mm_nt_norm_bwd(name, du, w, layer, x_in, gain, dx_res):
    nsh, T, n = du.shape
    D = x_in.shape[1]
    tm = _tile(T, 512)
    grp = nsh if n <= GROUP_ALL_MAX_N else nsh // 2
    steps = nsh // grp

    def body(du_ref, w_ref, x_ref, g_ref, dr_ref, dx_ref, dg_ref, acc_ref):
        i, j = pl.program_id(0), pl.program_id(1)

        @pl.when((i == 0) & (j == 0))
        def _():
            dg_ref[...] = jnp.zeros_like(dg_ref)

        part = _dot_nt(du_ref[0], w_ref[0])
        for k in range(1, grp):
            part = part + _dot_nt(du_ref[k], w_ref[k])

        @pl.when(j == 0)
        def _():
            acc_ref[...] = part

        @pl.when(j > 0)
        def _():
            acc_ref[...] += part

        @pl.when(j == steps - 1)
        def _():
            x = x_ref[...]
            dh = acc_ref[...]
            r = lax.rsqrt(jnp.mean(x * x, axis=-1, keepdims=True) + EPS)
            xh = x * r
            dxh = dh * g_ref[...]
            dx_ref[...] = dr_ref[...] + r * (dxh - xh * jnp.mean(dxh * xh, axis=-1, keepdims=True))
            dg_ref[...] += jnp.sum(dh * xh, axis=0, keepdims=True)

    return _call(body, name=name, grid=(T // tm, steps),
                 in_specs=[pl.BlockSpec((grp, tm, n), lambda i, j: (j, i, 0)),
                           pl.BlockSpec((grp, None, D, n), lambda i, j: (j, layer, 0, 0)),
                           pl.BlockSpec((tm, D), lambda i, j: (i, 0)),
                           pl.BlockSpec((1, D), lambda i, j: (0, 0)),
                           pl.BlockSpec((tm, D), lambda i, j: (i, 0))],
                 out_specs=[pl.BlockSpec((tm, D), lambda i, j: (i, 0)), pl.BlockSpec((1, D), lambda i, j: (0, 0))],
                 out_shape=[_sds((T, D), F32), _sds((1, D), F32)],
                 scratch=[pltpu.VMEM((tm, D), F32)])(du, w, x_in, gain.reshape(1, D), dx_res)


def mm_tn(name, a, b):
    na, T, ka = a.shape
    nb, _, kb = b.shape
    nj = max(na, nb)

    def body(a_ref, b_ref, o_ref):
        o_ref[...] = _dot_tn(a_ref[...], b_ref[...]).astype(o_ref.dtype)

    return _call(body, name=name, grid=(nj,),
                 in_specs=[pl.BlockSpec((None, T, ka), (lambda j: (j, 0, 0)) if na > 1 else (lambda j: (0, 0, 0))),
                           pl.BlockSpec((None, T, kb), (lambda j: (j, 0, 0)) if nb > 1 else (lambda j: (0, 0, 0)))],
                 out_specs=pl.BlockSpec((None, ka, kb), lambda j: (j, 0, 0)),
                 out_shape=_sds((nj, ka, kb), MM))(a, b)


def ffn_bwd(name, dy, wout, u, ag, dw, wt, x_in, gain, dx_res, layer, seq):
    _, T, D = dy.shape
    n = wt.shape[2]
    tm = _tile(seq, 512)
    tps = seq // tm
    hb = FFN_HALO
    _, nxt = _halo_maps(tm, hb, T)
    K = FFN_CONV_W
    te = tm + hb
    rows = _tile(hb, ROW_CHUNK_BWD)
    u4, w5 = u.reshape(2, 4, T, n), wt.reshape(2, 4, n, D)
    dw4 = dw.reshape(2, 4, dw.shape[1], K, n)
    tiles = _col_tiles(n, FFN_COL_TILE)

    def body(dy_ref, dyn_ref, wo_ref, u_ref, ag_ref, agn_ref, cw_ref, wt_ref, x_ref, g_ref, dr_ref,
             du_ref, ddw_ref, ddb_ref, dx_ref, dg_ref,
             dzf_ref, da_ref, acc_ref):
        i, j = pl.program_id(0), pl.program_id(1)
        last = (i % tps) == tps - 1

        @pl.when((i == 0) & (j == 0))
        def _():
            dg_ref[...] = jnp.zeros_like(dg_ref)

        @pl.when(i == 0)
        def _():
            ddw_ref[j] = jnp.zeros((2, K, n), F32)
            ddb_ref[j] = jnp.zeros((2, 1, n), F32)

        dyt, dyn = dy_ref[...], dyn_ref[...]
        for ci, (c0, wc) in enumerate(tiles):
            cols = pl.ds(c0, wc)
            dzf_ref[pl.ds(0, tm), cols] = _dot_nt(dyt, wo_ref[cols, :])
            dzf_ref[pl.ds(tm, hb), cols] = jnp.where(last, 0.0, _dot_nt(dyn, wo_ref[cols, :]))
            for r0 in range(0, te, rows):
                dzc = dzf_ref[pl.ds(r0, rows), cols]
                for half in range(2):
                    src = ag_ref[half, pl.ds(r0, rows), cols] if r0 < tm else agn_ref[half, pl.ds(r0 - tm, rows), cols]
                    da_ref[half, 0, pl.ds(r0, rows), cols] = dzc * src.astype(F32)
            for half in range(2):
                for s in range(1, K):
                    da_ref[half, s, pl.ds(0, tm), cols] = da_ref[half, 0, pl.ds(s, tm), cols]
                tap_acc = [jnp.zeros((rows, wc), F32) for _ in range(K)]
                bias_acc = jnp.zeros((rows, wc), F32)
                for r0 in range(0, tm, rows):
                    xr = u_ref[half, pl.ds(r0, rows), cols].astype(F32)
                    acc = None
                    for k in range(K):
                        d = da_ref[half, K - 1 - k, pl.ds(r0, rows), cols]
                        term = cw_ref[half, pl.ds(k, 1), cols] * d
                        acc = term if acc is None else acc + term
                        tap_acc[k] = tap_acc[k] + d * xr
                        if k == K - 1:
                            bias_acc = bias_acc + d
                    du_ref[half, pl.ds(r0, rows), cols] = acc.astype(du_ref.dtype)
                for k in range(K):
                    ddw_ref[j, half, pl.ds(k, 1), cols] += jnp.sum(tap_acc[k], axis=0, keepdims=True)
                ddb_ref[j, half, :, cols] += jnp.sum(bias_acc, axis=0, keepdims=True)
            part = _dot(du_ref[0, :, cols], wt_ref[0, cols, :]) + _dot(du_ref[1, :, cols], wt_ref[1, cols, :])
            if ci == 0:
                acc_ref[...] = part + jnp.where(j == 0, 0.0, acc_ref[...])
            else:
                acc_ref[...] += part
        @pl.when(j == 3)
        def _():
            x = x_ref[...]
            dh = acc_ref[...]
            r = lax.rsqrt(jnp.mean(x * x, axis=-1, keepdims=True) + EPS)
            xh = x * r
            dxh = dh * g_ref[...]
            dx_ref[...] = dr_ref[...] + r * (dxh - xh * jnp.mean(dxh * xh, axis=-1, keepdims=True))
            dg_ref[...] += jnp.sum(dh * xh, axis=0, keepdims=True)

    f32 = lambda *shape: pltpu.VMEM(shape, F32)
    du, ddw, ddb, dx, dg = _call(
        body, name=name, grid=(T // tm, 4),
        in_specs=[pl.BlockSpec((None, tm, D), lambda i, j: (0, i, 0)),
                  pl.BlockSpec((None, hb, D), lambda i, j: (0, nxt(i), 0)),
                  pl.BlockSpec((None, None, n, D), lambda i, j: (0, j, 0, 0)),
                  pl.BlockSpec((2, None, tm, n), lambda i, j: (0, j, i, 0)),
                  pl.BlockSpec((2, None, tm, n), lambda i, j: (0, j, i, 0)),
                  pl.BlockSpec((2, None, hb, n), lambda i, j: (0, j, nxt(i), 0)),
                  pl.BlockSpec((2, None, None, K, n), lambda i, j: (0, j, layer, 0, 0)),
                  pl.BlockSpec((2, None, n, D), lambda i, j: (0, j, 0, 0)),
                  pl.BlockSpec((tm, D), lambda i, j: (i, 0)),
                  pl.BlockSpec((1, D), lambda i, j: (0, 0)),
                  pl.BlockSpec((tm, D), lambda i, j: (i, 0))],
        out_specs=[pl.BlockSpec((2, None, tm, n), lambda i, j: (0, j, i, 0)),
                   pl.BlockSpec((4, 2, K, n), lambda i, j: (0, 0, 0, 0)),
                   pl.BlockSpec((4, 2, 1, n), lambda i, j: (0, 0, 0, 0)),
                   pl.BlockSpec((tm, D), lambda i, j: (i, 0)),
                   pl.BlockSpec((1, D), lambda i, j: (0, 0))],
        out_shape=[_sds((2, 4, T, n), MM), _sds((4, 2, K, n), F32), _sds((4, 2, 1, n), F32), _sds((T, D), F32),
                   _sds((1, D), F32)],
        scratch=[f32(te, n), f32(2, K, te, n), f32(tm, D)],
    )(dy, dy, wout, u4, ag, ag, dw4, w5, x_in, gain.reshape(1, D), dx_res)
    return du.reshape(8, T, n), jnp.swapaxes(ddw, 0, 1), jnp.swapaxes(ddb, 0, 1), dx, dg


def ln_silu_bwd(name, ds, c, g, b):
    T, D = c.shape
    tm = _tile(T, 512)

    def body(ds_ref, c_ref, g_ref, b_ref, dc_ref, dg_ref, db_ref):
        @pl.when(pl.program_id(0) == 0)
        def _():
            dg_ref[...] = jnp.zeros_like(dg_ref)
            db_ref[...] = jnp.zeros_like(db_ref)
        cf = c_ref[...]
        mu = jnp.mean(cf, axis=-1, keepdims=True)
        xc = cf - mu
        r = lax.rsqrt(jnp.mean(xc * xc, axis=-1, keepdims=True) + EPS)
        xh = xc * r
        t = xh * g_ref[...] + b_ref[...]
        sg = _sigmoid(t)
        dt = ds_ref[...] * (sg * (1.0 + t * (1.0 - sg)))
        dg_ref[...] += jnp.sum(dt * xh, axis=0, keepdims=True)
        db_ref[...] += jnp.sum(dt, axis=0, keepdims=True)
        dxh = dt * g_ref[...]
        dc_ref[...] = r * (dxh - jnp.mean(dxh, axis=-1, keepdims=True)
                           - xh * jnp.mean(dxh * xh, axis=-1, keepdims=True))

    vec = pl.BlockSpec((1, D), lambda i: (0, 0))
    return _call(body, name=name, grid=(T // tm,),
                 in_specs=[pl.BlockSpec((None, tm, D), lambda i: (0, i, 0)), pl.BlockSpec((tm, D), lambda i: (i, 0)),
                           vec, vec],
                 out_specs=[pl.BlockSpec((tm, D), lambda i: (i, 0)), vec, vec],
                 out_shape=[_sds((T, D), F32), _sds((1, D), F32), _sds((1, D), F32)])(
                     ds, c, g.reshape(1, D), b.reshape(1, D))


def cm_glu_conv_bwd(name, u, dc, dw, layer, seq):
    _, T, n = u.shape
    ct = dw.shape[-1]
    per = n // ct
    nct = 4 * per
    tm = _tile(seq, 512)
    tps = seq // tm
    hb = CONV_HALO
    prev, nxt = _halo_maps(tm, hb, T)
    u4 = u.reshape(2, 4, T, n)
    K = CONV_W

    rows = _tile(tm, 2 * ROW_CHUNK_FWD)
    wrows = 8

    def body(u_ref, up_ref, dc_ref, dcn_ref, w_ref, du_ref, ddw_ref, ddb_ref, dbi_ref, zp_ref, zd_ref):
        i = pl.program_id(1)
        first = (i % tps) == 0
        last = (i % tps) == tps - 1

        @pl.when(i == 0)
        def _():
            ddw_ref[...] = jnp.zeros_like(ddw_ref)
            ddb_ref[...] = jnp.zeros_like(ddb_ref)
            dbi_ref[...] = jnp.zeros_like(dbi_ref)

        uh = up_ref[...].astype(F32)
        zp_ref[0, pl.ds(hb, tm), :] = u_ref[0].astype(F32) * _sigmoid(u_ref[1].astype(F32))
        zp_ref[0, pl.ds(0, hb), :] = jnp.where(first, 0.0, uh[0] * _sigmoid(uh[1]))
        zd_ref[0, pl.ds(0, tm), :] = dc_ref[...]
        zd_ref[0, pl.ds(tm, hb), :] = jnp.where(last, 0.0, dcn_ref[...])
        _stage_shifts(zp_ref, hb + tm, back=True)
        _stage_shifts(zd_ref, tm + hb, back=False)
        for r0 in range(0, tm, rows):
            dp = None
            for j in range(K):
                term = w_ref[pl.ds(K - 1 - j, 1), :] * zd_ref[j % 8, pl.ds(r0 + (j - j % 8), rows), :]
                dp = term if dp is None else dp + term
            v = u_ref[0, pl.ds(r0, rows), :].astype(F32)
            sg = _sigmoid(u_ref[1, pl.ds(r0, rows), :].astype(F32))
            dv = dp * sg
            dg = dp * v * sg * (1.0 - sg)
            du_ref[0, pl.ds(r0, rows), :] = dv.astype(du_ref.dtype)
            du_ref[1, pl.ds(r0, rows), :] = dg.astype(du_ref.dtype)
            dbi_ref[0] += jnp.sum(dv, axis=0, keepdims=True)
            dbi_ref[1] += jnp.sum(dg, axis=0, keepdims=True)
        tap_acc = [jnp.zeros((wrows, ct), F32) for _ in range(K)]
        bias_acc = jnp.zeros((wrows, ct), F32)
        for r0 in range(0, tm, wrows):
            d = zd_ref[0, pl.ds(r0, wrows), :]
            for j in range(K):
                tap_acc[j] = tap_acc[j] + d * zp_ref[j % 8, pl.ds(hb + r0 - (j - j % 8), wrows), :]
            bias_acc = bias_acc + d
        for j in range(K):
            ddw_ref[pl.ds(K - 1 - j, 1), :] += jnp.sum(tap_acc[j], axis=0, keepdims=True)
        ddb_ref[...] += jnp.sum(bias_acc, axis=0, keepdims=True)

    du, ddw, ddb, dbi = _call(
        body, name=name, grid=(nct, T // tm),
        in_specs=[pl.BlockSpec((2, None, tm, ct), lambda c, i: (0, c // per, i, c % per)),
                  pl.BlockSpec((2, None, hb, ct), lambda c, i: (0, c // per, prev(i), c % per)),
                  pl.BlockSpec((tm, ct), lambda c, i: (i, c)),
                  pl.BlockSpec((hb, ct), lambda c, i: (nxt(i), c)),
                  pl.BlockSpec((None, None, K, ct), lambda c, i: (c, layer, 0, 0))],
        out_specs=[pl.BlockSpec((2, None, tm, ct), lambda c, i: (0, c // per, i, c % per)),
                   pl.BlockSpec((None, K, ct), lambda c, i: (c, 0, 0)),
                   pl.BlockSpec((None, 1, ct), lambda c, i: (c, 0, 0)),
                   pl.BlockSpec((2, None, 1, ct), lambda c, i: (0, c // per, 0, c % per))],
        out_shape=[_sds((2, 4, T, n), MM), _sds((nct, K, ct), F32), _sds((nct, 1, ct), F32), _sds((2, 4, 1, n), F32)],
        scratch=[pltpu.VMEM((8, tm + hb, ct), F32), pltpu.VMEM((8, tm + hb, ct), F32)])(u4, u4, dc, dc, dw)
    return du.reshape(8, T, n), ddw, ddb, dbi


def attn_bwd(name, q, kv, dm, merged, lse, dkv_prev, nb, seq):
    _, T, qn = q.shape
    kn = kv.shape[2]
    dh = qn * N_DEV // (N_GROUPS * N_HEADS)
    scale = 1.0 / (dh ** 0.5)
    qs, ks, vs = _head_specs(seq, dh, qn // dh, kn // dh)
    has_prev = dkv_prev is not None
    n_in = 6 + (1 if has_prev else 0)

    def body(*refs):
        q_ref, k_ref, v_ref, dm_ref, mg_ref, l_ref = refs[:6]
        pkv_ref = refs[6] if has_prev else None
        dq_ref, dkv_ref = refs[n_in:n_in + 2]
        qf, kf, vf, dqf, dkf, dvf, dlt = refs[n_in + 2:]
        pad = _att_pad()

        @pl.when(pl.program_id(2) == 0)
        def _():
            dlt[...] = jnp.broadcast_to(
                jnp.sum(dm_ref[...] * mg_ref[...].astype(F32), axis=-1, keepdims=True), (seq, dh))

        qf[...] = q_ref[...].astype(F32)
        for t_ref, s_ref in ((k_ref, kf), (v_ref, vf)):
            s_ref[pl.ds(0, pad), :] = jnp.zeros((pad, dh), F32)
            s_ref[pl.ds(pad, seq), :] = t_ref[...].astype(F32)
        dkf[...] = jnp.zeros_like(dkf)
        dvf[...] = jnp.zeros_like(dvf)
        for g in range(N_GROUPS):
            d = DILATIONS[g]
            nblk = seq // d // ATT_BLOCK

            def blocks(it, carry, d=d, nblk=nblk):
                first = it * ATT_BATCH
                rows = [_block_rows(first + b, d, nblk) for b in range(ATT_BATCH)]
                qb = jnp.stack([qf[rq, :] for rq, _ in rows]).astype(MM)
                dmb = jnp.stack([dm_ref[rq, :] for rq, _ in rows]).astype(MM)
                lse = jnp.stack([l_ref[rq, :][:, :1] for rq, _ in rows])
                delta = jnp.stack([dlt[rq, :][:, :1] for rq, _ in rows])
                kb = jnp.stack([kf[rk, :] for _, rk in rows]).astype(MM)
                vb = jnp.stack([vf[rk, :] for _, rk in rows]).astype(MM)
                s = jnp.where(_band_mask(first, nblk), _bdot(qb, kb, 2, 2) * scale, NEG)
                p = jnp.exp(s - lse)
                dsc = (p * (_bdot(dmb, vb, 2, 2) - delta) * scale).astype(MM)
                dv = _bdot(p.astype(MM), dmb, 1, 1)
                dk = _bdot(dsc, qb, 1, 1)
                dq = _bdot(dsc, kb, 2, 1)
                for b, (rq, rk) in enumerate(rows):
                    dqf[rq, :] = dq[b]
                    dkf[rk, :] += dk[b]
                    dvf[rk, :] += dv[b]
                return carry

            @pl.when(pl.program_id(2) == g)
            def _(blocks=blocks, d=d, nblk=nblk):
                lax.fori_loop(0, d * nblk // ATT_BATCH, blocks, 0)

        dq_ref[...] = dqf[...].astype(dq_ref.dtype)
        dk, dv = dkf[pl.ds(pad, seq), :], dvf[pl.ds(pad, seq), :]
        if has_prev:
            dk, dv = dk + pkv_ref[0].astype(F32), dv + pkv_ref[1].astype(F32)
        dkv_ref[0] = dk.astype(dkv_ref.dtype)
        dkv_ref[1] = dv.astype(dkv_ref.dtype)

    per = kn // dh

    def both(b, h, g):
        f = g * N_HEADS + h
        return (0, f // per, b, f % per)

    kv_spec = pl.BlockSpec((2, None, seq, dh), both)
    full = pl.BlockSpec((None, seq, dh), lambda b, h, g: (0, b, h))
    in_specs = [qs, ks, vs, full, full, pl.BlockSpec((seq, dh), lambda b, h, g: (b, h))]
    args = [q, kv, kv, dm, merged, lse]
    if has_prev:
        in_specs.append(kv_spec)
        args.append(dkv_prev.reshape(2, N_DEV // 2, T, kn))
    short, padded = pltpu.VMEM((seq, dh), F32), pltpu.VMEM((_att_pad() + seq, dh), F32)
    dq, dkv = _call(body, name=name, grid=(nb, N_HEADS, N_GROUPS), in_specs=in_specs, out_specs=[qs, kv_spec],
                    out_shape=[_sds(q.shape, MM), _sds((2, N_DEV // 2, T, kn), MM)],
                    scratch=[short, padded, padded, short, padded, padded, short])(*args)
    return dq, dkv.reshape(N_DEV, T, kn)


def _adamw_math(w, g, m, v):
    m = ADAM_B1 * m + (1.0 - ADAM_B1) * g
    v = ADAM_B2 * v + (1.0 - ADAM_B2) * (g * g)
    m_hat = m / (1.0 - ADAM_B1 ** ADAM_STEP)
    v_hat = v / (1.0 - ADAM_B2 ** ADAM_STEP)
    delta = -ADAM_LR * (m_hat / (jnp.sqrt(v_hat) + ADAM_EPS) + ADAM_WD * w)
    return delta, m, v


def adamw_sum(name, w, m, v, parts, after):
    L, R, C = w.shape
    tr = _tile(R, 256)

    def body(*refs):
        w_ref, m_ref, v_ref = refs[:3]
        p_refs = refs[3:3 + L]
        g_ref, d_ref, nm_ref, nv_ref = refs[4 + L:]
        for l in range(L):
            @pl.when(pl.program_id(0) == l)
            def _(p_ref=p_refs[l]):
                g = p_ref[0].astype(F32)
                for k in range(1, N_DEV):
                    g = g + p_ref[k].astype(F32)
                g_ref[...] = g
                d_ref[...], nm_ref[...], nv_ref[...] = _adamw_math(w_ref[...], g, m_ref[...], v_ref[...])

    blk = pl.BlockSpec((None, tr, C), lambda l, i: (l, i, 0))
    part = lambda k: pl.BlockSpec((N_DEV, tr, C), lambda l, i: (0, jnp.where(l == k, i, 0), 0))
    return _call(body, name=name, grid=(L, R // tr),
                 in_specs=[blk, blk, blk] + [part(k) for k in range(L)] + [pl.BlockSpec(memory_space=pl.ANY)],
                 out_specs=[blk] * 4, out_shape=[_sds((L, R, C), F32)] * 4)(w, m, v, *parts, after)


def sum_partials(name, parts):
    _, R, C = parts.shape
    tr = _tile(R, 512)

    def body(p_ref, o_ref):
        g = p_ref[0]
        for k in range(1, N_DEV):
            g = g + p_ref[k]
        o_ref[...] = g

    return _call(body, name=name, grid=(R // tr,),
                 in_specs=[pl.BlockSpec((N_DEV, tr, C), lambda i: (0, i, 0))],
                 out_specs=pl.BlockSpec((tr, C), lambda i: (i, 0)), out_shape=_sds((R, C), F32))(parts)


def adamw_small(name, w, g, m, v):
    R, C = w.shape
    tr = _tile(R, 512)

    def body(w_ref, g_ref, m_ref, v_ref, d_ref, nm_ref, nv_ref):
        d_ref[...], nm_ref[...], nv_ref[...] = _adamw_math(w_ref[...], g_ref[...], m_ref[...], v_ref[...])

    blk = pl.BlockSpec((tr, C), lambda i: (i, 0))
    return _call(body, name=name, grid=(R // tr,), in_specs=[blk] * 4, out_specs=[blk] * 3,
                 out_shape=[_sds((R, C), F32)] * 3)(w, g, m, v)


def _pack(arrays):
    pieces = []
    for a in arrays:
        f = a.reshape(-1).astype(F32)
        pieces.append(jnp.pad(f, (0, (-f.shape[0]) % PACK)))
    return jnp.concatenate(pieces).reshape(-1, 128)


def _unpack(flat, shapes):
    out, off = [], 0
    f = flat.reshape(-1)
    for s in shapes:
        size = 1
        for d in s:
            size *= d
        out.append(f[off:off + size].reshape(s))
        off += size + (-size) % PACK
    return out


def kernel(x, mix_pre_g, mix_post_g, ffn_pre_g, ffn_post_g, cm_w_in, cm_b_in, cm_dw, cm_dw_b, cm_ln_g, cm_ln_b, cm_w_out, cm_b_out, kv_norm_g, w_kv, w_q, w_o, ffn_w_in, ffn_dw, ffn_dw_b, ffn_w_out, loss_target, m_mix_pre_g, m_mix_post_g, m_ffn_pre_g, m_ffn_post_g, m_cm_w_in, m_cm_b_in, m_cm_dw, m_cm_dw_b, m_cm_ln_g, m_cm_ln_b, m_cm_w_out, m_cm_b_out, m_kv_norm_g, m_w_kv, m_w_q, m_w_o, m_ffn_w_in, m_ffn_dw, m_ffn_dw_b, m_ffn_w_out, v_mix_pre_g, v_mix_post_g, v_ffn_pre_g, v_ffn_post_g, v_cm_w_in, v_cm_b_in, v_cm_dw, v_cm_dw_b, v_cm_ln_g, v_cm_ln_b, v_cm_w_out, v_cm_b_out, v_kv_norm_g, v_w_kv, v_w_q, v_w_o, v_ffn_w_in, v_ffn_dw, v_ffn_dw_b, v_ffn_w_out):
    nb, seq, D = x.shape
    T = nb * seq
    me = _my_index()
    n_b = DEPTH - N_A

    nf = ffn_w_in.shape[-1]
    t_ = lambda t: jnp.swapaxes(t, 1, 2)
    fin_t, m_fin_t, v_fin_t = t_(ffn_w_in), t_(m_ffn_w_in), t_(v_ffn_w_in)

    stages = [(part, i) for i in range(DEPTH) for part in ("mix", "ffn")]
    stages.insert(stages.index(("ffn", N_A - 1)) + 1, ("kv", N_A - 1))

    def stage_sources(stage):
        part, i = stage
        if part == "ffn":
            src = {"fin": fin_t[i], "fout": ffn_w_out[i]}
        elif part == "kv":
            src = {"kv": w_kv}
        elif i < N_A:
            src = {"cin": cm_w_in[i], "cout": cm_w_out[i]}
        else:
            src = {"q": w_q[i - N_A], "o": w_o[i - N_A]}
        return {k: t.astype(MM) for k, t in src.items()}

    def begin_gather(stage, after):
        src = stage_sources(stage)
        names, arrays = list(src), list(src.values())
        tag = f"{stage[0]}{stage[1]}"
        lands = place_own(f"gather_own_{tag}", "gather", arrays)
        handle, token = exchange_begin(f"gather_begin_{tag}", "gather", arrays, lands, after)
        return (names, handle), token

    def end_gather(stage, pending, after):
        names, handle = pending
        W = dict(zip(names, exchange_end(f"gather_end_{stage[0]}{stage[1]}", handle, after)))
        for k in W:
            if k in ("cout", "o"):
                W[k] = W[k].reshape(1, 1, D, D)
            elif k == "fout":
                W[k] = W[k].reshape(1, 4, nf, D)
            else:
                W[k] = W[k][:, None]
        return W

    small = [cm_b_in[:, None, :], cm_dw, cm_dw_b[:, None, :], cm_ln_g, cm_ln_b, cm_b_out, ffn_dw]
    Bcin, DWc, DWBc, LNg, LNb, Bcout, DWf = all_gather("gather_small", small, [False] * len(small))
    LNg = jnp.swapaxes(LNg, 0, 1).reshape(N_A, D)
    LNb = jnp.swapaxes(LNb, 0, 1).reshape(N_A, D)
    Bcout = jnp.swapaxes(Bcout, 0, 1).reshape(N_A, D)
    DWBf = ffn_dw_b.reshape(DEPTH, N_DEV, 1, nf)
    zero_bias = jnp.zeros((D,), F32)

    xs = x.reshape(T, D)
    sv = []
    kv = hkv = None
    pending = {}
    pending[stages[0]], _ = begin_gather(stages[0], DWf)
    Ws = {stages[0]: end_gather(stages[0], pending[stages[0]], xs)}
    pending[stages[1]], token = begin_gather(stages[1], next(iter(Ws[stages[0]].values())))
    sv = [{} for _ in range(DEPTH)]
    for idx, stage in enumerate(stages):
        part, i = stage
        L, W = sv[i], Ws[stage]
        gain = {"mix": mix_pre_g[i], "ffn": ffn_pre_g[i], "kv": kv_norm_g}[part]
        following = stages[idx + 1] if idx + 1 < len(stages) else None
        if idx + 2 < len(stages):
            after = token if idx == 0 else next(iter(W.values()))
            pending[stages[idx + 2]], token = begin_gather(stages[idx + 2], after)
        if idx + 2 < len(stages) or idx == 0:
            gain = gain + token[0, 0]
        if part == "mix":
            L["x_in"] = xs
            if i < N_A:
                L["u"], L["h"] = norm_mm(f"cm_in_{i}", xs, gain, W["cin"], 0, Bcin[:, i:i + 1])
                L["c"] = cm_glu_conv(f"cm_conv_{i}", L["u"], DWc, DWBc, i, seq)
                L["s"] = ln_silu(f"cm_ln_{i}", L["c"], LNg[i], LNb[i])
                L["y"], xs = mm_resid_norm(f"cm_out_{i}", L["s"], W["cout"], 0, Bcout[i], xs, mix_post_g[i])
            else:
                L["q"], L["h"] = norm_mm(f"attn_q_{i}", xs, gain, W["q"], 0)
                L["mg"], L["lse"] = attn_fwd(f"attn_{i}", L["q"], kv, nb, seq)
                L["y"], xs = mm_resid_norm(f"attn_out_{i}", L["mg"], W["o"], 0, zero_bias, xs, mix_post_g[i])
            L["x1"] = xs
        elif part == "ffn":
            L["uf"], L["hf"], L["z"], L["ag"] = ffn_in_conv(f"ffn_in_{i}", xs, gain, W["fin"], DWf, DWBf, i, seq)
            L["yf"], xs = mm_resid_norm(f"ffn_out_{i}", L["z"], W["fout"], 0, zero_bias, xs, ffn_post_g[i])
        else:
            kv, hkv = norm_mm("kv_proj", xs, gain, W["kv"], 0)
        if following is not None:
            Ws[following] = end_gather(following, pending[following], kv if part == "kv" else xs)
    dx, loss_part = loss_fwd_bwd("loss", xs, loss_target.reshape(T, D))

    g_mix_pre, g_mix_post, g_ffn_pre, g_ffn_post = [None] * DEPTH, [None] * DEPTH, [None] * DEPTH, [None] * DEPTH
    g_ffn_dw, g_ffn_dwb = [None] * DEPTH, [None] * DEPTH
    g_cbin, g_cdw, g_cdwb, g_lng, g_lnb, g_cbout = ([None] * N_A for _ in range(6))
    g_kvn = dkv = None
    landed = [{} for _ in range(DEPTH)]
    in_flight = token = None
    for stage in reversed(stages):
        part, i = stage
        L, W = sv[i], Ws[stage]
        gain = {"mix": mix_post_g[i], "ffn": ffn_post_g[i], "kv": kv_norm_g}[part]
        if token is not None:
            gain = gain + token[0, 0]
        send = {}
        if part == "kv":
            send["kv"] = mm_tn("kv_wg", hkv[None], dkv)
            dx, g_kvn = mm_nt_norm_bwd("kv_bwd", dkv, W["kv"], 0, sv[i + 1]["x_in"], gain, dx)
        elif part == "ffn":
            dyf, g_ffn_post[i], _ = resid_norm_bwd(f"ffn_post_bwd_{i}", dx, L["yf"], gain)
            send["fout"] = mm_tn(f"ffn_out_wg_{i}", L["z"], dyf).reshape(N_DEV, nf // 2, D)
            duf, ddw, ddwb, dx, g_ffn_pre[i] = ffn_bwd(f"ffn_bwd_{i}", dyf, W["fout"], L["uf"], L["ag"], DWf, W["fin"],
                                                       L["x1"], ffn_pre_g[i], dx, i, seq)
            g_ffn_dw[i], g_ffn_dwb[i] = ddw.reshape(N_DEV, FFN_CONV_W, nf), ddwb.reshape(-1)
            send["fin"] = mm_tn(f"ffn_in_wg_{i}", duf, L["hf"][None])
        else:
            dy, g_mix_post[i], dyb = resid_norm_bwd(f"mix_post_bwd_{i}", dx, L["y"], gain)
            if i >= N_A:
                dm = mm_nt(f"attn_out_bwd_{i}", dy, W["o"], 0, F32)
                send["o"] = mm_tn(f"attn_out_wg_{i}", L["mg"], dy).reshape(N_DEV, D // N_DEV, D)
                dq, dkv = attn_bwd(f"attn_bwd_{i}", L["q"], kv, dm, L["mg"], L["lse"], dkv, nb, seq)
                send["q"] = mm_tn(f"attn_q_wg_{i}", L["h"][None], dq)
                dx, g_mix_pre[i] = mm_nt_norm_bwd(f"attn_q_bwd_{i}", dq, W["q"], 0, L["x_in"], mix_pre_g[i], dx)
            else:
                g_cbout[i] = dyb
                ds = mm_nt(f"cm_out_bwd_{i}", dy, W["cout"], 0, F32)
                send["cout"] = mm_tn(f"cm_out_wg_{i}", L["s"], dy).reshape(N_DEV, D // N_DEV, D)
                ln_gain = LNg[i]
                if stage == stages[0]:
                    early = [send.pop("cout")]
                    lands = place_own("scatter_own_cout0", "scatter", early)
                    early_handle, early_token = exchange_begin("scatter_begin_cout0", "scatter", early, lands, ds)
                    ln_gain = ln_gain + early_token[0, 0]
                dc, g_lng[i], g_lnb[i] = ln_silu_bwd(f"cm_ln_bwd_{i}", ds, L["c"], ln_gain, LNb[i])
                du, g_cdw[i], g_cdwb[i], dbi = cm_glu_conv_bwd(f"cm_conv_bwd_{i}", L["u"], dc, DWc, i, seq)
                g_cbin[i] = dbi.reshape(N_DEV, -1)
                send["cin"] = mm_tn(f"cm_in_wg_{i}", L["h"][None], du)
                dx, g_mix_pre[i] = mm_nt_norm_bwd(f"cm_in_bwd_{i}", du, W["cin"], 0, L["x_in"], mix_pre_g[i], dx)
        if in_flight is not None:
            (p, j), names, handle = in_flight
            landed[j].update(zip(names, exchange_end(f"scatter_end_{p}{j}", handle, dx)))
        names, arrays = list(send), list(send.values())
        lands = place_own(f"scatter_own_{part}{i}", "scatter", arrays)
        handle, token = exchange_begin(f"scatter_begin_{part}{i}", "scatter", arrays, lands, dx)
        in_flight = (stage, names, handle)
    grad_x = dx.reshape(nb, seq, D)

    rep_names = ["mix_pre_g", "mix_post_g", "ffn_pre_g", "ffn_post_g", "kv_norm_g", "ffn_dw_b"]
    rep_parts = [jnp.concatenate(g_mix_pre), jnp.concatenate(g_mix_post), jnp.concatenate(g_ffn_pre),
                 jnp.concatenate(g_ffn_post), g_kvn.reshape(-1), jnp.stack(g_ffn_dwb)]
    rep_w = [mix_pre_g, mix_post_g, ffn_pre_g, ffn_post_g, kv_norm_g, ffn_dw_b]
    rep_m = [m_mix_pre_g, m_mix_post_g, m_ffn_pre_g, m_ffn_post_g, m_kv_norm_g, m_ffn_dw_b]
    rep_v = [v_mix_pre_g, v_mix_post_g, v_ffn_pre_g, v_ffn_post_g, v_kv_norm_g, v_ffn_dw_b]
    sh_names = ["ffn_dw", "cm_b_in", "cm_dw", "cm_dw_b", "cm_ln_g", "cm_ln_b", "cm_b_out"]
    own = lambda per_layer, shard: jnp.stack([p.reshape((N_DEV,) + shard) for p in per_layer], axis=1)
    sh_parts = [own(g_ffn_dw, ffn_dw.shape[1:]), own(g_cbin, cm_b_in.shape[1:]), own(g_cdw, cm_dw.shape[1:]),
                own(g_cdwb, cm_dw_b.shape[1:]), own(g_lng, cm_ln_g.shape[1:]), own(g_lnb, cm_ln_b.shape[1:]),
                own(g_cbout, cm_b_out.shape[1:])]
    sh_w = [ffn_dw, cm_b_in, cm_dw, cm_dw_b, cm_ln_g, cm_ln_b, cm_b_out]
    sh_m = [m_ffn_dw, m_cm_b_in, m_cm_dw, m_cm_dw_b, m_cm_ln_g, m_cm_ln_b, m_cm_b_out]
    sh_v = [v_ffn_dw, v_cm_b_in, v_cm_dw, v_cm_dw_b, v_cm_ln_g, v_cm_ln_b, v_cm_b_out]
    rep_pack = _pack([loss_part] + rep_parts)
    sh_pack = jnp.stack([_pack([p[k] for p in sh_parts]) for k in range(N_DEV)])
    n_rep = rep_pack.shape[0]
    small = []
    for kind, pack in (("gather", rep_pack), ("scatter", sh_pack)):
        lands = place_own(f"{kind}_own_small", kind, [pack])
        handle, token = exchange_begin(f"{kind}_begin_small", kind, [pack], lands, token)
        small.append((kind, handle))

    def big_update(name, w, m, v, key, layers, after):
        as3 = lambda t: t.reshape((-1,) + t.shape[-2:])
        outs = adamw_sum(name, as3(w), as3(m), as3(v), [landed[i][key] for i in layers], after)
        return [t.reshape(w.shape) for t in outs]

    conf, attn = range(N_A), range(N_A, DEPTH)
    upd = {}
    upd["ffn_w_in"] = [t_(t) for t in big_update("adam_ffn_w_in", fin_t, m_fin_t, v_fin_t, "fin", range(DEPTH), token)]
    upd["ffn_w_out"] = big_update("adam_ffn_w_out", ffn_w_out, m_ffn_w_out, v_ffn_w_out, "fout", range(DEPTH), token)
    upd["w_kv"] = big_update("adam_w_kv", w_kv, m_w_kv, v_w_kv, "kv", [N_A - 1], token)
    upd["w_q"] = big_update("adam_w_q", w_q, m_w_q, v_w_q, "q", attn, token)
    upd["w_o"] = big_update("adam_w_o", w_o, m_w_o, v_w_o, "o", attn, upd["w_q"][0])
    (p, j), names, handle = in_flight
    landed[j].update(zip(names, exchange_end(f"scatter_end_{p}{j}", handle, upd["w_o"][0])))
    landed[0]["cout"] = exchange_end("scatter_end_cout0", early_handle, upd["w_o"][0])[0]
    upd["cm_w_in"] = big_update("adam_cm_w_in", cm_w_in, m_cm_w_in, v_cm_w_in, "cin", conf, token)
    upd["cm_w_out"] = big_update("adam_cm_w_out", cm_w_out, m_cm_w_out, v_cm_w_out, "cout", conf, upd["cm_w_in"][0])
    (rep_landed,), (sh_landed,) = (exchange_end(f"{kind}_end_small", handle, upd["cm_w_out"][0])
                                   for kind, handle in small)
    rep_sum = sum_partials("sum_small_rep", rep_landed)
    sh_sum = sum_partials("sum_small_sh", sh_landed)
    rep_shapes = [(1, 1)] + [w.shape for w in rep_w]
    sh_shapes = [w.shape for w in sh_w]
    g_small = jnp.concatenate([rep_sum, sh_sum])
    pad1 = jnp.zeros((1, 1), F32)
    d_s, m_s, v_s = adamw_small("adam_small", jnp.concatenate([_pack([pad1] + rep_w), _pack(sh_w)]), g_small,
                                jnp.concatenate([_pack([pad1] + rep_m), _pack(sh_m)]),
                                jnp.concatenate([_pack([pad1] + rep_v), _pack(sh_v)]))
    split = lambda t: (_unpack(t[:n_rep], rep_shapes), _unpack(t[n_rep:], sh_shapes))
    for (rep_t, sh_t), slot in zip([split(g_small), split(d_s), split(m_s), split(v_s)], range(4)):
        if slot == 0:
            loss = rep_t[0].reshape(())
        for name, t in zip(rep_names, rep_t[1:]):
            upd.setdefault(name, [None] * 4)[slot] = t
        for name, t in zip(sh_names, sh_t):
            upd.setdefault(name, [None] * 4)[slot] = t

    order = ["mix_pre_g", "mix_post_g", "ffn_pre_g", "ffn_post_g", "cm_w_in", "cm_b_in", "cm_dw", "cm_dw_b", "cm_ln_g",
             "cm_ln_b", "cm_w_out", "cm_b_out", "kv_norm_g", "w_kv", "w_q", "w_o", "ffn_w_in", "ffn_dw", "ffn_dw_b",
             "ffn_w_out"]
    return (loss, grad_x, *[upd[n][0] for n in order], *[upd[n][1] for n in order],
            *[upd[n][2] for n in order], *[upd[n][3] for n in order])
```

```python
import functools

import jax
import jax.numpy as jnp
from jax import lax
from jax.experimental import pallas as pl
from jax.experimental.pallas import tpu as pltpu

N_DEV = 8
N_A = 2
DEPTH = 4
N_HEADS = 8
N_GROUPS = 3
DILATIONS = (1, 4, 16)
ATT_BLOCK = 128
ATT_BATCH = 8
CONV_W = 31
FFN_CONV_W = 3
CONV_HALO = 32
FFN_HALO = 16
ROW_CHUNK_FWD = 16
ROW_CHUNK_BWD = 16
GROUP_ALL_MAX_N = 384
FFN_COL_TILE = 1024
EPS = 1e-6
NEG = -1e30
ADAM_LR, ADAM_B1, ADAM_B2, ADAM_EPS, ADAM_WD, ADAM_STEP = 0.001, 0.9, 0.999, 1e-08, 0.01, 10
MM = jnp.bfloat16
F32 = jnp.float32
VMEM_LIMIT_BYTES = 56 * 1024 * 1024
PACK = 1024
MESH_ID = pl.DeviceIdType.MESH

_pallas = pl.pallas_call


def _call(body, *, name, out_shape, grid=(), in_specs=None, out_specs=None, scratch=()):
    return _pallas(body, name=name, out_shape=out_shape, grid=grid, in_specs=in_specs, out_specs=out_specs,
                   scratch_shapes=list(scratch),
                   compiler_params=pltpu.CompilerParams(vmem_limit_bytes=VMEM_LIMIT_BYTES))


def _tile(n, pref):
    if n <= pref:
        return n
    t = pref - pref % 8
    while n % t:
        t -= 8
    assert t > 0, (n, pref)
    return t


def _sds(shape, dtype):
    return jax.ShapeDtypeStruct(tuple(shape), dtype)


def _dot(a, b):
    return jnp.dot(a, b, preferred_element_type=F32)


def _dot_nt(a, b):
    return lax.dot_general(a, b, (((1,), (1,)), ((), ())), preferred_element_type=F32)


def _dot_tn(a, b):
    return lax.dot_general(a, b, (((0,), (0,)), ((), ())), preferred_element_type=F32)


def _sigmoid(x):
    return 0.5 * jnp.tanh(0.5 * x) + 0.5


def _my_index():
    return 4 * lax.axis_index("x") + 2 * lax.axis_index("y") + lax.axis_index("c")


def _exchange(name, arrays, out_shapes, pieces, src_of, dst_of):
    n = len(arrays)
    base = [sum(pieces[:a]) for a in range(n)]
    total = sum(pieces)

    def body(*refs):
        ins, outs = refs[:n], refs[n:2 * n]
        send_sems, recv_sems, local_sems = refs[2 * n:]
        x, y, c = lax.axis_index("x"), lax.axis_index("y"), lax.axis_index("c")
        me = 4 * x + 2 * y + c
        copies = []
        for a in range(n):
            for k, (s, d) in enumerate(zip(src_of(a, ins[a], me), dst_of(a, outs[a], me))):
                cp = pltpu.make_async_copy(s, d, local_sems.at[base[a] + k])
                cp.start()
                copies.append(cp)
        remote = []
        for m in range(1, N_DEV):
            px, py, pc = x ^ (m >> 2), y ^ ((m >> 1) & 1), c ^ (m & 1)
            peer = 4 * px + 2 * py + pc
            for a in range(n):
                for k, (s, d) in enumerate(zip(src_of(a, ins[a], peer), dst_of(a, outs[a], me))):
                    cp = pltpu.make_async_remote_copy(src_ref=s, dst_ref=d, send_sem=send_sems.at[base[a] + k, m - 1],
                                                      recv_sem=recv_sems.at[base[a] + k, m - 1],
                                                      device_id=(px, py, pc), device_id_type=MESH_ID)
                    cp.start()
                    remote.append(cp)
        for cp in copies:
            cp.wait()
        for cp in remote:
            cp.wait_send()
        for m in range(1, N_DEV):
            px, py, pc = x ^ (m >> 2), y ^ ((m >> 1) & 1), c ^ (m & 1)
            peer = 4 * px + 2 * py + pc
            for a in range(n):
                for k, (s, d) in enumerate(zip(src_of(a, ins[a], me), dst_of(a, outs[a], peer))):
                    pltpu.make_async_remote_copy(src_ref=s, dst_ref=d, send_sem=send_sems.at[base[a] + k, m - 1],
                                                 recv_sem=recv_sems.at[base[a] + k, m - 1], device_id=(px, py, pc),
                                                 device_id_type=MESH_ID).wait_recv()

    any_spec = pl.BlockSpec(memory_space=pl.ANY)
    return _call(body, name=name, out_shape=[_sds(s, a.dtype) for s, a in zip(out_shapes, arrays)],
                 in_specs=[any_spec] * n, out_specs=[any_spec] * n,
                 scratch=[pltpu.SemaphoreType.DMA((total, N_DEV - 1)), pltpu.SemaphoreType.DMA((total, N_DEV - 1)),
                          pltpu.SemaphoreType.DMA((total,))])(*arrays)


def all_gather(name, arrays, row_sharded):
    def out_shape(a):
        s = arrays[a].shape
        return (s[0], N_DEV) + s[1:] if row_sharded[a] else (N_DEV,) + s

    def src_of(a, ref, peer):
        if row_sharded[a]:
            return [ref.at[l] for l in range(arrays[a].shape[0])]
        return [ref]

    def dst_of(a, ref, me):
        if row_sharded[a]:
            return [ref.at[l, me] for l in range(arrays[a].shape[0])]
        return [ref.at[me]]

    pieces = [arrays[a].shape[0] if row_sharded[a] else 1 for a in range(len(arrays))]
    return _exchange(name, arrays, [out_shape(a) for a in range(len(arrays))], pieces, src_of, dst_of)


def _src_view(kind, ref, peer):
    return ref if kind == "gather" else ref.at[peer]


def _peers(x, y, c):
    for m in range(1, N_DEV):
        px, py, pc = x ^ (m >> 2), y ^ ((m >> 1) & 1), c ^ (m & 1)
        yield m - 1, (px, py, pc), 4 * px + 2 * py + pc


def place_own(name, kind, srcs):
    n = len(srcs)
    shapes = [(N_DEV,) + s.shape if kind == "gather" else s.shape for s in srcs]
    steps = 2 if all(s.shape[-2] % 32 == 0 for s in srcs) else 1

    def body(*refs):
        for a in range(n):
            refs[n + a][...] = refs[a][...]

    def spec(shape, own_block):
        R, C = shape[-2:]
        tr = R // steps
        if own_block:
            return pl.BlockSpec((None, tr, C), lambda i: (_my_index(), i, 0))
        return pl.BlockSpec((tr, C), lambda i: (i, 0))

    return _call(body, name=name, grid=(steps,), out_shape=[_sds(s, a.dtype) for s, a in zip(shapes, srcs)],
                 in_specs=[spec(s.shape, kind == "scatter") for s in srcs],
                 out_specs=[spec(s, True) for s in shapes])(*srcs)


_HBM_SPEC = pl.BlockSpec(memory_space=pltpu.HBM)
_SEM_SPEC = pl.BlockSpec(memory_space=pltpu.SEMAPHORE)
_DATAFLOW = pltpu.SideEffectType.DATAFLOW_SIDE_EFFECTING


def _remote(kind, src, land, send_sems, recv_sems, a, slot, frm, to_id, at):
    return pltpu.make_async_remote_copy(src_ref=_src_view(kind, src, frm), dst_ref=land.at[at],
                                        send_sem=send_sems.at[a * (N_DEV - 1) + slot],
                                        recv_sem=recv_sems.at[a * (N_DEV - 1) + slot],
                                        device_id=to_id, device_id_type=MESH_ID)


def exchange_begin(name, kind, srcs, lands, after):
    n = len(srcs)

    def body(*refs):
        ins, lnd = refs[:n], refs[n:2 * n]
        send_sems, recv_sems = refs[2 * n + 1], refs[2 * n + 2]
        token = refs[-1]
        x, y, c = lax.axis_index("x"), lax.axis_index("y"), lax.axis_index("c")
        me = 4 * x + 2 * y + c
        for slot, peer_id, peer in _peers(x, y, c):
            for a in range(n):
                _remote(kind, ins[a], lnd[a], send_sems, recv_sems, a, slot, peer, peer_id, me).start()
        token[...] = jnp.zeros_like(token)

    hbm = lambda t: pltpu.HBM(t.shape, t.dtype)
    outs = _pallas(
        body, name=name,
        out_shape=(pltpu.SemaphoreType.DMA((n * (N_DEV - 1),)), pltpu.SemaphoreType.DMA((n * (N_DEV - 1),)),
                   *[hbm(t) for t in srcs], *[hbm(t) for t in lands], _sds((8, 128), F32)),
        in_specs=[_HBM_SPEC] * (2 * n) + [pl.BlockSpec(memory_space=pl.ANY)],
        out_specs=(_SEM_SPEC, _SEM_SPEC, *[_HBM_SPEC] * (2 * n), pl.BlockSpec(memory_space=pltpu.VMEM)),
        input_output_aliases={i: 2 + i for i in range(2 * n)},
        compiler_params=pltpu.CompilerParams(has_side_effects=_DATAFLOW),
    )(*[pltpu.with_memory_space_constraint(t, pltpu.HBM) for t in list(srcs) + list(lands)], after)
    return (kind, outs[0], outs[1], list(outs[2:2 + n]), list(outs[2 + n:2 + 2 * n])), outs[-1]


def exchange_end(name, handle, after):
    kind, send_sems, recv_sems, srcs, lands = handle
    n = len(srcs)

    def body(*refs):
        ins, lnd = refs[:n], refs[n:2 * n]
        s_sems, r_sems = refs[2 * n], refs[2 * n + 1]
        x, y, c = lax.axis_index("x"), lax.axis_index("y"), lax.axis_index("c")
        me = 4 * x + 2 * y + c
        for slot, peer_id, peer in _peers(x, y, c):
            for a in range(n):
                _remote(kind, ins[a], lnd[a], s_sems, r_sems, a, slot, peer, peer_id, me).wait_send()
        for slot, peer_id, peer in _peers(x, y, c):
            for a in range(n):
                _remote(kind, ins[a], lnd[a], s_sems, r_sems, a, slot, me, peer_id, peer).wait_recv()

    hbm = lambda t: pltpu.HBM(t.shape, t.dtype)
    outs = _pallas(
        body, name=name, out_shape=tuple(hbm(t) for t in srcs + lands),
        in_specs=[_HBM_SPEC] * (2 * n) + [_SEM_SPEC, _SEM_SPEC, pl.BlockSpec(memory_space=pl.ANY)],
        out_specs=tuple([_HBM_SPEC] * (2 * n)), input_output_aliases={i: i for i in range(2 * n)},
        compiler_params=pltpu.CompilerParams(has_side_effects=_DATAFLOW),
    )(*srcs, *lands, send_sems, recv_sems, after)
    return list(outs[n:])


def norm_mm(name, x, gain, w, layer, bias=None):
    T, D = x.shape
    nsh, _, _, n = w.shape
    tm = _tile(T, 1024)
    grp = nsh if n <= GROUP_ALL_MAX_N else nsh // 2
    steps = nsh // grp

    def body(*refs):
        if bias is None:
            x_ref, g_ref, w_ref, u_ref, h_ref = refs
        else:
            x_ref, g_ref, w_ref, b_ref, u_ref, h_ref = refs

        @pl.when(pl.program_id(1) == 0)
        def _():
            xf = x_ref[...]
            r = lax.rsqrt(jnp.mean(xf * xf, axis=-1, keepdims=True) + EPS)
            h_ref[...] = (xf * r * g_ref[...]).astype(h_ref.dtype)

        h = h_ref[...]
        for k in range(grp):
            acc = _dot(h, w_ref[k])
            if bias is not None:
                acc = acc + b_ref[k]
            u_ref[k] = acc.astype(u_ref.dtype)

    in_specs = [pl.BlockSpec((tm, D), lambda i, j: (i, 0)),
                pl.BlockSpec((1, D), lambda i, j: (0, 0)),
                pl.BlockSpec((grp, None, D, n), lambda i, j: (j, layer, 0, 0))]
    args = [x, gain.reshape(1, D), w]
    if bias is not None:
        in_specs.append(pl.BlockSpec((grp, None, 1, n), lambda i, j: (j, layer, 0, 0)))
        args.append(bias)
    return _call(body, name=name, grid=(T // tm, steps), in_specs=in_specs,
                 out_specs=[pl.BlockSpec((grp, tm, n), lambda i, j: (j, i, 0)),
                            pl.BlockSpec((tm, D), lambda i, j: (i, 0))],
                 out_shape=[_sds((nsh, T, n), MM), _sds((T, D), MM)])(*args)


def mm_resid_norm(name, a, w, layer, bias, x, gain):
    nk, T, kk = a.shape
    D = x.shape[1]
    tm = _tile(T, 512)

    def body(a_ref, w_ref, b_ref, x_ref, g_ref, y_ref, xn_ref):
        y = _dot(a_ref[0], w_ref[0])
        for q in range(1, nk):
            y = y + _dot(a_ref[q], w_ref[q])
        y = y + b_ref[...]
        y_ref[...] = y
        r = lax.rsqrt(jnp.mean(y * y, axis=-1, keepdims=True) + EPS)
        xn_ref[...] = x_ref[...] + y * r * g_ref[...]

    return _call(body, name=name, grid=(T // tm,),
                 in_specs=[pl.BlockSpec((nk, tm, kk), lambda i: (0, i, 0)),
                           pl.BlockSpec((None, nk, kk, D), lambda i: (layer, 0, 0, 0)),
                           pl.BlockSpec((1, D), lambda i: (0, 0)),
                           pl.BlockSpec((tm, D), lambda i: (i, 0)),
                           pl.BlockSpec((1, D), lambda i: (0, 0))],
                 out_specs=[pl.BlockSpec((tm, D), lambda i: (i, 0))] * 2,
                 out_shape=[_sds((T, D), F32)] * 2)(a, w, bias.reshape(1, D), x, gain.reshape(1, D))


def _halo_maps(tm, hb, T):
    per = tm // hb
    last = T // hb - 1
    return (lambda i: jnp.maximum(i * per - 1, 0)), (lambda i: jnp.minimum((i + 1) * per, last))


def cm_glu_conv(name, u, dw, dwb, layer, seq):
    _, T, n = u.shape
    ct = dw.shape[-1]
    per = n // ct
    nct = 4 * per
    tm = _tile(seq, 512)
    tps = seq // tm
    hb = CONV_HALO
    prev, _ = _halo_maps(tm, hb, T)
    u4 = u.reshape(2, 4, T, n)

    rows = _tile(tm, 2 * ROW_CHUNK_FWD)

    def body(u_ref, uh_ref, w_ref, b_ref, o_ref, z_ref):
        first = (pl.program_id(0) % tps) == 0
        um = u_ref[...].astype(F32)
        uh = uh_ref[...].astype(F32)
        z_ref[0, pl.ds(hb, tm), :] = um[0] * _sigmoid(um[1])
        z_ref[0, pl.ds(0, hb), :] = jnp.where(first, 0.0, uh[0] * _sigmoid(uh[1]))
        _stage_shifts(z_ref, hb + tm, back=True)
        for r0 in range(0, tm, rows):
            acc = b_ref[...]
            for j in range(CONV_W):
                acc = acc + w_ref[pl.ds(CONV_W - 1 - j, 1), :] * z_ref[j % 8, pl.ds(hb + r0 - (j - j % 8), rows), :]
            o_ref[pl.ds(r0, rows), :] = acc

    return _call(body, name=name, grid=(T // tm, nct),
                 in_specs=[pl.BlockSpec((2, None, tm, ct), lambda i, c: (0, c // per, i, c % per)),
                           pl.BlockSpec((2, None, hb, ct), lambda i, c: (0, c // per, prev(i), c % per)),
                           pl.BlockSpec((None, None, CONV_W, ct), lambda i, c: (c, layer, 0, 0)),
                           pl.BlockSpec((None, None, 1, ct), lambda i, c: (c, layer, 0, 0))],
                 out_specs=pl.BlockSpec((tm, ct), lambda i, c: (i, c)),
                 out_shape=_sds((T, nct * ct), F32),
                 scratch=[pltpu.VMEM((8, tm + hb, ct), F32)])(u4, u4, dw, dwb)


def _stage_shifts(z_ref, n_rows, back):
    for r in range(1, 8):
        if back:
            z_ref[r, pl.ds(8, n_rows - 8), :] = z_ref[0, pl.ds(8 - r, n_rows - 8), :]
        else:
            z_ref[r, pl.ds(0, n_rows - 8), :] = z_ref[0, pl.ds(r, n_rows - 8), :]


def ln_silu(name, c, g, b):
    T, D = c.shape
    tm = _tile(T, 512)

    def body(c_ref, g_ref, b_ref, s_ref):
        cf = c_ref[...]
        mu = jnp.mean(cf, axis=-1, keepdims=True)
        xc = cf - mu
        r = lax.rsqrt(jnp.mean(xc * xc, axis=-1, keepdims=True) + EPS)
        t = xc * r * g_ref[...] + b_ref[...]
        s_ref[...] = (t * _sigmoid(t)).astype(s_ref.dtype)

    return _call(body, name=name, grid=(T // tm,),
                 in_specs=[pl.BlockSpec((tm, D), lambda i: (i, 0)), pl.BlockSpec((1, D), lambda i: (0, 0)),
                           pl.BlockSpec((1, D), lambda i: (0, 0))],
                 out_specs=pl.BlockSpec((None, tm, D), lambda i: (0, i, 0)),
                 out_shape=_sds((1, T, D), MM))(c, g.reshape(1, D), b.reshape(1, D))


def _col_tiles(n, width):
    return [(c0, min(width, n - c0)) for c0 in range(0, n, width)]


def ffn_in_conv(name, x, gain, wt, dw, dwb, layer, seq):
    T, D = x.shape
    n = wt.shape[2]
    tm = _tile(seq, 1024)
    tps = seq // tm
    hb = 8
    rows = _tile(tm, ROW_CHUNK_FWD)
    K = FFN_CONV_W
    w5 = wt.reshape(2, 4, n, D)
    dw4 = dw.reshape(2, 4, dw.shape[1], K, n)
    dwb4 = dwb.reshape(dwb.shape[0], 2, 4, 1, n)

    def body(x_ref, g_ref, w_ref, cw_ref, cb_ref, u_ref, h_ref, z_ref, ag_ref, pad_ref, carry_ref):
        i, j = pl.program_id(0), pl.program_id(1)
        first = (i % tps) == 0

        @pl.when(j == 0)
        def _():
            xf = x_ref[...]
            r = lax.rsqrt(jnp.mean(xf * xf, axis=-1, keepdims=True) + EPS)
            h_ref[...] = (xf * r * g_ref[...]).astype(h_ref.dtype)

        h = h_ref[...]
        for c0, wc in _col_tiles(n, FFN_COL_TILE):
            cols = pl.ds(c0, wc)
            for half in range(2):
                res = _dot_nt(h, w_ref[half, cols, :])
                u_ref[half, :, cols] = res.astype(u_ref.dtype)
                pad_ref[half, pl.ds(hb, tm), cols] = res
                pad_ref[half, pl.ds(0, hb), cols] = jnp.where(first, 0.0, carry_ref[j, half, :, cols])
            for r0 in range(0, tm, rows):
                conv = []
                for half in range(2):
                    acc = cb_ref[half, :, cols] + cw_ref[half, pl.ds(K - 1, 1), cols] * pad_ref[
                        half, pl.ds(hb + r0, rows), cols]
                    for k in range(K - 1):
                        acc = acc + cw_ref[half, pl.ds(k, 1), cols] * pad_ref[
                            half, pl.ds(hb + r0 - (K - 1) + k, rows), cols]
                    conv.append(acc)
                a, g = conv
                sg = _sigmoid(g)
                silu = g * sg
                z_ref[pl.ds(r0, rows), cols] = (silu * a).astype(z_ref.dtype)
                ag_ref[0, pl.ds(r0, rows), cols] = silu.astype(ag_ref.dtype)
                ag_ref[1, pl.ds(r0, rows), cols] = (a * (sg * (1.0 + g * (1.0 - sg)))).astype(ag_ref.dtype)
            for half in range(2):
                carry_ref[j, half, :, cols] = pad_ref[half, pl.ds(tm, hb), cols]

    u, h, z, ag = _call(
        body, name=name, grid=(T // tm, 4),
        in_specs=[pl.BlockSpec((tm, D), lambda i, j: (i, 0)),
                  pl.BlockSpec((1, D), lambda i, j: (0, 0)),
                  pl.BlockSpec((2, None, n, D), lambda i, j: (0, j, 0, 0)),
                  pl.BlockSpec((2, None, None, K, n), lambda i, j: (0, j, layer, 0, 0)),
                  pl.BlockSpec((None, 2, None, 1, n), lambda i, j: (layer, 0, j, 0, 0))],
        out_specs=[pl.BlockSpec((2, None, tm, n), lambda i, j: (0, j, i, 0)),
                   pl.BlockSpec((tm, D), lambda i, j: (i, 0)),
                   pl.BlockSpec((None, tm, n), lambda i, j: (j, i, 0)),
                   pl.BlockSpec((2, None, tm, n), lambda i, j: (0, j, i, 0))],
        out_shape=[_sds((2, 4, T, n), MM), _sds((T, D), MM), _sds((4, T, n), MM), _sds((2, 4, T, n), MM)],
        scratch=[pltpu.VMEM((2, tm + hb, n), F32), pltpu.VMEM((4, 2, hb, n), F32)])(
            x, gain.reshape(1, D), w5, dw4, dwb4)
    return u.reshape(8, T, n), h, z, ag


def _head_specs(seq, dh, q_heads, kv_heads):
    def spec(per, base):
        def imap(b, h, g):
            f = base + g * N_HEADS + h
            return (f // per, b, f % per)
        return pl.BlockSpec((None, seq, dh), imap)
    return spec(q_heads, 0), spec(kv_heads, 0), spec(kv_heads, N_GROUPS * N_HEADS)


def _rows(start, d, blocks=1):
    size = blocks * ATT_BLOCK
    return pl.ds(start, size, stride=d) if d > 1 else pl.ds(start, size)


def _att_pad():
    return max(ATT_BLOCK * d for d in DILATIONS[:-1])


def _band_mask(first, nblk):
    keys = ATT_BLOCK if nblk == 1 else 2 * ATT_BLOCK
    shape = (ATT_BATCH, ATT_BLOCK, keys)
    qi = lax.broadcasted_iota(jnp.int32, shape, 1)
    kj = lax.broadcasted_iota(jnp.int32, shape, 2)
    if nblk == 1:
        return kj <= qi
    n = (first + lax.broadcasted_iota(jnp.int32, shape, 0)) % nblk
    return (kj >= qi) & (kj <= qi + ATT_BLOCK) & ((n > 0) | (kj >= ATT_BLOCK))


def _block_rows(idx, d, nblk):
    r, n = idx // nblk, idx % nblk
    rq = _rows(r + d * ATT_BLOCK * n, d)
    if nblk == 1:
        return rq, _rows(_att_pad() + r + d * ATT_BLOCK * n, d)
    return rq, _rows(_att_pad() + r + d * ATT_BLOCK * (n - 1), d, blocks=2)


def _bdot(a, b, ca, cb):
    return lax.dot_general(a, b, (((ca,), (cb,)), ((0,), (0,))), preferred_element_type=F32)


def attn_fwd(name, q, kv, nb, seq):
    _, T, qn = q.shape
    dh = qn * N_DEV // (N_GROUPS * N_HEADS)
    scale = 1.0 / (dh ** 0.5)
    qs, ks, vs = _head_specs(seq, dh, qn // dh, kv.shape[2] // dh)

    def body(q_ref, k_ref, v_ref, m_ref, l_ref, qf, kf, vf, *branch):
        og, lg = branch[:N_GROUPS], branch[N_GROUPS:]
        pad = _att_pad()
        qf[...] = q_ref[...].astype(F32)
        for t_ref, s_ref in ((k_ref, kf), (v_ref, vf)):
            s_ref[pl.ds(0, pad), :] = jnp.zeros((pad, dh), F32)
            s_ref[pl.ds(pad, seq), :] = t_ref[...].astype(F32)
        for g in range(N_GROUPS):
            d = DILATIONS[g]
            nblk = seq // d // ATT_BLOCK

            def blocks(it, carry, d=d, nblk=nblk, g=g):
                first = it * ATT_BATCH
                rows = [_block_rows(first + b, d, nblk) for b in range(ATT_BATCH)]
                qb = jnp.stack([qf[rq, :] for rq, _ in rows]).astype(MM)
                kb = jnp.stack([kf[rk, :] for _, rk in rows]).astype(MM)
                vb = jnp.stack([vf[rk, :] for _, rk in rows]).astype(MM)
                s = jnp.where(_band_mask(first, nblk), _bdot(qb, kb, 2, 2) * scale, NEG)
                m = jnp.max(s, axis=-1, keepdims=True)
                p = jnp.exp(s - m)
                den = jnp.sum(p, axis=-1, keepdims=True)
                o = _bdot(p.astype(MM), vb, 2, 1) / den
                lse = m + jnp.log(den)
                for b, (rq, _) in enumerate(rows):
                    og[g][rq, :] = o[b]
                    lg[g][rq, :] = jnp.broadcast_to(lse[b], (ATT_BLOCK, dh))
                return carry

            @pl.when(pl.program_id(2) == g)
            def _(blocks=blocks, d=d, nblk=nblk):
                lax.fori_loop(0, d * nblk // ATT_BATCH, blocks, 0)

        @pl.when(pl.program_id(2) == N_GROUPS - 1)
        def _():
            mx = jnp.maximum(jnp.maximum(lg[0][...], lg[1][...]), lg[2][...])
            e = [jnp.exp(lg[g][...] - mx) for g in range(N_GROUPS)]
            tot = e[0] + e[1] + e[2]
            m_ref[...] = ((e[0] * og[0][...] + e[1] * og[1][...] + e[2] * og[2][...]) / tot).astype(m_ref.dtype)
            l_ref[...] = mx + jnp.log(tot)

    return _call(body, name=name, grid=(nb, N_HEADS, N_GROUPS), in_specs=[qs, ks, vs],
                 out_specs=[pl.BlockSpec((None, seq, dh), lambda b, h, g: (0, b, h)),
                            pl.BlockSpec((seq, dh), lambda b, h, g: (b, h))],
                 out_shape=[_sds((1, T, N_HEADS * dh), MM), _sds((T, N_HEADS * dh), F32)],
                 scratch=[pltpu.VMEM((seq, dh), F32)] + [pltpu.VMEM((_att_pad() + seq, dh), F32)] * 2
                 + [pltpu.VMEM((seq, dh), F32)] * (2 * N_GROUPS))(q, kv, kv)


def loss_fwd_bwd(name, x, target):
    T, D = x.shape
    tm = _tile(T, 512)

    def body(x_ref, t_ref, dx_ref, l_ref):
        @pl.when(pl.program_id(0) == 0)
        def _():
            l_ref[...] = jnp.zeros_like(l_ref)
        err = x_ref[...] - t_ref[...]
        dx_ref[...] = err * (1.0 / D)
        l_ref[...] += 0.5 * jnp.sum(jnp.mean(err * err, axis=-1, keepdims=True), axis=0, keepdims=True)

    dx, l = _call(body, name=name, grid=(T // tm,),
                  in_specs=[pl.BlockSpec((tm, D), lambda i: (i, 0))] * 2,
                  out_specs=[pl.BlockSpec((tm, D), lambda i: (i, 0)), pl.BlockSpec((1, 1), lambda i: (0, 0))],
                  out_shape=[_sds((T, D), F32), _sds((1, 1), F32)])(x, target)
    return dx, l


def resid_norm_bwd(name, dx, y, gain):
    T, D = y.shape
    tm = _tile(T, 1024)

    def body(dx_ref, y_ref, g_ref, dy_ref, dg_ref, db_ref):
        @pl.when(pl.program_id(0) == 0)
        def _():
            dg_ref[...] = jnp.zeros_like(dg_ref)
            db_ref[...] = jnp.zeros_like(db_ref)
        y = y_ref[...]
        d = dx_ref[...]
        r = lax.rsqrt(jnp.mean(y * y, axis=-1, keepdims=True) + EPS)
        yh = y * r
        dyh = d * g_ref[...]
        dy = r * (dyh - yh * jnp.mean(dyh * yh, axis=-1, keepdims=True))
        dy_ref[...] = dy.astype(dy_ref.dtype)
        dg_ref[...] += jnp.sum(d * yh, axis=0, keepdims=True)
        db_ref[...] += jnp.sum(dy, axis=0, keepdims=True)

    return _call(body, name=name, grid=(T // tm,),
                 in_specs=[pl.BlockSpec((tm, D), lambda i: (i, 0))] * 2 + [pl.BlockSpec((1, D), lambda i: (0, 0))],
                 out_specs=[pl.BlockSpec((None, tm, D), lambda i: (0, i, 0))] + [pl.BlockSpec((1, D), lambda i: (0, 0))] * 2,
                 out_shape=[_sds((1, T, D), MM), _sds((1, D), F32), _sds((1, D), F32)])(dx, y, gain.reshape(1, D))


def mm_nt(name, dy, w, layer, out_dtype):
    _, T, D = dy.shape
    _, nk, kk, _ = w.shape
    tm = _tile(T, 1024)

    def body(dy_ref, w_ref, o_ref):
        o_ref[...] = _dot_nt(dy_ref[...], w_ref[...]).astype(o_ref.dtype)

    return _call(body, name=name, grid=(T // tm, nk),
                 in_specs=[pl.BlockSpec((None, tm, D), lambda i, q: (0, i, 0)),
                           pl.BlockSpec((None, None, kk, D), lambda i, q: (layer, q, 0, 0))],
                 out_specs=pl.BlockSpec((None, tm, kk), lambda i, q: (q, i, 0)),
                 out_shape=_sds((nk, T, kk), out_dtype))(dy, w)


def mm_nt_norm_bwd(name, du, w, layer, x_in, gain, dx_res):
    nsh, T, n = du.shape
    D = x_in.shape[1]
    tm = _tile(T, 512)
    grp = nsh if n <= GROUP_ALL_MAX_N else nsh // 2
    steps = nsh // grp

    def body(du_ref, w_ref, x_ref, g_ref, dr_ref, dx_ref, dg_ref, acc_ref):
        i, j = pl.program_id(0), pl.program_id(1)

        @pl.when((i == 0) & (j == 0))
        def _():
            dg_ref[...] = jnp.zeros_like(dg_ref)

        part = _dot_nt(du_ref[0], w_ref[0])
        for k in range(1, grp):
            part = part + _dot_nt(du_ref[k], w_ref[k])

        @pl.when(j == 0)
        def _():
            acc_ref[...] = part

        @pl.when(j > 0)
        def _():
            acc_ref[...] += part

        @pl.when(j == steps - 1)
        def _():
            x = x_ref[...]
            dh = acc_ref[...]
            r = lax.rsqrt(jnp.mean(x * x, axis=-1, keepdims=True) + EPS)
            xh = x * r
            dxh = dh * g_ref[...]
            dx_ref[...] = dr_ref[...] + r * (dxh - xh * jnp.mean(dxh * xh, axis=-1, keepdims=True))
            dg_ref[...] += jnp.sum(dh * xh, axis=0, keepdims=True)

    return _call(body, name=name, grid=(T // tm, steps),
                 in_specs=[pl.BlockSpec((grp, tm, n), lambda i, j: (j, i, 0)),
                           pl.BlockSpec((grp, None, D, n), lambda i, j: (j, layer, 0, 0)),
                           pl.BlockSpec((tm, D), lambda i, j: (i, 0)),
                           pl.BlockSpec((1, D), lambda i, j: (0, 0)),
                           pl.BlockSpec((tm, D), lambda i, j: (i, 0))],
                 out_specs=[pl.BlockSpec((tm, D), lambda i, j: (i, 0)), pl.BlockSpec((1, D), lambda i, j: (0, 0))],
                 out_shape=[_sds((T, D), F32), _sds((1, D), F32)],
                 scratch=[pltpu.VMEM((tm, D), F32)])(du, w, x_in, gain.reshape(1, D), dx_res)


def mm_tn(name, a, b):
    na, T, ka = a.shape
    nb, _, kb = b.shape
    nj = max(na, nb)

    def body(a_ref, b_ref, o_ref):
        o_ref[...] = _dot_tn(a_ref[...], b_ref[...]).astype(o_ref.dtype)

    return _call(body, name=name, grid=(nj,),
                 in_specs=[pl.BlockSpec((None, T, ka), (lambda j: (j, 0, 0)) if na > 1 else (lambda j: (0, 0, 0))),
                           pl.BlockSpec((None, T, kb), (lambda j: (j, 0, 0)) if nb > 1 else (lambda j: (0, 0, 0)))],
                 out_specs=pl.BlockSpec((None, ka, kb), lambda j: (j, 0, 0)),
                 out_shape=_sds((nj, ka, kb), MM))(a, b)


def ffn_bwd(name, dy, wout, u, ag, dw, wt, x_in, gain, dx_res, layer, seq):
    _, T, D = dy.shape
    n = wt.shape[2]
    tm = _tile(seq, 512)
    tps = seq // tm
    hb = FFN_HALO
    _, nxt = _halo_maps(tm, hb, T)
    K = FFN_CONV_W
    te = tm + hb
    rows = _tile(hb, ROW_CHUNK_BWD)
    u4, w5 = u.reshape(2, 4, T, n), wt.reshape(2, 4, n, D)
    dw4 = dw.reshape(2, 4, dw.shape[1], K, n)
    tiles = _col_tiles(n, FFN_COL_TILE)

    def body(dy_ref, dyn_ref, wo_ref, u_ref, ag_ref, agn_ref, cw_ref, wt_ref, x_ref, g_ref, dr_ref,
             du_ref, ddw_ref, ddb_ref, dx_ref, dg_ref,
             dzf_ref, da_ref, acc_ref):
        i, j = pl.program_id(0), pl.program_id(1)
        last = (i % tps) == tps - 1

        @pl.when((i == 0) & (j == 0))
        def _():
            dg_ref[...] = jnp.zeros_like(dg_ref)

        @pl.when(i == 0)
        def _():
            ddw_ref[j] = jnp.zeros((2, K, n), F32)
            ddb_ref[j] = jnp.zeros((2, 1, n), F32)

        dyt, dyn = dy_ref[...], dyn_ref[...]
        for ci, (c0, wc) in enumerate(tiles):
            cols = pl.ds(c0, wc)
            dzf_ref[pl.ds(0, tm), cols] = _dot_nt(dyt, wo_ref[cols, :])
            dzf_ref[pl.ds(tm, hb), cols] = jnp.where(last, 0.0, _dot_nt(dyn, wo_ref[cols, :]))
            for r0 in range(0, te, rows):
                dzc = dzf_ref[pl.ds(r0, rows), cols]
                for half in range(2):
                    src = ag_ref[half, pl.ds(r0, rows), cols] if r0 < tm else agn_ref[half, pl.ds(r0 - tm, rows), cols]
                    da_ref[half, 0, pl.ds(r0, rows), cols] = dzc * src.astype(F32)
            for half in range(2):
                for s in range(1, K):
                    da_ref[half, s, pl.ds(0, tm), cols] = da_ref[half, 0, pl.ds(s, tm), cols]
                tap_acc = [jnp.zeros((rows, wc), F32) for _ in range(K)]
                bias_acc = jnp.zeros((rows, wc), F32)
                for r0 in range(0, tm, rows):
                    xr = u_ref[half, pl.ds(r0, rows), cols].astype(F32)
                    acc = None
                    for k in range(K):
                        d = da_ref[half, K - 1 - k, pl.ds(r0, rows), cols]
                        term = cw_ref[half, pl.ds(k, 1), cols] * d
                        acc = term if acc is None else acc + term
                        tap_acc[k] = tap_acc[k] + d * xr
                        if k == K - 1:
                            bias_acc = bias_acc + d
                    du_ref[half, pl.ds(r0, rows), cols] = acc.astype(du_ref.dtype)
                for k in range(K):
                    ddw_ref[j, half, pl.ds(k, 1), cols] += jnp.sum(tap_acc[k], axis=0, keepdims=True)
                ddb_ref[j, half, :, cols] += jnp.sum(bias_acc, axis=0, keepdims=True)
            part = _dot(du_ref[0, :, cols], wt_ref[0, cols, :]) + _dot(du_ref[1, :, cols], wt_ref[1, cols, :])
            if ci == 0:
                acc_ref[...] = part + jnp.where(j == 0, 0.0, acc_ref[...])
            else:
                acc_ref[...] += part
        @pl.when(j == 3)
        def _():
            x = x_ref[...]
            dh = acc_ref[...]
            r = lax.rsqrt(jnp.mean(x * x, axis=-1, keepdims=True) + EPS)
            xh = x * r
            dxh = dh * g_ref[...]
            dx_ref[...] = dr_ref[...] + r * (dxh - xh * jnp.mean(dxh * xh, axis=-1, keepdims=True))
            dg_ref[...] += jnp.sum(dh * xh, axis=0, keepdims=True)

    f32 = lambda *shape: pltpu.VMEM(shape, F32)
    du, ddw, ddb, dx, dg = _call(
        body, name=name, grid=(T // tm, 4),
        in_specs=[pl.BlockSpec((None, tm, D), lambda i, j: (0, i, 0)),
                  pl.BlockSpec((None, hb, D), lambda i, j: (0, nxt(i), 0)),
                  pl.BlockSpec((None, None, n, D), lambda i, j: (0, j, 0, 0)),
                  pl.BlockSpec((2, None, tm, n), lambda i, j: (0, j, i, 0)),
                  pl.BlockSpec((2, None, tm, n), lambda i, j: (0, j, i, 0)),
                  pl.BlockSpec((2, None, hb, n), lambda i, j: (0, j, nxt(i), 0)),
                  pl.BlockSpec((2, None, None, K, n), lambda i, j: (0, j, layer, 0, 0)),
                  pl.BlockSpec((2, None, n, D), lambda i, j: (0, j, 0, 0)),
                  pl.BlockSpec((tm, D), lambda i, j: (i, 0)),
                  pl.BlockSpec((1, D), lambda i, j: (0, 0)),
                  pl.BlockSpec((tm, D), lambda i, j: (i, 0))],
        out_specs=[pl.BlockSpec((2, None, tm, n), lambda i, j: (0, j, i, 0)),
                   pl.BlockSpec((4, 2, K, n), lambda i, j: (0, 0, 0, 0)),
                   pl.BlockSpec((4, 2, 1, n), lambda i, j: (0, 0, 0, 0)),
                   pl.BlockSpec((tm, D), lambda i, j: (i, 0)),
                   pl.BlockSpec((1, D), lambda i, j: (0, 0))],
        out_shape=[_sds((2, 4, T, n), MM), _sds((4, 2, K, n), F32), _sds((4, 2, 1, n), F32), _sds((T, D), F32),
                   _sds((1, D), F32)],
        scratch=[f32(te, n), f32(2, K, te, n), f32(tm, D)],
    )(dy, dy, wout, u4, ag, ag, dw4, w5, x_in, gain.reshape(1, D), dx_res)
    return du.reshape(8, T, n), jnp.swapaxes(ddw, 0, 1), jnp.swapaxes(ddb, 0, 1), dx, dg


def ln_silu_bwd(name, ds, c, g, b):
    T, D = c.shape
    tm = _tile(T, 512)

    def body(ds_ref, c_ref, g_ref, b_ref, dc_ref, dg_ref, db_ref):
        @pl.when(pl.program_id(0) == 0)
        def _():
            dg_ref[...] = jnp.zeros_like(dg_ref)
            db_ref[...] = jnp.zeros_like(db_ref)
        cf = c_ref[...]
        mu = jnp.mean(cf, axis=-1, keepdims=True)
        xc = cf - mu
        r = lax.rsqrt(jnp.mean(xc * xc, axis=-1, keepdims=True) + EPS)
        xh = xc * r
        t = xh * g_ref[...] + b_ref[...]
        sg = _sigmoid(t)
        dt = ds_ref[...] * (sg * (1.0 + t * (1.0 - sg)))
        dg_ref[...] += jnp.sum(dt * xh, axis=0, keepdims=True)
        db_ref[...] += jnp.sum(dt, axis=0, keepdims=True)
        dxh = dt * g_ref[...]
        dc_ref[...] = r * (dxh - jnp.mean(dxh, axis=-1, keepdims=True)
                           - xh * jnp.mean(dxh * xh, axis=-1, keepdims=True))

    vec = pl.BlockSpec((1, D), lambda i: (0, 0))
    return _call(body, name=name, grid=(T // tm,),
                 in_specs=[pl.BlockSpec((None, tm, D), lambda i: (0, i, 0)), pl.BlockSpec((tm, D), lambda i: (i, 0)),
                           vec, vec],
                 out_specs=[pl.BlockSpec((tm, D), lambda i: (i, 0)), vec, vec],
                 out_shape=[_sds((T, D), F32), _sds((1, D), F32), _sds((1, D), F32)])(
                     ds, c, g.reshape(1, D), b.reshape(1, D))


def cm_glu_conv_bwd(name, u, dc, dw, layer, seq):
    _, T, n = u.shape
    ct = dw.shape[-1]
    per = n // ct
    nct = 4 * per
    tm = _tile(seq, 512)
    tps = seq // tm
    hb = CONV_HALO
    prev, nxt = _halo_maps(tm, hb, T)
    u4 = u.reshape(2, 4, T, n)
    K = CONV_W

    rows = _tile(tm, 2 * ROW_CHUNK_FWD)
    wrows = 8

    def body(u_ref, up_ref, dc_ref, dcn_ref, w_ref, du_ref, ddw_ref, ddb_ref, dbi_ref, zp_ref, zd_ref):
        i = pl.program_id(1)
        first = (i % tps) == 0
        last = (i % tps) == tps - 1

        @pl.when(i == 0)
        def _():
            ddw_ref[...] = jnp.zeros_like(ddw_ref)
            ddb_ref[...] = jnp.zeros_like(ddb_ref)
            dbi_ref[...] = jnp.zeros_like(dbi_ref)

        uh = up_ref[...].astype(F32)
        zp_ref[0, pl.ds(hb, tm), :] = u_ref[0].astype(F32) * _sigmoid(u_ref[1].astype(F32))
        zp_ref[0, pl.ds(0, hb), :] = jnp.where(first, 0.0, uh[0] * _sigmoid(uh[1]))
        zd_ref[0, pl.ds(0, tm), :] = dc_ref[...]
        zd_ref[0, pl.ds(tm, hb), :] = jnp.where(last, 0.0, dcn_ref[...])
        _stage_shifts(zp_ref, hb + tm, back=True)
        _stage_shifts(zd_ref, tm + hb, back=False)
        for r0 in range(0, tm, rows):
            dp = None
            for j in range(K):
                term = w_ref[pl.ds(K - 1 - j, 1), :] * zd_ref[j % 8, pl.ds(r0 + (j - j % 8), rows), :]
                dp = term if dp is None else dp + term
            v = u_ref[0, pl.ds(r0, rows), :].astype(F32)
            sg = _sigmoid(u_ref[1, pl.ds(r0, rows), :].astype(F32))
            dv = dp * sg
            dg = dp * v * sg * (1.0 - sg)
            du_ref[0, pl.ds(r0, rows), :] = dv.astype(du_ref.dtype)
            du_ref[1, pl.ds(r0, rows), :] = dg.astype(du_ref.dtype)
            dbi_ref[0] += jnp.sum(dv, axis=0, keepdims=True)
            dbi_ref[1] += jnp.sum(dg, axis=0, keepdims=True)
        tap_acc = [jnp.zeros((wrows, ct), F32) for _ in range(K)]
        bias_acc = jnp.zeros((wrows, ct), F32)
        for r0 in range(0, tm, wrows):
            d = zd_ref[0, pl.ds(r0, wrows), :]
            for j in range(K):
                tap_acc[j] = tap_acc[j] + d * zp_ref[j % 8, pl.ds(hb + r0 - (j - j % 8), wrows), :]
            bias_acc = bias_acc + d
        for j in range(K):
            ddw_ref[pl.ds(K - 1 - j, 1), :] += jnp.sum(tap_acc[j], axis=0, keepdims=True)
        ddb_ref[...] += jnp.sum(bias_acc, axis=0, keepdims=True)

    du, ddw, ddb, dbi = _call(
        body, name=name, grid=(nct, T // tm),
        in_specs=[pl.BlockSpec((2, None, tm, ct), lambda c, i: (0, c // per, i, c % per)),
                  pl.BlockSpec((2, None, hb, ct), lambda c, i: (0, c // per, prev(i), c % per)),
                  pl.BlockSpec((tm, ct), lambda c, i: (i, c)),
                  pl.BlockSpec((hb, ct), lambda c, i: (nxt(i), c)),
                  pl.BlockSpec((None, None, K, ct), lambda c, i: (c, layer, 0, 0))],
        out_specs=[pl.BlockSpec((2, None, tm, ct), lambda c, i: (0, c // per, i, c % per)),
                   pl.BlockSpec((None, K, ct), lambda c, i: (c, 0, 0)),
                   pl.BlockSpec((None, 1, ct), lambda c, i: (c, 0, 0)),
                   pl.BlockSpec((2, None, 1, ct), lambda c, i: (0, c // per, 0, c % per))],
        out_shape=[_sds((2, 4, T, n), MM), _sds((nct, K, ct), F32), _sds((nct, 1, ct), F32), _sds((2, 4, 1, n), F32)],
        scratch=[pltpu.VMEM((8, tm + hb, ct), F32), pltpu.VMEM((8, tm + hb, ct), F32)])(u4, u4, dc, dc, dw)
    return du.reshape(8, T, n), ddw, ddb, dbi


def attn_bwd(name, q, kv, dm, merged, lse, dkv_prev, nb, seq):
    _, T, qn = q.shape
    kn = kv.shape[2]
    dh = qn * N_DEV // (N_GROUPS * N_HEADS)
    scale = 1.0 / (dh ** 0.5)
    qs, ks, vs = _head_specs(seq, dh, qn // dh, kn // dh)
    has_prev = dkv_prev is not None
    n_in = 6 + (1 if has_prev else 0)

    def body(*refs):
        q_ref, k_ref, v_ref, dm_ref, mg_ref, l_ref = refs[:6]
        pkv_ref = refs[6] if has_prev else None
        dq_ref, dkv_ref = refs[n_in:n_in + 2]
        qf, kf, vf, dqf, dkf, dvf, dlt = refs[n_in + 2:]
        pad = _att_pad()

        @pl.when(pl.program_id(2) == 0)
        def _():
            dlt[...] = jnp.broadcast_to(
                jnp.sum(dm_ref[...] * mg_ref[...].astype(F32), axis=-1, keepdims=True), (seq, dh))

        qf[...] = q_ref[...].astype(F32)
        for t_ref, s_ref in ((k_ref, kf), (v_ref, vf)):
            s_ref[pl.ds(0, pad), :] = jnp.zeros((pad, dh), F32)
            s_ref[pl.ds(pad, seq), :] = t_ref[...].astype(F32)
        dkf[...] = jnp.zeros_like(dkf)
        dvf[...] = jnp.zeros_like(dvf)
        for g in range(N_GROUPS):
            d = DILATIONS[g]
            nblk = seq // d // ATT_BLOCK

            def blocks(it, carry, d=d, nblk=nblk):
                first = it * ATT_BATCH
                rows = [_block_rows(first + b, d, nblk) for b in range(ATT_BATCH)]
                qb = jnp.stack([qf[rq, :] for rq, _ in rows]).astype(MM)
                dmb = jnp.stack([dm_ref[rq, :] for rq, _ in rows]).astype(MM)
                lse = jnp.stack([l_ref[rq, :][:, :1] for rq, _ in rows])
                delta = jnp.stack([dlt[rq, :][:, :1] for rq, _ in rows])
                kb = jnp.stack([kf[rk, :] for _, rk in rows]).astype(MM)
                vb = jnp.stack([vf[rk, :] for _, rk in rows]).astype(MM)
                s = jnp.where(_band_mask(first, nblk), _bdot(qb, kb, 2, 2) * scale, NEG)
                p = jnp.exp(s - lse)
                dsc = (p * (_bdot(dmb, vb, 2, 2) - delta) * scale).astype(MM)
                dv = _bdot(p.astype(MM), dmb, 1, 1)
                dk = _bdot(dsc, qb, 1, 1)
                dq = _bdot(dsc, kb, 2, 1)
                for b, (rq, rk) in enumerate(rows):
                    dqf[rq, :] = dq[b]
                    dkf[rk, :] += dk[b]
                    dvf[rk, :] += dv[b]
                return carry

            @pl.when(pl.program_id(2) == g)
            def _(blocks=blocks, d=d, nblk=nblk):
                lax.fori_loop(0, d * nblk // ATT_BATCH, blocks, 0)

        dq_ref[...] = dqf[...].astype(dq_ref.dtype)
        dk, dv = dkf[pl.ds(pad, seq), :], dvf[pl.ds(pad, seq), :]
        if has_prev:
            dk, dv = dk + pkv_ref[0].astype(F32), dv + pkv_ref[1].astype(F32)
        dkv_ref[0] = dk.astype(dkv_ref.dtype)
        dkv_ref[1] = dv.astype(dkv_ref.dtype)

    per = kn // dh

    def both(b, h, g):
        f = g * N_HEADS + h
        return (0, f // per, b, f % per)

    kv_spec = pl.BlockSpec((2, None, seq, dh), both)
    full = pl.BlockSpec((None, seq, dh), lambda b, h, g: (0, b, h))
    in_specs = [qs, ks, vs, full, full, pl.BlockSpec((seq, dh), lambda b, h, g: (b, h))]
    args = [q, kv, kv, dm, merged, lse]
    if has_prev:
        in_specs.append(kv_spec)
        args.append(dkv_prev.reshape(2, N_DEV // 2, T, kn))
    short, padded = pltpu.VMEM((seq, dh), F32), pltpu.VMEM((_att_pad() + seq, dh), F32)
    dq, dkv = _call(body, name=name, grid=(nb, N_HEADS, N_GROUPS), in_specs=in_specs, out_specs=[qs, kv_spec],
                    out_shape=[_sds(q.shape, MM), _sds((2, N_DEV // 2, T, kn), MM)],
                    scratch=[short, padded, padded, short, padded, padded, short])(*args)
    return dq, dkv.reshape(N_DEV, T, kn)


def _adamw_math(w, g, m, v):
    m = ADAM_B1 * m + (1.0 - ADAM_B1) * g
    v = ADAM_B2 * v + (1.0 - ADAM_B2) * (g * g)
    m_hat = m / (1.0 - ADAM_B1 ** ADAM_STEP)
    v_hat = v / (1.0 - ADAM_B2 ** ADAM_STEP)
    delta = -ADAM_LR * (m_hat / (jnp.sqrt(v_hat) + ADAM_EPS) + ADAM_WD * w)
    return delta, m, v


def adamw_sum(name, w, m, v, parts, after):
    L, R, C = w.shape
    tr = _tile(R, 256)

    def body(*refs):
        w_ref, m_ref, v_ref = refs[:3]
        p_refs = refs[3:3 + L]
        g_ref, d_ref, nm_ref, nv_ref = refs[4 + L:]
        for l in range(L):
            @pl.when(pl.program_id(0) == l)
            def _(p_ref=p_refs[l]):
                g = p_ref[0].astype(F32)
                for k in range(1, N_DEV):
                    g = g + p_ref[k].astype(F32)
                g_ref[...] = g
                d_ref[...], nm_ref[...], nv_ref[...] = _adamw_math(w_ref[...], g, m_ref[...], v_ref[...])

    blk = pl.BlockSpec((None, tr, C), lambda l, i: (l, i, 0))
    part = lambda k: pl.BlockSpec((N_DEV, tr, C), lambda l, i: (0, jnp.where(l == k, i, 0), 0))
    return _call(body, name=name, grid=(L, R // tr),
                 in_specs=[blk, blk, blk] + [part(k) for k in range(L)] + [pl.BlockSpec(memory_space=pl.ANY)],
                 out_specs=[blk] * 4, out_shape=[_sds((L, R, C), F32)] * 4)(w, m, v, *parts, after)


def sum_partials(name, parts):
    _, R, C = parts.shape
    tr = _tile(R, 512)

    def body(p_ref, o_ref):
        g = p_ref[0]
        for k in range(1, N_DEV):
            g = g + p_ref[k]
        o_ref[...] = g

    return _call(body, name=name, grid=(R // tr,),
                 in_specs=[pl.BlockSpec((N_DEV, tr, C), lambda i: (0, i, 0))],
                 out_specs=pl.BlockSpec((tr, C), lambda i: (i, 0)), out_shape=_sds((R, C), F32))(parts)


def adamw_small(name, w, g, m, v):
    R, C = w.shape
    tr = _tile(R, 512)

    def body(w_ref, g_ref, m_ref, v_ref, d_ref, nm_ref, nv_ref):
        d_ref[...], nm_ref[...], nv_ref[...] = _adamw_math(w_ref[...], g_ref[...], m_ref[...], v_ref[...])

    blk = pl.BlockSpec((tr, C), lambda i: (i, 0))
    return _call(body, name=name, grid=(R // tr,), in_specs=[blk] * 4, out_specs=[blk] * 3,
                 out_shape=[_sds((R, C), F32)] * 3)(w, g, m, v)


def _pack(arrays):
    pieces = []
    for a in arrays:
        f = a.reshape(-1).astype(F32)
        pieces.append(jnp.pad(f, (0, (-f.shape[0]) % PACK)))
    return jnp.concatenate(pieces).reshape(-1, 128)


def _unpack(flat, shapes):
    out, off = [], 0
    f = flat.reshape(-1)
    for s in shapes:
        size = 1
        for d in s:
            size *= d
        out.append(f[off:off + size].reshape(s))
        off += size + (-size) % PACK
    return out


def kernel(x, mix_pre_g, mix_post_g, ffn_pre_g, ffn_post_g, cm_w_in, cm_b_in, cm_dw, cm_dw_b, cm_ln_g, cm_ln_b, cm_w_out, cm_b_out, kv_norm_g, w_kv, w_q, w_o, ffn_w_in, ffn_dw, ffn_dw_b, ffn_w_out, loss_target, m_mix_pre_g, m_mix_post_g, m_ffn_pre_g, m_ffn_post_g, m_cm_w_in, m_cm_b_in, m_cm_dw, m_cm_dw_b, m_cm_ln_g, m_cm_ln_b, m_cm_w_out, m_cm_b_out, m_kv_norm_g, m_w_kv, m_w_q, m_w_o, m_ffn_w_in, m_ffn_dw, m_ffn_dw_b, m_ffn_w_out, v_mix_pre_g, v_mix_post_g, v_ffn_pre_g, v_ffn_post_g, v_cm_w_in, v_cm_b_in, v_cm_dw, v_cm_dw_b, v_cm_ln_g, v_cm_ln_b, v_cm_w_out, v_cm_b_out, v_kv_norm_g, v_w_kv, v_w_q, v_w_o, v_ffn_w_in, v_ffn_dw, v_ffn_dw_b, v_ffn_w_out):
    nb, seq, D = x.shape
    T = nb * seq
    me = _my_index()
    n_b = DEPTH - N_A

    nf = ffn_w_in.shape[-1]
    t_ = lambda t: jnp.swapaxes(t, 1, 2)
    fin_t, m_fin_t, v_fin_t = t_(ffn_w_in), t_(m_ffn_w_in), t_(v_ffn_w_in)

    stages = [(part, i) for i in range(DEPTH) for part in (("cin", "cout") if i < N_A else ("mix",)) + ("ffn",)]
    stages.insert(stages.index(("ffn", N_A - 1)) + 1, ("kv", N_A - 1))

    def stage_sources(stage):
        part, i = stage
        src = {"ffn": lambda: {"fin": fin_t[i], "fout": ffn_w_out[i]}, "kv": lambda: {"kv": w_kv},
               "cin": lambda: {"cin": cm_w_in[i]}, "cout": lambda: {"cout": cm_w_out[i]},
               "mix": lambda: {"q": w_q[i - N_A], "o": w_o[i - N_A]}}[part]()
        return {k: t.astype(MM) for k, t in src.items()}

    def begin_gather(stage, after):
        src = stage_sources(stage)
        names, arrays = list(src), list(src.values())
        tag = f"{stage[0]}{stage[1]}"
        lands = place_own(f"gather_own_{tag}", "gather", arrays)
        handle, token = exchange_begin(f"gather_begin_{tag}", "gather", arrays, lands, after)
        return (names, handle), token

    def end_gather(stage, pending, after):
        names, handle = pending
        W = dict(zip(names, exchange_end(f"gather_end_{stage[0]}{stage[1]}", handle, after)))
        for k in W:
            if k in ("cout", "o"):
                W[k] = W[k].reshape(1, 1, D, D)
            elif k == "fout":
                W[k] = W[k].reshape(1, 4, nf, D)
            else:
                W[k] = W[k][:, None]
        return W

    small = [cm_b_in[:, None, :], cm_dw, cm_dw_b[:, None, :], cm_ln_g, cm_ln_b, cm_b_out, ffn_dw]
    Bcin, DWc, DWBc, LNg, LNb, Bcout, DWf = all_gather("gather_small", small, [False] * len(small))
    LNg = jnp.swapaxes(LNg, 0, 1).reshape(N_A, D)
    LNb = jnp.swapaxes(LNb, 0, 1).reshape(N_A, D)
    Bcout = jnp.swapaxes(Bcout, 0, 1).reshape(N_A, D)
    DWBf = ffn_dw_b.reshape(DEPTH, N_DEV, 1, nf)
    zero_bias = jnp.zeros((D,), F32)

    xs = x.reshape(T, D)
    sv = []
    kv = hkv = None
    pending = {}
    pending[stages[0]], _ = begin_gather(stages[0], DWf)
    Ws = {stages[0]: end_gather(stages[0], pending[stages[0]], xs)}
    pending[stages[1]], token = begin_gather(stages[1], next(iter(Ws[stages[0]].values())))
    sv = [{} for _ in range(DEPTH)]
    for idx, stage in enumerate(stages):
        part, i = stage
        L, W = sv[i], Ws[stage]
        gain = {"cin": mix_pre_g[i], "cout": mix_post_g[i], "mix": mix_pre_g[i], "ffn": ffn_pre_g[i],
                "kv": kv_norm_g}[part]
        following = stages[idx + 1] if idx + 1 < len(stages) else None
        if idx + 2 < len(stages):
            after = token if idx == 0 else next(iter(W.values()))
            pending[stages[idx + 2]], token = begin_gather(stages[idx + 2], after)
        if idx + 2 < len(stages) or idx == 0:
            gain = gain + token[0, 0]
        done = None
        if part == "cin":
            L["x_in"] = xs
            L["u"], L["h"] = norm_mm(f"cm_in_{i}", xs, gain, W["cin"], 0, Bcin[:, i:i + 1])
            L["c"] = cm_glu_conv(f"cm_conv_{i}", L["u"], DWc, DWBc, i, seq)
            L["s"] = done = ln_silu(f"cm_ln_{i}", L["c"], LNg[i], LNb[i])
        elif part == "cout":
            L["y"], xs = mm_resid_norm(f"cm_out_{i}", L["s"], W["cout"], 0, Bcout[i], xs, gain)
            L["x1"] = xs
        elif part == "mix":
            L["x_in"] = xs
            L["q"], L["h"] = norm_mm(f"attn_q_{i}", xs, gain, W["q"], 0)
            L["mg"], L["lse"] = attn_fwd(f"attn_{i}", L["q"], kv, nb, seq)
            L["y"], xs = mm_resid_norm(f"attn_out_{i}", L["mg"], W["o"], 0, zero_bias, xs, mix_post_g[i])
            L["x1"] = xs
        elif part == "ffn":
            L["uf"], L["hf"], L["z"], L["ag"] = ffn_in_conv(f"ffn_in_{i}", xs, gain, W["fin"], DWf, DWBf, i, seq)
            L["yf"], xs = mm_resid_norm(f"ffn_out_{i}", L["z"], W["fout"], 0, zero_bias, xs, ffn_post_g[i])
        else:
            kv, hkv = norm_mm("kv_proj", xs, gain, W["kv"], 0)
            done = kv
        if following is not None:
            Ws[following] = end_gather(following, pending[following], xs if done is None else done)
    dx, loss_part = loss_fwd_bwd("loss", xs, loss_target.reshape(T, D))

    g_mix_pre, g_mix_post, g_ffn_pre, g_ffn_post = [None] * DEPTH, [None] * DEPTH, [None] * DEPTH, [None] * DEPTH
    g_ffn_dw, g_ffn_dwb = [None] * DEPTH, [None] * DEPTH
    g_cbin, g_cdw, g_cdwb, g_lng, g_lnb, g_cbout = ([None] * N_A for _ in range(6))
    g_kvn = dkv = None
    landed = [{} for _ in range(DEPTH)]
    in_flight = token = None
    for stage in reversed(stages):
        part, i = stage
        L, W = sv[i], Ws[stage]
        gain = {"cin": LNg[i] if i < N_A else None, "cout": mix_post_g[i], "mix": mix_post_g[i],
                "ffn": ffn_post_g[i], "kv": kv_norm_g}[part]
        if token is not None:
            gain = gain + token[0, 0]
        send = {}
        done = None
        if part == "kv":
            send["kv"] = mm_tn("kv_wg", hkv[None], dkv)
            dx, g_kvn = mm_nt_norm_bwd("kv_bwd", dkv, W["kv"], 0, sv[i + 1]["x_in"], gain, dx)
        elif part == "ffn":
            dyf, g_ffn_post[i], _ = resid_norm_bwd(f"ffn_post_bwd_{i}", dx, L["yf"], gain)
            send["fout"] = mm_tn(f"ffn_out_wg_{i}", L["z"], dyf).reshape(N_DEV, nf // 2, D)
            duf, ddw, ddwb, dx, g_ffn_pre[i] = ffn_bwd(f"ffn_bwd_{i}", dyf, W["fout"], L["uf"], L["ag"], DWf, W["fin"],
                                                       L["x1"], ffn_pre_g[i], dx, i, seq)
            g_ffn_dw[i], g_ffn_dwb[i] = ddw.reshape(N_DEV, FFN_CONV_W, nf), ddwb.reshape(-1)
            send["fin"] = mm_tn(f"ffn_in_wg_{i}", duf, L["hf"][None])
        elif part == "mix":
            dy, g_mix_post[i], _ = resid_norm_bwd(f"mix_post_bwd_{i}", dx, L["y"], gain)
            dm = mm_nt(f"attn_out_bwd_{i}", dy, W["o"], 0, F32)
            send["o"] = mm_tn(f"attn_out_wg_{i}", L["mg"], dy).reshape(N_DEV, D // N_DEV, D)
            dq, dkv = attn_bwd(f"attn_bwd_{i}", L["q"], kv, dm, L["mg"], L["lse"], dkv, nb, seq)
            send["q"] = mm_tn(f"attn_q_wg_{i}", L["h"][None], dq)
            dx, g_mix_pre[i] = mm_nt_norm_bwd(f"attn_q_bwd_{i}", dq, W["q"], 0, L["x_in"], mix_pre_g[i], dx)
        elif part == "cout":
            dy, g_mix_post[i], g_cbout[i] = resid_norm_bwd(f"mix_post_bwd_{i}", dx, L["y"], gain)
            L["ds"] = done = mm_nt(f"cm_out_bwd_{i}", dy, W["cout"], 0, F32)
            send["cout"] = mm_tn(f"cm_out_wg_{i}", L["s"], dy).reshape(N_DEV, D // N_DEV, D)
        else:
            dc, g_lng[i], g_lnb[i] = ln_silu_bwd(f"cm_ln_bwd_{i}", L["ds"], L["c"], gain, LNb[i])
            du, g_cdw[i], g_cdwb[i], dbi = cm_glu_conv_bwd(f"cm_conv_bwd_{i}", L["u"], dc, DWc, i, seq)
            g_cbin[i] = dbi.reshape(N_DEV, -1)
            send["cin"] = mm_tn(f"cm_in_wg_{i}", L["h"][None], du)
            dx, g_mix_pre[i] = mm_nt_norm_bwd(f"cm_in_bwd_{i}", du, W["cin"], 0, L["x_in"], mix_pre_g[i], dx)
        done = dx if done is None else done
        if in_flight is not None:
            (p, j), names, handle = in_flight
            landed[j].update(zip(names, exchange_end(f"scatter_end_{p}{j}", handle, done)))
        names, arrays = list(send), list(send.values())
        lands = place_own(f"scatter_own_{part}{i}", "scatter", arrays)
        handle, token = exchange_begin(f"scatter_begin_{part}{i}", "scatter", arrays, lands, done)
        in_flight = (stage, names, handle)
    grad_x = dx.reshape(nb, seq, D)

    rep_names = ["mix_pre_g", "mix_post_g", "ffn_pre_g", "ffn_post_g", "kv_norm_g", "ffn_dw_b"]
    rep_parts = [jnp.concatenate(g_mix_pre), jnp.concatenate(g_mix_post), jnp.concatenate(g_ffn_pre),
                 jnp.concatenate(g_ffn_post), g_kvn.reshape(-1), jnp.stack(g_ffn_dwb)]
    rep_w = [mix_pre_g, mix_post_g, ffn_pre_g, ffn_post_g, kv_norm_g, ffn_dw_b]
    rep_m = [m_mix_pre_g, m_mix_post_g, m_ffn_pre_g, m_ffn_post_g, m_kv_norm_g, m_ffn_dw_b]
    rep_v = [v_mix_pre_g, v_mix_post_g, v_ffn_pre_g, v_ffn_post_g, v_kv_norm_g, v_ffn_dw_b]
    sh_names = ["ffn_dw", "cm_b_in", "cm_dw", "cm_dw_b", "cm_ln_g", "cm_ln_b", "cm_b_out"]
    own = lambda per_layer, shard: jnp.stack([p.reshape((N_DEV,) + shard) for p in per_layer], axis=1)
    sh_parts = [own(g_ffn_dw, ffn_dw.shape[1:]), own(g_cbin, cm_b_in.shape[1:]), own(g_cdw, cm_dw.shape[1:]),
                own(g_cdwb, cm_dw_b.shape[1:]), own(g_lng, cm_ln_g.shape[1:]), own(g_lnb, cm_ln_b.shape[1:]),
                own(g_cbout, cm_b_out.shape[1:])]
    sh_w = [ffn_dw, cm_b_in, cm_dw, cm_dw_b, cm_ln_g, cm_ln_b, cm_b_out]
    sh_m = [m_ffn_dw, m_cm_b_in, m_cm_dw, m_cm_dw_b, m_cm_ln_g, m_cm_ln_b, m_cm_b_out]
    sh_v = [v_ffn_dw, v_cm_b_in, v_cm_dw, v_cm_dw_b, v_cm_ln_g, v_cm_ln_b, v_cm_b_out]
    rep_pack = _pack([loss_part] + rep_parts)
    sh_pack = jnp.stack([_pack([p[k] for p in sh_parts]) for k in range(N_DEV)])
    n_rep = rep_pack.shape[0]
    small = []
    for kind, pack in (("gather", rep_pack), ("scatter", sh_pack)):
        lands = place_own(f"{kind}_own_small", kind, [pack])
        handle, token = exchange_begin(f"{kind}_begin_small", kind, [pack], lands, token)
        small.append((kind, handle))

    def big_update(name, w, m, v, key, layers, after):
        as3 = lambda t: t.reshape((-1,) + t.shape[-2:])
        outs = adamw_sum(name, as3(w), as3(m), as3(v), [landed[i][key] for i in layers], after)
        return [t.reshape(w.shape) for t in outs]

    conf, attn = range(N_A), range(N_A, DEPTH)
    upd = {}
    upd["ffn_w_in"] = [t_(t) for t in big_update("adam_ffn_w_in", fin_t, m_fin_t, v_fin_t, "fin", range(DEPTH), token)]
    upd["ffn_w_out"] = big_update("adam_ffn_w_out", ffn_w_out, m_ffn_w_out, v_ffn_w_out, "fout", range(DEPTH), token)
    upd["w_kv"] = big_update("adam_w_kv", w_kv, m_w_kv, v_w_kv, "kv", [N_A - 1], token)
    upd["w_q"] = big_update("adam_w_q", w_q, m_w_q, v_w_q, "q", attn, token)
    upd["w_o"] = big_update("adam_w_o", w_o, m_w_o, v_w_o, "o", attn, upd["w_q"][0])
    (p, j), names, handle = in_flight
    landed[j].update(zip(names, exchange_end(f"scatter_end_{p}{j}", handle, upd["w_o"][0])))
    upd["cm_w_in"] = big_update("adam_cm_w_in", cm_w_in, m_cm_w_in, v_cm_w_in, "cin", conf, token)
    upd["cm_w_out"] = big_update("adam_cm_w_out", cm_w_out, m_cm_w_out, v_cm_w_out, "cout", conf, upd["cm_w_in"][0])
    (rep_landed,), (sh_landed,) = (exchange_end(f"{kind}_end_small", handle, upd["cm_w_out"][0])
                                   for kind, handle in small)
    rep_sum = sum_partials("sum_small_rep", rep_landed)
    sh_sum = sum_partials("sum_small_sh", sh_landed)
    rep_shapes = [(1, 1)] + [w.shape for w in rep_w]
    sh_shapes = [w.shape for w in sh_w]
    g_small = jnp.concatenate([rep_sum, sh_sum])
    pad1 = jnp.zeros((1, 1), F32)
    d_s, m_s, v_s = adamw_small("adam_small", jnp.concatenate([_pack([pad1] + rep_w), _pack(sh_w)]), g_small,
                                jnp.concatenate([_pack([pad1] + rep_m), _pack(sh_m)]),
                                jnp.concatenate([_pack([pad1] + rep_v), _pack(sh_v)]))
    split = lambda t: (_unpack(t[:n_rep], rep_shapes), _unpack(t[n_rep:], sh_shapes))
    for (rep_t, sh_t), slot in zip([split(g_small), split(d_s), split(m_s), split(v_s)], range(4)):
        if slot == 0:
            loss = rep_t[0].reshape(())
        for name, t in zip(rep_names, rep_t[1:]):
            upd.setdefault(name, [None] * 4)[slot] = t
        for name, t in zip(sh_names, sh_t):
            upd.setdefault(name, [None] * 4)[slot] = t

    order = ["mix_pre_g", "mix_post_g", "ffn_pre_g", "ffn_post_g", "cm_w_in", "cm_b_in", "cm_dw", "cm_dw_b", "cm_ln_g",
             "cm_ln_b", "cm_w_out", "cm_b_out", "kv_norm_g", "w_kv", "w_q", "w_o", "ffn_w_in", "ffn_dw", "ffn_dw_b",
             "ffn_w_out"]
    return (loss, grad_x, *[upd[n][0] for n in order], *[upd[n][1] for n in order],
            *[upd[n][2] for n in order], *[upd[n][3] for n in order])
```

```python
import functools

import jax
import jax.numpy as jnp
from jax import lax
from jax.experimental import pallas as pl
from jax.experimental.pallas import tpu as pltpu

N_DEV = 8
N_A = 2
DEPTH = 4
N_HEADS = 8
N_GROUPS = 3
DILATIONS = (1, 4, 16)
ATT_BLOCK = 128
ATT_BATCH = 8
CONV_W = 31
FFN_CONV_W = 3
CONV_HALO = 32
FFN_HALO = 16
ROW_CHUNK_FWD = 16
ROW_CHUNK_BWD = 16
GROUP_ALL_MAX_N = 384
FFN_COL_TILE = 1024
EPS = 1e-6
NEG = -1e30
ADAM_LR, ADAM_B1, ADAM_B2, ADAM_EPS, ADAM_WD, ADAM_STEP = 0.001, 0.9, 0.999, 1e-08, 0.01, 10
MM = jnp.bfloat16
F32 = jnp.float32
VMEM_LIMIT_BYTES = 56 * 1024 * 1024
PACK = 1024
MESH_ID = pl.DeviceIdType.MESH

_pallas = pl.pallas_call


def _call(body, *, name, out_shape, grid=(), in_specs=None, out_specs=None, scratch=()):
    return _pallas(body, name=name, out_shape=out_shape, grid=grid, in_specs=in_specs, out_specs=out_specs,
                   scratch_shapes=list(scratch),
                   compiler_params=pltpu.CompilerParams(vmem_limit_bytes=VMEM_LIMIT_BYTES))


def _tile(n, pref):
    if n <= pref:
        return n
    t = pref - pref % 8
    while n % t:
        t -= 8
    assert t > 0, (n, pref)
    return t


def _sds(shape, dtype):
    return jax.ShapeDtypeStruct(tuple(shape), dtype)


def _dot(a, b):
    return jnp.dot(a, b, preferred_element_type=F32)


def _dot_nt(a, b):
    return lax.dot_general(a, b, (((1,), (1,)), ((), ())), preferred_element_type=F32)


def _dot_tn(a, b):
    return lax.dot_general(a, b, (((0,), (0,)), ((), ())), preferred_element_type=F32)


def _sigmoid(x):
    return 0.5 * jnp.tanh(0.5 * x) + 0.5


def _my_index():
    return 4 * lax.axis_index("x") + 2 * lax.axis_index("y") + lax.axis_index("c")


def _exchange(name, arrays, out_shapes, pieces, src_of, dst_of):
    n = len(arrays)
    base = [sum(pieces[:a]) for a in range(n)]
    total = sum(pieces)

    def body(*refs):
        ins, outs = refs[:n], refs[n:2 * n]
        send_sems, recv_sems, local_sems = refs[2 * n:]
        x, y, c = lax.axis_index("x"), lax.axis_index("y"), lax.axis_index("c")
        me = 4 * x + 2 * y + c
        copies = []
        for a in range(n):
            for k, (s, d) in enumerate(zip(src_of(a, ins[a], me), dst_of(a, outs[a], me))):
                cp = pltpu.make_async_copy(s, d, local_sems.at[base[a] + k])
                cp.start()
                copies.append(cp)
        remote = []
        for m in range(1, N_DEV):
            px, py, pc = x ^ (m >> 2), y ^ ((m >> 1) & 1), c ^ (m & 1)
            peer = 4 * px + 2 * py + pc
            for a in range(n):
                for k, (s, d) in enumerate(zip(src_of(a, ins[a], peer), dst_of(a, outs[a], me))):
                    cp = pltpu.make_async_remote_copy(src_ref=s, dst_ref=d, send_sem=send_sems.at[base[a] + k, m - 1],
                                                      recv_sem=recv_sems.at[base[a] + k, m - 1],
                                                      device_id=(px, py, pc), device_id_type=MESH_ID)
                    cp.start()
                    remote.append(cp)
        for cp in copies:
            cp.wait()
        for cp in remote:
            cp.wait_send()
        for m in range(1, N_DEV):
            px, py, pc = x ^ (m >> 2), y ^ ((m >> 1) & 1), c ^ (m & 1)
            peer = 4 * px + 2 * py + pc
            for a in range(n):
                for k, (s, d) in enumerate(zip(src_of(a, ins[a], me), dst_of(a, outs[a], peer))):
                    pltpu.make_async_remote_copy(src_ref=s, dst_ref=d, send_sem=send_sems.at[base[a] + k, m - 1],
                                                 recv_sem=recv_sems.at[base[a] + k, m - 1], device_id=(px, py, pc),
                                                 device_id_type=MESH_ID).wait_recv()

    any_spec = pl.BlockSpec(memory_space=pl.ANY)
    return _call(body, name=name, out_shape=[_sds(s, a.dtype) for s, a in zip(out_shapes, arrays)],
                 in_specs=[any_spec] * n, out_specs=[any_spec] * n,
                 scratch=[pltpu.SemaphoreType.DMA((total, N_DEV - 1)), pltpu.SemaphoreType.DMA((total, N_DEV - 1)),
                          pltpu.SemaphoreType.DMA((total,))])(*arrays)


def all_gather(name, arrays, row_sharded):
    def out_shape(a):
        s = arrays[a].shape
        return (s[0], N_DEV) + s[1:] if row_sharded[a] else (N_DEV,) + s

    def src_of(a, ref, peer):
        if row_sharded[a]:
            return [ref.at[l] for l in range(arrays[a].shape[0])]
        return [ref]

    def dst_of(a, ref, me):
        if row_sharded[a]:
            return [ref.at[l, me] for l in range(arrays[a].shape[0])]
        return [ref.at[me]]

    pieces = [arrays[a].shape[0] if row_sharded[a] else 1 for a in range(len(arrays))]
    return _exchange(name, arrays, [out_shape(a) for a in range(len(arrays))], pieces, src_of, dst_of)


def _src_view(kind, ref, peer):
    return ref if kind == "gather" else ref.at[peer]


def _peers(x, y, c):
    for m in range(1, N_DEV):
        px, py, pc = x ^ (m >> 2), y ^ ((m >> 1) & 1), c ^ (m & 1)
        yield m - 1, (px, py, pc), 4 * px + 2 * py + pc


def place_own(name, kind, srcs):
    n = len(srcs)
    shapes = [(N_DEV,) + s.shape if kind == "gather" else s.shape for s in srcs]
    steps = 2 if all(s.shape[-2] % 32 == 0 for s in srcs) else 1

    def body(*refs):
        for a in range(n):
            refs[n + a][...] = refs[a][...]

    def spec(shape, own_block):
        R, C = shape[-2:]
        tr = R // steps
        if own_block:
            return pl.BlockSpec((None, tr, C), lambda i: (_my_index(), i, 0))
        return pl.BlockSpec((tr, C), lambda i: (i, 0))

    return _call(body, name=name, grid=(steps,), out_shape=[_sds(s, a.dtype) for s, a in zip(shapes, srcs)],
                 in_specs=[spec(s.shape, kind == "scatter") for s in srcs],
                 out_specs=[spec(s, True) for s in shapes])(*srcs)


_HBM_SPEC = pl.BlockSpec(memory_space=pltpu.HBM)
_SEM_SPEC = pl.BlockSpec(memory_space=pltpu.SEMAPHORE)
_DATAFLOW = pltpu.SideEffectType.DATAFLOW_SIDE_EFFECTING


def _remote(kind, src, land, send_sems, recv_sems, a, slot, frm, to_id, at):
    return pltpu.make_async_remote_copy(src_ref=_src_view(kind, src, frm), dst_ref=land.at[at],
                                        send_sem=send_sems.at[a * (N_DEV - 1) + slot],
                                        recv_sem=recv_sems.at[a * (N_DEV - 1) + slot],
                                        device_id=to_id, device_id_type=MESH_ID)


def exchange_begin(name, kind, srcs, lands, after):
    n = len(srcs)

    def body(*refs):
        ins, lnd = refs[:n], refs[n:2 * n]
        send_sems, recv_sems = refs[2 * n + 1], refs[2 * n + 2]
        token = refs[-1]
        x, y, c = lax.axis_index("x"), lax.axis_index("y"), lax.axis_index("c")
        me = 4 * x + 2 * y + c
        for slot, peer_id, peer in _peers(x, y, c):
            for a in range(n):
                _remote(kind, ins[a], lnd[a], send_sems, recv_sems, a, slot, peer, peer_id, me).start()
        token[...] = jnp.zeros_like(token)

    hbm = lambda t: pltpu.HBM(t.shape, t.dtype)
    outs = _pallas(
        body, name=name,
        out_shape=(pltpu.SemaphoreType.DMA((n * (N_DEV - 1),)), pltpu.SemaphoreType.DMA((n * (N_DEV - 1),)),
                   *[hbm(t) for t in srcs], *[hbm(t) for t in lands], _sds((8, 128), F32)),
        in_specs=[_HBM_SPEC] * (2 * n) + [pl.BlockSpec(memory_space=pl.ANY)],
        out_specs=(_SEM_SPEC, _SEM_SPEC, *[_HBM_SPEC] * (2 * n), pl.BlockSpec(memory_space=pltpu.VMEM)),
        input_output_aliases={i: 2 + i for i in range(2 * n)},
        compiler_params=pltpu.CompilerParams(has_side_effects=_DATAFLOW),
    )(*[pltpu.with_memory_space_constraint(t, pltpu.HBM) for t in list(srcs) + list(lands)], after)
    return (kind, outs[0], outs[1], list(outs[2:2 + n]), list(outs[2 + n:2 + 2 * n])), outs[-1]


def exchange_end(name, handle, after):
    kind, send_sems, recv_sems, srcs, lands = handle
    n = len(srcs)

    def body(*refs):
        ins, lnd = refs[:n], refs[n:2 * n]
        s_sems, r_sems = refs[2 * n], refs[2 * n + 1]
        x, y, c = lax.axis_index("x"), lax.axis_index("y"), lax.axis_index("c")
        me = 4 * x + 2 * y + c
        for slot, peer_id, peer in _peers(x, y, c):
            for a in range(n):
                _remote(kind, ins[a], lnd[a], s_sems, r_sems, a, slot, peer, peer_id, me).wait_send()
        for slot, peer_id, peer in _peers(x, y, c):
            for a in range(n):
                _remote(kind, ins[a], lnd[a], s_sems, r_sems, a, slot, me, peer_id, peer).wait_recv()

    hbm = lambda t: pltpu.HBM(t.shape, t.dtype)
    outs = _pallas(
        body, name=name, out_shape=tuple(hbm(t) for t in srcs + lands),
        in_specs=[_HBM_SPEC] * (2 * n) + [_SEM_SPEC, _SEM_SPEC, pl.BlockSpec(memory_space=pl.ANY)],
        out_specs=tuple([_HBM_SPEC] * (2 * n)), input_output_aliases={i: i for i in range(2 * n)},
        compiler_params=pltpu.CompilerParams(has_side_effects=_DATAFLOW),
    )(*srcs, *lands, send_sems, recv_sems, after)
    return list(outs[n:])


def norm_mm(name, x, gain, w, layer, bias=None):
    T, D = x.shape
    nsh, _, _, n = w.shape
    tm = _tile(T, 1024)
    grp = nsh if n <= GROUP_ALL_MAX_N else nsh // 2
    steps = nsh // grp

    def body(*refs):
        if bias is None:
            x_ref, g_ref, w_ref, u_ref, h_ref = refs
        else:
            x_ref, g_ref, w_ref, b_ref, u_ref, h_ref = refs

        @pl.when(pl.program_id(1) == 0)
        def _():
            xf = x_ref[...]
            r = lax.rsqrt(jnp.mean(xf * xf, axis=-1, keepdims=True) + EPS)
            h_ref[...] = (xf * r * g_ref[...]).astype(h_ref.dtype)

        h = h_ref[...]
        for k in range(grp):
            acc = _dot(h, w_ref[k])
            if bias is not None:
                acc = acc + b_ref[k]
            u_ref[k] = acc.astype(u_ref.dtype)

    in_specs = [pl.BlockSpec((tm, D), lambda i, j: (i, 0)),
                pl.BlockSpec((1, D), lambda i, j: (0, 0)),
                pl.BlockSpec((grp, None, D, n), lambda i, j: (j, layer, 0, 0))]
    args = [x, gain.reshape(1, D), w]
    if bias is not None:
        in_specs.append(pl.BlockSpec((grp, None, 1, n), lambda i, j: (j, layer, 0, 0)))
        args.append(bias)
    return _call(body, name=name, grid=(T // tm, steps), in_specs=in_specs,
                 out_specs=[pl.BlockSpec((grp, tm, n), lambda i, j: (j, i, 0)),
                            pl.BlockSpec((tm, D), lambda i, j: (i, 0))],
                 out_shape=[_sds((nsh, T, n), MM), _sds((T, D), MM)])(*args)


def mm_resid_norm(name, a, w, layer, bias, x, gain):
    nk, T, kk = a.shape
    D = x.shape[1]
    tm = _tile(T, 1024 if nk * kk <= D else 512)

    def body(a_ref, w_ref, b_ref, x_ref, g_ref, y_ref, xn_ref):
        y = _dot(a_ref[0], w_ref[0])
        for q in range(1, nk):
            y = y + _dot(a_ref[q], w_ref[q])
        y = y + b_ref[...]
        y_ref[...] = y
        r = lax.rsqrt(jnp.mean(y * y, axis=-1, keepdims=True) + EPS)
        xn_ref[...] = x_ref[...] + y * r * g_ref[...]

    return _call(body, name=name, grid=(T // tm,),
                 in_specs=[pl.BlockSpec((nk, tm, kk), lambda i: (0, i, 0)),
                           pl.BlockSpec((None, nk, kk, D), lambda i: (layer, 0, 0, 0)),
                           pl.BlockSpec((1, D), lambda i: (0, 0)),
                           pl.BlockSpec((tm, D), lambda i: (i, 0)),
                           pl.BlockSpec((1, D), lambda i: (0, 0))],
                 out_specs=[pl.BlockSpec((tm, D), lambda i: (i, 0))] * 2,
                 out_shape=[_sds((T, D), F32)] * 2)(a, w, bias.reshape(1, D), x, gain.reshape(1, D))


def _halo_maps(tm, hb, T):
    per = tm // hb
    last = T // hb - 1
    return (lambda i: jnp.maximum(i * per - 1, 0)), (lambda i: jnp.minimum((i + 1) * per, last))


def cm_glu_conv(name, u, dw, dwb, layer, seq):
    _, T, n = u.shape
    ct = dw.shape[-1]
    per = n // ct
    nct = 4 * per
    tm = _tile(seq, 512)
    tps = seq // tm
    hb = CONV_HALO
    prev, _ = _halo_maps(tm, hb, T)
    u4 = u.reshape(2, 4, T, n)

    rows = _tile(tm, 2 * ROW_CHUNK_FWD)

    def body(u_ref, uh_ref, w_ref, b_ref, o_ref, z_ref):
        first = (pl.program_id(0) % tps) == 0
        um = u_ref[...].astype(F32)
        uh = uh_ref[...].astype(F32)
        z_ref[0, pl.ds(hb, tm), :] = um[0] * _sigmoid(um[1])
        z_ref[0, pl.ds(0, hb), :] = jnp.where(first, 0.0, uh[0] * _sigmoid(uh[1]))
        _stage_shifts(z_ref, hb + tm, back=True)
        for r0 in range(0, tm, rows):
            acc = b_ref[...]
            for j in range(CONV_W):
                acc = acc + w_ref[pl.ds(CONV_W - 1 - j, 1), :] * z_ref[j % 8, pl.ds(hb + r0 - (j - j % 8), rows), :]
            o_ref[pl.ds(r0, rows), :] = acc

    return _call(body, name=name, grid=(T // tm, nct),
                 in_specs=[pl.BlockSpec((2, None, tm, ct), lambda i, c: (0, c // per, i, c % per)),
                           pl.BlockSpec((2, None, hb, ct), lambda i, c: (0, c // per, prev(i), c % per)),
                           pl.BlockSpec((None, None, CONV_W, ct), lambda i, c: (c, layer, 0, 0)),
                           pl.BlockSpec((None, None, 1, ct), lambda i, c: (c, layer, 0, 0))],
                 out_specs=pl.BlockSpec((tm, ct), lambda i, c: (i, c)),
                 out_shape=_sds((T, nct * ct), F32),
                 scratch=[pltpu.VMEM((8, tm + hb, ct), F32)])(u4, u4, dw, dwb)


def _stage_shifts(z_ref, n_rows, back):
    for r in range(1, 8):
        if back:
            z_ref[r, pl.ds(8, n_rows - 8), :] = z_ref[0, pl.ds(8 - r, n_rows - 8), :]
        else:
            z_ref[r, pl.ds(0, n_rows - 8), :] = z_ref[0, pl.ds(r, n_rows - 8), :]


def ln_silu(name, c, g, b):
    T, D = c.shape
    tm = _tile(T, 1024)

    def body(c_ref, g_ref, b_ref, s_ref):
        cf = c_ref[...]
        mu = jnp.mean(cf, axis=-1, keepdims=True)
        xc = cf - mu
        r = lax.rsqrt(jnp.mean(xc * xc, axis=-1, keepdims=True) + EPS)
        t = xc * r * g_ref[...] + b_ref[...]
        s_ref[...] = (t * _sigmoid(t)).astype(s_ref.dtype)

    return _call(body, name=name, grid=(T // tm,),
                 in_specs=[pl.BlockSpec((tm, D), lambda i: (i, 0)), pl.BlockSpec((1, D), lambda i: (0, 0)),
                           pl.BlockSpec((1, D), lambda i: (0, 0))],
                 out_specs=pl.BlockSpec((None, tm, D), lambda i: (0, i, 0)),
                 out_shape=_sds((1, T, D), MM))(c, g.reshape(1, D), b.reshape(1, D))


def _col_tiles(n, width):
    return [(c0, min(width, n - c0)) for c0 in range(0, n, width)]


def ffn_in_conv(name, x, gain, wt, dw, dwb, layer, seq):
    T, D = x.shape
    n = wt.shape[2]
    tm = _tile(seq, 1024)
    tps = seq // tm
    hb = 8
    rows = _tile(tm, ROW_CHUNK_FWD)
    K = FFN_CONV_W
    w5 = wt.reshape(2, 4, n, D)
    dw4 = dw.reshape(2, 4, dw.shape[1], K, n)
    dwb4 = dwb.reshape(dwb.shape[0], 2, 4, 1, n)

    def body(x_ref, g_ref, w_ref, cw_ref, cb_ref, u_ref, h_ref, z_ref, ag_ref, pad_ref, carry_ref):
        i, j = pl.program_id(0), pl.program_id(1)
        first = (i % tps) == 0

        @pl.when(j == 0)
        def _():
            xf = x_ref[...]
            r = lax.rsqrt(jnp.mean(xf * xf, axis=-1, keepdims=True) + EPS)
            h_ref[...] = (xf * r * g_ref[...]).astype(h_ref.dtype)

        h = h_ref[...]
        for c0, wc in _col_tiles(n, FFN_COL_TILE):
            cols = pl.ds(c0, wc)
            for half in range(2):
                res = _dot_nt(h, w_ref[half, cols, :])
                u_ref[half, :, cols] = res.astype(u_ref.dtype)
                pad_ref[half, pl.ds(hb, tm), cols] = res
                pad_ref[half, pl.ds(0, hb), cols] = jnp.where(first, 0.0, carry_ref[j, half, :, cols])
            for r0 in range(0, tm, rows):
                conv = []
                for half in range(2):
                    acc = cb_ref[half, :, cols] + cw_ref[half, pl.ds(K - 1, 1), cols] * pad_ref[
                        half, pl.ds(hb + r0, rows), cols]
                    for k in range(K - 1):
                        acc = acc + cw_ref[half, pl.ds(k, 1), cols] * pad_ref[
                            half, pl.ds(hb + r0 - (K - 1) + k, rows), cols]
                    conv.append(acc)
                a, g = conv
                sg = _sigmoid(g)
                silu = g * sg
                z_ref[pl.ds(r0, rows), cols] = (silu * a).astype(z_ref.dtype)
                ag_ref[0, pl.ds(r0, rows), cols] = silu.astype(ag_ref.dtype)
                ag_ref[1, pl.ds(r0, rows), cols] = (a * (sg * (1.0 + g * (1.0 - sg)))).astype(ag_ref.dtype)
            for half in range(2):
                carry_ref[j, half, :, cols] = pad_ref[half, pl.ds(tm, hb), cols]

    u, h, z, ag = _call(
        body, name=name, grid=(T // tm, 4),
        in_specs=[pl.BlockSpec((tm, D), lambda i, j: (i, 0)),
                  pl.BlockSpec((1, D), lambda i, j: (0, 0)),
                  pl.BlockSpec((2, None, n, D), lambda i, j: (0, j, 0, 0)),
                  pl.BlockSpec((2, None, None, K, n), lambda i, j: (0, j, layer, 0, 0)),
                  pl.BlockSpec((None, 2, None, 1, n), lambda i, j: (layer, 0, j, 0, 0))],
        out_specs=[pl.BlockSpec((2, None, tm, n), lambda i, j: (0, j, i, 0)),
                   pl.BlockSpec((tm, D), lambda i, j: (i, 0)),
                   pl.BlockSpec((None, tm, n), lambda i, j: (j, i, 0)),
                   pl.BlockSpec((2, None, tm, n), lambda i, j: (0, j, i, 0))],
        out_shape=[_sds((2, 4, T, n), MM), _sds((T, D), MM), _sds((4, T, n), MM), _sds((2, 4, T, n), MM)],
        scratch=[pltpu.VMEM((2, tm + hb, n), F32), pltpu.VMEM((4, 2, hb, n), F32)])(
            x, gain.reshape(1, D), w5, dw4, dwb4)
    return u.reshape(8, T, n), h, z, ag


def _head_specs(seq, dh, q_heads, kv_heads):
    def spec(per, base):
        def imap(b, h, g):
            f = base + g * N_HEADS + h
            return (f // per, b, f % per)
        return pl.BlockSpec((None, seq, dh), imap)
    return spec(q_heads, 0), spec(kv_heads, 0), spec(kv_heads, N_GROUPS * N_HEADS)


def _rows(start, d, blocks=1):
    size = blocks * ATT_BLOCK
    return pl.ds(start, size, stride=d) if d > 1 else pl.ds(start, size)


def _att_pad():
    return max(ATT_BLOCK * d for d in DILATIONS[:-1])


def _band_mask(first, nblk):
    keys = ATT_BLOCK if nblk == 1 else 2 * ATT_BLOCK
    shape = (ATT_BATCH, ATT_BLOCK, keys)
    qi = lax.broadcasted_iota(jnp.int32, shape, 1)
    kj = lax.broadcasted_iota(jnp.int32, shape, 2)
    if nblk == 1:
        return kj <= qi
    n = (first + lax.broadcasted_iota(jnp.int32, shape, 0)) % nblk
    return (kj >= qi) & (kj <= qi + ATT_BLOCK) & ((n > 0) | (kj >= ATT_BLOCK))


def _block_rows(idx, d, nblk):
    r, n = idx // nblk, idx % nblk
    rq = _rows(r + d * ATT_BLOCK * n, d)
    if nblk == 1:
        return rq, _rows(_att_pad() + r + d * ATT_BLOCK * n, d)
    return rq, _rows(_att_pad() + r + d * ATT_BLOCK * (n - 1), d, blocks=2)


def _bdot(a, b, ca, cb):
    return lax.dot_general(a, b, (((ca,), (cb,)), ((0,), (0,))), preferred_element_type=F32)


def attn_fwd(name, q, kv, nb, seq):
    _, T, qn = q.shape
    dh = qn * N_DEV // (N_GROUPS * N_HEADS)
    scale = 1.0 / (dh ** 0.5)
    qs, ks, vs = _head_specs(seq, dh, qn // dh, kv.shape[2] // dh)

    def body(q_ref, k_ref, v_ref, m_ref, l_ref, qf, kf, vf, *branch):
        og, lg = branch[:N_GROUPS], branch[N_GROUPS:]
        pad = _att_pad()
        qf[...] = q_ref[...].astype(F32)
        for t_ref, s_ref in ((k_ref, kf), (v_ref, vf)):
            s_ref[pl.ds(0, pad), :] = jnp.zeros((pad, dh), F32)
            s_ref[pl.ds(pad, seq), :] = t_ref[...].astype(F32)
        for g in range(N_GROUPS):
            d = DILATIONS[g]
            nblk = seq // d // ATT_BLOCK

            def blocks(it, carry, d=d, nblk=nblk, g=g):
                first = it * ATT_BATCH
                rows = [_block_rows(first + b, d, nblk) for b in range(ATT_BATCH)]
                qb = jnp.stack([qf[rq, :] for rq, _ in rows]).astype(MM)
                kb = jnp.stack([kf[rk, :] for _, rk in rows]).astype(MM)
                vb = jnp.stack([vf[rk, :] for _, rk in rows]).astype(MM)
                s = jnp.where(_band_mask(first, nblk), _bdot(qb, kb, 2, 2) * scale, NEG)
                m = jnp.max(s, axis=-1, keepdims=True)
                p = jnp.exp(s - m)
                den = jnp.sum(p, axis=-1, keepdims=True)
                o = _bdot(p.astype(MM), vb, 2, 1) / den
                lse = m + jnp.log(den)
                for b, (rq, _) in enumerate(rows):
                    og[g][rq, :] = o[b]
                    lg[g][rq, :] = jnp.broadcast_to(lse[b], (ATT_BLOCK, dh))
                return carry

            @pl.when(pl.program_id(2) == g)
            def _(blocks=blocks, d=d, nblk=nblk):
                lax.fori_loop(0, d * nblk // ATT_BATCH, blocks, 0)

        @pl.when(pl.program_id(2) == N_GROUPS - 1)
        def _():
            mx = jnp.maximum(jnp.maximum(lg[0][...], lg[1][...]), lg[2][...])
            e = [jnp.exp(lg[g][...] - mx) for g in range(N_GROUPS)]
            tot = e[0] + e[1] + e[2]
            m_ref[...] = ((e[0] * og[0][...] + e[1] * og[1][...] + e[2] * og[2][...]) / tot).astype(m_ref.dtype)
            l_ref[...] = mx + jnp.log(tot)

    return _call(body, name=name, grid=(nb, N_HEADS, N_GROUPS), in_specs=[qs, ks, vs],
                 out_specs=[pl.BlockSpec((None, seq, dh), lambda b, h, g: (0, b, h)),
                            pl.BlockSpec((seq, dh), lambda b, h, g: (b, h))],
                 out_shape=[_sds((1, T, N_HEADS * dh), MM), _sds((T, N_HEADS * dh), F32)],
                 scratch=[pltpu.VMEM((seq, dh), F32)] + [pltpu.VMEM((_att_pad() + seq, dh), F32)] * 2
                 + [pltpu.VMEM((seq, dh), F32)] * (2 * N_GROUPS))(q, kv, kv)


def loss_fwd_bwd(name, x, target):
    T, D = x.shape
    tm = _tile(T, 1024)

    def body(x_ref, t_ref, dx_ref, l_ref):
        @pl.when(pl.program_id(0) == 0)
        def _():
            l_ref[...] = jnp.zeros_like(l_ref)
        err = x_ref[...] - t_ref[...]
        dx_ref[...] = err * (1.0 / D)
        l_ref[...] += 0.5 * jnp.sum(jnp.mean(err * err, axis=-1, keepdims=True), axis=0, keepdims=True)

    dx, l = _call(body, name=name, grid=(T // tm,),
                  in_specs=[pl.BlockSpec((tm, D), lambda i: (i, 0))] * 2,
                  out_specs=[pl.BlockSpec((tm, D), lambda i: (i, 0)), pl.BlockSpec((1, 1), lambda i: (0, 0))],
                  out_shape=[_sds((T, D), F32), _sds((1, 1), F32)])(x, target)
    return dx, l


def resid_norm_bwd(name, dx, y, gain):
    T, D = y.shape
    tm = _tile(T, 1024)

    def body(dx_ref, y_ref, g_ref, dy_ref, dg_ref, db_ref):
        @pl.when(pl.program_id(0) == 0)
        def _():
            dg_ref[...] = jnp.zeros_like(dg_ref)
            db_ref[...] = jnp.zeros_like(db_ref)
        y = y_ref[...]
        d = dx_ref[...]
        r = lax.rsqrt(jnp.mean(y * y, axis=-1, keepdims=True) + EPS)
        yh = y * r
        dyh = d * g_ref[...]
        dy = r * (dyh - yh * jnp.mean(dyh * yh, axis=-1, keepdims=True))
        dy_ref[...] = dy.astype(dy_ref.dtype)
        dg_ref[...] += jnp.sum(d * yh, axis=0, keepdims=True)
        db_ref[...] += jnp.sum(dy, axis=0, keepdims=True)

    return _call(body, name=name, grid=(T // tm,),
                 in_specs=[pl.BlockSpec((tm, D), lambda i: (i, 0))] * 2 + [pl.BlockSpec((1, D), lambda i: (0, 0))],
                 out_specs=[pl.BlockSpec((None, tm, D), lambda i: (0, i, 0))] + [pl.BlockSpec((1, D), lambda i: (0, 0))] * 2,
                 out_shape=[_sds((1, T, D), MM), _sds((1, D), F32), _sds((1, D), F32)])(dx, y, gain.reshape(1, D))


def mm_nt(name, dy, w, layer, out_dtype):
    _, T, D = dy.shape
    _, nk, kk, _ = w.shape
    tm = _tile(T, 1024)

    def body(dy_ref, w_ref, o_ref):
        o_ref[...] = _dot_nt(dy_ref[...], w_ref[...]).astype(o_ref.dtype)

    return _call(body, name=name, grid=(T // tm, nk),
                 in_specs=[pl.BlockSpec((None, tm, D), lambda i, q: (0, i, 0)),
                           pl.BlockSpec((None, None, kk, D), lambda i, q: (layer, q, 0, 0))],
                 out_specs=pl.BlockSpec((None, tm, kk), lambda i, q: (q, i, 0)),
                 out_shape=_sds((nk, T, kk), out_dtype))(dy, w)


def mm_nt_norm_bwd(name, du, w, layer, x_in, gain, dx_res):
    nsh, T, n = du.shape
    D = x_in.shape[1]
    tm = _tile(T, 512)
    grp = nsh if n <= GROUP_ALL_MAX_N else nsh // 2
    steps = nsh // grp

    def body(du_ref, w_ref, x_ref, g_ref, dr_ref, dx_ref, dg_ref, acc_ref):
        i, j = pl.program_id(0), pl.program_id(1)

        @pl.when((i == 0) & (j == 0))
        def _():
            dg_ref[...] = jnp.zeros_like(dg_ref)

        part = _dot_nt(du_ref[0], w_ref[0])
        for k in range(1, grp):
            part = part + _dot_nt(du_ref[k], w_ref[k])

        @pl.when(j == 0)
        def _():
            acc_ref[...] = part

        @pl.when(j > 0)
        def _():
            acc_ref[...] += part

        @pl.when(j == steps - 1)
        def _():
            x = x_ref[...]
            dh = acc_ref[...]
            r = lax.rsqrt(jnp.mean(x * x, axis=-1, keepdims=True) + EPS)
            xh = x * r
            dxh = dh * g_ref[...]
            dx_ref[...] = dr_ref[...] + r * (dxh - xh * jnp.mean(dxh * xh, axis=-1, keepdims=True))
            dg_ref[...] += jnp.sum(dh * xh, axis=0, keepdims=True)

    return _call(body, name=name, grid=(T // tm, steps),
                 in_specs=[pl.BlockSpec((grp, tm, n), lambda i, j: (j, i, 0)),
                           pl.BlockSpec((grp, None, D, n), lambda i, j: (j, layer, 0, 0)),
                           pl.BlockSpec((tm, D), lambda i, j: (i, 0)),
                           pl.BlockSpec((1, D), lambda i, j: (0, 0)),
                           pl.BlockSpec((tm, D), lambda i, j: (i, 0))],
                 out_specs=[pl.BlockSpec((tm, D), lambda i, j: (i, 0)), pl.BlockSpec((1, D), lambda i, j: (0, 0))],
                 out_shape=[_sds((T, D), F32), _sds((1, D), F32)],
                 scratch=[pltpu.VMEM((tm, D), F32)])(du, w, x_in, gain.reshape(1, D), dx_res)


def mm_tn(name, a, b):
    na, T, ka = a.shape
    nb, _, kb = b.shape
    nj = max(na, nb)

    def body(a_ref, b_ref, o_ref):
        o_ref[...] = _dot_tn(a_ref[...], b_ref[...]).astype(o_ref.dtype)

    return _call(body, name=name, grid=(nj,),
                 in_specs=[pl.BlockSpec((None, T, ka), (lambda j: (j, 0, 0)) if na > 1 else (lambda j: (0, 0, 0))),
                           pl.BlockSpec((None, T, kb), (lambda j: (j, 0, 0)) if nb > 1 else (lambda j: (0, 0, 0)))],
                 out_specs=pl.BlockSpec((None, ka, kb), lambda j: (j, 0, 0)),
                 out_shape=_sds((nj, ka, kb), MM))(a, b)


def ffn_bwd(name, dy, wout, u, ag, dw, wt, x_in, gain, dx_res, layer, seq):
    _, T, D = dy.shape
    n = wt.shape[2]
    tm = _tile(seq, 512)
    tps = seq // tm
    hb = FFN_HALO
    _, nxt = _halo_maps(tm, hb, T)
    K = FFN_CONV_W
    te = tm + hb
    rows = _tile(hb, ROW_CHUNK_BWD)
    u4, w5 = u.reshape(2, 4, T, n), wt.reshape(2, 4, n, D)
    dw4 = dw.reshape(2, 4, dw.shape[1], K, n)
    tiles = _col_tiles(n, FFN_COL_TILE)

    def body(dy_ref, dyn_ref, wo_ref, u_ref, ag_ref, agn_ref, cw_ref, wt_ref, x_ref, g_ref, dr_ref,
             du_ref, ddw_ref, ddb_ref, dx_ref, dg_ref,
             dzf_ref, da_ref, acc_ref):
        i, j = pl.program_id(0), pl.program_id(1)
        last = (i % tps) == tps - 1

        @pl.when((i == 0) & (j == 0))
        def _():
            dg_ref[...] = jnp.zeros_like(dg_ref)

        @pl.when(i == 0)
        def _():
            ddw_ref[j] = jnp.zeros((2, K, n), F32)
            ddb_ref[j] = jnp.zeros((2, 1, n), F32)

        dyt, dyn = dy_ref[...], dyn_ref[...]
        for ci, (c0, wc) in enumerate(tiles):
            cols = pl.ds(c0, wc)
            dzf_ref[pl.ds(0, tm), cols] = _dot_nt(dyt, wo_ref[cols, :])
            dzf_ref[pl.ds(tm, hb), cols] = jnp.where(last, 0.0, _dot_nt(dyn, wo_ref[cols, :]))
            for r0 in range(0, te, rows):
                dzc = dzf_ref[pl.ds(r0, rows), cols]
                for half in range(2):
                    src = ag_ref[half, pl.ds(r0, rows), cols] if r0 < tm else agn_ref[half, pl.ds(r0 - tm, rows), cols]
                    da_ref[half, 0, pl.ds(r0, rows), cols] = dzc * src.astype(F32)
            for half in range(2):
                for s in range(1, K):
                    da_ref[half, s, pl.ds(0, tm), cols] = da_ref[half, 0, pl.ds(s, tm), cols]
                tap_acc = [jnp.zeros((rows, wc), F32) for _ in range(K)]
                bias_acc = jnp.zeros((rows, wc), F32)
                for r0 in range(0, tm, rows):
                    xr = u_ref[half, pl.ds(r0, rows), cols].astype(F32)
                    acc = None
                    for k in range(K):
                        d = da_ref[half, K - 1 - k, pl.ds(r0, rows), cols]
                        term = cw_ref[half, pl.ds(k, 1), cols] * d
                        acc = term if acc is None else acc + term
                        tap_acc[k] = tap_acc[k] + d * xr
                        if k == K - 1:
                            bias_acc = bias_acc + d
                    du_ref[half, pl.ds(r0, rows), cols] = acc.astype(du_ref.dtype)
                for k in range(K):
                    ddw_ref[j, half, pl.ds(k, 1), cols] += jnp.sum(tap_acc[k], axis=0, keepdims=True)
                ddb_ref[j, half, :, cols] += jnp.sum(bias_acc, axis=0, keepdims=True)
            part = _dot(du_ref[0, :, cols], wt_ref[0, cols, :]) + _dot(du_ref[1, :, cols], wt_ref[1, cols, :])
            if ci == 0:
                acc_ref[...] = part + jnp.where(j == 0, 0.0, acc_ref[...])
            else:
                acc_ref[...] += part
        @pl.when(j == 3)
        def _():
            x = x_ref[...]
            dh = acc_ref[...]
            r = lax.rsqrt(jnp.mean(x * x, axis=-1, keepdims=True) + EPS)
            xh = x * r
            dxh = dh * g_ref[...]
            dx_ref[...] = dr_ref[...] + r * (dxh - xh * jnp.mean(dxh * xh, axis=-1, keepdims=True))
            dg_ref[...] += jnp.sum(dh * xh, axis=0, keepdims=True)

    f32 = lambda *shape: pltpu.VMEM(shape, F32)
    du, ddw, ddb, dx, dg = _call(
        body, name=name, grid=(T // tm, 4),
        in_specs=[pl.BlockSpec((None, tm, D), lambda i, j: (0, i, 0)),
                  pl.BlockSpec((None, hb, D), lambda i, j: (0, nxt(i), 0)),
                  pl.BlockSpec((None, None, n, D), lambda i, j: (0, j, 0, 0)),
                  pl.BlockSpec((2, None, tm, n), lambda i, j: (0, j, i, 0)),
                  pl.BlockSpec((2, None, tm, n), lambda i, j: (0, j, i, 0)),
                  pl.BlockSpec((2, None, hb, n), lambda i, j: (0, j, nxt(i), 0)),
                  pl.BlockSpec((2, None, None, K, n), lambda i, j: (0, j, layer, 0, 0)),
                  pl.BlockSpec((2, None, n, D), lambda i, j: (0, j, 0, 0)),
                  pl.BlockSpec((tm, D), lambda i, j: (i, 0)),
                  pl.BlockSpec((1, D), lambda i, j: (0, 0)),
                  pl.BlockSpec((tm, D), lambda i, j: (i, 0))],
        out_specs=[pl.BlockSpec((2, None, tm, n), lambda i, j: (0, j, i, 0)),
                   pl.BlockSpec((4, 2, K, n), lambda i, j: (0, 0, 0, 0)),
                   pl.BlockSpec((4, 2, 1, n), lambda i, j: (0, 0, 0, 0)),
                   pl.BlockSpec((tm, D), lambda i, j: (i, 0)),
                   pl.BlockSpec((1, D), lambda i, j: (0, 0))],
        out_shape=[_sds((2, 4, T, n), MM), _sds((4, 2, K, n), F32), _sds((4, 2, 1, n), F32), _sds((T, D), F32),
                   _sds((1, D), F32)],
        scratch=[f32(te, n), f32(2, K, te, n), f32(tm, D)],
    )(dy, dy, wout, u4, ag, ag, dw4, w5, x_in, gain.reshape(1, D), dx_res)
    return du.reshape(8, T, n), jnp.swapaxes(ddw, 0, 1), jnp.swapaxes(ddb, 0, 1), dx, dg


def ln_silu_bwd(name, ds, c, g, b):
    T, D = c.shape
    tm = _tile(T, 512)

    def body(ds_ref, c_ref, g_ref, b_ref, dc_ref, dg_ref, db_ref):
        @pl.when(pl.program_id(0) == 0)
        def _():
            dg_ref[...] = jnp.zeros_like(dg_ref)
            db_ref[...] = jnp.zeros_like(db_ref)
        cf = c_ref[...]
        mu = jnp.mean(cf, axis=-1, keepdims=True)
        xc = cf - mu
        r = lax.rsqrt(jnp.mean(xc * xc, axis=-1, keepdims=True) + EPS)
        xh = xc * r
        t = xh * g_ref[...] + b_ref[...]
        sg = _sigmoid(t)
        dt = ds_ref[...] * (sg * (1.0 + t * (1.0 - sg)))
        dg_ref[...] += jnp.sum(dt * xh, axis=0, keepdims=True)
        db_ref[...] += jnp.sum(dt, axis=0, keepdims=True)
        dxh = dt * g_ref[...]
        dc_ref[...] = r * (dxh - jnp.mean(dxh, axis=-1, keepdims=True)
                           - xh * jnp.mean(dxh * xh, axis=-1, keepdims=True))

    vec = pl.BlockSpec((1, D), lambda i: (0, 0))
    return _call(body, name=name, grid=(T // tm,),
                 in_specs=[pl.BlockSpec((None, tm, D), lambda i: (0, i, 0)), pl.BlockSpec((tm, D), lambda i: (i, 0)),
                           vec, vec],
                 out_specs=[pl.BlockSpec((tm, D), lambda i: (i, 0)), vec, vec],
                 out_shape=[_sds((T, D), F32), _sds((1, D), F32), _sds((1, D), F32)])(
                     ds, c, g.reshape(1, D), b.reshape(1, D))


def cm_glu_conv_bwd(name, u, dc, dw, layer, seq):
    _, T, n = u.shape
    ct = dw.shape[-1]
    per = n // ct
    nct = 4 * per
    tm = _tile(seq, 512)
    tps = seq // tm
    hb = CONV_HALO
    prev, nxt = _halo_maps(tm, hb, T)
    u4 = u.reshape(2, 4, T, n)
    K = CONV_W

    rows = _tile(tm, 2 * ROW_CHUNK_FWD)
    wrows = 8

    def body(u_ref, up_ref, dc_ref, dcn_ref, w_ref, du_ref, ddw_ref, ddb_ref, dbi_ref, zp_ref, zd_ref):
        i = pl.program_id(1)
        first = (i % tps) == 0
        last = (i % tps) == tps - 1

        @pl.when(i == 0)
        def _():
            ddw_ref[...] = jnp.zeros_like(ddw_ref)
            ddb_ref[...] = jnp.zeros_like(ddb_ref)
            dbi_ref[...] = jnp.zeros_like(dbi_ref)

        uh = up_ref[...].astype(F32)
        zp_ref[0, pl.ds(hb, tm), :] = u_ref[0].astype(F32) * _sigmoid(u_ref[1].astype(F32))
        zp_ref[0, pl.ds(0, hb), :] = jnp.where(first, 0.0, uh[0] * _sigmoid(uh[1]))
        zd_ref[0, pl.ds(0, tm), :] = dc_ref[...]
        zd_ref[0, pl.ds(tm, hb), :] = jnp.where(last, 0.0, dcn_ref[...])
        _stage_shifts(zp_ref, hb + tm, back=True)
        _stage_shifts(zd_ref, tm + hb, back=False)
        for r0 in range(0, tm, rows):
            dp = None
            for j in range(K):
                term = w_ref[pl.ds(K - 1 - j, 1), :] * zd_ref[j % 8, pl.ds(r0 + (j - j % 8), rows), :]
                dp = term if dp is None else dp + term
            v = u_ref[0, pl.ds(r0, rows), :].astype(F32)
            sg = _sigmoid(u_ref[1, pl.ds(r0, rows), :].astype(F32))
            dv = dp * sg
            dg = dp * v * sg * (1.0 - sg)
            du_ref[0, pl.ds(r0, rows), :] = dv.astype(du_ref.dtype)
            du_ref[1, pl.ds(r0, rows), :] = dg.astype(du_ref.dtype)
            dbi_ref[0] += jnp.sum(dv, axis=0, keepdims=True)
            dbi_ref[1] += jnp.sum(dg, axis=0, keepdims=True)
        tap_acc = [jnp.zeros((wrows, ct), F32) for _ in range(K)]
        bias_acc = jnp.zeros((wrows, ct), F32)
        for r0 in range(0, tm, wrows):
            d = zd_ref[0, pl.ds(r0, wrows), :]
            for j in range(K):
                tap_acc[j] = tap_acc[j] + d * zp_ref[j % 8, pl.ds(hb + r0 - (j - j % 8), wrows), :]
            bias_acc = bias_acc + d
        for j in range(K):
            ddw_ref[pl.ds(K - 1 - j, 1), :] += jnp.sum(tap_acc[j], axis=0, keepdims=True)
        ddb_ref[...] += jnp.sum(bias_acc, axis=0, keepdims=True)

    du, ddw, ddb, dbi = _call(
        body, name=name, grid=(nct, T // tm),
        in_specs=[pl.BlockSpec((2, None, tm, ct), lambda c, i: (0, c // per, i, c % per)),
                  pl.BlockSpec((2, None, hb, ct), lambda c, i: (0, c // per, prev(i), c % per)),
                  pl.BlockSpec((tm, ct), lambda c, i: (i, c)),
                  pl.BlockSpec((hb, ct), lambda c, i: (nxt(i), c)),
                  pl.BlockSpec((None, None, K, ct), lambda c, i: (c, layer, 0, 0))],
        out_specs=[pl.BlockSpec((2, None, tm, ct), lambda c, i: (0, c // per, i, c % per)),
                   pl.BlockSpec((None, K, ct), lambda c, i: (c, 0, 0)),
                   pl.BlockSpec((None, 1, ct), lambda c, i: (c, 0, 0)),
                   pl.BlockSpec((2, None, 1, ct), lambda c, i: (0, c // per, 0, c % per))],
        out_shape=[_sds((2, 4, T, n), MM), _sds((nct, K, ct), F32), _sds((nct, 1, ct), F32), _sds((2, 4, 1, n), F32)],
        scratch=[pltpu.VMEM((8, tm + hb, ct), F32), pltpu.VMEM((8, tm + hb, ct), F32)])(u4, u4, dc, dc, dw)
    return du.reshape(8, T, n), ddw, ddb, dbi


def attn_bwd(name, q, kv, dm, merged, lse, dkv_prev, nb, seq):
    _, T, qn = q.shape
    kn = kv.shape[2]
    dh = qn * N_DEV // (N_GROUPS * N_HEADS)
    scale = 1.0 / (dh ** 0.5)
    qs, ks, vs = _head_specs(seq, dh, qn // dh, kn // dh)
    has_prev = dkv_prev is not None
    n_in = 6 + (1 if has_prev else 0)

    def body(*refs):
        q_ref, k_ref, v_ref, dm_ref, mg_ref, l_ref = refs[:6]
        pkv_ref = refs[6] if has_prev else None
        dq_ref, dkv_ref = refs[n_in:n_in + 2]
        qf, kf, vf, dqf, dkf, dvf, dlt = refs[n_in + 2:]
        pad = _att_pad()

        @pl.when(pl.program_id(2) == 0)
        def _():
            dlt[...] = jnp.broadcast_to(
                jnp.sum(dm_ref[...] * mg_ref[...].astype(F32), axis=-1, keepdims=True), (seq, dh))

        qf[...] = q_ref[...].astype(F32)
        for t_ref, s_ref in ((k_ref, kf), (v_ref, vf)):
            s_ref[pl.ds(0, pad), :] = jnp.zeros((pad, dh), F32)
            s_ref[pl.ds(pad, seq), :] = t_ref[...].astype(F32)
        dkf[...] = jnp.zeros_like(dkf)
        dvf[...] = jnp.zeros_like(dvf)
        for g in range(N_GROUPS):
            d = DILATIONS[g]
            nblk = seq // d // ATT_BLOCK

            def blocks(it, carry, d=d, nblk=nblk):
                first = it * ATT_BATCH
                rows = [_block_rows(first + b, d, nblk) for b in range(ATT_BATCH)]
                qb = jnp.stack([qf[rq, :] for rq, _ in rows]).astype(MM)
                dmb = jnp.stack([dm_ref[rq, :] for rq, _ in rows]).astype(MM)
                lse = jnp.stack([l_ref[rq, :][:, :1] for rq, _ in rows])
                delta = jnp.stack([dlt[rq, :][:, :1] for rq, _ in rows])
                kb = jnp.stack([kf[rk, :] for _, rk in rows]).astype(MM)
                vb = jnp.stack([vf[rk, :] for _, rk in rows]).astype(MM)
                s = jnp.where(_band_mask(first, nblk), _bdot(qb, kb, 2, 2) * scale, NEG)
                p = jnp.exp(s - lse)
                dsc = (p * (_bdot(dmb, vb, 2, 2) - delta) * scale).astype(MM)
                dv = _bdot(p.astype(MM), dmb, 1, 1)
                dk = _bdot(dsc, qb, 1, 1)
                dq = _bdot(dsc, kb, 2, 1)
                for b, (rq, rk) in enumerate(rows):
                    dqf[rq, :] = dq[b]
                    dkf[rk, :] += dk[b]
                    dvf[rk, :] += dv[b]
                return carry

            @pl.when(pl.program_id(2) == g)
            def _(blocks=blocks, d=d, nblk=nblk):
                lax.fori_loop(0, d * nblk // ATT_BATCH, blocks, 0)

        dq_ref[...] = dqf[...].astype(dq_ref.dtype)
        dk, dv = dkf[pl.ds(pad, seq), :], dvf[pl.ds(pad, seq), :]
        if has_prev:
            dk, dv = dk + pkv_ref[0].astype(F32), dv + pkv_ref[1].astype(F32)
        dkv_ref[0] = dk.astype(dkv_ref.dtype)
        dkv_ref[1] = dv.astype(dkv_ref.dtype)

    per = kn // dh

    def both(b, h, g):
        f = g * N_HEADS + h
        return (0, f // per, b, f % per)

    kv_spec = pl.BlockSpec((2, None, seq, dh), both)
    full = pl.BlockSpec((None, seq, dh), lambda b, h, g: (0, b, h))
    in_specs = [qs, ks, vs, full, full, pl.BlockSpec((seq, dh), lambda b, h, g: (b, h))]
    args = [q, kv, kv, dm, merged, lse]
    if has_prev:
        in_specs.append(kv_spec)
        args.append(dkv_prev.reshape(2, N_DEV // 2, T, kn))
    short, padded = pltpu.VMEM((seq, dh), F32), pltpu.VMEM((_att_pad() + seq, dh), F32)
    dq, dkv = _call(body, name=name, grid=(nb, N_HEADS, N_GROUPS), in_specs=in_specs, out_specs=[qs, kv_spec],
                    out_shape=[_sds(q.shape, MM), _sds((2, N_DEV // 2, T, kn), MM)],
                    scratch=[short, padded, padded, short, padded, padded, short])(*args)
    return dq, dkv.reshape(N_DEV, T, kn)


def _adamw_math(w, g, m, v):
    m = ADAM_B1 * m + (1.0 - ADAM_B1) * g
    v = ADAM_B2 * v + (1.0 - ADAM_B2) * (g * g)
    m_hat = m / (1.0 - ADAM_B1 ** ADAM_STEP)
    v_hat = v / (1.0 - ADAM_B2 ** ADAM_STEP)
    delta = -ADAM_LR * (m_hat / (jnp.sqrt(v_hat) + ADAM_EPS) + ADAM_WD * w)
    return delta, m, v


def adamw_sum(name, w, m, v, parts, after):
    L, R, C = w.shape
    tr = _tile(R, 256)

    def body(*refs):
        w_ref, m_ref, v_ref = refs[:3]
        p_refs = refs[3:3 + L]
        g_ref, d_ref, nm_ref, nv_ref = refs[4 + L:]
        for l in range(L):
            @pl.when(pl.program_id(0) == l)
            def _(p_ref=p_refs[l]):
                g = p_ref[0].astype(F32)
                for k in range(1, N_DEV):
                    g = g + p_ref[k].astype(F32)
                g_ref[...] = g
                d_ref[...], nm_ref[...], nv_ref[...] = _adamw_math(w_ref[...], g, m_ref[...], v_ref[...])

    blk = pl.BlockSpec((None, tr, C), lambda l, i: (l, i, 0))
    part = lambda k: pl.BlockSpec((N_DEV, tr, C), lambda l, i: (0, jnp.where(l == k, i, 0), 0))
    return _call(body, name=name, grid=(L, R // tr),
                 in_specs=[blk, blk, blk] + [part(k) for k in range(L)] + [pl.BlockSpec(memory_space=pl.ANY)],
                 out_specs=[blk] * 4, out_shape=[_sds((L, R, C), F32)] * 4)(w, m, v, *parts, after)


def sum_partials(name, parts):
    _, R, C = parts.shape
    tr = _tile(R, 512)

    def body(p_ref, o_ref):
        g = p_ref[0]
        for k in range(1, N_DEV):
            g = g + p_ref[k]
        o_ref[...] = g

    return _call(body, name=name, grid=(R // tr,),
                 in_specs=[pl.BlockSpec((N_DEV, tr, C), lambda i: (0, i, 0))],
                 out_specs=pl.BlockSpec((tr, C), lambda i: (i, 0)), out_shape=_sds((R, C), F32))(parts)


def adamw_small(name, w, g, m, v):
    R, C = w.shape
    tr = _tile(R, 512)

    def body(w_ref, g_ref, m_ref, v_ref, d_ref, nm_ref, nv_ref):
        d_ref[...], nm_ref[...], nv_ref[...] = _adamw_math(w_ref[...], g_ref[...], m_ref[...], v_ref[...])

    blk = pl.BlockSpec((tr, C), lambda i: (i, 0))
    return _call(body, name=name, grid=(R // tr,), in_specs=[blk] * 4, out_specs=[blk] * 3,
                 out_shape=[_sds((R, C), F32)] * 3)(w, g, m, v)


def _pack(arrays):
    pieces = []
    for a in arrays:
        f = a.reshape(-1).astype(F32)
        pieces.append(jnp.pad(f, (0, (-f.shape[0]) % PACK)))
    return jnp.concatenate(pieces).reshape(-1, 128)


def _unpack(flat, shapes):
    out, off = [], 0
    f = flat.reshape(-1)
    for s in shapes:
        size = 1
        for d in s:
            size *= d
        out.append(f[off:off + size].reshape(s))
        off += size + (-size) % PACK
    return out


def kernel(x, mix_pre_g, mix_post_g, ffn_pre_g, ffn_post_g, cm_w_in, cm_b_in, cm_dw, cm_dw_b, cm_ln_g, cm_ln_b, cm_w_out, cm_b_out, kv_norm_g, w_kv, w_q, w_o, ffn_w_in, ffn_dw, ffn_dw_b, ffn_w_out, loss_target, m_mix_pre_g, m_mix_post_g, m_ffn_pre_g, m_ffn_post_g, m_cm_w_in, m_cm_b_in, m_cm_dw, m_cm_dw_b, m_cm_ln_g, m_cm_ln_b, m_cm_w_out, m_cm_b_out, m_kv_norm_g, m_w_kv, m_w_q, m_w_o, m_ffn_w_in, m_ffn_dw, m_ffn_dw_b, m_ffn_w_out, v_mix_pre_g, v_mix_post_g, v_ffn_pre_g, v_ffn_post_g, v_cm_w_in, v_cm_b_in, v_cm_dw, v_cm_dw_b, v_cm_ln_g, v_cm_ln_b, v_cm_w_out, v_cm_b_out, v_kv_norm_g, v_w_kv, v_w_q, v_w_o, v_ffn_w_in, v_ffn_dw, v_ffn_dw_b, v_ffn_w_out):
    nb, seq, D = x.shape
    T = nb * seq
    me = _my_index()
    n_b = DEPTH - N_A

    nf = ffn_w_in.shape[-1]
    t_ = lambda t: jnp.swapaxes(t, 1, 2)
    fin_t, m_fin_t, v_fin_t = t_(ffn_w_in), t_(m_ffn_w_in), t_(v_ffn_w_in)

    stages = [(part, i) for i in range(DEPTH) for part in ("mix", "ffn")]
    stages.insert(stages.index(("ffn", N_A - 1)) + 1, ("kv", N_A - 1))

    def stage_sources(stage):
        part, i = stage
        if part == "ffn":
            src = {"fin": fin_t[i], "fout": ffn_w_out[i]}
        elif part == "kv":
            src = {"kv": w_kv}
        elif i < N_A:
            src = {"cin": cm_w_in[i], "cout": cm_w_out[i]}
        else:
            src = {"q": w_q[i - N_A], "o": w_o[i - N_A]}
        return {k: t.astype(MM) for k, t in src.items()}

    def begin_gather(stage, after):
        src = stage_sources(stage)
        names, arrays = list(src), list(src.values())
        tag = f"{stage[0]}{stage[1]}"
        lands = place_own(f"gather_own_{tag}", "gather", arrays)
        handle, token = exchange_begin(f"gather_begin_{tag}", "gather", arrays, lands, after)
        return (names, handle), token

    def end_gather(stage, pending, after):
        names, handle = pending
        W = dict(zip(names, exchange_end(f"gather_end_{stage[0]}{stage[1]}", handle, after)))
        for k in W:
            if k in ("cout", "o"):
                W[k] = W[k].reshape(1, 1, D, D)
            elif k == "fout":
                W[k] = W[k].reshape(1, 4, nf, D)
            else:
                W[k] = W[k][:, None]
        return W

    small = [cm_b_in[:, None, :], cm_dw, cm_dw_b[:, None, :], cm_ln_g, cm_ln_b, cm_b_out, ffn_dw]
    Bcin, DWc, DWBc, LNg, LNb, Bcout, DWf = all_gather("gather_small", small, [False] * len(small))
    LNg = jnp.swapaxes(LNg, 0, 1).reshape(N_A, D)
    LNb = jnp.swapaxes(LNb, 0, 1).reshape(N_A, D)
    Bcout = jnp.swapaxes(Bcout, 0, 1).reshape(N_A, D)
    DWBf = ffn_dw_b.reshape(DEPTH, N_DEV, 1, nf)
    zero_bias = jnp.zeros((D,), F32)

    xs = x.reshape(T, D)
    sv = []
    kv = hkv = None
    pending = {}
    pending[stages[0]], _ = begin_gather(stages[0], DWf)
    Ws = {stages[0]: end_gather(stages[0], pending[stages[0]], xs)}
    pending[stages[1]], token = begin_gather(stages[1], next(iter(Ws[stages[0]].values())))
    sv = [{} for _ in range(DEPTH)]
    for idx, stage in enumerate(stages):
        part, i = stage
        L, W = sv[i], Ws[stage]
        gain = {"mix": mix_pre_g[i], "ffn": ffn_pre_g[i], "kv": kv_norm_g}[part]
        following = stages[idx + 1] if idx + 1 < len(stages) else None
        if idx + 2 < len(stages):
            after = token if idx == 0 else next(iter(W.values()))
            pending[stages[idx + 2]], token = begin_gather(stages[idx + 2], after)
        if idx + 2 < len(stages) or idx == 0:
            gain = gain + token[0, 0]
        if part == "mix":
            L["x_in"] = xs
            if i < N_A:
                L["u"], L["h"] = norm_mm(f"cm_in_{i}", xs, gain, W["cin"], 0, Bcin[:, i:i + 1])
                L["c"] = cm_glu_conv(f"cm_conv_{i}", L["u"], DWc, DWBc, i, seq)
                L["s"] = ln_silu(f"cm_ln_{i}", L["c"], LNg[i], LNb[i])
                L["y"], xs = mm_resid_norm(f"cm_out_{i}", L["s"], W["cout"], 0, Bcout[i], xs, mix_post_g[i])
            else:
                L["q"], L["h"] = norm_mm(f"attn_q_{i}", xs, gain, W["q"], 0)
                L["mg"], L["lse"] = attn_fwd(f"attn_{i}", L["q"], kv, nb, seq)
                L["y"], xs = mm_resid_norm(f"attn_out_{i}", L["mg"], W["o"], 0, zero_bias, xs, mix_post_g[i])
            L["x1"] = xs
        elif part == "ffn":
            L["uf"], L["hf"], L["z"], L["ag"] = ffn_in_conv(f"ffn_in_{i}", xs, gain, W["fin"], DWf, DWBf, i, seq)
            L["yf"], xs = mm_resid_norm(f"ffn_out_{i}", L["z"], W["fout"], 0, zero_bias, xs, ffn_post_g[i])
        else:
            kv, hkv = norm_mm("kv_proj", xs, gain, W["kv"], 0)
        if following is not None:
            Ws[following] = end_gather(following, pending[following], kv if part == "kv" else xs)
    dx, loss_part = loss_fwd_bwd("loss", xs, loss_target.reshape(T, D))

    g_mix_pre, g_mix_post, g_ffn_pre, g_ffn_post = [None] * DEPTH, [None] * DEPTH, [None] * DEPTH, [None] * DEPTH
    g_ffn_dw, g_ffn_dwb = [None] * DEPTH, [None] * DEPTH
    g_cbin, g_cdw, g_cdwb, g_lng, g_lnb, g_cbout = ([None] * N_A for _ in range(6))
    g_kvn = dkv = None
    landed = [{} for _ in range(DEPTH)]
    in_flight = token = None
    for stage in reversed(stages):
        part, i = stage
        L, W = sv[i], Ws[stage]
        gain = {"mix": mix_post_g[i], "ffn": ffn_post_g[i], "kv": kv_norm_g}[part]
        if token is not None:
            gain = gain + token[0, 0]
        send = {}
        if part == "kv":
            send["kv"] = mm_tn("kv_wg", hkv[None], dkv)
            dx, g_kvn = mm_nt_norm_bwd("kv_bwd", dkv, W["kv"], 0, sv[i + 1]["x_in"], gain, dx)
        elif part == "ffn":
            dyf, g_ffn_post[i], _ = resid_norm_bwd(f"ffn_post_bwd_{i}", dx, L["yf"], gain)
            send["fout"] = mm_tn(f"ffn_out_wg_{i}", L["z"], dyf).reshape(N_DEV, nf // 2, D)
            duf, ddw, ddwb, dx, g_ffn_pre[i] = ffn_bwd(f"ffn_bwd_{i}", dyf, W["fout"], L["uf"], L["ag"], DWf, W["fin"],
                                                       L["x1"], ffn_pre_g[i], dx, i, seq)
            g_ffn_dw[i], g_ffn_dwb[i] = ddw.reshape(N_DEV, FFN_CONV_W, nf), ddwb.reshape(-1)
            send["fin"] = mm_tn(f"ffn_in_wg_{i}", duf, L["hf"][None])
        else:
            dy, g_mix_post[i], dyb = resid_norm_bwd(f"mix_post_bwd_{i}", dx, L["y"], gain)
            if i >= N_A:
                dm = mm_nt(f"attn_out_bwd_{i}", dy, W["o"], 0, F32)
                send["o"] = mm_tn(f"attn_out_wg_{i}", L["mg"], dy).reshape(N_DEV, D // N_DEV, D)
                dq, dkv = attn_bwd(f"attn_bwd_{i}", L["q"], kv, dm, L["mg"], L["lse"], dkv, nb, seq)
                send["q"] = mm_tn(f"attn_q_wg_{i}", L["h"][None], dq)
                dx, g_mix_pre[i] = mm_nt_norm_bwd(f"attn_q_bwd_{i}", dq, W["q"], 0, L["x_in"], mix_pre_g[i], dx)
            else:
                g_cbout[i] = dyb
                ds = mm_nt(f"cm_out_bwd_{i}", dy, W["cout"], 0, F32)
                send["cout"] = mm_tn(f"cm_out_wg_{i}", L["s"], dy).reshape(N_DEV, D // N_DEV, D)
                ln_gain = LNg[i]
                if stage == stages[0]:
                    early = [send.pop("cout")]
                    lands = place_own("scatter_own_cout0", "scatter", early)
                    early_handle, early_token = exchange_begin("scatter_begin_cout0", "scatter", early, lands, ds)
                    ln_gain = ln_gain + early_token[0, 0]
                dc, g_lng[i], g_lnb[i] = ln_silu_bwd(f"cm_ln_bwd_{i}", ds, L["c"], ln_gain, LNb[i])
                du, g_cdw[i], g_cdwb[i], dbi = cm_glu_conv_bwd(f"cm_conv_bwd_{i}", L["u"], dc, DWc, i, seq)
                g_cbin[i] = dbi.reshape(N_DEV, -1)
                send["cin"] = mm_tn(f"cm_in_wg_{i}", L["h"][None], du)
                dx, g_mix_pre[i] = mm_nt_norm_bwd(f"cm_in_bwd_{i}", du, W["cin"], 0, L["x_in"], mix_pre_g[i], dx)
        if in_flight is not None:
            (p, j), names, handle = in_flight
            landed[j].update(zip(names, exchange_end(f"scatter_end_{p}{j}", handle, dx)))
        names, arrays = list(send), list(send.values())
        lands = place_own(f"scatter_own_{part}{i}", "scatter", arrays)
        handle, token = exchange_begin(f"scatter_begin_{part}{i}", "scatter", arrays, lands, dx)
        in_flight = (stage, names, handle)
    grad_x = dx.reshape(nb, seq, D)

    rep_names = ["mix_pre_g", "mix_post_g", "ffn_pre_g", "ffn_post_g", "kv_norm_g", "ffn_dw_b"]
    rep_parts = [jnp.concatenate(g_mix_pre), jnp.concatenate(g_mix_post), jnp.concatenate(g_ffn_pre),
                 jnp.concatenate(g_ffn_post), g_kvn.reshape(-1), jnp.stack(g_ffn_dwb)]
    rep_w = [mix_pre_g, mix_post_g, ffn_pre_g, ffn_post_g, kv_norm_g, ffn_dw_b]
    rep_m = [m_mix_pre_g, m_mix_post_g, m_ffn_pre_g, m_ffn_post_g, m_kv_norm_g, m_ffn_dw_b]
    rep_v = [v_mix_pre_g, v_mix_post_g, v_ffn_pre_g, v_ffn_post_g, v_kv_norm_g, v_ffn_dw_b]
    sh_names = ["ffn_dw", "cm_b_in", "cm_dw", "cm_dw_b", "cm_ln_g", "cm_ln_b", "cm_b_out"]
    own = lambda per_layer, shard: jnp.stack([p.reshape((N_DEV,) + shard) for p in per_layer], axis=1)
    sh_parts = [own(g_ffn_dw, ffn_dw.shape[1:]), own(g_cbin, cm_b_in.shape[1:]), own(g_cdw, cm_dw.shape[1:]),
                own(g_cdwb, cm_dw_b.shape[1:]), own(g_lng, cm_ln_g.shape[1:]), own(g_lnb, cm_ln_b.shape[1:]),
                own(g_cbout, cm_b_out.shape[1:])]
    sh_w = [ffn_dw, cm_b_in, cm_dw, cm_dw_b, cm_ln_g, cm_ln_b, cm_b_out]
    sh_m = [m_ffn_dw, m_cm_b_in, m_cm_dw, m_cm_dw_b, m_cm_ln_g, m_cm_ln_b, m_cm_b_out]
    sh_v = [v_ffn_dw, v_cm_b_in, v_cm_dw, v_cm_dw_b, v_cm_ln_g, v_cm_ln_b, v_cm_b_out]
    rep_pack = _pack([loss_part] + rep_parts)
    sh_pack = jnp.stack([_pack([p[k] for p in sh_parts]) for k in range(N_DEV)])
    n_rep = rep_pack.shape[0]
    small = []
    for kind, pack in (("gather", rep_pack), ("scatter", sh_pack)):
        lands = place_own(f"{kind}_own_small", kind, [pack])
        handle, token = exchange_begin(f"{kind}_begin_small", kind, [pack], lands, token)
        small.append((kind, handle))

    def big_update(name, w, m, v, key, layers, after):
        as3 = lambda t: t.reshape((-1,) + t.shape[-2:])
        outs = adamw_sum(name, as3(w), as3(m), as3(v), [landed[i][key] for i in layers], after)
        return [t.reshape(w.shape) for t in outs]

    conf, attn = range(N_A), range(N_A, DEPTH)
    upd = {}
    upd["ffn_w_in"] = [t_(t) for t in big_update("adam_ffn_w_in", fin_t, m_fin_t, v_fin_t, "fin", range(DEPTH), token)]
    upd["ffn_w_out"] = big_update("adam_ffn_w_out", ffn_w_out, m_ffn_w_out, v_ffn_w_out, "fout", range(DEPTH), token)
    upd["w_kv"] = big_update("adam_w_kv", w_kv, m_w_kv, v_w_kv, "kv", [N_A - 1], token)
    upd["w_q"] = big_update("adam_w_q", w_q, m_w_q, v_w_q, "q", attn, token)
    upd["w_o"] = big_update("adam_w_o", w_o, m_w_o, v_w_o, "o", attn, upd["w_q"][0])
    (p, j), names, handle = in_flight
    landed[j].update(zip(names, exchange_end(f"scatter_end_{p}{j}", handle, upd["w_o"][0])))
    landed[0]["cout"] = exchange_end("scatter_end_cout0", early_handle, upd["w_o"][0])[0]
    upd["cm_w_in"] = big_update("adam_cm_w_in", cm_w_in, m_cm_w_in, v_cm_w_in, "cin", conf, token)
    upd["cm_w_out"] = big_update("adam_cm_w_out", cm_w_out, m_cm_w_out, v_cm_w_out, "cout", conf, upd["cm_w_in"][0])
    (rep_landed,), (sh_landed,) = (exchange_end(f"{kind}_end_small", handle, upd["cm_w_out"][0])
                                   for kind, handle in small)
    rep_sum = sum_partials("sum_small_rep", rep_landed)
    sh_sum = sum_partials("sum_small_sh", sh_landed)
    rep_shapes = [(1, 1)] + [w.shape for w in rep_w]
    sh_shapes = [w.shape for w in sh_w]
    g_small = jnp.concatenate([rep_sum, sh_sum])
    pad1 = jnp.zeros((1, 1), F32)
    d_s, m_s, v_s = adamw_small("adam_small", jnp.concatenate([_pack([pad1] + rep_w), _pack(sh_w)]), g_small,
                                jnp.concatenate([_pack([pad1] + rep_m), _pack(sh_m)]),
                                jnp.concatenate([_pack([pad1] + rep_v), _pack(sh_v)]))
    split = lambda t: (_unpack(t[:n_rep], rep_shapes), _unpack(t[n_rep:], sh_shapes))
    for (rep_t, sh_t), slot in zip([split(g_small), split(d_s), split(m_s), split(v_s)], range(4)):
        if slot == 0:
            loss = rep_t[0].reshape(())
        for name, t in zip(rep_names, rep_t[1:]):
            upd.setdefault(name, [None] * 4)[slot] = t
        for name, t in zip(sh_names, sh_t):
            upd.setdefault(name, [None] * 4)[slot] = t

    order = ["mix_pre_g", "mix_post_g", "ffn_pre_g", "ffn_post_g", "cm_w_in", "cm_b_in", "cm_dw", "cm_dw_b", "cm_ln_g",
             "cm_ln_b", "cm_w_out", "cm_b_out", "kv_norm_g", "w_kv", "w_q", "w_o", "ffn_w_in", "ffn_dw", "ffn_dw_b",
             "ffn_w_out"]
    return (loss, grad_x, *[upd[n][0] for n in order], *[upd[n][1] for n in order],
            *[upd[n][2] for n in order], *[upd[n][3] for n in order])
```

```python
import functools

import jax
import jax.numpy as jnp
from jax import lax
from jax.experimental import pallas as pl
from jax.experimental.pallas import tpu as pltpu

N_DEV = 8
N_A = 2
DEPTH = 4
N_HEADS = 8
N_GROUPS = 3
DILATIONS = (1, 4, 16)
ATT_BLOCK = 128
ATT_BATCH = 16
CONV_W = 31
FFN_CONV_W = 3
CONV_HALO = 32
FFN_HALO = 16
ROW_CHUNK_FWD = 16
ROW_CHUNK_BWD = 16
GROUP_ALL_MAX_N = 384
FFN_COL_TILE = 1024
EPS = 1e-6
NEG = -1e30
ADAM_LR, ADAM_B1, ADAM_B2, ADAM_EPS, ADAM_WD, ADAM_STEP = 0.001, 0.9, 0.999, 1e-08, 0.01, 10
MM = jnp.bfloat16
F32 = jnp.float32
VMEM_LIMIT_BYTES = 56 * 1024 * 1024
PACK = 1024
MESH_ID = pl.DeviceIdType.MESH

_pallas = pl.pallas_call


def _call(body, *, name, out_shape, grid=(), in_specs=None, out_specs=None, scratch=()):
    return _pallas(body, name=name, out_shape=out_shape, grid=grid, in_specs=in_specs, out_specs=out_specs,
                   scratch_shapes=list(scratch),
                   compiler_params=pltpu.CompilerParams(vmem_limit_bytes=VMEM_LIMIT_BYTES))


def _tile(n, pref):
    if n <= pref:
        return n
    t = pref - pref % 8
    while n % t:
        t -= 8
    assert t > 0, (n, pref)
    return t


def _sds(shape, dtype):
    return jax.ShapeDtypeStruct(tuple(shape), dtype)


def _dot(a, b):
    return jnp.dot(a, b, preferred_element_type=F32)


def _dot_nt(a, b):
    return lax.dot_general(a, b, (((1,), (1,)), ((), ())), preferred_element_type=F32)


def _dot_tn(a, b):
    return lax.dot_general(a, b, (((0,), (0,)), ((), ())), preferred_element_type=F32)


def _sigmoid(x):
    return 0.5 * jnp.tanh(0.5 * x) + 0.5


def _my_index():
    return 4 * lax.axis_index("x") + 2 * lax.axis_index("y") + lax.axis_index("c")


def _exchange(name, arrays, out_shapes, pieces, src_of, dst_of):
    n = len(arrays)
    base = [sum(pieces[:a]) for a in range(n)]
    total = sum(pieces)

    def body(*refs):
        ins, outs = refs[:n], refs[n:2 * n]
        send_sems, recv_sems, local_sems = refs[2 * n:]
        x, y, c = lax.axis_index("x"), lax.axis_index("y"), lax.axis_index("c")
        me = 4 * x + 2 * y + c
        copies = []
        for a in range(n):
            for k, (s, d) in enumerate(zip(src_of(a, ins[a], me), dst_of(a, outs[a], me))):
                cp = pltpu.make_async_copy(s, d, local_sems.at[base[a] + k])
                cp.start()
                copies.append(cp)
        remote = []
        for m in range(1, N_DEV):
            px, py, pc = x ^ (m >> 2), y ^ ((m >> 1) & 1), c ^ (m & 1)
            peer = 4 * px + 2 * py + pc
            for a in range(n):
                for k, (s, d) in enumerate(zip(src_of(a, ins[a], peer), dst_of(a, outs[a], me))):
                    cp = pltpu.make_async_remote_copy(src_ref=s, dst_ref=d, send_sem=send_sems.at[base[a] + k, m - 1],
                                                      recv_sem=recv_sems.at[base[a] + k, m - 1],
                                                      device_id=(px, py, pc), device_id_type=MESH_ID)
                    cp.start()
                    remote.append(cp)
        for cp in copies:
            cp.wait()
        for cp in remote:
            cp.wait_send()
        for m in range(1, N_DEV):
            px, py, pc = x ^ (m >> 2), y ^ ((m >> 1) & 1), c ^ (m & 1)
            peer = 4 * px + 2 * py + pc
            for a in range(n):
                for k, (s, d) in enumerate(zip(src_of(a, ins[a], me), dst_of(a, outs[a], peer))):
                    pltpu.make_async_remote_copy(src_ref=s, dst_ref=d, send_sem=send_sems.at[base[a] + k, m - 1],
                                                 recv_sem=recv_sems.at[base[a] + k, m - 1], device_id=(px, py, pc),
                                                 device_id_type=MESH_ID).wait_recv()

    any_spec = pl.BlockSpec(memory_space=pl.ANY)
    return _call(body, name=name, out_shape=[_sds(s, a.dtype) for s, a in zip(out_shapes, arrays)],
                 in_specs=[any_spec] * n, out_specs=[any_spec] * n,
                 scratch=[pltpu.SemaphoreType.DMA((total, N_DEV - 1)), pltpu.SemaphoreType.DMA((total, N_DEV - 1)),
                          pltpu.SemaphoreType.DMA((total,))])(*arrays)


def all_gather(name, arrays, row_sharded):
    def out_shape(a):
        s = arrays[a].shape
        return (s[0], N_DEV) + s[1:] if row_sharded[a] else (N_DEV,) + s

    def src_of(a, ref, peer):
        if row_sharded[a]:
            return [ref.at[l] for l in range(arrays[a].shape[0])]
        return [ref]

    def dst_of(a, ref, me):
        if row_sharded[a]:
            return [ref.at[l, me] for l in range(arrays[a].shape[0])]
        return [ref.at[me]]

    pieces = [arrays[a].shape[0] if row_sharded[a] else 1 for a in range(len(arrays))]
    return _exchange(name, arrays, [out_shape(a) for a in range(len(arrays))], pieces, src_of, dst_of)


def _src_view(kind, ref, peer):
    return ref if kind == "gather" else ref.at[peer]


def _peers(x, y, c):
    for m in range(1, N_DEV):
        px, py, pc = x ^ (m >> 2), y ^ ((m >> 1) & 1), c ^ (m & 1)
        yield m - 1, (px, py, pc), 4 * px + 2 * py + pc


def place_own(name, kind, srcs):
    n = len(srcs)
    shapes = [(N_DEV,) + s.shape if kind == "gather" else s.shape for s in srcs]
    steps = 2 if all(s.shape[-2] % 32 == 0 for s in srcs) else 1

    def body(*refs):
        for a in range(n):
            refs[n + a][...] = refs[a][...]

    def spec(shape, own_block):
        R, C = shape[-2:]
        tr = R // steps
        if own_block:
            return pl.BlockSpec((None, tr, C), lambda i: (_my_index(), i, 0))
        return pl.BlockSpec((tr, C), lambda i: (i, 0))

    return _call(body, name=name, grid=(steps,), out_shape=[_sds(s, a.dtype) for s, a in zip(shapes, srcs)],
                 in_specs=[spec(s.shape, kind == "scatter") for s in srcs],
                 out_specs=[spec(s, True) for s in shapes])(*srcs)


_HBM_SPEC = pl.BlockSpec(memory_space=pltpu.HBM)
_SEM_SPEC = pl.BlockSpec(memory_space=pltpu.SEMAPHORE)
_DATAFLOW = pltpu.SideEffectType.DATAFLOW_SIDE_EFFECTING


def _remote(kind, src, land, send_sems, recv_sems, a, slot, frm, to_id, at):
    return pltpu.make_async_remote_copy(src_ref=_src_view(kind, src, frm), dst_ref=land.at[at],
                                        send_sem=send_sems.at[a * (N_DEV - 1) + slot],
                                        recv_sem=recv_sems.at[a * (N_DEV - 1) + slot],
                                        device_id=to_id, device_id_type=MESH_ID)


def exchange_begin(name, kind, srcs, lands, after):
    n = len(srcs)

    def body(*refs):
        ins, lnd = refs[:n], refs[n:2 * n]
        send_sems, recv_sems = refs[2 * n + 1], refs[2 * n + 2]
        token = refs[-1]
        x, y, c = lax.axis_index("x"), lax.axis_index("y"), lax.axis_index("c")
        me = 4 * x + 2 * y + c
        for slot, peer_id, peer in _peers(x, y, c):
            for a in range(n):
                _remote(kind, ins[a], lnd[a], send_sems, recv_sems, a, slot, peer, peer_id, me).start()
        token[...] = jnp.zeros_like(token)

    hbm = lambda t: pltpu.HBM(t.shape, t.dtype)
    outs = _pallas(
        body, name=name,
        out_shape=(pltpu.SemaphoreType.DMA((n * (N_DEV - 1),)), pltpu.SemaphoreType.DMA((n * (N_DEV - 1),)),
                   *[hbm(t) for t in srcs], *[hbm(t) for t in lands], _sds((8, 128), F32)),
        in_specs=[_HBM_SPEC] * (2 * n) + [pl.BlockSpec(memory_space=pl.ANY)],
        out_specs=(_SEM_SPEC, _SEM_SPEC, *[_HBM_SPEC] * (2 * n), pl.BlockSpec(memory_space=pltpu.VMEM)),
        input_output_aliases={i: 2 + i for i in range(2 * n)},
        compiler_params=pltpu.CompilerParams(has_side_effects=_DATAFLOW),
    )(*[pltpu.with_memory_space_constraint(t, pltpu.HBM) for t in list(srcs) + list(lands)], after)
    return (kind, outs[0], outs[1], list(outs[2:2 + n]), list(outs[2 + n:2 + 2 * n])), outs[-1]


def exchange_end(name, handle, after):
    kind, send_sems, recv_sems, srcs, lands = handle
    n = len(srcs)

    def body(*refs):
        ins, lnd = refs[:n], refs[n:2 * n]
        s_sems, r_sems = refs[2 * n], refs[2 * n + 1]
        x, y, c = lax.axis_index("x"), lax.axis_index("y"), lax.axis_index("c")
        me = 4 * x + 2 * y + c
        for slot, peer_id, peer in _peers(x, y, c):
            for a in range(n):
                _remote(kind, ins[a], lnd[a], s_sems, r_sems, a, slot, peer, peer_id, me).wait_send()
        for slot, peer_id, peer in _peers(x, y, c):
            for a in range(n):
                _remote(kind, ins[a], lnd[a], s_sems, r_sems, a, slot, me, peer_id, peer).wait_recv()

    hbm = lambda t: pltpu.HBM(t.shape, t.dtype)
    outs = _pallas(
        body, name=name, out_shape=tuple(hbm(t) for t in srcs + lands),
        in_specs=[_HBM_SPEC] * (2 * n) + [_SEM_SPEC, _SEM_SPEC, pl.BlockSpec(memory_space=pl.ANY)],
        out_specs=tuple([_HBM_SPEC] * (2 * n)), input_output_aliases={i: i for i in range(2 * n)},
        compiler_params=pltpu.CompilerParams(has_side_effects=_DATAFLOW),
    )(*srcs, *lands, send_sems, recv_sems, after)
    return list(outs[n:])


def norm_mm(name, x, gain, w, layer, bias=None):
    T, D = x.shape
    nsh, _, _, n = w.shape
    tm = _tile(T, 1024)
    grp = nsh if n <= GROUP_ALL_MAX_N else nsh // 2
    steps = nsh // grp

    def body(*refs):
        if bias is None:
            x_ref, g_ref, w_ref, u_ref, h_ref = refs
        else:
            x_ref, g_ref, w_ref, b_ref, u_ref, h_ref = refs

        @pl.when(pl.program_id(1) == 0)
        def _():
            xf = x_ref[...]
            r = lax.rsqrt(jnp.mean(xf * xf, axis=-1, keepdims=True) + EPS)
            h_ref[...] = (xf * r * g_ref[...]).astype(h_ref.dtype)

        h = h_ref[...]
        for k in range(grp):
            acc = _dot(h, w_ref[k])
            if bias is not None:
                acc = acc + b_ref[k]
            u_ref[k] = acc.astype(u_ref.dtype)

    in_specs = [pl.BlockSpec((tm, D), lambda i, j: (i, 0)),
                pl.BlockSpec((1, D), lambda i, j: (0, 0)),
                pl.BlockSpec((grp, None, D, n), lambda i, j: (j, layer, 0, 0))]
    args = [x, gain.reshape(1, D), w]
    if bias is not None:
        in_specs.append(pl.BlockSpec((grp, None, 1, n), lambda i, j: (j, layer, 0, 0)))
        args.append(bias)
    return _call(body, name=name, grid=(T // tm, steps), in_specs=in_specs,
                 out_specs=[pl.BlockSpec((grp, tm, n), lambda i, j: (j, i, 0)),
                            pl.BlockSpec((tm, D), lambda i, j: (i, 0))],
                 out_shape=[_sds((nsh, T, n), MM), _sds((T, D), MM)])(*args)


def mm_resid_norm(name, a, w, layer, bias, x, gain):
    nk, T, kk = a.shape
    D = x.shape[1]
    tm = _tile(T, 512)

    def body(a_ref, w_ref, b_ref, x_ref, g_ref, y_ref, xn_ref):
        y = _dot(a_ref[0], w_ref[0])
        for q in range(1, nk):
            y = y + _dot(a_ref[q], w_ref[q])
        y = y + b_ref[...]
        y_ref[...] = y
        r = lax.rsqrt(jnp.mean(y * y, axis=-1, keepdims=True) + EPS)
        xn_ref[...] = x_ref[...] + y * r * g_ref[...]

    return _call(body, name=name, grid=(T // tm,),
                 in_specs=[pl.BlockSpec((nk, tm, kk), lambda i: (0, i, 0)),
                           pl.BlockSpec((None, nk, kk, D), lambda i: (layer, 0, 0, 0)),
                           pl.BlockSpec((1, D), lambda i: (0, 0)),
                           pl.BlockSpec((tm, D), lambda i: (i, 0)),
                           pl.BlockSpec((1, D), lambda i: (0, 0))],
                 out_specs=[pl.BlockSpec((tm, D), lambda i: (i, 0))] * 2,
                 out_shape=[_sds((T, D), F32)] * 2)(a, w, bias.reshape(1, D), x, gain.reshape(1, D))


def _halo_maps(tm, hb, T):
    per = tm // hb
    last = T // hb - 1
    return (lambda i: jnp.maximum(i * per - 1, 0)), (lambda i: jnp.minimum((i + 1) * per, last))


def cm_glu_conv(name, u, dw, dwb, layer, seq):
    _, T, n = u.shape
    ct = dw.shape[-1]
    per = n // ct
    nct = 4 * per
    tm = _tile(seq, 512)
    tps = seq // tm
    hb = CONV_HALO
    prev, _ = _halo_maps(tm, hb, T)
    u4 = u.reshape(2, 4, T, n)

    rows = _tile(tm, 2 * ROW_CHUNK_FWD)

    def body(u_ref, uh_ref, w_ref, b_ref, o_ref, z_ref):
        first = (pl.program_id(0) % tps) == 0
        um = u_ref[...].astype(F32)
        uh = uh_ref[...].astype(F32)
        z_ref[0, pl.ds(hb, tm), :] = um[0] * _sigmoid(um[1])
        z_ref[0, pl.ds(0, hb), :] = jnp.where(first, 0.0, uh[0] * _sigmoid(uh[1]))
        _stage_shifts(z_ref, hb + tm, back=True)
        for r0 in range(0, tm, rows):
            acc = b_ref[...]
            for j in range(CONV_W):
                acc = acc + w_ref[pl.ds(CONV_W - 1 - j, 1), :] * z_ref[j % 8, pl.ds(hb + r0 - (j - j % 8), rows), :]
            o_ref[pl.ds(r0, rows), :] = acc

    return _call(body, name=name, grid=(T // tm, nct),
                 in_specs=[pl.BlockSpec((2, None, tm, ct), lambda i, c: (0, c // per, i, c % per)),
                           pl.BlockSpec((2, None, hb, ct), lambda i, c: (0, c // per, prev(i), c % per)),
                           pl.BlockSpec((None, None, CONV_W, ct), lambda i, c: (c, layer, 0, 0)),
                           pl.BlockSpec((None, None, 1, ct), lambda i, c: (c, layer, 0, 0))],
                 out_specs=pl.BlockSpec((tm, ct), lambda i, c: (i, c)),
                 out_shape=_sds((T, nct * ct), F32),
                 scratch=[pltpu.VMEM((8, tm + hb, ct), F32)])(u4, u4, dw, dwb)


def _stage_shifts(z_ref, n_rows, back):
    for r in range(1, 8):
        if back:
            z_ref[r, pl.ds(8, n_rows - 8), :] = z_ref[0, pl.ds(8 - r, n_rows - 8), :]
        else:
            z_ref[r, pl.ds(0, n_rows - 8), :] = z_ref[0, pl.ds(r, n_rows - 8), :]


def ln_silu(name, c, g, b):
    T, D = c.shape
    tm = _tile(T, 512)

    def body(c_ref, g_ref, b_ref, s_ref):
        cf = c_ref[...]
        mu = jnp.mean(cf, axis=-1, keepdims=True)
        xc = cf - mu
        r = lax.rsqrt(jnp.mean(xc * xc, axis=-1, keepdims=True) + EPS)
        t = xc * r * g_ref[...] + b_ref[...]
        s_ref[...] = (t * _sigmoid(t)).astype(s_ref.dtype)

    return _call(body, name=name, grid=(T // tm,),
                 in_specs=[pl.BlockSpec((tm, D), lambda i: (i, 0)), pl.BlockSpec((1, D), lambda i: (0, 0)),
                           pl.BlockSpec((1, D), lambda i: (0, 0))],
                 out_specs=pl.BlockSpec((None, tm, D), lambda i: (0, i, 0)),
                 out_shape=_sds((1, T, D), MM))(c, g.reshape(1, D), b.reshape(1, D))


def _col_tiles(n, width):
    return [(c0, min(width, n - c0)) for c0 in range(0, n, width)]


def ffn_in_conv(name, x, gain, wt, dw, dwb, layer, seq):
    T, D = x.shape
    n = wt.shape[2]
    tm = _tile(seq, 1024)
    tps = seq // tm
    hb = 8
    rows = _tile(tm, ROW_CHUNK_FWD)
    K = FFN_CONV_W
    w5 = wt.reshape(2, 4, n, D)
    dw4 = dw.reshape(2, 4, dw.shape[1], K, n)
    dwb4 = dwb.reshape(dwb.shape[0], 2, 4, 1, n)

    def body(x_ref, g_ref, w_ref, cw_ref, cb_ref, u_ref, h_ref, z_ref, ag_ref, pad_ref, carry_ref):
        i, j = pl.program_id(0), pl.program_id(1)
        first = (i % tps) == 0

        @pl.when(j == 0)
        def _():
            xf = x_ref[...]
            r = lax.rsqrt(jnp.mean(xf * xf, axis=-1, keepdims=True) + EPS)
            h_ref[...] = (xf * r * g_ref[...]).astype(h_ref.dtype)

        h = h_ref[...]
        for c0, wc in _col_tiles(n, FFN_COL_TILE):
            cols = pl.ds(c0, wc)
            for half in range(2):
                res = _dot_nt(h, w_ref[half, cols, :])
                u_ref[half, :, cols] = res.astype(u_ref.dtype)
                pad_ref[half, pl.ds(hb, tm), cols] = res
                pad_ref[half, pl.ds(0, hb), cols] = jnp.where(first, 0.0, carry_ref[j, half, :, cols])
            for r0 in range(0, tm, rows):
                conv = []
                for half in range(2):
                    acc = cb_ref[half, :, cols] + cw_ref[half, pl.ds(K - 1, 1), cols] * pad_ref[
                        half, pl.ds(hb + r0, rows), cols]
                    for k in range(K - 1):
                        acc = acc + cw_ref[half, pl.ds(k, 1), cols] * pad_ref[
                            half, pl.ds(hb + r0 - (K - 1) + k, rows), cols]
                    conv.append(acc)
                a, g = conv
                sg = _sigmoid(g)
                silu = g * sg
                z_ref[pl.ds(r0, rows), cols] = (silu * a).astype(z_ref.dtype)
                ag_ref[0, pl.ds(r0, rows), cols] = silu.astype(ag_ref.dtype)
                ag_ref[1, pl.ds(r0, rows), cols] = (a * (sg * (1.0 + g * (1.0 - sg)))).astype(ag_ref.dtype)
            for half in range(2):
                carry_ref[j, half, :, cols] = pad_ref[half, pl.ds(tm, hb), cols]

    u, h, z, ag = _call(
        body, name=name, grid=(T // tm, 4),
        in_specs=[pl.BlockSpec((tm, D), lambda i, j: (i, 0)),
                  pl.BlockSpec((1, D), lambda i, j: (0, 0)),
                  pl.BlockSpec((2, None, n, D), lambda i, j: (0, j, 0, 0)),
                  pl.BlockSpec((2, None, None, K, n), lambda i, j: (0, j, layer, 0, 0)),
                  pl.BlockSpec((None, 2, None, 1, n), lambda i, j: (layer, 0, j, 0, 0))],
        out_specs=[pl.BlockSpec((2, None, tm, n), lambda i, j: (0, j, i, 0)),
                   pl.BlockSpec((tm, D), lambda i, j: (i, 0)),
                   pl.BlockSpec((None, tm, n), lambda i, j: (j, i, 0)),
                   pl.BlockSpec((2, None, tm, n), lambda i, j: (0, j, i, 0))],
        out_shape=[_sds((2, 4, T, n), MM), _sds((T, D), MM), _sds((4, T, n), MM), _sds((2, 4, T, n), MM)],
        scratch=[pltpu.VMEM((2, tm + hb, n), F32), pltpu.VMEM((4, 2, hb, n), F32)])(
            x, gain.reshape(1, D), w5, dw4, dwb4)
    return u.reshape(8, T, n), h, z, ag


def _head_specs(seq, dh, q_heads, kv_heads):
    def spec(per, base):
        def imap(b, h, g):
            f = base + g * N_HEADS + h
            return (f // per, b, f % per)
        return pl.BlockSpec((None, seq, dh), imap)
    return spec(q_heads, 0), spec(kv_heads, 0), spec(kv_heads, N_GROUPS * N_HEADS)


def _rows(start, d, blocks=1):
    size = blocks * ATT_BLOCK
    return pl.ds(start, size, stride=d) if d > 1 else pl.ds(start, size)


def _att_pad():
    return max(ATT_BLOCK * d for d in DILATIONS[:-1])


def _band_mask(first, nblk):
    keys = ATT_BLOCK if nblk == 1 else 2 * ATT_BLOCK
    shape = (ATT_BATCH, ATT_BLOCK, keys)
    qi = lax.broadcasted_iota(jnp.int32, shape, 1)
    kj = lax.broadcasted_iota(jnp.int32, shape, 2)
    if nblk == 1:
        return kj <= qi
    n = (first + lax.broadcasted_iota(jnp.int32, shape, 0)) % nblk
    return (kj >= qi) & (kj <= qi + ATT_BLOCK) & ((n > 0) | (kj >= ATT_BLOCK))


def _block_rows(idx, d, nblk):
    r, n = idx // nblk, idx % nblk
    rq = _rows(r + d * ATT_BLOCK * n, d)
    if nblk == 1:
        return rq, _rows(_att_pad() + r + d * ATT_BLOCK * n, d)
    return rq, _rows(_att_pad() + r + d * ATT_BLOCK * (n - 1), d, blocks=2)


def _bdot(a, b, ca, cb):
    return lax.dot_general(a, b, (((ca,), (cb,)), ((0,), (0,))), preferred_element_type=F32)


def attn_fwd(name, q, kv, nb, seq):
    _, T, qn = q.shape
    dh = qn * N_DEV // (N_GROUPS * N_HEADS)
    scale = 1.0 / (dh ** 0.5)
    qs, ks, vs = _head_specs(seq, dh, qn // dh, kv.shape[2] // dh)

    def body(q_ref, k_ref, v_ref, m_ref, l_ref, qf, kf, vf, *branch):
        og, lg = branch[:N_GROUPS], branch[N_GROUPS:]
        pad = _att_pad()
        qf[...] = q_ref[...].astype(F32)
        for t_ref, s_ref in ((k_ref, kf), (v_ref, vf)):
            s_ref[pl.ds(0, pad), :] = jnp.zeros((pad, dh), F32)
            s_ref[pl.ds(pad, seq), :] = t_ref[...].astype(F32)
        for g in range(N_GROUPS):
            d = DILATIONS[g]
            nblk = seq // d // ATT_BLOCK

            def blocks(it, carry, d=d, nblk=nblk, g=g):
                first = it * ATT_BATCH
                rows = [_block_rows(first + b, d, nblk) for b in range(ATT_BATCH)]
                qb = jnp.stack([qf[rq, :] for rq, _ in rows]).astype(MM)
                kb = jnp.stack([kf[rk, :] for _, rk in rows]).astype(MM)
                vb = jnp.stack([vf[rk, :] for _, rk in rows]).astype(MM)
                s = jnp.where(_band_mask(first, nblk), _bdot(qb, kb, 2, 2) * scale, NEG)
                m = jnp.max(s, axis=-1, keepdims=True)
                p = jnp.exp(s - m)
                den = jnp.sum(p, axis=-1, keepdims=True)
                o = _bdot(p.astype(MM), vb, 2, 1) / den
                lse = m + jnp.log(den)
                for b, (rq, _) in enumerate(rows):
                    og[g][rq, :] = o[b]
                    lg[g][rq, :] = jnp.broadcast_to(lse[b], (ATT_BLOCK, dh))
                return carry

            @pl.when(pl.program_id(2) == g)
            def _(blocks=blocks, d=d, nblk=nblk):
                lax.fori_loop(0, d * nblk // ATT_BATCH, blocks, 0)

        @pl.when(pl.program_id(2) == N_GROUPS - 1)
        def _():
            mx = jnp.maximum(jnp.maximum(lg[0][...], lg[1][...]), lg[2][...])
            e = [jnp.exp(lg[g][...] - mx) for g in range(N_GROUPS)]
            tot = e[0] + e[1] + e[2]
            m_ref[...] = ((e[0] * og[0][...] + e[1] * og[1][...] + e[2] * og[2][...]) / tot).astype(m_ref.dtype)
            l_ref[...] = mx + jnp.log(tot)

    return _call(body, name=name, grid=(nb, N_HEADS, N_GROUPS), in_specs=[qs, ks, vs],
                 out_specs=[pl.BlockSpec((None, seq, dh), lambda b, h, g: (0, b, h)),
                            pl.BlockSpec((seq, dh), lambda b, h, g: (b, h))],
                 out_shape=[_sds((1, T, N_HEADS * dh), MM), _sds((T, N_HEADS * dh), F32)],
                 scratch=[pltpu.VMEM((seq, dh), F32)] + [pltpu.VMEM((_att_pad() + seq, dh), F32)] * 2
                 + [pltpu.VMEM((seq, dh), F32)] * (2 * N_GROUPS))(q, kv, kv)


def loss_fwd_bwd(name, x, target):
    T, D = x.shape
    tm = _tile(T, 512)

    def body(x_ref, t_ref, dx_ref, l_ref):
        @pl.when(pl.program_id(0) == 0)
        def _():
            l_ref[...] = jnp.zeros_like(l_ref)
        err = x_ref[...] - t_ref[...]
        dx_ref[...] = err * (1.0 / D)
        l_ref[...] += 0.5 * jnp.sum(jnp.mean(err * err, axis=-1, keepdims=True), axis=0, keepdims=True)

    dx, l = _call(body, name=name, grid=(T // tm,),
                  in_specs=[pl.BlockSpec((tm, D), lambda i: (i, 0))] * 2,
                  out_specs=[pl.BlockSpec((tm, D), lambda i: (i, 0)), pl.BlockSpec((1, 1), lambda i: (0, 0))],
                  out_shape=[_sds((T, D), F32), _sds((1, 1), F32)])(x, target)
    return dx, l


def resid_norm_bwd(name, dx, y, gain):
    T, D = y.shape
    tm = _tile(T, 1024)

    def body(dx_ref, y_ref, g_ref, dy_ref, dg_ref, db_ref):
        @pl.when(pl.program_id(0) == 0)
        def _():
            dg_ref[...] = jnp.zeros_like(dg_ref)
            db_ref[...] = jnp.zeros_like(db_ref)
        y = y_ref[...]
        d = dx_ref[...]
        r = lax.rsqrt(jnp.mean(y * y, axis=-1, keepdims=True) + EPS)
        yh = y * r
        dyh = d * g_ref[...]
        dy = r * (dyh - yh * jnp.mean(dyh * yh, axis=-1, keepdims=True))
        dy_ref[...] = dy.astype(dy_ref.dtype)
        dg_ref[...] += jnp.sum(d * yh, axis=0, keepdims=True)
        db_ref[...] += jnp.sum(dy, axis=0, keepdims=True)

    return _call(body, name=name, grid=(T // tm,),
                 in_specs=[pl.BlockSpec((tm, D), lambda i: (i, 0))] * 2 + [pl.BlockSpec((1, D), lambda i: (0, 0))],
                 out_specs=[pl.BlockSpec((None, tm, D), lambda i: (0, i, 0))] + [pl.BlockSpec((1, D), lambda i: (0, 0))] * 2,
                 out_shape=[_sds((1, T, D), MM), _sds((1, D), F32), _sds((1, D), F32)])(dx, y, gain.reshape(1, D))


def mm_nt(name, dy, w, layer, out_dtype):
    _, T, D = dy.shape
    _, nk, kk, _ = w.shape
    tm = _tile(T, 1024)

    def body(dy_ref, w_ref, o_ref):
        o_ref[...] = _dot_nt(dy_ref[...], w_ref[...]).astype(o_ref.dtype)

    return _call(body, name=name, grid=(T // tm, nk),
                 in_specs=[pl.BlockSpec((None, tm, D), lambda i, q: (0, i, 0)),
                           pl.BlockSpec((None, None, kk, D), lambda i, q: (layer, q, 0, 0))],
                 out_specs=pl.BlockSpec((None, tm, kk), lambda i, q: (q, i, 0)),
                 out_shape=_sds((nk, T, kk), out_dtype))(dy, w)


def mm_nt_norm_bwd(name, du, w, layer, x_in, gain, dx_res):
    nsh, T, n = du.shape
    D = x_in.shape[1]
    tm = _tile(T, 512)
    grp = nsh if n <= GROUP_ALL_MAX_N else nsh // 2
    steps = nsh // grp

    def body(du_ref, w_ref, x_ref, g_ref, dr_ref, dx_ref, dg_ref, acc_ref):
        i, j = pl.program_id(0), pl.program_id(1)

        @pl.when((i == 0) & (j == 0))
        def _():
            dg_ref[...] = jnp.zeros_like(dg_ref)

        part = _dot_nt(du_ref[0], w_ref[0])
        for k in range(1, grp):
            part = part + _dot_nt(du_ref[k], w_ref[k])

        @pl.when(j == 0)
        def _():
            acc_ref[...] = part

        @pl.when(j > 0)
        def _():
            acc_ref[...] += part

        @pl.when(j == steps - 1)
        def _():
            x = x_ref[...]
            dh = acc_ref[...]
            r = lax.rsqrt(jnp.mean(x * x, axis=-1, keepdims=True) + EPS)
            xh = x * r
            dxh = dh * g_ref[...]
            dx_ref[...] = dr_ref[...] + r * (dxh - xh * jnp.mean(dxh * xh, axis=-1, keepdims=True))
            dg_ref[...] += jnp.sum(dh * xh, axis=0, keepdims=True)

    return _call(body, name=name, grid=(T // tm, steps),
                 in_specs=[pl.BlockSpec((grp, tm, n), lambda i, j: (j, i, 0)),
                           pl.BlockSpec((grp, None, D, n), lambda i, j: (j, layer, 0, 0)),
                           pl.BlockSpec((tm, D), lambda i, j: (i, 0)),
                           pl.BlockSpec((1, D), lambda i, j: (0, 0)),
                           pl.BlockSpec((tm, D), lambda i, j: (i, 0))],
                 out_specs=[pl.BlockSpec((tm, D), lambda i, j: (i, 0)), pl.BlockSpec((1, D), lambda i, j: (0, 0))],
                 out_shape=[_sds((T, D), F32), _sds((1, D), F32)],
                 scratch=[pltpu.VMEM((tm, D), F32)])(du, w, x_in, gain.reshape(1, D), dx_res)


def mm_tn(name, a, b):
    na, T, ka = a.shape
    nb, _, kb = b.shape
    nj = max(na, nb)

    def body(a_ref, b_ref, o_ref):
        o_ref[...] = _dot_tn(a_ref[...], b_ref[...]).astype(o_ref.dtype)

    return _call(body, name=name, grid=(nj,),
                 in_specs=[pl.BlockSpec((None, T, ka), (lambda j: (j, 0, 0)) if na > 1 else (lambda j: (0, 0, 0))),
                           pl.BlockSpec((None, T, kb), (lambda j: (j, 0, 0)) if nb > 1 else (lambda j: (0, 0, 0)))],
                 out_specs=pl.BlockSpec((None, ka, kb), lambda j: (j, 0, 0)),
                 out_shape=_sds((nj, ka, kb), MM))(a, b)


def ffn_bwd(name, dy, wout, u, ag, dw, wt, x_in, gain, dx_res, layer, seq):
    _, T, D = dy.shape
    n = wt.shape[2]
    tm = _tile(seq, 512)
    tps = seq // tm
    hb = FFN_HALO
    _, nxt = _halo_maps(tm, hb, T)
    K = FFN_CONV_W
    te = tm + hb
    rows = _tile(hb, ROW_CHUNK_BWD)
    u4, w5 = u.reshape(2, 4, T, n), wt.reshape(2, 4, n, D)
    dw4 = dw.reshape(2, 4, dw.shape[1], K, n)
    tiles = _col_tiles(n, FFN_COL_TILE)

    def body(dy_ref, dyn_ref, wo_ref, u_ref, ag_ref, agn_ref, cw_ref, wt_ref, x_ref, g_ref, dr_ref,
             du_ref, ddw_ref, ddb_ref, dx_ref, dg_ref,
             dzf_ref, da_ref, acc_ref):
        i, j = pl.program_id(0), pl.program_id(1)
        last = (i % tps) == tps - 1

        @pl.when((i == 0) & (j == 0))
        def _():
            dg_ref[...] = jnp.zeros_like(dg_ref)

        @pl.when(i == 0)
        def _():
            ddw_ref[j] = jnp.zeros((2, K, n), F32)
            ddb_ref[j] = jnp.zeros((2, 1, n), F32)

        dyt, dyn = dy_ref[...], dyn_ref[...]
        for ci, (c0, wc) in enumerate(tiles):
            cols = pl.ds(c0, wc)
            dzf_ref[pl.ds(0, tm), cols] = _dot_nt(dyt, wo_ref[cols, :])
            dzf_ref[pl.ds(tm, hb), cols] = jnp.where(last, 0.0, _dot_nt(dyn, wo_ref[cols, :]))
            for r0 in range(0, te, rows):
                dzc = dzf_ref[pl.ds(r0, rows), cols]
                for half in range(2):
                    src = ag_ref[half, pl.ds(r0, rows), cols] if r0 < tm else agn_ref[half, pl.ds(r0 - tm, rows), cols]
                    da_ref[half, 0, pl.ds(r0, rows), cols] = dzc * src.astype(F32)
            for half in range(2):
                for s in range(1, K):
                    da_ref[half, s, pl.ds(0, tm), cols] = da_ref[half, 0, pl.ds(s, tm), cols]
                tap_acc = [jnp.zeros((rows, wc), F32) for _ in range(K)]
                bias_acc = jnp.zeros((rows, wc), F32)
                for r0 in range(0, tm, rows):
                    xr = u_ref[half, pl.ds(r0, rows), cols].astype(F32)
                    acc = None
                    for k in range(K):
                        d = da_ref[half, K - 1 - k, pl.ds(r0, rows), cols]
                        term = cw_ref[half, pl.ds(k, 1), cols] * d
                        acc = term if acc is None else acc + term
                        tap_acc[k] = tap_acc[k] + d * xr
                        if k == K - 1:
                            bias_acc = bias_acc + d
                    du_ref[half, pl.ds(r0, rows), cols] = acc.astype(du_ref.dtype)
                for k in range(K):
                    ddw_ref[j, half, pl.ds(k, 1), cols] += jnp.sum(tap_acc[k], axis=0, keepdims=True)
                ddb_ref[j, half, :, cols] += jnp.sum(bias_acc, axis=0, keepdims=True)
            part = _dot(du_ref[0, :, cols], wt_ref[0, cols, :]) + _dot(du_ref[1, :, cols], wt_ref[1, cols, :])
            if ci == 0:
                acc_ref[...] = part + jnp.where(j == 0, 0.0, acc_ref[...])
            else:
                acc_ref[...] += part
        @pl.when(j == 3)
        def _():
            x = x_ref[...]
            dh = acc_ref[...]
            r = lax.rsqrt(jnp.mean(x * x, axis=-1, keepdims=True) + EPS)
            xh = x * r
            dxh = dh * g_ref[...]
            dx_ref[...] = dr_ref[...] + r * (dxh - xh * jnp.mean(dxh * xh, axis=-1, keepdims=True))
            dg_ref[...] += jnp.sum(dh * xh, axis=0, keepdims=True)

    f32 = lambda *shape: pltpu.VMEM(shape, F32)
    du, ddw, ddb, dx, dg = _call(
        body, name=name, grid=(T // tm, 4),
        in_specs=[pl.BlockSpec((None, tm, D), lambda i, j: (0, i, 0)),
                  pl.BlockSpec((None, hb, D), lambda i, j: (0, nxt(i), 0)),
                  pl.BlockSpec((None, None, n, D), lambda i, j: (0, j, 0, 0)),
                  pl.BlockSpec((2, None, tm, n), lambda i, j: (0, j, i, 0)),
                  pl.BlockSpec((2, None, tm, n), lambda i, j: (0, j, i, 0)),
                  pl.BlockSpec((2, None, hb, n), lambda i, j: (0, j, nxt(i), 0)),
                  pl.BlockSpec((2, None, None, K, n), lambda i, j: (0, j, layer, 0, 0)),
                  pl.BlockSpec((2, None, n, D), lambda i, j: (0, j, 0, 0)),
                  pl.BlockSpec((tm, D), lambda i, j: (i, 0)),
                  pl.BlockSpec((1, D), lambda i, j: (0, 0)),
                  pl.BlockSpec((tm, D), lambda i, j: (i, 0))],
        out_specs=[pl.BlockSpec((2, None, tm, n), lambda i, j: (0, j, i, 0)),
                   pl.BlockSpec((4, 2, K, n), lambda i, j: (0, 0, 0, 0)),
                   pl.BlockSpec((4, 2, 1, n), lambda i, j: (0, 0, 0, 0)),
                   pl.BlockSpec((tm, D), lambda i, j: (i, 0)),
                   pl.BlockSpec((1, D), lambda i, j: (0, 0))],
        out_shape=[_sds((2, 4, T, n), MM), _sds((4, 2, K, n), F32), _sds((4, 2, 1, n), F32), _sds((T, D), F32),
                   _sds((1, D), F32)],
        scratch=[f32(te, n), f32(2, K, te, n), f32(tm, D)],
    )(dy, dy, wout, u4, ag, ag, dw4, w5, x_in, gain.reshape(1, D), dx_res)
    return du.reshape(8, T, n), jnp.swapaxes(ddw, 0, 1), jnp.swapaxes(ddb, 0, 1), dx, dg


def ln_silu_bwd(name, ds, c, g, b):
    T, D = c.shape
    tm = _tile(T, 512)

    def body(ds_ref, c_ref, g_ref, b_ref, dc_ref, dg_ref, db_ref):
        @pl.when(pl.program_id(0) == 0)
        def _():
            dg_ref[...] = jnp.zeros_like(dg_ref)
            db_ref[...] = jnp.zeros_like(db_ref)
        cf = c_ref[...]
        mu = jnp.mean(cf, axis=-1, keepdims=True)
        xc = cf - mu
        r = lax.rsqrt(jnp.mean(xc * xc, axis=-1, keepdims=True) + EPS)
        xh = xc * r
        t = xh * g_ref[...] + b_ref[...]
        sg = _sigmoid(t)
        dt = ds_ref[...] * (sg * (1.0 + t * (1.0 - sg)))
        dg_ref[...] += jnp.sum(dt * xh, axis=0, keepdims=True)
        db_ref[...] += jnp.sum(dt, axis=0, keepdims=True)
        dxh = dt * g_ref[...]
        dc_ref[...] = r * (dxh - jnp.mean(dxh, axis=-1, keepdims=True)
                           - xh * jnp.mean(dxh * xh, axis=-1, keepdims=True))

    vec = pl.BlockSpec((1, D), lambda i: (0, 0))
    return _call(body, name=name, grid=(T // tm,),
                 in_specs=[pl.BlockSpec((None, tm, D), lambda i: (0, i, 0)), pl.BlockSpec((tm, D), lambda i: (i, 0)),
                           vec, vec],
                 out_specs=[pl.BlockSpec((tm, D), lambda i: (i, 0)), vec, vec],
                 out_shape=[_sds((T, D), F32), _sds((1, D), F32), _sds((1, D), F32)])(
                     ds, c, g.reshape(1, D), b.reshape(1, D))


def cm_glu_conv_bwd(name, u, dc, dw, layer, seq):
    _, T, n = u.shape
    ct = dw.shape[-1]
    per = n // ct
    nct = 4 * per
    tm = _tile(seq, 512)
    tps = seq // tm
    hb = CONV_HALO
    prev, nxt = _halo_maps(tm, hb, T)
    u4 = u.reshape(2, 4, T, n)
    K = CONV_W

    rows = _tile(tm, 2 * ROW_CHUNK_FWD)
    wrows = 8

    def body(u_ref, up_ref, dc_ref, dcn_ref, w_ref, du_ref, ddw_ref, ddb_ref, dbi_ref, zp_ref, zd_ref):
        i = pl.program_id(1)
        first = (i % tps) == 0
        last = (i % tps) == tps - 1

        @pl.when(i == 0)
        def _():
            ddw_ref[...] = jnp.zeros_like(ddw_ref)
            ddb_ref[...] = jnp.zeros_like(ddb_ref)
            dbi_ref[...] = jnp.zeros_like(dbi_ref)

        uh = up_ref[...].astype(F32)
        zp_ref[0, pl.ds(hb, tm), :] = u_ref[0].astype(F32) * _sigmoid(u_ref[1].astype(F32))
        zp_ref[0, pl.ds(0, hb), :] = jnp.where(first, 0.0, uh[0] * _sigmoid(uh[1]))
        zd_ref[0, pl.ds(0, tm), :] = dc_ref[...]
        zd_ref[0, pl.ds(tm, hb), :] = jnp.where(last, 0.0, dcn_ref[...])
        _stage_shifts(zp_ref, hb + tm, back=True)
        _stage_shifts(zd_ref, tm + hb, back=False)
        for r0 in range(0, tm, rows):
            dp = None
            for j in range(K):
                term = w_ref[pl.ds(K - 1 - j, 1), :] * zd_ref[j % 8, pl.ds(r0 + (j - j % 8), rows), :]
                dp = term if dp is None else dp + term
            v = u_ref[0, pl.ds(r0, rows), :].astype(F32)
            sg = _sigmoid(u_ref[1, pl.ds(r0, rows), :].astype(F32))
            dv = dp * sg
            dg = dp * v * sg * (1.0 - sg)
            du_ref[0, pl.ds(r0, rows), :] = dv.astype(du_ref.dtype)
            du_ref[1, pl.ds(r0, rows), :] = dg.astype(du_ref.dtype)
            dbi_ref[0] += jnp.sum(dv, axis=0, keepdims=True)
            dbi_ref[1] += jnp.sum(dg, axis=0, keepdims=True)
        tap_acc = [jnp.zeros((wrows, ct), F32) for _ in range(K)]
        bias_acc = jnp.zeros((wrows, ct), F32)
        for r0 in range(0, tm, wrows):
            d = zd_ref[0, pl.ds(r0, wrows), :]
            for j in range(K):
                tap_acc[j] = tap_acc[j] + d * zp_ref[j % 8, pl.ds(hb + r0 - (j - j % 8), wrows), :]
            bias_acc = bias_acc + d
        for j in range(K):
            ddw_ref[pl.ds(K - 1 - j, 1), :] += jnp.sum(tap_acc[j], axis=0, keepdims=True)
        ddb_ref[...] += jnp.sum(bias_acc, axis=0, keepdims=True)

    du, ddw, ddb, dbi = _call(
        body, name=name, grid=(nct, T // tm),
        in_specs=[pl.BlockSpec((2, None, tm, ct), lambda c, i: (0, c // per, i, c % per)),
                  pl.BlockSpec((2, None, hb, ct), lambda c, i: (0, c // per, prev(i), c % per)),
                  pl.BlockSpec((tm, ct), lambda c, i: (i, c)),
                  pl.BlockSpec((hb, ct), lambda c, i: (nxt(i), c)),
                  pl.BlockSpec((None, None, K, ct), lambda c, i: (c, layer, 0, 0))],
        out_specs=[pl.BlockSpec((2, None, tm, ct), lambda c, i: (0, c // per, i, c % per)),
                   pl.BlockSpec((None, K, ct), lambda c, i: (c, 0, 0)),
                   pl.BlockSpec((None, 1, ct), lambda c, i: (c, 0, 0)),
                   pl.BlockSpec((2, None, 1, ct), lambda c, i: (0, c // per, 0, c % per))],
        out_shape=[_sds((2, 4, T, n), MM), _sds((nct, K, ct), F32), _sds((nct, 1, ct), F32), _sds((2, 4, 1, n), F32)],
        scratch=[pltpu.VMEM((8, tm + hb, ct), F32), pltpu.VMEM((8, tm + hb, ct), F32)])(u4, u4, dc, dc, dw)
    return du.reshape(8, T, n), ddw, ddb, dbi


def attn_bwd(name, q, kv, dm, merged, lse, dkv_prev, nb, seq):
    _, T, qn = q.shape
    kn = kv.shape[2]
    dh = qn * N_DEV // (N_GROUPS * N_HEADS)
    scale = 1.0 / (dh ** 0.5)
    qs, ks, vs = _head_specs(seq, dh, qn // dh, kn // dh)
    has_prev = dkv_prev is not None
    n_in = 6 + (1 if has_prev else 0)

    def body(*refs):
        q_ref, k_ref, v_ref, dm_ref, mg_ref, l_ref = refs[:6]
        pkv_ref = refs[6] if has_prev else None
        dq_ref, dkv_ref = refs[n_in:n_in + 2]
        qf, kf, vf, dqf, dkf, dvf, dlt = refs[n_in + 2:]
        pad = _att_pad()

        @pl.when(pl.program_id(2) == 0)
        def _():
            dlt[...] = jnp.broadcast_to(
                jnp.sum(dm_ref[...] * mg_ref[...].astype(F32), axis=-1, keepdims=True), (seq, dh))

        qf[...] = q_ref[...].astype(F32)
        for t_ref, s_ref in ((k_ref, kf), (v_ref, vf)):
            s_ref[pl.ds(0, pad), :] = jnp.zeros((pad, dh), F32)
            s_ref[pl.ds(pad, seq), :] = t_ref[...].astype(F32)
        dkf[...] = jnp.zeros_like(dkf)
        dvf[...] = jnp.zeros_like(dvf)
        for g in range(N_GROUPS):
            d = DILATIONS[g]
            nblk = seq // d // ATT_BLOCK

            def blocks(it, carry, d=d, nblk=nblk):
                first = it * ATT_BATCH
                rows = [_block_rows(first + b, d, nblk) for b in range(ATT_BATCH)]
                qb = jnp.stack([qf[rq, :] for rq, _ in rows]).astype(MM)
                dmb = jnp.stack([dm_ref[rq, :] for rq, _ in rows]).astype(MM)
                lse = jnp.stack([l_ref[rq, :][:, :1] for rq, _ in rows])
                delta = jnp.stack([dlt[rq, :][:, :1] for rq, _ in rows])
                kb = jnp.stack([kf[rk, :] for _, rk in rows]).astype(MM)
                vb = jnp.stack([vf[rk, :] for _, rk in rows]).astype(MM)
                s = jnp.where(_band_mask(first, nblk), _bdot(qb, kb, 2, 2) * scale, NEG)
                p = jnp.exp(s - lse)
                dsc = (p * (_bdot(dmb, vb, 2, 2) - delta) * scale).astype(MM)
                dv = _bdot(p.astype(MM), dmb, 1, 1)
                dk = _bdot(dsc, qb, 1, 1)
                dq = _bdot(dsc, kb, 2, 1)
                for b, (rq, rk) in enumerate(rows):
                    dqf[rq, :] = dq[b]
                    dkf[rk, :] += dk[b]
                    dvf[rk, :] += dv[b]
                return carry

            @pl.when(pl.program_id(2) == g)
            def _(blocks=blocks, d=d, nblk=nblk):
                lax.fori_loop(0, d * nblk // ATT_BATCH, blocks, 0)

        dq_ref[...] = dqf[...].astype(dq_ref.dtype)
        dk, dv = dkf[pl.ds(pad, seq), :], dvf[pl.ds(pad, seq), :]
        if has_prev:
            dk, dv = dk + pkv_ref[0].astype(F32), dv + pkv_ref[1].astype(F32)
        dkv_ref[0] = dk.astype(dkv_ref.dtype)
        dkv_ref[1] = dv.astype(dkv_ref.dtype)

    per = kn // dh

    def both(b, h, g):
        f = g * N_HEADS + h
        return (0, f // per, b, f % per)

    kv_spec = pl.BlockSpec((2, None, seq, dh), both)
    full = pl.BlockSpec((None, seq, dh), lambda b, h, g: (0, b, h))
    in_specs = [qs, ks, vs, full, full, pl.BlockSpec((seq, dh), lambda b, h, g: (b, h))]
    args = [q, kv, kv, dm, merged, lse]
    if has_prev:
        in_specs.append(kv_spec)
        args.append(dkv_prev.reshape(2, N_DEV // 2, T, kn))
    short, padded = pltpu.VMEM((seq, dh), F32), pltpu.VMEM((_att_pad() + seq, dh), F32)
    dq, dkv = _call(body, name=name, grid=(nb, N_HEADS, N_GROUPS), in_specs=in_specs, out_specs=[qs, kv_spec],
                    out_shape=[_sds(q.shape, MM), _sds((2, N_DEV // 2, T, kn), MM)],
                    scratch=[short, padded, padded, short, padded, padded, short])(*args)
    return dq, dkv.reshape(N_DEV, T, kn)


def _adamw_math(w, g, m, v):
    m = ADAM_B1 * m + (1.0 - ADAM_B1) * g
    v = ADAM_B2 * v + (1.0 - ADAM_B2) * (g * g)
    m_hat = m / (1.0 - ADAM_B1 ** ADAM_STEP)
    v_hat = v / (1.0 - ADAM_B2 ** ADAM_STEP)
    delta = -ADAM_LR * (m_hat / (jnp.sqrt(v_hat) + ADAM_EPS) + ADAM_WD * w)
    return delta, m, v


def adamw_sum(name, w, m, v, parts, after):
    L, R, C = w.shape
    tr = _tile(R, 256)

    def body(*refs):
        w_ref, m_ref, v_ref = refs[:3]
        p_refs = refs[3:3 + L]
        g_ref, d_ref, nm_ref, nv_ref = refs[4 + L:]
        for l in range(L):
            @pl.when(pl.program_id(0) == l)
            def _(p_ref=p_refs[l]):
                g = p_ref[0].astype(F32)
                for k in range(1, N_DEV):
                    g = g + p_ref[k].astype(F32)
                g_ref[...] = g
                d_ref[...], nm_ref[...], nv_ref[...] = _adamw_math(w_ref[...], g, m_ref[...], v_ref[...])

    blk = pl.BlockSpec((None, tr, C), lambda l, i: (l, i, 0))
    part = lambda k: pl.BlockSpec((N_DEV, tr, C), lambda l, i: (0, jnp.where(l == k, i, 0), 0))
    return _call(body, name=name, grid=(L, R // tr),
                 in_specs=[blk, blk, blk] + [part(k) for k in range(L)] + [pl.BlockSpec(memory_space=pl.ANY)],
                 out_specs=[blk] * 4, out_shape=[_sds((L, R, C), F32)] * 4)(w, m, v, *parts, after)


def sum_partials(name, parts):
    _, R, C = parts.shape
    tr = _tile(R, 512)

    def body(p_ref, o_ref):
        g = p_ref[0]
        for k in range(1, N_DEV):
            g = g + p_ref[k]
        o_ref[...] = g

    return _call(body, name=name, grid=(R // tr,),
                 in_specs=[pl.BlockSpec((N_DEV, tr, C), lambda i: (0, i, 0))],
                 out_specs=pl.BlockSpec((tr, C), lambda i: (i, 0)), out_shape=_sds((R, C), F32))(parts)


def adamw_small(name, w, g, m, v):
    R, C = w.shape
    tr = _tile(R, 512)

    def body(w_ref, g_ref, m_ref, v_ref, d_ref, nm_ref, nv_ref):
        d_ref[...], nm_ref[...], nv_ref[...] = _adamw_math(w_ref[...], g_ref[...], m_ref[...], v_ref[...])

    blk = pl.BlockSpec((tr, C), lambda i: (i, 0))
    return _call(body, name=name, grid=(R // tr,), in_specs=[blk] * 4, out_specs=[blk] * 3,
                 out_shape=[_sds((R, C), F32)] * 3)(w, g, m, v)


def _pack(arrays):
    pieces = []
    for a in arrays:
        f = a.reshape(-1).astype(F32)
        pieces.append(jnp.pad(f, (0, (-f.shape[0]) % PACK)))
    return jnp.concatenate(pieces).reshape(-1, 128)


def _unpack(flat, shapes):
    out, off = [], 0
    f = flat.reshape(-1)
    for s in shapes:
        size = 1
        for d in s:
            size *= d
        out.append(f[off:off + size].reshape(s))
        off += size + (-size) % PACK
    return out


def kernel(x, mix_pre_g, mix_post_g, ffn_pre_g, ffn_post_g, cm_w_in, cm_b_in, cm_dw, cm_dw_b, cm_ln_g, cm_ln_b, cm_w_out, cm_b_out, kv_norm_g, w_kv, w_q, w_o, ffn_w_in, ffn_dw, ffn_dw_b, ffn_w_out, loss_target, m_mix_pre_g, m_mix_post_g, m_ffn_pre_g, m_ffn_post_g, m_cm_w_in, m_cm_b_in, m_cm_dw, m_cm_dw_b, m_cm_ln_g, m_cm_ln_b, m_cm_w_out, m_cm_b_out, m_kv_norm_g, m_w_kv, m_w_q, m_w_o, m_ffn_w_in, m_ffn_dw, m_ffn_dw_b, m_ffn_w_out, v_mix_pre_g, v_mix_post_g, v_ffn_pre_g, v_ffn_post_g, v_cm_w_in, v_cm_b_in, v_cm_dw, v_cm_dw_b, v_cm_ln_g, v_cm_ln_b, v_cm_w_out, v_cm_b_out, v_kv_norm_g, v_w_kv, v_w_q, v_w_o, v_ffn_w_in, v_ffn_dw, v_ffn_dw_b, v_ffn_w_out):
    nb, seq, D = x.shape
    T = nb * seq
    me = _my_index()
    n_b = DEPTH - N_A

    nf = ffn_w_in.shape[-1]
    t_ = lambda t: jnp.swapaxes(t, 1, 2)
    fin_t, m_fin_t, v_fin_t = t_(ffn_w_in), t_(m_ffn_w_in), t_(v_ffn_w_in)

    stages = [(part, i) for i in range(DEPTH) for part in ("mix", "ffn")]
    stages.insert(stages.index(("ffn", N_A - 1)) + 1, ("kv", N_A - 1))

    def stage_sources(stage):
        part, i = stage
        if part == "ffn":
            src = {"fin": fin_t[i], "fout": ffn_w_out[i]}
        elif part == "kv":
            src = {"kv": w_kv}
        elif i < N_A:
            src = {"cin": cm_w_in[i], "cout": cm_w_out[i]}
        else:
            src = {"q": w_q[i - N_A], "o": w_o[i - N_A]}
        return {k: t.astype(MM) for k, t in src.items()}

    def begin_gather(stage, after):
        src = stage_sources(stage)
        names, arrays = list(src), list(src.values())
        tag = f"{stage[0]}{stage[1]}"
        lands = place_own(f"gather_own_{tag}", "gather", arrays)
        handle, token = exchange_begin(f"gather_begin_{tag}", "gather", arrays, lands, after)
        return (names, handle), token

    def end_gather(stage, pending, after):
        names, handle = pending
        W = dict(zip(names, exchange_end(f"gather_end_{stage[0]}{stage[1]}", handle, after)))
        for k in W:
            if k in ("cout", "o"):
                W[k] = W[k].reshape(1, 1, D, D)
            elif k == "fout":
                W[k] = W[k].reshape(1, 4, nf, D)
            else:
                W[k] = W[k][:, None]
        return W

    small = [cm_b_in[:, None, :], cm_dw, cm_dw_b[:, None, :], cm_ln_g, cm_ln_b, cm_b_out, ffn_dw]
    Bcin, DWc, DWBc, LNg, LNb, Bcout, DWf = all_gather("gather_small", small, [False] * len(small))
    LNg = jnp.swapaxes(LNg, 0, 1).reshape(N_A, D)
    LNb = jnp.swapaxes(LNb, 0, 1).reshape(N_A, D)
    Bcout = jnp.swapaxes(Bcout, 0, 1).reshape(N_A, D)
    DWBf = ffn_dw_b.reshape(DEPTH, N_DEV, 1, nf)
    zero_bias = jnp.zeros((D,), F32)

    xs = x.reshape(T, D)
    sv = []
    kv = hkv = None
    pending = {}
    pending[stages[0]], _ = begin_gather(stages[0], DWf)
    Ws = {stages[0]: end_gather(stages[0], pending[stages[0]], xs)}
    pending[stages[1]], token = begin_gather(stages[1], next(iter(Ws[stages[0]].values())))
    sv = [{} for _ in range(DEPTH)]
    for idx, stage in enumerate(stages):
        part, i = stage
        L, W = sv[i], Ws[stage]
        gain = {"mix": mix_pre_g[i], "ffn": ffn_pre_g[i], "kv": kv_norm_g}[part]
        following = stages[idx + 1] if idx + 1 < len(stages) else None
        if idx + 2 < len(stages):
            after = token if idx == 0 else next(iter(W.values()))
            pending[stages[idx + 2]], token = begin_gather(stages[idx + 2], after)
        if idx + 2 < len(stages) or idx == 0:
            gain = gain + token[0, 0]
        if part == "mix":
            L["x_in"] = xs
            if i < N_A:
                L["u"], L["h"] = norm_mm(f"cm_in_{i}", xs, gain, W["cin"], 0, Bcin[:, i:i + 1])
                L["c"] = cm_glu_conv(f"cm_conv_{i}", L["u"], DWc, DWBc, i, seq)
                L["s"] = ln_silu(f"cm_ln_{i}", L["c"], LNg[i], LNb[i])
                L["y"], xs = mm_resid_norm(f"cm_out_{i}", L["s"], W["cout"], 0, Bcout[i], xs, mix_post_g[i])
            else:
                L["q"], L["h"] = norm_mm(f"attn_q_{i}", xs, gain, W["q"], 0)
                L["mg"], L["lse"] = attn_fwd(f"attn_{i}", L["q"], kv, nb, seq)
                L["y"], xs = mm_resid_norm(f"attn_out_{i}", L["mg"], W["o"], 0, zero_bias, xs, mix_post_g[i])
            L["x1"] = xs
        elif part == "ffn":
            L["uf"], L["hf"], L["z"], L["ag"] = ffn_in_conv(f"ffn_in_{i}", xs, gain, W["fin"], DWf, DWBf, i, seq)
            L["yf"], xs = mm_resid_norm(f"ffn_out_{i}", L["z"], W["fout"], 0, zero_bias, xs, ffn_post_g[i])
        else:
            kv, hkv = norm_mm("kv_proj", xs, gain, W["kv"], 0)
        if following is not None:
            Ws[following] = end_gather(following, pending[following], kv if part == "kv" else xs)
    dx, loss_part = loss_fwd_bwd("loss", xs, loss_target.reshape(T, D))

    g_mix_pre, g_mix_post, g_ffn_pre, g_ffn_post = [None] * DEPTH, [None] * DEPTH, [None] * DEPTH, [None] * DEPTH
    g_ffn_dw, g_ffn_dwb = [None] * DEPTH, [None] * DEPTH
    g_cbin, g_cdw, g_cdwb, g_lng, g_lnb, g_cbout = ([None] * N_A for _ in range(6))
    g_kvn = dkv = None
    landed = [{} for _ in range(DEPTH)]
    in_flight = token = None
    for stage in reversed(stages):
        part, i = stage
        L, W = sv[i], Ws[stage]
        gain = {"mix": mix_post_g[i], "ffn": ffn_post_g[i], "kv": kv_norm_g}[part]
        if token is not None:
            gain = gain + token[0, 0]
        send = {}
        if part == "kv":
            send["kv"] = mm_tn("kv_wg", hkv[None], dkv)
            dx, g_kvn = mm_nt_norm_bwd("kv_bwd", dkv, W["kv"], 0, sv[i + 1]["x_in"], gain, dx)
        elif part == "ffn":
            dyf, g_ffn_post[i], _ = resid_norm_bwd(f"ffn_post_bwd_{i}", dx, L["yf"], gain)
            send["fout"] = mm_tn(f"ffn_out_wg_{i}", L["z"], dyf).reshape(N_DEV, nf // 2, D)
            duf, ddw, ddwb, dx, g_ffn_pre[i] = ffn_bwd(f"ffn_bwd_{i}", dyf, W["fout"], L["uf"], L["ag"], DWf, W["fin"],
                                                       L["x1"], ffn_pre_g[i], dx, i, seq)
            g_ffn_dw[i], g_ffn_dwb[i] = ddw.reshape(N_DEV, FFN_CONV_W, nf), ddwb.reshape(-1)
            send["fin"] = mm_tn(f"ffn_in_wg_{i}", duf, L["hf"][None])
        else:
            dy, g_mix_post[i], dyb = resid_norm_bwd(f"mix_post_bwd_{i}", dx, L["y"], gain)
            if i >= N_A:
                dm = mm_nt(f"attn_out_bwd_{i}", dy, W["o"], 0, F32)
                send["o"] = mm_tn(f"attn_out_wg_{i}", L["mg"], dy).reshape(N_DEV, D // N_DEV, D)
                dq, dkv = attn_bwd(f"attn_bwd_{i}", L["q"], kv, dm, L["mg"], L["lse"], dkv, nb, seq)
                send["q"] = mm_tn(f"attn_q_wg_{i}", L["h"][None], dq)
                dx, g_mix_pre[i] = mm_nt_norm_bwd(f"attn_q_bwd_{i}", dq, W["q"], 0, L["x_in"], mix_pre_g[i], dx)
            else:
                g_cbout[i] = dyb
                ds = mm_nt(f"cm_out_bwd_{i}", dy, W["cout"], 0, F32)
                send["cout"] = mm_tn(f"cm_out_wg_{i}", L["s"], dy).reshape(N_DEV, D // N_DEV, D)
                ln_gain = LNg[i]
                if stage == stages[0]:
                    early = [send.pop("cout")]
                    lands = place_own("scatter_own_cout0", "scatter", early)
                    early_handle, early_token = exchange_begin("scatter_begin_cout0", "scatter", early, lands, ds)
                    ln_gain = ln_gain + early_token[0, 0]
                dc, g_lng[i], g_lnb[i] = ln_silu_bwd(f"cm_ln_bwd_{i}", ds, L["c"], ln_gain, LNb[i])
                du, g_cdw[i], g_cdwb[i], dbi = cm_glu_conv_bwd(f"cm_conv_bwd_{i}", L["u"], dc, DWc, i, seq)
                g_cbin[i] = dbi.reshape(N_DEV, -1)
                send["cin"] = mm_tn(f"cm_in_wg_{i}", L["h"][None], du)
                dx, g_mix_pre[i] = mm_nt_norm_bwd(f"cm_in_bwd_{i}", du, W["cin"], 0, L["x_in"], mix_pre_g[i], dx)
        if in_flight is not None:
            (p, j), names, handle = in_flight
            landed[j].update(zip(names, exchange_end(f"scatter_end_{p}{j}", handle, dx)))
        names, arrays = list(send), list(send.values())
        lands = place_own(f"scatter_own_{part}{i}", "scatter", arrays)
        handle, token = exchange_begin(f"scatter_begin_{part}{i}", "scatter", arrays, lands, dx)
        in_flight = (stage, names, handle)
    grad_x = dx.reshape(nb, seq, D)

    rep_names = ["mix_pre_g", "mix_post_g", "ffn_pre_g", "ffn_post_g", "kv_norm_g", "ffn_dw_b"]
    rep_parts = [jnp.concatenate(g_mix_pre), jnp.concatenate(g_mix_post), jnp.concatenate(g_ffn_pre),
                 jnp.concatenate(g_ffn_post), g_kvn.reshape(-1), jnp.stack(g_ffn_dwb)]
    rep_w = [mix_pre_g, mix_post_g, ffn_pre_g, ffn_post_g, kv_norm_g, ffn_dw_b]
    rep_m = [m_mix_pre_g, m_mix_post_g, m_ffn_pre_g, m_ffn_post_g, m_kv_norm_g, m_ffn_dw_b]
    rep_v = [v_mix_pre_g, v_mix_post_g, v_ffn_pre_g, v_ffn_post_g, v_kv_norm_g, v_ffn_dw_b]
    sh_names = ["ffn_dw", "cm_b_in", "cm_dw", "cm_dw_b", "cm_ln_g", "cm_ln_b", "cm_b_out"]
    own = lambda per_layer, shard: jnp.stack([p.reshape((N_DEV,) + shard) for p in per_layer], axis=1)
    sh_parts = [own(g_ffn_dw, ffn_dw.shape[1:]), own(g_cbin, cm_b_in.shape[1:]), own(g_cdw, cm_dw.shape[1:]),
                own(g_cdwb, cm_dw_b.shape[1:]), own(g_lng, cm_ln_g.shape[1:]), own(g_lnb, cm_ln_b.shape[1:]),
                own(g_cbout, cm_b_out.shape[1:])]
    sh_w = [ffn_dw, cm_b_in, cm_dw, cm_dw_b, cm_ln_g, cm_ln_b, cm_b_out]
    sh_m = [m_ffn_dw, m_cm_b_in, m_cm_dw, m_cm_dw_b, m_cm_ln_g, m_cm_ln_b, m_cm_b_out]
    sh_v = [v_ffn_dw, v_cm_b_in, v_cm_dw, v_cm_dw_b, v_cm_ln_g, v_cm_ln_b, v_cm_b_out]
    rep_pack = _pack([loss_part] + rep_parts)
    sh_pack = jnp.stack([_pack([p[k] for p in sh_parts]) for k in range(N_DEV)])
    n_rep = rep_pack.shape[0]
    small = []
    for kind, pack in (("gather", rep_pack), ("scatter", sh_pack)):
        lands = place_own(f"{kind}_own_small", kind, [pack])
        handle, token = exchange_begin(f"{kind}_begin_small", kind, [pack], lands, token)
        small.append((kind, handle))

    def big_update(name, w, m, v, key, layers, after):
        as3 = lambda t: t.reshape((-1,) + t.shape[-2:])
        outs = adamw_sum(name, as3(w), as3(m), as3(v), [landed[i][key] for i in layers], after)
        return [t.reshape(w.shape) for t in outs]

    conf, attn = range(N_A), range(N_A, DEPTH)
    upd = {}
    upd["ffn_w_in"] = [t_(t) for t in big_update("adam_ffn_w_in", fin_t, m_fin_t, v_fin_t, "fin", range(DEPTH), token)]
    upd["ffn_w_out"] = big_update("adam_ffn_w_out", ffn_w_out, m_ffn_w_out, v_ffn_w_out, "fout", range(DEPTH), token)
    upd["w_kv"] = big_update("adam_w_kv", w_kv, m_w_kv, v_w_kv, "kv", [N_A - 1], token)
    upd["w_q"] = big_update("adam_w_q", w_q, m_w_q, v_w_q, "q", attn, token)
    upd["w_o"] = big_update("adam_w_o", w_o, m_w_o, v_w_o, "o", attn, upd["w_q"][0])
    (p, j), names, handle = in_flight
    landed[j].update(zip(names, exchange_end(f"scatter_end_{p}{j}", handle, upd["w_o"][0])))
    landed[0]["cout"] = exchange_end("scatter_end_cout0", early_handle, upd["w_o"][0])[0]
    upd["cm_w_in"] = big_update("adam_cm_w_in", cm_w_in, m_cm_w_in, v_cm_w_in, "cin", conf, token)
    upd["cm_w_out"] = big_update("adam_cm_w_out", cm_w_out, m_cm_w_out, v_cm_w_out, "cout", conf, upd["cm_w_in"][0])
    (rep_landed,), (sh_landed,) = (exchange_end(f"{kind}_end_small", handle, upd["cm_w_out"][0])
                                   for kind, handle in small)
    rep_sum = sum_partials("sum_small_rep", rep_landed)
    sh_sum = sum_partials("sum_small_sh", sh_landed)
    rep_shapes = [(1, 1)] + [w.shape for w in rep_w]
    sh_shapes = [w.shape for w in sh_w]
    g_small = jnp.concatenate([rep_sum, sh_sum])
    pad1 = jnp.zeros((1, 1), F32)
    d_s, m_s, v_s = adamw_small("adam_small", jnp.concatenate([_pack([pad1] + rep_w), _pack(sh_w)]), g_small,
                                jnp.concatenate([_pack([pad1] + rep_m), _pack(sh_m)]),
                                jnp.concatenate([_pack([pad1] + rep_v), _pack(sh_v)]))
    split = lambda t: (_unpack(t[:n_rep], rep_shapes), _unpack(t[n_rep:], sh_shapes))
    for (rep_t, sh_t), slot in zip([split(g_small), split(d_s), split(m_s), split(v_s)], range(4)):
        if slot == 0:
            loss = rep_t[0].reshape(())
        for name, t in zip(rep_names, rep_t[1:]):
            upd.setdefault(name, [None] * 4)[slot] = t
        for name, t in zip(sh_names, sh_t):
            upd.setdefault(name, [None] * 4)[slot] = t

    order = ["mix_pre_g", "mix_post_g", "ffn_pre_g", "ffn_post_g", "cm_w_in", "cm_b_in", "cm_dw", "cm_dw_b", "cm_ln_g",
             "cm_ln_b", "cm_w_out", "cm_b_out", "kv_norm_g", "w_kv", "w_q", "w_o", "ffn_w_in", "ffn_dw", "ffn_dw_b",
             "ffn_w_out"]
    return (loss, grad_x, *[upd[n][0] for n in order], *[upd[n][1] for n in order],
            *[upd[n][2] for n in order], *[upd[n][3] for n in order])
```

```python
import functools

import jax
import jax.numpy as jnp
from jax import lax
from jax.experimental import pallas as pl
from jax.experimental.pallas import tpu as pltpu

N_DEV = 8
N_A = 2
DEPTH = 4
N_HEADS = 8
N_GROUPS = 3
DILATIONS = (1, 4, 16)
ATT_BLOCK = 128
ATT_BATCH = 16
CONV_W = 31
FFN_CONV_W = 3
CONV_HALO = 32
FFN_HALO = 16
ROW_CHUNK_FWD = 16
ROW_CHUNK_BWD = 16
GROUP_ALL_MAX_N = 384
FFN_COL_TILE = 1024
EPS = 1e-6
NEG = -1e30
ADAM_LR, ADAM_B1, ADAM_B2, ADAM_EPS, ADAM_WD, ADAM_STEP = 0.001, 0.9, 0.999, 1e-08, 0.01, 10
MM = jnp.bfloat16
F32 = jnp.float32
VMEM_LIMIT_BYTES = 56 * 1024 * 1024
PACK = 1024
MESH_ID = pl.DeviceIdType.MESH

_pallas = pl.pallas_call


def _call(body, *, name, out_shape, grid=(), in_specs=None, out_specs=None, scratch=()):
    return _pallas(body, name=name, out_shape=out_shape, grid=grid, in_specs=in_specs, out_specs=out_specs,
                   scratch_shapes=list(scratch),
                   compiler_params=pltpu.CompilerParams(vmem_limit_bytes=VMEM_LIMIT_BYTES))


def _tile(n, pref):
    if n <= pref:
        return n
    t = pref - pref % 8
    while n % t:
        t -= 8
    assert t > 0, (n, pref)
    return t


def _sds(shape, dtype):
    return jax.ShapeDtypeStruct(tuple(shape), dtype)


def _dot(a, b):
    return jnp.dot(a, b, preferred_element_type=F32)


def _dot_nt(a, b):
    return lax.dot_general(a, b, (((1,), (1,)), ((), ())), preferred_element_type=F32)


def _dot_tn(a, b):
    return lax.dot_general(a, b, (((0,), (0,)), ((), ())), preferred_element_type=F32)


def _sigmoid(x):
    return 0.5 * jnp.tanh(0.5 * x) + 0.5


def _my_index():
    return 4 * lax.axis_index("x") + 2 * lax.axis_index("y") + lax.axis_index("c")


def _exchange(name, arrays, out_shapes, pieces, src_of, dst_of):
    n = len(arrays)
    base = [sum(pieces[:a]) for a in range(n)]
    total = sum(pieces)

    def body(*refs):
        ins, outs = refs[:n], refs[n:2 * n]
        send_sems, recv_sems, local_sems = refs[2 * n:]
        x, y, c = lax.axis_index("x"), lax.axis_index("y"), lax.axis_index("c")
        me = 4 * x + 2 * y + c
        copies = []
        for a in range(n):
            for k, (s, d) in enumerate(zip(src_of(a, ins[a], me), dst_of(a, outs[a], me))):
                cp = pltpu.make_async_copy(s, d, local_sems.at[base[a] + k])
                cp.start()
                copies.append(cp)
        remote = []
        for m in range(1, N_DEV):
            px, py, pc = x ^ (m >> 2), y ^ ((m >> 1) & 1), c ^ (m & 1)
            peer = 4 * px + 2 * py + pc
            for a in range(n):
                for k, (s, d) in enumerate(zip(src_of(a, ins[a], peer), dst_of(a, outs[a], me))):
                    cp = pltpu.make_async_remote_copy(src_ref=s, dst_ref=d, send_sem=send_sems.at[base[a] + k, m - 1],
                                                      recv_sem=recv_sems.at[base[a] + k, m - 1],
                                                      device_id=(px, py, pc), device_id_type=MESH_ID)
                    cp.start()
                    remote.append(cp)
        for cp in copies:
            cp.wait()
        for cp in remote:
            cp.wait_send()
        for m in range(1, N_DEV):
            px, py, pc = x ^ (m >> 2), y ^ ((m >> 1) & 1), c ^ (m & 1)
            peer = 4 * px + 2 * py + pc
            for a in range(n):
                for k, (s, d) in enumerate(zip(src_of(a, ins[a], me), dst_of(a, outs[a], peer))):
                    pltpu.make_async_remote_copy(src_ref=s, dst_ref=d, send_sem=send_sems.at[base[a] + k, m - 1],
                                                 recv_sem=recv_sems.at[base[a] + k, m - 1], device_id=(px, py, pc),
                                                 device_id_type=MESH_ID).wait_recv()

    any_spec = pl.BlockSpec(memory_space=pl.ANY)
    return _call(body, name=name, out_shape=[_sds(s, a.dtype) for s, a in zip(out_shapes, arrays)],
                 in_specs=[any_spec] * n, out_specs=[any_spec] * n,
                 scratch=[pltpu.SemaphoreType.DMA((total, N_DEV - 1)), pltpu.SemaphoreType.DMA((total, N_DEV - 1)),
                          pltpu.SemaphoreType.DMA((total,))])(*arrays)


def all_gather(name, arrays, row_sharded):
    def out_shape(a):
        s = arrays[a].shape
        return (s[0], N_DEV) + s[1:] if row_sharded[a] else (N_DEV,) + s

    def src_of(a, ref, peer):
        if row_sharded[a]:
            return [ref.at[l] for l in range(arrays[a].shape[0])]
        return [ref]

    def dst_of(a, ref, me):
        if row_sharded[a]:
            return [ref.at[l, me] for l in range(arrays[a].shape[0])]
        return [ref.at[me]]

    pieces = [arrays[a].shape[0] if row_sharded[a] else 1 for a in range(len(arrays))]
    return _exchange(name, arrays, [out_shape(a) for a in range(len(arrays))], pieces, src_of, dst_of)


def _src_view(kind, ref, peer):
    return ref if kind == "gather" else ref.at[peer]


def _peers(x, y, c):
    for m in range(1, N_DEV):
        px, py, pc = x ^ (m >> 2), y ^ ((m >> 1) & 1), c ^ (m & 1)
        yield m - 1, (px, py, pc), 4 * px + 2 * py + pc


def place_own(name, kind, srcs):
    n = len(srcs)
    shapes = [(N_DEV,) + s.shape if kind == "gather" else s.shape for s in srcs]
    steps = 2 if all(s.shape[-2] % 32 == 0 for s in srcs) else 1

    def body(*refs):
        for a in range(n):
            refs[n + a][...] = refs[a][...]

    def spec(shape, own_block):
        R, C = shape[-2:]
        tr = R // steps
        if own_block:
            return pl.BlockSpec((None, tr, C), lambda i: (_my_index(), i, 0))
        return pl.BlockSpec((tr, C), lambda i: (i, 0))

    return _call(body, name=name, grid=(steps,), out_shape=[_sds(s, a.dtype) for s, a in zip(shapes, srcs)],
                 in_specs=[spec(s.shape, kind == "scatter") for s in srcs],
                 out_specs=[spec(s, True) for s in shapes])(*srcs)


_HBM_SPEC = pl.BlockSpec(memory_space=pltpu.HBM)
_SEM_SPEC = pl.BlockSpec(memory_space=pltpu.SEMAPHORE)
_DATAFLOW = pltpu.SideEffectType.DATAFLOW_SIDE_EFFECTING


def _remote(kind, src, land, send_sems, recv_sems, a, slot, frm, to_id, at):
    return pltpu.make_async_remote_copy(src_ref=_src_view(kind, src, frm), dst_ref=land.at[at],
                                        send_sem=send_sems.at[a * (N_DEV - 1) + slot],
                                        recv_sem=recv_sems.at[a * (N_DEV - 1) + slot],
                                        device_id=to_id, device_id_type=MESH_ID)


def exchange_begin(name, kind, srcs, lands, after):
    n = len(srcs)

    def body(*refs):
        ins, lnd = refs[:n], refs[n:2 * n]
        send_sems, recv_sems = refs[2 * n + 1], refs[2 * n + 2]
        token = refs[-1]
        x, y, c = lax.axis_index("x"), lax.axis_index("y"), lax.axis_index("c")
        me = 4 * x + 2 * y + c
        for slot, peer_id, peer in _peers(x, y, c):
            for a in range(n):
                _remote(kind, ins[a], lnd[a], send_sems, recv_sems, a, slot, peer, peer_id, me).start()
        token[...] = jnp.zeros_like(token)

    hbm = lambda t: pltpu.HBM(t.shape, t.dtype)
    outs = _pallas(
        body, name=name,
        out_shape=(pltpu.SemaphoreType.DMA((n * (N_DEV - 1),)), pltpu.SemaphoreType.DMA((n * (N_DEV - 1),)),
                   *[hbm(t) for t in srcs], *[hbm(t) for t in lands], _sds((8, 128), F32)),
        in_specs=[_HBM_SPEC] * (2 * n) + [pl.BlockSpec(memory_space=pl.ANY)],
        out_specs=(_SEM_SPEC, _SEM_SPEC, *[_HBM_SPEC] * (2 * n), pl.BlockSpec(memory_space=pltpu.VMEM)),
        input_output_aliases={i: 2 + i for i in range(2 * n)},
        compiler_params=pltpu.CompilerParams(has_side_effects=_DATAFLOW),
    )(*[pltpu.with_memory_space_constraint(t, pltpu.HBM) for t in list(srcs) + list(lands)], after)
    return (kind, outs[0], outs[1], list(outs[2:2 + n]), list(outs[2 + n:2 + 2 * n])), outs[-1]


def exchange_end(name, handle, after):
    kind, send_sems, recv_sems, srcs, lands = handle
    n = len(srcs)

    def body(*refs):
        ins, lnd = refs[:n], refs[n:2 * n]
        s_sems, r_sems = refs[2 * n], refs[2 * n + 1]
        x, y, c = lax.axis_index("x"), lax.axis_index("y"), lax.axis_index("c")
        me = 4 * x + 2 * y + c
        for slot, peer_id, peer in _peers(x, y, c):
            for a in range(n):
                _remote(kind, ins[a], lnd[a], s_sems, r_sems, a, slot, peer, peer_id, me).wait_send()
        for slot, peer_id, peer in _peers(x, y, c):
            for a in range(n):
                _remote(kind, ins[a], lnd[a], s_sems, r_sems, a, slot, me, peer_id, peer).wait_recv()

    hbm = lambda t: pltpu.HBM(t.shape, t.dtype)
    outs = _pallas(
        body, name=name, out_shape=tuple(hbm(t) for t in srcs + lands),
        in_specs=[_HBM_SPEC] * (2 * n) + [_SEM_SPEC, _SEM_SPEC, pl.BlockSpec(memory_space=pl.ANY)],
        out_specs=tuple([_HBM_SPEC] * (2 * n)), input_output_aliases={i: i for i in range(2 * n)},
        compiler_params=pltpu.CompilerParams(has_side_effects=_DATAFLOW),
    )(*srcs, *lands, send_sems, recv_sems, after)
    return list(outs[n:])


def norm_mm(name, x, gain, w, layer, bias=None):
    T, D = x.shape
    nsh, _, _, n = w.shape
    tm = _tile(T, 1024)
    grp = nsh if n <= GROUP_ALL_MAX_N else nsh // 2
    steps = nsh // grp

    def body(*refs):
        if bias is None:
            x_ref, g_ref, w_ref, u_ref, h_ref = refs
        else:
            x_ref, g_ref, w_ref, b_ref, u_ref, h_ref = refs

        @pl.when(pl.program_id(1) == 0)
        def _():
            xf = x_ref[...]
            r = lax.rsqrt(jnp.mean(xf * xf, axis=-1, keepdims=True) + EPS)
            h_ref[...] = (xf * r * g_ref[...]).astype(h_ref.dtype)

        h = h_ref[...]
        for k in range(grp):
            acc = _dot(h, w_ref[k])
            if bias is not None:
                acc = acc + b_ref[k]
            u_ref[k] = acc.astype(u_ref.dtype)

    in_specs = [pl.BlockSpec((tm, D), lambda i, j: (i, 0)),
                pl.BlockSpec((1, D), lambda i, j: (0, 0)),
                pl.BlockSpec((grp, None, D, n), lambda i, j: (j, layer, 0, 0))]
    args = [x, gain.reshape(1, D), w]
    if bias is not None:
        in_specs.append(pl.BlockSpec((grp, None, 1, n), lambda i, j: (j, layer, 0, 0)))
        args.append(bias)
    return _call(body, name=name, grid=(T // tm, steps), in_specs=in_specs,
                 out_specs=[pl.BlockSpec((grp, tm, n), lambda i, j: (j, i, 0)),
                            pl.BlockSpec((tm, D), lambda i, j: (i, 0))],
                 out_shape=[_sds((nsh, T, n), MM), _sds((T, D), MM)])(*args)


def mm_resid_norm(name, a, w, layer, bias, x, gain):
    nk, T, kk = a.shape
    D = x.shape[1]
    tm = _tile(T, 1024)

    def body(a_ref, w_ref, b_ref, x_ref, g_ref, y_ref, xn_ref):
        y = _dot(a_ref[0], w_ref[0])
        for q in range(1, nk):
            y = y + _dot(a_ref[q], w_ref[q])
        y = y + b_ref[...]
        y_ref[...] = y
        r = lax.rsqrt(jnp.mean(y * y, axis=-1, keepdims=True) + EPS)
        xn_ref[...] = x_ref[...] + y * r * g_ref[...]

    return _call(body, name=name, grid=(T // tm,),
                 in_specs=[pl.BlockSpec((nk, tm, kk), lambda i: (0, i, 0)),
                           pl.BlockSpec((None, nk, kk, D), lambda i: (layer, 0, 0, 0), pipeline_mode=pl.Buffered(1)),
                           pl.BlockSpec((1, D), lambda i: (0, 0)),
                           pl.BlockSpec((tm, D), lambda i: (i, 0)),
                           pl.BlockSpec((1, D), lambda i: (0, 0))],
                 out_specs=[pl.BlockSpec((tm, D), lambda i: (i, 0))] * 2,
                 out_shape=[_sds((T, D), F32)] * 2)(a, w, bias.reshape(1, D), x, gain.reshape(1, D))


def _halo_maps(tm, hb, T):
    per = tm // hb
    last = T // hb - 1
    return (lambda i: jnp.maximum(i * per - 1, 0)), (lambda i: jnp.minimum((i + 1) * per, last))


def cm_glu_conv(name, u, dw, dwb, layer, seq):
    _, T, n = u.shape
    ct = dw.shape[-1]
    per = n // ct
    nct = 4 * per
    tm = _tile(seq, 512)
    tps = seq // tm
    hb = CONV_HALO
    prev, _ = _halo_maps(tm, hb, T)
    u4 = u.reshape(2, 4, T, n)

    rows = _tile(tm, 2 * ROW_CHUNK_FWD)

    def body(u_ref, uh_ref, w_ref, b_ref, o_ref, z_ref):
        first = (pl.program_id(0) % tps) == 0
        um = u_ref[...].astype(F32)
        uh = uh_ref[...].astype(F32)
        z_ref[0, pl.ds(hb, tm), :] = um[0] * _sigmoid(um[1])
        z_ref[0, pl.ds(0, hb), :] = jnp.where(first, 0.0, uh[0] * _sigmoid(uh[1]))
        _stage_shifts(z_ref, hb + tm, back=True)
        for r0 in range(0, tm, rows):
            acc = b_ref[...]
            for j in range(CONV_W):
                acc = acc + w_ref[pl.ds(CONV_W - 1 - j, 1), :] * z_ref[j % 8, pl.ds(hb + r0 - (j - j % 8), rows), :]
            o_ref[pl.ds(r0, rows), :] = acc

    return _call(body, name=name, grid=(T // tm, nct),
                 in_specs=[pl.BlockSpec((2, None, tm, ct), lambda i, c: (0, c // per, i, c % per)),
                           pl.BlockSpec((2, None, hb, ct), lambda i, c: (0, c // per, prev(i), c % per)),
                           pl.BlockSpec((None, None, CONV_W, ct), lambda i, c: (c, layer, 0, 0)),
                           pl.BlockSpec((None, None, 1, ct), lambda i, c: (c, layer, 0, 0))],
                 out_specs=pl.BlockSpec((tm, ct), lambda i, c: (i, c)),
                 out_shape=_sds((T, nct * ct), F32),
                 scratch=[pltpu.VMEM((8, tm + hb, ct), F32)])(u4, u4, dw, dwb)


def _stage_shifts(z_ref, n_rows, back):
    for r in range(1, 8):
        if back:
            z_ref[r, pl.ds(8, n_rows - 8), :] = z_ref[0, pl.ds(8 - r, n_rows - 8), :]
        else:
            z_ref[r, pl.ds(0, n_rows - 8), :] = z_ref[0, pl.ds(r, n_rows - 8), :]


def ln_silu(name, c, g, b):
    T, D = c.shape
    tm = _tile(T, 512)

    def body(c_ref, g_ref, b_ref, s_ref):
        cf = c_ref[...]
        mu = jnp.mean(cf, axis=-1, keepdims=True)
        xc = cf - mu
        r = lax.rsqrt(jnp.mean(xc * xc, axis=-1, keepdims=True) + EPS)
        t = xc * r * g_ref[...] + b_ref[...]
        s_ref[...] = (t * _sigmoid(t)).astype(s_ref.dtype)

    return _call(body, name=name, grid=(T // tm,),
                 in_specs=[pl.BlockSpec((tm, D), lambda i: (i, 0)), pl.BlockSpec((1, D), lambda i: (0, 0)),
                           pl.BlockSpec((1, D), lambda i: (0, 0))],
                 out_specs=pl.BlockSpec((None, tm, D), lambda i: (0, i, 0)),
                 out_shape=_sds((1, T, D), MM))(c, g.reshape(1, D), b.reshape(1, D))


def _col_tiles(n, width):
    return [(c0, min(width, n - c0)) for c0 in range(0, n, width)]


def ffn_in_conv(name, x, gain, wt, dw, dwb, layer, seq):
    T, D = x.shape
    n = wt.shape[2]
    tm = _tile(seq, 1024)
    tps = seq // tm
    hb = 8
    rows = _tile(tm, ROW_CHUNK_FWD)
    K = FFN_CONV_W
    w5 = wt.reshape(2, 4, n, D)
    dw4 = dw.reshape(2, 4, dw.shape[1], K, n)
    dwb4 = dwb.reshape(dwb.shape[0], 2, 4, 1, n)

    def body(x_ref, g_ref, w_ref, cw_ref, cb_ref, u_ref, h_ref, z_ref, ag_ref, pad_ref, carry_ref):
        i, j = pl.program_id(0), pl.program_id(1)
        first = (i % tps) == 0

        @pl.when(j == 0)
        def _():
            xf = x_ref[...]
            r = lax.rsqrt(jnp.mean(xf * xf, axis=-1, keepdims=True) + EPS)
            h_ref[...] = (xf * r * g_ref[...]).astype(h_ref.dtype)

        h = h_ref[...]
        for c0, wc in _col_tiles(n, FFN_COL_TILE):
            cols = pl.ds(c0, wc)
            for half in range(2):
                res = _dot_nt(h, w_ref[half, cols, :])
                u_ref[half, :, cols] = res.astype(u_ref.dtype)
                pad_ref[half, pl.ds(hb, tm), cols] = res
                pad_ref[half, pl.ds(0, hb), cols] = jnp.where(first, 0.0, carry_ref[j, half, :, cols])
            for r0 in range(0, tm, rows):
                conv = []
                for half in range(2):
                    acc = cb_ref[half, :, cols] + cw_ref[half, pl.ds(K - 1, 1), cols] * pad_ref[
                        half, pl.ds(hb + r0, rows), cols]
                    for k in range(K - 1):
                        acc = acc + cw_ref[half, pl.ds(k, 1), cols] * pad_ref[
                            half, pl.ds(hb + r0 - (K - 1) + k, rows), cols]
                    conv.append(acc)
                a, g = conv
                sg = _sigmoid(g)
                silu = g * sg
                z_ref[pl.ds(r0, rows), cols] = (silu * a).astype(z_ref.dtype)
                ag_ref[0, pl.ds(r0, rows), cols] = silu.astype(ag_ref.dtype)
                ag_ref[1, pl.ds(r0, rows), cols] = (a * (sg * (1.0 + g * (1.0 - sg)))).astype(ag_ref.dtype)
            for half in range(2):
                carry_ref[j, half, :, cols] = pad_ref[half, pl.ds(tm, hb), cols]

    u, h, z, ag = _call(
        body, name=name, grid=(T // tm, 4),
        in_specs=[pl.BlockSpec((tm, D), lambda i, j: (i, 0)),
                  pl.BlockSpec((1, D), lambda i, j: (0, 0)),
                  pl.BlockSpec((2, None, n, D), lambda i, j: (0, j, 0, 0)),
                  pl.BlockSpec((2, None, None, K, n), lambda i, j: (0, j, layer, 0, 0)),
                  pl.BlockSpec((None, 2, None, 1, n), lambda i, j: (layer, 0, j, 0, 0))],
        out_specs=[pl.BlockSpec((2, None, tm, n), lambda i, j: (0, j, i, 0)),
                   pl.BlockSpec((tm, D), lambda i, j: (i, 0)),
                   pl.BlockSpec((None, tm, n), lambda i, j: (j, i, 0)),
                   pl.BlockSpec((2, None, tm, n), lambda i, j: (0, j, i, 0))],
        out_shape=[_sds((2, 4, T, n), MM), _sds((T, D), MM), _sds((4, T, n), MM), _sds((2, 4, T, n), MM)],
        scratch=[pltpu.VMEM((2, tm + hb, n), F32), pltpu.VMEM((4, 2, hb, n), F32)])(
            x, gain.reshape(1, D), w5, dw4, dwb4)
    return u.reshape(8, T, n), h, z, ag


def _head_specs(seq, dh, q_heads, kv_heads):
    def spec(per, base):
        def imap(b, h, g):
            f = base + g * N_HEADS + h
            return (f // per, b, f % per)
        return pl.BlockSpec((None, seq, dh), imap)
    return spec(q_heads, 0), spec(kv_heads, 0), spec(kv_heads, N_GROUPS * N_HEADS)


def _rows(start, d, blocks=1):
    size = blocks * ATT_BLOCK
    return pl.ds(start, size, stride=d) if d > 1 else pl.ds(start, size)


def _att_pad():
    return max(ATT_BLOCK * d for d in DILATIONS[:-1])


def _band_mask(first, nblk):
    keys = ATT_BLOCK if nblk == 1 else 2 * ATT_BLOCK
    shape = (ATT_BATCH, ATT_BLOCK, keys)
    qi = lax.broadcasted_iota(jnp.int32, shape, 1)
    kj = lax.broadcasted_iota(jnp.int32, shape, 2)
    if nblk == 1:
        return kj <= qi
    n = (first + lax.broadcasted_iota(jnp.int32, shape, 0)) % nblk
    return (kj >= qi) & (kj <= qi + ATT_BLOCK) & ((n > 0) | (kj >= ATT_BLOCK))


def _block_rows(idx, d, nblk):
    r, n = idx // nblk, idx % nblk
    rq = _rows(r + d * ATT_BLOCK * n, d)
    if nblk == 1:
        return rq, _rows(_att_pad() + r + d * ATT_BLOCK * n, d)
    return rq, _rows(_att_pad() + r + d * ATT_BLOCK * (n - 1), d, blocks=2)


def _bdot(a, b, ca, cb):
    return lax.dot_general(a, b, (((ca,), (cb,)), ((0,), (0,))), preferred_element_type=F32)


def attn_fwd(name, q, kv, nb, seq):
    _, T, qn = q.shape
    dh = qn * N_DEV // (N_GROUPS * N_HEADS)
    scale = 1.0 / (dh ** 0.5)
    qs, ks, vs = _head_specs(seq, dh, qn // dh, kv.shape[2] // dh)

    def body(q_ref, k_ref, v_ref, m_ref, l_ref, qf, kf, vf, *branch):
        og, lg = branch[:N_GROUPS], branch[N_GROUPS:]
        pad = _att_pad()
        qf[...] = q_ref[...].astype(F32)
        for t_ref, s_ref in ((k_ref, kf), (v_ref, vf)):
            s_ref[pl.ds(0, pad), :] = jnp.zeros((pad, dh), F32)
            s_ref[pl.ds(pad, seq), :] = t_ref[...].astype(F32)
        for g in range(N_GROUPS):
            d = DILATIONS[g]
            nblk = seq // d // ATT_BLOCK

            def blocks(it, carry, d=d, nblk=nblk, g=g):
                first = it * ATT_BATCH
                rows = [_block_rows(first + b, d, nblk) for b in range(ATT_BATCH)]
                qb = jnp.stack([qf[rq, :] for rq, _ in rows]).astype(MM)
                kb = jnp.stack([kf[rk, :] for _, rk in rows]).astype(MM)
                vb = jnp.stack([vf[rk, :] for _, rk in rows]).astype(MM)
                s = jnp.where(_band_mask(first, nblk), _bdot(qb, kb, 2, 2) * scale, NEG)
                m = jnp.max(s, axis=-1, keepdims=True)
                p = jnp.exp(s - m)
                den = jnp.sum(p, axis=-1, keepdims=True)
                o = _bdot(p.astype(MM), vb, 2, 1) / den
                lse = m + jnp.log(den)
                for b, (rq, _) in enumerate(rows):
                    og[g][rq, :] = o[b]
                    lg[g][rq, :] = jnp.broadcast_to(lse[b], (ATT_BLOCK, dh))
                return carry

            @pl.when(pl.program_id(2) == g)
            def _(blocks=blocks, d=d, nblk=nblk):
                lax.fori_loop(0, d * nblk // ATT_BATCH, blocks, 0)

        @pl.when(pl.program_id(2) == N_GROUPS - 1)
        def _():
            mx = jnp.maximum(jnp.maximum(lg[0][...], lg[1][...]), lg[2][...])
            e = [jnp.exp(lg[g][...] - mx) for g in range(N_GROUPS)]
            tot = e[0] + e[1] + e[2]
            m_ref[...] = ((e[0] * og[0][...] + e[1] * og[1][...] + e[2] * og[2][...]) / tot).astype(m_ref.dtype)
            l_ref[...] = mx + jnp.log(tot)

    return _call(body, name=name, grid=(nb, N_HEADS, N_GROUPS), in_specs=[qs, ks, vs],
                 out_specs=[pl.BlockSpec((None, seq, dh), lambda b, h, g: (0, b, h)),
                            pl.BlockSpec((seq, dh), lambda b, h, g: (b, h))],
                 out_shape=[_sds((1, T, N_HEADS * dh), MM), _sds((T, N_HEADS * dh), F32)],
                 scratch=[pltpu.VMEM((seq, dh), F32)] + [pltpu.VMEM((_att_pad() + seq, dh), F32)] * 2
                 + [pltpu.VMEM((seq, dh), F32)] * (2 * N_GROUPS))(q, kv, kv)


def loss_fwd_bwd(name, x, target):
    T, D = x.shape
    tm = _tile(T, 512)

    def body(x_ref, t_ref, dx_ref, l_ref):
        @pl.when(pl.program_id(0) == 0)
        def _():
            l_ref[...] = jnp.zeros_like(l_ref)
        err = x_ref[...] - t_ref[...]
        dx_ref[...] = err * (1.0 / D)
        l_ref[...] += 0.5 * jnp.sum(jnp.mean(err * err, axis=-1, keepdims=True), axis=0, keepdims=True)

    dx, l = _call(body, name=name, grid=(T // tm,),
                  in_specs=[pl.BlockSpec((tm, D), lambda i: (i, 0))] * 2,
                  out_specs=[pl.BlockSpec((tm, D), lambda i: (i, 0)), pl.BlockSpec((1, 1), lambda i: (0, 0))],
                  out_shape=[_sds((T, D), F32), _sds((1, 1), F32)])(x, target)
    return dx, l


def resid_norm_bwd(name, dx, y, gain):
    T, D = y.shape
    tm = _tile(T, 1024)

    def body(dx_ref, y_ref, g_ref, dy_ref, dg_ref, db_ref):
        @pl.when(pl.program_id(0) == 0)
        def _():
            dg_ref[...] = jnp.zeros_like(dg_ref)
            db_ref[...] = jnp.zeros_like(db_ref)
        y = y_ref[...]
        d = dx_ref[...]
        r = lax.rsqrt(jnp.mean(y * y, axis=-1, keepdims=True) + EPS)
        yh = y * r
        dyh = d * g_ref[...]
        dy = r * (dyh - yh * jnp.mean(dyh * yh, axis=-1, keepdims=True))
        dy_ref[...] = dy.astype(dy_ref.dtype)
        dg_ref[...] += jnp.sum(d * yh, axis=0, keepdims=True)
        db_ref[...] += jnp.sum(dy, axis=0, keepdims=True)

    return _call(body, name=name, grid=(T // tm,),
                 in_specs=[pl.BlockSpec((tm, D), lambda i: (i, 0))] * 2 + [pl.BlockSpec((1, D), lambda i: (0, 0))],
                 out_specs=[pl.BlockSpec((None, tm, D), lambda i: (0, i, 0))] + [pl.BlockSpec((1, D), lambda i: (0, 0))] * 2,
                 out_shape=[_sds((1, T, D), MM), _sds((1, D), F32), _sds((1, D), F32)])(dx, y, gain.reshape(1, D))


def mm_nt(name, dy, w, layer, out_dtype):
    _, T, D = dy.shape
    _, nk, kk, _ = w.shape
    tm = _tile(T, 1024)

    def body(dy_ref, w_ref, o_ref):
        o_ref[...] = _dot_nt(dy_ref[...], w_ref[...]).astype(o_ref.dtype)

    return _call(body, name=name, grid=(T // tm, nk),
                 in_specs=[pl.BlockSpec((None, tm, D), lambda i, q: (0, i, 0)),
                           pl.BlockSpec((None, None, kk, D), lambda i, q: (layer, q, 0, 0))],
                 out_specs=pl.BlockSpec((None, tm, kk), lambda i, q: (q, i, 0)),
                 out_shape=_sds((nk, T, kk), out_dtype))(dy, w)


def mm_nt_norm_bwd(name, du, w, layer, x_in, gain, dx_res):
    nsh, T, n = du.shape
    D = x_in.shape[1]
    tm = _tile(T, 512)
    grp = nsh if n <= GROUP_ALL_MAX_N else nsh // 2
    steps = nsh // grp

    def body(du_ref, w_ref, x_ref, g_ref, dr_ref, dx_ref, dg_ref, acc_ref):
        i, j = pl.program_id(0), pl.program_id(1)

        @pl.when((i == 0) & (j == 0))
        def _():
            dg_ref[...] = jnp.zeros_like(dg_ref)

        part = _dot_nt(du_ref[0], w_ref[0])
        for k in range(1, grp):
            part = part + _dot_nt(du_ref[k], w_ref[k])

        @pl.when(j == 0)
        def _():
            acc_ref[...] = part

        @pl.when(j > 0)
        def _():
            acc_ref[...] += part

        @pl.when(j == steps - 1)
        def _():
            x = x_ref[...]
            dh = acc_ref[...]
            r = lax.rsqrt(jnp.mean(x * x, axis=-1, keepdims=True) + EPS)
            xh = x * r
            dxh = dh * g_ref[...]
            dx_ref[...] = dr_ref[...] + r * (dxh - xh * jnp.mean(dxh * xh, axis=-1, keepdims=True))
            dg_ref[...] += jnp.sum(dh * xh, axis=0, keepdims=True)

    return _call(body, name=name, grid=(T // tm, steps),
                 in_specs=[pl.BlockSpec((grp, tm, n), lambda i, j: (j, i, 0)),
                           pl.BlockSpec((grp, None, D, n), lambda i, j: (j, layer, 0, 0)),
                           pl.BlockSpec((tm, D), lambda i, j: (i, 0)),
                           pl.BlockSpec((1, D), lambda i, j: (0, 0)),
                           pl.BlockSpec((tm, D), lambda i, j: (i, 0))],
                 out_specs=[pl.BlockSpec((tm, D), lambda i, j: (i, 0)), pl.BlockSpec((1, D), lambda i, j: (0, 0))],
                 out_shape=[_sds((T, D), F32), _sds((1, D), F32)],
                 scratch=[pltpu.VMEM((tm, D), F32)])(du, w, x_in, gain.reshape(1, D), dx_res)


def mm_tn(name, a, b):
    na, T, ka = a.shape
    nb, _, kb = b.shape
    nj = max(na, nb)

    def body(a_ref, b_ref, o_ref):
        o_ref[...] = _dot_tn(a_ref[...], b_ref[...]).astype(o_ref.dtype)

    return _call(body, name=name, grid=(nj,),
                 in_specs=[pl.BlockSpec((None, T, ka), (lambda j: (j, 0, 0)) if na > 1 else (lambda j: (0, 0, 0))),
                           pl.BlockSpec((None, T, kb), (lambda j: (j, 0, 0)) if nb > 1 else (lambda j: (0, 0, 0)))],
                 out_specs=pl.BlockSpec((None, ka, kb), lambda j: (j, 0, 0)),
                 out_shape=_sds((nj, ka, kb), MM))(a, b)


def ffn_bwd(name, dy, wout, u, ag, dw, wt, x_in, gain, dx_res, layer, seq):
    _, T, D = dy.shape
    n = wt.shape[2]
    tm = _tile(seq, 512)
    tps = seq // tm
    hb = FFN_HALO
    _, nxt = _halo_maps(tm, hb, T)
    K = FFN_CONV_W
    te = tm + hb
    rows = _tile(hb, ROW_CHUNK_BWD)
    u4, w5 = u.reshape(2, 4, T, n), wt.reshape(2, 4, n, D)
    dw4 = dw.reshape(2, 4, dw.shape[1], K, n)
    tiles = _col_tiles(n, FFN_COL_TILE)

    def body(dy_ref, dyn_ref, wo_ref, u_ref, ag_ref, agn_ref, cw_ref, wt_ref, x_ref, g_ref, dr_ref,
             du_ref, ddw_ref, ddb_ref, dx_ref, dg_ref,
             dzf_ref, da_ref, acc_ref):
        i, j = pl.program_id(0), pl.program_id(1)
        last = (i % tps) == tps - 1

        @pl.when((i == 0) & (j == 0))
        def _():
            dg_ref[...] = jnp.zeros_like(dg_ref)

        @pl.when(i == 0)
        def _():
            ddw_ref[j] = jnp.zeros((2, K, n), F32)
            ddb_ref[j] = jnp.zeros((2, 1, n), F32)

        dyt, dyn = dy_ref[...], dyn_ref[...]
        for ci, (c0, wc) in enumerate(tiles):
            cols = pl.ds(c0, wc)
            dzf_ref[pl.ds(0, tm), cols] = _dot_nt(dyt, wo_ref[cols, :])
            dzf_ref[pl.ds(tm, hb), cols] = jnp.where(last, 0.0, _dot_nt(dyn, wo_ref[cols, :]))
            for r0 in range(0, te, rows):
                dzc = dzf_ref[pl.ds(r0, rows), cols]
                for half in range(2):
                    src = ag_ref[half, pl.ds(r0, rows), cols] if r0 < tm else agn_ref[half, pl.ds(r0 - tm, rows), cols]
                    da_ref[half, 0, pl.ds(r0, rows), cols] = dzc * src.astype(F32)
            for half in range(2):
                for s in range(1, K):
                    da_ref[half, s, pl.ds(0, tm), cols] = da_ref[half, 0, pl.ds(s, tm), cols]
                tap_acc = [jnp.zeros((rows, wc), F32) for _ in range(K)]
                bias_acc = jnp.zeros((rows, wc), F32)
                for r0 in range(0, tm, rows):
                    xr = u_ref[half, pl.ds(r0, rows), cols].astype(F32)
                    acc = None
                    for k in range(K):
                        d = da_ref[half, K - 1 - k, pl.ds(r0, rows), cols]
                        term = cw_ref[half, pl.ds(k, 1), cols] * d
                        acc = term if acc is None else acc + term
                        tap_acc[k] = tap_acc[k] + d * xr
                        if k == K - 1:
                            bias_acc = bias_acc + d
                    du_ref[half, pl.ds(r0, rows), cols] = acc.astype(du_ref.dtype)
                for k in range(K):
                    ddw_ref[j, half, pl.ds(k, 1), cols] += jnp.sum(tap_acc[k], axis=0, keepdims=True)
                ddb_ref[j, half, :, cols] += jnp.sum(bias_acc, axis=0, keepdims=True)
            part = _dot(du_ref[0, :, cols], wt_ref[0, cols, :]) + _dot(du_ref[1, :, cols], wt_ref[1, cols, :])
            if ci == 0:
                acc_ref[...] = part + jnp.where(j == 0, 0.0, acc_ref[...])
            else:
                acc_ref[...] += part
        @pl.when(j == 3)
        def _():
            x = x_ref[...]
            dh = acc_ref[...]
            r = lax.rsqrt(jnp.mean(x * x, axis=-1, keepdims=True) + EPS)
            xh = x * r
            dxh = dh * g_ref[...]
            dx_ref[...] = dr_ref[...] + r * (dxh - xh * jnp.mean(dxh * xh, axis=-1, keepdims=True))
            dg_ref[...] += jnp.sum(dh * xh, axis=0, keepdims=True)

    f32 = lambda *shape: pltpu.VMEM(shape, F32)
    du, ddw, ddb, dx, dg = _call(
        body, name=name, grid=(T // tm, 4),
        in_specs=[pl.BlockSpec((None, tm, D), lambda i, j: (0, i, 0)),
                  pl.BlockSpec((None, hb, D), lambda i, j: (0, nxt(i), 0)),
                  pl.BlockSpec((None, None, n, D), lambda i, j: (0, j, 0, 0)),
                  pl.BlockSpec((2, None, tm, n), lambda i, j: (0, j, i, 0)),
                  pl.BlockSpec((2, None, tm, n), lambda i, j: (0, j, i, 0)),
                  pl.BlockSpec((2, None, hb, n), lambda i, j: (0, j, nxt(i), 0)),
                  pl.BlockSpec((2, None, None, K, n), lambda i, j: (0, j, layer, 0, 0)),
                  pl.BlockSpec((2, None, n, D), lambda i, j: (0, j, 0, 0)),
                  pl.BlockSpec((tm, D), lambda i, j: (i, 0)),
                  pl.BlockSpec((1, D), lambda i, j: (0, 0)),
                  pl.BlockSpec((tm, D), lambda i, j: (i, 0))],
        out_specs=[pl.BlockSpec((2, None, tm, n), lambda i, j: (0, j, i, 0)),
                   pl.BlockSpec((4, 2, K, n), lambda i, j: (0, 0, 0, 0)),
                   pl.BlockSpec((4, 2, 1, n), lambda i, j: (0, 0, 0, 0)),
                   pl.BlockSpec((tm, D), lambda i, j: (i, 0)),
                   pl.BlockSpec((1, D), lambda i, j: (0, 0))],
        out_shape=[_sds((2, 4, T, n), MM), _sds((4, 2, K, n), F32), _sds((4, 2, 1, n), F32), _sds((T, D), F32),
                   _sds((1, D), F32)],
        scratch=[f32(te, n), f32(2, K, te, n), f32(tm, D)],
    )(dy, dy, wout, u4, ag, ag, dw4, w5, x_in, gain.reshape(1, D), dx_res)
    return du.reshape(8, T, n), jnp.swapaxes(ddw, 0, 1), jnp.swapaxes(ddb, 0, 1), dx, dg


def ln_silu_bwd(name, ds, c, g, b):
    T, D = c.shape
    tm = _tile(T, 512)

    def body(ds_ref, c_ref, g_ref, b_ref, dc_ref, dg_ref, db_ref):
        @pl.when(pl.program_id(0) == 0)
        def _():
            dg_ref[...] = jnp.zeros_like(dg_ref)
            db_ref[...] = jnp.zeros_like(db_ref)
        cf = c_ref[...]
        mu = jnp.mean(cf, axis=-1, keepdims=True)
        xc = cf - mu
        r = lax.rsqrt(jnp.mean(xc * xc, axis=-1, keepdims=True) + EPS)
        xh = xc * r
        t = xh * g_ref[...] + b_ref[...]
        sg = _sigmoid(t)
        dt = ds_ref[...] * (sg * (1.0 + t * (1.0 - sg)))
        dg_ref[...] += jnp.sum(dt * xh, axis=0, keepdims=True)
        db_ref[...] += jnp.sum(dt, axis=0, keepdims=True)
        dxh = dt * g_ref[...]
        dc_ref[...] = r * (dxh - jnp.mean(dxh, axis=-1, keepdims=True)
                           - xh * jnp.mean(dxh * xh, axis=-1, keepdims=True))

    vec = pl.BlockSpec((1, D), lambda i: (0, 0))
    return _call(body, name=name, grid=(T // tm,),
                 in_specs=[pl.BlockSpec((None, tm, D), lambda i: (0, i, 0)), pl.BlockSpec((tm, D), lambda i: (i, 0)),
                           vec, vec],
                 out_specs=[pl.BlockSpec((tm, D), lambda i: (i, 0)), vec, vec],
                 out_shape=[_sds((T, D), F32), _sds((1, D), F32), _sds((1, D), F32)])(
                     ds, c, g.reshape(1, D), b.reshape(1, D))


def cm_glu_conv_bwd(name, u, dc, dw, layer, seq):
    _, T, n = u.shape
    ct = dw.shape[-1]
    per = n // ct
    nct = 4 * per
    tm = _tile(seq, 512)
    tps = seq // tm
    hb = CONV_HALO
    prev, nxt = _halo_maps(tm, hb, T)
    u4 = u.reshape(2, 4, T, n)
    K = CONV_W

    rows = _tile(tm, 2 * ROW_CHUNK_FWD)
    wrows = 8

    def body(u_ref, up_ref, dc_ref, dcn_ref, w_ref, du_ref, ddw_ref, ddb_ref, dbi_ref, zp_ref, zd_ref):
        i = pl.program_id(1)
        first = (i % tps) == 0
        last = (i % tps) == tps - 1

        @pl.when(i == 0)
        def _():
            ddw_ref[...] = jnp.zeros_like(ddw_ref)
            ddb_ref[...] = jnp.zeros_like(ddb_ref)
            dbi_ref[...] = jnp.zeros_like(dbi_ref)

        uh = up_ref[...].astype(F32)
        zp_ref[0, pl.ds(hb, tm), :] = u_ref[0].astype(F32) * _sigmoid(u_ref[1].astype(F32))
        zp_ref[0, pl.ds(0, hb), :] = jnp.where(first, 0.0, uh[0] * _sigmoid(uh[1]))
        zd_ref[0, pl.ds(0, tm), :] = dc_ref[...]
        zd_ref[0, pl.ds(tm, hb), :] = jnp.where(last, 0.0, dcn_ref[...])
        _stage_shifts(zp_ref, hb + tm, back=True)
        _stage_shifts(zd_ref, tm + hb, back=False)
        for r0 in range(0, tm, rows):
            dp = None
            for j in range(K):
                term = w_ref[pl.ds(K - 1 - j, 1), :] * zd_ref[j % 8, pl.ds(r0 + (j - j % 8), rows), :]
                dp = term if dp is None else dp + term
            v = u_ref[0, pl.ds(r0, rows), :].astype(F32)
            sg = _sigmoid(u_ref[1, pl.ds(r0, rows), :].astype(F32))
            dv = dp * sg
            dg = dp * v * sg * (1.0 - sg)
            du_ref[0, pl.ds(r0, rows), :] = dv.astype(du_ref.dtype)
            du_ref[1, pl.ds(r0, rows), :] = dg.astype(du_ref.dtype)
            dbi_ref[0] += jnp.sum(dv, axis=0, keepdims=True)
            dbi_ref[1] += jnp.sum(dg, axis=0, keepdims=True)
        tap_acc = [jnp.zeros((wrows, ct), F32) for _ in range(K)]
        bias_acc = jnp.zeros((wrows, ct), F32)
        for r0 in range(0, tm, wrows):
            d = zd_ref[0, pl.ds(r0, wrows), :]
            for j in range(K):
                tap_acc[j] = tap_acc[j] + d * zp_ref[j % 8, pl.ds(hb + r0 - (j - j % 8), wrows), :]
            bias_acc = bias_acc + d
        for j in range(K):
            ddw_ref[pl.ds(K - 1 - j, 1), :] += jnp.sum(tap_acc[j], axis=0, keepdims=True)
        ddb_ref[...] += jnp.sum(bias_acc, axis=0, keepdims=True)

    du, ddw, ddb, dbi = _call(
        body, name=name, grid=(nct, T // tm),
        in_specs=[pl.BlockSpec((2, None, tm, ct), lambda c, i: (0, c // per, i, c % per)),
                  pl.BlockSpec((2, None, hb, ct), lambda c, i: (0, c // per, prev(i), c % per)),
                  pl.BlockSpec((tm, ct), lambda c, i: (i, c)),
                  pl.BlockSpec((hb, ct), lambda c, i: (nxt(i), c)),
                  pl.BlockSpec((None, None, K, ct), lambda c, i: (c, layer, 0, 0))],
        out_specs=[pl.BlockSpec((2, None, tm, ct), lambda c, i: (0, c // per, i, c % per)),
                   pl.BlockSpec((None, K, ct), lambda c, i: (c, 0, 0)),
                   pl.BlockSpec((None, 1, ct), lambda c, i: (c, 0, 0)),
                   pl.BlockSpec((2, None, 1, ct), lambda c, i: (0, c // per, 0, c % per))],
        out_shape=[_sds((2, 4, T, n), MM), _sds((nct, K, ct), F32), _sds((nct, 1, ct), F32), _sds((2, 4, 1, n), F32)],
        scratch=[pltpu.VMEM((8, tm + hb, ct), F32), pltpu.VMEM((8, tm + hb, ct), F32)])(u4, u4, dc, dc, dw)
    return du.reshape(8, T, n), ddw, ddb, dbi


def attn_bwd(name, q, kv, dm, merged, lse, dkv_prev, nb, seq):
    _, T, qn = q.shape
    kn = kv.shape[2]
    dh = qn * N_DEV // (N_GROUPS * N_HEADS)
    scale = 1.0 / (dh ** 0.5)
    qs, ks, vs = _head_specs(seq, dh, qn // dh, kn // dh)
    has_prev = dkv_prev is not None
    n_in = 6 + (1 if has_prev else 0)

    def body(*refs):
        q_ref, k_ref, v_ref, dm_ref, mg_ref, l_ref = refs[:6]
        pkv_ref = refs[6] if has_prev else None
        dq_ref, dkv_ref = refs[n_in:n_in + 2]
        qf, kf, vf, dqf, dkf, dvf, dlt = refs[n_in + 2:]
        pad = _att_pad()

        @pl.when(pl.program_id(2) == 0)
        def _():
            dlt[...] = jnp.broadcast_to(
                jnp.sum(dm_ref[...] * mg_ref[...].astype(F32), axis=-1, keepdims=True), (seq, dh))

        qf[...] = q_ref[...].astype(F32)
        for t_ref, s_ref in ((k_ref, kf), (v_ref, vf)):
            s_ref[pl.ds(0, pad), :] = jnp.zeros((pad, dh), F32)
            s_ref[pl.ds(pad, seq), :] = t_ref[...].astype(F32)
        dkf[...] = jnp.zeros_like(dkf)
        dvf[...] = jnp.zeros_like(dvf)
        for g in range(N_GROUPS):
            d = DILATIONS[g]
            nblk = seq // d // ATT_BLOCK

            def blocks(it, carry, d=d, nblk=nblk):
                first = it * ATT_BATCH
                rows = [_block_rows(first + b, d, nblk) for b in range(ATT_BATCH)]
                qb = jnp.stack([qf[rq, :] for rq, _ in rows]).astype(MM)
                dmb = jnp.stack([dm_ref[rq, :] for rq, _ in rows]).astype(MM)
                lse = jnp.stack([l_ref[rq, :][:, :1] for rq, _ in rows])
                delta = jnp.stack([dlt[rq, :][:, :1] for rq, _ in rows])
                kb = jnp.stack([kf[rk, :] for _, rk in rows]).astype(MM)
                vb = jnp.stack([vf[rk, :] for _, rk in rows]).astype(MM)
                s = jnp.where(_band_mask(first, nblk), _bdot(qb, kb, 2, 2) * scale, NEG)
                p = jnp.exp(s - lse)
                dsc = (p * (_bdot(dmb, vb, 2, 2) - delta) * scale).astype(MM)
                dv = _bdot(p.astype(MM), dmb, 1, 1)
                dk = _bdot(dsc, qb, 1, 1)
                dq = _bdot(dsc, kb, 2, 1)
                for b, (rq, rk) in enumerate(rows):
                    dqf[rq, :] = dq[b]
                    dkf[rk, :] += dk[b]
                    dvf[rk, :] += dv[b]
                return carry

            @pl.when(pl.program_id(2) == g)
            def _(blocks=blocks, d=d, nblk=nblk):
                lax.fori_loop(0, d * nblk // ATT_BATCH, blocks, 0)

        dq_ref[...] = dqf[...].astype(dq_ref.dtype)
        dk, dv = dkf[pl.ds(pad, seq), :], dvf[pl.ds(pad, seq), :]
        if has_prev:
            dk, dv = dk + pkv_ref[0].astype(F32), dv + pkv_ref[1].astype(F32)
        dkv_ref[0] = dk.astype(dkv_ref.dtype)
        dkv_ref[1] = dv.astype(dkv_ref.dtype)

    per = kn // dh

    def both(b, h, g):
        f = g * N_HEADS + h
        return (0, f // per, b, f % per)

    kv_spec = pl.BlockSpec((2, None, seq, dh), both)
    full = pl.BlockSpec((None, seq, dh), lambda b, h, g: (0, b, h))
    in_specs = [qs, ks, vs, full, full, pl.BlockSpec((seq, dh), lambda b, h, g: (b, h))]
    args = [q, kv, kv, dm, merged, lse]
    if has_prev:
        in_specs.append(kv_spec)
        args.append(dkv_prev.reshape(2, N_DEV // 2, T, kn))
    short, padded = pltpu.VMEM((seq, dh), F32), pltpu.VMEM((_att_pad() + seq, dh), F32)
    dq, dkv = _call(body, name=name, grid=(nb, N_HEADS, N_GROUPS), in_specs=in_specs, out_specs=[qs, kv_spec],
                    out_shape=[_sds(q.shape, MM), _sds((2, N_DEV // 2, T, kn), MM)],
                    scratch=[short, padded, padded, short, padded, padded, short])(*args)
    return dq, dkv.reshape(N_DEV, T, kn)


def _adamw_math(w, g, m, v):
    m = ADAM_B1 * m + (1.0 - ADAM_B1) * g
    v = ADAM_B2 * v + (1.0 - ADAM_B2) * (g * g)
    m_hat = m / (1.0 - ADAM_B1 ** ADAM_STEP)
    v_hat = v / (1.0 - ADAM_B2 ** ADAM_STEP)
    delta = -ADAM_LR * (m_hat / (jnp.sqrt(v_hat) + ADAM_EPS) + ADAM_WD * w)
    return delta, m, v


def adamw_sum(name, w, m, v, parts, after):
    L, R, C = w.shape
    tr = _tile(R, 256)

    def body(*refs):
        w_ref, m_ref, v_ref = refs[:3]
        p_refs = refs[3:3 + L]
        g_ref, d_ref, nm_ref, nv_ref = refs[4 + L:]
        for l in range(L):
            @pl.when(pl.program_id(0) == l)
            def _(p_ref=p_refs[l]):
                g = p_ref[0].astype(F32)
                for k in range(1, N_DEV):
                    g = g + p_ref[k].astype(F32)
                g_ref[...] = g
                d_ref[...], nm_ref[...], nv_ref[...] = _adamw_math(w_ref[...], g, m_ref[...], v_ref[...])

    blk = pl.BlockSpec((None, tr, C), lambda l, i: (l, i, 0))
    part = lambda k: pl.BlockSpec((N_DEV, tr, C), lambda l, i: (0, jnp.where(l == k, i, 0), 0))
    return _call(body, name=name, grid=(L, R // tr),
                 in_specs=[blk, blk, blk] + [part(k) for k in range(L)] + [pl.BlockSpec(memory_space=pl.ANY)],
                 out_specs=[blk] * 4, out_shape=[_sds((L, R, C), F32)] * 4)(w, m, v, *parts, after)


def sum_partials(name, parts):
    _, R, C = parts.shape
    tr = _tile(R, 512)

    def body(p_ref, o_ref):
        g = p_ref[0]
        for k in range(1, N_DEV):
            g = g + p_ref[k]
        o_ref[...] = g

    return _call(body, name=name, grid=(R // tr,),
                 in_specs=[pl.BlockSpec((N_DEV, tr, C), lambda i: (0, i, 0))],
                 out_specs=pl.BlockSpec((tr, C), lambda i: (i, 0)), out_shape=_sds((R, C), F32))(parts)


def adamw_small(name, w, g, m, v):
    R, C = w.shape
    tr = _tile(R, 512)

    def body(w_ref, g_ref, m_ref, v_ref, d_ref, nm_ref, nv_ref):
        d_ref[...], nm_ref[...], nv_ref[...] = _adamw_math(w_ref[...], g_ref[...], m_ref[...], v_ref[...])

    blk = pl.BlockSpec((tr, C), lambda i: (i, 0))
    return _call(body, name=name, grid=(R // tr,), in_specs=[blk] * 4, out_specs=[blk] * 3,
                 out_shape=[_sds((R, C), F32)] * 3)(w, g, m, v)


def _pack(arrays):
    pieces = []
    for a in arrays:
        f = a.reshape(-1).astype(F32)
        pieces.append(jnp.pad(f, (0, (-f.shape[0]) % PACK)))
    return jnp.concatenate(pieces).reshape(-1, 128)


def _unpack(flat, shapes):
    out, off = [], 0
    f = flat.reshape(-1)
    for s in shapes:
        size = 1
        for d in s:
            size *= d
        out.append(f[off:off + size].reshape(s))
        off += size + (-size) % PACK
    return out


def kernel(x, mix_pre_g, mix_post_g, ffn_pre_g, ffn_post_g, cm_w_in, cm_b_in, cm_dw, cm_dw_b, cm_ln_g, cm_ln_b, cm_w_out, cm_b_out, kv_norm_g, w_kv, w_q, w_o, ffn_w_in, ffn_dw, ffn_dw_b, ffn_w_out, loss_target, m_mix_pre_g, m_mix_post_g, m_ffn_pre_g, m_ffn_post_g, m_cm_w_in, m_cm_b_in, m_cm_dw, m_cm_dw_b, m_cm_ln_g, m_cm_ln_b, m_cm_w_out, m_cm_b_out, m_kv_norm_g, m_w_kv, m_w_q, m_w_o, m_ffn_w_in, m_ffn_dw, m_ffn_dw_b, m_ffn_w_out, v_mix_pre_g, v_mix_post_g, v_ffn_pre_g, v_ffn_post_g, v_cm_w_in, v_cm_b_in, v_cm_dw, v_cm_dw_b, v_cm_ln_g, v_cm_ln_b, v_cm_w_out, v_cm_b_out, v_kv_norm_g, v_w_kv, v_w_q, v_w_o, v_ffn_w_in, v_ffn_dw, v_ffn_dw_b, v_ffn_w_out):
    nb, seq, D = x.shape
    T = nb * seq
    me = _my_index()
    n_b = DEPTH - N_A

    nf = ffn_w_in.shape[-1]
    t_ = lambda t: jnp.swapaxes(t, 1, 2)
    fin_t, m_fin_t, v_fin_t = t_(ffn_w_in), t_(m_ffn_w_in), t_(v_ffn_w_in)

    stages = [(part, i) for i in range(DEPTH) for part in ("mix", "ffn")]
    stages.insert(stages.index(("ffn", N_A - 1)) + 1, ("kv", N_A - 1))

    def stage_sources(stage):
        part, i = stage
        if part == "ffn":
            src = {"fin": fin_t[i], "fout": ffn_w_out[i]}
        elif part == "kv":
            src = {"kv": w_kv}
        elif i < N_A:
            src = {"cin": cm_w_in[i], "cout": cm_w_out[i]}
        else:
            src = {"q": w_q[i - N_A], "o": w_o[i - N_A]}
        return {k: t.astype(MM) for k, t in src.items()}

    def begin_gather(stage, after):
        src = stage_sources(stage)
        names, arrays = list(src), list(src.values())
        tag = f"{stage[0]}{stage[1]}"
        lands = place_own(f"gather_own_{tag}", "gather", arrays)
        handle, token = exchange_begin(f"gather_begin_{tag}", "gather", arrays, lands, after)
        return (names, handle), token

    def end_gather(stage, pending, after):
        names, handle = pending
        W = dict(zip(names, exchange_end(f"gather_end_{stage[0]}{stage[1]}", handle, after)))
        for k in W:
            if k in ("cout", "o"):
                W[k] = W[k].reshape(1, 1, D, D)
            elif k == "fout":
                W[k] = W[k].reshape(1, 4, nf, D)
            else:
                W[k] = W[k][:, None]
        return W

    small = [cm_b_in[:, None, :], cm_dw, cm_dw_b[:, None, :], cm_ln_g, cm_ln_b, cm_b_out, ffn_dw]
    Bcin, DWc, DWBc, LNg, LNb, Bcout, DWf = all_gather("gather_small", small, [False] * len(small))
    LNg = jnp.swapaxes(LNg, 0, 1).reshape(N_A, D)
    LNb = jnp.swapaxes(LNb, 0, 1).reshape(N_A, D)
    Bcout = jnp.swapaxes(Bcout, 0, 1).reshape(N_A, D)
    DWBf = ffn_dw_b.reshape(DEPTH, N_DEV, 1, nf)
    zero_bias = jnp.zeros((D,), F32)

    xs = x.reshape(T, D)
    sv = []
    kv = hkv = None
    pending = {}
    pending[stages[0]], _ = begin_gather(stages[0], DWf)
    Ws = {stages[0]: end_gather(stages[0], pending[stages[0]], xs)}
    pending[stages[1]], token = begin_gather(stages[1], next(iter(Ws[stages[0]].values())))
    sv = [{} for _ in range(DEPTH)]
    for idx, stage in enumerate(stages):
        part, i = stage
        L, W = sv[i], Ws[stage]
        gain = {"mix": mix_pre_g[i], "ffn": ffn_pre_g[i], "kv": kv_norm_g}[part]
        following = stages[idx + 1] if idx + 1 < len(stages) else None
        if idx + 2 < len(stages):
            after = token if idx == 0 else next(iter(W.values()))
            pending[stages[idx + 2]], token = begin_gather(stages[idx + 2], after)
        if idx + 2 < len(stages) or idx == 0:
            gain = gain + token[0, 0]
        if part == "mix":
            L["x_in"] = xs
            if i < N_A:
                L["u"], L["h"] = norm_mm(f"cm_in_{i}", xs, gain, W["cin"], 0, Bcin[:, i:i + 1])
                L["c"] = cm_glu_conv(f"cm_conv_{i}", L["u"], DWc, DWBc, i, seq)
                L["s"] = ln_silu(f"cm_ln_{i}", L["c"], LNg[i], LNb[i])
                L["y"], xs = mm_resid_norm(f"cm_out_{i}", L["s"], W["cout"], 0, Bcout[i], xs, mix_post_g[i])
            else:
                L["q"], L["h"] = norm_mm(f"attn_q_{i}", xs, gain, W["q"], 0)
                L["mg"], L["lse"] = attn_fwd(f"attn_{i}", L["q"], kv, nb, seq)
                L["y"], xs = mm_resid_norm(f"attn_out_{i}", L["mg"], W["o"], 0, zero_bias, xs, mix_post_g[i])
            L["x1"] = xs
        elif part == "ffn":
            L["uf"], L["hf"], L["z"], L["ag"] = ffn_in_conv(f"ffn_in_{i}", xs, gain, W["fin"], DWf, DWBf, i, seq)
            L["yf"], xs = mm_resid_norm(f"ffn_out_{i}", L["z"], W["fout"], 0, zero_bias, xs, ffn_post_g[i])
        else:
            kv, hkv = norm_mm("kv_proj", xs, gain, W["kv"], 0)
        if following is not None:
            Ws[following] = end_gather(following, pending[following], kv if part == "kv" else xs)
    dx, loss_part = loss_fwd_bwd("loss", xs, loss_target.reshape(T, D))

    g_mix_pre, g_mix_post, g_ffn_pre, g_ffn_post = [None] * DEPTH, [None] * DEPTH, [None] * DEPTH, [None] * DEPTH
    g_ffn_dw, g_ffn_dwb = [None] * DEPTH, [None] * DEPTH
    g_cbin, g_cdw, g_cdwb, g_lng, g_lnb, g_cbout = ([None] * N_A for _ in range(6))
    g_kvn = dkv = None
    landed = [{} for _ in range(DEPTH)]
    in_flight = token = None
    for stage in reversed(stages):
        part, i = stage
        L, W = sv[i], Ws[stage]
        gain = {"mix": mix_post_g[i], "ffn": ffn_post_g[i], "kv": kv_norm_g}[part]
        if token is not None:
            gain = gain + token[0, 0]
        send = {}
        if part == "kv":
            send["kv"] = mm_tn("kv_wg", hkv[None], dkv)
            dx, g_kvn = mm_nt_norm_bwd("kv_bwd", dkv, W["kv"], 0, sv[i + 1]["x_in"], gain, dx)
        elif part == "ffn":
            dyf, g_ffn_post[i], _ = resid_norm_bwd(f"ffn_post_bwd_{i}", dx, L["yf"], gain)
            send["fout"] = mm_tn(f"ffn_out_wg_{i}", L["z"], dyf).reshape(N_DEV, nf // 2, D)
            duf, ddw, ddwb, dx, g_ffn_pre[i] = ffn_bwd(f"ffn_bwd_{i}", dyf, W["fout"], L["uf"], L["ag"], DWf, W["fin"],
                                                       L["x1"], ffn_pre_g[i], dx, i, seq)
            g_ffn_dw[i], g_ffn_dwb[i] = ddw.reshape(N_DEV, FFN_CONV_W, nf), ddwb.reshape(-1)
            send["fin"] = mm_tn(f"ffn_in_wg_{i}", duf, L["hf"][None])
        else:
            dy, g_mix_post[i], dyb = resid_norm_bwd(f"mix_post_bwd_{i}", dx, L["y"], gain)
            if i >= N_A:
                dm = mm_nt(f"attn_out_bwd_{i}", dy, W["o"], 0, F32)
                send["o"] = mm_tn(f"attn_out_wg_{i}", L["mg"], dy).reshape(N_DEV, D // N_DEV, D)
                dq, dkv = attn_bwd(f"attn_bwd_{i}", L["q"], kv, dm, L["mg"], L["lse"], dkv, nb, seq)
                send["q"] = mm_tn(f"attn_q_wg_{i}", L["h"][None], dq)
                dx, g_mix_pre[i] = mm_nt_norm_bwd(f"attn_q_bwd_{i}", dq, W["q"], 0, L["x_in"], mix_pre_g[i], dx)
            else:
                g_cbout[i] = dyb
                ds = mm_nt(f"cm_out_bwd_{i}", dy, W["cout"], 0, F32)
                send["cout"] = mm_tn(f"cm_out_wg_{i}", L["s"], dy).reshape(N_DEV, D // N_DEV, D)
                ln_gain = LNg[i]
                if stage == stages[0]:
                    early = [send.pop("cout")]
                    lands = place_own("scatter_own_cout0", "scatter", early)
                    early_handle, early_token = exchange_begin("scatter_begin_cout0", "scatter", early, lands, ds)
                    ln_gain = ln_gain + early_token[0, 0]
                dc, g_lng[i], g_lnb[i] = ln_silu_bwd(f"cm_ln_bwd_{i}", ds, L["c"], ln_gain, LNb[i])
                du, g_cdw[i], g_cdwb[i], dbi = cm_glu_conv_bwd(f"cm_conv_bwd_{i}", L["u"], dc, DWc, i, seq)
                g_cbin[i] = dbi.reshape(N_DEV, -1)
                send["cin"] = mm_tn(f"cm_in_wg_{i}", L["h"][None], du)
                dx, g_mix_pre[i] = mm_nt_norm_bwd(f"cm_in_bwd_{i}", du, W["cin"], 0, L["x_in"], mix_pre_g[i], dx)
        if in_flight is not None:
            (p, j), names, handle = in_flight
            landed[j].update(zip(names, exchange_end(f"scatter_end_{p}{j}", handle, dx)))
        names, arrays = list(send), list(send.values())
        lands = place_own(f"scatter_own_{part}{i}", "scatter", arrays)
        handle, token = exchange_begin(f"scatter_begin_{part}{i}", "scatter", arrays, lands, dx)
        in_flight = (stage, names, handle)
    grad_x = dx.reshape(nb, seq, D)

    rep_names = ["mix_pre_g", "mix_post_g", "ffn_pre_g", "ffn_post_g", "kv_norm_g", "ffn_dw_b"]
    rep_parts = [jnp.concatenate(g_mix_pre), jnp.concatenate(g_mix_post), jnp.concatenate(g_ffn_pre),
                 jnp.concatenate(g_ffn_post), g_kvn.reshape(-1), jnp.stack(g_ffn_dwb)]
    rep_w = [mix_pre_g, mix_post_g, ffn_pre_g, ffn_post_g, kv_norm_g, ffn_dw_b]
    rep_m = [m_mix_pre_g, m_mix_post_g, m_ffn_pre_g, m_ffn_post_g, m_kv_norm_g, m_ffn_dw_b]
    rep_v = [v_mix_pre_g, v_mix_post_g, v_ffn_pre_g, v_ffn_post_g, v_kv_norm_g, v_ffn_dw_b]
    sh_names = ["ffn_dw", "cm_b_in", "cm_dw", "cm_dw_b", "cm_ln_g", "cm_ln_b", "cm_b_out"]
    own = lambda per_layer, shard: jnp.stack([p.reshape((N_DEV,) + shard) for p in per_layer], axis=1)
    sh_parts = [own(g_ffn_dw, ffn_dw.shape[1:]), own(g_cbin, cm_b_in.shape[1:]), own(g_cdw, cm_dw.shape[1:]),
                own(g_cdwb, cm_dw_b.shape[1:]), own(g_lng, cm_ln_g.shape[1:]), own(g_lnb, cm_ln_b.shape[1:]),
                own(g_cbout, cm_b_out.shape[1:])]
    sh_w = [ffn_dw, cm_b_in, cm_dw, cm_dw_b, cm_ln_g, cm_ln_b, cm_b_out]
    sh_m = [m_ffn_dw, m_cm_b_in, m_cm_dw, m_cm_dw_b, m_cm_ln_g, m_cm_ln_b, m_cm_b_out]
    sh_v = [v_ffn_dw, v_cm_b_in, v_cm_dw, v_cm_dw_b, v_cm_ln_g, v_cm_ln_b, v_cm_b_out]
    rep_pack = _pack([loss_part] + rep_parts)
    sh_pack = jnp.stack([_pack([p[k] for p in sh_parts]) for k in range(N_DEV)])
    n_rep = rep_pack.shape[0]
    small = []
    for kind, pack in (("gather", rep_pack), ("scatter", sh_pack)):
        lands = place_own(f"{kind}_own_small", kind, [pack])
        handle, token = exchange_begin(f"{kind}_begin_small", kind, [pack], lands, token)
        small.append((kind, handle))

    def big_update(name, w, m, v, key, layers, after):
        as3 = lambda t: t.reshape((-1,) + t.shape[-2:])
        outs = adamw_sum(name, as3(w), as3(m), as3(v), [landed[i][key] for i in layers], after)
        return [t.reshape(w.shape) for t in outs]

    conf, attn = range(N_A), range(N_A, DEPTH)
    upd = {}
    upd["ffn_w_in"] = [t_(t) for t in big_update("adam_ffn_w_in", fin_t, m_fin_t, v_fin_t, "fin", range(DEPTH), token)]
    upd["ffn_w_out"] = big_update("adam_ffn_w_out", ffn_w_out, m_ffn_w_out, v_ffn_w_out, "fout", range(DEPTH), token)
    upd["w_kv"] = big_update("adam_w_kv", w_kv, m_w_kv, v_w_kv, "kv", [N_A - 1], token)
    upd["w_q"] = big_update("adam_w_q", w_q, m_w_q, v_w_q, "q", attn, token)
    upd["w_o"] = big_update("adam_w_o", w_o, m_w_o, v_w_o, "o", attn, upd["w_q"][0])
    (p, j), names, handle = in_flight
    landed[j].update(zip(names, exchange_end(f"scatter_end_{p}{j}", handle, upd["w_o"][0])))
    landed[0]["cout"] = exchange_end("scatter_end_cout0", early_handle, upd["w_o"][0])[0]
    upd["cm_w_in"] = big_update("adam_cm_w_in", cm_w_in, m_cm_w_in, v_cm_w_in, "cin", conf, token)
    upd["cm_w_out"] = big_update("adam_cm_w_out", cm_w_out, m_cm_w_out, v_cm_w_out, "cout", conf, upd["cm_w_in"][0])
    (rep_landed,), (sh_landed,) = (exchange_end(f"{kind}_end_small", handle, upd["cm_w_out"][0])
                                   for kind, handle in small)
    rep_sum = sum_partials("sum_small_rep", rep_landed)
    sh_sum = sum_partials("sum_small_sh", sh_landed)
    rep_shapes = [(1, 1)] + [w.shape for w in rep_w]
    sh_shapes = [w.shape for w in sh_w]
    g_small = jnp.concatenate([rep_sum, sh_sum])
    pad1 = jnp.zeros((1, 1), F32)
    d_s, m_s, v_s = adamw_small("adam_small", jnp.concatenate([_pack([pad1] + rep_w), _pack(sh_w)]), g_small,
                                jnp.concatenate([_pack([pad1] + rep_m), _pack(sh_m)]),
                                jnp.concatenate([_pack([pad1] + rep_v), _pack(sh_v)]))
    split = lambda t: (_unpack(t[:n_rep], rep_shapes), _unpack(t[n_rep:], sh_shapes))
    for (rep_t, sh_t), slot in zip([split(g_small), split(d_s), split(m_s), split(v_s)], range(4)):
        if slot == 0:
            loss = rep_t[0].reshape(())
        for name, t in zip(rep_names, rep_t[1:]):
            upd.setdefault(name, [None] * 4)[slot] = t
        for name, t in zip(sh_names, sh_t):
            upd.setdefault(name, [None] * 4)[slot] = t

    order = ["mix_pre_g", "mix_post_g", "ffn_pre_g", "ffn_post_g", "cm_w_in", "cm_b_in", "cm_dw", "cm_dw_b", "cm_ln_g",
             "cm_ln_b", "cm_w_out", "cm_b_out", "kv_norm_g", "w_kv", "w_q", "w_o", "ffn_w_in", "ffn_dw", "ffn_dw_b",
             "ffn_w_out"]
    return (loss, grad_x, *[upd[n][0] for n in order], *[upd[n][1] for n in order],
            *[upd[n][2] for n in order], *[upd[n][3] for n in order])
```
